```python
import math
import jax, jax.numpy as jnp
from jax import lax
import numpy as np

D_MODEL = 1024
BATCH = 8
SEQ = 8192
DEPTH = 2

N_EVEN = (DEPTH + 1) // 2
N_ODD = DEPTH // 2
LRU_WIDTH = D_MODEL
LRU_BLOCKS = 8
LRU_BLOCK = LRU_WIDTH // LRU_BLOCKS
LRU_C = 8.0
CONV_WIDTH = 4
SB_HEADS = 8
SB_HEAD_DIM = 128
SB_WIDTH = SB_HEADS * SB_HEAD_DIM
C_HEADS = 16
C_KV_HEADS = 2
C_GROUP = C_HEADS // C_KV_HEADS
C_HEAD_DIM = 64
C_WIDTH = C_HEADS * C_HEAD_DIM
C_KV_WIDTH = C_KV_HEADS * C_HEAD_DIM
WINDOW = 128
Q_BLOCK = 128
EVEN_IN = 2 * LRU_WIDTH + 4 * SB_WIDTH
EVEN_MIX = LRU_WIDTH + SB_WIDTH
ODD_IN = 2 * C_WIDTH + 2 * C_KV_WIDTH
DEEPNORM_ALPHA = float((2 * DEPTH) ** 0.25)
DEEPNORM_BETA = float((8 * DEPTH) ** -0.25)
LN_EPS = 1e-5

kernel_name = "hybrid_rglru_stickbreak_swa_deepnorm"


def _layer_norm(x, g, b):
    xf = x.astype(jnp.float32)
    mu = jnp.mean(xf, axis=-1, keepdims=True)
    var = jnp.mean(jnp.square(xf - mu), axis=-1, keepdims=True)
    y = (xf - mu) * lax.rsqrt(var + LN_EPS) * g.astype(jnp.float32) + b.astype(jnp.float32)
    return y.astype(x.dtype)


def _causal_dwconv(x, w, b):
    s = x.shape[1]
    xp = jnp.pad(x, ((0, 0), (CONV_WIDTH - 1, 0), (0, 0)))
    y = b
    for k in range(CONV_WIDTH):
        y = y + w[k] * xp[:, k:k + s]
    return y


def _rg_lru(x, w_a, b_a, w_x, b_x, lam):
    bsz, s, _ = x.shape
    xb = x.reshape(bsz, s, LRU_BLOCKS, LRU_BLOCK)
    r = jax.nn.sigmoid(jnp.einsum('bsnd,nde->bsne', xb, w_a).reshape(bsz, s, LRU_WIDTH) + b_a)
    i = jax.nn.sigmoid(jnp.einsum('bsnd,nde->bsne', xb, w_x).reshape(bsz, s, LRU_WIDTH) + b_x)
    log_a = LRU_C * r.astype(jnp.float32) * jax.nn.log_sigmoid(lam.astype(jnp.float32))
    a = jnp.exp(log_a)
    u = jnp.sqrt(-jnp.expm1(2.0 * log_a)) * (i * x).astype(jnp.float32)

    def combine(lhs, rhs):
        a1, b1 = lhs
        a2, b2 = rhs
        return a1 * a2, a2 * b1 + b2

    _, h = lax.associative_scan(combine, (a, u), axis=1)
    return h.astype(x.dtype)


def _stick_breaking(q, k, v):
    bsz, s, h, dh = q.shape
    nb = s // Q_BLOCK
    scale = 1.0 / math.sqrt(dh)
    qb = q.reshape(bsz, nb, Q_BLOCK, h, dh).transpose(1, 0, 2, 3, 4)
    kpos = jnp.arange(s)

    def one_block(args):
        n, qblk = args
        z = jnp.einsum('bqhd,bkhd->bhqk', qblk, k).astype(jnp.float32) * scale
        qpos = n * Q_BLOCK + jnp.arange(Q_BLOCK)
        causal = kpos[None, :] < qpos[:, None]
        log_1mb = jnp.where(causal, jax.nn.log_sigmoid(-z), 0.0)
        suffix = lax.cumsum(log_1mb, axis=3, reverse=True) - log_1mb
        w = jnp.where(causal, jnp.exp(jax.nn.log_sigmoid(z) + suffix), 0.0)
        return jnp.einsum('bhqk,bkhd->bqhd', w.astype(v.dtype), v)

    o = lax.map(one_block, (jnp.arange(nb), qb))
    return o.transpose(1, 0, 2, 3, 4).reshape(bsz, s, h, dh)


def _alibi_slopes(n_heads):
    return np.array([2.0 ** (-8.0 * (i + 1) / n_heads) for i in range(n_heads)], dtype=np.float32)


def _swa_sinks_alibi(q, k, v, sinks):
    bsz, s, _, dh = q.shape
    nb = s // Q_BLOCK
    scale = 1.0 / math.sqrt(dh)
    qb = q.reshape(bsz, nb, Q_BLOCK, C_KV_HEADS, C_GROUP, dh)
    kp = jnp.pad(k, ((0, 0), (Q_BLOCK, 0), (0, 0), (0, 0))).reshape(bsz, nb + 1, Q_BLOCK, C_KV_HEADS, dh)
    vp = jnp.pad(v, ((0, 0), (Q_BLOCK, 0), (0, 0), (0, 0))).reshape(bsz, nb + 1, Q_BLOCK, C_KV_HEADS, dh)
    kc = jnp.concatenate([kp[:, :-1], kp[:, 1:]], axis=2)
    vc = jnp.concatenate([vp[:, :-1], vp[:, 1:]], axis=2)
    sc = jnp.einsum('bnqcgd,bnkcd->bncgqk', qb, kc).astype(jnp.float32) * scale
    i = jnp.arange(Q_BLOCK)[:, None]
    j = jnp.arange(2 * Q_BLOCK)[None, :]
    dist = (i - j + Q_BLOCK).astype(jnp.float32)
    kpos = jnp.arange(nb)[:, None, None] * Q_BLOCK - Q_BLOCK + j[None]
    valid = (dist >= 0) & (dist < WINDOW) & (kpos >= 0)
    slopes = jnp.asarray(_alibi_slopes(C_HEADS)).reshape(C_KV_HEADS, C_GROUP)
    sc = sc - slopes[:, :, None, None] * dist
    sc = jnp.where(valid[None, :, None, None], sc, -jnp.inf)
    sink = sinks.astype(jnp.float32).reshape(C_KV_HEADS, C_GROUP)[:, :, None, None]
    m = jnp.maximum(jnp.max(sc, axis=-1, keepdims=True), sink)
    p = jnp.exp(sc - m)
    denom = jnp.sum(p, axis=-1, keepdims=True) + jnp.exp(sink - m)
    p = (p / denom).astype(v.dtype)
    o = jnp.einsum('bncgqk,bnkcd->bnqcgd', p, vc)
    return o.reshape(bsz, s, C_HEADS, dh)


def _even_layer(x, w_in, conv_w, conv_b, w_gate_a, b_gate_a, w_gate_x, b_gate_x, lru_lambda, w_out):
    bsz, s, _ = x.shape
    hproj = jnp.einsum('bsd,de->bse', x, w_in)
    splits = np.cumsum([LRU_WIDTH, LRU_WIDTH, SB_WIDTH, SB_WIDTH, SB_WIDTH])
    a_x, a_g, b_q, b_k, b_v, b_g = jnp.split(hproj, splits, axis=-1)
    a_h = _rg_lru(_causal_dwconv(a_x, conv_w, conv_b), w_gate_a, b_gate_a, w_gate_x, b_gate_x, lru_lambda)
    y_a = a_h * jax.nn.silu(a_g)
    shp = (bsz, s, SB_HEADS, SB_HEAD_DIM)
    o_b = _stick_breaking(b_q.reshape(shp), b_k.reshape(shp), b_v.reshape(shp)).reshape(bsz, s, SB_WIDTH)
    y_b = o_b * jax.nn.silu(b_g)
    y = jnp.concatenate([y_a, y_b], axis=-1)
    return jnp.einsum('bse,ed->bsd', y, w_out)


def _odd_layer(x, w_in, sinks, w_out):
    bsz, s, _ = x.shape
    hproj = jnp.einsum('bsd,de->bse', x, w_in)
    splits = np.cumsum([C_WIDTH, C_KV_WIDTH, C_KV_WIDTH])
    q, k, v, g = jnp.split(hproj, splits, axis=-1)
    o = _swa_sinks_alibi(q.reshape(bsz, s, C_HEADS, C_HEAD_DIM),
                         k.reshape(bsz, s, C_KV_HEADS, C_HEAD_DIM),
                         v.reshape(bsz, s, C_KV_HEADS, C_HEAD_DIM), sinks).reshape(bsz, s, C_WIDTH)
    y = o * jax.nn.silu(g)
    return jnp.einsum('bse,ed->bsd', y, w_out)


def _fwd_setup_inputs(seed: int = 0) -> dict:
    key = jax.random.key(seed)
    ks = jax.random.split(key, 20)
    f32 = jnp.float32
    nrm = lambda k, shp, sc: jax.random.normal(k, shp, f32) * sc
    x = jax.random.normal(ks[0], (BATCH, SEQ, D_MODEL), f32)
    e_w_in = nrm(ks[1], (N_EVEN, D_MODEL, EVEN_IN), D_MODEL ** -0.5)
    e_conv_w = nrm(ks[2], (N_EVEN, CONV_WIDTH, LRU_WIDTH), CONV_WIDTH ** -0.5)
    e_conv_b = nrm(ks[3], (N_EVEN, LRU_WIDTH), 0.02)
    e_w_gate_a = nrm(ks[4], (N_EVEN, LRU_BLOCKS, LRU_BLOCK, LRU_BLOCK), LRU_BLOCK ** -0.5)
    e_b_gate_a = nrm(ks[5], (N_EVEN, LRU_WIDTH), 0.02)
    e_w_gate_x = nrm(ks[6], (N_EVEN, LRU_BLOCKS, LRU_BLOCK, LRU_BLOCK), LRU_BLOCK ** -0.5)
    e_b_gate_x = nrm(ks[7], (N_EVEN, LRU_WIDTH), 0.02)
    u = jax.random.uniform(ks[8], (N_EVEN, LRU_WIDTH), f32, 0.9, 0.999)
    a0 = u ** (1.0 / LRU_C)
    e_lru_lambda = jnp.log(a0) - jnp.log1p(-a0)
    e_w_out = nrm(ks[9], (N_EVEN, EVEN_MIX, D_MODEL), EVEN_MIX ** -0.5 * DEEPNORM_BETA)
    e_ln_g = 1.0 + nrm(ks[10], (N_EVEN, D_MODEL), 0.02)
    e_ln_b = nrm(ks[11], (N_EVEN, D_MODEL), 0.02)
    o_w_in = nrm(ks[12], (N_ODD, D_MODEL, ODD_IN), D_MODEL ** -0.5)
    o_sinks = nrm(ks[13], (N_ODD, C_HEADS), 1.0)
    o_w_out = nrm(ks[14], (N_ODD, C_WIDTH, D_MODEL), C_WIDTH ** -0.5 * DEEPNORM_BETA)
    o_ln_g = 1.0 + nrm(ks[15], (N_ODD, D_MODEL), 0.02)
    o_ln_b = nrm(ks[16], (N_ODD, D_MODEL), 0.02)
    return {"x": x, "e_w_in": e_w_in, "e_conv_w": e_conv_w, "e_conv_b": e_conv_b,
            "e_w_gate_a": e_w_gate_a, "e_b_gate_a": e_b_gate_a, "e_w_gate_x": e_w_gate_x,
            "e_b_gate_x": e_b_gate_x, "e_lru_lambda": e_lru_lambda, "e_w_out": e_w_out,
            "e_ln_g": e_ln_g, "e_ln_b": e_ln_b, "o_w_in": o_w_in, "o_sinks": o_sinks,
            "o_w_out": o_w_out, "o_ln_g": o_ln_g, "o_ln_b": o_ln_b}


def _fwd_reference(x, e_w_in, e_conv_w, e_conv_b, e_w_gate_a, e_b_gate_a, e_w_gate_x, e_b_gate_x,
              e_lru_lambda, e_w_out, e_ln_g, e_ln_b, o_w_in, o_sinks, o_w_out, o_ln_g, o_ln_b):
    for layer in range(DEPTH):
        li = layer // 2
        if layer % 2 == 0:
            y = _even_layer(x, e_w_in[li], e_conv_w[li], e_conv_b[li], e_w_gate_a[li], e_b_gate_a[li],
                            e_w_gate_x[li], e_b_gate_x[li], e_lru_lambda[li], e_w_out[li])
            x = _layer_norm(DEEPNORM_ALPHA * x + y, e_ln_g[li], e_ln_b[li])
        else:
            y = _odd_layer(x, o_w_in[li], o_sinks[li], o_w_out[li])
            x = _layer_norm(DEEPNORM_ALPHA * x + y, o_ln_g[li], o_ln_b[li])
    return x


import jax as _jax
import jax.numpy as _jnp

TWIN_FORMAT = 'train_step'
FWD_PARAMS = ['x', 'e_w_in', 'e_conv_w', 'e_conv_b', 'e_w_gate_a', 'e_b_gate_a', 'e_w_gate_x', 'e_b_gate_x', 'e_lru_lambda', 'e_w_out', 'e_ln_g', 'e_ln_b', 'o_w_in', 'o_sinks', 'o_w_out', 'o_ln_g', 'o_ln_b']
TWIN_WEIGHTS = ['e_w_in', 'e_conv_w', 'e_conv_b', 'e_w_gate_a', 'e_b_gate_a', 'e_w_gate_x', 'e_b_gate_x', 'e_lru_lambda', 'e_w_out', 'e_ln_g', 'e_ln_b', 'o_w_in', 'o_sinks', 'o_w_out', 'o_ln_g', 'o_ln_b']
TWIN_DIFF_INPUT = 'x'
TWIN_INPUTS = ['x', 'e_w_in', 'e_conv_w', 'e_conv_b', 'e_w_gate_a', 'e_b_gate_a', 'e_w_gate_x', 'e_b_gate_x', 'e_lru_lambda', 'e_w_out', 'e_ln_g', 'e_ln_b', 'o_w_in', 'o_sinks', 'o_w_out', 'o_ln_g', 'o_ln_b', 'loss_target', 'm_e_w_in', 'm_e_conv_w', 'm_e_conv_b', 'm_e_w_gate_a', 'm_e_b_gate_a', 'm_e_w_gate_x', 'm_e_b_gate_x', 'm_e_lru_lambda', 'm_e_w_out', 'm_e_ln_g', 'm_e_ln_b', 'm_o_w_in', 'm_o_sinks', 'm_o_w_out', 'm_o_ln_g', 'm_o_ln_b', 'v_e_w_in', 'v_e_conv_w', 'v_e_conv_b', 'v_e_w_gate_a', 'v_e_b_gate_a', 'v_e_w_gate_x', 'v_e_b_gate_x', 'v_e_lru_lambda', 'v_e_w_out', 'v_e_ln_g', 'v_e_ln_b', 'v_o_w_in', 'v_o_sinks', 'v_o_w_out', 'v_o_ln_g', 'v_o_ln_b']
TWIN_OUTPUTS = ['loss', 'grad_x', 'grad_e_w_in', 'grad_e_conv_w', 'grad_e_conv_b', 'grad_e_w_gate_a', 'grad_e_b_gate_a', 'grad_e_w_gate_x', 'grad_e_b_gate_x', 'grad_e_lru_lambda', 'grad_e_w_out', 'grad_e_ln_g', 'grad_e_ln_b', 'grad_o_w_in', 'grad_o_sinks', 'grad_o_w_out', 'grad_o_ln_g', 'grad_o_ln_b', 'delta_e_w_in', 'delta_e_conv_w', 'delta_e_conv_b', 'delta_e_w_gate_a', 'delta_e_b_gate_a', 'delta_e_w_gate_x', 'delta_e_b_gate_x', 'delta_e_lru_lambda', 'delta_e_w_out', 'delta_e_ln_g', 'delta_e_ln_b', 'delta_o_w_in', 'delta_o_sinks', 'delta_o_w_out', 'delta_o_ln_g', 'delta_o_ln_b', 'new_m_e_w_in', 'new_m_e_conv_w', 'new_m_e_conv_b', 'new_m_e_w_gate_a', 'new_m_e_b_gate_a', 'new_m_e_w_gate_x', 'new_m_e_b_gate_x', 'new_m_e_lru_lambda', 'new_m_e_w_out', 'new_m_e_ln_g', 'new_m_e_ln_b', 'new_m_o_w_in', 'new_m_o_sinks', 'new_m_o_w_out', 'new_m_o_ln_g', 'new_m_o_ln_b', 'new_v_e_w_in', 'new_v_e_conv_w', 'new_v_e_conv_b', 'new_v_e_w_gate_a', 'new_v_e_b_gate_a', 'new_v_e_w_gate_x', 'new_v_e_b_gate_x', 'new_v_e_lru_lambda', 'new_v_e_w_out', 'new_v_e_ln_g', 'new_v_e_ln_b', 'new_v_o_w_in', 'new_v_o_sinks', 'new_v_o_w_out', 'new_v_o_ln_g', 'new_v_o_ln_b']
TWIN_LEAF_KINDS = {'loss': 'loss', 'grad_x': 'grad_x', 'grad_e_w_in': 'grad_w', 'grad_e_conv_w': 'grad_w', 'grad_e_conv_b': 'grad_w', 'grad_e_w_gate_a': 'grad_w', 'grad_e_b_gate_a': 'grad_w', 'grad_e_w_gate_x': 'grad_w', 'grad_e_b_gate_x': 'grad_w', 'grad_e_lru_lambda': 'grad_w', 'grad_e_w_out': 'grad_w', 'grad_e_ln_g': 'grad_w', 'grad_e_ln_b': 'grad_w', 'grad_o_w_in': 'grad_w', 'grad_o_sinks': 'grad_w', 'grad_o_w_out': 'grad_w', 'grad_o_ln_g': 'grad_w', 'grad_o_ln_b': 'grad_w', 'delta_e_w_in': 'delta_w', 'delta_e_conv_w': 'delta_w', 'delta_e_conv_b': 'delta_w', 'delta_e_w_gate_a': 'delta_w', 'delta_e_b_gate_a': 'delta_w', 'delta_e_w_gate_x': 'delta_w', 'delta_e_b_gate_x': 'delta_w', 'delta_e_lru_lambda': 'delta_w', 'delta_e_w_out': 'delta_w', 'delta_e_ln_g': 'delta_w', 'delta_e_ln_b': 'delta_w', 'delta_o_w_in': 'delta_w', 'delta_o_sinks': 'delta_w', 'delta_o_w_out': 'delta_w', 'delta_o_ln_g': 'delta_w', 'delta_o_ln_b': 'delta_w', 'new_m_e_w_in': 'new_m', 'new_m_e_conv_w': 'new_m', 'new_m_e_conv_b': 'new_m', 'new_m_e_w_gate_a': 'new_m', 'new_m_e_b_gate_a': 'new_m', 'new_m_e_w_gate_x': 'new_m', 'new_m_e_b_gate_x': 'new_m', 'new_m_e_lru_lambda': 'new_m', 'new_m_e_w_out': 'new_m', 'new_m_e_ln_g': 'new_m', 'new_m_e_ln_b': 'new_m', 'new_m_o_w_in': 'new_m', 'new_m_o_sinks': 'new_m', 'new_m_o_w_out': 'new_m', 'new_m_o_ln_g': 'new_m', 'new_m_o_ln_b': 'new_m', 'new_v_e_w_in': 'new_v', 'new_v_e_conv_w': 'new_v', 'new_v_e_conv_b': 'new_v', 'new_v_e_w_gate_a': 'new_v', 'new_v_e_b_gate_a': 'new_v', 'new_v_e_w_gate_x': 'new_v', 'new_v_e_b_gate_x': 'new_v', 'new_v_e_lru_lambda': 'new_v', 'new_v_e_w_out': 'new_v', 'new_v_e_ln_g': 'new_v', 'new_v_e_ln_b': 'new_v', 'new_v_o_w_in': 'new_v', 'new_v_o_sinks': 'new_v', 'new_v_o_w_out': 'new_v', 'new_v_o_ln_g': 'new_v', 'new_v_o_ln_b': 'new_v'}


def _forward(args):
    return _fwd_reference(*[args[k] for k in FWD_PARAMS])


def _output_shape():
    def fwd():
        inp = _fwd_setup_inputs(0)
        return _fwd_reference(*[inp[k] for k in FWD_PARAMS])
    out = _jax.eval_shape(fwd)
    return out.shape, out.dtype

N_MICROBATCH = 1
ADAM_LR = 0.001
ADAM_B1 = 0.9
ADAM_B2 = 0.999
ADAM_EPS = 1e-08
ADAM_WD = 0.01
ADAM_STEP = 10
PER_EXAMPLE_BATCH_AXIS = {'x': 0, 'loss_target': 0}
SHARED_INPUTS = []
_WEIGHT_DTYPES = {'e_w_in': _jnp.float32, 'e_conv_w': _jnp.float32, 'e_conv_b': _jnp.float32, 'e_w_gate_a': _jnp.float32, 'e_b_gate_a': _jnp.float32, 'e_w_gate_x': _jnp.float32, 'e_b_gate_x': _jnp.float32, 'e_lru_lambda': _jnp.float32, 'e_w_out': _jnp.float32, 'e_ln_g': _jnp.float32, 'e_ln_b': _jnp.float32, 'o_w_in': _jnp.float32, 'o_sinks': _jnp.float32, 'o_w_out': _jnp.float32, 'o_ln_g': _jnp.float32, 'o_ln_b': _jnp.float32}
MOMENT_SCALE = {'e_w_in': 2.123342e-02, 'e_conv_w': 2.664212e-02, 'e_conv_b': 4.339532e-01, 'e_w_gate_a': 1.054939e-02, 'e_b_gate_a': 7.077137e-03, 'e_w_gate_x': 1.938115e-02, 'e_b_gate_x': 9.644265e-03, 'e_lru_lambda': 1.385695e-02, 'e_w_out': 7.255714e-02, 'e_ln_g': 2.322655e+00, 'e_ln_b': 9.918009e-01, 'o_w_in': 1.890360e-02, 'o_sinks': 4.493753e-02, 'o_w_out': 3.152940e-02, 'o_ln_g': 6.399815e+01, 'o_ln_b': 1.565768e+00}


def _to_microbatches(a, axis):
    t = _jnp.moveaxis(a, axis, 0)
    t = t.reshape((N_MICROBATCH, t.shape[0] // N_MICROBATCH) + t.shape[1:])
    return _jnp.moveaxis(t, 1, axis + 1)


def setup_inputs(seed: int = 0) -> dict:
    inp = _fwd_setup_inputs(seed)
    key = _jax.random.fold_in(_jax.random.key(seed), 7919)
    shape, _ = _output_shape()
    out = dict(inp)
    out["loss_target"] = _jax.random.normal(_jax.random.fold_in(key, 0), shape, _jnp.float32)
    for i, name in enumerate(TWIN_WEIGHTS):
        w = inp[name].astype(_jnp.float32)
        if MOMENT_SCALE is None:
            s = _jnp.sqrt(_jnp.mean(_jnp.square(w)) + 1e-30)
        else:
            s = MOMENT_SCALE[name]
        km, kv = _jax.random.split(_jax.random.fold_in(key, i + 1))
        out[name] = w
        out["m_" + name] = s * _jax.random.normal(km, w.shape, _jnp.float32)
        out["v_" + name] = (s * s) * _jax.random.uniform(kv, w.shape, _jnp.float32, 0.5, 1.5)
    if N_MICROBATCH > 1:
        for name, axis in PER_EXAMPLE_BATCH_AXIS.items():
            out[name] = _to_microbatches(out[name], axis)
    return {'x': out['x'], 'e_w_in': out['e_w_in'], 'e_conv_w': out['e_conv_w'], 'e_conv_b': out['e_conv_b'], 'e_w_gate_a': out['e_w_gate_a'], 'e_b_gate_a': out['e_b_gate_a'], 'e_w_gate_x': out['e_w_gate_x'], 'e_b_gate_x': out['e_b_gate_x'], 'e_lru_lambda': out['e_lru_lambda'], 'e_w_out': out['e_w_out'], 'e_ln_g': out['e_ln_g'], 'e_ln_b': out['e_ln_b'], 'o_w_in': out['o_w_in'], 'o_sinks': out['o_sinks'], 'o_w_out': out['o_w_out'], 'o_ln_g': out['o_ln_g'], 'o_ln_b': out['o_ln_b'], 'loss_target': out['loss_target'], 'm_e_w_in': out['m_e_w_in'], 'm_e_conv_w': out['m_e_conv_w'], 'm_e_conv_b': out['m_e_conv_b'], 'm_e_w_gate_a': out['m_e_w_gate_a'], 'm_e_b_gate_a': out['m_e_b_gate_a'], 'm_e_w_gate_x': out['m_e_w_gate_x'], 'm_e_b_gate_x': out['m_e_b_gate_x'], 'm_e_lru_lambda': out['m_e_lru_lambda'], 'm_e_w_out': out['m_e_w_out'], 'm_e_ln_g': out['m_e_ln_g'], 'm_e_ln_b': out['m_e_ln_b'], 'm_o_w_in': out['m_o_w_in'], 'm_o_sinks': out['m_o_sinks'], 'm_o_w_out': out['m_o_w_out'], 'm_o_ln_g': out['m_o_ln_g'], 'm_o_ln_b': out['m_o_ln_b'], 'v_e_w_in': out['v_e_w_in'], 'v_e_conv_w': out['v_e_conv_w'], 'v_e_conv_b': out['v_e_conv_b'], 'v_e_w_gate_a': out['v_e_w_gate_a'], 'v_e_b_gate_a': out['v_e_b_gate_a'], 'v_e_w_gate_x': out['v_e_w_gate_x'], 'v_e_b_gate_x': out['v_e_b_gate_x'], 'v_e_lru_lambda': out['v_e_lru_lambda'], 'v_e_w_out': out['v_e_w_out'], 'v_e_ln_g': out['v_e_ln_g'], 'v_e_ln_b': out['v_e_ln_b'], 'v_o_w_in': out['v_o_w_in'], 'v_o_sinks': out['v_o_sinks'], 'v_o_w_out': out['v_o_w_out'], 'v_o_ln_g': out['v_o_ln_g'], 'v_o_ln_b': out['v_o_ln_b']}


def _loss(weights, diff, rest, loss_target):
    with _jax.named_scope("forward"):
        args = {**rest, TWIN_DIFF_INPUT: diff, **{k: w.astype(_WEIGHT_DTYPES[k]) for k, w in weights.items()}}
        y = _forward(args)
    with _jax.named_scope("loss_head"):
        err = _jnp.square(y.astype(_jnp.float32) - loss_target)
        return 0.5 * _jnp.sum(_jnp.mean(err, axis=-1)) if err.ndim else 0.5 * err


def _adamw(w, g, m, v):
    m = ADAM_B1 * m + (1.0 - ADAM_B1) * g
    v = ADAM_B2 * v + (1.0 - ADAM_B2) * _jnp.square(g)
    m_hat = m / (1.0 - ADAM_B1 ** ADAM_STEP)
    v_hat = v / (1.0 - ADAM_B2 ** ADAM_STEP)
    delta = -ADAM_LR * (m_hat / (_jnp.sqrt(v_hat) + ADAM_EPS) + ADAM_WD * w)
    return delta, m, v


def reference(x, e_w_in, e_conv_w, e_conv_b, e_w_gate_a, e_b_gate_a, e_w_gate_x, e_b_gate_x, e_lru_lambda, e_w_out, e_ln_g, e_ln_b, o_w_in, o_sinks, o_w_out, o_ln_g, o_ln_b, loss_target, m_e_w_in, m_e_conv_w, m_e_conv_b, m_e_w_gate_a, m_e_b_gate_a, m_e_w_gate_x, m_e_b_gate_x, m_e_lru_lambda, m_e_w_out, m_e_ln_g, m_e_ln_b, m_o_w_in, m_o_sinks, m_o_w_out, m_o_ln_g, m_o_ln_b, v_e_w_in, v_e_conv_w, v_e_conv_b, v_e_w_gate_a, v_e_b_gate_a, v_e_w_gate_x, v_e_b_gate_x, v_e_lru_lambda, v_e_w_out, v_e_ln_g, v_e_ln_b, v_o_w_in, v_o_sinks, v_o_w_out, v_o_ln_g, v_o_ln_b):
    given = dict(x=x, e_w_in=e_w_in, e_conv_w=e_conv_w, e_conv_b=e_conv_b, e_w_gate_a=e_w_gate_a, e_b_gate_a=e_b_gate_a, e_w_gate_x=e_w_gate_x, e_b_gate_x=e_b_gate_x, e_lru_lambda=e_lru_lambda, e_w_out=e_w_out, e_ln_g=e_ln_g, e_ln_b=e_ln_b, o_w_in=o_w_in, o_sinks=o_sinks, o_w_out=o_w_out, o_ln_g=o_ln_g, o_ln_b=o_ln_b, loss_target=loss_target, m_e_w_in=m_e_w_in, m_e_conv_w=m_e_conv_w, m_e_conv_b=m_e_conv_b, m_e_w_gate_a=m_e_w_gate_a, m_e_b_gate_a=m_e_b_gate_a, m_e_w_gate_x=m_e_w_gate_x, m_e_b_gate_x=m_e_b_gate_x, m_e_lru_lambda=m_e_lru_lambda, m_e_w_out=m_e_w_out, m_e_ln_g=m_e_ln_g, m_e_ln_b=m_e_ln_b, m_o_w_in=m_o_w_in, m_o_sinks=m_o_sinks, m_o_w_out=m_o_w_out, m_o_ln_g=m_o_ln_g, m_o_ln_b=m_o_ln_b, v_e_w_in=v_e_w_in, v_e_conv_w=v_e_conv_w, v_e_conv_b=v_e_conv_b, v_e_w_gate_a=v_e_w_gate_a, v_e_b_gate_a=v_e_b_gate_a, v_e_w_gate_x=v_e_w_gate_x, v_e_b_gate_x=v_e_b_gate_x, v_e_lru_lambda=v_e_lru_lambda, v_e_w_out=v_e_w_out, v_e_ln_g=v_e_ln_g, v_e_ln_b=v_e_ln_b, v_o_w_in=v_o_w_in, v_o_sinks=v_o_sinks, v_o_w_out=v_o_w_out, v_o_ln_g=v_o_ln_g, v_o_ln_b=v_o_ln_b)
    weights = {n: given[n] for n in TWIN_WEIGHTS}
    shared = {n: given[n] for n in SHARED_INPUTS}
    per_example = {n: given[n] for n in ['x']}
    grad_fn = _jax.value_and_grad(_loss, argnums=(0, 1))

    def one_microbatch(ex, loss_target):
        ex = dict(ex)
        diff = ex.pop(TWIN_DIFF_INPUT)
        return grad_fn(weights, diff, {**shared, **ex}, loss_target)

    if N_MICROBATCH == 1:
        loss, (grad_w, grad_x) = one_microbatch(per_example, given["loss_target"])
    else:
        def body(carry, xs):
            loss_sum, grad_sum = carry
            l_k, (gw_k, gx_k) = one_microbatch(xs[0], xs[1])
            with _jax.named_scope("update"):
                return (loss_sum + l_k, _jax.tree.map(_jnp.add, grad_sum, gw_k)), gx_k

        init = (_jnp.zeros((), _jnp.float32), _jax.tree.map(_jnp.zeros_like, weights))
        (loss, grad_w), grad_x = _jax.lax.scan(body, init, (per_example, given["loss_target"]))
    with _jax.named_scope("update"):
        delta_w, new_m, new_v = {}, {}, {}
        for n in TWIN_WEIGHTS:
            delta_w[n], new_m[n], new_v[n] = _adamw(weights[n], grad_w[n], given["m_" + n], given["v_" + n])
    return (loss, grad_x, *[grad_w[n] for n in TWIN_WEIGHTS], *[delta_w[n] for n in TWIN_WEIGHTS],
            *[new_m[n] for n in TWIN_WEIGHTS], *[new_v[n] for n in TWIN_WEIGHTS])
```

```python
import functools
import math

import jax
import jax.numpy as jnp
from jax import lax
from jax.experimental import pallas as pl
from jax.experimental.pallas import tpu as pltpu

F32 = jnp.float32
BF16 = jnp.bfloat16

N_DEV = 8
D_MODEL = 1024
LRU_BLOCKS = 8
LRU_BLOCK = 128
LRU_C = 8.0
SB_HEADS = 8
SB_HEAD_DIM = 128
C_HEADS = 16
C_KV_HEADS = 2
C_GROUP = 8
C_HEAD_DIM = 64
WINDOW = 128
DEPTH = 2
ALPHA = float((2 * DEPTH) ** 0.25)
LN_EPS = 1e-5
ADAM_LR = 0.001
ADAM_B1 = 0.9
ADAM_B2 = 0.999
ADAM_EPS = 1e-08
ADAM_WD = 0.01
ADAM_STEP = 10

VMEM_LIMIT = 56 * 1024 * 1024

NN = ((1,), (0,))
NT = ((1,), (1,))
TN = ((0,), (0,))


def _dot(a, b, dims):
    return lax.dot_general(a, b, (dims, ((), ())), preferred_element_type=F32)


def _sigmoid(x):
    return 1.0 / (1.0 + jnp.exp(-x))


def _cparams(sem, vmem=VMEM_LIMIT):
    return pltpu.CompilerParams(dimension_semantics=sem, vmem_limit_bytes=vmem)


def _matmul(a, b, *, mode, n_out=None, b_off=0, out_dtype=F32, res=None, alpha=1.0,
            tm=512, tn=512, tk=1024, name):
    if mode == "nn":
        m, k = a.shape
    elif mode == "nt":
        m, k = a.shape
    else:
        k, m = a.shape
    n = n_out if n_out is not None else (b.shape[0] if mode == "nt" else b.shape[1])
    tm, tn, tk = min(tm, m), min(tn, n), min(tk, k)
    assert m % tm == 0 and n % tn == 0 and k % tk == 0 and b_off % tn == 0
    nk = k // tk
    jo = b_off // tn
    dims = {"nn": NN, "nt": NT, "tn": TN}[mode]

    def body(*refs):
        if res is None:
            a_ref, b_ref, o_ref, acc_ref = refs
            r_ref = None
        else:
            a_ref, b_ref, r_ref, o_ref, acc_ref = refs
        kk = pl.program_id(2)
        part = _dot(a_ref[...].astype(BF16), b_ref[...].astype(BF16), dims)

        @pl.when(kk == 0)
        def _():
            acc_ref[...] = part

        @pl.when(kk > 0)
        def _():
            acc_ref[...] += part

        @pl.when(kk == nk - 1)
        def _():
            out = acc_ref[...]
            if r_ref is not None:
                out = out + alpha * r_ref[...]
            o_ref[...] = out.astype(out_dtype)

    if mode == "tn":
        a_spec = pl.BlockSpec((tk, tm), lambda i, j, kk: (kk, i))
    else:
        a_spec = pl.BlockSpec((tm, tk), lambda i, j, kk: (i, kk))
    if mode == "nt":
        b_spec = pl.BlockSpec((tn, tk), lambda i, j, kk: (j + jo, kk))
    else:
        b_spec = pl.BlockSpec((tk, tn), lambda i, j, kk: (kk, j + jo))
    o_spec = pl.BlockSpec((tm, tn), lambda i, j, kk: (i, j))
    in_specs = [a_spec, b_spec]
    args = [a, b]
    if res is not None:
        in_specs.append(o_spec)
        args.append(res)
    return pl.pallas_call(
        body, name=name, grid=(m // tm, n // tn, nk),
        in_specs=in_specs, out_specs=o_spec,
        out_shape=jax.ShapeDtypeStruct((m, n), out_dtype),
        scratch_shapes=[pltpu.VMEM((tm, tn), F32)],
        compiler_params=_cparams(("parallel", "parallel", "arbitrary")),
    )(*args)


def _rowwise(fn, row_ins, vec_ins, row_outs, acc_outs, *, tr=256, name):
    s = row_ins[0][0].shape[0]
    assert s % tr == 0
    n_ri, n_vi, n_ro = len(row_ins), len(vec_ins), len(row_outs)

    def body(*refs):
        rows = [r[...] for r in refs[:n_ri]]
        vecs = [r[...] for r in refs[n_ri:n_ri + n_vi]]
        o_refs = refs[n_ri + n_vi:n_ri + n_vi + n_ro]
        a_refs = refs[n_ri + n_vi + n_ro:]
        outs, accs = fn(rows, vecs)
        for o_ref, o in zip(o_refs, outs):
            o_ref[...] = o.astype(o_ref.dtype)
        if a_refs:
            i = pl.program_id(0)

            @pl.when(i == 0)
            def _():
                for a_ref, part in zip(a_refs, accs):
                    a_ref[...] = part

            @pl.when(i > 0)
            def _():
                for a_ref, part in zip(a_refs, accs):
                    a_ref[...] += part

    in_specs = [pl.BlockSpec((tr, c), functools.partial(lambda i, cb: (i, cb), cb=cb)) for (_, c, cb) in row_ins]
    in_specs += [pl.BlockSpec(v.shape, functools.partial(lambda i, nd: (0,) * nd, nd=v.ndim)) for v in vec_ins]
    out_specs = [pl.BlockSpec((tr, c), lambda i: (i, 0)) for (c, _) in row_outs]
    out_specs += [pl.BlockSpec((r, c), lambda i: (0, 0)) for (r, c) in acc_outs]
    out_shape = [jax.ShapeDtypeStruct((s, c), dt) for (c, dt) in row_outs]
    out_shape += [jax.ShapeDtypeStruct((r, c), F32) for (r, c) in acc_outs]
    return pl.pallas_call(
        body, name=name, grid=(s // tr,), in_specs=in_specs, out_specs=out_specs, out_shape=out_shape,
        compiler_params=_cparams(("arbitrary",)),
    )(*[a for (a, _, _) in row_ins], *vec_ins)


def _ln_stats(z):
    mu = jnp.mean(z, axis=-1, keepdims=True)
    zc = z - mu
    var = jnp.mean(zc * zc, axis=-1, keepdims=True)
    rstd = lax.rsqrt(var + LN_EPS)
    return zc * rstd, rstd


def _ln_bwd(dy, xhat, rstd, g):
    dxh = dy * g
    m1 = jnp.mean(dxh, axis=-1, keepdims=True)
    m2 = jnp.mean(dxh * xhat, axis=-1, keepdims=True)
    return rstd * (dxh - m1 - xhat * m2)


def _colsum(x):
    return jnp.sum(x, axis=0, keepdims=True)


def _ln_fwd(x, y, g, b, *, name):
    def fn(rows, vecs):
        z = ALPHA * rows[0] + rows[1]
        xhat, _ = _ln_stats(z)
        x1 = xhat * vecs[0] + vecs[1]
        return [z, x1, x1], []
    c = x.shape[1]
    return _rowwise(fn, [(x, c, 0), (y, c, 0)], [g, b], [(c, F32), (c, F32), (c, BF16)], [], name=name)


def _ln_loss_bwd(x, y, target, g, b, *, name):
    c = x.shape[1]

    def fn(rows, vecs):
        z = ALPHA * rows[0] + rows[1]
        xhat, rstd = _ln_stats(z)
        d = xhat * vecs[0] + vecs[1] - rows[2]
        dy = d * (1.0 / c)
        dz = _ln_bwd(dy, xhat, rstd, vecs[0])
        return [dz], [_colsum(0.5 * d * d * (1.0 / c)), _colsum(dy * xhat), _colsum(dy)]
    return _rowwise(fn, [(x, c, 0), (y, c, 0), (target, c, 0)], [g, b], [(c, F32)], [(1, c)] * 3, name=name)


def _ln_bwd_call(z, dy, g, *, name):
    c = z.shape[1]

    def fn(rows, vecs):
        xhat, rstd = _ln_stats(rows[0])
        dz = _ln_bwd(rows[1], xhat, rstd, vecs[0])
        return [dz], [_colsum(rows[1] * xhat), _colsum(rows[1])]
    return _rowwise(fn, [(z, c, 0), (dy, c, 0)], [g], [(c, F32)], [(1, c)] * 2, name=name)


def _gate_bwd(dy, o, gate, gate_cb, *, do_dtype, name):
    dy_arr, dy_cb = dy
    c = o.shape[1]

    def fn(rows, vecs):
        d, oo, gg = rows
        sg = _sigmoid(gg)
        return [d * (gg * sg), d * oo * (sg * (1.0 + gg * (1.0 - sg)))], []
    return _rowwise(fn, [(dy_arr, c, dy_cb), (o, c, 0), (gate, c, gate_cb)], [], [(c, do_dtype), (c, BF16)], [], name=name)


LRU_T = 256
HALO = 8


def _log1p(y):
    u = 1.0 + y
    return jnp.where(u == 1.0, y, jnp.log(u) * (y / (u - 1.0)))


def _lru_gates(c, wa_ref, wx_ref, pv):
    c16 = c.astype(BF16)
    pre_r = jnp.concatenate(
        [_dot(c16[:, n * LRU_BLOCK:(n + 1) * LRU_BLOCK], wa_ref[n], NN) for n in range(LRU_BLOCKS)], axis=1)
    pre_i = jnp.concatenate(
        [_dot(c16[:, n * LRU_BLOCK:(n + 1) * LRU_BLOCK], wx_ref[n], NN) for n in range(LRU_BLOCKS)], axis=1)
    r = _sigmoid(pre_r + pv[5:6])
    ig = _sigmoid(pre_i + pv[6:7])
    lam = pv[7:8]
    ls = jnp.minimum(lam, 0.0) - _log1p(jnp.exp(-jnp.abs(lam)))
    la = LRU_C * r * ls
    a = jnp.exp(la)
    a2 = a * a
    m = jnp.sqrt(jnp.tanh(-la) * (a2 + 1.0))
    return c16, r, ig, ls, la, a, a2, m


def _conv(ext_ref, x, pv, t):
    return (pv[4:5] + pv[3:4] * x + pv[2:3] * ext_ref[pl.ds(HALO - 1, t), :]
            + pv[1:2] * ext_ref[pl.ds(HALO - 2, t), :] + pv[0:1] * ext_ref[pl.ds(HALO - 3, t), :])


def _rglru_fwd(h0a, pvec, wa16, wx16):
    s = h0a.shape[0]
    w = D_MODEL
    t = min(LRU_T, s)
    assert s % t == 0

    def body(ax_ref, ag_ref, pv_ref, wa_ref, wx_ref, ya_ref, h_ref, ext_ref, a_ref, u_ref, hc_ref):
        i = pl.program_id(0)

        @pl.when(i == 0)
        def _():
            ext_ref[pl.ds(0, HALO), :] = jnp.zeros((HALO, w), F32)
            hc_ref[...] = jnp.zeros((1, w), F32)

        pv = pv_ref[...]
        ax = ax_ref[...]
        ext_ref[pl.ds(HALO, t), :] = ax
        c = _conv(ext_ref, ax, pv, t)
        ext_ref[pl.ds(0, HALO), :] = ax[t - HALO:, :]
        _, _, ig, _, _, a, _, m = _lru_gates(c, wa_ref, wx_ref, pv)
        a_ref[...] = a
        u_ref[...] = m * (ig * c)

        def step(k, h):
            h = a_ref[pl.ds(k, 1), :] * h + u_ref[pl.ds(k, 1), :]
            h_ref[pl.ds(k, 1), :] = h
            return h

        hc_ref[...] = lax.fori_loop(0, t, step, hc_ref[...], unroll=8)
        ag = ag_ref[...]
        ya_ref[...] = (h_ref[...] * (ag * _sigmoid(ag))).astype(BF16)

    full = lambda shp: pl.BlockSpec(shp, lambda i: (0,) * len(shp))
    return pl.pallas_call(
        body, name="rglru_fwd", grid=(s // t,),
        in_specs=[pl.BlockSpec((t, w), lambda i: (i, 0)), pl.BlockSpec((t, w), lambda i: (i, 1)),
                  full((8, w)), full((LRU_BLOCKS, LRU_BLOCK, LRU_BLOCK)), full((LRU_BLOCKS, LRU_BLOCK, LRU_BLOCK))],
        out_specs=[pl.BlockSpec((t, w), lambda i: (i, 0)), pl.BlockSpec((t, w), lambda i: (i, 0))],
        out_shape=[jax.ShapeDtypeStruct((s, w), BF16), jax.ShapeDtypeStruct((s, w), F32)],
        scratch_shapes=[pltpu.VMEM((t + HALO, w), F32), pltpu.VMEM((t, w), F32), pltpu.VMEM((t, w), F32),
                        pltpu.VMEM((1, w), F32)],
        compiler_params=_cparams(("arbitrary",)),
    )(h0a, h0a, pvec, wa16, wx16)


def _rglru_bwd(h0a, h, dh, pvec, wa16, wx16):
    s = h0a.shape[0]
    w = D_MODEL
    t = min(LRU_T, s)
    nb = s // t
    hb = t // HALO

    def body(ax_ref, axh_ref, h_ref, hh_ref, dh_ref, pv_ref, wa_ref, wx_ref,
             dax_ref, dwa_ref, dwx_ref, dpv_ref, ext_ref, hext_ref, dcext_ref, a_ref, g_ref, gc_ref):
        i = pl.program_id(0)
        blk = nb - 1 - i

        @pl.when(i == 0)
        def _():
            dwa_ref[...] = jnp.zeros_like(dwa_ref)
            dwx_ref[...] = jnp.zeros_like(dwx_ref)
            dpv_ref[...] = jnp.zeros_like(dpv_ref)
            gc_ref[...] = jnp.zeros((1, w), F32)
            dcext_ref[pl.ds(t, HALO), :] = jnp.zeros((HALO, w), F32)

        pv = pv_ref[...]
        ax = ax_ref[...]
        keep = (blk > 0).astype(F32)
        ext_ref[pl.ds(0, HALO), :] = axh_ref[...] * keep
        ext_ref[pl.ds(HALO, t), :] = ax
        hext_ref[pl.ds(0, HALO), :] = hh_ref[...] * keep
        hext_ref[pl.ds(HALO, t), :] = h_ref[...]
        c = _conv(ext_ref, ax, pv, t)
        c16, r, ig, ls, _, a, a2, m = _lru_gates(c, wa_ref, wx_ref, pv)
        a_ref[...] = a

        def step(k, carry):
            row = t - 1 - k
            g = dh_ref[pl.ds(row, 1), :] + carry
            g_ref[pl.ds(row, 1), :] = g
            return a_ref[pl.ds(row, 1), :] * g

        gc_ref[...] = lax.fori_loop(0, t, step, gc_ref[...], unroll=8)
        g = g_ref[...]
        hprev = hext_ref[pl.ds(HALO - 1, t), :]
        gm = g * m
        d_la = g * hprev * a - (g * ig * c) * a2 / m
        d_pr = d_la * (LRU_C * ls) * r * (1.0 - r)
        d_pi = gm * c * ig * (1.0 - ig)
        dc = gm * ig
        dpr16 = d_pr.astype(BF16)
        dpi16 = d_pi.astype(BF16)
        dc_parts = []
        for n in range(LRU_BLOCKS):
            sl = slice(n * LRU_BLOCK, (n + 1) * LRU_BLOCK)
            dwa_ref[n] += _dot(c16[:, sl], dpr16[:, sl], TN)
            dwx_ref[n] += _dot(c16[:, sl], dpi16[:, sl], TN)
            dc_parts.append(_dot(dpr16[:, sl], wa_ref[n], NT) + _dot(dpi16[:, sl], wx_ref[n], NT))
        dc = dc + jnp.concatenate(dc_parts, axis=1)
        dcext_ref[pl.ds(0, t), :] = dc
        dax = (pv[3:4] * dc + pv[2:3] * dcext_ref[pl.ds(1, t), :] + pv[1:2] * dcext_ref[pl.ds(2, t), :]
               + pv[0:1] * dcext_ref[pl.ds(3, t), :])
        dax_ref[...] = dax.astype(BF16)
        dcext_ref[pl.ds(t, HALO), :] = dc[:HALO, :]
        sums = [_colsum(dc * ext_ref[pl.ds(HALO - 3 + kk, t), :]) for kk in range(4)]
        sums += [_colsum(dc), _colsum(d_pr), _colsum(d_pi), _colsum(d_la * (LRU_C * r))]
        for kk, part in enumerate(sums):
            dpv_ref[pl.ds(kk, 1), :] += part

        @pl.when(i == nb - 1)
        def _():
            lam = pv[7:8]
            dls = dpv_ref[pl.ds(7, 1), :]
            dpv_ref[pl.ds(7, 1), :] = dls * _sigmoid(-lam)

    full = lambda shp: pl.BlockSpec(shp, lambda i: (0,) * len(shp))
    rev = lambda cb: pl.BlockSpec((t, w), functools.partial(lambda i, cb: (nb - 1 - i, cb), cb=cb))
    halo = lambda cb: pl.BlockSpec(
        (HALO, w), functools.partial(lambda i, cb: (jnp.maximum((nb - 1 - i) * hb - 1, 0), cb), cb=cb))
    gw = (LRU_BLOCKS, LRU_BLOCK, LRU_BLOCK)
    return pl.pallas_call(
        body, name="rglru_bwd", grid=(nb,),
        in_specs=[rev(0), halo(0), rev(0), halo(0), rev(0), full((8, w)), full(gw), full(gw)],
        out_specs=[rev(0), full(gw), full(gw), full((8, w))],
        out_shape=[jax.ShapeDtypeStruct((s, w), BF16), jax.ShapeDtypeStruct(gw, F32),
                   jax.ShapeDtypeStruct(gw, F32), jax.ShapeDtypeStruct((8, w), F32)],
        scratch_shapes=[pltpu.VMEM((t + HALO, w), F32), pltpu.VMEM((t + HALO, w), F32),
                        pltpu.VMEM((t + HALO, w), F32), pltpu.VMEM((t, w), F32), pltpu.VMEM((t, w), F32),
                        pltpu.VMEM((1, w), F32)],
        compiler_params=_cparams(("arbitrary",)),
    )(h0a, h0a, h, h, dh, pvec, wa16, wx16)


SB_T = 256


def _split16(x):
    hi = x.astype(BF16)
    lo = (x - hi.astype(F32)).astype(BF16)
    return jnp.concatenate([hi, lo], axis=0)


def _sb_tile(q, k, scale, tri, run, causal):
    tq = q.shape[0]
    z = _dot(q, k, NT) * scale
    log1mb = -(jnp.maximum(z, 0.0) + jnp.log(1.0 + jnp.exp(-jnp.abs(z))))
    if causal is not None:
        log1mb = jnp.where(causal, log1mb, 0.0)
    cs = _dot(_split16(log1mb), tri, NN)
    cs = cs[:tq] + cs[tq:]
    wgt = jnp.exp(z + cs + run)
    if causal is not None:
        wgt = jnp.where(causal, wgt, 0.0)
    return z, wgt, run + cs[:, 0:1]


def _sb_fwd(qkv, bg):
    s = qkv.shape[0]
    t = min(SB_T, s)
    nq = s // t
    dh = SB_HEAD_DIM
    scale = 1.0 / math.sqrt(dh)

    def body(q_ref, k_ref, v_ref, bg_ref, o_ref, y_ref):
        iq = pl.program_id(1)
        q = q_ref[...]
        row = lax.broadcasted_iota(jnp.int32, (t, t), 0)
        col = lax.broadcasted_iota(jnp.int32, (t, t), 1)
        tri = jnp.where(row >= col, 1.0, 0.0).astype(BF16)
        causal = col < row

        def tile(kb, run, acc, mask):
            off = pl.multiple_of(kb * t, t)
            _, wgt, run = _sb_tile(q, k_ref[pl.ds(off, t), :], scale, tri, run, mask)
            return run, acc + _dot(wgt.astype(BF16), v_ref[pl.ds(off, t), :], NN)

        run, acc = tile(iq, jnp.zeros((t, 1), F32), jnp.zeros((t, dh), F32), causal)

        def loop(j, carry):
            return tile(iq - 1 - j, carry[0], carry[1], None)

        run, acc = lax.fori_loop(0, iq, loop, (run, acc))
        o_ref[...] = acc
        g = bg_ref[...]
        y_ref[...] = (acc * (g * _sigmoid(g))).astype(BF16)

    blk = lambda off: pl.BlockSpec((t, dh), functools.partial(lambda h, i, off: (i, h + off), off=off))
    col = lambda off: pl.BlockSpec((s, dh), functools.partial(lambda h, i, off: (0, h + off), off=off))
    return pl.pallas_call(
        body, name="sb_fwd", grid=(SB_HEADS, nq),
        in_specs=[blk(0), col(SB_HEADS), col(2 * SB_HEADS), blk(0)],
        out_specs=[blk(0), blk(0)],
        out_shape=[jax.ShapeDtypeStruct((s, SB_HEADS * dh), F32), jax.ShapeDtypeStruct((s, SB_HEADS * dh), BF16)],
        compiler_params=_cparams(("parallel", "arbitrary")),
    )(qkv, qkv, qkv, bg)


def _sb_bwd(qkv, dob):
    s = qkv.shape[0]
    t = min(SB_T, s)
    nq = s // t
    dh = SB_HEAD_DIM
    scale = 1.0 / math.sqrt(dh)

    def body(q_ref, k_ref, v_ref, do_ref, dq_ref, dk_ref, dv_ref, e_ref, b_ref):
        iq = pl.program_id(1)

        @pl.when(iq == 0)
        def _():
            dk_ref[...] = jnp.zeros_like(dk_ref)
            dv_ref[...] = jnp.zeros_like(dv_ref)

        q = q_ref[...]
        do = do_ref[...]
        row = lax.broadcasted_iota(jnp.int32, (t, t), 0)
        col = lax.broadcasted_iota(jnp.int32, (t, t), 1)
        tri = jnp.where(row >= col, 1.0, 0.0).astype(BF16)
        tri_x = jnp.where(row < col, 1.0, 0.0).astype(BF16)
        causal = col < row

        def sweep1(kb, run, mask):
            off = pl.multiple_of(kb * t, t)
            v = v_ref[pl.ds(off, t), :]
            z, wgt, run = _sb_tile(q, k_ref[pl.ds(off, t), :], scale, tri, run, mask)
            e_ref[kb] = wgt * _dot(do, v, NT)
            b_ref[kb] = _sigmoid(z)
            dv_ref[pl.ds(off, t), :] += _dot(wgt.astype(BF16), do, TN)
            return run

        run = sweep1(iq, jnp.zeros((t, 1), F32), causal)
        lax.fori_loop(0, iq, lambda j, r: sweep1(iq - 1 - j, r, None), run)

        def sweep2(kb, pre, dq, mask):
            off = pl.multiple_of(kb * t, t)
            e = e_ref[kb]
            beta = b_ref[kb]
            ps = _dot(_split16(e), tri_x, NN)
            ps = ps[:t] + ps[t:]
            dz = e * (1.0 - beta) - beta * (ps + pre)
            if mask is not None:
                dz = jnp.where(mask, dz, 0.0)
            dz16 = (dz * scale).astype(BF16)
            dk_ref[pl.ds(off, t), :] += _dot(dz16, q, TN)
            dq = dq + _dot(dz16, k_ref[pl.ds(off, t), :], NN)
            return pre + ps[:, t - 1:t] + e[:, t - 1:t], dq

        pre, dq = lax.fori_loop(0, iq, lambda kb, c: sweep2(kb, c[0], c[1], None),
                                (jnp.zeros((t, 1), F32), jnp.zeros((t, dh), F32)))
        _, dq = sweep2(iq, pre, dq, causal)
        dq_ref[...] = dq.astype(BF16)

    blk = lambda off: pl.BlockSpec((t, dh), functools.partial(lambda h, i, off: (i, h + off), off=off))
    col = lambda off: pl.BlockSpec((s, dh), functools.partial(lambda h, i, off: (0, h + off), off=off))
    wide = SB_HEADS * dh
    return pl.pallas_call(
        body, name="sb_bwd", grid=(SB_HEADS, nq),
        in_specs=[blk(0), col(SB_HEADS), col(2 * SB_HEADS), blk(0)],
        out_specs=[blk(0), col(0), col(0)],
        out_shape=[jax.ShapeDtypeStruct((s, wide), BF16), jax.ShapeDtypeStruct((s, wide), F32),
                   jax.ShapeDtypeStruct((s, wide), F32)],
        scratch_shapes=[pltpu.VMEM((nq, t, t), F32), pltpu.VMEM((nq, t, t), F32)],
        compiler_params=_cparams(("parallel", "arbitrary")),
    )(qkv, qkv, qkv, dob)


def _alibi_slope(h):
    return float(2.0 ** (-8.0 * (h + 1) / C_HEADS))


def _swa_probs(qh, kw, sink, slope, dist, valid, scale):
    sc = _dot(qh, kw, NT) * scale - slope * dist
    sc = jnp.where(valid, sc, -1e30)
    m = jnp.maximum(jnp.max(sc, axis=-1, keepdims=True), sink)
    p = jnp.exp(sc - m)
    ps = jnp.exp(sink - m)
    inv = 1.0 / (jnp.sum(p, axis=-1, keepdims=True) + ps)
    return p * inv, ps * inv


def _swa_masks(n):
    qb = WINDOW
    i = lax.broadcasted_iota(jnp.int32, (qb, 2 * qb), 0)
    j = lax.broadcasted_iota(jnp.int32, (qb, 2 * qb), 1)
    d = i - j + qb
    valid = (d >= 0) & (d < WINDOW) & ((j >= qb) | (n > 0))
    return d.astype(F32), valid


def _swa_fwd(q16, gate, kvp, sinks):
    s = q16.shape[0]
    qb = WINDOW
    hd = C_HEAD_DIM
    scale = 1.0 / math.sqrt(hd)
    kvw = C_KV_HEADS * hd

    def body(sink_ref, q_ref, g_ref, kp_ref, ko_ref, o_ref, y_ref):
        n = pl.program_id(0)
        dist, valid = _swa_masks(n)
        q = q_ref[...]
        kv = jnp.concatenate([kp_ref[...], ko_ref[...]], axis=0)
        for h in range(C_HEADS):
            c = h // C_GROUP
            kw = kv[:, c * hd:(c + 1) * hd]
            vw = kv[:, kvw + c * hd:kvw + (c + 1) * hd]
            p, _ = _swa_probs(q[:, h * hd:(h + 1) * hd], kw, sink_ref[h], _alibi_slope(h), dist, valid, scale)
            o_ref[:, h * hd:(h + 1) * hd] = _dot(p.astype(BF16), vw, NN)
        g = g_ref[...]
        y_ref[...] = (o_ref[...] * (g * _sigmoid(g))).astype(BF16)

    wide = C_HEADS * hd
    return pl.pallas_call(
        body, name="swa_fwd", grid=(s // qb,),
        in_specs=[pl.BlockSpec(memory_space=pltpu.SMEM),
                  pl.BlockSpec((qb, wide), lambda n: (n, 0)), pl.BlockSpec((qb, wide), lambda n: (n, 0)),
                  pl.BlockSpec((qb, 2 * kvw), lambda n: (n, 0)), pl.BlockSpec((qb, 2 * kvw), lambda n: (n + 1, 0))],
        out_specs=[pl.BlockSpec((qb, wide), lambda n: (n, 0)), pl.BlockSpec((qb, wide), lambda n: (n, 0))],
        out_shape=[jax.ShapeDtypeStruct((s, wide), F32), jax.ShapeDtypeStruct((s, wide), BF16)],
        compiler_params=_cparams(("arbitrary",)),
    )(sinks, q16, gate, kvp, kvp)


def _swa_bwd(q16, do16, kvp, sinks):
    s = q16.shape[0]
    qb = WINDOW
    hd = C_HEAD_DIM
    scale = 1.0 / math.sqrt(hd)
    kvw = C_KV_HEADS * hd
    nblk = s // qb

    def body(sink_ref, q_ref, do_ref, kp_ref, ko_ref, dq_ref, dkv_ref, ds_ref, sacc_ref):
        n = pl.program_id(0)

        @pl.when(n == 0)
        def _():
            dkv_ref[...] = jnp.zeros_like(dkv_ref)
            sacc_ref[...] = jnp.zeros_like(sacc_ref)

        dist, valid = _swa_masks(n)
        lane = lax.broadcasted_iota(jnp.int32, (qb, 128), 1)
        q = q_ref[...]
        do = do_ref[...]
        kv = jnp.concatenate([kp_ref[...], ko_ref[...]], axis=0)
        off = pl.multiple_of(n * qb, qb)
        sacc = sacc_ref[...]
        for c in range(C_KV_HEADS):
            kw = kv[:, c * hd:(c + 1) * hd]
            vw = kv[:, kvw + c * hd:kvw + (c + 1) * hd]
            dkw = jnp.zeros((2 * qb, hd), F32)
            dvw = jnp.zeros((2 * qb, hd), F32)
            for hh in range(C_GROUP):
                h = c * C_GROUP + hh
                qh = q[:, h * hd:(h + 1) * hd]
                doh = do[:, h * hd:(h + 1) * hd]
                p, ps = _swa_probs(qh, kw, sink_ref[h], _alibi_slope(h), dist, valid, scale)
                dp = _dot(doh, vw, NT)
                dd = jnp.sum(p * dp, axis=-1, keepdims=True)
                ds16 = (p * (dp - dd) * scale).astype(BF16)
                sacc = sacc + jnp.where(lane == h, ps * dd, 0.0)
                dq_ref[:, h * hd:(h + 1) * hd] = _dot(ds16, kw, NN).astype(BF16)
                dkw = dkw + _dot(ds16, qh, TN)
                dvw = dvw + _dot(p.astype(BF16), doh, TN)
            dkv_ref[pl.ds(off, 2 * qb), c * hd:(c + 1) * hd] += dkw
            dkv_ref[pl.ds(off, 2 * qb), kvw + c * hd:kvw + (c + 1) * hd] += dvw
        sacc_ref[...] = sacc

        @pl.when(n == nblk - 1)
        def _():
            ds_ref[...] = -jnp.sum(sacc, axis=0, keepdims=True)

    wide = C_HEADS * hd
    return pl.pallas_call(
        body, name="swa_bwd", grid=(nblk,),
        in_specs=[pl.BlockSpec(memory_space=pltpu.SMEM),
                  pl.BlockSpec((qb, wide), lambda n: (n, 0)), pl.BlockSpec((qb, wide), lambda n: (n, 0)),
                  pl.BlockSpec((qb, 2 * kvw), lambda n: (n, 0)), pl.BlockSpec((qb, 2 * kvw), lambda n: (n + 1, 0))],
        out_specs=[pl.BlockSpec((qb, wide), lambda n: (n, 0)),
                   pl.BlockSpec((s + qb, 2 * kvw), lambda n: (0, 0)), pl.BlockSpec((1, 128), lambda n: (0, 0))],
        out_shape=[jax.ShapeDtypeStruct((s, wide), BF16), jax.ShapeDtypeStruct((s + qb, 2 * kvw), F32),
                   jax.ShapeDtypeStruct((1, 128), F32)],
        scratch_shapes=[pltpu.VMEM((qb, 128), F32)],
        compiler_params=_cparams(("arbitrary",)),
    )(sinks, q16, do16, kvp, kvp)


def _adamw(parts, w, m, v, *, name, tr=248):
    npart, r, c = parts.shape
    assert r % tr == 0 and tr % 8 == 0
    c1 = 1.0 / (1.0 - ADAM_B1 ** ADAM_STEP)
    c2 = 1.0 / (1.0 - ADAM_B2 ** ADAM_STEP)

    def body(p_ref, w_ref, m_ref, v_ref, g_ref, d_ref, nm_ref, nv_ref):
        g = p_ref[0]
        for j in range(1, npart):
            g = g + p_ref[j]
        nm = ADAM_B1 * m_ref[...] + (1.0 - ADAM_B1) * g
        nv = ADAM_B2 * v_ref[...] + (1.0 - ADAM_B2) * (g * g)
        g_ref[...] = g
        nm_ref[...] = nm
        nv_ref[...] = nv
        d_ref[...] = -ADAM_LR * ((nm * c1) / (jnp.sqrt(nv * c2) + ADAM_EPS) + ADAM_WD * w_ref[...])

    spec = pl.BlockSpec((tr, c), lambda i: (i, 0))
    return pl.pallas_call(
        body, name=name, grid=(r // tr,),
        in_specs=[pl.BlockSpec((npart, tr, c), lambda i: (0, i, 0)), spec, spec, spec],
        out_specs=[spec] * 4, out_shape=[jax.ShapeDtypeStruct((r, c), F32)] * 4,
        compiler_params=_cparams(("parallel",)),
    )(parts, w, m, v)


def _exchange(bufs, gather, *, name):
    nb = len(bufs)

    def body(*refs):
        src = refs[:nb]
        dst = refs[nb:2 * nb]
        send_sems, recv_sems, local_sems = refs[2 * nb:]
        x, y, c = lax.axis_index("x"), lax.axis_index("y"), lax.axis_index("c")
        me = 4 * x + 2 * y + c
        copies = []
        for b in range(nb):
            mine = src[b] if gather[b] else src[b].at[me]
            local = pltpu.make_async_copy(mine, dst[b].at[me], local_sems.at[b])
            local.start()
            copies.append(local)
        remote = []
        for k in range(1, N_DEV):
            px, py, pc = x ^ (k >> 2), y ^ ((k >> 1) & 1), c ^ (k & 1)
            peer = 4 * px + 2 * py + pc
            for b in range(nb):
                cp = pltpu.make_async_remote_copy(
                    src_ref=src[b] if gather[b] else src[b].at[peer],
                    dst_ref=dst[b].at[me],
                    send_sem=send_sems.at[b, k - 1], recv_sem=recv_sems.at[b, k - 1],
                    device_id=(px, py, pc), device_id_type=pl.DeviceIdType.MESH)
                cp.start()
                remote.append((cp, b, k, peer))
        for cp, b, k, peer in remote:
            cp.wait_send()
        for cp, b, k, peer in remote:
            pltpu.make_async_remote_copy(
                src_ref=src[b] if gather[b] else src[b].at[me],
                dst_ref=dst[b].at[peer],
                send_sem=send_sems.at[b, k - 1], recv_sem=recv_sems.at[b, k - 1],
                device_id=(x, y, c), device_id_type=pl.DeviceIdType.MESH).wait_recv()
        for local in copies:
            local.wait()

    out_shape = []
    for b, g in zip(bufs, gather):
        shp = b.shape if g else b.shape[1:]
        out_shape.append(jax.ShapeDtypeStruct((N_DEV,) + tuple(shp), b.dtype))
    hbm = pl.BlockSpec(memory_space=pl.ANY)
    return pl.pallas_call(
        body, name=name, in_specs=[hbm] * nb, out_specs=[hbm] * nb, out_shape=out_shape,
        scratch_shapes=[pltpu.SemaphoreType.DMA((nb, N_DEV - 1)), pltpu.SemaphoreType.DMA((nb, N_DEV - 1)),
                        pltpu.SemaphoreType.DMA((nb,))],
        compiler_params=pltpu.CompilerParams(has_side_effects=True),
    )(*bufs)


R_WIN_E, R_WOUT_E, R_WIN_O, R_WOUT_O, R_SMALL, R_REPL = 768, 256, 288, 128, 8, 40
SEG_ROWS = R_WIN_E + R_WOUT_E + R_WIN_O + R_WOUT_O + R_SMALL + R_REPL
REPL_ROWS = N_DEV * R_REPL
REPL_LEN = REPL_ROWS * D_MODEL


def _rows(a):
    return a.reshape(-1, D_MODEL)


def _pack_small(conv_w, ln_g, ln_b):
    lead = conv_w.shape[:-2]
    flat = jnp.concatenate([conv_w.reshape(lead + (512,)), ln_g.reshape(lead + (128,)), ln_b.reshape(lead + (128,))],
                           axis=-1)
    return jnp.pad(flat[..., None, :], [(0, 0)] * len(lead) + [(0, R_SMALL - 1), (0, D_MODEL - 768)])


def _pack_sharded(w_in_e, w_out_e, w_in_o, w_out_o, conv_w, ln_g, ln_b):
    return jnp.concatenate([_rows(w_in_e), _rows(w_out_e), _rows(w_in_o), _rows(w_out_o),
                            _pack_small(conv_w[0], ln_g[0], ln_b[0]), jnp.zeros((R_REPL, D_MODEL), F32)], axis=0)


def _unpack_sharded(p):
    r0 = 0
    out = []
    for rows, shp in ((R_WIN_E, (1, D_MODEL, 768)), (R_WOUT_E, (1, 256, D_MODEL)), (R_WIN_O, (1, D_MODEL, 288)),
                      (R_WOUT_O, (1, 128, D_MODEL))):
        out.append(p[r0:r0 + rows].reshape(shp))
        r0 += rows
    small = p[r0]
    out.append(small[0:512].reshape(1, 4, 128))
    out.append(small[512:640].reshape(1, 128))
    out.append(small[640:768].reshape(1, 128))
    return out


REPL_SHAPES = ((1, 8, 128, 128), (1, 8, 128, 128), (1, 1024), (1, 1024), (1, 1024), (1, 1024), (1, 1024), (1, 1024),
               (1, 16))


def _pack_repl(parts):
    flat = jnp.concatenate([p.reshape(-1) for p in parts])
    return jnp.concatenate([flat, jnp.zeros((REPL_LEN - flat.shape[0],), F32)]).reshape(REPL_ROWS, D_MODEL)


def _unpack_repl(p):
    flat = p.reshape(-1)
    out, o = [], 0
    for shp in REPL_SHAPES:
        n = math.prod(shp)
        out.append(flat[o:o + n].reshape(shp))
        o += n
    return out


def kernel(x, e_w_in, e_conv_w, e_conv_b, e_w_gate_a, e_b_gate_a, e_w_gate_x, e_b_gate_x, e_lru_lambda, e_w_out, e_ln_g, e_ln_b, o_w_in, o_sinks, o_w_out, o_ln_g, o_ln_b, loss_target, m_e_w_in, m_e_conv_w, m_e_conv_b, m_e_w_gate_a, m_e_b_gate_a, m_e_w_gate_x, m_e_b_gate_x, m_e_lru_lambda, m_e_w_out, m_e_ln_g, m_e_ln_b, m_o_w_in, m_o_sinks, m_o_w_out, m_o_ln_g, m_o_ln_b, v_e_w_in, v_e_conv_w, v_e_conv_b, v_e_w_gate_a, v_e_b_gate_a, v_e_w_gate_x, v_e_b_gate_x, v_e_lru_lambda, v_e_w_out, v_e_ln_g, v_e_ln_b, v_o_w_in, v_o_sinks, v_o_w_out, v_o_ln_g, v_o_ln_b):
    d = D_MODEL
    x0 = x[0]
    target = loss_target[0]
    s = x0.shape[0]

    wpack = jnp.concatenate([_rows(e_w_in), _rows(e_w_out), _rows(o_w_in), _rows(o_w_out)], axis=0).astype(BF16)
    spack = _pack_small(e_conv_w[0], o_ln_g[0], o_ln_b[0])
    wall, sall = _exchange([wpack, spack], [True, True], name="gather_weights")
    r0 = 0
    wi_e = wall[:, r0:r0 + R_WIN_E].reshape(N_DEV, d, 768).transpose(1, 0, 2).reshape(d, 6 * d)
    r0 += R_WIN_E
    wo_e = wall[:, r0:r0 + R_WOUT_E].reshape(2 * d, d)
    r0 += R_WOUT_E
    wi_o = wall[:, r0:r0 + R_WIN_O].reshape(N_DEV, d, 288).transpose(1, 0, 2).reshape(d, 2304)
    r0 += R_WIN_O
    wo_o = wall[:, r0:r0 + R_WOUT_O].reshape(d, d)
    wi_o = jnp.concatenate([wi_o[:, :1024], wi_o[:, 1280:], wi_o[:, 1024:1280]], axis=1)
    conv_w = sall[:, 0, 0:512].reshape(N_DEV, 4, 128).transpose(1, 0, 2).reshape(4, d)
    ln_g_o = sall[:, 0, 512:640].reshape(1, d)
    ln_b_o = sall[:, 0, 640:768].reshape(1, d)
    pvec = jnp.concatenate([conv_w, e_conv_b, e_b_gate_a, e_b_gate_x, e_lru_lambda], axis=0)
    wa16 = e_w_gate_a[0].astype(BF16)
    wx16 = e_w_gate_x[0].astype(BF16)
    sinks = o_sinks[0]

    x0_16 = x0.astype(BF16)
    h0a = _matmul(x0_16, wi_e, mode="nn", n_out=2 * d, b_off=0, name="l0_in_a")
    qkv = _matmul(x0_16, wi_e, mode="nn", n_out=3 * d, b_off=2 * d, out_dtype=BF16, name="l0_in_qkv")
    bg = _matmul(x0_16, wi_e, mode="nn", n_out=d, b_off=5 * d, name="l0_in_bg")
    ya, hst = _rglru_fwd(h0a, pvec, wa16, wx16)
    ob, yb = _sb_fwd(qkv, bg)
    yab = jnp.concatenate([ya, yb], axis=1)
    y0 = _matmul(yab, wo_e, mode="nn", tk=2048, name="l0_out")
    z0, x1, x1_16 = _ln_fwd(x0, y0, e_ln_g, e_ln_b, name="l0_ln")

    q1 = _matmul(x1_16, wi_o, mode="nn", n_out=d, b_off=0, out_dtype=BF16, name="l1_in_q")
    g1 = _matmul(x1_16, wi_o, mode="nn", n_out=d, b_off=d, name="l1_in_g")
    kv1 = _matmul(x1_16, wi_o, mode="nn", n_out=256, b_off=2 * d, tn=256, out_dtype=BF16, name="l1_in_kv")
    kvp = jnp.concatenate([jnp.zeros((WINDOW, 256), BF16), kv1], axis=0)
    o1, y1 = _swa_fwd(q1, g1, kvp, sinks)
    yy1 = _matmul(y1, wo_o, mode="nn", name="l1_out")
    dz1, loss_cols, dg_o, db_o = _ln_loss_bwd(x1, yy1, target, ln_g_o, ln_b_o, name="l1_ln_loss")
    loss = lax.psum(jnp.sum(loss_cols), ("x", "y", "c"))

    dz1_16 = dz1.astype(BF16)
    dwo_o = _matmul(y1, dz1_16, mode="tn", tk=512, name="l1_dwout")
    dy1 = _matmul(dz1_16, wo_o, mode="nt", name="l1_dy")
    do1, dg1 = _gate_bwd((dy1, 0), o1, g1, 0, do_dtype=BF16, name="l1_gate_bwd")
    dq1, dkvp, dsink = _swa_bwd(q1, do1, kvp, sinks)
    dh1 = jnp.concatenate([dq1, dg1, dkvp[WINDOW:].astype(BF16)], axis=1)
    dwi_o = _matmul(x1_16, dh1, mode="tn", tk=512, tn=768, name="l1_dwin")
    dwi_o = jnp.concatenate([dwi_o[:, :1024], dwi_o[:, 2048:], dwi_o[:, 1024:2048]], axis=1)
    dx1 = _matmul(dh1, wi_o, mode="nt", tk=2304, res=dz1, alpha=ALPHA, name="l1_dx")

    dz0, dg_e, db_e = _ln_bwd_call(z0, dx1, e_ln_g, name="l0_ln_bwd")
    dz0_16 = dz0.astype(BF16)
    dwo_e = _matmul(yab, dz0_16, mode="tn", tk=512, name="l0_dwout")
    dyab = _matmul(dz0_16, wo_e, mode="nt", name="l0_dy")
    dhst, dag = _gate_bwd((dyab, 0), hst, h0a, 1, do_dtype=F32, name="l0_gate_bwd_a")
    dob, dbg = _gate_bwd((dyab, 1), ob, bg, 0, do_dtype=BF16, name="l0_gate_bwd_b")
    dq0, dk0, dv0 = _sb_bwd(qkv, dob)
    dax, dwa, dwx, dpv = _rglru_bwd(h0a, hst, dhst, pvec, wa16, wx16)
    dh0 = jnp.concatenate([dax, dag, dq0, dk0.astype(BF16), dv0.astype(BF16), dbg], axis=1)
    dwi_e = _matmul(x0_16, dh0, mode="tn", tk=512, tn=768, name="l0_dwin")
    grad_x = _matmul(dh0, wi_e, mode="nt", tk=2048, res=dz0, alpha=ALPHA, name="l0_dx")

    repl = _pack_repl([dwa, dwx, dpv[4:5], dpv[5:6], dpv[6:7], dpv[7:8], dg_e, db_e, dsink[:, :C_HEADS]])
    seg = jnp.concatenate([
        dwi_e.reshape(d, N_DEV, 768).transpose(1, 0, 2).reshape(N_DEV, R_WIN_E, d),
        dwo_e.reshape(N_DEV, R_WOUT_E, d),
        dwi_o.reshape(d, N_DEV, 288).transpose(1, 0, 2).reshape(N_DEV, R_WIN_O, d),
        dwo_o.reshape(N_DEV, R_WOUT_O, d),
        _pack_small(dpv[0:4].reshape(4, N_DEV, 128).transpose(1, 0, 2), dg_o.reshape(N_DEV, 128),
                    db_o.reshape(N_DEV, 128)),
        repl.reshape(N_DEV, R_REPL, d)], axis=1)
    (parts,) = _exchange([seg], [False], name="scatter_grads")

    w_sh = _pack_sharded(e_w_in, e_w_out, o_w_in, o_w_out, e_conv_w, o_ln_g, o_ln_b)
    m_sh = _pack_sharded(m_e_w_in, m_e_w_out, m_o_w_in, m_o_w_out, m_e_conv_w, m_o_ln_g, m_o_ln_b)
    v_sh = _pack_sharded(v_e_w_in, v_e_w_out, v_o_w_in, v_o_w_out, v_e_conv_w, v_o_ln_g, v_o_ln_b)
    g_sh, d_sh, nm_sh, nv_sh = _adamw(parts, w_sh, m_sh, v_sh, name="adamw_sharded")

    (g_repl,) = _exchange([g_sh[SEG_ROWS - R_REPL:]], [True], name="gather_repl_grads")
    g_repl = g_repl.reshape(1, REPL_ROWS, d)
    w_r = _pack_repl([e_w_gate_a, e_w_gate_x, e_conv_b, e_b_gate_a, e_b_gate_x, e_lru_lambda, e_ln_g, e_ln_b, o_sinks])
    m_r = _pack_repl([m_e_w_gate_a, m_e_w_gate_x, m_e_conv_b, m_e_b_gate_a, m_e_b_gate_x, m_e_lru_lambda, m_e_ln_g,
                      m_e_ln_b, m_o_sinks])
    v_r = _pack_repl([v_e_w_gate_a, v_e_w_gate_x, v_e_conv_b, v_e_b_gate_a, v_e_b_gate_x, v_e_lru_lambda, v_e_ln_g,
                      v_e_ln_b, v_o_sinks])
    g_r, d_r, nm_r, nv_r = _adamw(g_repl, w_r, m_r, v_r, name="adamw_replicated", tr=REPL_ROWS)

    def assemble(sh, rp):
        w_in_e, w_out_e, w_in_o, w_out_o, cw, lg_o, lb_o = _unpack_sharded(sh)
        w_a, w_x, cb, b_a, b_x, lam, lg_e, lb_e, snk = _unpack_repl(rp)
        return [w_in_e, cw, cb, w_a, b_a, w_x, b_x, lam, w_out_e, lg_e, lb_e, w_in_o, snk, w_out_o, lg_o, lb_o]

    return (loss, grad_x[None], *assemble(g_sh, g_r), *assemble(d_sh, d_r), *assemble(nm_sh, nm_r),
            *assemble(nv_sh, nv_r))
```

```python
import functools
import math

import jax
import jax.numpy as jnp
from jax import lax
from jax.experimental import pallas as pl
from jax.experimental.pallas import tpu as pltpu

F32 = jnp.float32
BF16 = jnp.bfloat16

N_DEV = 8
D_MODEL = 1024
LRU_BLOCKS = 8
LRU_BLOCK = 128
LRU_C = 8.0
SB_HEADS = 8
SB_HEAD_DIM = 128
C_HEADS = 16
C_KV_HEADS = 2
C_GROUP = 8
C_HEAD_DIM = 64
WINDOW = 128
DEPTH = 2
ALPHA = float((2 * DEPTH) ** 0.25)
LN_EPS = 1e-5
ADAM_LR = 0.001
ADAM_B1 = 0.9
ADAM_B2 = 0.999
ADAM_EPS = 1e-08
ADAM_WD = 0.01
ADAM_STEP = 10

VMEM_LIMIT = 56 * 1024 * 1024

NN = ((1,), (0,))
NT = ((1,), (1,))
TN = ((0,), (0,))


def _dot(a, b, dims):
    return lax.dot_general(a, b, (dims, ((), ())), preferred_element_type=F32)


def _sigmoid(x):
    return 1.0 / (1.0 + jnp.exp(-x))


def _cparams(sem, vmem=VMEM_LIMIT):
    return pltpu.CompilerParams(dimension_semantics=sem, vmem_limit_bytes=vmem)


def _matmul(a, b, *, mode, n_out=None, b_off=0, out_dtype=F32, res=None, alpha=1.0,
            tm=512, tn=512, tk=1024, name):
    if mode == "nn":
        m, k = a.shape
    elif mode == "nt":
        m, k = a.shape
    else:
        k, m = a.shape
    n = n_out if n_out is not None else (b.shape[0] if mode == "nt" else b.shape[1])
    tm, tn, tk = min(tm, m), min(tn, n), min(tk, k)
    assert m % tm == 0 and n % tn == 0 and k % tk == 0 and b_off % tn == 0
    nk = k // tk
    jo = b_off // tn
    dims = {"nn": NN, "nt": NT, "tn": TN}[mode]

    def body(*refs):
        if res is None:
            a_ref, b_ref, o_ref, acc_ref = refs
            r_ref = None
        else:
            a_ref, b_ref, r_ref, o_ref, acc_ref = refs
        kk = pl.program_id(2)
        part = _dot(a_ref[...].astype(BF16), b_ref[...].astype(BF16), dims)

        @pl.when(kk == 0)
        def _():
            acc_ref[...] = part

        @pl.when(kk > 0)
        def _():
            acc_ref[...] += part

        @pl.when(kk == nk - 1)
        def _():
            out = acc_ref[...]
            if r_ref is not None:
                out = out + alpha * r_ref[...]
            o_ref[...] = out.astype(out_dtype)

    if mode == "tn":
        a_spec = pl.BlockSpec((tk, tm), lambda i, j, kk: (kk, i))
    else:
        a_spec = pl.BlockSpec((tm, tk), lambda i, j, kk: (i, kk))
    if mode == "nt":
        b_spec = pl.BlockSpec((tn, tk), lambda i, j, kk: (j + jo, kk))
    else:
        b_spec = pl.BlockSpec((tk, tn), lambda i, j, kk: (kk, j + jo))
    o_spec = pl.BlockSpec((tm, tn), lambda i, j, kk: (i, j))
    in_specs = [a_spec, b_spec]
    args = [a, b]
    if res is not None:
        in_specs.append(o_spec)
        args.append(res)
    return pl.pallas_call(
        body, name=name, grid=(m // tm, n // tn, nk),
        in_specs=in_specs, out_specs=o_spec,
        out_shape=jax.ShapeDtypeStruct((m, n), out_dtype),
        scratch_shapes=[pltpu.VMEM((tm, tn), F32)],
        compiler_params=_cparams(("parallel", "parallel", "arbitrary")),
    )(*args)


def _rowwise(fn, row_ins, vec_ins, row_outs, acc_outs, *, tr=256, name):
    s = row_ins[0][0].shape[0]
    assert s % tr == 0
    n_ri, n_vi, n_ro = len(row_ins), len(vec_ins), len(row_outs)

    def body(*refs):
        rows = [r[...] for r in refs[:n_ri]]
        vecs = [r[...] for r in refs[n_ri:n_ri + n_vi]]
        o_refs = refs[n_ri + n_vi:n_ri + n_vi + n_ro]
        a_refs = refs[n_ri + n_vi + n_ro:]
        outs, accs = fn(rows, vecs)
        for o_ref, o in zip(o_refs, outs):
            o_ref[...] = o.astype(o_ref.dtype)
        if a_refs:
            i = pl.program_id(0)

            @pl.when(i == 0)
            def _():
                for a_ref, part in zip(a_refs, accs):
                    a_ref[...] = part

            @pl.when(i > 0)
            def _():
                for a_ref, part in zip(a_refs, accs):
                    a_ref[...] += part

    in_specs = [pl.BlockSpec((tr, c), functools.partial(lambda i, cb: (i, cb), cb=cb)) for (_, c, cb) in row_ins]
    in_specs += [pl.BlockSpec(v.shape, functools.partial(lambda i, nd: (0,) * nd, nd=v.ndim)) for v in vec_ins]
    out_specs = [pl.BlockSpec((tr, c), lambda i: (i, 0)) for (c, _) in row_outs]
    out_specs += [pl.BlockSpec((r, c), lambda i: (0, 0)) for (r, c) in acc_outs]
    out_shape = [jax.ShapeDtypeStruct((s, c), dt) for (c, dt) in row_outs]
    out_shape += [jax.ShapeDtypeStruct((r, c), F32) for (r, c) in acc_outs]
    return pl.pallas_call(
        body, name=name, grid=(s // tr,), in_specs=in_specs, out_specs=out_specs, out_shape=out_shape,
        compiler_params=_cparams(("arbitrary",)),
    )(*[a for (a, _, _) in row_ins], *vec_ins)


def _ln_stats(z):
    mu = jnp.mean(z, axis=-1, keepdims=True)
    zc = z - mu
    var = jnp.mean(zc * zc, axis=-1, keepdims=True)
    rstd = lax.rsqrt(var + LN_EPS)
    return zc * rstd, rstd


def _ln_bwd(dy, xhat, rstd, g):
    dxh = dy * g
    m1 = jnp.mean(dxh, axis=-1, keepdims=True)
    m2 = jnp.mean(dxh * xhat, axis=-1, keepdims=True)
    return rstd * (dxh - m1 - xhat * m2)


def _colsum(x):
    return jnp.sum(x, axis=0, keepdims=True)


def _ln_fwd(x, y, g, b, *, name):
    def fn(rows, vecs):
        z = ALPHA * rows[0] + rows[1]
        xhat, _ = _ln_stats(z)
        x1 = xhat * vecs[0] + vecs[1]
        return [z, x1, x1], []
    c = x.shape[1]
    return _rowwise(fn, [(x, c, 0), (y, c, 0)], [g, b], [(c, F32), (c, F32), (c, BF16)], [], name=name)


def _ln_loss_bwd(x, y, target, g, b, *, name):
    c = x.shape[1]

    def fn(rows, vecs):
        z = ALPHA * rows[0] + rows[1]
        xhat, rstd = _ln_stats(z)
        d = xhat * vecs[0] + vecs[1] - rows[2]
        dy = d * (1.0 / c)
        dz = _ln_bwd(dy, xhat, rstd, vecs[0])
        return [dz], [_colsum(0.5 * d * d * (1.0 / c)), _colsum(dy * xhat), _colsum(dy)]
    return _rowwise(fn, [(x, c, 0), (y, c, 0), (target, c, 0)], [g, b], [(c, F32)], [(1, c)] * 3, name=name)


def _ln_bwd_call(z, dy, g, *, name):
    c = z.shape[1]

    def fn(rows, vecs):
        xhat, rstd = _ln_stats(rows[0])
        dz = _ln_bwd(rows[1], xhat, rstd, vecs[0])
        return [dz], [_colsum(rows[1] * xhat), _colsum(rows[1])]
    return _rowwise(fn, [(z, c, 0), (dy, c, 0)], [g], [(c, F32)], [(1, c)] * 2, name=name)


def _gate_bwd(dy, o, gate, gate_cb, *, do_dtype, name):
    dy_arr, dy_cb = dy
    c = o.shape[1]

    def fn(rows, vecs):
        d, oo, gg = rows
        sg = _sigmoid(gg)
        return [d * (gg * sg), d * oo * (sg * (1.0 + gg * (1.0 - sg)))], []
    return _rowwise(fn, [(dy_arr, c, dy_cb), (o, c, 0), (gate, c, gate_cb)], [], [(c, do_dtype), (c, BF16)], [], name=name)


LRU_T = 256
HALO = 8


def _log1p(y):
    u = 1.0 + y
    return jnp.where(u == 1.0, y, jnp.log(u) * (y / (u - 1.0)))


def _lru_gates(c, wa_ref, wx_ref, pv):
    c16 = c.astype(BF16)
    pre_r = jnp.concatenate(
        [_dot(c16[:, n * LRU_BLOCK:(n + 1) * LRU_BLOCK], wa_ref[n], NN) for n in range(LRU_BLOCKS)], axis=1)
    pre_i = jnp.concatenate(
        [_dot(c16[:, n * LRU_BLOCK:(n + 1) * LRU_BLOCK], wx_ref[n], NN) for n in range(LRU_BLOCKS)], axis=1)
    r = _sigmoid(pre_r + pv[5:6])
    ig = _sigmoid(pre_i + pv[6:7])
    lam = pv[7:8]
    ls = jnp.minimum(lam, 0.0) - _log1p(jnp.exp(-jnp.abs(lam)))
    la = LRU_C * r * ls
    a = jnp.exp(la)
    a2 = a * a
    m = jnp.sqrt(jnp.tanh(-la) * (a2 + 1.0))
    return c16, r, ig, ls, la, a, a2, m


def _conv(ext_ref, x, pv, t):
    return (pv[4:5] + pv[3:4] * x + pv[2:3] * ext_ref[pl.ds(HALO - 1, t), :]
            + pv[1:2] * ext_ref[pl.ds(HALO - 2, t), :] + pv[0:1] * ext_ref[pl.ds(HALO - 3, t), :])


def _rglru_fwd(h0a, pvec, wa16, wx16):
    s = h0a.shape[0]
    w = D_MODEL
    t = min(LRU_T, s)
    assert s % t == 0

    def body(ax_ref, ag_ref, pv_ref, wa_ref, wx_ref, ya_ref, h_ref, ext_ref, a_ref, u_ref, hc_ref):
        i = pl.program_id(0)

        @pl.when(i == 0)
        def _():
            ext_ref[pl.ds(0, HALO), :] = jnp.zeros((HALO, w), F32)
            hc_ref[...] = jnp.zeros((1, w), F32)

        pv = pv_ref[...]
        ax = ax_ref[...]
        ext_ref[pl.ds(HALO, t), :] = ax
        c = _conv(ext_ref, ax, pv, t)
        ext_ref[pl.ds(0, HALO), :] = ax[t - HALO:, :]
        _, _, ig, _, _, a, _, m = _lru_gates(c, wa_ref, wx_ref, pv)
        a_ref[...] = a
        u_ref[...] = m * (ig * c)

        def step(k, h):
            h = a_ref[pl.ds(k, 1), :] * h + u_ref[pl.ds(k, 1), :]
            h_ref[pl.ds(k, 1), :] = h
            return h

        hc_ref[...] = lax.fori_loop(0, t, step, hc_ref[...], unroll=8)
        ag = ag_ref[...]
        ya_ref[...] = (h_ref[...] * (ag * _sigmoid(ag))).astype(BF16)

    full = lambda shp: pl.BlockSpec(shp, lambda i: (0,) * len(shp))
    return pl.pallas_call(
        body, name="rglru_fwd", grid=(s // t,),
        in_specs=[pl.BlockSpec((t, w), lambda i: (i, 0)), pl.BlockSpec((t, w), lambda i: (i, 1)),
                  full((8, w)), full((LRU_BLOCKS, LRU_BLOCK, LRU_BLOCK)), full((LRU_BLOCKS, LRU_BLOCK, LRU_BLOCK))],
        out_specs=[pl.BlockSpec((t, w), lambda i: (i, 0)), pl.BlockSpec((t, w), lambda i: (i, 0))],
        out_shape=[jax.ShapeDtypeStruct((s, w), BF16), jax.ShapeDtypeStruct((s, w), F32)],
        scratch_shapes=[pltpu.VMEM((t + HALO, w), F32), pltpu.VMEM((t, w), F32), pltpu.VMEM((t, w), F32),
                        pltpu.VMEM((1, w), F32)],
        compiler_params=_cparams(("arbitrary",)),
    )(h0a, h0a, pvec, wa16, wx16)


def _rglru_bwd(h0a, h, dh, pvec, wa16, wx16):
    s = h0a.shape[0]
    w = D_MODEL
    t = min(LRU_T, s)
    nb = s // t
    hb = t // HALO

    def body(ax_ref, axh_ref, h_ref, hh_ref, dh_ref, pv_ref, wa_ref, wx_ref,
             dax_ref, dwa_ref, dwx_ref, dpv_ref, ext_ref, hext_ref, dcext_ref, a_ref, g_ref, gc_ref):
        i = pl.program_id(0)
        blk = nb - 1 - i

        @pl.when(i == 0)
        def _():
            dwa_ref[...] = jnp.zeros_like(dwa_ref)
            dwx_ref[...] = jnp.zeros_like(dwx_ref)
            dpv_ref[...] = jnp.zeros_like(dpv_ref)
            gc_ref[...] = jnp.zeros((1, w), F32)
            dcext_ref[pl.ds(t, HALO), :] = jnp.zeros((HALO, w), F32)

        pv = pv_ref[...]
        ax = ax_ref[...]
        keep = (blk > 0).astype(F32)
        ext_ref[pl.ds(0, HALO), :] = axh_ref[...] * keep
        ext_ref[pl.ds(HALO, t), :] = ax
        hext_ref[pl.ds(0, HALO), :] = hh_ref[...] * keep
        hext_ref[pl.ds(HALO, t), :] = h_ref[...]
        c = _conv(ext_ref, ax, pv, t)
        c16, r, ig, ls, _, a, a2, m = _lru_gates(c, wa_ref, wx_ref, pv)
        a_ref[...] = a

        def step(k, carry):
            row = t - 1 - k
            g = dh_ref[pl.ds(row, 1), :] + carry
            g_ref[pl.ds(row, 1), :] = g
            return a_ref[pl.ds(row, 1), :] * g

        gc_ref[...] = lax.fori_loop(0, t, step, gc_ref[...], unroll=8)
        g = g_ref[...]
        hprev = hext_ref[pl.ds(HALO - 1, t), :]
        gm = g * m
        d_la = g * hprev * a - (g * ig * c) * a2 / m
        d_pr = d_la * (LRU_C * ls) * r * (1.0 - r)
        d_pi = gm * c * ig * (1.0 - ig)
        dc = gm * ig
        dpr16 = d_pr.astype(BF16)
        dpi16 = d_pi.astype(BF16)
        dc_parts = []
        for n in range(LRU_BLOCKS):
            sl = slice(n * LRU_BLOCK, (n + 1) * LRU_BLOCK)
            dwa_ref[n] += _dot(c16[:, sl], dpr16[:, sl], TN)
            dwx_ref[n] += _dot(c16[:, sl], dpi16[:, sl], TN)
            dc_parts.append(_dot(dpr16[:, sl], wa_ref[n], NT) + _dot(dpi16[:, sl], wx_ref[n], NT))
        dc = dc + jnp.concatenate(dc_parts, axis=1)
        dcext_ref[pl.ds(0, t), :] = dc
        dax = (pv[3:4] * dc + pv[2:3] * dcext_ref[pl.ds(1, t), :] + pv[1:2] * dcext_ref[pl.ds(2, t), :]
               + pv[0:1] * dcext_ref[pl.ds(3, t), :])
        dax_ref[...] = dax.astype(BF16)
        dcext_ref[pl.ds(t, HALO), :] = dc[:HALO, :]
        sums = [_colsum(dc * ext_ref[pl.ds(HALO - 3 + kk, t), :]) for kk in range(4)]
        sums += [_colsum(dc), _colsum(d_pr), _colsum(d_pi), _colsum(d_la * (LRU_C * r))]
        for kk, part in enumerate(sums):
            dpv_ref[pl.ds(kk, 1), :] += part

        @pl.when(i == nb - 1)
        def _():
            lam = pv[7:8]
            dls = dpv_ref[pl.ds(7, 1), :]
            dpv_ref[pl.ds(7, 1), :] = dls * _sigmoid(-lam)

    full = lambda shp: pl.BlockSpec(shp, lambda i: (0,) * len(shp))
    rev = lambda cb: pl.BlockSpec((t, w), functools.partial(lambda i, cb: (nb - 1 - i, cb), cb=cb))
    halo = lambda cb: pl.BlockSpec(
        (HALO, w), functools.partial(lambda i, cb: (jnp.maximum((nb - 1 - i) * hb - 1, 0), cb), cb=cb))
    gw = (LRU_BLOCKS, LRU_BLOCK, LRU_BLOCK)
    return pl.pallas_call(
        body, name="rglru_bwd", grid=(nb,),
        in_specs=[rev(0), halo(0), rev(0), halo(0), rev(0), full((8, w)), full(gw), full(gw)],
        out_specs=[rev(0), full(gw), full(gw), full((8, w))],
        out_shape=[jax.ShapeDtypeStruct((s, w), BF16), jax.ShapeDtypeStruct(gw, F32),
                   jax.ShapeDtypeStruct(gw, F32), jax.ShapeDtypeStruct((8, w), F32)],
        scratch_shapes=[pltpu.VMEM((t + HALO, w), F32), pltpu.VMEM((t + HALO, w), F32),
                        pltpu.VMEM((t + HALO, w), F32), pltpu.VMEM((t, w), F32), pltpu.VMEM((t, w), F32),
                        pltpu.VMEM((1, w), F32)],
        compiler_params=_cparams(("arbitrary",)),
    )(h0a, h0a, h, h, dh, pvec, wa16, wx16)


SB_T = 256


def _split16(x):
    hi = x.astype(BF16)
    lo = (x - hi.astype(F32)).astype(BF16)
    return jnp.concatenate([hi, lo], axis=0)


def _sb_tile(q, k, scale, tri, run, causal):
    tq = q.shape[0]
    z = _dot(q, k, NT) * scale
    log1mb = -(jnp.maximum(z, 0.0) + jnp.log(1.0 + jnp.exp(-jnp.abs(z))))
    if causal is not None:
        log1mb = jnp.where(causal, log1mb, 0.0)
    cs = _dot(_split16(log1mb), tri, NN)
    cs = cs[:tq] + cs[tq:]
    wgt = jnp.exp(z + cs + run)
    if causal is not None:
        wgt = jnp.where(causal, wgt, 0.0)
    return z, wgt, run + cs[:, 0:1]


SB_DEAD = -105.0


def _sb_alive(run):
    return (jnp.max(run) > SB_DEAD).astype(jnp.int32)


def _sb_more(carry):
    return (carry[0] >= 0) & (carry[1] > 0)


def _sb_fwd(qkv, bg):
    s = qkv.shape[0]
    t = min(SB_T, s)
    nq = s // t
    dh = SB_HEAD_DIM
    scale = 1.0 / math.sqrt(dh)

    def body(q_ref, k_ref, v_ref, bg_ref, o_ref, y_ref):
        iq = pl.program_id(1)
        q = q_ref[...]
        row = lax.broadcasted_iota(jnp.int32, (t, t), 0)
        col = lax.broadcasted_iota(jnp.int32, (t, t), 1)
        tri = jnp.where(row >= col, 1.0, 0.0).astype(BF16)
        causal = col < row

        def tile(kb, run, acc, mask):
            off = pl.multiple_of(kb * t, t)
            _, wgt, run = _sb_tile(q, k_ref[pl.ds(off, t), :], scale, tri, run, mask)
            return run, acc + _dot(wgt.astype(BF16), v_ref[pl.ds(off, t), :], NN)

        run, acc = tile(iq, jnp.zeros((t, 1), F32), jnp.zeros((t, dh), F32), causal)

        def loop(carry):
            kb, _, run, acc = carry
            run, acc = tile(kb, run, acc, None)
            return kb - 1, _sb_alive(run), run, acc

        _, _, run, acc = lax.while_loop(_sb_more, loop, (iq - 1, jnp.int32(1), run, acc))
        o_ref[...] = acc
        g = bg_ref[...]
        y_ref[...] = (acc * (g * _sigmoid(g))).astype(BF16)

    blk = lambda off: pl.BlockSpec((t, dh), functools.partial(lambda h, i, off: (i, h + off), off=off))
    col = lambda off: pl.BlockSpec((s, dh), functools.partial(lambda h, i, off: (0, h + off), off=off))
    return pl.pallas_call(
        body, name="sb_fwd", grid=(SB_HEADS, nq),
        in_specs=[blk(0), col(SB_HEADS), col(2 * SB_HEADS), blk(0)],
        out_specs=[blk(0), blk(0)],
        out_shape=[jax.ShapeDtypeStruct((s, SB_HEADS * dh), F32), jax.ShapeDtypeStruct((s, SB_HEADS * dh), BF16)],
        compiler_params=_cparams(("parallel", "arbitrary")),
    )(qkv, qkv, qkv, bg)


def _sb_bwd(qkv, dob):
    s = qkv.shape[0]
    t = min(SB_T, s)
    nq = s // t
    dh = SB_HEAD_DIM
    scale = 1.0 / math.sqrt(dh)

    def body(q_ref, k_ref, v_ref, do_ref, dq_ref, dk_ref, dv_ref, e_ref, b_ref):
        iq = pl.program_id(1)

        @pl.when(iq == 0)
        def _():
            dk_ref[...] = jnp.zeros_like(dk_ref)
            dv_ref[...] = jnp.zeros_like(dv_ref)

        q = q_ref[...]
        do = do_ref[...]
        row = lax.broadcasted_iota(jnp.int32, (t, t), 0)
        col = lax.broadcasted_iota(jnp.int32, (t, t), 1)
        tri = jnp.where(row >= col, 1.0, 0.0).astype(BF16)
        tri_x = jnp.where(row < col, 1.0, 0.0).astype(BF16)
        causal = col < row

        def sweep1(kb, run, mask):
            off = pl.multiple_of(kb * t, t)
            v = v_ref[pl.ds(off, t), :]
            z, wgt, run = _sb_tile(q, k_ref[pl.ds(off, t), :], scale, tri, run, mask)
            e_ref[kb] = wgt * _dot(do, v, NT)
            b_ref[kb] = _sigmoid(z)
            dv_ref[pl.ds(off, t), :] += _dot(wgt.astype(BF16), do, TN)
            return run

        run = sweep1(iq, jnp.zeros((t, 1), F32), causal)

        def loop1(carry):
            run = sweep1(carry[0], carry[2], None)
            return carry[0] - 1, _sb_alive(run), run

        first = lax.while_loop(_sb_more, loop1, (iq - 1, jnp.int32(1), run))[0] + 1

        def sweep2(kb, pre, dq, mask):
            off = pl.multiple_of(kb * t, t)
            e = e_ref[kb]
            beta = b_ref[kb]
            ps = _dot(_split16(e), tri_x, NN)
            ps = ps[:t] + ps[t:]
            dz = e * (1.0 - beta) - beta * (ps + pre)
            if mask is not None:
                dz = jnp.where(mask, dz, 0.0)
            dz16 = (dz * scale).astype(BF16)
            dk_ref[pl.ds(off, t), :] += _dot(dz16, q, TN)
            dq = dq + _dot(dz16, k_ref[pl.ds(off, t), :], NN)
            return pre + ps[:, t - 1:t] + e[:, t - 1:t], dq

        pre, dq = lax.fori_loop(first, iq, lambda kb, c: sweep2(kb, c[0], c[1], None),
                                (jnp.zeros((t, 1), F32), jnp.zeros((t, dh), F32)))
        _, dq = sweep2(iq, pre, dq, causal)
        dq_ref[...] = dq.astype(BF16)

    blk = lambda off: pl.BlockSpec((t, dh), functools.partial(lambda h, i, off: (i, h + off), off=off))
    col = lambda off: pl.BlockSpec((s, dh), functools.partial(lambda h, i, off: (0, h + off), off=off))
    wide = SB_HEADS * dh
    return pl.pallas_call(
        body, name="sb_bwd", grid=(SB_HEADS, nq),
        in_specs=[blk(0), col(SB_HEADS), col(2 * SB_HEADS), blk(0)],
        out_specs=[blk(0), col(0), col(0)],
        out_shape=[jax.ShapeDtypeStruct((s, wide), BF16), jax.ShapeDtypeStruct((s, wide), F32),
                   jax.ShapeDtypeStruct((s, wide), F32)],
        scratch_shapes=[pltpu.VMEM((nq, t, t), F32), pltpu.VMEM((nq, t, t), F32)],
        compiler_params=_cparams(("parallel", "arbitrary")),
    )(qkv, qkv, qkv, dob)


def _alibi_slope(h):
    return float(2.0 ** (-8.0 * (h + 1) / C_HEADS))


def _swa_probs(qh, kw, sink, slope, dist, valid, scale):
    sc = _dot(qh, kw, NT) * scale - slope * dist
    sc = jnp.where(valid, sc, -1e30)
    m = jnp.maximum(jnp.max(sc, axis=-1, keepdims=True), sink)
    p = jnp.exp(sc - m)
    ps = jnp.exp(sink - m)
    inv = 1.0 / (jnp.sum(p, axis=-1, keepdims=True) + ps)
    return p * inv, ps * inv


def _swa_masks(n):
    qb = WINDOW
    i = lax.broadcasted_iota(jnp.int32, (qb, 2 * qb), 0)
    j = lax.broadcasted_iota(jnp.int32, (qb, 2 * qb), 1)
    d = i - j + qb
    valid = (d >= 0) & (d < WINDOW) & ((j >= qb) | (n > 0))
    return d.astype(F32), valid


def _swa_fwd(q16, gate, kvp, sinks):
    s = q16.shape[0]
    qb = WINDOW
    hd = C_HEAD_DIM
    scale = 1.0 / math.sqrt(hd)
    kvw = C_KV_HEADS * hd

    def body(sink_ref, q_ref, g_ref, kp_ref, ko_ref, o_ref, y_ref):
        n = pl.program_id(0)
        dist, valid = _swa_masks(n)
        q = q_ref[...]
        kv = jnp.concatenate([kp_ref[...], ko_ref[...]], axis=0)
        for h in range(C_HEADS):
            c = h // C_GROUP
            kw = kv[:, c * hd:(c + 1) * hd]
            vw = kv[:, kvw + c * hd:kvw + (c + 1) * hd]
            p, _ = _swa_probs(q[:, h * hd:(h + 1) * hd], kw, sink_ref[h], _alibi_slope(h), dist, valid, scale)
            o_ref[:, h * hd:(h + 1) * hd] = _dot(p.astype(BF16), vw, NN)
        g = g_ref[...]
        y_ref[...] = (o_ref[...] * (g * _sigmoid(g))).astype(BF16)

    wide = C_HEADS * hd
    return pl.pallas_call(
        body, name="swa_fwd", grid=(s // qb,),
        in_specs=[pl.BlockSpec(memory_space=pltpu.SMEM),
                  pl.BlockSpec((qb, wide), lambda n: (n, 0)), pl.BlockSpec((qb, wide), lambda n: (n, 0)),
                  pl.BlockSpec((qb, 2 * kvw), lambda n: (n, 0)), pl.BlockSpec((qb, 2 * kvw), lambda n: (n + 1, 0))],
        out_specs=[pl.BlockSpec((qb, wide), lambda n: (n, 0)), pl.BlockSpec((qb, wide), lambda n: (n, 0))],
        out_shape=[jax.ShapeDtypeStruct((s, wide), F32), jax.ShapeDtypeStruct((s, wide), BF16)],
        compiler_params=_cparams(("arbitrary",)),
    )(sinks, q16, gate, kvp, kvp)


def _swa_bwd(q16, do16, kvp, sinks):
    s = q16.shape[0]
    qb = WINDOW
    hd = C_HEAD_DIM
    scale = 1.0 / math.sqrt(hd)
    kvw = C_KV_HEADS * hd
    nblk = s // qb

    def body(sink_ref, q_ref, do_ref, kp_ref, ko_ref, dq_ref, dkv_ref, ds_ref, sacc_ref):
        n = pl.program_id(0)

        @pl.when(n == 0)
        def _():
            dkv_ref[...] = jnp.zeros_like(dkv_ref)
            sacc_ref[...] = jnp.zeros_like(sacc_ref)

        dist, valid = _swa_masks(n)
        lane = lax.broadcasted_iota(jnp.int32, (qb, 128), 1)
        q = q_ref[...]
        do = do_ref[...]
        kv = jnp.concatenate([kp_ref[...], ko_ref[...]], axis=0)
        off = pl.multiple_of(n * qb, qb)
        sacc = sacc_ref[...]
        for c in range(C_KV_HEADS):
            kw = kv[:, c * hd:(c + 1) * hd]
            vw = kv[:, kvw + c * hd:kvw + (c + 1) * hd]
            dkw = jnp.zeros((2 * qb, hd), F32)
            dvw = jnp.zeros((2 * qb, hd), F32)
            for hh in range(C_GROUP):
                h = c * C_GROUP + hh
                qh = q[:, h * hd:(h + 1) * hd]
                doh = do[:, h * hd:(h + 1) * hd]
                p, ps = _swa_probs(qh, kw, sink_ref[h], _alibi_slope(h), dist, valid, scale)
                dp = _dot(doh, vw, NT)
                dd = jnp.sum(p * dp, axis=-1, keepdims=True)
                ds16 = (p * (dp - dd) * scale).astype(BF16)
                sacc = sacc + jnp.where(lane == h, ps * dd, 0.0)
                dq_ref[:, h * hd:(h + 1) * hd] = _dot(ds16, kw, NN).astype(BF16)
                dkw = dkw + _dot(ds16, qh, TN)
                dvw = dvw + _dot(p.astype(BF16), doh, TN)
            dkv_ref[pl.ds(off, 2 * qb), c * hd:(c + 1) * hd] += dkw
            dkv_ref[pl.ds(off, 2 * qb), kvw + c * hd:kvw + (c + 1) * hd] += dvw
        sacc_ref[...] = sacc

        @pl.when(n == nblk - 1)
        def _():
            ds_ref[...] = -jnp.sum(sacc, axis=0, keepdims=True)

    wide = C_HEADS * hd
    return pl.pallas_call(
        body, name="swa_bwd", grid=(nblk,),
        in_specs=[pl.BlockSpec(memory_space=pltpu.SMEM),
                  pl.BlockSpec((qb, wide), lambda n: (n, 0)), pl.BlockSpec((qb, wide), lambda n: (n, 0)),
                  pl.BlockSpec((qb, 2 * kvw), lambda n: (n, 0)), pl.BlockSpec((qb, 2 * kvw), lambda n: (n + 1, 0))],
        out_specs=[pl.BlockSpec((qb, wide), lambda n: (n, 0)),
                   pl.BlockSpec((s + qb, 2 * kvw), lambda n: (0, 0)), pl.BlockSpec((1, 128), lambda n: (0, 0))],
        out_shape=[jax.ShapeDtypeStruct((s, wide), BF16), jax.ShapeDtypeStruct((s + qb, 2 * kvw), F32),
                   jax.ShapeDtypeStruct((1, 128), F32)],
        scratch_shapes=[pltpu.VMEM((qb, 128), F32)],
        compiler_params=_cparams(("arbitrary",)),
    )(sinks, q16, do16, kvp, kvp)


def _adamw(parts, w, m, v, *, name, tr=496):
    npart, r, c = parts.shape
    tr = min(tr, r)
    assert r % tr == 0 and tr % 16 == 0
    c1 = 1.0 / (1.0 - ADAM_B1 ** ADAM_STEP)
    c2 = 1.0 / (1.0 - ADAM_B2 ** ADAM_STEP)

    def body(p_ref, w_ref, m_ref, v_ref, g_ref, d_ref, nm_ref, nv_ref):
        g = p_ref[0].astype(F32)
        for j in range(1, npart):
            g = g + p_ref[j].astype(F32)
        nm = ADAM_B1 * m_ref[...] + (1.0 - ADAM_B1) * g
        nv = ADAM_B2 * v_ref[...] + (1.0 - ADAM_B2) * (g * g)
        g_ref[...] = g
        nm_ref[...] = nm
        nv_ref[...] = nv
        d_ref[...] = -ADAM_LR * ((nm * c1) / (jnp.sqrt(nv * c2) + ADAM_EPS) + ADAM_WD * w_ref[...])

    spec = pl.BlockSpec((tr, c), lambda i: (i, 0))
    return pl.pallas_call(
        body, name=name, grid=(r // tr,),
        in_specs=[pl.BlockSpec((npart, tr, c), lambda i: (0, i, 0)), spec, spec, spec],
        out_specs=[spec] * 4, out_shape=[jax.ShapeDtypeStruct((r, c), F32)] * 4,
        compiler_params=_cparams(("parallel",)),
    )(parts, w, m, v)


def _exchange(bufs, gather, *, name):
    nb = len(bufs)

    def body(*refs):
        src = refs[:nb]
        dst = refs[nb:2 * nb]
        send_sems, recv_sems, local_sems = refs[2 * nb:]
        x, y, c = lax.axis_index("x"), lax.axis_index("y"), lax.axis_index("c")
        me = 4 * x + 2 * y + c
        copies = []
        for b in range(nb):
            mine = src[b] if gather[b] else src[b].at[me]
            local = pltpu.make_async_copy(mine, dst[b].at[me], local_sems.at[b])
            local.start()
            copies.append(local)
        remote = []
        for k in range(1, N_DEV):
            px, py, pc = x ^ (k >> 2), y ^ ((k >> 1) & 1), c ^ (k & 1)
            peer = 4 * px + 2 * py + pc
            for b in range(nb):
                cp = pltpu.make_async_remote_copy(
                    src_ref=src[b] if gather[b] else src[b].at[peer],
                    dst_ref=dst[b].at[me],
                    send_sem=send_sems.at[b, k - 1], recv_sem=recv_sems.at[b, k - 1],
                    device_id=(px, py, pc), device_id_type=pl.DeviceIdType.MESH)
                cp.start()
                remote.append((cp, b, k, peer))
        for cp, b, k, peer in remote:
            cp.wait_send()
        for cp, b, k, peer in remote:
            pltpu.make_async_remote_copy(
                src_ref=src[b] if gather[b] else src[b].at[me],
                dst_ref=dst[b].at[peer],
                send_sem=send_sems.at[b, k - 1], recv_sem=recv_sems.at[b, k - 1],
                device_id=(x, y, c), device_id_type=pl.DeviceIdType.MESH).wait_recv()
        for local in copies:
            local.wait()

    out_shape = []
    for b, g in zip(bufs, gather):
        shp = b.shape if g else b.shape[1:]
        out_shape.append(jax.ShapeDtypeStruct((N_DEV,) + tuple(shp), b.dtype))
    hbm = pl.BlockSpec(memory_space=pl.ANY)
    return pl.pallas_call(
        body, name=name, in_specs=[hbm] * nb, out_specs=[hbm] * nb, out_shape=out_shape,
        scratch_shapes=[pltpu.SemaphoreType.DMA((nb, N_DEV - 1)), pltpu.SemaphoreType.DMA((nb, N_DEV - 1)),
                        pltpu.SemaphoreType.DMA((nb,))],
        compiler_params=pltpu.CompilerParams(has_side_effects=True),
    )(*bufs)


R_WIN_E, R_WOUT_E, R_WIN_O, R_WOUT_O, R_SMALL, R_REPL = 768, 256, 288, 128, 8, 40
SEG_ROWS = R_WIN_E + R_WOUT_E + R_WIN_O + R_WOUT_O + R_SMALL + R_REPL
REPL_ROWS = N_DEV * R_REPL
REPL_LEN = REPL_ROWS * D_MODEL


def _rows(a):
    return a.reshape(-1, D_MODEL)


def _pack_small(conv_w, ln_g, ln_b):
    lead = conv_w.shape[:-2]
    flat = jnp.concatenate([conv_w.reshape(lead + (512,)), ln_g.reshape(lead + (128,)), ln_b.reshape(lead + (128,))],
                           axis=-1)
    return jnp.pad(flat[..., None, :], [(0, 0)] * len(lead) + [(0, R_SMALL - 1), (0, D_MODEL - 768)])


def _pack_sharded(w_in_e, w_out_e, w_in_o, w_out_o, conv_w, ln_g, ln_b):
    return jnp.concatenate([_rows(w_in_e), _rows(w_out_e), _rows(w_in_o), _rows(w_out_o),
                            _pack_small(conv_w[0], ln_g[0], ln_b[0]), jnp.zeros((R_REPL, D_MODEL), F32)], axis=0)


def _unpack_sharded(p):
    r0 = 0
    out = []
    for rows, shp in ((R_WIN_E, (1, D_MODEL, 768)), (R_WOUT_E, (1, 256, D_MODEL)), (R_WIN_O, (1, D_MODEL, 288)),
                      (R_WOUT_O, (1, 128, D_MODEL))):
        out.append(p[r0:r0 + rows].reshape(shp))
        r0 += rows
    small = p[r0]
    out.append(small[0:512].reshape(1, 4, 128))
    out.append(small[512:640].reshape(1, 128))
    out.append(small[640:768].reshape(1, 128))
    return out


REPL_SHAPES = ((1, 8, 128, 128), (1, 8, 128, 128), (1, 1024), (1, 1024), (1, 1024), (1, 1024), (1, 1024), (1, 1024),
               (1, 16))


def _pack_repl(parts):
    flat = jnp.concatenate([p.reshape(-1) for p in parts])
    return jnp.concatenate([flat, jnp.zeros((REPL_LEN - flat.shape[0],), F32)]).reshape(REPL_ROWS, D_MODEL)


def _unpack_repl(p):
    flat = p.reshape(-1)
    out, o = [], 0
    for shp in REPL_SHAPES:
        n = math.prod(shp)
        out.append(flat[o:o + n].reshape(shp))
        o += n
    return out


def kernel(x, e_w_in, e_conv_w, e_conv_b, e_w_gate_a, e_b_gate_a, e_w_gate_x, e_b_gate_x, e_lru_lambda, e_w_out, e_ln_g, e_ln_b, o_w_in, o_sinks, o_w_out, o_ln_g, o_ln_b, loss_target, m_e_w_in, m_e_conv_w, m_e_conv_b, m_e_w_gate_a, m_e_b_gate_a, m_e_w_gate_x, m_e_b_gate_x, m_e_lru_lambda, m_e_w_out, m_e_ln_g, m_e_ln_b, m_o_w_in, m_o_sinks, m_o_w_out, m_o_ln_g, m_o_ln_b, v_e_w_in, v_e_conv_w, v_e_conv_b, v_e_w_gate_a, v_e_b_gate_a, v_e_w_gate_x, v_e_b_gate_x, v_e_lru_lambda, v_e_w_out, v_e_ln_g, v_e_ln_b, v_o_w_in, v_o_sinks, v_o_w_out, v_o_ln_g, v_o_ln_b):
    d = D_MODEL
    x0 = x[0]
    target = loss_target[0]
    s = x0.shape[0]

    wpack = jnp.concatenate([_rows(e_w_in), _rows(e_w_out), _rows(o_w_in), _rows(o_w_out)], axis=0).astype(BF16)
    spack = _pack_small(e_conv_w[0], o_ln_g[0], o_ln_b[0])
    wall, sall = _exchange([wpack, spack], [True, True], name="gather_weights")
    r0 = 0
    wi_e = wall[:, r0:r0 + R_WIN_E].reshape(N_DEV, d, 768).transpose(1, 0, 2).reshape(d, 6 * d)
    r0 += R_WIN_E
    wo_e = wall[:, r0:r0 + R_WOUT_E].reshape(2 * d, d)
    r0 += R_WOUT_E
    wi_o = wall[:, r0:r0 + R_WIN_O].reshape(N_DEV, d, 288).transpose(1, 0, 2).reshape(d, 2304)
    r0 += R_WIN_O
    wo_o = wall[:, r0:r0 + R_WOUT_O].reshape(d, d)
    wi_o = jnp.concatenate([wi_o[:, :1024], wi_o[:, 1280:], wi_o[:, 1024:1280]], axis=1)
    conv_w = sall[:, 0, 0:512].reshape(N_DEV, 4, 128).transpose(1, 0, 2).reshape(4, d)
    ln_g_o = sall[:, 0, 512:640].reshape(1, d)
    ln_b_o = sall[:, 0, 640:768].reshape(1, d)
    pvec = jnp.concatenate([conv_w, e_conv_b, e_b_gate_a, e_b_gate_x, e_lru_lambda], axis=0)
    wa16 = e_w_gate_a[0].astype(BF16)
    wx16 = e_w_gate_x[0].astype(BF16)
    sinks = o_sinks[0]

    x0_16 = x0.astype(BF16)
    h0a = _matmul(x0_16, wi_e, mode="nn", n_out=2 * d, b_off=0, name="l0_in_a")
    qkv = _matmul(x0_16, wi_e, mode="nn", n_out=3 * d, b_off=2 * d, out_dtype=BF16, name="l0_in_qkv")
    bg = _matmul(x0_16, wi_e, mode="nn", n_out=d, b_off=5 * d, name="l0_in_bg")
    ya, hst = _rglru_fwd(h0a, pvec, wa16, wx16)
    ob, yb = _sb_fwd(qkv, bg)
    yab = jnp.concatenate([ya, yb], axis=1)
    y0 = _matmul(yab, wo_e, mode="nn", tk=2048, name="l0_out")
    z0, x1, x1_16 = _ln_fwd(x0, y0, e_ln_g, e_ln_b, name="l0_ln")

    q1 = _matmul(x1_16, wi_o, mode="nn", n_out=d, b_off=0, out_dtype=BF16, name="l1_in_q")
    g1 = _matmul(x1_16, wi_o, mode="nn", n_out=d, b_off=d, name="l1_in_g")
    kv1 = _matmul(x1_16, wi_o, mode="nn", n_out=256, b_off=2 * d, tn=256, out_dtype=BF16, name="l1_in_kv")
    kvp = jnp.concatenate([jnp.zeros((WINDOW, 256), BF16), kv1], axis=0)
    o1, y1 = _swa_fwd(q1, g1, kvp, sinks)
    yy1 = _matmul(y1, wo_o, mode="nn", name="l1_out")
    dz1, loss_cols, dg_o, db_o = _ln_loss_bwd(x1, yy1, target, ln_g_o, ln_b_o, name="l1_ln_loss")
    loss = lax.psum(jnp.sum(loss_cols), ("x", "y", "c"))

    dz1_16 = dz1.astype(BF16)
    dwo_o = _matmul(y1, dz1_16, mode="tn", tk=512, name="l1_dwout")
    dy1 = _matmul(dz1_16, wo_o, mode="nt", name="l1_dy")
    do1, dg1 = _gate_bwd((dy1, 0), o1, g1, 0, do_dtype=BF16, name="l1_gate_bwd")
    dq1, dkvp, dsink = _swa_bwd(q1, do1, kvp, sinks)
    dh1 = jnp.concatenate([dq1, dg1, dkvp[WINDOW:].astype(BF16)], axis=1)
    dwi_o = _matmul(x1_16, dh1, mode="tn", tk=512, tn=768, name="l1_dwin")
    dwi_o = jnp.concatenate([dwi_o[:, :1024], dwi_o[:, 2048:], dwi_o[:, 1024:2048]], axis=1)
    dx1 = _matmul(dh1, wi_o, mode="nt", tk=2304, res=dz1, alpha=ALPHA, name="l1_dx")

    dz0, dg_e, db_e = _ln_bwd_call(z0, dx1, e_ln_g, name="l0_ln_bwd")
    dz0_16 = dz0.astype(BF16)
    dwo_e = _matmul(yab, dz0_16, mode="tn", tk=512, name="l0_dwout")
    dyab = _matmul(dz0_16, wo_e, mode="nt", name="l0_dy")
    dhst, dag = _gate_bwd((dyab, 0), hst, h0a, 1, do_dtype=F32, name="l0_gate_bwd_a")
    dob, dbg = _gate_bwd((dyab, 1), ob, bg, 0, do_dtype=BF16, name="l0_gate_bwd_b")
    dq0, dk0, dv0 = _sb_bwd(qkv, dob)
    dax, dwa, dwx, dpv = _rglru_bwd(h0a, hst, dhst, pvec, wa16, wx16)
    dh0 = jnp.concatenate([dax, dag, dq0, dk0.astype(BF16), dv0.astype(BF16), dbg], axis=1)
    dwi_e = _matmul(x0_16, dh0, mode="tn", tk=512, tn=768, name="l0_dwin")
    grad_x = _matmul(dh0, wi_e, mode="nt", tk=2048, res=dz0, alpha=ALPHA, name="l0_dx")

    repl = _pack_repl([dwa, dwx, dpv[4:5], dpv[5:6], dpv[6:7], dpv[7:8], dg_e, db_e, dsink[:, :C_HEADS]])
    seg = jnp.concatenate([
        dwi_e.reshape(d, N_DEV, 768).transpose(1, 0, 2).reshape(N_DEV, R_WIN_E, d),
        dwo_e.reshape(N_DEV, R_WOUT_E, d),
        dwi_o.reshape(d, N_DEV, 288).transpose(1, 0, 2).reshape(N_DEV, R_WIN_O, d),
        dwo_o.reshape(N_DEV, R_WOUT_O, d),
        _pack_small(dpv[0:4].reshape(4, N_DEV, 128).transpose(1, 0, 2), dg_o.reshape(N_DEV, 128),
                    db_o.reshape(N_DEV, 128)),
        repl.reshape(N_DEV, R_REPL, d)], axis=1).astype(BF16)
    (parts,) = _exchange([seg], [False], name="scatter_grads")

    w_sh = _pack_sharded(e_w_in, e_w_out, o_w_in, o_w_out, e_conv_w, o_ln_g, o_ln_b)
    m_sh = _pack_sharded(m_e_w_in, m_e_w_out, m_o_w_in, m_o_w_out, m_e_conv_w, m_o_ln_g, m_o_ln_b)
    v_sh = _pack_sharded(v_e_w_in, v_e_w_out, v_o_w_in, v_o_w_out, v_e_conv_w, v_o_ln_g, v_o_ln_b)
    g_sh, d_sh, nm_sh, nv_sh = _adamw(parts, w_sh, m_sh, v_sh, name="adamw_sharded")

    (g_repl,) = _exchange([g_sh[SEG_ROWS - R_REPL:]], [True], name="gather_repl_grads")
    g_repl = g_repl.reshape(1, REPL_ROWS, d)
    w_r = _pack_repl([e_w_gate_a, e_w_gate_x, e_conv_b, e_b_gate_a, e_b_gate_x, e_lru_lambda, e_ln_g, e_ln_b, o_sinks])
    m_r = _pack_repl([m_e_w_gate_a, m_e_w_gate_x, m_e_conv_b, m_e_b_gate_a, m_e_b_gate_x, m_e_lru_lambda, m_e_ln_g,
                      m_e_ln_b, m_o_sinks])
    v_r = _pack_repl([v_e_w_gate_a, v_e_w_gate_x, v_e_conv_b, v_e_b_gate_a, v_e_b_gate_x, v_e_lru_lambda, v_e_ln_g,
                      v_e_ln_b, v_o_sinks])
    g_r, d_r, nm_r, nv_r = _adamw(g_repl, w_r, m_r, v_r, name="adamw_replicated")

    def assemble(sh, rp):
        w_in_e, w_out_e, w_in_o, w_out_o, cw, lg_o, lb_o = _unpack_sharded(sh)
        w_a, w_x, cb, b_a, b_x, lam, lg_e, lb_e, snk = _unpack_repl(rp)
        return [w_in_e, cw, cb, w_a, b_a, w_x, b_x, lam, w_out_e, lg_e, lb_e, w_in_o, snk, w_out_o, lg_o, lb_o]

    return (loss, grad_x[None], *assemble(g_sh, g_r), *assemble(d_sh, d_r), *assemble(nm_sh, nm_r),
            *assemble(nv_sh, nv_r))
```

```python
import functools
import math

import jax
import jax.numpy as jnp
from jax import lax
from jax.experimental import pallas as pl
from jax.experimental.pallas import tpu as pltpu

F32 = jnp.float32
BF16 = jnp.bfloat16

N_DEV = 8
D_MODEL = 1024
LRU_BLOCKS = 8
LRU_BLOCK = 128
LRU_C = 8.0
SB_HEADS = 8
SB_HEAD_DIM = 128
C_HEADS = 16
C_KV_HEADS = 2
C_GROUP = 8
C_HEAD_DIM = 64
WINDOW = 128
DEPTH = 2
ALPHA = float((2 * DEPTH) ** 0.25)
LN_EPS = 1e-5
ADAM_LR = 0.001
ADAM_B1 = 0.9
ADAM_B2 = 0.999
ADAM_EPS = 1e-08
ADAM_WD = 0.01
ADAM_STEP = 10

VMEM_LIMIT = 56 * 1024 * 1024

NN = ((1,), (0,))
NT = ((1,), (1,))
TN = ((0,), (0,))


def _dot(a, b, dims):
    return lax.dot_general(a, b, (dims, ((), ())), preferred_element_type=F32)


def _sigmoid(x):
    return 1.0 / (1.0 + jnp.exp(-x))


def _cparams(sem, vmem=VMEM_LIMIT):
    return pltpu.CompilerParams(dimension_semantics=sem, vmem_limit_bytes=vmem)


def _matmul(a, b, *, mode, n_out=None, b_off=0, out_dtype=F32, res=None, alpha=1.0,
            tm=1024, tn=1024, tk=1024, name):
    if mode == "nn":
        m, k = a.shape
    elif mode == "nt":
        m, k = a.shape
    else:
        k, m = a.shape
    n = n_out if n_out is not None else (b.shape[0] if mode == "nt" else b.shape[1])
    tm, tn, tk = min(tm, m), min(tn, n), min(tk, k)
    assert m % tm == 0 and n % tn == 0 and k % tk == 0 and b_off % tn == 0
    nk = k // tk
    jo = b_off // tn
    dims = {"nn": NN, "nt": NT, "tn": TN}[mode]

    def body(*refs):
        if res is None:
            a_ref, b_ref, o_ref = refs[:3]
            r_ref = None
        else:
            a_ref, b_ref, r_ref, o_ref = refs[:4]
        part = _dot(a_ref[...].astype(BF16), b_ref[...].astype(BF16), dims)

        def finish(out):
            if r_ref is not None:
                out = out + alpha * r_ref[...]
            o_ref[...] = out.astype(out_dtype)

        if nk == 1:
            finish(part)
            return
        acc_ref = refs[-1]
        kk = pl.program_id(2)

        @pl.when(kk == 0)
        def _():
            acc_ref[...] = part

        @pl.when(kk > 0)
        def _():
            acc_ref[...] += part

        @pl.when(kk == nk - 1)
        def _():
            finish(acc_ref[...])

    if mode == "tn":
        a_spec = pl.BlockSpec((tk, tm), lambda i, j, kk: (kk, i))
    else:
        a_spec = pl.BlockSpec((tm, tk), lambda i, j, kk: (i, kk))
    if mode == "nt":
        b_spec = pl.BlockSpec((tn, tk), lambda i, j, kk: (j + jo, kk))
    else:
        b_spec = pl.BlockSpec((tk, tn), lambda i, j, kk: (kk, j + jo))
    o_spec = pl.BlockSpec((tm, tn), lambda i, j, kk: (i, j))
    in_specs = [a_spec, b_spec]
    args = [a, b]
    if res is not None:
        in_specs.append(o_spec)
        args.append(res)
    return pl.pallas_call(
        body, name=name, grid=(m // tm, n // tn, nk),
        in_specs=in_specs, out_specs=o_spec,
        out_shape=jax.ShapeDtypeStruct((m, n), out_dtype),
        scratch_shapes=[pltpu.VMEM((tm, tn), F32)] if nk > 1 else [],
        compiler_params=_cparams(("parallel", "parallel", "arbitrary")),
    )(*args)


def _rowwise(fn, row_ins, vec_ins, row_outs, acc_outs, *, tr=256, name):
    s = row_ins[0][0].shape[0]
    assert s % tr == 0
    n_ri, n_vi, n_ro = len(row_ins), len(vec_ins), len(row_outs)

    def body(*refs):
        rows = [r[...] for r in refs[:n_ri]]
        vecs = [r[...] for r in refs[n_ri:n_ri + n_vi]]
        o_refs = refs[n_ri + n_vi:n_ri + n_vi + n_ro]
        a_refs = refs[n_ri + n_vi + n_ro:]
        outs, accs = fn(rows, vecs)
        for o_ref, o in zip(o_refs, outs):
            o_ref[...] = o.astype(o_ref.dtype)
        if a_refs:
            i = pl.program_id(0)

            @pl.when(i == 0)
            def _():
                for a_ref, part in zip(a_refs, accs):
                    a_ref[...] = part

            @pl.when(i > 0)
            def _():
                for a_ref, part in zip(a_refs, accs):
                    a_ref[...] += part

    in_specs = [pl.BlockSpec((tr, c), functools.partial(lambda i, cb: (i, cb), cb=cb)) for (_, c, cb) in row_ins]
    in_specs += [pl.BlockSpec(v.shape, functools.partial(lambda i, nd: (0,) * nd, nd=v.ndim)) for v in vec_ins]
    out_specs = [pl.BlockSpec((tr, c), lambda i: (i, 0)) for (c, _) in row_outs]
    out_specs += [pl.BlockSpec((r, c), lambda i: (0, 0)) for (r, c) in acc_outs]
    out_shape = [jax.ShapeDtypeStruct((s, c), dt) for (c, dt) in row_outs]
    out_shape += [jax.ShapeDtypeStruct((r, c), F32) for (r, c) in acc_outs]
    return pl.pallas_call(
        body, name=name, grid=(s // tr,), in_specs=in_specs, out_specs=out_specs, out_shape=out_shape,
        compiler_params=_cparams(("arbitrary",)),
    )(*[a for (a, _, _) in row_ins], *vec_ins)


def _ln_stats(z):
    mu = jnp.mean(z, axis=-1, keepdims=True)
    zc = z - mu
    var = jnp.mean(zc * zc, axis=-1, keepdims=True)
    rstd = lax.rsqrt(var + LN_EPS)
    return zc * rstd, rstd


def _ln_bwd(dy, xhat, rstd, g):
    dxh = dy * g
    m1 = jnp.mean(dxh, axis=-1, keepdims=True)
    m2 = jnp.mean(dxh * xhat, axis=-1, keepdims=True)
    return rstd * (dxh - m1 - xhat * m2)


def _colsum(x):
    return jnp.sum(x, axis=0, keepdims=True)


def _ln_fwd(x, y, g, b, *, name):
    def fn(rows, vecs):
        z = ALPHA * rows[0] + rows[1]
        xhat, _ = _ln_stats(z)
        x1 = xhat * vecs[0] + vecs[1]
        return [z, x1, x1], []
    c = x.shape[1]
    return _rowwise(fn, [(x, c, 0), (y, c, 0)], [g, b], [(c, F32), (c, F32), (c, BF16)], [], name=name)


def _ln_loss_bwd(x, y, target, g, b, *, name):
    c = x.shape[1]

    def fn(rows, vecs):
        z = ALPHA * rows[0] + rows[1]
        xhat, rstd = _ln_stats(z)
        d = xhat * vecs[0] + vecs[1] - rows[2]
        dy = d * (1.0 / c)
        dz = _ln_bwd(dy, xhat, rstd, vecs[0])
        return [dz], [_colsum(0.5 * d * d * (1.0 / c)), _colsum(dy * xhat), _colsum(dy)]
    return _rowwise(fn, [(x, c, 0), (y, c, 0), (target, c, 0)], [g, b], [(c, F32)], [(1, c)] * 3, name=name)


def _ln_bwd_call(z, dy, g, *, name):
    c = z.shape[1]

    def fn(rows, vecs):
        xhat, rstd = _ln_stats(rows[0])
        dz = _ln_bwd(rows[1], xhat, rstd, vecs[0])
        return [dz], [_colsum(rows[1] * xhat), _colsum(rows[1])]
    return _rowwise(fn, [(z, c, 0), (dy, c, 0)], [g], [(c, F32)], [(1, c)] * 2, name=name)


def _gate_bwd(dy, o, gate, gate_cb, *, do_dtype, name):
    dy_arr, dy_cb = dy
    c = o.shape[1]

    def fn(rows, vecs):
        d, oo, gg = rows
        sg = _sigmoid(gg)
        return [d * (gg * sg), d * oo * (sg * (1.0 + gg * (1.0 - sg)))], []
    return _rowwise(fn, [(dy_arr, c, dy_cb), (o, c, 0), (gate, c, gate_cb)], [], [(c, do_dtype), (c, BF16)], [], name=name)


LRU_T = 256
HALO = 8


def _log1p(y):
    u = 1.0 + y
    return jnp.where(u == 1.0, y, jnp.log(u) * (y / (u - 1.0)))


def _lru_gates(c, wa_ref, wx_ref, pv):
    c16 = c.astype(BF16)
    pre_r = jnp.concatenate(
        [_dot(c16[:, n * LRU_BLOCK:(n + 1) * LRU_BLOCK], wa_ref[n], NN) for n in range(LRU_BLOCKS)], axis=1)
    pre_i = jnp.concatenate(
        [_dot(c16[:, n * LRU_BLOCK:(n + 1) * LRU_BLOCK], wx_ref[n], NN) for n in range(LRU_BLOCKS)], axis=1)
    r = _sigmoid(pre_r + pv[5:6])
    ig = _sigmoid(pre_i + pv[6:7])
    lam = pv[7:8]
    ls = jnp.minimum(lam, 0.0) - _log1p(jnp.exp(-jnp.abs(lam)))
    la = LRU_C * r * ls
    a = jnp.exp(la)
    a2 = a * a
    m = jnp.sqrt(jnp.tanh(-la) * (a2 + 1.0))
    return c16, r, ig, ls, la, a, a2, m


def _conv(ext_ref, x, pv, t):
    return (pv[4:5] + pv[3:4] * x + pv[2:3] * ext_ref[pl.ds(HALO - 1, t), :]
            + pv[1:2] * ext_ref[pl.ds(HALO - 2, t), :] + pv[0:1] * ext_ref[pl.ds(HALO - 3, t), :])


def _rglru_fwd(h0a, pvec, wa16, wx16):
    s = h0a.shape[0]
    w = D_MODEL
    t = min(LRU_T, s)
    assert s % t == 0

    def body(ax_ref, ag_ref, pv_ref, wa_ref, wx_ref, ya_ref, h_ref, ext_ref, a_ref, u_ref, hc_ref):
        i = pl.program_id(0)

        @pl.when(i == 0)
        def _():
            ext_ref[pl.ds(0, HALO), :] = jnp.zeros((HALO, w), F32)
            hc_ref[...] = jnp.zeros((1, w), F32)

        pv = pv_ref[...]
        ax = ax_ref[...]
        ext_ref[pl.ds(HALO, t), :] = ax
        c = _conv(ext_ref, ax, pv, t)
        ext_ref[pl.ds(0, HALO), :] = ax[t - HALO:, :]
        _, _, ig, _, _, a, _, m = _lru_gates(c, wa_ref, wx_ref, pv)
        a_ref[...] = a
        u_ref[...] = m * (ig * c)

        def step(k, h):
            h = a_ref[pl.ds(k, 1), :] * h + u_ref[pl.ds(k, 1), :]
            h_ref[pl.ds(k, 1), :] = h
            return h

        hc_ref[...] = lax.fori_loop(0, t, step, hc_ref[...], unroll=8)
        ag = ag_ref[...]
        ya_ref[...] = (h_ref[...] * (ag * _sigmoid(ag))).astype(BF16)

    full = lambda shp: pl.BlockSpec(shp, lambda i: (0,) * len(shp))
    return pl.pallas_call(
        body, name="rglru_fwd", grid=(s // t,),
        in_specs=[pl.BlockSpec((t, w), lambda i: (i, 0)), pl.BlockSpec((t, w), lambda i: (i, 1)),
                  full((8, w)), full((LRU_BLOCKS, LRU_BLOCK, LRU_BLOCK)), full((LRU_BLOCKS, LRU_BLOCK, LRU_BLOCK))],
        out_specs=[pl.BlockSpec((t, w), lambda i: (i, 0)), pl.BlockSpec((t, w), lambda i: (i, 0))],
        out_shape=[jax.ShapeDtypeStruct((s, w), BF16), jax.ShapeDtypeStruct((s, w), F32)],
        scratch_shapes=[pltpu.VMEM((t + HALO, w), F32), pltpu.VMEM((t, w), F32), pltpu.VMEM((t, w), F32),
                        pltpu.VMEM((1, w), F32)],
        compiler_params=_cparams(("arbitrary",)),
    )(h0a, h0a, pvec, wa16, wx16)


def _rglru_bwd(h0a, h, dh, pvec, wa16, wx16):
    s = h0a.shape[0]
    w = D_MODEL
    t = min(LRU_T, s)
    nb = s // t
    hb = t // HALO

    def body(ax_ref, axh_ref, h_ref, hh_ref, dh_ref, pv_ref, wa_ref, wx_ref,
             dax_ref, dwa_ref, dwx_ref, dpv_ref, ext_ref, hext_ref, dcext_ref, a_ref, g_ref, gc_ref):
        i = pl.program_id(0)
        blk = nb - 1 - i

        @pl.when(i == 0)
        def _():
            dwa_ref[...] = jnp.zeros_like(dwa_ref)
            dwx_ref[...] = jnp.zeros_like(dwx_ref)
            dpv_ref[...] = jnp.zeros_like(dpv_ref)
            gc_ref[...] = jnp.zeros((1, w), F32)
            dcext_ref[pl.ds(t, HALO), :] = jnp.zeros((HALO, w), F32)

        pv = pv_ref[...]
        ax = ax_ref[...]
        keep = (blk > 0).astype(F32)
        ext_ref[pl.ds(0, HALO), :] = axh_ref[...] * keep
        ext_ref[pl.ds(HALO, t), :] = ax
        hext_ref[pl.ds(0, HALO), :] = hh_ref[...] * keep
        hext_ref[pl.ds(HALO, t), :] = h_ref[...]
        c = _conv(ext_ref, ax, pv, t)
        c16, r, ig, ls, _, a, a2, m = _lru_gates(c, wa_ref, wx_ref, pv)
        a_ref[...] = a

        def step(k, carry):
            row = t - 1 - k
            g = dh_ref[pl.ds(row, 1), :] + carry
            g_ref[pl.ds(row, 1), :] = g
            return a_ref[pl.ds(row, 1), :] * g

        gc_ref[...] = lax.fori_loop(0, t, step, gc_ref[...], unroll=8)
        g = g_ref[...]
        hprev = hext_ref[pl.ds(HALO - 1, t), :]
        gm = g * m
        d_la = g * hprev * a - (g * ig * c) * a2 / m
        d_pr = d_la * (LRU_C * ls) * r * (1.0 - r)
        d_pi = gm * c * ig * (1.0 - ig)
        dc = gm * ig
        dpr16 = d_pr.astype(BF16)
        dpi16 = d_pi.astype(BF16)
        dc_parts = []
        for n in range(LRU_BLOCKS):
            sl = slice(n * LRU_BLOCK, (n + 1) * LRU_BLOCK)
            dwa_ref[n] += _dot(c16[:, sl], dpr16[:, sl], TN)
            dwx_ref[n] += _dot(c16[:, sl], dpi16[:, sl], TN)
            dc_parts.append(_dot(dpr16[:, sl], wa_ref[n], NT) + _dot(dpi16[:, sl], wx_ref[n], NT))
        dc = dc + jnp.concatenate(dc_parts, axis=1)
        dcext_ref[pl.ds(0, t), :] = dc
        dax = (pv[3:4] * dc + pv[2:3] * dcext_ref[pl.ds(1, t), :] + pv[1:2] * dcext_ref[pl.ds(2, t), :]
               + pv[0:1] * dcext_ref[pl.ds(3, t), :])
        dax_ref[...] = dax.astype(BF16)
        dcext_ref[pl.ds(t, HALO), :] = dc[:HALO, :]
        sums = [_colsum(dc * ext_ref[pl.ds(HALO - 3 + kk, t), :]) for kk in range(4)]
        sums += [_colsum(dc), _colsum(d_pr), _colsum(d_pi), _colsum(d_la * (LRU_C * r))]
        for kk, part in enumerate(sums):
            dpv_ref[pl.ds(kk, 1), :] += part

        @pl.when(i == nb - 1)
        def _():
            lam = pv[7:8]
            dls = dpv_ref[pl.ds(7, 1), :]
            dpv_ref[pl.ds(7, 1), :] = dls * _sigmoid(-lam)

    full = lambda shp: pl.BlockSpec(shp, lambda i: (0,) * len(shp))
    rev = lambda cb: pl.BlockSpec((t, w), functools.partial(lambda i, cb: (nb - 1 - i, cb), cb=cb))
    halo = lambda cb: pl.BlockSpec(
        (HALO, w), functools.partial(lambda i, cb: (jnp.maximum((nb - 1 - i) * hb - 1, 0), cb), cb=cb))
    gw = (LRU_BLOCKS, LRU_BLOCK, LRU_BLOCK)
    return pl.pallas_call(
        body, name="rglru_bwd", grid=(nb,),
        in_specs=[rev(0), halo(0), rev(0), halo(0), rev(0), full((8, w)), full(gw), full(gw)],
        out_specs=[rev(0), full(gw), full(gw), full((8, w))],
        out_shape=[jax.ShapeDtypeStruct((s, w), BF16), jax.ShapeDtypeStruct(gw, F32),
                   jax.ShapeDtypeStruct(gw, F32), jax.ShapeDtypeStruct((8, w), F32)],
        scratch_shapes=[pltpu.VMEM((t + HALO, w), F32), pltpu.VMEM((t + HALO, w), F32),
                        pltpu.VMEM((t + HALO, w), F32), pltpu.VMEM((t, w), F32), pltpu.VMEM((t, w), F32),
                        pltpu.VMEM((1, w), F32)],
        compiler_params=_cparams(("arbitrary",)),
    )(h0a, h0a, h, h, dh, pvec, wa16, wx16)


SB_T = 256


def _split16(x):
    hi = x.astype(BF16)
    lo = (x - hi.astype(F32)).astype(BF16)
    return jnp.concatenate([hi, lo], axis=0)


def _sb_tile(q, k, scale, tri, run, causal):
    tq = q.shape[0]
    z = _dot(q, k, NT) * scale
    log1mb = -(jnp.maximum(z, 0.0) + jnp.log(1.0 + jnp.exp(-jnp.abs(z))))
    if causal is not None:
        log1mb = jnp.where(causal, log1mb, 0.0)
    cs = _dot(_split16(log1mb), tri, NN)
    cs = cs[:tq] + cs[tq:]
    wgt = jnp.exp(z + cs + run)
    if causal is not None:
        wgt = jnp.where(causal, wgt, 0.0)
    return z, wgt, run + cs[:, 0:1]


SB_DEAD = -105.0


def _sb_alive(run):
    return (jnp.max(run) > SB_DEAD).astype(jnp.int32)


def _sb_more(carry):
    return (carry[0] >= 0) & (carry[1] > 0)


def _sb_fwd(qkv, bg):
    s = qkv.shape[0]
    t = min(SB_T, s)
    nq = s // t
    dh = SB_HEAD_DIM
    scale = 1.0 / math.sqrt(dh)

    def body(q_ref, k_ref, v_ref, bg_ref, o_ref, y_ref):
        iq = pl.program_id(1)
        q = q_ref[...]
        row = lax.broadcasted_iota(jnp.int32, (t, t), 0)
        col = lax.broadcasted_iota(jnp.int32, (t, t), 1)
        tri = jnp.where(row >= col, 1.0, 0.0).astype(BF16)
        causal = col < row

        def tile(kb, run, acc, mask):
            off = pl.multiple_of(kb * t, t)
            _, wgt, run = _sb_tile(q, k_ref[pl.ds(off, t), :], scale, tri, run, mask)
            return run, acc + _dot(wgt.astype(BF16), v_ref[pl.ds(off, t), :], NN)

        run, acc = tile(iq, jnp.zeros((t, 1), F32), jnp.zeros((t, dh), F32), causal)

        def loop(carry):
            kb, _, run, acc = carry
            run, acc = tile(kb, run, acc, None)
            return kb - 1, _sb_alive(run), run, acc

        _, _, run, acc = lax.while_loop(_sb_more, loop, (iq - 1, jnp.int32(1), run, acc))
        o_ref[...] = acc
        g = bg_ref[...]
        y_ref[...] = (acc * (g * _sigmoid(g))).astype(BF16)

    blk = lambda off: pl.BlockSpec((t, dh), functools.partial(lambda h, i, off: (i, h + off), off=off))
    col = lambda off: pl.BlockSpec((s, dh), functools.partial(lambda h, i, off: (0, h + off), off=off))
    return pl.pallas_call(
        body, name="sb_fwd", grid=(SB_HEADS, nq),
        in_specs=[blk(0), col(SB_HEADS), col(2 * SB_HEADS), blk(0)],
        out_specs=[blk(0), blk(0)],
        out_shape=[jax.ShapeDtypeStruct((s, SB_HEADS * dh), F32), jax.ShapeDtypeStruct((s, SB_HEADS * dh), BF16)],
        compiler_params=_cparams(("parallel", "arbitrary")),
    )(qkv, qkv, qkv, bg)


def _sb_bwd(qkv, dob):
    s = qkv.shape[0]
    t = min(SB_T, s)
    nq = s // t
    dh = SB_HEAD_DIM
    scale = 1.0 / math.sqrt(dh)

    def body(q_ref, k_ref, v_ref, do_ref, dq_ref, dk_ref, dv_ref, e_ref, b_ref):
        iq = pl.program_id(1)

        @pl.when(iq == 0)
        def _():
            dk_ref[...] = jnp.zeros_like(dk_ref)
            dv_ref[...] = jnp.zeros_like(dv_ref)

        q = q_ref[...]
        do = do_ref[...]
        row = lax.broadcasted_iota(jnp.int32, (t, t), 0)
        col = lax.broadcasted_iota(jnp.int32, (t, t), 1)
        tri = jnp.where(row >= col, 1.0, 0.0).astype(BF16)
        tri_x = jnp.where(row < col, 1.0, 0.0).astype(BF16)
        causal = col < row

        def sweep1(kb, run, mask):
            off = pl.multiple_of(kb * t, t)
            v = v_ref[pl.ds(off, t), :]
            z, wgt, run = _sb_tile(q, k_ref[pl.ds(off, t), :], scale, tri, run, mask)
            e_ref[kb] = wgt * _dot(do, v, NT)
            b_ref[kb] = _sigmoid(z)
            dv_ref[pl.ds(off, t), :] += _dot(wgt.astype(BF16), do, TN)
            return run

        run = sweep1(iq, jnp.zeros((t, 1), F32), causal)

        def loop1(carry):
            run = sweep1(carry[0], carry[2], None)
            return carry[0] - 1, _sb_alive(run), run

        first = lax.while_loop(_sb_more, loop1, (iq - 1, jnp.int32(1), run))[0] + 1

        def sweep2(kb, pre, dq, mask):
            off = pl.multiple_of(kb * t, t)
            e = e_ref[kb]
            beta = b_ref[kb]
            ps = _dot(_split16(e), tri_x, NN)
            ps = ps[:t] + ps[t:]
            dz = e * (1.0 - beta) - beta * (ps + pre)
            if mask is not None:
                dz = jnp.where(mask, dz, 0.0)
            dz16 = (dz * scale).astype(BF16)
            dk_ref[pl.ds(off, t), :] += _dot(dz16, q, TN)
            dq = dq + _dot(dz16, k_ref[pl.ds(off, t), :], NN)
            return pre + ps[:, t - 1:t] + e[:, t - 1:t], dq

        pre, dq = lax.fori_loop(first, iq, lambda kb, c: sweep2(kb, c[0], c[1], None),
                                (jnp.zeros((t, 1), F32), jnp.zeros((t, dh), F32)))
        _, dq = sweep2(iq, pre, dq, causal)
        dq_ref[...] = dq.astype(BF16)

    blk = lambda off: pl.BlockSpec((t, dh), functools.partial(lambda h, i, off: (i, h + off), off=off))
    col = lambda off: pl.BlockSpec((s, dh), functools.partial(lambda h, i, off: (0, h + off), off=off))
    wide = SB_HEADS * dh
    return pl.pallas_call(
        body, name="sb_bwd", grid=(SB_HEADS, nq),
        in_specs=[blk(0), col(SB_HEADS), col(2 * SB_HEADS), blk(0)],
        out_specs=[blk(0), col(0), col(0)],
        out_shape=[jax.ShapeDtypeStruct((s, wide), BF16), jax.ShapeDtypeStruct((s, wide), F32),
                   jax.ShapeDtypeStruct((s, wide), F32)],
        scratch_shapes=[pltpu.VMEM((nq, t, t), F32), pltpu.VMEM((nq, t, t), F32)],
        compiler_params=_cparams(("parallel", "arbitrary")),
    )(qkv, qkv, qkv, dob)


def _alibi_slope(h):
    return float(2.0 ** (-8.0 * (h + 1) / C_HEADS))


GROUP_ROWS = C_GROUP * WINDOW


def _swa_group_consts(n, c, sink_ref):
    qb = WINDOW
    head = lax.shift_right_logical(lax.broadcasted_iota(jnp.int32, (GROUP_ROWS, 1), 0), qb.bit_length() - 1)
    slope = jnp.zeros((GROUP_ROWS, 1), F32)
    sink = jnp.zeros((GROUP_ROWS, 1), F32)
    for hh in range(C_GROUP):
        slope = jnp.where(head == hh, _alibi_slope(c * C_GROUP + hh), slope)
        sink = jnp.where(head == hh, sink_ref[c * C_GROUP + hh], sink)
    i = lax.broadcasted_iota(jnp.int32, (GROUP_ROWS, 2 * qb), 0) & (qb - 1)
    j = lax.broadcasted_iota(jnp.int32, (GROUP_ROWS, 2 * qb), 1)
    d = i - j + qb
    valid = (d >= 0) & (d < WINDOW) & ((j >= qb) | (n > 0))
    return slope, sink, d.astype(F32), valid


def _swa_probs(qg, kw, sink, slope, dist, valid, scale):
    sc = _dot(qg, kw, NT) * scale - slope * dist
    sc = jnp.where(valid, sc, -1e30)
    m = jnp.maximum(jnp.max(sc, axis=-1, keepdims=True), sink)
    p = jnp.exp(sc - m)
    ps = jnp.exp(sink - m)
    inv = 1.0 / (jnp.sum(p, axis=-1, keepdims=True) + ps)
    return p * inv, ps * inv


def _stack_heads(x, c):
    hd = C_HEAD_DIM
    return jnp.concatenate([x[:, (c * C_GROUP + hh) * hd:(c * C_GROUP + hh + 1) * hd] for hh in range(C_GROUP)], axis=0)


def _swa_fwd(q16, gate, kvp, sinks):
    s = q16.shape[0]
    qb = WINDOW
    hd = C_HEAD_DIM
    scale = 1.0 / math.sqrt(hd)
    kvw = C_KV_HEADS * hd

    def body(sink_ref, q_ref, g_ref, kp_ref, ko_ref, o_ref, y_ref):
        n = pl.program_id(0)
        q = q_ref[...]
        kv = jnp.concatenate([kp_ref[...], ko_ref[...]], axis=0)
        for c in range(C_KV_HEADS):
            slope, sink, dist, valid = _swa_group_consts(n, c, sink_ref)
            kw = kv[:, c * hd:(c + 1) * hd]
            vw = kv[:, kvw + c * hd:kvw + (c + 1) * hd]
            p, _ = _swa_probs(_stack_heads(q, c), kw, sink, slope, dist, valid, scale)
            og = _dot(p.astype(BF16), vw, NN)
            for hh in range(C_GROUP):
                h = c * C_GROUP + hh
                o_ref[:, h * hd:(h + 1) * hd] = og[hh * qb:(hh + 1) * qb, :]
        g = g_ref[...]
        y_ref[...] = (o_ref[...] * (g * _sigmoid(g))).astype(BF16)

    wide = C_HEADS * hd
    return pl.pallas_call(
        body, name="swa_fwd", grid=(s // qb,),
        in_specs=[pl.BlockSpec(memory_space=pltpu.SMEM),
                  pl.BlockSpec((qb, wide), lambda n: (n, 0)), pl.BlockSpec((qb, wide), lambda n: (n, 0)),
                  pl.BlockSpec((qb, 2 * kvw), lambda n: (n, 0)), pl.BlockSpec((qb, 2 * kvw), lambda n: (n + 1, 0))],
        out_specs=[pl.BlockSpec((qb, wide), lambda n: (n, 0)), pl.BlockSpec((qb, wide), lambda n: (n, 0))],
        out_shape=[jax.ShapeDtypeStruct((s, wide), F32), jax.ShapeDtypeStruct((s, wide), BF16)],
        compiler_params=_cparams(("arbitrary",)),
    )(sinks, q16, gate, kvp, kvp)


def _swa_bwd(q16, do16, kvp, sinks):
    s = q16.shape[0]
    qb = WINDOW
    hd = C_HEAD_DIM
    scale = 1.0 / math.sqrt(hd)
    kvw = C_KV_HEADS * hd
    nblk = s // qb

    def body(sink_ref, q_ref, do_ref, kp_ref, ko_ref, dq_ref, dkv_ref, ds_ref, sacc_ref):
        n = pl.program_id(0)

        @pl.when(n == 0)
        def _():
            dkv_ref[...] = jnp.zeros_like(dkv_ref)
            sacc_ref[...] = jnp.zeros_like(sacc_ref)

        q = q_ref[...]
        do = do_ref[...]
        kv = jnp.concatenate([kp_ref[...], ko_ref[...]], axis=0)
        off = pl.multiple_of(n * qb, qb)
        for c in range(C_KV_HEADS):
            slope, sink, dist, valid = _swa_group_consts(n, c, sink_ref)
            kw = kv[:, c * hd:(c + 1) * hd]
            vw = kv[:, kvw + c * hd:kvw + (c + 1) * hd]
            qg = _stack_heads(q, c)
            dog = _stack_heads(do, c)
            p, ps = _swa_probs(qg, kw, sink, slope, dist, valid, scale)
            dp = _dot(dog, vw, NT)
            dd = jnp.sum(p * dp, axis=-1, keepdims=True)
            ds16 = (p * (dp - dd) * scale).astype(BF16)
            sacc_ref[c] += ps * dd
            dqg = _dot(ds16, kw, NN).astype(BF16)
            for hh in range(C_GROUP):
                h = c * C_GROUP + hh
                dq_ref[:, h * hd:(h + 1) * hd] = dqg[hh * qb:(hh + 1) * qb, :]
            dkv_ref[pl.ds(off, 2 * qb), c * hd:(c + 1) * hd] += _dot(ds16, qg, TN)
            dkv_ref[pl.ds(off, 2 * qb), kvw + c * hd:kvw + (c + 1) * hd] += _dot(p.astype(BF16), dog, TN)

        @pl.when(n == nblk - 1)
        def _():
            lane = lax.broadcasted_iota(jnp.int32, (1, 128), 1)
            row = jnp.zeros((1, 128), F32)
            for c in range(C_KV_HEADS):
                for hh in range(C_GROUP):
                    tot = jnp.sum(sacc_ref[c, pl.ds(hh * qb, qb), :], axis=0, keepdims=True)
                    row = jnp.where(lane == c * C_GROUP + hh, -tot, row)
            ds_ref[...] = row

    wide = C_HEADS * hd
    return pl.pallas_call(
        body, name="swa_bwd", grid=(nblk,),
        in_specs=[pl.BlockSpec(memory_space=pltpu.SMEM),
                  pl.BlockSpec((qb, wide), lambda n: (n, 0)), pl.BlockSpec((qb, wide), lambda n: (n, 0)),
                  pl.BlockSpec((qb, 2 * kvw), lambda n: (n, 0)), pl.BlockSpec((qb, 2 * kvw), lambda n: (n + 1, 0))],
        out_specs=[pl.BlockSpec((qb, wide), lambda n: (n, 0)),
                   pl.BlockSpec((s + qb, 2 * kvw), lambda n: (0, 0)), pl.BlockSpec((1, 128), lambda n: (0, 0))],
        out_shape=[jax.ShapeDtypeStruct((s, wide), BF16), jax.ShapeDtypeStruct((s + qb, 2 * kvw), F32),
                   jax.ShapeDtypeStruct((1, 128), F32)],
        scratch_shapes=[pltpu.VMEM((C_KV_HEADS, GROUP_ROWS, 1), F32)],
        compiler_params=_cparams(("arbitrary",)),
    )(sinks, q16, do16, kvp, kvp)


def _adamw(parts, w, m, v, *, name, tr=496):
    npart, r, c = parts.shape
    tr = min(tr, r)
    assert r % tr == 0 and tr % 16 == 0
    c1 = 1.0 / (1.0 - ADAM_B1 ** ADAM_STEP)
    c2 = 1.0 / (1.0 - ADAM_B2 ** ADAM_STEP)

    def body(p_ref, w_ref, m_ref, v_ref, g_ref, d_ref, nm_ref, nv_ref):
        g = p_ref[0].astype(F32)
        for j in range(1, npart):
            g = g + p_ref[j].astype(F32)
        nm = ADAM_B1 * m_ref[...] + (1.0 - ADAM_B1) * g
        nv = ADAM_B2 * v_ref[...] + (1.0 - ADAM_B2) * (g * g)
        g_ref[...] = g
        nm_ref[...] = nm
        nv_ref[...] = nv
        d_ref[...] = -ADAM_LR * ((nm * c1) / (jnp.sqrt(nv * c2) + ADAM_EPS) + ADAM_WD * w_ref[...])

    spec = pl.BlockSpec((tr, c), lambda i: (i, 0))
    return pl.pallas_call(
        body, name=name, grid=(r // tr,),
        in_specs=[pl.BlockSpec((npart, tr, c), lambda i: (0, i, 0)), spec, spec, spec],
        out_specs=[spec] * 4, out_shape=[jax.ShapeDtypeStruct((r, c), F32)] * 4,
        compiler_params=_cparams(("parallel",)),
    )(parts, w, m, v)


def _exchange(bufs, gather, *, name):
    nb = len(bufs)

    def body(*refs):
        src = refs[:nb]
        dst = refs[nb:2 * nb]
        send_sems, recv_sems, local_sems = refs[2 * nb:]
        x, y, c = lax.axis_index("x"), lax.axis_index("y"), lax.axis_index("c")
        me = 4 * x + 2 * y + c
        copies = []
        for b in range(nb):
            mine = src[b] if gather[b] else src[b].at[me]
            local = pltpu.make_async_copy(mine, dst[b].at[me], local_sems.at[b])
            local.start()
            copies.append(local)
        remote = []
        for k in range(1, N_DEV):
            px, py, pc = x ^ (k >> 2), y ^ ((k >> 1) & 1), c ^ (k & 1)
            peer = 4 * px + 2 * py + pc
            for b in range(nb):
                cp = pltpu.make_async_remote_copy(
                    src_ref=src[b] if gather[b] else src[b].at[peer],
                    dst_ref=dst[b].at[me],
                    send_sem=send_sems.at[b, k - 1], recv_sem=recv_sems.at[b, k - 1],
                    device_id=(px, py, pc), device_id_type=pl.DeviceIdType.MESH)
                cp.start()
                remote.append((cp, b, k, peer))
        for cp, b, k, peer in remote:
            cp.wait_send()
        for cp, b, k, peer in remote:
            pltpu.make_async_remote_copy(
                src_ref=src[b] if gather[b] else src[b].at[me],
                dst_ref=dst[b].at[peer],
                send_sem=send_sems.at[b, k - 1], recv_sem=recv_sems.at[b, k - 1],
                device_id=(x, y, c), device_id_type=pl.DeviceIdType.MESH).wait_recv()
        for local in copies:
            local.wait()

    out_shape = []
    for b, g in zip(bufs, gather):
        shp = b.shape if g else b.shape[1:]
        out_shape.append(jax.ShapeDtypeStruct((N_DEV,) + tuple(shp), b.dtype))
    hbm = pl.BlockSpec(memory_space=pl.ANY)
    return pl.pallas_call(
        body, name=name, in_specs=[hbm] * nb, out_specs=[hbm] * nb, out_shape=out_shape,
        scratch_shapes=[pltpu.SemaphoreType.DMA((nb, N_DEV - 1)), pltpu.SemaphoreType.DMA((nb, N_DEV - 1)),
                        pltpu.SemaphoreType.DMA((nb,))],
        compiler_params=pltpu.CompilerParams(has_side_effects=True),
    )(*bufs)


R_WIN_E, R_WOUT_E, R_WIN_O, R_WOUT_O, R_SMALL, R_REPL = 768, 256, 288, 128, 8, 40
SEG_ROWS = R_WIN_E + R_WOUT_E + R_WIN_O + R_WOUT_O + R_SMALL + R_REPL
REPL_ROWS = N_DEV * R_REPL
REPL_LEN = REPL_ROWS * D_MODEL


def _rows(a):
    return a.reshape(-1, D_MODEL)


def _pack_small(conv_w, ln_g, ln_b):
    lead = conv_w.shape[:-2]
    flat = jnp.concatenate([conv_w.reshape(lead + (512,)), ln_g.reshape(lead + (128,)), ln_b.reshape(lead + (128,))],
                           axis=-1)
    return jnp.pad(flat[..., None, :], [(0, 0)] * len(lead) + [(0, R_SMALL - 1), (0, D_MODEL - 768)])


def _pack_sharded(w_in_e, w_out_e, w_in_o, w_out_o, conv_w, ln_g, ln_b):
    return jnp.concatenate([_rows(w_in_e), _rows(w_out_e), _rows(w_in_o), _rows(w_out_o),
                            _pack_small(conv_w[0], ln_g[0], ln_b[0]), jnp.zeros((R_REPL, D_MODEL), F32)], axis=0)


def _unpack_sharded(p):
    r0 = 0
    out = []
    for rows, shp in ((R_WIN_E, (1, D_MODEL, 768)), (R_WOUT_E, (1, 256, D_MODEL)), (R_WIN_O, (1, D_MODEL, 288)),
                      (R_WOUT_O, (1, 128, D_MODEL))):
        out.append(p[r0:r0 + rows].reshape(shp))
        r0 += rows
    small = p[r0]
    out.append(small[0:512].reshape(1, 4, 128))
    out.append(small[512:640].reshape(1, 128))
    out.append(small[640:768].reshape(1, 128))
    return out


REPL_SHAPES = ((1, 8, 128, 128), (1, 8, 128, 128), (1, 1024), (1, 1024), (1, 1024), (1, 1024), (1, 1024), (1, 1024),
               (1, 16))


def _pack_repl(parts):
    flat = jnp.concatenate([p.reshape(-1) for p in parts])
    return jnp.concatenate([flat, jnp.zeros((REPL_LEN - flat.shape[0],), F32)]).reshape(REPL_ROWS, D_MODEL)


def _unpack_repl(p):
    flat = p.reshape(-1)
    out, o = [], 0
    for shp in REPL_SHAPES:
        n = math.prod(shp)
        out.append(flat[o:o + n].reshape(shp))
        o += n
    return out


def kernel(x, e_w_in, e_conv_w, e_conv_b, e_w_gate_a, e_b_gate_a, e_w_gate_x, e_b_gate_x, e_lru_lambda, e_w_out, e_ln_g, e_ln_b, o_w_in, o_sinks, o_w_out, o_ln_g, o_ln_b, loss_target, m_e_w_in, m_e_conv_w, m_e_conv_b, m_e_w_gate_a, m_e_b_gate_a, m_e_w_gate_x, m_e_b_gate_x, m_e_lru_lambda, m_e_w_out, m_e_ln_g, m_e_ln_b, m_o_w_in, m_o_sinks, m_o_w_out, m_o_ln_g, m_o_ln_b, v_e_w_in, v_e_conv_w, v_e_conv_b, v_e_w_gate_a, v_e_b_gate_a, v_e_w_gate_x, v_e_b_gate_x, v_e_lru_lambda, v_e_w_out, v_e_ln_g, v_e_ln_b, v_o_w_in, v_o_sinks, v_o_w_out, v_o_ln_g, v_o_ln_b):
    d = D_MODEL
    x0 = x[0]
    target = loss_target[0]
    s = x0.shape[0]

    wpack = jnp.concatenate([_rows(e_w_in), _rows(e_w_out), _rows(o_w_in), _rows(o_w_out)], axis=0).astype(BF16)
    spack = _pack_small(e_conv_w[0], o_ln_g[0], o_ln_b[0])
    wall, sall = _exchange([wpack, spack], [True, True], name="gather_weights")
    r0 = 0
    wi_e = wall[:, r0:r0 + R_WIN_E].reshape(N_DEV, d, 768).transpose(1, 0, 2).reshape(d, 6 * d)
    r0 += R_WIN_E
    wo_e = wall[:, r0:r0 + R_WOUT_E].reshape(2 * d, d)
    r0 += R_WOUT_E
    wi_o = wall[:, r0:r0 + R_WIN_O].reshape(N_DEV, d, 288).transpose(1, 0, 2).reshape(d, 2304)
    r0 += R_WIN_O
    wo_o = wall[:, r0:r0 + R_WOUT_O].reshape(d, d)
    wi_o = jnp.concatenate([wi_o[:, :1024], wi_o[:, 1280:], wi_o[:, 1024:1280]], axis=1)
    conv_w = sall[:, 0, 0:512].reshape(N_DEV, 4, 128).transpose(1, 0, 2).reshape(4, d)
    ln_g_o = sall[:, 0, 512:640].reshape(1, d)
    ln_b_o = sall[:, 0, 640:768].reshape(1, d)
    pvec = jnp.concatenate([conv_w, e_conv_b, e_b_gate_a, e_b_gate_x, e_lru_lambda], axis=0)
    wa16 = e_w_gate_a[0].astype(BF16)
    wx16 = e_w_gate_x[0].astype(BF16)
    sinks = o_sinks[0]

    x0_16 = x0.astype(BF16)
    h0a = _matmul(x0_16, wi_e, mode="nn", n_out=2 * d, b_off=0, name="l0_in_a")
    qkv = _matmul(x0_16, wi_e, mode="nn", n_out=3 * d, b_off=2 * d, out_dtype=BF16, name="l0_in_qkv")
    bg = _matmul(x0_16, wi_e, mode="nn", n_out=d, b_off=5 * d, name="l0_in_bg")
    ya, hst = _rglru_fwd(h0a, pvec, wa16, wx16)
    ob, yb = _sb_fwd(qkv, bg)
    yab = jnp.concatenate([ya, yb], axis=1)
    y0 = _matmul(yab, wo_e, mode="nn", tk=2048, name="l0_out")
    z0, x1, x1_16 = _ln_fwd(x0, y0, e_ln_g, e_ln_b, name="l0_ln")

    q1 = _matmul(x1_16, wi_o, mode="nn", n_out=d, b_off=0, out_dtype=BF16, name="l1_in_q")
    g1 = _matmul(x1_16, wi_o, mode="nn", n_out=d, b_off=d, name="l1_in_g")
    kv1 = _matmul(x1_16, wi_o, mode="nn", n_out=256, b_off=2 * d, tn=256, out_dtype=BF16, name="l1_in_kv")
    kvp = jnp.concatenate([jnp.zeros((WINDOW, 256), BF16), kv1], axis=0)
    o1, y1 = _swa_fwd(q1, g1, kvp, sinks)
    yy1 = _matmul(y1, wo_o, mode="nn", name="l1_out")
    dz1, loss_cols, dg_o, db_o = _ln_loss_bwd(x1, yy1, target, ln_g_o, ln_b_o, name="l1_ln_loss")
    loss_hi = jnp.sum(loss_cols).astype(BF16).astype(F32)
    loss_terms = jnp.stack([loss_hi, jnp.sum(loss_cols) - loss_hi]).reshape(1, 2)

    dz1_16 = dz1.astype(BF16)
    dwo_o = _matmul(y1, dz1_16, mode="tn", name="l1_dwout")
    dy1 = _matmul(dz1_16, wo_o, mode="nt", name="l1_dy")
    do1, dg1 = _gate_bwd((dy1, 0), o1, g1, 0, do_dtype=BF16, name="l1_gate_bwd")
    dq1, dkvp, dsink = _swa_bwd(q1, do1, kvp, sinks)
    dh1 = jnp.concatenate([dq1, dg1, dkvp[WINDOW:].astype(BF16)], axis=1)
    dwi_o = _matmul(x1_16, dh1, mode="tn", tn=1152, name="l1_dwin")
    dwi_o = jnp.concatenate([dwi_o[:, :1024], dwi_o[:, 2048:], dwi_o[:, 1024:2048]], axis=1)
    dx1 = _matmul(dh1, wi_o, mode="nt", tm=512, tk=2304, res=dz1, alpha=ALPHA, name="l1_dx")

    dz0, dg_e, db_e = _ln_bwd_call(z0, dx1, e_ln_g, name="l0_ln_bwd")
    dz0_16 = dz0.astype(BF16)
    dwo_e = _matmul(yab, dz0_16, mode="tn", name="l0_dwout")
    dyab = _matmul(dz0_16, wo_e, mode="nt", name="l0_dy")
    dhst, dag = _gate_bwd((dyab, 0), hst, h0a, 1, do_dtype=F32, name="l0_gate_bwd_a")
    dob, dbg = _gate_bwd((dyab, 1), ob, bg, 0, do_dtype=BF16, name="l0_gate_bwd_b")
    dq0, dk0, dv0 = _sb_bwd(qkv, dob)
    dax, dwa, dwx, dpv = _rglru_bwd(h0a, hst, dhst, pvec, wa16, wx16)
    dh0 = jnp.concatenate([dax, dag, dq0, dk0.astype(BF16), dv0.astype(BF16), dbg], axis=1)
    dwi_e = _matmul(x0_16, dh0, mode="tn", name="l0_dwin")
    grad_x = _matmul(dh0, wi_e, mode="nt", tk=2048, res=dz0, alpha=ALPHA, name="l0_dx")

    repl = _pack_repl([dwa, dwx, dpv[4:5], dpv[5:6], dpv[6:7], dpv[7:8], dg_e, db_e, dsink[:, :C_HEADS], loss_terms])
    seg = jnp.concatenate([
        dwi_e.reshape(d, N_DEV, 768).transpose(1, 0, 2).reshape(N_DEV, R_WIN_E, d),
        dwo_e.reshape(N_DEV, R_WOUT_E, d),
        dwi_o.reshape(d, N_DEV, 288).transpose(1, 0, 2).reshape(N_DEV, R_WIN_O, d),
        dwo_o.reshape(N_DEV, R_WOUT_O, d),
        _pack_small(dpv[0:4].reshape(4, N_DEV, 128).transpose(1, 0, 2), dg_o.reshape(N_DEV, 128),
                    db_o.reshape(N_DEV, 128)),
        repl.reshape(N_DEV, R_REPL, d)], axis=1).astype(BF16)
    (parts,) = _exchange([seg], [False], name="scatter_grads")

    w_sh = _pack_sharded(e_w_in, e_w_out, o_w_in, o_w_out, e_conv_w, o_ln_g, o_ln_b)
    m_sh = _pack_sharded(m_e_w_in, m_e_w_out, m_o_w_in, m_o_w_out, m_e_conv_w, m_o_ln_g, m_o_ln_b)
    v_sh = _pack_sharded(v_e_w_in, v_e_w_out, v_o_w_in, v_o_w_out, v_e_conv_w, v_o_ln_g, v_o_ln_b)
    g_sh, d_sh, nm_sh, nv_sh = _adamw(parts, w_sh, m_sh, v_sh, name="adamw_sharded")

    (g_repl,) = _exchange([g_sh[SEG_ROWS - R_REPL:]], [True], name="gather_repl_grads")
    g_repl = g_repl.reshape(1, REPL_ROWS, d)
    w_r = _pack_repl([e_w_gate_a, e_w_gate_x, e_conv_b, e_b_gate_a, e_b_gate_x, e_lru_lambda, e_ln_g, e_ln_b, o_sinks])
    m_r = _pack_repl([m_e_w_gate_a, m_e_w_gate_x, m_e_conv_b, m_e_b_gate_a, m_e_b_gate_x, m_e_lru_lambda, m_e_ln_g,
                      m_e_ln_b, m_o_sinks])
    v_r = _pack_repl([v_e_w_gate_a, v_e_w_gate_x, v_e_conv_b, v_e_b_gate_a, v_e_b_gate_x, v_e_lru_lambda, v_e_ln_g,
                      v_e_ln_b, v_o_sinks])
    g_r, d_r, nm_r, nv_r = _adamw(g_repl, w_r, m_r, v_r, name="adamw_replicated")
    loss_at = sum(math.prod(shp) for shp in REPL_SHAPES)
    loss = g_r[loss_at // d, loss_at % d] + g_r[(loss_at + 1) // d, (loss_at + 1) % d]

    def assemble(sh, rp):
        w_in_e, w_out_e, w_in_o, w_out_o, cw, lg_o, lb_o = _unpack_sharded(sh)
        w_a, w_x, cb, b_a, b_x, lam, lg_e, lb_e, snk = _unpack_repl(rp)
        return [w_in_e, cw, cb, w_a, b_a, w_x, b_x, lam, w_out_e, lg_e, lb_e, w_in_o, snk, w_out_o, lg_o, lb_o]

    return (loss, grad_x[None], *assemble(g_sh, g_r), *assemble(d_sh, d_r), *assemble(nm_sh, nm_r),
            *assemble(nv_sh, nv_r))
```

```python
import functools
import math

import jax
import jax.numpy as jnp
from jax import lax
from jax.experimental import pallas as pl
from jax.experimental.pallas import tpu as pltpu

F32 = jnp.float32
BF16 = jnp.bfloat16

N_DEV = 8
D_MODEL = 1024
LRU_BLOCKS = 8
LRU_BLOCK = 128
LRU_C = 8.0
SB_HEADS = 8
SB_HEAD_DIM = 128
C_HEADS = 16
C_KV_HEADS = 2
C_GROUP = 8
C_HEAD_DIM = 64
WINDOW = 128
DEPTH = 2
ALPHA = float((2 * DEPTH) ** 0.25)
LN_EPS = 1e-5
ADAM_LR = 0.001
ADAM_B1 = 0.9
ADAM_B2 = 0.999
ADAM_EPS = 1e-08
ADAM_WD = 0.01
ADAM_STEP = 10

VMEM_LIMIT = 56 * 1024 * 1024

NN = ((1,), (0,))
NT = ((1,), (1,))
TN = ((0,), (0,))


def _dot(a, b, dims):
    return lax.dot_general(a, b, (dims, ((), ())), preferred_element_type=F32)


def _sigmoid(x):
    return 1.0 / (1.0 + jnp.exp(-x))


def _cparams(sem, vmem=VMEM_LIMIT):
    return pltpu.CompilerParams(dimension_semantics=sem, vmem_limit_bytes=vmem)


def _matmul(a, b, *, mode, n_out=None, b_off=0, out_dtype=F32, res=None, alpha=1.0,
            tm=1024, tn=1024, tk=1024, comm=None, name):
    if mode == "nn":
        m, k = a.shape
    elif mode == "nt":
        m, k = a.shape
    else:
        k, m = a.shape
    n = n_out if n_out is not None else (b.shape[0] if mode == "nt" else b.shape[1])
    tm, tn, tk = min(tm, m), min(tn, n), min(tk, k)
    assert m % tm == 0 and n % tn == 0 and k % tk == 0 and b_off % tn == 0
    grid = (m // tm, n // tn, k // tk)
    nk = grid[2]
    jo = b_off // tn
    dims = {"nn": NN, "nt": NT, "tn": TN}[mode]
    n_in = 2 + (res is not None)
    nc = len(comm[0]) if comm else 0

    def body(*refs):
        a_ref, b_ref = refs[:2]
        r_ref = refs[2] if res is not None else None
        o_ref = refs[n_in + nc]
        if comm:
            step = [pl.program_id(ax) for ax in range(3)]
            c_src = refs[n_in:n_in + nc]
            c_dst = refs[n_in + nc + 1:n_in + 2 * nc + 1]
            sems = refs[len(refs) - 3:]

            @pl.when((step[0] == 0) & (step[1] == 0) & (step[2] == 0))
            def _():
                _exchange_start(c_src, c_dst, comm[1], sems)

        part = _dot(a_ref[...].astype(BF16), b_ref[...].astype(BF16), dims)

        def finish(out):
            if r_ref is not None:
                out = out + alpha * r_ref[...]
            o_ref[...] = out.astype(out_dtype)

        if nk == 1:
            finish(part)
        else:
            acc_ref = refs[n_in + 2 * nc + 1]
            kk = pl.program_id(2)

            @pl.when(kk == 0)
            def _():
                acc_ref[...] = part

            @pl.when(kk > 0)
            def _():
                acc_ref[...] += part

            @pl.when(kk == nk - 1)
            def _():
                finish(acc_ref[...])

        if comm:
            @pl.when((step[0] == grid[0] - 1) & (step[1] == grid[1] - 1) & (step[2] == grid[2] - 1))
            def _():
                _exchange_wait(c_src, c_dst, comm[1], sems)

    if mode == "tn":
        a_spec = pl.BlockSpec((tk, tm), lambda i, j, kk: (kk, i))
    else:
        a_spec = pl.BlockSpec((tm, tk), lambda i, j, kk: (i, kk))
    if mode == "nt":
        b_spec = pl.BlockSpec((tn, tk), lambda i, j, kk: (j + jo, kk))
    else:
        b_spec = pl.BlockSpec((tk, tn), lambda i, j, kk: (kk, j + jo))
    o_spec = pl.BlockSpec((tm, tn), lambda i, j, kk: (i, j))
    hbm = pl.BlockSpec(memory_space=pl.ANY)
    in_specs = [a_spec, b_spec] + ([o_spec] if res is not None else []) + [hbm] * nc
    args = [a, b] + ([res] if res is not None else []) + (list(comm[0]) if comm else [])
    out_shape = [jax.ShapeDtypeStruct((m, n), out_dtype)] + (_exchange_out_shapes(*comm) if comm else [])
    scratch = ([pltpu.VMEM((tm, tn), F32)] if nk > 1 else []) + (_exchange_sems(nc) if comm else [])
    if comm:
        params = pltpu.CompilerParams(dimension_semantics=("arbitrary",) * 3, vmem_limit_bytes=VMEM_LIMIT,
                                      has_side_effects=True)
    else:
        params = _cparams(("parallel", "parallel", "arbitrary"))
    outs = pl.pallas_call(
        body, name=name, grid=grid, in_specs=in_specs, out_specs=[o_spec] + [hbm] * nc, out_shape=out_shape,
        scratch_shapes=scratch, compiler_params=params,
    )(*args)
    return tuple(outs) if comm else outs[0]


def _rowwise(fn, row_ins, vec_ins, row_outs, acc_outs, *, tr=256, name):
    s = row_ins[0][0].shape[0]
    assert s % tr == 0
    n_ri, n_vi, n_ro = len(row_ins), len(vec_ins), len(row_outs)

    def body(*refs):
        rows = [r[...] for r in refs[:n_ri]]
        vecs = [r[...] for r in refs[n_ri:n_ri + n_vi]]
        o_refs = refs[n_ri + n_vi:n_ri + n_vi + n_ro]
        a_refs = refs[n_ri + n_vi + n_ro:]
        outs, accs = fn(rows, vecs)
        for o_ref, o in zip(o_refs, outs):
            o_ref[...] = o.astype(o_ref.dtype)
        if a_refs:
            i = pl.program_id(0)

            @pl.when(i == 0)
            def _():
                for a_ref, part in zip(a_refs, accs):
                    a_ref[...] = part

            @pl.when(i > 0)
            def _():
                for a_ref, part in zip(a_refs, accs):
                    a_ref[...] += part

    in_specs = [pl.BlockSpec((tr, c), functools.partial(lambda i, cb: (i, cb), cb=cb)) for (_, c, cb) in row_ins]
    in_specs += [pl.BlockSpec(v.shape, functools.partial(lambda i, nd: (0,) * nd, nd=v.ndim)) for v in vec_ins]
    out_specs = [pl.BlockSpec((tr, c), lambda i: (i, 0)) for (c, _) in row_outs]
    out_specs += [pl.BlockSpec((r, c), lambda i: (0, 0)) for (r, c) in acc_outs]
    out_shape = [jax.ShapeDtypeStruct((s, c), dt) for (c, dt) in row_outs]
    out_shape += [jax.ShapeDtypeStruct((r, c), F32) for (r, c) in acc_outs]
    return pl.pallas_call(
        body, name=name, grid=(s // tr,), in_specs=in_specs, out_specs=out_specs, out_shape=out_shape,
        compiler_params=_cparams(("arbitrary",)),
    )(*[a for (a, _, _) in row_ins], *vec_ins)


def _ln_stats(z):
    mu = jnp.mean(z, axis=-1, keepdims=True)
    zc = z - mu
    var = jnp.mean(zc * zc, axis=-1, keepdims=True)
    rstd = lax.rsqrt(var + LN_EPS)
    return zc * rstd, rstd


def _ln_bwd(dy, xhat, rstd, g):
    dxh = dy * g
    m1 = jnp.mean(dxh, axis=-1, keepdims=True)
    m2 = jnp.mean(dxh * xhat, axis=-1, keepdims=True)
    return rstd * (dxh - m1 - xhat * m2)


def _colsum(x):
    return jnp.sum(x, axis=0, keepdims=True)


def _ln_fwd(x, y, g, b, *, name):
    def fn(rows, vecs):
        z = ALPHA * rows[0] + rows[1]
        xhat, _ = _ln_stats(z)
        x1 = xhat * vecs[0] + vecs[1]
        return [z, x1, x1], []
    c = x.shape[1]
    return _rowwise(fn, [(x, c, 0), (y, c, 0)], [g, b], [(c, F32), (c, F32), (c, BF16)], [], name=name)


def _ln_loss_bwd(x, y, target, g, b, *, name):
    c = x.shape[1]

    def fn(rows, vecs):
        z = ALPHA * rows[0] + rows[1]
        xhat, rstd = _ln_stats(z)
        d = xhat * vecs[0] + vecs[1] - rows[2]
        dy = d * (1.0 / c)
        dz = _ln_bwd(dy, xhat, rstd, vecs[0])
        return [dz], [_colsum(0.5 * d * d * (1.0 / c)), _colsum(dy * xhat), _colsum(dy)]
    return _rowwise(fn, [(x, c, 0), (y, c, 0), (target, c, 0)], [g, b], [(c, F32)], [(1, c)] * 3, name=name)


def _ln_bwd_call(z, dy, g, *, name):
    c = z.shape[1]

    def fn(rows, vecs):
        xhat, rstd = _ln_stats(rows[0])
        dz = _ln_bwd(rows[1], xhat, rstd, vecs[0])
        return [dz], [_colsum(rows[1] * xhat), _colsum(rows[1])]
    return _rowwise(fn, [(z, c, 0), (dy, c, 0)], [g], [(c, F32)], [(1, c)] * 2, name=name)


def _gate_bwd(dy, o, gate, gate_cb, *, do_dtype, name):
    dy_arr, dy_cb = dy
    c = o.shape[1]

    def fn(rows, vecs):
        d, oo, gg = rows
        sg = _sigmoid(gg)
        return [d * (gg * sg), d * oo * (sg * (1.0 + gg * (1.0 - sg)))], []
    return _rowwise(fn, [(dy_arr, c, dy_cb), (o, c, 0), (gate, c, gate_cb)], [], [(c, do_dtype), (c, BF16)], [], name=name)


LRU_T = 256
HALO = 8


def _log1p(y):
    u = 1.0 + y
    return jnp.where(u == 1.0, y, jnp.log(u) * (y / (u - 1.0)))


def _lru_gates(c, wa_ref, wx_ref, pv):
    c16 = c.astype(BF16)
    pre_r = jnp.concatenate(
        [_dot(c16[:, n * LRU_BLOCK:(n + 1) * LRU_BLOCK], wa_ref[n], NN) for n in range(LRU_BLOCKS)], axis=1)
    pre_i = jnp.concatenate(
        [_dot(c16[:, n * LRU_BLOCK:(n + 1) * LRU_BLOCK], wx_ref[n], NN) for n in range(LRU_BLOCKS)], axis=1)
    r = _sigmoid(pre_r + pv[5:6])
    ig = _sigmoid(pre_i + pv[6:7])
    lam = pv[7:8]
    ls = jnp.minimum(lam, 0.0) - _log1p(jnp.exp(-jnp.abs(lam)))
    la = LRU_C * r * ls
    a = jnp.exp(la)
    a2 = a * a
    m = jnp.sqrt(jnp.tanh(-la) * (a2 + 1.0))
    return c16, r, ig, ls, la, a, a2, m


def _conv(ext_ref, x, pv, t):
    return (pv[4:5] + pv[3:4] * x + pv[2:3] * ext_ref[pl.ds(HALO - 1, t), :]
            + pv[1:2] * ext_ref[pl.ds(HALO - 2, t), :] + pv[0:1] * ext_ref[pl.ds(HALO - 3, t), :])


def _rglru_fwd(h0a, pvec, wa16, wx16):
    s = h0a.shape[0]
    w = D_MODEL
    t = min(LRU_T, s)
    assert s % t == 0

    def body(ax_ref, ag_ref, pv_ref, wa_ref, wx_ref, ya_ref, h_ref, ext_ref, a_ref, u_ref, hc_ref):
        i = pl.program_id(0)

        @pl.when(i == 0)
        def _():
            ext_ref[pl.ds(0, HALO), :] = jnp.zeros((HALO, w), F32)
            hc_ref[...] = jnp.zeros((1, w), F32)

        pv = pv_ref[...]
        ax = ax_ref[...]
        ext_ref[pl.ds(HALO, t), :] = ax
        c = _conv(ext_ref, ax, pv, t)
        ext_ref[pl.ds(0, HALO), :] = ax[t - HALO:, :]
        _, _, ig, _, _, a, _, m = _lru_gates(c, wa_ref, wx_ref, pv)
        a_ref[...] = a
        u_ref[...] = m * (ig * c)

        def step(k, h):
            h = a_ref[pl.ds(k, 1), :] * h + u_ref[pl.ds(k, 1), :]
            h_ref[pl.ds(k, 1), :] = h
            return h

        hc_ref[...] = lax.fori_loop(0, t, step, hc_ref[...], unroll=8)
        ag = ag_ref[...]
        ya_ref[...] = (h_ref[...] * (ag * _sigmoid(ag))).astype(BF16)

    full = lambda shp: pl.BlockSpec(shp, lambda i: (0,) * len(shp))
    return pl.pallas_call(
        body, name="rglru_fwd", grid=(s // t,),
        in_specs=[pl.BlockSpec((t, w), lambda i: (i, 0)), pl.BlockSpec((t, w), lambda i: (i, 1)),
                  full((8, w)), full((LRU_BLOCKS, LRU_BLOCK, LRU_BLOCK)), full((LRU_BLOCKS, LRU_BLOCK, LRU_BLOCK))],
        out_specs=[pl.BlockSpec((t, w), lambda i: (i, 0)), pl.BlockSpec((t, w), lambda i: (i, 0))],
        out_shape=[jax.ShapeDtypeStruct((s, w), BF16), jax.ShapeDtypeStruct((s, w), F32)],
        scratch_shapes=[pltpu.VMEM((t + HALO, w), F32), pltpu.VMEM((t, w), F32), pltpu.VMEM((t, w), F32),
                        pltpu.VMEM((1, w), F32)],
        compiler_params=_cparams(("arbitrary",)),
    )(h0a, h0a, pvec, wa16, wx16)


def _rglru_bwd(h0a, h, dh, pvec, wa16, wx16):
    s = h0a.shape[0]
    w = D_MODEL
    t = min(LRU_T, s)
    nb = s // t
    hb = t // HALO

    def body(ax_ref, axh_ref, h_ref, hh_ref, dh_ref, pv_ref, wa_ref, wx_ref,
             dax_ref, dwa_ref, dwx_ref, dpv_ref, ext_ref, hext_ref, dcext_ref, a_ref, g_ref, gc_ref):
        i = pl.program_id(0)
        blk = nb - 1 - i

        @pl.when(i == 0)
        def _():
            dwa_ref[...] = jnp.zeros_like(dwa_ref)
            dwx_ref[...] = jnp.zeros_like(dwx_ref)
            dpv_ref[...] = jnp.zeros_like(dpv_ref)
            gc_ref[...] = jnp.zeros((1, w), F32)
            dcext_ref[pl.ds(t, HALO), :] = jnp.zeros((HALO, w), F32)

        pv = pv_ref[...]
        ax = ax_ref[...]
        keep = (blk > 0).astype(F32)
        ext_ref[pl.ds(0, HALO), :] = axh_ref[...] * keep
        ext_ref[pl.ds(HALO, t), :] = ax
        hext_ref[pl.ds(0, HALO), :] = hh_ref[...] * keep
        hext_ref[pl.ds(HALO, t), :] = h_ref[...]
        c = _conv(ext_ref, ax, pv, t)
        c16, r, ig, ls, _, a, a2, m = _lru_gates(c, wa_ref, wx_ref, pv)
        a_ref[...] = a

        def step(k, carry):
            row = t - 1 - k
            g = dh_ref[pl.ds(row, 1), :] + carry
            g_ref[pl.ds(row, 1), :] = g
            return a_ref[pl.ds(row, 1), :] * g

        gc_ref[...] = lax.fori_loop(0, t, step, gc_ref[...], unroll=8)
        g = g_ref[...]
        hprev = hext_ref[pl.ds(HALO - 1, t), :]
        gm = g * m
        d_la = g * hprev * a - (g * ig * c) * a2 / m
        d_pr = d_la * (LRU_C * ls) * r * (1.0 - r)
        d_pi = gm * c * ig * (1.0 - ig)
        dc = gm * ig
        dpr16 = d_pr.astype(BF16)
        dpi16 = d_pi.astype(BF16)
        dc_parts = []
        for n in range(LRU_BLOCKS):
            sl = slice(n * LRU_BLOCK, (n + 1) * LRU_BLOCK)
            dwa_ref[n] += _dot(c16[:, sl], dpr16[:, sl], TN)
            dwx_ref[n] += _dot(c16[:, sl], dpi16[:, sl], TN)
            dc_parts.append(_dot(dpr16[:, sl], wa_ref[n], NT) + _dot(dpi16[:, sl], wx_ref[n], NT))
        dc = dc + jnp.concatenate(dc_parts, axis=1)
        dcext_ref[pl.ds(0, t), :] = dc
        dax = (pv[3:4] * dc + pv[2:3] * dcext_ref[pl.ds(1, t), :] + pv[1:2] * dcext_ref[pl.ds(2, t), :]
               + pv[0:1] * dcext_ref[pl.ds(3, t), :])
        dax_ref[...] = dax.astype(BF16)
        dcext_ref[pl.ds(t, HALO), :] = dc[:HALO, :]
        sums = [_colsum(dc * ext_ref[pl.ds(HALO - 3 + kk, t), :]) for kk in range(4)]
        sums += [_colsum(dc), _colsum(d_pr), _colsum(d_pi), _colsum(d_la * (LRU_C * r))]
        for kk, part in enumerate(sums):
            dpv_ref[pl.ds(kk, 1), :] += part

        @pl.when(i == nb - 1)
        def _():
            lam = pv[7:8]
            dls = dpv_ref[pl.ds(7, 1), :]
            dpv_ref[pl.ds(7, 1), :] = dls * _sigmoid(-lam)

    full = lambda shp: pl.BlockSpec(shp, lambda i: (0,) * len(shp))
    rev = lambda cb: pl.BlockSpec((t, w), functools.partial(lambda i, cb: (nb - 1 - i, cb), cb=cb))
    halo = lambda cb: pl.BlockSpec(
        (HALO, w), functools.partial(lambda i, cb: (jnp.maximum((nb - 1 - i) * hb - 1, 0), cb), cb=cb))
    gw = (LRU_BLOCKS, LRU_BLOCK, LRU_BLOCK)
    return pl.pallas_call(
        body, name="rglru_bwd", grid=(nb,),
        in_specs=[rev(0), halo(0), rev(0), halo(0), rev(0), full((8, w)), full(gw), full(gw)],
        out_specs=[rev(0), full(gw), full(gw), full((8, w))],
        out_shape=[jax.ShapeDtypeStruct((s, w), BF16), jax.ShapeDtypeStruct(gw, F32),
                   jax.ShapeDtypeStruct(gw, F32), jax.ShapeDtypeStruct((8, w), F32)],
        scratch_shapes=[pltpu.VMEM((t + HALO, w), F32), pltpu.VMEM((t + HALO, w), F32),
                        pltpu.VMEM((t + HALO, w), F32), pltpu.VMEM((t, w), F32), pltpu.VMEM((t, w), F32),
                        pltpu.VMEM((1, w), F32)],
        compiler_params=_cparams(("arbitrary",)),
    )(h0a, h0a, h, h, dh, pvec, wa16, wx16)


SB_T = 256


def _split16(x):
    hi = x.astype(BF16)
    lo = (x - hi.astype(F32)).astype(BF16)
    return jnp.concatenate([hi, lo], axis=0)


def _sb_tile(q, k, scale, tri, run, causal):
    tq = q.shape[0]
    z = _dot(q, k, NT) * scale
    log1mb = -(jnp.maximum(z, 0.0) + jnp.log(1.0 + jnp.exp(-jnp.abs(z))))
    if causal is not None:
        log1mb = jnp.where(causal, log1mb, 0.0)
    cs = _dot(_split16(log1mb), tri, NN)
    cs = cs[:tq] + cs[tq:]
    wgt = jnp.exp(z + cs + run)
    if causal is not None:
        wgt = jnp.where(causal, wgt, 0.0)
    return z, wgt, run + cs[:, 0:1]


SB_DEAD = -105.0


def _sb_alive(run):
    return (jnp.max(run) > SB_DEAD).astype(jnp.int32)


def _sb_more(carry):
    return (carry[0] >= 0) & (carry[1] > 0)


def _sb_fwd(qkv, bg):
    s = qkv.shape[0]
    t = min(SB_T, s)
    nq = s // t
    dh = SB_HEAD_DIM
    scale = 1.0 / math.sqrt(dh)

    def body(q_ref, k_ref, v_ref, bg_ref, o_ref, y_ref):
        iq = pl.program_id(1)
        q = q_ref[...]
        row = lax.broadcasted_iota(jnp.int32, (t, t), 0)
        col = lax.broadcasted_iota(jnp.int32, (t, t), 1)
        tri = jnp.where(row >= col, 1.0, 0.0).astype(BF16)
        causal = col < row

        def tile(kb, run, acc, mask):
            off = pl.multiple_of(kb * t, t)
            _, wgt, run = _sb_tile(q, k_ref[pl.ds(off, t), :], scale, tri, run, mask)
            return run, acc + _dot(wgt.astype(BF16), v_ref[pl.ds(off, t), :], NN)

        run, acc = tile(iq, jnp.zeros((t, 1), F32), jnp.zeros((t, dh), F32), causal)

        def loop(carry):
            kb, _, run, acc = carry
            run, acc = tile(kb, run, acc, None)
            return kb - 1, _sb_alive(run), run, acc

        _, _, run, acc = lax.while_loop(_sb_more, loop, (iq - 1, jnp.int32(1), run, acc))
        o_ref[...] = acc
        g = bg_ref[...]
        y_ref[...] = (acc * (g * _sigmoid(g))).astype(BF16)

    blk = lambda off: pl.BlockSpec((t, dh), functools.partial(lambda h, i, off: (i, h + off), off=off))
    col = lambda off: pl.BlockSpec((s, dh), functools.partial(lambda h, i, off: (0, h + off), off=off))
    return pl.pallas_call(
        body, name="sb_fwd", grid=(SB_HEADS, nq),
        in_specs=[blk(0), col(SB_HEADS), col(2 * SB_HEADS), blk(0)],
        out_specs=[blk(0), blk(0)],
        out_shape=[jax.ShapeDtypeStruct((s, SB_HEADS * dh), F32), jax.ShapeDtypeStruct((s, SB_HEADS * dh), BF16)],
        compiler_params=_cparams(("parallel", "arbitrary")),
    )(qkv, qkv, qkv, bg)


def _sb_bwd(qkv, dob):
    s = qkv.shape[0]
    t = min(SB_T, s)
    nq = s // t
    dh = SB_HEAD_DIM
    scale = 1.0 / math.sqrt(dh)

    def body(q_ref, k_ref, v_ref, do_ref, dq_ref, dk_ref, dv_ref, e_ref, b_ref):
        iq = pl.program_id(1)

        @pl.when(iq == 0)
        def _():
            dk_ref[...] = jnp.zeros_like(dk_ref)
            dv_ref[...] = jnp.zeros_like(dv_ref)

        q = q_ref[...]
        do = do_ref[...]
        row = lax.broadcasted_iota(jnp.int32, (t, t), 0)
        col = lax.broadcasted_iota(jnp.int32, (t, t), 1)
        tri = jnp.where(row >= col, 1.0, 0.0).astype(BF16)
        tri_x = jnp.where(row < col, 1.0, 0.0).astype(BF16)
        causal = col < row

        def sweep1(kb, run, mask):
            off = pl.multiple_of(kb * t, t)
            v = v_ref[pl.ds(off, t), :]
            z, wgt, run = _sb_tile(q, k_ref[pl.ds(off, t), :], scale, tri, run, mask)
            e_ref[kb] = wgt * _dot(do, v, NT)
            b_ref[kb] = _sigmoid(z)
            dv_ref[pl.ds(off, t), :] += _dot(wgt.astype(BF16), do, TN)
            return run

        run = sweep1(iq, jnp.zeros((t, 1), F32), causal)

        def loop1(carry):
            run = sweep1(carry[0], carry[2], None)
            return carry[0] - 1, _sb_alive(run), run

        first = lax.while_loop(_sb_more, loop1, (iq - 1, jnp.int32(1), run))[0] + 1

        def sweep2(kb, pre, dq, mask):
            off = pl.multiple_of(kb * t, t)
            e = e_ref[kb]
            beta = b_ref[kb]
            ps = _dot(_split16(e), tri_x, NN)
            ps = ps[:t] + ps[t:]
            dz = e * (1.0 - beta) - beta * (ps + pre)
            if mask is not None:
                dz = jnp.where(mask, dz, 0.0)
            dz16 = (dz * scale).astype(BF16)
            dk_ref[pl.ds(off, t), :] += _dot(dz16, q, TN)
            dq = dq + _dot(dz16, k_ref[pl.ds(off, t), :], NN)
            return pre + ps[:, t - 1:t] + e[:, t - 1:t], dq

        pre, dq = lax.fori_loop(first, iq, lambda kb, c: sweep2(kb, c[0], c[1], None),
                                (jnp.zeros((t, 1), F32), jnp.zeros((t, dh), F32)))
        _, dq = sweep2(iq, pre, dq, causal)
        dq_ref[...] = dq.astype(BF16)

    blk = lambda off: pl.BlockSpec((t, dh), functools.partial(lambda h, i, off: (i, h + off), off=off))
    col = lambda off: pl.BlockSpec((s, dh), functools.partial(lambda h, i, off: (0, h + off), off=off))
    wide = SB_HEADS * dh
    return pl.pallas_call(
        body, name="sb_bwd", grid=(SB_HEADS, nq),
        in_specs=[blk(0), col(SB_HEADS), col(2 * SB_HEADS), blk(0)],
        out_specs=[blk(0), col(0), col(0)],
        out_shape=[jax.ShapeDtypeStruct((s, wide), BF16), jax.ShapeDtypeStruct((s, wide), F32),
                   jax.ShapeDtypeStruct((s, wide), F32)],
        scratch_shapes=[pltpu.VMEM((nq, t, t), F32), pltpu.VMEM((nq, t, t), F32)],
        compiler_params=_cparams(("parallel", "arbitrary")),
    )(qkv, qkv, qkv, dob)


def _alibi_slope(h):
    return float(2.0 ** (-8.0 * (h + 1) / C_HEADS))


GROUP_ROWS = C_GROUP * WINDOW


def _swa_group_consts(n, c, sink_ref):
    qb = WINDOW
    head = lax.shift_right_logical(lax.broadcasted_iota(jnp.int32, (GROUP_ROWS, 1), 0), qb.bit_length() - 1)
    slope = jnp.zeros((GROUP_ROWS, 1), F32)
    sink = jnp.zeros((GROUP_ROWS, 1), F32)
    for hh in range(C_GROUP):
        slope = jnp.where(head == hh, _alibi_slope(c * C_GROUP + hh), slope)
        sink = jnp.where(head == hh, sink_ref[c * C_GROUP + hh], sink)
    i = lax.broadcasted_iota(jnp.int32, (GROUP_ROWS, 2 * qb), 0) & (qb - 1)
    j = lax.broadcasted_iota(jnp.int32, (GROUP_ROWS, 2 * qb), 1)
    d = i - j + qb
    valid = (d >= 0) & (d < WINDOW) & ((j >= qb) | (n > 0))
    return slope, sink, d.astype(F32), valid


def _swa_probs(qg, kw, sink, slope, dist, valid, scale):
    sc = _dot(qg, kw, NT) * scale - slope * dist
    sc = jnp.where(valid, sc, -1e30)
    m = jnp.maximum(jnp.max(sc, axis=-1, keepdims=True), sink)
    p = jnp.exp(sc - m)
    ps = jnp.exp(sink - m)
    inv = 1.0 / (jnp.sum(p, axis=-1, keepdims=True) + ps)
    return p * inv, ps * inv


def _stack_heads(x, c):
    hd = C_HEAD_DIM
    return jnp.concatenate([x[:, (c * C_GROUP + hh) * hd:(c * C_GROUP + hh + 1) * hd] for hh in range(C_GROUP)], axis=0)


def _swa_fwd(q16, gate, kvp, sinks):
    s = q16.shape[0]
    qb = WINDOW
    hd = C_HEAD_DIM
    scale = 1.0 / math.sqrt(hd)
    kvw = C_KV_HEADS * hd

    def body(sink_ref, q_ref, g_ref, kp_ref, ko_ref, o_ref, y_ref):
        n = pl.program_id(0)
        q = q_ref[...]
        kv = jnp.concatenate([kp_ref[...], ko_ref[...]], axis=0)
        for c in range(C_KV_HEADS):
            slope, sink, dist, valid = _swa_group_consts(n, c, sink_ref)
            kw = kv[:, c * hd:(c + 1) * hd]
            vw = kv[:, kvw + c * hd:kvw + (c + 1) * hd]
            p, _ = _swa_probs(_stack_heads(q, c), kw, sink, slope, dist, valid, scale)
            og = _dot(p.astype(BF16), vw, NN)
            for hh in range(C_GROUP):
                h = c * C_GROUP + hh
                o_ref[:, h * hd:(h + 1) * hd] = og[hh * qb:(hh + 1) * qb, :]
        g = g_ref[...]
        y_ref[...] = (o_ref[...] * (g * _sigmoid(g))).astype(BF16)

    wide = C_HEADS * hd
    return pl.pallas_call(
        body, name="swa_fwd", grid=(s // qb,),
        in_specs=[pl.BlockSpec(memory_space=pltpu.SMEM),
                  pl.BlockSpec((qb, wide), lambda n: (n, 0)), pl.BlockSpec((qb, wide), lambda n: (n, 0)),
                  pl.BlockSpec((qb, 2 * kvw), lambda n: (n, 0)), pl.BlockSpec((qb, 2 * kvw), lambda n: (n + 1, 0))],
        out_specs=[pl.BlockSpec((qb, wide), lambda n: (n, 0)), pl.BlockSpec((qb, wide), lambda n: (n, 0))],
        out_shape=[jax.ShapeDtypeStruct((s, wide), F32), jax.ShapeDtypeStruct((s, wide), BF16)],
        compiler_params=_cparams(("arbitrary",)),
    )(sinks, q16, gate, kvp, kvp)


def _swa_bwd(q16, do16, kvp, sinks):
    s = q16.shape[0]
    qb = WINDOW
    hd = C_HEAD_DIM
    scale = 1.0 / math.sqrt(hd)
    kvw = C_KV_HEADS * hd
    nblk = s // qb

    def body(sink_ref, q_ref, do_ref, kp_ref, ko_ref, dq_ref, dkv_ref, ds_ref, sacc_ref):
        n = pl.program_id(0)

        @pl.when(n == 0)
        def _():
            dkv_ref[...] = jnp.zeros_like(dkv_ref)
            sacc_ref[...] = jnp.zeros_like(sacc_ref)

        q = q_ref[...]
        do = do_ref[...]
        kv = jnp.concatenate([kp_ref[...], ko_ref[...]], axis=0)
        off = pl.multiple_of(n * qb, qb)
        for c in range(C_KV_HEADS):
            slope, sink, dist, valid = _swa_group_consts(n, c, sink_ref)
            kw = kv[:, c * hd:(c + 1) * hd]
            vw = kv[:, kvw + c * hd:kvw + (c + 1) * hd]
            qg = _stack_heads(q, c)
            dog = _stack_heads(do, c)
            p, ps = _swa_probs(qg, kw, sink, slope, dist, valid, scale)
            dp = _dot(dog, vw, NT)
            dd = jnp.sum(p * dp, axis=-1, keepdims=True)
            ds16 = (p * (dp - dd) * scale).astype(BF16)
            sacc_ref[c] += ps * dd
            dqg = _dot(ds16, kw, NN).astype(BF16)
            for hh in range(C_GROUP):
                h = c * C_GROUP + hh
                dq_ref[:, h * hd:(h + 1) * hd] = dqg[hh * qb:(hh + 1) * qb, :]
            dkv_ref[pl.ds(off, 2 * qb), c * hd:(c + 1) * hd] += _dot(ds16, qg, TN)
            dkv_ref[pl.ds(off, 2 * qb), kvw + c * hd:kvw + (c + 1) * hd] += _dot(p.astype(BF16), dog, TN)

        @pl.when(n == nblk - 1)
        def _():
            lane = lax.broadcasted_iota(jnp.int32, (1, 128), 1)
            row = jnp.zeros((1, 128), F32)
            for c in range(C_KV_HEADS):
                for hh in range(C_GROUP):
                    tot = jnp.sum(sacc_ref[c, pl.ds(hh * qb, qb), :], axis=0, keepdims=True)
                    row = jnp.where(lane == c * C_GROUP + hh, -tot, row)
            ds_ref[...] = row

    wide = C_HEADS * hd
    return pl.pallas_call(
        body, name="swa_bwd", grid=(nblk,),
        in_specs=[pl.BlockSpec(memory_space=pltpu.SMEM),
                  pl.BlockSpec((qb, wide), lambda n: (n, 0)), pl.BlockSpec((qb, wide), lambda n: (n, 0)),
                  pl.BlockSpec((qb, 2 * kvw), lambda n: (n, 0)), pl.BlockSpec((qb, 2 * kvw), lambda n: (n + 1, 0))],
        out_specs=[pl.BlockSpec((qb, wide), lambda n: (n, 0)),
                   pl.BlockSpec((s + qb, 2 * kvw), lambda n: (0, 0)), pl.BlockSpec((1, 128), lambda n: (0, 0))],
        out_shape=[jax.ShapeDtypeStruct((s, wide), BF16), jax.ShapeDtypeStruct((s + qb, 2 * kvw), F32),
                   jax.ShapeDtypeStruct((1, 128), F32)],
        scratch_shapes=[pltpu.VMEM((C_KV_HEADS, GROUP_ROWS, 1), F32)],
        compiler_params=_cparams(("arbitrary",)),
    )(sinks, q16, do16, kvp, kvp)


def _adamw(parts, w, m, v, *, name, tr=496):
    npart, r, c = parts.shape
    tr = min(tr, r)
    assert r % tr == 0 and tr % 16 == 0
    c1 = 1.0 / (1.0 - ADAM_B1 ** ADAM_STEP)
    c2 = 1.0 / (1.0 - ADAM_B2 ** ADAM_STEP)

    def body(p_ref, w_ref, m_ref, v_ref, g_ref, d_ref, nm_ref, nv_ref):
        g = p_ref[0].astype(F32)
        for j in range(1, npart):
            g = g + p_ref[j].astype(F32)
        nm = ADAM_B1 * m_ref[...] + (1.0 - ADAM_B1) * g
        nv = ADAM_B2 * v_ref[...] + (1.0 - ADAM_B2) * (g * g)
        g_ref[...] = g
        nm_ref[...] = nm
        nv_ref[...] = nv
        d_ref[...] = -ADAM_LR * ((nm * c1) / (jnp.sqrt(nv * c2) + ADAM_EPS) + ADAM_WD * w_ref[...])

    spec = pl.BlockSpec((tr, c), lambda i: (i, 0))
    return pl.pallas_call(
        body, name=name, grid=(r // tr,),
        in_specs=[pl.BlockSpec((npart, tr, c), lambda i: (0, i, 0)), spec, spec, spec],
        out_specs=[spec] * 4, out_shape=[jax.ShapeDtypeStruct((r, c), F32)] * 4,
        compiler_params=_cparams(("parallel",)),
    )(parts, w, m, v)


def _exchange(bufs, gather, *, name):
    nb = len(bufs)

    def body(*refs):
        _exchange_start(refs[:nb], refs[nb:2 * nb], gather, refs[2 * nb:])
        _exchange_wait(refs[:nb], refs[nb:2 * nb], gather, refs[2 * nb:])

    hbm = pl.BlockSpec(memory_space=pl.ANY)
    return pl.pallas_call(
        body, name=name, in_specs=[hbm] * nb, out_specs=[hbm] * nb, out_shape=_exchange_out_shapes(bufs, gather),
        scratch_shapes=_exchange_sems(nb),
        compiler_params=pltpu.CompilerParams(has_side_effects=True),
    )(*bufs)


def _exchange_out_shapes(bufs, gather):
    return [jax.ShapeDtypeStruct((N_DEV,) + tuple(b.shape if g else b.shape[1:]), b.dtype) for b, g in zip(bufs, gather)]


def _exchange_sems(nb):
    return [pltpu.SemaphoreType.DMA((nb, N_DEV - 1)), pltpu.SemaphoreType.DMA((nb, N_DEV - 1)),
            pltpu.SemaphoreType.DMA((nb,))]


def _exchange_copies(src, dst, gather, sems):
    send_sems, recv_sems, local_sems = sems
    x, y, c = lax.axis_index("x"), lax.axis_index("y"), lax.axis_index("c")
    me = 4 * x + 2 * y + c
    local, sends, recvs = [], [], []
    for b in range(len(src)):
        mine = src[b] if gather[b] else src[b].at[me]
        local.append(pltpu.make_async_copy(mine, dst[b].at[me], local_sems.at[b]))
        for k in range(1, N_DEV):
            px, py, pc = x ^ (k >> 2), y ^ ((k >> 1) & 1), c ^ (k & 1)
            peer = 4 * px + 2 * py + pc
            pair = dict(send_sem=send_sems.at[b, k - 1], recv_sem=recv_sems.at[b, k - 1],
                        device_id_type=pl.DeviceIdType.MESH)
            sends.append(pltpu.make_async_remote_copy(
                src_ref=src[b] if gather[b] else src[b].at[peer], dst_ref=dst[b].at[me], device_id=(px, py, pc), **pair))
            recvs.append(pltpu.make_async_remote_copy(
                src_ref=mine, dst_ref=dst[b].at[peer], device_id=(x, y, c), **pair))
    return local, sends, recvs


def _exchange_start(src, dst, gather, sems):
    local, sends, _ = _exchange_copies(src, dst, gather, sems)
    for cp in local + sends:
        cp.start()


def _exchange_wait(src, dst, gather, sems):
    local, sends, recvs = _exchange_copies(src, dst, gather, sems)
    for cp in sends:
        cp.wait_send()
    for cp in recvs:
        cp.wait_recv()
    for cp in local:
        cp.wait()


R_WIN_E, R_WOUT_E, R_WIN_O, R_WOUT_O, R_SMALL, R_REPL = 768, 256, 288, 128, 16, 48
REPL_ROWS = N_DEV * R_REPL
REPL_LEN = REPL_ROWS * D_MODEL


def _rows(a):
    return a.reshape(-1, D_MODEL)


def _small_block(parts):
    flat = jnp.concatenate(parts, axis=-1)
    lead = flat.ndim - 1
    return jnp.pad(flat[..., None, :], [(0, 0)] * lead + [(0, R_SMALL - 1), (0, D_MODEL - flat.shape[-1])])


def _pack_groups(w_in_e, w_out_e, w_in_o, w_out_o, conv_w, ln_g, ln_b):
    g1 = jnp.concatenate([_rows(w_in_o), _rows(w_out_o), _small_block([ln_g.reshape(128), ln_b.reshape(128)])], axis=0)
    g3 = jnp.concatenate([_rows(w_in_e), _small_block([conv_w.reshape(512)]), jnp.zeros((R_REPL, D_MODEL), F32)], axis=0)
    return g1, _rows(w_out_e), g3


def _unpack_groups(g1, g2, g3):
    w_in_o = g1[0:R_WIN_O].reshape(1, D_MODEL, 288)
    w_out_o = g1[R_WIN_O:R_WIN_O + R_WOUT_O].reshape(1, 128, D_MODEL)
    small1 = g1[R_WIN_O + R_WOUT_O]
    small3 = g3[R_WIN_E]
    return (g3[0:R_WIN_E].reshape(1, D_MODEL, 768), g2.reshape(1, 256, D_MODEL), w_in_o, w_out_o,
            small3[0:512].reshape(1, 4, 128), small1[0:128].reshape(1, 128), small1[128:256].reshape(1, 128))


REPL_SHAPES = ((1, 8, 128, 128), (1, 8, 128, 128), (1, 1024), (1, 1024), (1, 1024), (1, 1024), (1, 1024), (1, 1024),
               (1, 16))


def _pack_repl(parts):
    flat = jnp.concatenate([p.reshape(-1) for p in parts])
    return jnp.concatenate([flat, jnp.zeros((REPL_LEN - flat.shape[0],), F32)]).reshape(REPL_ROWS, D_MODEL)


def _unpack_repl(p):
    flat = p.reshape(-1)
    out, o = [], 0
    for shp in REPL_SHAPES:
        n = math.prod(shp)
        out.append(flat[o:o + n].reshape(shp))
        o += n
    return out


def kernel(x, e_w_in, e_conv_w, e_conv_b, e_w_gate_a, e_b_gate_a, e_w_gate_x, e_b_gate_x, e_lru_lambda, e_w_out, e_ln_g, e_ln_b, o_w_in, o_sinks, o_w_out, o_ln_g, o_ln_b, loss_target, m_e_w_in, m_e_conv_w, m_e_conv_b, m_e_w_gate_a, m_e_b_gate_a, m_e_w_gate_x, m_e_b_gate_x, m_e_lru_lambda, m_e_w_out, m_e_ln_g, m_e_ln_b, m_o_w_in, m_o_sinks, m_o_w_out, m_o_ln_g, m_o_ln_b, v_e_w_in, v_e_conv_w, v_e_conv_b, v_e_w_gate_a, v_e_b_gate_a, v_e_w_gate_x, v_e_b_gate_x, v_e_lru_lambda, v_e_w_out, v_e_ln_g, v_e_ln_b, v_o_w_in, v_o_sinks, v_o_w_out, v_o_ln_g, v_o_ln_b):
    d = D_MODEL
    x0 = x[0]
    target = loss_target[0]
    s = x0.shape[0]

    spack = _small_block([e_conv_w.reshape(512), o_ln_g.reshape(128), o_ln_b.reshape(128)])
    wall_in_e, sall = _exchange([_rows(e_w_in).astype(BF16), spack], [True, True], name="gather_w_in")
    wi_e = wall_in_e.reshape(N_DEV, d, 768).transpose(1, 0, 2).reshape(d, 6 * d)
    conv_w = sall[:, 0, 0:512].reshape(N_DEV, 4, 128).transpose(1, 0, 2).reshape(4, d)
    ln_g_o = sall[:, 0, 512:640].reshape(1, d)
    ln_b_o = sall[:, 0, 640:768].reshape(1, d)
    pvec = jnp.concatenate([conv_w, e_conv_b, e_b_gate_a, e_b_gate_x, e_lru_lambda], axis=0)
    wa16 = e_w_gate_a[0].astype(BF16)
    wx16 = e_w_gate_x[0].astype(BF16)
    sinks = o_sinks[0]

    x0_16 = x0.astype(BF16)
    h0a, wall_out_e = _matmul(x0_16, wi_e, mode="nn", n_out=2 * d, b_off=0, name="l0_in_a",
                              comm=([_rows(e_w_out).astype(BF16)], [True]))
    wpack_o = jnp.concatenate([_rows(o_w_in), _rows(o_w_out)], axis=0).astype(BF16)
    qkv, wall_o = _matmul(x0_16, wi_e, mode="nn", n_out=3 * d, b_off=2 * d, out_dtype=BF16, name="l0_in_qkv",
                          comm=([wpack_o], [True]))
    bg = _matmul(x0_16, wi_e, mode="nn", n_out=d, b_off=5 * d, name="l0_in_bg")
    wo_e = wall_out_e.reshape(2 * d, d)
    wi_o = wall_o[:, 0:R_WIN_O].reshape(N_DEV, d, 288).transpose(1, 0, 2).reshape(d, 2304)
    wi_o = jnp.concatenate([wi_o[:, :1024], wi_o[:, 1280:], wi_o[:, 1024:1280]], axis=1)
    wo_o = wall_o[:, R_WIN_O:R_WIN_O + R_WOUT_O].reshape(d, d)
    ya, hst = _rglru_fwd(h0a, pvec, wa16, wx16)
    ob, yb = _sb_fwd(qkv, bg)
    yab = jnp.concatenate([ya, yb], axis=1)
    y0 = _matmul(yab, wo_e, mode="nn", tk=2048, name="l0_out")
    z0, x1, x1_16 = _ln_fwd(x0, y0, e_ln_g, e_ln_b, name="l0_ln")

    q1 = _matmul(x1_16, wi_o, mode="nn", n_out=d, b_off=0, out_dtype=BF16, name="l1_in_q")
    g1 = _matmul(x1_16, wi_o, mode="nn", n_out=d, b_off=d, name="l1_in_g")
    kv1 = _matmul(x1_16, wi_o, mode="nn", n_out=256, b_off=2 * d, tn=256, out_dtype=BF16, name="l1_in_kv")
    kvp = jnp.concatenate([jnp.zeros((WINDOW, 256), BF16), kv1], axis=0)
    o1, y1 = _swa_fwd(q1, g1, kvp, sinks)
    yy1 = _matmul(y1, wo_o, mode="nn", name="l1_out")
    dz1, loss_cols, dg_o, db_o = _ln_loss_bwd(x1, yy1, target, ln_g_o, ln_b_o, name="l1_ln_loss")
    loss_hi = jnp.sum(loss_cols).astype(BF16).astype(F32)
    loss_terms = jnp.stack([loss_hi, jnp.sum(loss_cols) - loss_hi]).reshape(1, 2)

    dz1_16 = dz1.astype(BF16)
    dwo_o = _matmul(y1, dz1_16, mode="tn", name="l1_dwout")
    dy1 = _matmul(dz1_16, wo_o, mode="nt", name="l1_dy")
    do1, dg1 = _gate_bwd((dy1, 0), o1, g1, 0, do_dtype=BF16, name="l1_gate_bwd")
    dq1, dkvp, dsink = _swa_bwd(q1, do1, kvp, sinks)
    dh1 = jnp.concatenate([dq1, dg1, dkvp[WINDOW:].astype(BF16)], axis=1)
    dwi_o = _matmul(x1_16, dh1, mode="tn", tn=1152, name="l1_dwin")
    dwi_o = jnp.concatenate([dwi_o[:, :1024], dwi_o[:, 2048:], dwi_o[:, 1024:2048]], axis=1)
    seg1 = jnp.concatenate([
        dwi_o.reshape(d, N_DEV, 288).transpose(1, 0, 2).reshape(N_DEV, R_WIN_O, d),
        dwo_o.reshape(N_DEV, R_WOUT_O, d),
        _small_block([dg_o.reshape(N_DEV, 128), db_o.reshape(N_DEV, 128)])], axis=1).astype(BF16)
    dx1, parts1 = _matmul(dh1, wi_o, mode="nt", tm=512, tk=2304, res=dz1, alpha=ALPHA, name="l1_dx",
                          comm=([seg1], [False]))

    dz0, dg_e, db_e = _ln_bwd_call(z0, dx1, e_ln_g, name="l0_ln_bwd")
    dz0_16 = dz0.astype(BF16)
    dwo_e = _matmul(yab, dz0_16, mode="tn", name="l0_dwout")
    dyab, parts2 = _matmul(dz0_16, wo_e, mode="nt", name="l0_dy",
                           comm=([dwo_e.reshape(N_DEV, R_WOUT_E, d).astype(BF16)], [False]))
    dhst, dag = _gate_bwd((dyab, 0), hst, h0a, 1, do_dtype=F32, name="l0_gate_bwd_a")
    dob, dbg = _gate_bwd((dyab, 1), ob, bg, 0, do_dtype=BF16, name="l0_gate_bwd_b")
    dq0, dk0, dv0 = _sb_bwd(qkv, dob)
    dax, dwa, dwx, dpv = _rglru_bwd(h0a, hst, dhst, pvec, wa16, wx16)
    dh0 = jnp.concatenate([dax, dag, dq0, dk0.astype(BF16), dv0.astype(BF16), dbg], axis=1)
    dwi_e = _matmul(x0_16, dh0, mode="tn", name="l0_dwin")
    repl = _pack_repl([dwa, dwx, dpv[4:5], dpv[5:6], dpv[6:7], dpv[7:8], dg_e, db_e, dsink[:, :C_HEADS], loss_terms])
    seg3 = jnp.concatenate([
        dwi_e.reshape(d, N_DEV, 768).transpose(1, 0, 2).reshape(N_DEV, R_WIN_E, d),
        _small_block([dpv[0:4].reshape(4, N_DEV, 128).transpose(1, 0, 2).reshape(N_DEV, 512)]),
        repl.reshape(N_DEV, R_REPL, d)], axis=1).astype(BF16)
    grad_x, parts3 = _matmul(dh0, wi_e, mode="nt", tk=2048, res=dz0, alpha=ALPHA, name="l0_dx",
                             comm=([seg3], [False]))

    w_g = _pack_groups(e_w_in, e_w_out, o_w_in, o_w_out, e_conv_w, o_ln_g, o_ln_b)
    m_g = _pack_groups(m_e_w_in, m_e_w_out, m_o_w_in, m_o_w_out, m_e_conv_w, m_o_ln_g, m_o_ln_b)
    v_g = _pack_groups(v_e_w_in, v_e_w_out, v_o_w_in, v_o_w_out, v_e_conv_w, v_o_ln_g, v_o_ln_b)
    res1 = _adamw(parts1, w_g[0], m_g[0], v_g[0], name="adamw_layer1")
    res2 = _adamw(parts2, w_g[1], m_g[1], v_g[1], name="adamw_w_out_e")
    res3 = _adamw(parts3, w_g[2], m_g[2], v_g[2], tr=416, name="adamw_w_in_e")

    (g_repl,) = _exchange([res3[0][R_WIN_E + R_SMALL:]], [True], name="gather_repl_grads")
    g_repl = g_repl.reshape(1, REPL_ROWS, d)
    w_r = _pack_repl([e_w_gate_a, e_w_gate_x, e_conv_b, e_b_gate_a, e_b_gate_x, e_lru_lambda, e_ln_g, e_ln_b, o_sinks])
    m_r = _pack_repl([m_e_w_gate_a, m_e_w_gate_x, m_e_conv_b, m_e_b_gate_a, m_e_b_gate_x, m_e_lru_lambda, m_e_ln_g,
                      m_e_ln_b, m_o_sinks])
    v_r = _pack_repl([v_e_w_gate_a, v_e_w_gate_x, v_e_conv_b, v_e_b_gate_a, v_e_b_gate_x, v_e_lru_lambda, v_e_ln_g,
                      v_e_ln_b, v_o_sinks])
    g_r, d_r, nm_r, nv_r = _adamw(g_repl, w_r, m_r, v_r, name="adamw_replicated")
    loss_at = sum(math.prod(shp) for shp in REPL_SHAPES)
    loss = g_r[loss_at // d, loss_at % d] + g_r[(loss_at + 1) // d, (loss_at + 1) % d]

    def assemble(i, rp):
        w_in_e, w_out_e, w_in_o, w_out_o, cw, lg_o, lb_o = _unpack_groups(res1[i], res2[i], res3[i])
        w_a, w_x, cb, b_a, b_x, lam, lg_e, lb_e, snk = _unpack_repl(rp)
        return [w_in_e, cw, cb, w_a, b_a, w_x, b_x, lam, w_out_e, lg_e, lb_e, w_in_o, snk, w_out_o, lg_o, lb_o]

    return (loss, grad_x[None], *assemble(0, g_r), *assemble(1, d_r), *assemble(2, nm_r), *assemble(3, nv_r))
```

```python
import functools
import math

import jax
import jax.numpy as jnp
from jax import lax
from jax.experimental import pallas as pl
from jax.experimental.pallas import tpu as pltpu

F32 = jnp.float32
BF16 = jnp.bfloat16

N_DEV = 8
D_MODEL = 1024
LRU_BLOCKS = 8
LRU_BLOCK = 128
LRU_C = 8.0
SB_HEADS = 8
SB_HEAD_DIM = 128
C_HEADS = 16
C_KV_HEADS = 2
C_GROUP = 8
C_HEAD_DIM = 64
WINDOW = 128
DEPTH = 2
ALPHA = float((2 * DEPTH) ** 0.25)
LN_EPS = 1e-5
ADAM_LR = 0.001
ADAM_B1 = 0.9
ADAM_B2 = 0.999
ADAM_EPS = 1e-08
ADAM_WD = 0.01
ADAM_STEP = 10

VMEM_LIMIT = 56 * 1024 * 1024

NN = ((1,), (0,))
NT = ((1,), (1,))
TN = ((0,), (0,))


def _dot(a, b, dims):
    return lax.dot_general(a, b, (dims, ((), ())), preferred_element_type=F32)


def _sigmoid(x):
    return 1.0 / (1.0 + jnp.exp(-x))


def _cparams(sem, vmem=VMEM_LIMIT):
    return pltpu.CompilerParams(dimension_semantics=sem, vmem_limit_bytes=vmem)


def _matmul(a, b, *, mode, n_out=None, b_off=0, out_dtype=F32, res=None, alpha=1.0,
            tm=1024, tn=1024, tk=1024, comm=None, name):
    if mode == "nn":
        m, k = a.shape
    elif mode == "nt":
        m, k = a.shape
    else:
        k, m = a.shape
    n = n_out if n_out is not None else (b.shape[0] if mode == "nt" else b.shape[1])
    tm, tn, tk = min(tm, m), min(tn, n), min(tk, k)
    assert m % tm == 0 and n % tn == 0 and k % tk == 0 and b_off % tn == 0
    grid = (m // tm, n // tn, k // tk)
    nk = grid[2]
    jo = b_off // tn
    dims = {"nn": NN, "nt": NT, "tn": TN}[mode]
    n_in = 2 + (res is not None)
    nc = len(comm[0]) if comm else 0

    def body(*refs):
        a_ref, b_ref = refs[:2]
        r_ref = refs[2] if res is not None else None
        o_ref = refs[n_in + nc]
        if comm:
            step = [pl.program_id(ax) for ax in range(3)]
            c_src = refs[n_in:n_in + nc]
            c_dst = refs[n_in + nc + 1:n_in + 2 * nc + 1]
            sems = refs[len(refs) - 3:]

            @pl.when((step[0] == 0) & (step[1] == 0) & (step[2] == 0))
            def _():
                _exchange_start(c_src, c_dst, comm[1], sems)

        part = _dot(a_ref[...].astype(BF16), b_ref[...].astype(BF16), dims)

        def finish(out):
            if r_ref is not None:
                out = out + alpha * r_ref[...]
            o_ref[...] = out.astype(out_dtype)

        if nk == 1:
            finish(part)
        else:
            acc_ref = refs[n_in + 2 * nc + 1]
            kk = pl.program_id(2)

            @pl.when(kk == 0)
            def _():
                acc_ref[...] = part

            @pl.when(kk > 0)
            def _():
                acc_ref[...] += part

            @pl.when(kk == nk - 1)
            def _():
                finish(acc_ref[...])

        if comm:
            @pl.when((step[0] == grid[0] - 1) & (step[1] == grid[1] - 1) & (step[2] == grid[2] - 1))
            def _():
                _exchange_wait(c_src, c_dst, comm[1], sems)

    if mode == "tn":
        a_spec = pl.BlockSpec((tk, tm), lambda i, j, kk: (kk, i))
    else:
        a_spec = pl.BlockSpec((tm, tk), lambda i, j, kk: (i, kk))
    if mode == "nt":
        b_spec = pl.BlockSpec((tn, tk), lambda i, j, kk: (j + jo, kk))
    else:
        b_spec = pl.BlockSpec((tk, tn), lambda i, j, kk: (kk, j + jo))
    o_spec = pl.BlockSpec((tm, tn), lambda i, j, kk: (i, j))
    hbm = pl.BlockSpec(memory_space=pl.ANY)
    in_specs = [a_spec, b_spec] + ([o_spec] if res is not None else []) + [hbm] * nc
    args = [a, b] + ([res] if res is not None else []) + (list(comm[0]) if comm else [])
    out_shape = [jax.ShapeDtypeStruct((m, n), out_dtype)] + (_exchange_out_shapes(*comm) if comm else [])
    scratch = ([pltpu.VMEM((tm, tn), F32)] if nk > 1 else []) + (_exchange_sems(nc) if comm else [])
    if comm:
        params = pltpu.CompilerParams(dimension_semantics=("arbitrary",) * 3, vmem_limit_bytes=VMEM_LIMIT,
                                      has_side_effects=True)
    else:
        params = _cparams(("parallel", "parallel", "arbitrary"))
    outs = pl.pallas_call(
        body, name=name, grid=grid, in_specs=in_specs, out_specs=[o_spec] + [hbm] * nc, out_shape=out_shape,
        scratch_shapes=scratch, compiler_params=params,
    )(*args)
    return tuple(outs) if comm else outs[0]


def _rowwise(fn, row_ins, vec_ins, row_outs, acc_outs, *, tr=256, name):
    s = row_ins[0][0].shape[0]
    assert s % tr == 0
    n_ri, n_vi, n_ro = len(row_ins), len(vec_ins), len(row_outs)

    def body(*refs):
        rows = [r[...] for r in refs[:n_ri]]
        vecs = [r[...] for r in refs[n_ri:n_ri + n_vi]]
        o_refs = refs[n_ri + n_vi:n_ri + n_vi + n_ro]
        a_refs = refs[n_ri + n_vi + n_ro:]
        outs, accs = fn(rows, vecs)
        for o_ref, o in zip(o_refs, outs):
            o_ref[...] = o.astype(o_ref.dtype)
        if a_refs:
            i = pl.program_id(0)

            @pl.when(i == 0)
            def _():
                for a_ref, part in zip(a_refs, accs):
                    a_ref[...] = part

            @pl.when(i > 0)
            def _():
                for a_ref, part in zip(a_refs, accs):
                    a_ref[...] += part

    in_specs = [pl.BlockSpec((tr, c), functools.partial(lambda i, cb: (i, cb), cb=cb)) for (_, c, cb) in row_ins]
    in_specs += [pl.BlockSpec(v.shape, functools.partial(lambda i, nd: (0,) * nd, nd=v.ndim)) for v in vec_ins]
    out_specs = [pl.BlockSpec((tr, c), lambda i: (i, 0)) for (c, _) in row_outs]
    out_specs += [pl.BlockSpec((r, c), lambda i: (0, 0)) for (r, c) in acc_outs]
    out_shape = [jax.ShapeDtypeStruct((s, c), dt) for (c, dt) in row_outs]
    out_shape += [jax.ShapeDtypeStruct((r, c), F32) for (r, c) in acc_outs]
    return pl.pallas_call(
        body, name=name, grid=(s // tr,), in_specs=in_specs, out_specs=out_specs, out_shape=out_shape,
        compiler_params=_cparams(("arbitrary",)),
    )(*[a for (a, _, _) in row_ins], *vec_ins)


def _ln_stats(z):
    mu = jnp.mean(z, axis=-1, keepdims=True)
    zc = z - mu
    var = jnp.mean(zc * zc, axis=-1, keepdims=True)
    rstd = lax.rsqrt(var + LN_EPS)
    return zc * rstd, rstd


def _ln_bwd(dy, xhat, rstd, g):
    dxh = dy * g
    m1 = jnp.mean(dxh, axis=-1, keepdims=True)
    m2 = jnp.mean(dxh * xhat, axis=-1, keepdims=True)
    return rstd * (dxh - m1 - xhat * m2)


def _colsum(x):
    return jnp.sum(x, axis=0, keepdims=True)


def _ln_fwd(x, y, g, b, *, name):
    def fn(rows, vecs):
        z = ALPHA * rows[0] + rows[1]
        xhat, _ = _ln_stats(z)
        x1 = xhat * vecs[0] + vecs[1]
        return [z, x1, x1], []
    c = x.shape[1]
    return _rowwise(fn, [(x, c, 0), (y, c, 0)], [g, b], [(c, F32), (c, F32), (c, BF16)], [], name=name)


def _ln_loss_bwd(x, y, target, g, b, *, name):
    c = x.shape[1]

    def fn(rows, vecs):
        z = ALPHA * rows[0] + rows[1]
        xhat, rstd = _ln_stats(z)
        d = xhat * vecs[0] + vecs[1] - rows[2]
        dy = d * (1.0 / c)
        dz = _ln_bwd(dy, xhat, rstd, vecs[0])
        return [dz, dz], [_colsum(0.5 * d * d * (1.0 / c)), _colsum(dy * xhat), _colsum(dy)]
    return _rowwise(fn, [(x, c, 0), (y, c, 0), (target, c, 0)], [g, b], [(c, F32), (c, BF16)], [(1, c)] * 3, name=name)


def _ln_bwd_call(z, dy, g, *, name):
    c = z.shape[1]

    def fn(rows, vecs):
        xhat, rstd = _ln_stats(rows[0])
        dz = _ln_bwd(rows[1], xhat, rstd, vecs[0])
        return [dz, dz], [_colsum(rows[1] * xhat), _colsum(rows[1])]
    return _rowwise(fn, [(z, c, 0), (dy, c, 0)], [g], [(c, F32), (c, BF16)], [(1, c)] * 2, name=name)


def _gate_bwd(dy, o, gate, gate_cb, *, do_dtype, name):
    dy_arr, dy_cb = dy
    c = o.shape[1]

    def fn(rows, vecs):
        d, oo, gg = rows
        sg = _sigmoid(gg)
        return [d * (gg * sg), d * oo * (sg * (1.0 + gg * (1.0 - sg)))], []
    return _rowwise(fn, [(dy_arr, c, dy_cb), (o, c, 0), (gate, c, gate_cb)], [], [(c, do_dtype), (c, BF16)], [], name=name)


LRU_T = 256
HALO = 8


def _log1p(y):
    u = 1.0 + y
    return jnp.where(u == 1.0, y, jnp.log(u) * (y / (u - 1.0)))


def _lru_gates(c, wa_ref, wx_ref, pv):
    c16 = c.astype(BF16)
    pre_r = jnp.concatenate(
        [_dot(c16[:, n * LRU_BLOCK:(n + 1) * LRU_BLOCK], wa_ref[n], NN) for n in range(LRU_BLOCKS)], axis=1)
    pre_i = jnp.concatenate(
        [_dot(c16[:, n * LRU_BLOCK:(n + 1) * LRU_BLOCK], wx_ref[n], NN) for n in range(LRU_BLOCKS)], axis=1)
    r = _sigmoid(pre_r + pv[5:6])
    ig = _sigmoid(pre_i + pv[6:7])
    lam = pv[7:8]
    ls = jnp.minimum(lam, 0.0) - _log1p(jnp.exp(-jnp.abs(lam)))
    la = LRU_C * r * ls
    a = jnp.exp(la)
    a2 = a * a
    m = jnp.sqrt(jnp.tanh(-la) * (a2 + 1.0))
    return c16, r, ig, ls, la, a, a2, m


def _conv(ext_ref, x, pv, t):
    return (pv[4:5] + pv[3:4] * x + pv[2:3] * ext_ref[pl.ds(HALO - 1, t), :]
            + pv[1:2] * ext_ref[pl.ds(HALO - 2, t), :] + pv[0:1] * ext_ref[pl.ds(HALO - 3, t), :])


def _rglru_fwd(h0a, pvec, wa16, wx16):
    s = h0a.shape[0]
    w = D_MODEL
    t = min(LRU_T, s)
    assert s % t == 0

    def body(ax_ref, ag_ref, pv_ref, wa_ref, wx_ref, ya_ref, h_ref, ext_ref, a_ref, u_ref, hc_ref):
        i = pl.program_id(0)

        @pl.when(i == 0)
        def _():
            ext_ref[pl.ds(0, HALO), :] = jnp.zeros((HALO, w), F32)
            hc_ref[...] = jnp.zeros((1, w), F32)

        pv = pv_ref[...]
        ax = ax_ref[...]
        ext_ref[pl.ds(HALO, t), :] = ax
        c = _conv(ext_ref, ax, pv, t)
        ext_ref[pl.ds(0, HALO), :] = ax[t - HALO:, :]
        _, _, ig, _, _, a, _, m = _lru_gates(c, wa_ref, wx_ref, pv)
        a_ref[...] = a
        u_ref[...] = m * (ig * c)

        def step(k, h):
            h = a_ref[pl.ds(k, 1), :] * h + u_ref[pl.ds(k, 1), :]
            h_ref[pl.ds(k, 1), :] = h
            return h

        hc_ref[...] = lax.fori_loop(0, t, step, hc_ref[...], unroll=8)
        ag = ag_ref[...]
        ya_ref[...] = (h_ref[...] * (ag * _sigmoid(ag))).astype(BF16)

    full = lambda shp: pl.BlockSpec(shp, lambda i: (0,) * len(shp))
    return pl.pallas_call(
        body, name="rglru_fwd", grid=(s // t,),
        in_specs=[pl.BlockSpec((t, w), lambda i: (i, 0)), pl.BlockSpec((t, w), lambda i: (i, 1)),
                  full((8, w)), full((LRU_BLOCKS, LRU_BLOCK, LRU_BLOCK)), full((LRU_BLOCKS, LRU_BLOCK, LRU_BLOCK))],
        out_specs=[pl.BlockSpec((t, w), lambda i: (i, 0)), pl.BlockSpec((t, w), lambda i: (i, 0))],
        out_shape=[jax.ShapeDtypeStruct((s, w), BF16), jax.ShapeDtypeStruct((s, w), F32)],
        scratch_shapes=[pltpu.VMEM((t + HALO, w), F32), pltpu.VMEM((t, w), F32), pltpu.VMEM((t, w), F32),
                        pltpu.VMEM((1, w), F32)],
        compiler_params=_cparams(("arbitrary",)),
    )(h0a, h0a, pvec, wa16, wx16)


def _rglru_bwd(h0a, h, dh, pvec, wa16, wx16):
    s = h0a.shape[0]
    w = D_MODEL
    t = min(LRU_T, s)
    nb = s // t
    hb = t // HALO

    def body(ax_ref, axh_ref, h_ref, hh_ref, dh_ref, pv_ref, wa_ref, wx_ref,
             dax_ref, dwa_ref, dwx_ref, dpv_ref, ext_ref, hext_ref, dcext_ref, a_ref, g_ref, gc_ref):
        i = pl.program_id(0)
        blk = nb - 1 - i

        @pl.when(i == 0)
        def _():
            dwa_ref[...] = jnp.zeros_like(dwa_ref)
            dwx_ref[...] = jnp.zeros_like(dwx_ref)
            dpv_ref[...] = jnp.zeros_like(dpv_ref)
            gc_ref[...] = jnp.zeros((1, w), F32)
            dcext_ref[pl.ds(t, HALO), :] = jnp.zeros((HALO, w), F32)

        pv = pv_ref[...]
        ax = ax_ref[...]
        keep = (blk > 0).astype(F32)
        ext_ref[pl.ds(0, HALO), :] = axh_ref[...] * keep
        ext_ref[pl.ds(HALO, t), :] = ax
        hext_ref[pl.ds(0, HALO), :] = hh_ref[...] * keep
        hext_ref[pl.ds(HALO, t), :] = h_ref[...]
        c = _conv(ext_ref, ax, pv, t)
        c16, r, ig, ls, _, a, a2, m = _lru_gates(c, wa_ref, wx_ref, pv)
        a_ref[...] = a

        def step(k, carry):
            row = t - 1 - k
            g = dh_ref[pl.ds(row, 1), :] + carry
            g_ref[pl.ds(row, 1), :] = g
            return a_ref[pl.ds(row, 1), :] * g

        gc_ref[...] = lax.fori_loop(0, t, step, gc_ref[...], unroll=8)
        g = g_ref[...]
        hprev = hext_ref[pl.ds(HALO - 1, t), :]
        gm = g * m
        d_la = g * hprev * a - (g * ig * c) * a2 / m
        d_pr = d_la * (LRU_C * ls) * r * (1.0 - r)
        d_pi = gm * c * ig * (1.0 - ig)
        dc = gm * ig
        dpr16 = d_pr.astype(BF16)
        dpi16 = d_pi.astype(BF16)
        dc_parts = []
        for n in range(LRU_BLOCKS):
            sl = slice(n * LRU_BLOCK, (n + 1) * LRU_BLOCK)
            dwa_ref[n] += _dot(c16[:, sl], dpr16[:, sl], TN)
            dwx_ref[n] += _dot(c16[:, sl], dpi16[:, sl], TN)
            dc_parts.append(_dot(dpr16[:, sl], wa_ref[n], NT) + _dot(dpi16[:, sl], wx_ref[n], NT))
        dc = dc + jnp.concatenate(dc_parts, axis=1)
        dcext_ref[pl.ds(0, t), :] = dc
        dax = (pv[3:4] * dc + pv[2:3] * dcext_ref[pl.ds(1, t), :] + pv[1:2] * dcext_ref[pl.ds(2, t), :]
               + pv[0:1] * dcext_ref[pl.ds(3, t), :])
        dax_ref[...] = dax.astype(BF16)
        dcext_ref[pl.ds(t, HALO), :] = dc[:HALO, :]
        sums = [_colsum(dc * ext_ref[pl.ds(HALO - 3 + kk, t), :]) for kk in range(4)]
        sums += [_colsum(dc), _colsum(d_pr), _colsum(d_pi), _colsum(d_la * (LRU_C * r))]
        for kk, part in enumerate(sums):
            dpv_ref[pl.ds(kk, 1), :] += part

        @pl.when(i == nb - 1)
        def _():
            lam = pv[7:8]
            dls = dpv_ref[pl.ds(7, 1), :]
            dpv_ref[pl.ds(7, 1), :] = dls * _sigmoid(-lam)

    full = lambda shp: pl.BlockSpec(shp, lambda i: (0,) * len(shp))
    rev = lambda cb: pl.BlockSpec((t, w), functools.partial(lambda i, cb: (nb - 1 - i, cb), cb=cb))
    halo = lambda cb: pl.BlockSpec(
        (HALO, w), functools.partial(lambda i, cb: (jnp.maximum((nb - 1 - i) * hb - 1, 0), cb), cb=cb))
    gw = (LRU_BLOCKS, LRU_BLOCK, LRU_BLOCK)
    return pl.pallas_call(
        body, name="rglru_bwd", grid=(nb,),
        in_specs=[rev(0), halo(0), rev(0), halo(0), rev(0), full((8, w)), full(gw), full(gw)],
        out_specs=[rev(0), full(gw), full(gw), full((8, w))],
        out_shape=[jax.ShapeDtypeStruct((s, w), BF16), jax.ShapeDtypeStruct(gw, F32),
                   jax.ShapeDtypeStruct(gw, F32), jax.ShapeDtypeStruct((8, w), F32)],
        scratch_shapes=[pltpu.VMEM((t + HALO, w), F32), pltpu.VMEM((t + HALO, w), F32),
                        pltpu.VMEM((t + HALO, w), F32), pltpu.VMEM((t, w), F32), pltpu.VMEM((t, w), F32),
                        pltpu.VMEM((1, w), F32)],
        compiler_params=_cparams(("arbitrary",)),
    )(h0a, h0a, h, h, dh, pvec, wa16, wx16)


SB_T = 256


def _split16(x):
    hi = x.astype(BF16)
    lo = (x - hi.astype(F32)).astype(BF16)
    return jnp.concatenate([hi, lo], axis=0)


def _sb_tile(q, k, scale, tri, run, causal):
    tq = q.shape[0]
    z = _dot(q, k, NT) * scale
    log1mb = -(jnp.maximum(z, 0.0) + jnp.log(1.0 + jnp.exp(-jnp.abs(z))))
    if causal is not None:
        log1mb = jnp.where(causal, log1mb, 0.0)
    cs = _dot(_split16(log1mb), tri, NN)
    cs = cs[:tq] + cs[tq:]
    wgt = jnp.exp(z + cs + run)
    if causal is not None:
        wgt = jnp.where(causal, wgt, 0.0)
    return z, wgt, run + cs[:, 0:1]


SB_DEAD = -105.0


def _sb_alive(run):
    return (jnp.max(run) > SB_DEAD).astype(jnp.int32)


def _sb_more(carry):
    return (carry[0] >= 0) & (carry[1] > 0)


def _sb_fwd(qkv, bg):
    s = qkv.shape[0]
    t = min(SB_T, s)
    nq = s // t
    dh = SB_HEAD_DIM
    scale = 1.0 / math.sqrt(dh)

    def body(q_ref, k_ref, v_ref, bg_ref, o_ref, y_ref):
        iq = pl.program_id(1)
        q = q_ref[...]
        row = lax.broadcasted_iota(jnp.int32, (t, t), 0)
        col = lax.broadcasted_iota(jnp.int32, (t, t), 1)
        tri = jnp.where(row >= col, 1.0, 0.0).astype(BF16)
        causal = col < row

        def tile(kb, run, mask):
            off = pl.multiple_of(kb * t, t)
            _, wgt, run = _sb_tile(q, k_ref[pl.ds(off, t), :], scale, tri, run, mask)
            return run, _dot(wgt.astype(BF16), v_ref[pl.ds(off, t), :], NN)

        has_left = iq > 0
        run_d, acc_d = tile(iq, jnp.zeros((t, 1), F32), causal)
        run_l, acc_l = tile(jnp.maximum(iq - 1, 0), run_d, None)
        run = jnp.where(has_left, run_l, run_d)
        acc = acc_d + jnp.where(has_left, acc_l, 0.0)

        def loop(carry):
            kb, _, run, acc = carry
            run, part = tile(kb, run, None)
            return kb - 1, _sb_alive(run), run, acc + part

        _, _, run, acc = lax.while_loop(_sb_more, loop, (iq - 2, _sb_alive(run), run, acc))
        o_ref[...] = acc
        g = bg_ref[...]
        y_ref[...] = (acc * (g * _sigmoid(g))).astype(BF16)

    blk = lambda off: pl.BlockSpec((t, dh), functools.partial(lambda h, i, off: (i, h + off), off=off))
    col = lambda off: pl.BlockSpec((s, dh), functools.partial(lambda h, i, off: (0, h + off), off=off))
    return pl.pallas_call(
        body, name="sb_fwd", grid=(SB_HEADS, nq),
        in_specs=[blk(0), col(SB_HEADS), col(2 * SB_HEADS), blk(0)],
        out_specs=[blk(0), blk(0)],
        out_shape=[jax.ShapeDtypeStruct((s, SB_HEADS * dh), F32), jax.ShapeDtypeStruct((s, SB_HEADS * dh), BF16)],
        compiler_params=_cparams(("parallel", "arbitrary")),
    )(qkv, qkv, qkv, bg)


def _sb_bwd(qkv, dob):
    s = qkv.shape[0]
    t = min(SB_T, s)
    nq = s // t
    dh = SB_HEAD_DIM
    scale = 1.0 / math.sqrt(dh)

    def body(q_ref, k_ref, v_ref, do_ref, dq_ref, dk_ref, dv_ref, e_ref, b_ref):
        iq = pl.program_id(1)

        @pl.when(iq == 0)
        def _():
            dk_ref[...] = jnp.zeros_like(dk_ref)
            dv_ref[...] = jnp.zeros_like(dv_ref)

        q = q_ref[...]
        do = do_ref[...]
        row = lax.broadcasted_iota(jnp.int32, (t, t), 0)
        col = lax.broadcasted_iota(jnp.int32, (t, t), 1)
        tri = jnp.where(row >= col, 1.0, 0.0).astype(BF16)
        tri_x = jnp.where(row < col, 1.0, 0.0).astype(BF16)
        causal = col < row

        def sweep1(kb, slot, run, mask, live=None):
            off = pl.multiple_of(kb * t, t)
            v = v_ref[pl.ds(off, t), :]
            z, wgt, run = _sb_tile(q, k_ref[pl.ds(off, t), :], scale, tri, run, mask)
            if live is not None:
                wgt = jnp.where(live, wgt, 0.0)
            e_ref[slot] = wgt * _dot(do, v, NT)
            b_ref[slot] = _sigmoid(z)
            dv_ref[pl.ds(off, t), :] += _dot(wgt.astype(BF16), do, TN)
            return run

        has_left = iq > 0
        left = jnp.maximum(iq - 1, 0)
        left_slot = jnp.where(has_left, iq - 1, nq)
        run_d = sweep1(iq, iq, jnp.zeros((t, 1), F32), causal)
        run = jnp.where(has_left, sweep1(left, left_slot, run_d, None, has_left), run_d)

        def loop1(carry):
            run = sweep1(carry[0], carry[0], carry[2], None)
            return carry[0] - 1, _sb_alive(run), run

        first = lax.while_loop(_sb_more, loop1, (iq - 2, _sb_alive(run), run))[0] + 1

        def sweep2(kb, slot, pre, mask, live=None):
            off = pl.multiple_of(kb * t, t)
            e = e_ref[slot]
            beta = b_ref[slot]
            ps = _dot(_split16(e), tri_x, NN)
            ps = ps[:t] + ps[t:]
            dz = e * (1.0 - beta) - beta * (ps + pre)
            if mask is not None:
                dz = jnp.where(mask, dz, 0.0)
            if live is not None:
                dz = jnp.where(live, dz, 0.0)
            dz16 = (dz * scale).astype(BF16)
            dk_ref[pl.ds(off, t), :] += _dot(dz16, q, TN)
            return pre + ps[:, t - 1:t] + e[:, t - 1:t], _dot(dz16, k_ref[pl.ds(off, t), :], NN)

        def loop2(kb, carry):
            pre, part = sweep2(kb, kb, carry[0], None)
            return pre, carry[1] + part

        pre, dq = lax.fori_loop(first, iq - 1, loop2, (jnp.zeros((t, 1), F32), jnp.zeros((t, dh), F32)))
        pre, dq_l = sweep2(left, left_slot, pre, None, has_left)
        _, dq_d = sweep2(iq, iq, pre, causal)
        dq_ref[...] = (dq + dq_l + dq_d).astype(BF16)

    blk = lambda off: pl.BlockSpec((t, dh), functools.partial(lambda h, i, off: (i, h + off), off=off))
    col = lambda off: pl.BlockSpec((s, dh), functools.partial(lambda h, i, off: (0, h + off), off=off))
    wide = SB_HEADS * dh
    return pl.pallas_call(
        body, name="sb_bwd", grid=(SB_HEADS, nq),
        in_specs=[blk(0), col(SB_HEADS), col(2 * SB_HEADS), blk(0)],
        out_specs=[blk(0), col(0), col(0)],
        out_shape=[jax.ShapeDtypeStruct((s, wide), BF16), jax.ShapeDtypeStruct((s, wide), F32),
                   jax.ShapeDtypeStruct((s, wide), F32)],
        scratch_shapes=[pltpu.VMEM((nq + 1, t, t), F32), pltpu.VMEM((nq + 1, t, t), F32)],
        compiler_params=_cparams(("parallel", "arbitrary")),
    )(qkv, qkv, qkv, dob)


def _alibi_slope(h):
    return float(2.0 ** (-8.0 * (h + 1) / C_HEADS))


GROUP_ROWS = C_GROUP * WINDOW


def _swa_group_consts(n, c, sink_ref):
    qb = WINDOW
    head = lax.shift_right_logical(lax.broadcasted_iota(jnp.int32, (GROUP_ROWS, 1), 0), qb.bit_length() - 1)
    slope = jnp.zeros((GROUP_ROWS, 1), F32)
    sink = jnp.zeros((GROUP_ROWS, 1), F32)
    for hh in range(C_GROUP):
        slope = jnp.where(head == hh, _alibi_slope(c * C_GROUP + hh), slope)
        sink = jnp.where(head == hh, sink_ref[c * C_GROUP + hh], sink)
    i = lax.broadcasted_iota(jnp.int32, (GROUP_ROWS, 2 * qb), 0) & (qb - 1)
    j = lax.broadcasted_iota(jnp.int32, (GROUP_ROWS, 2 * qb), 1)
    d = i - j + qb
    valid = (d >= 0) & (d < WINDOW) & ((j >= qb) | (n > 0))
    return slope, sink, d.astype(F32), valid


def _swa_probs(qg, kw, sink, slope, dist, valid, scale):
    sc = _dot(qg, kw, NT) * scale - slope * dist
    sc = jnp.where(valid, sc, -1e30)
    m = jnp.maximum(jnp.max(sc, axis=-1, keepdims=True), sink)
    p = jnp.exp(sc - m)
    ps = jnp.exp(sink - m)
    inv = 1.0 / (jnp.sum(p, axis=-1, keepdims=True) + ps)
    return p * inv, ps * inv


def _stack_heads(x, c):
    hd = C_HEAD_DIM
    return jnp.concatenate([x[:, (c * C_GROUP + hh) * hd:(c * C_GROUP + hh + 1) * hd] for hh in range(C_GROUP)], axis=0)


def _swa_fwd(q16, gate, kvp, sinks):
    s = q16.shape[0]
    qb = WINDOW
    hd = C_HEAD_DIM
    scale = 1.0 / math.sqrt(hd)
    kvw = C_KV_HEADS * hd

    def body(sink_ref, q_ref, g_ref, kp_ref, ko_ref, o_ref, y_ref):
        n = pl.program_id(0)
        q = q_ref[...]
        kv = jnp.concatenate([kp_ref[...], ko_ref[...]], axis=0)
        for c in range(C_KV_HEADS):
            slope, sink, dist, valid = _swa_group_consts(n, c, sink_ref)
            kw = kv[:, c * hd:(c + 1) * hd]
            vw = kv[:, kvw + c * hd:kvw + (c + 1) * hd]
            p, _ = _swa_probs(_stack_heads(q, c), kw, sink, slope, dist, valid, scale)
            og = _dot(p.astype(BF16), vw, NN)
            for hh in range(C_GROUP):
                h = c * C_GROUP + hh
                o_ref[:, h * hd:(h + 1) * hd] = og[hh * qb:(hh + 1) * qb, :]
        g = g_ref[...]
        y_ref[...] = (o_ref[...] * (g * _sigmoid(g))).astype(BF16)

    wide = C_HEADS * hd
    return pl.pallas_call(
        body, name="swa_fwd", grid=(s // qb,),
        in_specs=[pl.BlockSpec(memory_space=pltpu.SMEM),
                  pl.BlockSpec((qb, wide), lambda n: (n, 0)), pl.BlockSpec((qb, wide), lambda n: (n, 0)),
                  pl.BlockSpec((qb, 2 * kvw), lambda n: (n, 0)), pl.BlockSpec((qb, 2 * kvw), lambda n: (n + 1, 0))],
        out_specs=[pl.BlockSpec((qb, wide), lambda n: (n, 0)), pl.BlockSpec((qb, wide), lambda n: (n, 0))],
        out_shape=[jax.ShapeDtypeStruct((s, wide), F32), jax.ShapeDtypeStruct((s, wide), BF16)],
        compiler_params=_cparams(("arbitrary",)),
    )(sinks, q16, gate, kvp, kvp)


def _swa_bwd(q16, do16, kvp, sinks):
    s = q16.shape[0]
    qb = WINDOW
    hd = C_HEAD_DIM
    scale = 1.0 / math.sqrt(hd)
    kvw = C_KV_HEADS * hd
    nblk = s // qb

    def body(sink_ref, q_ref, do_ref, kp_ref, ko_ref, dq_ref, dkv_ref, ds_ref, sacc_ref):
        n = pl.program_id(0)

        @pl.when(n == 0)
        def _():
            dkv_ref[...] = jnp.zeros_like(dkv_ref)
            sacc_ref[...] = jnp.zeros_like(sacc_ref)

        q = q_ref[...]
        do = do_ref[...]
        kv = jnp.concatenate([kp_ref[...], ko_ref[...]], axis=0)
        off = pl.multiple_of(n * qb, qb)
        for c in range(C_KV_HEADS):
            slope, sink, dist, valid = _swa_group_consts(n, c, sink_ref)
            kw = kv[:, c * hd:(c + 1) * hd]
            vw = kv[:, kvw + c * hd:kvw + (c + 1) * hd]
            qg = _stack_heads(q, c)
            dog = _stack_heads(do, c)
            p, ps = _swa_probs(qg, kw, sink, slope, dist, valid, scale)
            dp = _dot(dog, vw, NT)
            dd = jnp.sum(p * dp, axis=-1, keepdims=True)
            ds16 = (p * (dp - dd) * scale).astype(BF16)
            sacc_ref[c] += ps * dd
            dqg = _dot(ds16, kw, NN).astype(BF16)
            for hh in range(C_GROUP):
                h = c * C_GROUP + hh
                dq_ref[:, h * hd:(h + 1) * hd] = dqg[hh * qb:(hh + 1) * qb, :]
            dkv_ref[pl.ds(off, 2 * qb), c * hd:(c + 1) * hd] += _dot(ds16, qg, TN)
            dkv_ref[pl.ds(off, 2 * qb), kvw + c * hd:kvw + (c + 1) * hd] += _dot(p.astype(BF16), dog, TN)

        @pl.when(n == nblk - 1)
        def _():
            lane = lax.broadcasted_iota(jnp.int32, (1, 128), 1)
            row = jnp.zeros((1, 128), F32)
            for c in range(C_KV_HEADS):
                for hh in range(C_GROUP):
                    tot = jnp.sum(sacc_ref[c, pl.ds(hh * qb, qb), :], axis=0, keepdims=True)
                    row = jnp.where(lane == c * C_GROUP + hh, -tot, row)
            ds_ref[...] = row

    wide = C_HEADS * hd
    return pl.pallas_call(
        body, name="swa_bwd", grid=(nblk,),
        in_specs=[pl.BlockSpec(memory_space=pltpu.SMEM),
                  pl.BlockSpec((qb, wide), lambda n: (n, 0)), pl.BlockSpec((qb, wide), lambda n: (n, 0)),
                  pl.BlockSpec((qb, 2 * kvw), lambda n: (n, 0)), pl.BlockSpec((qb, 2 * kvw), lambda n: (n + 1, 0))],
        out_specs=[pl.BlockSpec((qb, wide), lambda n: (n, 0)),
                   pl.BlockSpec((s + qb, 2 * kvw), lambda n: (0, 0)), pl.BlockSpec((1, 128), lambda n: (0, 0))],
        out_shape=[jax.ShapeDtypeStruct((s, wide), BF16), jax.ShapeDtypeStruct((s + qb, 2 * kvw), F32),
                   jax.ShapeDtypeStruct((1, 128), F32)],
        scratch_shapes=[pltpu.VMEM((C_KV_HEADS, GROUP_ROWS, 1), F32)],
        compiler_params=_cparams(("arbitrary",)),
    )(sinks, q16, do16, kvp, kvp)


def _adamw(parts, w, m, v, *, name, tr=496):
    npart, r, c = parts.shape
    tr = min(tr, r)
    assert r % tr == 0 and tr % 16 == 0
    c1 = 1.0 / (1.0 - ADAM_B1 ** ADAM_STEP)
    c2 = 1.0 / (1.0 - ADAM_B2 ** ADAM_STEP)

    def body(p_ref, w_ref, m_ref, v_ref, g_ref, d_ref, nm_ref, nv_ref):
        g = p_ref[0].astype(F32)
        for j in range(1, npart):
            g = g + p_ref[j].astype(F32)
        nm = ADAM_B1 * m_ref[...] + (1.0 - ADAM_B1) * g
        nv = ADAM_B2 * v_ref[...] + (1.0 - ADAM_B2) * (g * g)
        g_ref[...] = g
        nm_ref[...] = nm
        nv_ref[...] = nv
        d_ref[...] = -ADAM_LR * ((nm * c1) / (jnp.sqrt(nv * c2) + ADAM_EPS) + ADAM_WD * w_ref[...])

    spec = pl.BlockSpec((tr, c), lambda i: (i, 0))
    return pl.pallas_call(
        body, name=name, grid=(r // tr,),
        in_specs=[pl.BlockSpec((npart, tr, c), lambda i: (0, i, 0)), spec, spec, spec],
        out_specs=[spec] * 4, out_shape=[jax.ShapeDtypeStruct((r, c), F32)] * 4,
        compiler_params=_cparams(("parallel",)),
    )(parts, w, m, v)


def _exchange(bufs, gather, *, name):
    nb = len(bufs)

    def body(*refs):
        _exchange_start(refs[:nb], refs[nb:2 * nb], gather, refs[2 * nb:])
        _exchange_wait(refs[:nb], refs[nb:2 * nb], gather, refs[2 * nb:])

    hbm = pl.BlockSpec(memory_space=pl.ANY)
    return pl.pallas_call(
        body, name=name, in_specs=[hbm] * nb, out_specs=[hbm] * nb, out_shape=_exchange_out_shapes(bufs, gather),
        scratch_shapes=_exchange_sems(nb),
        compiler_params=pltpu.CompilerParams(has_side_effects=True),
    )(*bufs)


def _exchange_out_shapes(bufs, gather):
    return [jax.ShapeDtypeStruct((N_DEV,) + tuple(b.shape if g else b.shape[1:]), b.dtype) for b, g in zip(bufs, gather)]


def _exchange_sems(nb):
    return [pltpu.SemaphoreType.DMA((nb, N_DEV - 1)), pltpu.SemaphoreType.DMA((nb, N_DEV - 1)),
            pltpu.SemaphoreType.DMA((nb,))]


def _exchange_copies(src, dst, gather, sems):
    send_sems, recv_sems, local_sems = sems
    x, y, c = lax.axis_index("x"), lax.axis_index("y"), lax.axis_index("c")
    me = 4 * x + 2 * y + c
    local, sends, recvs = [], [], []
    for b in range(len(src)):
        mine = src[b] if gather[b] else src[b].at[me]
        local.append(pltpu.make_async_copy(mine, dst[b].at[me], local_sems.at[b]))
        for k in range(1, N_DEV):
            px, py, pc = x ^ (k >> 2), y ^ ((k >> 1) & 1), c ^ (k & 1)
            peer = 4 * px + 2 * py + pc
            pair = dict(send_sem=send_sems.at[b, k - 1], recv_sem=recv_sems.at[b, k - 1],
                        device_id_type=pl.DeviceIdType.MESH)
            sends.append(pltpu.make_async_remote_copy(
                src_ref=src[b] if gather[b] else src[b].at[peer], dst_ref=dst[b].at[me], device_id=(px, py, pc), **pair))
            recvs.append(pltpu.make_async_remote_copy(
                src_ref=mine, dst_ref=dst[b].at[peer], device_id=(x, y, c), **pair))
    return local, sends, recvs


def _exchange_start(src, dst, gather, sems):
    local, sends, _ = _exchange_copies(src, dst, gather, sems)
    for cp in local + sends:
        cp.start()


def _exchange_wait(src, dst, gather, sems):
    local, sends, recvs = _exchange_copies(src, dst, gather, sems)
    for cp in sends:
        cp.wait_send()
    for cp in recvs:
        cp.wait_recv()
    for cp in local:
        cp.wait()


R_WIN_E, R_WOUT_E, R_WIN_O, R_WOUT_O, R_SMALL, R_REPL = 768, 256, 288, 128, 16, 48
REPL_ROWS = N_DEV * R_REPL
REPL_LEN = REPL_ROWS * D_MODEL


def _rows(a):
    return a.reshape(-1, D_MODEL)


def _small_block(parts):
    flat = jnp.concatenate(parts, axis=-1)
    lead = flat.ndim - 1
    return jnp.pad(flat[..., None, :], [(0, 0)] * lead + [(0, R_SMALL - 1), (0, D_MODEL - flat.shape[-1])])


def _pack_groups(w_in_e, w_out_e, w_in_o, w_out_o, conv_w, ln_g, ln_b):
    g1 = jnp.concatenate([_rows(w_in_o), _rows(w_out_o), _small_block([ln_g.reshape(128), ln_b.reshape(128)])], axis=0)
    g3 = jnp.concatenate([_rows(w_in_e), _small_block([conv_w.reshape(512)]), jnp.zeros((R_REPL, D_MODEL), F32)], axis=0)
    return g1, _rows(w_out_e), g3


def _unpack_groups(g1, g2, g3):
    w_in_o = g1[0:R_WIN_O].reshape(1, D_MODEL, 288)
    w_out_o = g1[R_WIN_O:R_WIN_O + R_WOUT_O].reshape(1, 128, D_MODEL)
    small1 = g1[R_WIN_O + R_WOUT_O]
    small3 = g3[R_WIN_E]
    return (g3[0:R_WIN_E].reshape(1, D_MODEL, 768), g2.reshape(1, 256, D_MODEL), w_in_o, w_out_o,
            small3[0:512].reshape(1, 4, 128), small1[0:128].reshape(1, 128), small1[128:256].reshape(1, 128))


REPL_SHAPES = ((1, 8, 128, 128), (1, 8, 128, 128), (1, 1024), (1, 1024), (1, 1024), (1, 1024), (1, 1024), (1, 1024),
               (1, 16))


def _pack_repl(parts):
    flat = jnp.concatenate([p.reshape(-1) for p in parts])
    return jnp.concatenate([flat, jnp.zeros((REPL_LEN - flat.shape[0],), F32)]).reshape(REPL_ROWS, D_MODEL)


def _unpack_repl(p):
    flat = p.reshape(-1)
    out, o = [], 0
    for shp in REPL_SHAPES:
        n = math.prod(shp)
        out.append(flat[o:o + n].reshape(shp))
        o += n
    return out


def kernel(x, e_w_in, e_conv_w, e_conv_b, e_w_gate_a, e_b_gate_a, e_w_gate_x, e_b_gate_x, e_lru_lambda, e_w_out, e_ln_g, e_ln_b, o_w_in, o_sinks, o_w_out, o_ln_g, o_ln_b, loss_target, m_e_w_in, m_e_conv_w, m_e_conv_b, m_e_w_gate_a, m_e_b_gate_a, m_e_w_gate_x, m_e_b_gate_x, m_e_lru_lambda, m_e_w_out, m_e_ln_g, m_e_ln_b, m_o_w_in, m_o_sinks, m_o_w_out, m_o_ln_g, m_o_ln_b, v_e_w_in, v_e_conv_w, v_e_conv_b, v_e_w_gate_a, v_e_b_gate_a, v_e_w_gate_x, v_e_b_gate_x, v_e_lru_lambda, v_e_w_out, v_e_ln_g, v_e_ln_b, v_o_w_in, v_o_sinks, v_o_w_out, v_o_ln_g, v_o_ln_b):
    d = D_MODEL
    x0 = x[0]
    target = loss_target[0]
    s = x0.shape[0]

    spack = _small_block([e_conv_w.reshape(512), o_ln_g.reshape(128), o_ln_b.reshape(128)])
    wall_in_e, sall = _exchange([_rows(e_w_in).astype(BF16), spack], [True, True], name="gather_w_in")
    wi_e = wall_in_e.reshape(N_DEV, d, 768).transpose(1, 0, 2).reshape(d, 6 * d)
    conv_w = sall[:, 0, 0:512].reshape(N_DEV, 4, 128).transpose(1, 0, 2).reshape(4, d)
    ln_g_o = sall[:, 0, 512:640].reshape(1, d)
    ln_b_o = sall[:, 0, 640:768].reshape(1, d)
    pvec = jnp.concatenate([conv_w, e_conv_b, e_b_gate_a, e_b_gate_x, e_lru_lambda], axis=0)
    wa16 = e_w_gate_a[0].astype(BF16)
    wx16 = e_w_gate_x[0].astype(BF16)
    sinks = o_sinks[0]

    x0_16 = x0.astype(BF16)
    h0a, wall_out_e = _matmul(x0_16, wi_e, mode="nn", n_out=2 * d, b_off=0, name="l0_in_a",
                              comm=([_rows(e_w_out).astype(BF16)], [True]))
    wpack_o = jnp.concatenate([_rows(o_w_in), _rows(o_w_out)], axis=0).astype(BF16)
    qkv, wall_o = _matmul(x0_16, wi_e, mode="nn", n_out=3 * d, b_off=2 * d, out_dtype=BF16, name="l0_in_qkv",
                          comm=([wpack_o], [True]))
    bg = _matmul(x0_16, wi_e, mode="nn", n_out=d, b_off=5 * d, name="l0_in_bg")
    wo_e = wall_out_e.reshape(2 * d, d)
    wi_o = wall_o[:, 0:R_WIN_O].reshape(N_DEV, d, 288).transpose(1, 0, 2).reshape(d, 2304)
    wi_o = jnp.concatenate([wi_o[:, :1024], wi_o[:, 1280:], wi_o[:, 1024:1280]], axis=1)
    wo_o = wall_o[:, R_WIN_O:R_WIN_O + R_WOUT_O].reshape(d, d)
    ya, hst = _rglru_fwd(h0a, pvec, wa16, wx16)
    ob, yb = _sb_fwd(qkv, bg)
    yab = jnp.concatenate([ya, yb], axis=1)
    y0 = _matmul(yab, wo_e, mode="nn", tk=2048, name="l0_out")
    z0, x1, x1_16 = _ln_fwd(x0, y0, e_ln_g, e_ln_b, name="l0_ln")

    q1 = _matmul(x1_16, wi_o, mode="nn", n_out=d, b_off=0, out_dtype=BF16, name="l1_in_q")
    g1 = _matmul(x1_16, wi_o, mode="nn", n_out=d, b_off=d, name="l1_in_g")
    kv1 = _matmul(x1_16, wi_o, mode="nn", n_out=256, b_off=2 * d, tn=256, out_dtype=BF16, name="l1_in_kv")
    kvp = jnp.concatenate([jnp.zeros((WINDOW, 256), BF16), kv1], axis=0)
    o1, y1 = _swa_fwd(q1, g1, kvp, sinks)
    yy1 = _matmul(y1, wo_o, mode="nn", name="l1_out")
    dz1, dz1_16, loss_cols, dg_o, db_o = _ln_loss_bwd(x1, yy1, target, ln_g_o, ln_b_o, name="l1_ln_loss")
    loss_hi = jnp.sum(loss_cols).astype(BF16).astype(F32)
    loss_terms = jnp.stack([loss_hi, jnp.sum(loss_cols) - loss_hi]).reshape(1, 2)

    dwo_o = _matmul(y1, dz1_16, mode="tn", name="l1_dwout")
    dy1 = _matmul(dz1_16, wo_o, mode="nt", name="l1_dy")
    do1, dg1 = _gate_bwd((dy1, 0), o1, g1, 0, do_dtype=BF16, name="l1_gate_bwd")
    dq1, dkvp, dsink = _swa_bwd(q1, do1, kvp, sinks)
    dh1 = jnp.concatenate([dq1, dg1, dkvp[WINDOW:].astype(BF16)], axis=1)
    dwi_o = _matmul(x1_16, dh1, mode="tn", tn=1152, name="l1_dwin")
    dwi_o = jnp.concatenate([dwi_o[:, :1024], dwi_o[:, 2048:], dwi_o[:, 1024:2048]], axis=1)
    seg1 = jnp.concatenate([
        dwi_o.reshape(d, N_DEV, 288).transpose(1, 0, 2).reshape(N_DEV, R_WIN_O, d),
        dwo_o.reshape(N_DEV, R_WOUT_O, d),
        _small_block([dg_o.reshape(N_DEV, 128), db_o.reshape(N_DEV, 128)])], axis=1).astype(BF16)
    dx1, parts1 = _matmul(dh1, wi_o, mode="nt", tm=512, tk=2304, res=dz1, alpha=ALPHA, name="l1_dx",
                          comm=([seg1], [False]))

    dz0, dz0_16, dg_e, db_e = _ln_bwd_call(z0, dx1, e_ln_g, name="l0_ln_bwd")
    dwo_e = _matmul(yab, dz0_16, mode="tn", name="l0_dwout")
    dyab, parts2 = _matmul(dz0_16, wo_e, mode="nt", name="l0_dy",
                           comm=([dwo_e.reshape(N_DEV, R_WOUT_E, d).astype(BF16)], [False]))
    dhst, dag = _gate_bwd((dyab, 0), hst, h0a, 1, do_dtype=F32, name="l0_gate_bwd_a")
    dob, dbg = _gate_bwd((dyab, 1), ob, bg, 0, do_dtype=BF16, name="l0_gate_bwd_b")
    dq0, dk0, dv0 = _sb_bwd(qkv, dob)
    dax, dwa, dwx, dpv = _rglru_bwd(h0a, hst, dhst, pvec, wa16, wx16)
    dh0 = jnp.concatenate([dax, dag, dq0, dk0.astype(BF16), dv0.astype(BF16), dbg], axis=1)
    dwi_e = _matmul(x0_16, dh0, mode="tn", name="l0_dwin")
    repl = _pack_repl([dwa, dwx, dpv[4:5], dpv[5:6], dpv[6:7], dpv[7:8], dg_e, db_e, dsink[:, :C_HEADS], loss_terms])
    seg3 = jnp.concatenate([
        dwi_e.reshape(d, N_DEV, 768).transpose(1, 0, 2).reshape(N_DEV, R_WIN_E, d),
        _small_block([dpv[0:4].reshape(4, N_DEV, 128).transpose(1, 0, 2).reshape(N_DEV, 512)]),
        repl.reshape(N_DEV, R_REPL, d)], axis=1).astype(BF16)
    grad_x, parts3 = _matmul(dh0, wi_e, mode="nt", tk=2048, res=dz0, alpha=ALPHA, name="l0_dx",
                             comm=([seg3], [False]))

    w_g = _pack_groups(e_w_in, e_w_out, o_w_in, o_w_out, e_conv_w, o_ln_g, o_ln_b)
    m_g = _pack_groups(m_e_w_in, m_e_w_out, m_o_w_in, m_o_w_out, m_e_conv_w, m_o_ln_g, m_o_ln_b)
    v_g = _pack_groups(v_e_w_in, v_e_w_out, v_o_w_in, v_o_w_out, v_e_conv_w, v_o_ln_g, v_o_ln_b)
    res1 = _adamw(parts1, w_g[0], m_g[0], v_g[0], name="adamw_layer1")
    res2 = _adamw(parts2, w_g[1], m_g[1], v_g[1], name="adamw_w_out_e")
    res3 = _adamw(parts3, w_g[2], m_g[2], v_g[2], tr=416, name="adamw_w_in_e")

    (g_repl,) = _exchange([res3[0][R_WIN_E + R_SMALL:]], [True], name="gather_repl_grads")
    g_repl = g_repl.reshape(1, REPL_ROWS, d)
    w_r = _pack_repl([e_w_gate_a, e_w_gate_x, e_conv_b, e_b_gate_a, e_b_gate_x, e_lru_lambda, e_ln_g, e_ln_b, o_sinks])
    m_r = _pack_repl([m_e_w_gate_a, m_e_w_gate_x, m_e_conv_b, m_e_b_gate_a, m_e_b_gate_x, m_e_lru_lambda, m_e_ln_g,
                      m_e_ln_b, m_o_sinks])
    v_r = _pack_repl([v_e_w_gate_a, v_e_w_gate_x, v_e_conv_b, v_e_b_gate_a, v_e_b_gate_x, v_e_lru_lambda, v_e_ln_g,
                      v_e_ln_b, v_o_sinks])
    g_r, d_r, nm_r, nv_r = _adamw(g_repl, w_r, m_r, v_r, name="adamw_replicated")
    loss_at = sum(math.prod(shp) for shp in REPL_SHAPES)
    loss = g_r[loss_at // d, loss_at % d] + g_r[(loss_at + 1) // d, (loss_at + 1) % d]

    def assemble(i, rp):
        w_in_e, w_out_e, w_in_o, w_out_o, cw, lg_o, lb_o = _unpack_groups(res1[i], res2[i], res3[i])
        w_a, w_x, cb, b_a, b_x, lam, lg_e, lb_e, snk = _unpack_repl(rp)
        return [w_in_e, cw, cb, w_a, b_a, w_x, b_x, lam, w_out_e, lg_e, lb_e, w_in_o, snk, w_out_o, lg_o, lb_o]

    return (loss, grad_x[None], *assemble(0, g_r), *assemble(1, d_r), *assemble(2, nm_r), *assemble(3, nv_r))
```

```python
import functools
import math

import jax
import jax.numpy as jnp
from jax import lax
from jax.experimental import pallas as pl
from jax.experimental.pallas import tpu as pltpu

F32 = jnp.float32
BF16 = jnp.bfloat16

N_DEV = 8
D_MODEL = 1024
LRU_BLOCKS = 8
LRU_BLOCK = 128
LRU_C = 8.0
SB_HEADS = 8
SB_HEAD_DIM = 128
C_HEADS = 16
C_KV_HEADS = 2
C_GROUP = 8
C_HEAD_DIM = 64
WINDOW = 128
DEPTH = 2
ALPHA = float((2 * DEPTH) ** 0.25)
LN_EPS = 1e-5
ADAM_LR = 0.001
ADAM_B1 = 0.9
ADAM_B2 = 0.999
ADAM_EPS = 1e-08
ADAM_WD = 0.01
ADAM_STEP = 10

VMEM_LIMIT = 56 * 1024 * 1024

NN = ((1,), (0,))
NT = ((1,), (1,))
TN = ((0,), (0,))


def _dot(a, b, dims):
    return lax.dot_general(a, b, (dims, ((), ())), preferred_element_type=F32)


def _sigmoid(x):
    return 1.0 / (1.0 + jnp.exp(-x))


def _cparams(sem, vmem=VMEM_LIMIT):
    return pltpu.CompilerParams(dimension_semantics=sem, vmem_limit_bytes=vmem)


def _matmul(a, b, *, mode, n_out=None, b_off=0, out_dtype=F32, res=None, alpha=1.0,
            tm=1024, tn=1024, tk=1024, comm=None, epi=None, name):
    if mode == "nn":
        m, k = a.shape
    elif mode == "nt":
        m, k = a.shape
    else:
        k, m = a.shape
    n = n_out if n_out is not None else (b.shape[0] if mode == "nt" else b.shape[1])
    tm, tn, tk = min(tm, m), min(tn, n), min(tk, k)
    assert m % tm == 0 and n % tn == 0 and k % tk == 0 and b_off % tn == 0
    grid = (m // tm, n // tn, k // tk)
    nk = grid[2]
    jo = b_off // tn
    dims = {"nn": NN, "nt": NT, "tn": TN}[mode]
    nc = len(comm[0]) if comm else 0
    if epi:
        assert res is None
        epi_fn, epi_rows, epi_vecs, epi_dtypes, n_sums = epi
        assert n_sums == 0 or grid[1] == 1
    else:
        epi_rows, epi_vecs, epi_dtypes, n_sums = [], [], [out_dtype], 0
    n_in = 2 + (res is not None) + len(epi_rows) + len(epi_vecs)
    n_res = len(epi_dtypes) + n_sums

    def body(*refs):
        a_ref, b_ref = refs[:2]
        r_ref = refs[2] if res is not None else None
        row_refs = refs[n_in - len(epi_rows) - len(epi_vecs):n_in - len(epi_vecs)]
        vec_refs = refs[n_in - len(epi_vecs):n_in]
        o_refs = refs[n_in + nc:n_in + nc + len(epi_dtypes)]
        sum_refs = refs[n_in + nc + len(epi_dtypes):n_in + nc + n_res]
        if comm:
            step = [pl.program_id(ax) for ax in range(3)]
            c_src = refs[n_in:n_in + nc]
            c_dst = refs[n_in + nc + n_res:n_in + 2 * nc + n_res]
            sems = refs[len(refs) - 3:]

            @pl.when((step[0] == 0) & (step[1] == 0) & (step[2] == 0))
            def _():
                _exchange_start(c_src, c_dst, comm[1], sems)

        part = _dot(a_ref[...].astype(BF16), b_ref[...].astype(BF16), dims)

        def finish(out):
            if not epi:
                if r_ref is not None:
                    out = out + alpha * r_ref[...]
                o_refs[0][...] = out.astype(out_dtype)
                return
            outs, sums = epi_fn(out, [r[...] for r in row_refs], [v[...] for v in vec_refs])
            for o_ref, o in zip(o_refs, outs):
                o_ref[...] = o.astype(o_ref.dtype)
            if sum_refs:
                first_rows = pl.program_id(0) == 0

                @pl.when(first_rows)
                def _():
                    for s_ref, part_sum in zip(sum_refs, sums):
                        s_ref[...] = part_sum

                @pl.when(jnp.logical_not(first_rows))
                def _():
                    for s_ref, part_sum in zip(sum_refs, sums):
                        s_ref[...] += part_sum

        if nk == 1:
            finish(part)
        else:
            acc_ref = refs[n_in + 2 * nc + n_res]
            kk = pl.program_id(2)

            @pl.when(kk == 0)
            def _():
                acc_ref[...] = part

            @pl.when(kk > 0)
            def _():
                acc_ref[...] += part

            @pl.when(kk == nk - 1)
            def _():
                finish(acc_ref[...])

        if comm:
            @pl.when((step[0] == grid[0] - 1) & (step[1] == grid[1] - 1) & (step[2] == grid[2] - 1))
            def _():
                _exchange_wait(c_src, c_dst, comm[1], sems)

    if mode == "tn":
        a_spec = pl.BlockSpec((tk, tm), lambda i, j, kk: (kk, i))
    else:
        a_spec = pl.BlockSpec((tm, tk), lambda i, j, kk: (i, kk))
    if mode == "nt":
        b_spec = pl.BlockSpec((tn, tk), lambda i, j, kk: (j + jo, kk))
    else:
        b_spec = pl.BlockSpec((tk, tn), lambda i, j, kk: (kk, j + jo))
    o_spec = pl.BlockSpec((tm, tn), lambda i, j, kk: (i, j))
    hbm = pl.BlockSpec(memory_space=pl.ANY)
    row_specs = [pl.BlockSpec((tm, tn), functools.partial(lambda i, j, kk, o: (i, j + o), o=off // tn))
                 for (_, off) in epi_rows]
    vec_specs = [pl.BlockSpec(v.shape, functools.partial(lambda i, j, kk, nd: (0,) * nd, nd=v.ndim)) for v in epi_vecs]
    in_specs = [a_spec, b_spec] + ([o_spec] if res is not None else []) + row_specs + vec_specs + [hbm] * nc
    args = ([a, b] + ([res] if res is not None else []) + [r for (r, _) in epi_rows] + list(epi_vecs)
            + (list(comm[0]) if comm else []))
    out_specs = [o_spec] * len(epi_dtypes) + [pl.BlockSpec((1, tn), lambda i, j, kk: (0, j))] * n_sums + [hbm] * nc
    out_shape = ([jax.ShapeDtypeStruct((m, n), dt) for dt in epi_dtypes] + [jax.ShapeDtypeStruct((1, n), F32)] * n_sums
                 + (_exchange_out_shapes(*comm) if comm else []))
    scratch = ([pltpu.VMEM((tm, tn), F32)] if nk > 1 else []) + (_exchange_sems(nc) if comm else [])
    if comm or n_sums:
        params = pltpu.CompilerParams(dimension_semantics=("arbitrary",) * 3, vmem_limit_bytes=VMEM_LIMIT,
                                      has_side_effects=bool(comm))
    else:
        params = _cparams(("parallel", "parallel", "arbitrary"))
    outs = pl.pallas_call(
        body, name=name, grid=grid, in_specs=in_specs, out_specs=out_specs, out_shape=out_shape,
        scratch_shapes=scratch, compiler_params=params,
    )(*args)
    return tuple(outs) if (comm or epi) else outs[0]


def _ln_stats(z):
    mu = jnp.mean(z, axis=-1, keepdims=True)
    zc = z - mu
    var = jnp.mean(zc * zc, axis=-1, keepdims=True)
    rstd = lax.rsqrt(var + LN_EPS)
    return zc * rstd, rstd


def _ln_bwd(dy, xhat, rstd, g):
    dxh = dy * g
    m1 = jnp.mean(dxh, axis=-1, keepdims=True)
    m2 = jnp.mean(dxh * xhat, axis=-1, keepdims=True)
    return rstd * (dxh - m1 - xhat * m2)


def _colsum(x):
    return jnp.sum(x, axis=0, keepdims=True)


def _epi_ln_fwd(acc, rows, vecs):
    z = ALPHA * rows[0] + acc
    xhat, _ = _ln_stats(z)
    x1 = xhat * vecs[0] + vecs[1]
    return [z, x1, x1], []


def _epi_ln_loss_bwd(acc, rows, vecs):
    inv_c = 1.0 / acc.shape[-1]
    z = ALPHA * rows[0] + acc
    xhat, rstd = _ln_stats(z)
    d = xhat * vecs[0] + vecs[1] - rows[1]
    dy = d * inv_c
    dz = _ln_bwd(dy, xhat, rstd, vecs[0])
    return [dz, dz], [_colsum(0.5 * d * d * inv_c), _colsum(dy * xhat), _colsum(dy)]


def _epi_ln_bwd(acc, rows, vecs):
    dy = acc + ALPHA * rows[0]
    xhat, rstd = _ln_stats(rows[1])
    dz = _ln_bwd(dy, xhat, rstd, vecs[0])
    return [dz, dz], [_colsum(dy * xhat), _colsum(dy)]


def _epi_gate_bwd(acc, rows, vecs):
    oo, gg = rows
    sg = _sigmoid(gg)
    return [acc * (gg * sg), acc * oo * (sg * (1.0 + gg * (1.0 - sg)))], []


LRU_T = 256
HALO = 8


def _log1p(y):
    u = 1.0 + y
    return jnp.where(u == 1.0, y, jnp.log(u) * (y / (u - 1.0)))


def _lru_gates(c, wa_ref, wx_ref, pv):
    c16 = c.astype(BF16)
    pre_r = jnp.concatenate(
        [_dot(c16[:, n * LRU_BLOCK:(n + 1) * LRU_BLOCK], wa_ref[n], NN) for n in range(LRU_BLOCKS)], axis=1)
    pre_i = jnp.concatenate(
        [_dot(c16[:, n * LRU_BLOCK:(n + 1) * LRU_BLOCK], wx_ref[n], NN) for n in range(LRU_BLOCKS)], axis=1)
    r = _sigmoid(pre_r + pv[5:6])
    ig = _sigmoid(pre_i + pv[6:7])
    lam = pv[7:8]
    ls = jnp.minimum(lam, 0.0) - _log1p(jnp.exp(-jnp.abs(lam)))
    la = LRU_C * r * ls
    a = jnp.exp(la)
    a2 = a * a
    m = jnp.sqrt(jnp.tanh(-la) * (a2 + 1.0))
    return c16, r, ig, ls, la, a, a2, m


def _conv(ext_ref, x, pv, t):
    return (pv[4:5] + pv[3:4] * x + pv[2:3] * ext_ref[pl.ds(HALO - 1, t), :]
            + pv[1:2] * ext_ref[pl.ds(HALO - 2, t), :] + pv[0:1] * ext_ref[pl.ds(HALO - 3, t), :])


def _rglru_fwd(h0a, pvec, wa16, wx16):
    s = h0a.shape[0]
    w = D_MODEL
    t = min(LRU_T, s)
    assert s % t == 0

    def body(ax_ref, ag_ref, pv_ref, wa_ref, wx_ref, ya_ref, h_ref, ext_ref, a_ref, u_ref, hc_ref):
        i = pl.program_id(0)

        @pl.when(i == 0)
        def _():
            ext_ref[pl.ds(0, HALO), :] = jnp.zeros((HALO, w), F32)
            hc_ref[...] = jnp.zeros((1, w), F32)

        pv = pv_ref[...]
        ax = ax_ref[...]
        ext_ref[pl.ds(HALO, t), :] = ax
        c = _conv(ext_ref, ax, pv, t)
        ext_ref[pl.ds(0, HALO), :] = ax[t - HALO:, :]
        _, _, ig, _, _, a, _, m = _lru_gates(c, wa_ref, wx_ref, pv)
        a_ref[...] = a
        u_ref[...] = m * (ig * c)

        def step(k, h):
            h = a_ref[pl.ds(k, 1), :] * h + u_ref[pl.ds(k, 1), :]
            h_ref[pl.ds(k, 1), :] = h
            return h

        hc_ref[...] = lax.fori_loop(0, t, step, hc_ref[...], unroll=8)
        ag = ag_ref[...]
        ya_ref[...] = (h_ref[...] * (ag * _sigmoid(ag))).astype(BF16)

    full = lambda shp: pl.BlockSpec(shp, lambda i: (0,) * len(shp))
    return pl.pallas_call(
        body, name="rglru_fwd", grid=(s // t,),
        in_specs=[pl.BlockSpec((t, w), lambda i: (i, 0)), pl.BlockSpec((t, w), lambda i: (i, 1)),
                  full((8, w)), full((LRU_BLOCKS, LRU_BLOCK, LRU_BLOCK)), full((LRU_BLOCKS, LRU_BLOCK, LRU_BLOCK))],
        out_specs=[pl.BlockSpec((t, w), lambda i: (i, 0)), pl.BlockSpec((t, w), lambda i: (i, 0))],
        out_shape=[jax.ShapeDtypeStruct((s, w), BF16), jax.ShapeDtypeStruct((s, w), F32)],
        scratch_shapes=[pltpu.VMEM((t + HALO, w), F32), pltpu.VMEM((t, w), F32), pltpu.VMEM((t, w), F32),
                        pltpu.VMEM((1, w), F32)],
        compiler_params=_cparams(("arbitrary",)),
    )(h0a, h0a, pvec, wa16, wx16)


def _rglru_bwd(h0a, h, dh, pvec, wa16, wx16):
    s = h0a.shape[0]
    w = D_MODEL
    t = min(LRU_T, s)
    nb = s // t
    hb = t // HALO

    def body(ax_ref, axh_ref, h_ref, hh_ref, dh_ref, pv_ref, wa_ref, wx_ref,
             dax_ref, dwa_ref, dwx_ref, dpv_ref, ext_ref, hext_ref, dcext_ref, a_ref, g_ref, gc_ref):
        i = pl.program_id(0)
        blk = nb - 1 - i

        @pl.when(i == 0)
        def _():
            dwa_ref[...] = jnp.zeros_like(dwa_ref)
            dwx_ref[...] = jnp.zeros_like(dwx_ref)
            dpv_ref[...] = jnp.zeros_like(dpv_ref)
            gc_ref[...] = jnp.zeros((1, w), F32)
            dcext_ref[pl.ds(t, HALO), :] = jnp.zeros((HALO, w), F32)

        pv = pv_ref[...]
        ax = ax_ref[...]
        keep = (blk > 0).astype(F32)
        ext_ref[pl.ds(0, HALO), :] = axh_ref[...] * keep
        ext_ref[pl.ds(HALO, t), :] = ax
        hext_ref[pl.ds(0, HALO), :] = hh_ref[...] * keep
        hext_ref[pl.ds(HALO, t), :] = h_ref[...]
        c = _conv(ext_ref, ax, pv, t)
        c16, r, ig, ls, _, a, a2, m = _lru_gates(c, wa_ref, wx_ref, pv)
        a_ref[...] = a

        def step(k, carry):
            row = t - 1 - k
            g = dh_ref[pl.ds(row, 1), :] + carry
            g_ref[pl.ds(row, 1), :] = g
            return a_ref[pl.ds(row, 1), :] * g

        gc_ref[...] = lax.fori_loop(0, t, step, gc_ref[...], unroll=8)
        g = g_ref[...]
        hprev = hext_ref[pl.ds(HALO - 1, t), :]
        gm = g * m
        d_la = g * hprev * a - (g * ig * c) * a2 / m
        d_pr = d_la * (LRU_C * ls) * r * (1.0 - r)
        d_pi = gm * c * ig * (1.0 - ig)
        dc = gm * ig
        dpr16 = d_pr.astype(BF16)
        dpi16 = d_pi.astype(BF16)
        dc_parts = []
        for n in range(LRU_BLOCKS):
            sl = slice(n * LRU_BLOCK, (n + 1) * LRU_BLOCK)
            dwa_ref[n] += _dot(c16[:, sl], dpr16[:, sl], TN)
            dwx_ref[n] += _dot(c16[:, sl], dpi16[:, sl], TN)
            dc_parts.append(_dot(dpr16[:, sl], wa_ref[n], NT) + _dot(dpi16[:, sl], wx_ref[n], NT))
        dc = dc + jnp.concatenate(dc_parts, axis=1)
        dcext_ref[pl.ds(0, t), :] = dc
        dax = (pv[3:4] * dc + pv[2:3] * dcext_ref[pl.ds(1, t), :] + pv[1:2] * dcext_ref[pl.ds(2, t), :]
               + pv[0:1] * dcext_ref[pl.ds(3, t), :])
        dax_ref[...] = dax.astype(BF16)
        dcext_ref[pl.ds(t, HALO), :] = dc[:HALO, :]
        sums = [_colsum(dc * ext_ref[pl.ds(HALO - 3 + kk, t), :]) for kk in range(4)]
        sums += [_colsum(dc), _colsum(d_pr), _colsum(d_pi), _colsum(d_la * (LRU_C * r))]
        for kk, part in enumerate(sums):
            dpv_ref[pl.ds(kk, 1), :] += part

        @pl.when(i == nb - 1)
        def _():
            lam = pv[7:8]
            dls = dpv_ref[pl.ds(7, 1), :]
            dpv_ref[pl.ds(7, 1), :] = dls * _sigmoid(-lam)

    full = lambda shp: pl.BlockSpec(shp, lambda i: (0,) * len(shp))
    rev = lambda cb: pl.BlockSpec((t, w), functools.partial(lambda i, cb: (nb - 1 - i, cb), cb=cb))
    halo = lambda cb: pl.BlockSpec(
        (HALO, w), functools.partial(lambda i, cb: (jnp.maximum((nb - 1 - i) * hb - 1, 0), cb), cb=cb))
    gw = (LRU_BLOCKS, LRU_BLOCK, LRU_BLOCK)
    return pl.pallas_call(
        body, name="rglru_bwd", grid=(nb,),
        in_specs=[rev(0), halo(0), rev(0), halo(0), rev(0), full((8, w)), full(gw), full(gw)],
        out_specs=[rev(0), full(gw), full(gw), full((8, w))],
        out_shape=[jax.ShapeDtypeStruct((s, w), BF16), jax.ShapeDtypeStruct(gw, F32),
                   jax.ShapeDtypeStruct(gw, F32), jax.ShapeDtypeStruct((8, w), F32)],
        scratch_shapes=[pltpu.VMEM((t + HALO, w), F32), pltpu.VMEM((t + HALO, w), F32),
                        pltpu.VMEM((t + HALO, w), F32), pltpu.VMEM((t, w), F32), pltpu.VMEM((t, w), F32),
                        pltpu.VMEM((1, w), F32)],
        compiler_params=_cparams(("arbitrary",)),
    )(h0a, h0a, h, h, dh, pvec, wa16, wx16)


SB_T = 256


def _split16(x):
    hi = x.astype(BF16)
    lo = (x - hi.astype(F32)).astype(BF16)
    return jnp.concatenate([hi, lo], axis=0)


def _sb_tile(q, k, scale, tri, run, causal):
    tq = q.shape[0]
    z = _dot(q, k, NT) * scale
    log1mb = -(jnp.maximum(z, 0.0) + jnp.log(1.0 + jnp.exp(-jnp.abs(z))))
    if causal is not None:
        log1mb = jnp.where(causal, log1mb, 0.0)
    cs = _dot(_split16(log1mb), tri, NN)
    cs = cs[:tq] + cs[tq:]
    wgt = jnp.exp(z + cs + run)
    if causal is not None:
        wgt = jnp.where(causal, wgt, 0.0)
    return z, wgt, run + cs[:, 0:1]


SB_DEAD = -105.0


def _sb_alive(run):
    return (jnp.max(run) > SB_DEAD).astype(jnp.int32)


def _sb_more(carry):
    return (carry[0] >= 0) & (carry[1] > 0)


def _sb_fwd(qkv, bg):
    s = qkv.shape[0]
    t = min(SB_T, s)
    nq = s // t
    dh = SB_HEAD_DIM
    scale = 1.0 / math.sqrt(dh)

    def body(q_ref, k_ref, v_ref, bg_ref, o_ref, y_ref):
        iq = pl.program_id(1)
        q = q_ref[...]
        row = lax.broadcasted_iota(jnp.int32, (t, t), 0)
        col = lax.broadcasted_iota(jnp.int32, (t, t), 1)
        tri = jnp.where(row >= col, 1.0, 0.0).astype(BF16)
        causal = col < row

        def tile(kb, run, mask):
            off = pl.multiple_of(kb * t, t)
            _, wgt, run = _sb_tile(q, k_ref[pl.ds(off, t), :], scale, tri, run, mask)
            return run, _dot(wgt.astype(BF16), v_ref[pl.ds(off, t), :], NN)

        has_left = iq > 0
        run_d, acc_d = tile(iq, jnp.zeros((t, 1), F32), causal)
        run_l, acc_l = tile(jnp.maximum(iq - 1, 0), run_d, None)
        run = jnp.where(has_left, run_l, run_d)
        acc = acc_d + jnp.where(has_left, acc_l, 0.0)

        def loop(carry):
            kb, _, run, acc = carry
            run, part = tile(kb, run, None)
            return kb - 1, _sb_alive(run), run, acc + part

        _, _, run, acc = lax.while_loop(_sb_more, loop, (iq - 2, _sb_alive(run), run, acc))
        o_ref[...] = acc
        g = bg_ref[...]
        y_ref[...] = (acc * (g * _sigmoid(g))).astype(BF16)

    blk = lambda off: pl.BlockSpec((t, dh), functools.partial(lambda h, i, off: (i, h + off), off=off))
    col = lambda off: pl.BlockSpec((s, dh), functools.partial(lambda h, i, off: (0, h + off), off=off))
    return pl.pallas_call(
        body, name="sb_fwd", grid=(SB_HEADS, nq),
        in_specs=[blk(0), col(SB_HEADS), col(2 * SB_HEADS), blk(0)],
        out_specs=[blk(0), blk(0)],
        out_shape=[jax.ShapeDtypeStruct((s, SB_HEADS * dh), F32), jax.ShapeDtypeStruct((s, SB_HEADS * dh), BF16)],
        compiler_params=_cparams(("parallel", "arbitrary")),
    )(qkv, qkv, qkv, bg)


def _sb_bwd(qkv, dob):
    s = qkv.shape[0]
    t = min(SB_T, s)
    nq = s // t
    dh = SB_HEAD_DIM
    scale = 1.0 / math.sqrt(dh)

    def body(q_ref, k_ref, v_ref, do_ref, dq_ref, dk_ref, dv_ref, e_ref, b_ref):
        iq = pl.program_id(1)

        @pl.when(iq == 0)
        def _():
            dk_ref[...] = jnp.zeros_like(dk_ref)
            dv_ref[...] = jnp.zeros_like(dv_ref)

        q = q_ref[...]
        do = do_ref[...]
        row = lax.broadcasted_iota(jnp.int32, (t, t), 0)
        col = lax.broadcasted_iota(jnp.int32, (t, t), 1)
        tri = jnp.where(row >= col, 1.0, 0.0).astype(BF16)
        tri_x = jnp.where(row < col, 1.0, 0.0).astype(BF16)
        causal = col < row

        def sweep1(kb, slot, run, mask, live=None):
            off = pl.multiple_of(kb * t, t)
            v = v_ref[pl.ds(off, t), :]
            z, wgt, run = _sb_tile(q, k_ref[pl.ds(off, t), :], scale, tri, run, mask)
            if live is not None:
                wgt = jnp.where(live, wgt, 0.0)
            e_ref[slot] = wgt * _dot(do, v, NT)
            b_ref[slot] = _sigmoid(z)
            dv_ref[pl.ds(off, t), :] += _dot(wgt.astype(BF16), do, TN)
            return run

        has_left = iq > 0
        left = jnp.maximum(iq - 1, 0)
        left_slot = jnp.where(has_left, iq - 1, nq)
        run_d = sweep1(iq, iq, jnp.zeros((t, 1), F32), causal)
        run = jnp.where(has_left, sweep1(left, left_slot, run_d, None, has_left), run_d)

        def loop1(carry):
            run = sweep1(carry[0], carry[0], carry[2], None)
            return carry[0] - 1, _sb_alive(run), run

        first = lax.while_loop(_sb_more, loop1, (iq - 2, _sb_alive(run), run))[0] + 1

        def sweep2(kb, slot, pre, mask, live=None):
            off = pl.multiple_of(kb * t, t)
            e = e_ref[slot]
            beta = b_ref[slot]
            ps = _dot(_split16(e), tri_x, NN)
            ps = ps[:t] + ps[t:]
            dz = e * (1.0 - beta) - beta * (ps + pre)
            if mask is not None:
                dz = jnp.where(mask, dz, 0.0)
            if live is not None:
                dz = jnp.where(live, dz, 0.0)
            dz16 = (dz * scale).astype(BF16)
            dk_ref[pl.ds(off, t), :] += _dot(dz16, q, TN)
            return pre + ps[:, t - 1:t] + e[:, t - 1:t], _dot(dz16, k_ref[pl.ds(off, t), :], NN)

        def loop2(kb, carry):
            pre, part = sweep2(kb, kb, carry[0], None)
            return pre, carry[1] + part

        pre, dq = lax.fori_loop(first, iq - 1, loop2, (jnp.zeros((t, 1), F32), jnp.zeros((t, dh), F32)))
        pre, dq_l = sweep2(left, left_slot, pre, None, has_left)
        _, dq_d = sweep2(iq, iq, pre, causal)
        dq_ref[...] = (dq + dq_l + dq_d).astype(BF16)

    blk = lambda off: pl.BlockSpec((t, dh), functools.partial(lambda h, i, off: (i, h + off), off=off))
    col = lambda off: pl.BlockSpec((s, dh), functools.partial(lambda h, i, off: (0, h + off), off=off))
    wide = SB_HEADS * dh
    return pl.pallas_call(
        body, name="sb_bwd", grid=(SB_HEADS, nq),
        in_specs=[blk(0), col(SB_HEADS), col(2 * SB_HEADS), blk(0)],
        out_specs=[blk(0), col(0), col(0)],
        out_shape=[jax.ShapeDtypeStruct((s, wide), BF16), jax.ShapeDtypeStruct((s, wide), F32),
                   jax.ShapeDtypeStruct((s, wide), F32)],
        scratch_shapes=[pltpu.VMEM((nq + 1, t, t), F32), pltpu.VMEM((nq + 1, t, t), F32)],
        compiler_params=_cparams(("parallel", "arbitrary")),
    )(qkv, qkv, qkv, dob)


def _alibi_slope(h):
    return float(2.0 ** (-8.0 * (h + 1) / C_HEADS))


GROUP_ROWS = C_GROUP * WINDOW


def _swa_group_consts(n, c, sink_ref):
    qb = WINDOW
    head = lax.shift_right_logical(lax.broadcasted_iota(jnp.int32, (GROUP_ROWS, 1), 0), qb.bit_length() - 1)
    slope = jnp.zeros((GROUP_ROWS, 1), F32)
    sink = jnp.zeros((GROUP_ROWS, 1), F32)
    for hh in range(C_GROUP):
        slope = jnp.where(head == hh, _alibi_slope(c * C_GROUP + hh), slope)
        sink = jnp.where(head == hh, sink_ref[c * C_GROUP + hh], sink)
    i = lax.broadcasted_iota(jnp.int32, (GROUP_ROWS, 2 * qb), 0) & (qb - 1)
    j = lax.broadcasted_iota(jnp.int32, (GROUP_ROWS, 2 * qb), 1)
    d = i - j + qb
    valid = (d >= 0) & (d < WINDOW) & ((j >= qb) | (n > 0))
    return slope, sink, d.astype(F32), valid


def _swa_probs(qg, kw, sink, slope, dist, valid, scale):
    sc = _dot(qg, kw, NT) * scale - slope * dist
    sc = jnp.where(valid, sc, -1e30)
    m = jnp.maximum(jnp.max(sc, axis=-1, keepdims=True), sink)
    p = jnp.exp(sc - m)
    ps = jnp.exp(sink - m)
    inv = 1.0 / (jnp.sum(p, axis=-1, keepdims=True) + ps)
    return p * inv, ps * inv


def _stack_heads(x, c):
    hd = C_HEAD_DIM
    return jnp.concatenate([x[:, (c * C_GROUP + hh) * hd:(c * C_GROUP + hh + 1) * hd] for hh in range(C_GROUP)], axis=0)


def _swa_fwd(q16, gate, kvp, sinks):
    s = q16.shape[0]
    qb = WINDOW
    hd = C_HEAD_DIM
    scale = 1.0 / math.sqrt(hd)
    kvw = C_KV_HEADS * hd

    def body(sink_ref, q_ref, g_ref, kp_ref, ko_ref, o_ref, y_ref):
        n = pl.program_id(0)
        q = q_ref[...]
        kv = jnp.concatenate([kp_ref[...], ko_ref[...]], axis=0)
        for c in range(C_KV_HEADS):
            slope, sink, dist, valid = _swa_group_consts(n, c, sink_ref)
            kw = kv[:, c * hd:(c + 1) * hd]
            vw = kv[:, kvw + c * hd:kvw + (c + 1) * hd]
            p, _ = _swa_probs(_stack_heads(q, c), kw, sink, slope, dist, valid, scale)
            og = _dot(p.astype(BF16), vw, NN)
            for hh in range(C_GROUP):
                h = c * C_GROUP + hh
                o_ref[:, h * hd:(h + 1) * hd] = og[hh * qb:(hh + 1) * qb, :]
        g = g_ref[...]
        y_ref[...] = (o_ref[...] * (g * _sigmoid(g))).astype(BF16)

    wide = C_HEADS * hd
    return pl.pallas_call(
        body, name="swa_fwd", grid=(s // qb,),
        in_specs=[pl.BlockSpec(memory_space=pltpu.SMEM),
                  pl.BlockSpec((qb, wide), lambda n: (n, 0)), pl.BlockSpec((qb, wide), lambda n: (n, 0)),
                  pl.BlockSpec((qb, 2 * kvw), lambda n: (n, 0)), pl.BlockSpec((qb, 2 * kvw), lambda n: (n + 1, 0))],
        out_specs=[pl.BlockSpec((qb, wide), lambda n: (n, 0)), pl.BlockSpec((qb, wide), lambda n: (n, 0))],
        out_shape=[jax.ShapeDtypeStruct((s, wide), F32), jax.ShapeDtypeStruct((s, wide), BF16)],
        compiler_params=_cparams(("arbitrary",)),
    )(sinks, q16, gate, kvp, kvp)


def _swa_bwd(q16, do16, kvp, sinks):
    s = q16.shape[0]
    qb = WINDOW
    hd = C_HEAD_DIM
    scale = 1.0 / math.sqrt(hd)
    kvw = C_KV_HEADS * hd
    nblk = s // qb

    def body(sink_ref, q_ref, do_ref, kp_ref, ko_ref, dq_ref, dkv_ref, ds_ref, sacc_ref):
        n = pl.program_id(0)

        @pl.when(n == 0)
        def _():
            dkv_ref[...] = jnp.zeros_like(dkv_ref)
            sacc_ref[...] = jnp.zeros_like(sacc_ref)

        q = q_ref[...]
        do = do_ref[...]
        kv = jnp.concatenate([kp_ref[...], ko_ref[...]], axis=0)
        off = pl.multiple_of(n * qb, qb)
        for c in range(C_KV_HEADS):
            slope, sink, dist, valid = _swa_group_consts(n, c, sink_ref)
            kw = kv[:, c * hd:(c + 1) * hd]
            vw = kv[:, kvw + c * hd:kvw + (c + 1) * hd]
            qg = _stack_heads(q, c)
            dog = _stack_heads(do, c)
            p, ps = _swa_probs(qg, kw, sink, slope, dist, valid, scale)
            dp = _dot(dog, vw, NT)
            dd = jnp.sum(p * dp, axis=-1, keepdims=True)
            ds16 = (p * (dp - dd) * scale).astype(BF16)
            sacc_ref[c] += ps * dd
            dqg = _dot(ds16, kw, NN).astype(BF16)
            for hh in range(C_GROUP):
                h = c * C_GROUP + hh
                dq_ref[:, h * hd:(h + 1) * hd] = dqg[hh * qb:(hh + 1) * qb, :]
            dkv_ref[pl.ds(off, 2 * qb), c * hd:(c + 1) * hd] += _dot(ds16, qg, TN)
            dkv_ref[pl.ds(off, 2 * qb), kvw + c * hd:kvw + (c + 1) * hd] += _dot(p.astype(BF16), dog, TN)

        @pl.when(n == nblk - 1)
        def _():
            lane = lax.broadcasted_iota(jnp.int32, (1, 128), 1)
            row = jnp.zeros((1, 128), F32)
            for c in range(C_KV_HEADS):
                for hh in range(C_GROUP):
                    tot = jnp.sum(sacc_ref[c, pl.ds(hh * qb, qb), :], axis=0, keepdims=True)
                    row = jnp.where(lane == c * C_GROUP + hh, -tot, row)
            ds_ref[...] = row

    wide = C_HEADS * hd
    return pl.pallas_call(
        body, name="swa_bwd", grid=(nblk,),
        in_specs=[pl.BlockSpec(memory_space=pltpu.SMEM),
                  pl.BlockSpec((qb, wide), lambda n: (n, 0)), pl.BlockSpec((qb, wide), lambda n: (n, 0)),
                  pl.BlockSpec((qb, 2 * kvw), lambda n: (n, 0)), pl.BlockSpec((qb, 2 * kvw), lambda n: (n + 1, 0))],
        out_specs=[pl.BlockSpec((qb, wide), lambda n: (n, 0)),
                   pl.BlockSpec((s + qb, 2 * kvw), lambda n: (0, 0)), pl.BlockSpec((1, 128), lambda n: (0, 0))],
        out_shape=[jax.ShapeDtypeStruct((s, wide), BF16), jax.ShapeDtypeStruct((s + qb, 2 * kvw), F32),
                   jax.ShapeDtypeStruct((1, 128), F32)],
        scratch_shapes=[pltpu.VMEM((C_KV_HEADS, GROUP_ROWS, 1), F32)],
        compiler_params=_cparams(("arbitrary",)),
    )(sinks, q16, do16, kvp, kvp)


def _adamw(parts, w, m, v, *, name, tr=496):
    npart, r, c = parts.shape
    tr = min(tr, r)
    assert r % tr == 0 and tr % 16 == 0
    c1 = 1.0 / (1.0 - ADAM_B1 ** ADAM_STEP)
    c2 = 1.0 / (1.0 - ADAM_B2 ** ADAM_STEP)

    def body(p_ref, w_ref, m_ref, v_ref, g_ref, d_ref, nm_ref, nv_ref):
        g = p_ref[0].astype(F32)
        for j in range(1, npart):
            g = g + p_ref[j].astype(F32)
        nm = ADAM_B1 * m_ref[...] + (1.0 - ADAM_B1) * g
        nv = ADAM_B2 * v_ref[...] + (1.0 - ADAM_B2) * (g * g)
        g_ref[...] = g
        nm_ref[...] = nm
        nv_ref[...] = nv
        d_ref[...] = -ADAM_LR * ((nm * c1) / (jnp.sqrt(nv * c2) + ADAM_EPS) + ADAM_WD * w_ref[...])

    spec = pl.BlockSpec((tr, c), lambda i: (i, 0))
    return pl.pallas_call(
        body, name=name, grid=(r // tr,),
        in_specs=[pl.BlockSpec((npart, tr, c), lambda i: (0, i, 0)), spec, spec, spec],
        out_specs=[spec] * 4, out_shape=[jax.ShapeDtypeStruct((r, c), F32)] * 4,
        compiler_params=_cparams(("parallel",)),
    )(parts, w, m, v)


def _exchange(bufs, gather, *, name):
    nb = len(bufs)

    def body(*refs):
        _exchange_start(refs[:nb], refs[nb:2 * nb], gather, refs[2 * nb:])
        _exchange_wait(refs[:nb], refs[nb:2 * nb], gather, refs[2 * nb:])

    hbm = pl.BlockSpec(memory_space=pl.ANY)
    return pl.pallas_call(
        body, name=name, in_specs=[hbm] * nb, out_specs=[hbm] * nb, out_shape=_exchange_out_shapes(bufs, gather),
        scratch_shapes=_exchange_sems(nb),
        compiler_params=pltpu.CompilerParams(has_side_effects=True),
    )(*bufs)


def _exchange_out_shapes(bufs, gather):
    return [jax.ShapeDtypeStruct((N_DEV,) + tuple(b.shape if g else b.shape[1:]), b.dtype) for b, g in zip(bufs, gather)]


def _exchange_sems(nb):
    return [pltpu.SemaphoreType.DMA((nb, N_DEV - 1)), pltpu.SemaphoreType.DMA((nb, N_DEV - 1)),
            pltpu.SemaphoreType.DMA((nb,))]


def _exchange_copies(src, dst, gather, sems):
    send_sems, recv_sems, local_sems = sems
    x, y, c = lax.axis_index("x"), lax.axis_index("y"), lax.axis_index("c")
    me = 4 * x + 2 * y + c
    local, sends, recvs = [], [], []
    for b in range(len(src)):
        mine = src[b] if gather[b] else src[b].at[me]
        local.append(pltpu.make_async_copy(mine, dst[b].at[me], local_sems.at[b]))
        for k in range(1, N_DEV):
            px, py, pc = x ^ (k >> 2), y ^ ((k >> 1) & 1), c ^ (k & 1)
            peer = 4 * px + 2 * py + pc
            pair = dict(send_sem=send_sems.at[b, k - 1], recv_sem=recv_sems.at[b, k - 1],
                        device_id_type=pl.DeviceIdType.MESH)
            sends.append(pltpu.make_async_remote_copy(
                src_ref=src[b] if gather[b] else src[b].at[peer], dst_ref=dst[b].at[me], device_id=(px, py, pc), **pair))
            recvs.append(pltpu.make_async_remote_copy(
                src_ref=mine, dst_ref=dst[b].at[peer], device_id=(x, y, c), **pair))
    return local, sends, recvs


def _exchange_start(src, dst, gather, sems):
    local, sends, _ = _exchange_copies(src, dst, gather, sems)
    for cp in local + sends:
        cp.start()


def _exchange_wait(src, dst, gather, sems):
    local, sends, recvs = _exchange_copies(src, dst, gather, sems)
    for cp in sends:
        cp.wait_send()
    for cp in recvs:
        cp.wait_recv()
    for cp in local:
        cp.wait()


R_WIN_E, R_WOUT_E, R_WIN_O, R_WOUT_O, R_SMALL, R_REPL = 768, 256, 288, 128, 16, 48
REPL_ROWS = N_DEV * R_REPL
REPL_LEN = REPL_ROWS * D_MODEL


def _rows(a):
    return a.reshape(-1, D_MODEL)


def _small_block(parts):
    flat = jnp.concatenate(parts, axis=-1)
    lead = flat.ndim - 1
    return jnp.pad(flat[..., None, :], [(0, 0)] * lead + [(0, R_SMALL - 1), (0, D_MODEL - flat.shape[-1])])


def _pack_groups(w_in_e, w_out_e, w_in_o, w_out_o, conv_w, ln_g, ln_b):
    g1 = jnp.concatenate([_rows(w_in_o), _rows(w_out_o), _small_block([ln_g.reshape(128), ln_b.reshape(128)])], axis=0)
    g3 = jnp.concatenate([_rows(w_in_e), _small_block([conv_w.reshape(512)]), jnp.zeros((R_REPL, D_MODEL), F32)], axis=0)
    return g1, _rows(w_out_e), g3


def _unpack_groups(g1, g2, g3):
    w_in_o = g1[0:R_WIN_O].reshape(1, D_MODEL, 288)
    w_out_o = g1[R_WIN_O:R_WIN_O + R_WOUT_O].reshape(1, 128, D_MODEL)
    small1 = g1[R_WIN_O + R_WOUT_O]
    small3 = g3[R_WIN_E]
    return (g3[0:R_WIN_E].reshape(1, D_MODEL, 768), g2.reshape(1, 256, D_MODEL), w_in_o, w_out_o,
            small3[0:512].reshape(1, 4, 128), small1[0:128].reshape(1, 128), small1[128:256].reshape(1, 128))


REPL_SHAPES = ((1, 8, 128, 128), (1, 8, 128, 128), (1, 1024), (1, 1024), (1, 1024), (1, 1024), (1, 1024), (1, 1024),
               (1, 16))


def _pack_repl(parts):
    flat = jnp.concatenate([p.reshape(-1) for p in parts])
    return jnp.concatenate([flat, jnp.zeros((REPL_LEN - flat.shape[0],), F32)]).reshape(REPL_ROWS, D_MODEL)


def _unpack_repl(p):
    flat = p.reshape(-1)
    out, o = [], 0
    for shp in REPL_SHAPES:
        n = math.prod(shp)
        out.append(flat[o:o + n].reshape(shp))
        o += n
    return out


def kernel(x, e_w_in, e_conv_w, e_conv_b, e_w_gate_a, e_b_gate_a, e_w_gate_x, e_b_gate_x, e_lru_lambda, e_w_out, e_ln_g, e_ln_b, o_w_in, o_sinks, o_w_out, o_ln_g, o_ln_b, loss_target, m_e_w_in, m_e_conv_w, m_e_conv_b, m_e_w_gate_a, m_e_b_gate_a, m_e_w_gate_x, m_e_b_gate_x, m_e_lru_lambda, m_e_w_out, m_e_ln_g, m_e_ln_b, m_o_w_in, m_o_sinks, m_o_w_out, m_o_ln_g, m_o_ln_b, v_e_w_in, v_e_conv_w, v_e_conv_b, v_e_w_gate_a, v_e_b_gate_a, v_e_w_gate_x, v_e_b_gate_x, v_e_lru_lambda, v_e_w_out, v_e_ln_g, v_e_ln_b, v_o_w_in, v_o_sinks, v_o_w_out, v_o_ln_g, v_o_ln_b):
    d = D_MODEL
    x0 = x[0]
    target = loss_target[0]
    s = x0.shape[0]

    spack = _small_block([e_conv_w.reshape(512), o_ln_g.reshape(128), o_ln_b.reshape(128)])
    wall_in_e, sall = _exchange([_rows(e_w_in).astype(BF16), spack], [True, True], name="gather_w_in")
    wi_e = wall_in_e.reshape(N_DEV, d, 768).transpose(1, 0, 2).reshape(d, 6 * d)
    conv_w = sall[:, 0, 0:512].reshape(N_DEV, 4, 128).transpose(1, 0, 2).reshape(4, d)
    ln_g_o = sall[:, 0, 512:640].reshape(1, d)
    ln_b_o = sall[:, 0, 640:768].reshape(1, d)
    pvec = jnp.concatenate([conv_w, e_conv_b, e_b_gate_a, e_b_gate_x, e_lru_lambda], axis=0)
    wa16 = e_w_gate_a[0].astype(BF16)
    wx16 = e_w_gate_x[0].astype(BF16)
    sinks = o_sinks[0]

    x0_16 = x0.astype(BF16)
    h0a, wall_out_e = _matmul(x0_16, wi_e, mode="nn", n_out=2 * d, b_off=0, name="l0_in_a",
                              comm=([_rows(e_w_out).astype(BF16)], [True]))
    wpack_o = jnp.concatenate([_rows(o_w_in), _rows(o_w_out)], axis=0).astype(BF16)
    qkv, wall_o = _matmul(x0_16, wi_e, mode="nn", n_out=3 * d, b_off=2 * d, out_dtype=BF16, name="l0_in_qkv",
                          comm=([wpack_o], [True]))
    bg = _matmul(x0_16, wi_e, mode="nn", n_out=d, b_off=5 * d, name="l0_in_bg")
    wo_e = wall_out_e.reshape(2 * d, d)
    wi_o = wall_o[:, 0:R_WIN_O].reshape(N_DEV, d, 288).transpose(1, 0, 2).reshape(d, 2304)
    wi_o = jnp.concatenate([wi_o[:, :1024], wi_o[:, 1280:], wi_o[:, 1024:1280]], axis=1)
    wo_o = wall_o[:, R_WIN_O:R_WIN_O + R_WOUT_O].reshape(d, d)
    ya, hst = _rglru_fwd(h0a, pvec, wa16, wx16)
    ob, yb = _sb_fwd(qkv, bg)
    yab = jnp.concatenate([ya, yb], axis=1)
    z0, x1, x1_16 = _matmul(yab, wo_e, mode="nn", tm=512, tk=2048, name="l0_out",
                            epi=(_epi_ln_fwd, [(x0, 0)], [e_ln_g, e_ln_b], [F32, F32, BF16], 0))

    q1 = _matmul(x1_16, wi_o, mode="nn", n_out=d, b_off=0, out_dtype=BF16, name="l1_in_q")
    g1 = _matmul(x1_16, wi_o, mode="nn", n_out=d, b_off=d, name="l1_in_g")
    kv1 = _matmul(x1_16, wi_o, mode="nn", n_out=256, b_off=2 * d, tn=256, out_dtype=BF16, name="l1_in_kv")
    kvp = jnp.concatenate([jnp.zeros((WINDOW, 256), BF16), kv1], axis=0)
    o1, y1 = _swa_fwd(q1, g1, kvp, sinks)
    dz1, dz1_16, loss_cols, dg_o, db_o = _matmul(
        y1, wo_o, mode="nn", tm=512, name="l1_out",
        epi=(_epi_ln_loss_bwd, [(x1, 0), (target, 0)], [ln_g_o, ln_b_o], [F32, BF16], 3))
    loss_hi = jnp.sum(loss_cols).astype(BF16).astype(F32)
    loss_terms = jnp.stack([loss_hi, jnp.sum(loss_cols) - loss_hi]).reshape(1, 2)

    dwo_o = _matmul(y1, dz1_16, mode="tn", name="l1_dwout")
    do1, dg1 = _matmul(dz1_16, wo_o, mode="nt", tm=512, name="l1_dy",
                       epi=(_epi_gate_bwd, [(o1, 0), (g1, 0)], [], [BF16, BF16], 0))
    dq1, dkvp, dsink = _swa_bwd(q1, do1, kvp, sinks)
    dh1 = jnp.concatenate([dq1, dg1, dkvp[WINDOW:].astype(BF16)], axis=1)
    dwi_o = _matmul(x1_16, dh1, mode="tn", tn=1152, name="l1_dwin")
    dwi_o = jnp.concatenate([dwi_o[:, :1024], dwi_o[:, 2048:], dwi_o[:, 1024:2048]], axis=1)
    seg1 = jnp.concatenate([
        dwi_o.reshape(d, N_DEV, 288).transpose(1, 0, 2).reshape(N_DEV, R_WIN_O, d),
        dwo_o.reshape(N_DEV, R_WOUT_O, d),
        _small_block([dg_o.reshape(N_DEV, 128), db_o.reshape(N_DEV, 128)])], axis=1).astype(BF16)
    dz0, dz0_16, dg_e, db_e, parts1 = _matmul(
        dh1, wi_o, mode="nt", tm=512, tk=2304, name="l1_dx", comm=([seg1], [False]),
        epi=(_epi_ln_bwd, [(dz1, 0), (z0, 0)], [e_ln_g], [F32, BF16], 2))

    dwo_e = _matmul(yab, dz0_16, mode="tn", name="l0_dwout")
    dhst, dag, parts2 = _matmul(
        dz0_16, wo_e, mode="nt", tm=512, n_out=d, b_off=0, name="l0_dy_a",
        comm=([dwo_e.reshape(N_DEV, R_WOUT_E, d).astype(BF16)], [False]),
        epi=(_epi_gate_bwd, [(hst, 0), (h0a, d)], [], [F32, BF16], 0))
    dob, dbg = _matmul(dz0_16, wo_e, mode="nt", tm=512, n_out=d, b_off=d, name="l0_dy_b",
                       epi=(_epi_gate_bwd, [(ob, 0), (bg, 0)], [], [BF16, BF16], 0))
    dq0, dk0, dv0 = _sb_bwd(qkv, dob)
    dax, dwa, dwx, dpv = _rglru_bwd(h0a, hst, dhst, pvec, wa16, wx16)
    dh0 = jnp.concatenate([dax, dag, dq0, dk0.astype(BF16), dv0.astype(BF16), dbg], axis=1)
    dwi_e = _matmul(x0_16, dh0, mode="tn", name="l0_dwin")
    repl = _pack_repl([dwa, dwx, dpv[4:5], dpv[5:6], dpv[6:7], dpv[7:8], dg_e, db_e, dsink[:, :C_HEADS], loss_terms])
    seg3 = jnp.concatenate([
        dwi_e.reshape(d, N_DEV, 768).transpose(1, 0, 2).reshape(N_DEV, R_WIN_E, d),
        _small_block([dpv[0:4].reshape(4, N_DEV, 128).transpose(1, 0, 2).reshape(N_DEV, 512)]),
        repl.reshape(N_DEV, R_REPL, d)], axis=1).astype(BF16)
    grad_x, parts3 = _matmul(dh0, wi_e, mode="nt", tk=2048, res=dz0, alpha=ALPHA, name="l0_dx",
                             comm=([seg3], [False]))

    w_g = _pack_groups(e_w_in, e_w_out, o_w_in, o_w_out, e_conv_w, o_ln_g, o_ln_b)
    m_g = _pack_groups(m_e_w_in, m_e_w_out, m_o_w_in, m_o_w_out, m_e_conv_w, m_o_ln_g, m_o_ln_b)
    v_g = _pack_groups(v_e_w_in, v_e_w_out, v_o_w_in, v_o_w_out, v_e_conv_w, v_o_ln_g, v_o_ln_b)
    res1 = _adamw(parts1, w_g[0], m_g[0], v_g[0], name="adamw_layer1")
    res2 = _adamw(parts2, w_g[1], m_g[1], v_g[1], name="adamw_w_out_e")
    res3 = _adamw(parts3, w_g[2], m_g[2], v_g[2], tr=416, name="adamw_w_in_e")

    (g_repl,) = _exchange([res3[0][R_WIN_E + R_SMALL:]], [True], name="gather_repl_grads")
    g_repl = g_repl.reshape(1, REPL_ROWS, d)
    w_r = _pack_repl([e_w_gate_a, e_w_gate_x, e_conv_b, e_b_gate_a, e_b_gate_x, e_lru_lambda, e_ln_g, e_ln_b, o_sinks])
    m_r = _pack_repl([m_e_w_gate_a, m_e_w_gate_x, m_e_conv_b, m_e_b_gate_a, m_e_b_gate_x, m_e_lru_lambda, m_e_ln_g,
                      m_e_ln_b, m_o_sinks])
    v_r = _pack_repl([v_e_w_gate_a, v_e_w_gate_x, v_e_conv_b, v_e_b_gate_a, v_e_b_gate_x, v_e_lru_lambda, v_e_ln_g,
                      v_e_ln_b, v_o_sinks])
    g_r, d_r, nm_r, nv_r = _adamw(g_repl, w_r, m_r, v_r, name="adamw_replicated")
    loss_at = sum(math.prod(shp) for shp in REPL_SHAPES)
    loss = g_r[loss_at // d, loss_at % d] + g_r[(loss_at + 1) // d, (loss_at + 1) % d]

    def assemble(i, rp):
        w_in_e, w_out_e, w_in_o, w_out_o, cw, lg_o, lb_o = _unpack_groups(res1[i], res2[i], res3[i])
        w_a, w_x, cb, b_a, b_x, lam, lg_e, lb_e, snk = _unpack_repl(rp)
        return [w_in_e, cw, cb, w_a, b_a, w_x, b_x, lam, w_out_e, lg_e, lb_e, w_in_o, snk, w_out_o, lg_o, lb_o]

    return (loss, grad_x[None], *assemble(0, g_r), *assemble(1, d_r), *assemble(2, nm_r), *assemble(3, nv_r))
```

```python
import functools
import math

import jax
import jax.numpy as jnp
from jax import lax
from jax.experimental import pallas as pl
from jax.experimental.pallas import tpu as pltpu

F32 = jnp.float32
BF16 = jnp.bfloat16

N_DEV = 8
D_MODEL = 1024
LRU_BLOCKS = 8
LRU_BLOCK = 128
LRU_C = 8.0
SB_HEADS = 8
SB_HEAD_DIM = 128
C_HEADS = 16
C_KV_HEADS = 2
C_GROUP = 8
C_HEAD_DIM = 64
WINDOW = 128
DEPTH = 2
ALPHA = float((2 * DEPTH) ** 0.25)
LN_EPS = 1e-5
ADAM_LR = 0.001
ADAM_B1 = 0.9
ADAM_B2 = 0.999
ADAM_EPS = 1e-08
ADAM_WD = 0.01
ADAM_STEP = 10

VMEM_LIMIT = 56 * 1024 * 1024

NN = ((1,), (0,))
NT = ((1,), (1,))
TN = ((0,), (0,))


def _dot(a, b, dims):
    return lax.dot_general(a, b, (dims, ((), ())), preferred_element_type=F32)


def _sigmoid(x):
    return 1.0 / (1.0 + jnp.exp(-x))


def _cparams(sem, vmem=VMEM_LIMIT):
    return pltpu.CompilerParams(dimension_semantics=sem, vmem_limit_bytes=vmem)


def _matmul(a, b, *, mode, n_out=None, b_off=0, out_dtype=F32, res=None, alpha=1.0,
            tm=1024, tn=1024, tk=1024, comm=None, epi=None, name):
    if mode == "nn":
        m, k = a.shape
    elif mode == "nt":
        m, k = a.shape
    else:
        k, m = a.shape
    n = n_out if n_out is not None else (b.shape[0] if mode == "nt" else b.shape[1])
    tm, tn, tk = min(tm, m), min(tn, n), min(tk, k)
    assert m % tm == 0 and n % tn == 0 and k % tk == 0 and b_off % tn == 0
    grid = (m // tm, n // tn, k // tk)
    nk = grid[2]
    jo = b_off // tn
    dims = {"nn": NN, "nt": NT, "tn": TN}[mode]
    nc = len(comm[0]) if comm else 0
    if epi:
        assert res is None
        epi_fn, epi_rows, epi_vecs, epi_dtypes, n_sums = epi
        assert n_sums == 0 or grid[1] == 1
    else:
        epi_rows, epi_vecs, epi_dtypes, n_sums = [], [], [out_dtype], 0
    n_in = 2 + (res is not None) + len(epi_rows) + len(epi_vecs)
    n_res = len(epi_dtypes) + n_sums

    def body(*refs):
        a_ref, b_ref = refs[:2]
        r_ref = refs[2] if res is not None else None
        row_refs = refs[n_in - len(epi_rows) - len(epi_vecs):n_in - len(epi_vecs)]
        vec_refs = refs[n_in - len(epi_vecs):n_in]
        o_refs = refs[n_in + nc:n_in + nc + len(epi_dtypes)]
        sum_refs = refs[n_in + nc + len(epi_dtypes):n_in + nc + n_res]
        if comm:
            step = [pl.program_id(ax) for ax in range(3)]
            c_src = refs[n_in:n_in + nc]
            c_dst = refs[n_in + nc + n_res:n_in + 2 * nc + n_res]
            sems = refs[len(refs) - 3:]

            @pl.when((step[0] == 0) & (step[1] == 0) & (step[2] == 0))
            def _():
                _exchange_start(c_src, c_dst, comm[1], sems)

        part = _dot(a_ref[...].astype(BF16), b_ref[...].astype(BF16), dims)

        def finish(out):
            if not epi:
                if r_ref is not None:
                    out = out + alpha * r_ref[...]
                o_refs[0][...] = out.astype(out_dtype)
                return
            outs, sums = epi_fn(out, [r[...] for r in row_refs], [v[...] for v in vec_refs])
            for o_ref, o in zip(o_refs, outs):
                o_ref[...] = o.astype(o_ref.dtype)
            if sum_refs:
                first_rows = pl.program_id(0) == 0

                @pl.when(first_rows)
                def _():
                    for s_ref, part_sum in zip(sum_refs, sums):
                        s_ref[...] = part_sum

                @pl.when(jnp.logical_not(first_rows))
                def _():
                    for s_ref, part_sum in zip(sum_refs, sums):
                        s_ref[...] += part_sum

        if nk == 1:
            finish(part)
        else:
            acc_ref = refs[n_in + 2 * nc + n_res]
            kk = pl.program_id(2)

            @pl.when(kk == 0)
            def _():
                acc_ref[...] = part

            @pl.when(kk > 0)
            def _():
                acc_ref[...] += part

            @pl.when(kk == nk - 1)
            def _():
                finish(acc_ref[...])

        if comm:
            @pl.when((step[0] == grid[0] - 1) & (step[1] == grid[1] - 1) & (step[2] == grid[2] - 1))
            def _():
                _exchange_wait(c_src, c_dst, comm[1], sems)

    if mode == "tn":
        a_spec = pl.BlockSpec((tk, tm), lambda i, j, kk: (kk, i))
    else:
        a_spec = pl.BlockSpec((tm, tk), lambda i, j, kk: (i, kk))
    if mode == "nt":
        b_spec = pl.BlockSpec((tn, tk), lambda i, j, kk: (j + jo, kk))
    else:
        b_spec = pl.BlockSpec((tk, tn), lambda i, j, kk: (kk, j + jo))
    o_spec = pl.BlockSpec((tm, tn), lambda i, j, kk: (i, j))
    hbm = pl.BlockSpec(memory_space=pl.ANY)
    row_specs = [pl.BlockSpec((tm, tn), functools.partial(lambda i, j, kk, o: (i, j + o), o=off // tn))
                 for (_, off) in epi_rows]
    vec_specs = [pl.BlockSpec(v.shape, functools.partial(lambda i, j, kk, nd: (0,) * nd, nd=v.ndim)) for v in epi_vecs]
    in_specs = [a_spec, b_spec] + ([o_spec] if res is not None else []) + row_specs + vec_specs + [hbm] * nc
    args = ([a, b] + ([res] if res is not None else []) + [r for (r, _) in epi_rows] + list(epi_vecs)
            + (list(comm[0]) if comm else []))
    out_specs = [o_spec] * len(epi_dtypes) + [pl.BlockSpec((1, tn), lambda i, j, kk: (0, j))] * n_sums + [hbm] * nc
    out_shape = ([jax.ShapeDtypeStruct((m, n), dt) for dt in epi_dtypes] + [jax.ShapeDtypeStruct((1, n), F32)] * n_sums
                 + (_exchange_out_shapes(*comm) if comm else []))
    scratch = ([pltpu.VMEM((tm, tn), F32)] if nk > 1 else []) + (_exchange_sems(nc) if comm else [])
    if comm or n_sums:
        params = pltpu.CompilerParams(dimension_semantics=("arbitrary",) * 3, vmem_limit_bytes=VMEM_LIMIT,
                                      has_side_effects=bool(comm))
    else:
        params = _cparams(("parallel", "parallel", "arbitrary"))
    outs = pl.pallas_call(
        body, name=name, grid=grid, in_specs=in_specs, out_specs=out_specs, out_shape=out_shape,
        scratch_shapes=scratch, compiler_params=params,
    )(*args)
    return tuple(outs) if (comm or epi) else outs[0]


def _ln_stats(z):
    mu = jnp.mean(z, axis=-1, keepdims=True)
    zc = z - mu
    var = jnp.mean(zc * zc, axis=-1, keepdims=True)
    rstd = lax.rsqrt(var + LN_EPS)
    return zc * rstd, rstd


def _ln_bwd(dy, xhat, rstd, g):
    dxh = dy * g
    m1 = jnp.mean(dxh, axis=-1, keepdims=True)
    m2 = jnp.mean(dxh * xhat, axis=-1, keepdims=True)
    return rstd * (dxh - m1 - xhat * m2)


def _colsum(x):
    return jnp.sum(x, axis=0, keepdims=True)


def _epi_ln_fwd(acc, rows, vecs):
    z = ALPHA * rows[0] + acc
    xhat, _ = _ln_stats(z)
    x1 = xhat * vecs[0] + vecs[1]
    return [z, x1, x1], []


def _epi_ln_loss_bwd(acc, rows, vecs):
    inv_c = 1.0 / acc.shape[-1]
    z = ALPHA * rows[0] + acc
    xhat, rstd = _ln_stats(z)
    d = xhat * vecs[0] + vecs[1] - rows[1]
    dy = d * inv_c
    dz = _ln_bwd(dy, xhat, rstd, vecs[0])
    return [dz, dz], [_colsum(0.5 * d * d * inv_c), _colsum(dy * xhat), _colsum(dy)]


def _epi_ln_bwd(acc, rows, vecs):
    dy = acc + ALPHA * rows[0]
    xhat, rstd = _ln_stats(rows[1])
    dz = _ln_bwd(dy, xhat, rstd, vecs[0])
    return [dz, dz], [_colsum(dy * xhat), _colsum(dy)]


def _epi_gate_bwd(acc, rows, vecs):
    oo, gg = rows
    sg = _sigmoid(gg)
    return [acc * (gg * sg), acc * oo * (sg * (1.0 + gg * (1.0 - sg)))], []


LRU_T = 256
HALO = 8


def _log1p(y):
    u = 1.0 + y
    return jnp.where(u == 1.0, y, jnp.log(u) * (y / (u - 1.0)))


def _lru_gates(c, wa_ref, wx_ref, pv):
    c16 = c.astype(BF16)
    pre_r = jnp.concatenate(
        [_dot(c16[:, n * LRU_BLOCK:(n + 1) * LRU_BLOCK], wa_ref[n], NN) for n in range(LRU_BLOCKS)], axis=1)
    pre_i = jnp.concatenate(
        [_dot(c16[:, n * LRU_BLOCK:(n + 1) * LRU_BLOCK], wx_ref[n], NN) for n in range(LRU_BLOCKS)], axis=1)
    r = _sigmoid(pre_r + pv[5:6])
    ig = _sigmoid(pre_i + pv[6:7])
    lam = pv[7:8]
    ls = jnp.minimum(lam, 0.0) - _log1p(jnp.exp(-jnp.abs(lam)))
    la = LRU_C * r * ls
    a = jnp.exp(la)
    a2 = a * a
    m = jnp.sqrt(jnp.tanh(-la) * (a2 + 1.0))
    return c16, r, ig, ls, la, a, a2, m


def _conv(ext_ref, x, pv, t):
    return (pv[4:5] + pv[3:4] * x + pv[2:3] * ext_ref[pl.ds(HALO - 1, t), :]
            + pv[1:2] * ext_ref[pl.ds(HALO - 2, t), :] + pv[0:1] * ext_ref[pl.ds(HALO - 3, t), :])


def _rglru_fwd(h0a, pvec, wa16, wx16):
    s = h0a.shape[0]
    w = D_MODEL
    t = min(LRU_T, s)
    assert s % t == 0

    def body(ax_ref, ag_ref, pv_ref, wa_ref, wx_ref, ya_ref, h_ref, ext_ref, a_ref, u_ref, hc_ref):
        i = pl.program_id(0)

        @pl.when(i == 0)
        def _():
            ext_ref[pl.ds(0, HALO), :] = jnp.zeros((HALO, w), F32)
            hc_ref[...] = jnp.zeros((1, w), F32)

        pv = pv_ref[...]
        ax = ax_ref[...]
        ext_ref[pl.ds(HALO, t), :] = ax
        c = _conv(ext_ref, ax, pv, t)
        ext_ref[pl.ds(0, HALO), :] = ax[t - HALO:, :]
        _, _, ig, _, _, a, _, m = _lru_gates(c, wa_ref, wx_ref, pv)
        a_ref[...] = a
        u_ref[...] = m * (ig * c)

        def step(k, h):
            h = a_ref[pl.ds(k, 1), :] * h + u_ref[pl.ds(k, 1), :]
            h_ref[pl.ds(k, 1), :] = h
            return h

        hc_ref[...] = lax.fori_loop(0, t, step, hc_ref[...], unroll=8)
        ag = ag_ref[...]
        ya_ref[...] = (h_ref[...] * (ag * _sigmoid(ag))).astype(BF16)

    full = lambda shp: pl.BlockSpec(shp, lambda i: (0,) * len(shp))
    return pl.pallas_call(
        body, name="rglru_fwd", grid=(s // t,),
        in_specs=[pl.BlockSpec((t, w), lambda i: (i, 0)), pl.BlockSpec((t, w), lambda i: (i, 1)),
                  full((8, w)), full((LRU_BLOCKS, LRU_BLOCK, LRU_BLOCK)), full((LRU_BLOCKS, LRU_BLOCK, LRU_BLOCK))],
        out_specs=[pl.BlockSpec((t, w), lambda i: (i, 0)), pl.BlockSpec((t, w), lambda i: (i, 0))],
        out_shape=[jax.ShapeDtypeStruct((s, w), BF16), jax.ShapeDtypeStruct((s, w), F32)],
        scratch_shapes=[pltpu.VMEM((t + HALO, w), F32), pltpu.VMEM((t, w), F32), pltpu.VMEM((t, w), F32),
                        pltpu.VMEM((1, w), F32)],
        compiler_params=_cparams(("arbitrary",)),
    )(h0a, h0a, pvec, wa16, wx16)


def _rglru_bwd(h0a, h, dh, pvec, wa16, wx16):
    s = h0a.shape[0]
    w = D_MODEL
    t = min(LRU_T, s)
    nb = s // t
    hb = t // HALO

    def body(ax_ref, axh_ref, h_ref, hh_ref, dh_ref, pv_ref, wa_ref, wx_ref,
             dax_ref, dwa_ref, dwx_ref, dpv_ref, ext_ref, hext_ref, dcext_ref, a_ref, g_ref, gc_ref):
        i = pl.program_id(0)
        blk = nb - 1 - i

        @pl.when(i == 0)
        def _():
            dwa_ref[...] = jnp.zeros_like(dwa_ref)
            dwx_ref[...] = jnp.zeros_like(dwx_ref)
            dpv_ref[...] = jnp.zeros_like(dpv_ref)
            gc_ref[...] = jnp.zeros((1, w), F32)
            dcext_ref[pl.ds(t, HALO), :] = jnp.zeros((HALO, w), F32)

        pv = pv_ref[...]
        ax = ax_ref[...]
        keep = (blk > 0).astype(F32)
        ext_ref[pl.ds(0, HALO), :] = axh_ref[...] * keep
        ext_ref[pl.ds(HALO, t), :] = ax
        hext_ref[pl.ds(0, HALO), :] = hh_ref[...] * keep
        hext_ref[pl.ds(HALO, t), :] = h_ref[...]
        c = _conv(ext_ref, ax, pv, t)
        c16, r, ig, ls, _, a, a2, m = _lru_gates(c, wa_ref, wx_ref, pv)
        a_ref[...] = a

        def step(k, carry):
            row = t - 1 - k
            g = dh_ref[pl.ds(row, 1), :] + carry
            g_ref[pl.ds(row, 1), :] = g
            return a_ref[pl.ds(row, 1), :] * g

        gc_ref[...] = lax.fori_loop(0, t, step, gc_ref[...], unroll=8)
        g = g_ref[...]
        hprev = hext_ref[pl.ds(HALO - 1, t), :]
        gm = g * m
        d_la = g * hprev * a - (g * ig * c) * a2 / m
        d_pr = d_la * (LRU_C * ls) * r * (1.0 - r)
        d_pi = gm * c * ig * (1.0 - ig)
        dc = gm * ig
        dpr16 = d_pr.astype(BF16)
        dpi16 = d_pi.astype(BF16)
        dc_parts = []
        for n in range(LRU_BLOCKS):
            sl = slice(n * LRU_BLOCK, (n + 1) * LRU_BLOCK)
            dwa_ref[n] += _dot(c16[:, sl], dpr16[:, sl], TN)
            dwx_ref[n] += _dot(c16[:, sl], dpi16[:, sl], TN)
            dc_parts.append(_dot(dpr16[:, sl], wa_ref[n], NT) + _dot(dpi16[:, sl], wx_ref[n], NT))
        dc = dc + jnp.concatenate(dc_parts, axis=1)
        dcext_ref[pl.ds(0, t), :] = dc
        dax = (pv[3:4] * dc + pv[2:3] * dcext_ref[pl.ds(1, t), :] + pv[1:2] * dcext_ref[pl.ds(2, t), :]
               + pv[0:1] * dcext_ref[pl.ds(3, t), :])
        dax_ref[...] = dax.astype(BF16)
        dcext_ref[pl.ds(t, HALO), :] = dc[:HALO, :]
        sums = [_colsum(dc * ext_ref[pl.ds(HALO - 3 + kk, t), :]) for kk in range(4)]
        sums += [_colsum(dc), _colsum(d_pr), _colsum(d_pi), _colsum(d_la * (LRU_C * r))]
        for kk, part in enumerate(sums):
            dpv_ref[pl.ds(kk, 1), :] += part

        @pl.when(i == nb - 1)
        def _():
            lam = pv[7:8]
            dls = dpv_ref[pl.ds(7, 1), :]
            dpv_ref[pl.ds(7, 1), :] = dls * _sigmoid(-lam)

    full = lambda shp: pl.BlockSpec(shp, lambda i: (0,) * len(shp))
    rev = lambda cb: pl.BlockSpec((t, w), functools.partial(lambda i, cb: (nb - 1 - i, cb), cb=cb))
    halo = lambda cb: pl.BlockSpec(
        (HALO, w), functools.partial(lambda i, cb: (jnp.maximum((nb - 1 - i) * hb - 1, 0), cb), cb=cb))
    gw = (LRU_BLOCKS, LRU_BLOCK, LRU_BLOCK)
    return pl.pallas_call(
        body, name="rglru_bwd", grid=(nb,),
        in_specs=[rev(0), halo(0), rev(0), halo(0), rev(0), full((8, w)), full(gw), full(gw)],
        out_specs=[rev(0), full(gw), full(gw), full((8, w))],
        out_shape=[jax.ShapeDtypeStruct((s, w), BF16), jax.ShapeDtypeStruct(gw, F32),
                   jax.ShapeDtypeStruct(gw, F32), jax.ShapeDtypeStruct((8, w), F32)],
        scratch_shapes=[pltpu.VMEM((t + HALO, w), F32), pltpu.VMEM((t + HALO, w), F32),
                        pltpu.VMEM((t + HALO, w), F32), pltpu.VMEM((t, w), F32), pltpu.VMEM((t, w), F32),
                        pltpu.VMEM((1, w), F32)],
        compiler_params=_cparams(("arbitrary",)),
    )(h0a, h0a, h, h, dh, pvec, wa16, wx16)


SB_T = 256


def _split16(x):
    hi = x.astype(BF16)
    lo = (x - hi.astype(F32)).astype(BF16)
    return jnp.concatenate([hi, lo], axis=0)


def _sb_tile(q, k, scale, tri, run, causal):
    tq = q.shape[0]
    z = _dot(q, k, NT) * scale
    log1mb = -(jnp.maximum(z, 0.0) + jnp.log(1.0 + jnp.exp(-jnp.abs(z))))
    if causal is not None:
        log1mb = jnp.where(causal, log1mb, 0.0)
    cs = _dot(_split16(log1mb), tri, NN)
    cs = cs[:tq] + cs[tq:]
    wgt = jnp.exp(z + cs + run)
    if causal is not None:
        wgt = jnp.where(causal, wgt, 0.0)
    return z, wgt, run + cs[:, 0:1]


SB_DEAD = -105.0


def _sb_alive(run):
    return (jnp.max(run) > SB_DEAD).astype(jnp.int32)


def _sb_more(carry):
    return (carry[0] >= 0) & (carry[1] > 0)


def _sb_fwd(qkv, bg):
    s = qkv.shape[0]
    t = min(SB_T, s)
    nq = s // t
    dh = SB_HEAD_DIM
    scale = 1.0 / math.sqrt(dh)

    def body(q_ref, k_ref, v_ref, bg_ref, o_ref, y_ref):
        iq = pl.program_id(1)
        q = q_ref[...]
        row = lax.broadcasted_iota(jnp.int32, (t, t), 0)
        col = lax.broadcasted_iota(jnp.int32, (t, t), 1)
        tri = jnp.where(row >= col, 1.0, 0.0).astype(BF16)
        causal = col < row

        def tile(kb, run, mask):
            off = pl.multiple_of(kb * t, t)
            _, wgt, run = _sb_tile(q, k_ref[pl.ds(off, t), :], scale, tri, run, mask)
            return run, _dot(wgt.astype(BF16), v_ref[pl.ds(off, t), :], NN)

        has_left = iq > 0
        run_d, acc_d = tile(iq, jnp.zeros((t, 1), F32), causal)
        run_l, acc_l = tile(jnp.maximum(iq - 1, 0), run_d, None)
        run = jnp.where(has_left, run_l, run_d)
        acc = acc_d + jnp.where(has_left, acc_l, 0.0)

        def loop(carry):
            kb, _, run, acc = carry
            run, part = tile(kb, run, None)
            return kb - 1, _sb_alive(run), run, acc + part

        _, _, run, acc = lax.while_loop(_sb_more, loop, (iq - 2, _sb_alive(run), run, acc))
        o_ref[...] = acc
        g = bg_ref[...]
        y_ref[...] = (acc * (g * _sigmoid(g))).astype(BF16)

    blk = lambda off: pl.BlockSpec((t, dh), functools.partial(lambda h, i, off: (i, h + off), off=off))
    col = lambda off: pl.BlockSpec((s, dh), functools.partial(lambda h, i, off: (0, h + off), off=off))
    return pl.pallas_call(
        body, name="sb_fwd", grid=(SB_HEADS, nq),
        in_specs=[blk(0), col(SB_HEADS), col(2 * SB_HEADS), blk(0)],
        out_specs=[blk(0), blk(0)],
        out_shape=[jax.ShapeDtypeStruct((s, SB_HEADS * dh), F32), jax.ShapeDtypeStruct((s, SB_HEADS * dh), BF16)],
        compiler_params=_cparams(("parallel", "arbitrary")),
    )(qkv, qkv, qkv, bg)


def _sb_bwd(qkv, dob):
    s = qkv.shape[0]
    t = min(SB_T, s)
    nq = s // t
    dh = SB_HEAD_DIM
    scale = 1.0 / math.sqrt(dh)

    def body(q_ref, k_ref, v_ref, do_ref, dq_ref, dk_ref, dv_ref, e_ref, b_ref):
        iq = pl.program_id(1)

        @pl.when(iq == 0)
        def _():
            dk_ref[...] = jnp.zeros_like(dk_ref)
            dv_ref[...] = jnp.zeros_like(dv_ref)

        q = q_ref[...]
        do = do_ref[...]
        row = lax.broadcasted_iota(jnp.int32, (t, t), 0)
        col = lax.broadcasted_iota(jnp.int32, (t, t), 1)
        tri = jnp.where(row >= col, 1.0, 0.0).astype(BF16)
        tri_x = jnp.where(row < col, 1.0, 0.0).astype(BF16)
        causal = col < row

        def sweep1(kb, slot, run, mask, live=None):
            off = pl.multiple_of(kb * t, t)
            v = v_ref[pl.ds(off, t), :]
            z, wgt, run = _sb_tile(q, k_ref[pl.ds(off, t), :], scale, tri, run, mask)
            if live is not None:
                wgt = jnp.where(live, wgt, 0.0)
            e_ref[slot] = wgt * _dot(do, v, NT)
            b_ref[slot] = _sigmoid(z)
            dv_ref[pl.ds(off, t), :] += _dot(wgt.astype(BF16), do, TN)
            return run

        has_left = iq > 0
        left = jnp.maximum(iq - 1, 0)
        left_slot = jnp.where(has_left, iq - 1, nq)
        run_d = sweep1(iq, iq, jnp.zeros((t, 1), F32), causal)
        run = jnp.where(has_left, sweep1(left, left_slot, run_d, None, has_left), run_d)

        def loop1(carry):
            run = sweep1(carry[0], carry[0], carry[2], None)
            return carry[0] - 1, _sb_alive(run), run

        first = lax.while_loop(_sb_more, loop1, (iq - 2, _sb_alive(run), run))[0] + 1

        def sweep2(kb, slot, pre, mask, live=None):
            off = pl.multiple_of(kb * t, t)
            e = e_ref[slot]
            beta = b_ref[slot]
            ps = _dot(_split16(e), tri_x, NN)
            ps = ps[:t] + ps[t:]
            dz = e * (1.0 - beta) - beta * (ps + pre)
            if mask is not None:
                dz = jnp.where(mask, dz, 0.0)
            if live is not None:
                dz = jnp.where(live, dz, 0.0)
            dz16 = (dz * scale).astype(BF16)
            dk_ref[pl.ds(off, t), :] += _dot(dz16, q, TN)
            return pre + ps[:, t - 1:t] + e[:, t - 1:t], _dot(dz16, k_ref[pl.ds(off, t), :], NN)

        def loop2(kb, carry):
            pre, part = sweep2(kb, kb, carry[0], None)
            return pre, carry[1] + part

        pre, dq = lax.fori_loop(first, iq - 1, loop2, (jnp.zeros((t, 1), F32), jnp.zeros((t, dh), F32)))
        pre, dq_l = sweep2(left, left_slot, pre, None, has_left)
        _, dq_d = sweep2(iq, iq, pre, causal)
        dq_ref[...] = (dq + dq_l + dq_d).astype(BF16)

    blk = lambda off: pl.BlockSpec((t, dh), functools.partial(lambda h, i, off: (i, h + off), off=off))
    col = lambda off: pl.BlockSpec((s, dh), functools.partial(lambda h, i, off: (0, h + off), off=off))
    wide = SB_HEADS * dh
    return pl.pallas_call(
        body, name="sb_bwd", grid=(SB_HEADS, nq),
        in_specs=[blk(0), col(SB_HEADS), col(2 * SB_HEADS), blk(0)],
        out_specs=[blk(0), col(0), col(0)],
        out_shape=[jax.ShapeDtypeStruct((s, wide), BF16), jax.ShapeDtypeStruct((s, wide), F32),
                   jax.ShapeDtypeStruct((s, wide), F32)],
        scratch_shapes=[pltpu.VMEM((nq + 1, t, t), F32), pltpu.VMEM((nq + 1, t, t), F32)],
        compiler_params=_cparams(("parallel", "arbitrary")),
    )(qkv, qkv, qkv, dob)


def _alibi_slope(h):
    return float(2.0 ** (-8.0 * (h + 1) / C_HEADS))


GROUP_ROWS = C_GROUP * WINDOW


def _swa_group_consts(n, c, sink_ref):
    qb = WINDOW
    head = lax.shift_right_logical(lax.broadcasted_iota(jnp.int32, (GROUP_ROWS, 1), 0), qb.bit_length() - 1)
    slope = jnp.zeros((GROUP_ROWS, 1), F32)
    sink = jnp.zeros((GROUP_ROWS, 1), F32)
    for hh in range(C_GROUP):
        slope = jnp.where(head == hh, _alibi_slope(c * C_GROUP + hh), slope)
        sink = jnp.where(head == hh, sink_ref[c * C_GROUP + hh], sink)
    i = lax.broadcasted_iota(jnp.int32, (GROUP_ROWS, 2 * qb), 0) & (qb - 1)
    j = lax.broadcasted_iota(jnp.int32, (GROUP_ROWS, 2 * qb), 1)
    d = i - j + qb
    valid = (d >= 0) & (d < WINDOW) & ((j >= qb) | (n > 0))
    return slope, sink, d.astype(F32), valid


def _swa_probs(qg, kw, sink, slope, dist, valid, scale):
    sc = _dot(qg, kw, NT) * scale - slope * dist
    sc = jnp.where(valid, sc, -1e30)
    m = jnp.maximum(jnp.max(sc, axis=-1, keepdims=True), sink)
    p = jnp.exp(sc - m)
    ps = jnp.exp(sink - m)
    inv = 1.0 / (jnp.sum(p, axis=-1, keepdims=True) + ps)
    return p * inv, ps * inv


def _stack_heads(x, c):
    hd = C_HEAD_DIM
    return jnp.concatenate([x[:, (c * C_GROUP + hh) * hd:(c * C_GROUP + hh + 1) * hd] for hh in range(C_GROUP)], axis=0)


def _swa_fwd(q16, gate, kvp, sinks):
    s = q16.shape[0]
    qb = WINDOW
    hd = C_HEAD_DIM
    scale = 1.0 / math.sqrt(hd)
    kvw = C_KV_HEADS * hd

    def body(sink_ref, q_ref, g_ref, kp_ref, ko_ref, o_ref, y_ref):
        n = pl.program_id(0)
        q = q_ref[...]
        kv = jnp.concatenate([kp_ref[...], ko_ref[...]], axis=0)
        for c in range(C_KV_HEADS):
            slope, sink, dist, valid = _swa_group_consts(n, c, sink_ref)
            kw = kv[:, c * hd:(c + 1) * hd]
            vw = kv[:, kvw + c * hd:kvw + (c + 1) * hd]
            p, _ = _swa_probs(_stack_heads(q, c), kw, sink, slope, dist, valid, scale)
            og = _dot(p.astype(BF16), vw, NN)
            for hh in range(C_GROUP):
                h = c * C_GROUP + hh
                o_ref[:, h * hd:(h + 1) * hd] = og[hh * qb:(hh + 1) * qb, :]
        g = g_ref[...]
        y_ref[...] = (o_ref[...] * (g * _sigmoid(g))).astype(BF16)

    wide = C_HEADS * hd
    return pl.pallas_call(
        body, name="swa_fwd", grid=(s // qb,),
        in_specs=[pl.BlockSpec(memory_space=pltpu.SMEM),
                  pl.BlockSpec((qb, wide), lambda n: (n, 0)), pl.BlockSpec((qb, wide), lambda n: (n, 0)),
                  pl.BlockSpec((qb, 2 * kvw), lambda n: (n, 0)), pl.BlockSpec((qb, 2 * kvw), lambda n: (n + 1, 0))],
        out_specs=[pl.BlockSpec((qb, wide), lambda n: (n, 0)), pl.BlockSpec((qb, wide), lambda n: (n, 0))],
        out_shape=[jax.ShapeDtypeStruct((s, wide), F32), jax.ShapeDtypeStruct((s, wide), BF16)],
        compiler_params=_cparams(("arbitrary",)),
    )(sinks, q16, gate, kvp, kvp)


def _swa_bwd(q16, do16, kvp, sinks):
    s = q16.shape[0]
    qb = WINDOW
    hd = C_HEAD_DIM
    scale = 1.0 / math.sqrt(hd)
    kvw = C_KV_HEADS * hd
    nblk = s // qb

    def body(sink_ref, q_ref, do_ref, kp_ref, ko_ref, dq_ref, dkv_ref, ds_ref, sacc_ref):
        n = pl.program_id(0)

        @pl.when(n == 0)
        def _():
            dkv_ref[...] = jnp.zeros_like(dkv_ref)
            sacc_ref[...] = jnp.zeros_like(sacc_ref)

        q = q_ref[...]
        do = do_ref[...]
        kv = jnp.concatenate([kp_ref[...], ko_ref[...]], axis=0)
        off = pl.multiple_of(n * qb, qb)
        for c in range(C_KV_HEADS):
            slope, sink, dist, valid = _swa_group_consts(n, c, sink_ref)
            kw = kv[:, c * hd:(c + 1) * hd]
            vw = kv[:, kvw + c * hd:kvw + (c + 1) * hd]
            qg = _stack_heads(q, c)
            dog = _stack_heads(do, c)
            p, ps = _swa_probs(qg, kw, sink, slope, dist, valid, scale)
            dp = _dot(dog, vw, NT)
            dd = jnp.sum(p * dp, axis=-1, keepdims=True)
            ds16 = (p * (dp - dd) * scale).astype(BF16)
            sacc_ref[c] += ps * dd
            dqg = _dot(ds16, kw, NN).astype(BF16)
            for hh in range(C_GROUP):
                h = c * C_GROUP + hh
                dq_ref[:, h * hd:(h + 1) * hd] = dqg[hh * qb:(hh + 1) * qb, :]
            dkv_ref[pl.ds(off, 2 * qb), c * hd:(c + 1) * hd] += _dot(ds16, qg, TN)
            dkv_ref[pl.ds(off, 2 * qb), kvw + c * hd:kvw + (c + 1) * hd] += _dot(p.astype(BF16), dog, TN)

        @pl.when(n == nblk - 1)
        def _():
            lane = lax.broadcasted_iota(jnp.int32, (1, 128), 1)
            row = jnp.zeros((1, 128), F32)
            for c in range(C_KV_HEADS):
                for hh in range(C_GROUP):
                    tot = jnp.sum(sacc_ref[c, pl.ds(hh * qb, qb), :], axis=0, keepdims=True)
                    row = jnp.where(lane == c * C_GROUP + hh, -tot, row)
            ds_ref[...] = row

    wide = C_HEADS * hd
    return pl.pallas_call(
        body, name="swa_bwd", grid=(nblk,),
        in_specs=[pl.BlockSpec(memory_space=pltpu.SMEM),
                  pl.BlockSpec((qb, wide), lambda n: (n, 0)), pl.BlockSpec((qb, wide), lambda n: (n, 0)),
                  pl.BlockSpec((qb, 2 * kvw), lambda n: (n, 0)), pl.BlockSpec((qb, 2 * kvw), lambda n: (n + 1, 0))],
        out_specs=[pl.BlockSpec((qb, wide), lambda n: (n, 0)),
                   pl.BlockSpec((s + qb, 2 * kvw), lambda n: (0, 0)), pl.BlockSpec((1, 128), lambda n: (0, 0))],
        out_shape=[jax.ShapeDtypeStruct((s, wide), BF16), jax.ShapeDtypeStruct((s + qb, 2 * kvw), F32),
                   jax.ShapeDtypeStruct((1, 128), F32)],
        scratch_shapes=[pltpu.VMEM((C_KV_HEADS, GROUP_ROWS, 1), F32)],
        compiler_params=_cparams(("arbitrary",)),
    )(sinks, q16, do16, kvp, kvp)


def _adamw(parts, w, m, v, *, name, tr=496):
    npart, r, c = parts.shape
    tr = min(tr, r)
    assert r % tr == 0 and tr % 16 == 0
    c1 = 1.0 / (1.0 - ADAM_B1 ** ADAM_STEP)
    c2 = 1.0 / (1.0 - ADAM_B2 ** ADAM_STEP)

    def body(p_ref, w_ref, m_ref, v_ref, g_ref, d_ref, nm_ref, nv_ref):
        g = p_ref[0].astype(F32)
        for j in range(1, npart):
            g = g + p_ref[j].astype(F32)
        nm = ADAM_B1 * m_ref[...] + (1.0 - ADAM_B1) * g
        nv = ADAM_B2 * v_ref[...] + (1.0 - ADAM_B2) * (g * g)
        g_ref[...] = g
        nm_ref[...] = nm
        nv_ref[...] = nv
        d_ref[...] = -ADAM_LR * ((nm * c1) / (jnp.sqrt(nv * c2) + ADAM_EPS) + ADAM_WD * w_ref[...])

    spec = pl.BlockSpec((tr, c), lambda i: (i, 0))
    return pl.pallas_call(
        body, name=name, grid=(r // tr,),
        in_specs=[pl.BlockSpec((npart, tr, c), lambda i: (0, i, 0)), spec, spec, spec],
        out_specs=[spec] * 4, out_shape=[jax.ShapeDtypeStruct((r, c), F32)] * 4,
        compiler_params=_cparams(("parallel",)),
    )(parts, w, m, v)


GATHER = "gather"
A2A = "a2a"
GATHER_COLS = "gather_cols"
A2A_COLS = "a2a_cols"


def _exchange(bufs, gather, *, name):
    nb = len(bufs)

    def body(*refs):
        _exchange_start(refs[:nb], refs[nb:2 * nb], gather, refs[2 * nb:])
        _exchange_wait(refs[:nb], refs[nb:2 * nb], gather, refs[2 * nb:])

    hbm = pl.BlockSpec(memory_space=pl.ANY)
    return pl.pallas_call(
        body, name=name, in_specs=[hbm] * nb, out_specs=[hbm] * nb, out_shape=_exchange_out_shapes(bufs, gather),
        scratch_shapes=_exchange_sems(nb),
        compiler_params=pltpu.CompilerParams(has_side_effects=True),
    )(*bufs)


def _exchange_out_shapes(bufs, gather):
    shapes = {GATHER: lambda s: (N_DEV,) + s, A2A: lambda s: s, GATHER_COLS: lambda s: (s[0], N_DEV * s[1]),
              A2A_COLS: lambda s: (N_DEV, s[0], s[1] // N_DEV)}
    return [jax.ShapeDtypeStruct(shapes[g](tuple(b.shape)), b.dtype) for b, g in zip(bufs, gather)]


def _sent(src, mode, peer):
    if mode == A2A:
        return src.at[peer]
    if mode == A2A_COLS:
        w = src.shape[1] // N_DEV
        return src.at[:, pl.ds(pl.multiple_of(peer * w, 128), w)]
    return src


def _slot(dst, mode, dev):
    if mode == GATHER_COLS:
        w = dst.shape[1] // N_DEV
        return dst.at[:, pl.ds(pl.multiple_of(dev * w, 128), w)]
    return dst.at[dev]


def _exchange_sems(nb):
    return [pltpu.SemaphoreType.DMA((nb, N_DEV - 1)), pltpu.SemaphoreType.DMA((nb, N_DEV - 1)),
            pltpu.SemaphoreType.DMA((nb,))]


def _exchange_copies(src, dst, gather, sems):
    send_sems, recv_sems, local_sems = sems
    x, y, c = lax.axis_index("x"), lax.axis_index("y"), lax.axis_index("c")
    me = 4 * x + 2 * y + c
    local, sends, recvs = [], [], []
    for b in range(len(src)):
        mine = _sent(src[b], gather[b], me)
        local.append(pltpu.make_async_copy(mine, _slot(dst[b], gather[b], me), local_sems.at[b]))
        for k in range(1, N_DEV):
            px, py, pc = x ^ (k >> 2), y ^ ((k >> 1) & 1), c ^ (k & 1)
            peer = 4 * px + 2 * py + pc
            pair = dict(send_sem=send_sems.at[b, k - 1], recv_sem=recv_sems.at[b, k - 1],
                        device_id_type=pl.DeviceIdType.MESH)
            sends.append(pltpu.make_async_remote_copy(
                src_ref=_sent(src[b], gather[b], peer), dst_ref=_slot(dst[b], gather[b], me), device_id=(px, py, pc),
                **pair))
            recvs.append(pltpu.make_async_remote_copy(
                src_ref=mine, dst_ref=_slot(dst[b], gather[b], peer), device_id=(x, y, c), **pair))
    return local, sends, recvs


def _exchange_start(src, dst, gather, sems):
    local, sends, _ = _exchange_copies(src, dst, gather, sems)
    for cp in local + sends:
        cp.start()


def _exchange_wait(src, dst, gather, sems):
    local, sends, recvs = _exchange_copies(src, dst, gather, sems)
    for cp in sends:
        cp.wait_send()
    for cp in recvs:
        cp.wait_recv()
    for cp in local:
        cp.wait()


R_SMALL, R_REPL = 16, 48
REPL_ROWS = N_DEV * R_REPL
REPL_LEN = REPL_ROWS * D_MODEL


def _small_block(parts):
    flat = jnp.concatenate(parts, axis=-1)
    lead = flat.ndim - 1
    return jnp.pad(flat[..., None, :], [(0, 0)] * lead + [(0, R_SMALL - 1), (0, D_MODEL - flat.shape[-1])])


def _small_vectors(conv_w, ln_g, ln_b):
    block = _small_block([conv_w.reshape(512), ln_g.reshape(128), ln_b.reshape(128)])
    return jnp.concatenate([block, jnp.zeros((R_REPL, D_MODEL), F32)], axis=0)


REPL_SHAPES = ((1, 8, 128, 128), (1, 8, 128, 128), (1, 1024), (1, 1024), (1, 1024), (1, 1024), (1, 1024), (1, 1024),
               (1, 16))


def _pack_repl(parts):
    flat = jnp.concatenate([p.reshape(-1) for p in parts])
    return jnp.concatenate([flat, jnp.zeros((REPL_LEN - flat.shape[0],), F32)]).reshape(REPL_ROWS, D_MODEL)


def _unpack_repl(p):
    flat = p.reshape(-1)
    out, o = [], 0
    for shp in REPL_SHAPES:
        n = math.prod(shp)
        out.append(flat[o:o + n].reshape(shp))
        o += n
    return out


def kernel(x, e_w_in, e_conv_w, e_conv_b, e_w_gate_a, e_b_gate_a, e_w_gate_x, e_b_gate_x, e_lru_lambda, e_w_out, e_ln_g, e_ln_b, o_w_in, o_sinks, o_w_out, o_ln_g, o_ln_b, loss_target, m_e_w_in, m_e_conv_w, m_e_conv_b, m_e_w_gate_a, m_e_b_gate_a, m_e_w_gate_x, m_e_b_gate_x, m_e_lru_lambda, m_e_w_out, m_e_ln_g, m_e_ln_b, m_o_w_in, m_o_sinks, m_o_w_out, m_o_ln_g, m_o_ln_b, v_e_w_in, v_e_conv_w, v_e_conv_b, v_e_w_gate_a, v_e_b_gate_a, v_e_w_gate_x, v_e_b_gate_x, v_e_lru_lambda, v_e_w_out, v_e_ln_g, v_e_ln_b, v_o_w_in, v_o_sinks, v_o_w_out, v_o_ln_g, v_o_ln_b):
    d = D_MODEL
    x0 = x[0]
    target = loss_target[0]
    s = x0.shape[0]

    spack = _small_block([e_conv_w.reshape(512), o_ln_g.reshape(128), o_ln_b.reshape(128)])
    wi_e, sall = _exchange([e_w_in[0].astype(BF16), spack], [GATHER_COLS, GATHER], name="gather_w_in")
    conv_w = sall[:, 0, 0:512].reshape(N_DEV, 4, 128).transpose(1, 0, 2).reshape(4, d)
    ln_g_o = sall[:, 0, 512:640].reshape(1, d)
    ln_b_o = sall[:, 0, 640:768].reshape(1, d)
    pvec = jnp.concatenate([conv_w, e_conv_b, e_b_gate_a, e_b_gate_x, e_lru_lambda], axis=0)
    wa16 = e_w_gate_a[0].astype(BF16)
    wx16 = e_w_gate_x[0].astype(BF16)
    sinks = o_sinks[0]

    x0_16 = x0.astype(BF16)
    h0a, wall_out_e = _matmul(x0_16, wi_e, mode="nn", n_out=2 * d, b_off=0, name="l0_in_a",
                              comm=([e_w_out[0].astype(BF16)], [GATHER]))
    qkv, wall_in_o, wall_out_o = _matmul(
        x0_16, wi_e, mode="nn", n_out=3 * d, b_off=2 * d, out_dtype=BF16, name="l0_in_qkv",
        comm=([o_w_in[0].astype(BF16), o_w_out[0].astype(BF16)], [GATHER, GATHER]))
    bg = _matmul(x0_16, wi_e, mode="nn", n_out=d, b_off=5 * d, name="l0_in_bg")
    wo_e = wall_out_e.reshape(2 * d, d)
    wi_o = wall_in_o.transpose(1, 0, 2).reshape(d, 2304)
    wi_o = jnp.concatenate([wi_o[:, :1024], wi_o[:, 1280:], wi_o[:, 1024:1280]], axis=1)
    wo_o = wall_out_o.reshape(d, d)
    ya, hst = _rglru_fwd(h0a, pvec, wa16, wx16)
    ob, yb = _sb_fwd(qkv, bg)
    yab = jnp.concatenate([ya, yb], axis=1)
    z0, x1, x1_16 = _matmul(yab, wo_e, mode="nn", tm=512, tk=2048, name="l0_out",
                            epi=(_epi_ln_fwd, [(x0, 0)], [e_ln_g, e_ln_b], [F32, F32, BF16], 0))

    q1 = _matmul(x1_16, wi_o, mode="nn", n_out=d, b_off=0, out_dtype=BF16, name="l1_in_q")
    g1 = _matmul(x1_16, wi_o, mode="nn", n_out=d, b_off=d, name="l1_in_g")
    kv1 = _matmul(x1_16, wi_o, mode="nn", n_out=256, b_off=2 * d, tn=256, out_dtype=BF16, name="l1_in_kv")
    kvp = jnp.concatenate([jnp.zeros((WINDOW, 256), BF16), kv1], axis=0)
    o1, y1 = _swa_fwd(q1, g1, kvp, sinks)
    dz1, dz1_16, loss_cols, dg_o, db_o = _matmul(
        y1, wo_o, mode="nn", tm=512, name="l1_out",
        epi=(_epi_ln_loss_bwd, [(x1, 0), (target, 0)], [ln_g_o, ln_b_o], [F32, BF16], 3))
    loss_hi = jnp.sum(loss_cols).astype(BF16).astype(F32)
    loss_terms = jnp.stack([loss_hi, jnp.sum(loss_cols) - loss_hi]).reshape(1, 2)

    dwo_o = _matmul(y1, dz1_16, mode="tn", out_dtype=BF16, name="l1_dwout")
    do1, dg1 = _matmul(dz1_16, wo_o, mode="nt", tm=512, name="l1_dy",
                       epi=(_epi_gate_bwd, [(o1, 0), (g1, 0)], [], [BF16, BF16], 0))
    dq1, dkvp, dsink = _swa_bwd(q1, do1, kvp, sinks)
    dh1 = jnp.concatenate([dq1, dg1, dkvp[WINDOW:].astype(BF16)], axis=1)
    dwi_o = _matmul(x1_16, dh1, mode="tn", tn=1152, out_dtype=BF16, name="l1_dwin")
    dwi_o = jnp.concatenate([dwi_o[:, :1024], dwi_o[:, 2048:], dwi_o[:, 1024:2048]], axis=1)
    dz0, dz0_16, dg_e, db_e, parts_in_o, parts_out_o = _matmul(
        dh1, wi_o, mode="nt", tm=512, tk=2304, name="l1_dx",
        comm=([dwi_o.reshape(d, N_DEV, 288).transpose(1, 0, 2), dwo_o.reshape(N_DEV, 128, d)], [A2A, A2A]),
        epi=(_epi_ln_bwd, [(dz1, 0), (z0, 0)], [e_ln_g], [F32, BF16], 2))

    dwo_e = _matmul(yab, dz0_16, mode="tn", out_dtype=BF16, name="l0_dwout")
    dhst, dag, parts_out_e = _matmul(
        dz0_16, wo_e, mode="nt", tm=512, n_out=d, b_off=0, name="l0_dy_a",
        comm=([dwo_e.reshape(N_DEV, 256, d)], [A2A]),
        epi=(_epi_gate_bwd, [(hst, 0), (h0a, d)], [], [F32, BF16], 0))
    dob, dbg = _matmul(dz0_16, wo_e, mode="nt", tm=512, n_out=d, b_off=d, name="l0_dy_b",
                       epi=(_epi_gate_bwd, [(ob, 0), (bg, 0)], [], [BF16, BF16], 0))
    dq0, dk0, dv0 = _sb_bwd(qkv, dob)
    dax, dwa, dwx, dpv = _rglru_bwd(h0a, hst, dhst, pvec, wa16, wx16)
    dh0 = jnp.concatenate([dax, dag, dq0, dk0.astype(BF16), dv0.astype(BF16), dbg], axis=1)
    dwi_e = _matmul(x0_16, dh0, mode="tn", out_dtype=BF16, name="l0_dwin")
    repl = _pack_repl([dwa, dwx, dpv[4:5], dpv[5:6], dpv[6:7], dpv[7:8], dg_e, db_e, dsink[:, :C_HEADS], loss_terms])
    small = jnp.concatenate([
        _small_block([dpv[0:4].reshape(4, N_DEV, 128).transpose(1, 0, 2).reshape(N_DEV, 512),
                      dg_o.reshape(N_DEV, 128), db_o.reshape(N_DEV, 128)]),
        repl.reshape(N_DEV, R_REPL, d)], axis=1).astype(BF16)
    grad_x, parts_in_e, parts_small = _matmul(dh0, wi_e, mode="nt", tk=2048, res=dz0, alpha=ALPHA, name="l0_dx",
                                              comm=([dwi_e, small], [A2A_COLS, A2A]))

    res_in_e = _adamw(parts_in_e, e_w_in[0], m_e_w_in[0], v_e_w_in[0], tr=512, name="adamw_w_in_e")
    res_out_e = _adamw(parts_out_e, e_w_out[0], m_e_w_out[0], v_e_w_out[0], name="adamw_w_out_e")
    res_in_o = _adamw(parts_in_o, o_w_in[0], m_o_w_in[0], v_o_w_in[0], tr=512, name="adamw_w_in_o")
    res_out_o = _adamw(parts_out_o, o_w_out[0], m_o_w_out[0], v_o_w_out[0], name="adamw_w_out_o")
    res_small = _adamw(parts_small, _small_vectors(e_conv_w, o_ln_g, o_ln_b),
                       _small_vectors(m_e_conv_w, m_o_ln_g, m_o_ln_b),
                       _small_vectors(v_e_conv_w, v_o_ln_g, v_o_ln_b), name="adamw_vectors")

    (g_repl,) = _exchange([res_small[0][R_SMALL:]], [GATHER], name="gather_repl_grads")
    g_repl = g_repl.reshape(1, REPL_ROWS, d)
    w_r = _pack_repl([e_w_gate_a, e_w_gate_x, e_conv_b, e_b_gate_a, e_b_gate_x, e_lru_lambda, e_ln_g, e_ln_b, o_sinks])
    m_r = _pack_repl([m_e_w_gate_a, m_e_w_gate_x, m_e_conv_b, m_e_b_gate_a, m_e_b_gate_x, m_e_lru_lambda, m_e_ln_g,
                      m_e_ln_b, m_o_sinks])
    v_r = _pack_repl([v_e_w_gate_a, v_e_w_gate_x, v_e_conv_b, v_e_b_gate_a, v_e_b_gate_x, v_e_lru_lambda, v_e_ln_g,
                      v_e_ln_b, v_o_sinks])
    g_r, d_r, nm_r, nv_r = _adamw(g_repl, w_r, m_r, v_r, name="adamw_replicated")
    loss_at = sum(math.prod(shp) for shp in REPL_SHAPES)
    loss = g_r[loss_at // d, loss_at % d] + g_r[(loss_at + 1) // d, (loss_at + 1) % d]

    def assemble(i, rp):
        vec = res_small[i][0]
        cw, lg_o, lb_o = vec[0:512].reshape(1, 4, 128), vec[512:640].reshape(1, 128), vec[640:768].reshape(1, 128)
        w_a, w_x, cb, b_a, b_x, lam, lg_e, lb_e, snk = _unpack_repl(rp)
        return [res_in_e[i][None], cw, cb, w_a, b_a, w_x, b_x, lam, res_out_e[i][None], lg_e, lb_e,
                res_in_o[i][None], snk, res_out_o[i][None], lg_o, lb_o]

    return (loss, grad_x[None], *assemble(0, g_r), *assemble(1, d_r), *assemble(2, nm_r), *assemble(3, nv_r))
```

```python
import functools
import math

import jax
import jax.numpy as jnp
from jax import lax
from jax.experimental import pallas as pl
from jax.experimental.pallas import tpu as pltpu

F32 = jnp.float32
BF16 = jnp.bfloat16

N_DEV = 8
D_MODEL = 1024
LRU_BLOCKS = 8
LRU_BLOCK = 128
LRU_C = 8.0
SB_HEADS = 8
SB_HEAD_DIM = 128
C_HEADS = 16
C_KV_HEADS = 2
C_GROUP = 8
C_HEAD_DIM = 64
WINDOW = 128
DEPTH = 2
ALPHA = float((2 * DEPTH) ** 0.25)
LN_EPS = 1e-5
ADAM_LR = 0.001
ADAM_B1 = 0.9
ADAM_B2 = 0.999
ADAM_EPS = 1e-08
ADAM_WD = 0.01
ADAM_STEP = 10

VMEM_LIMIT = 56 * 1024 * 1024

NN = ((1,), (0,))
NT = ((1,), (1,))
TN = ((0,), (0,))


def _dot(a, b, dims):
    return lax.dot_general(a, b, (dims, ((), ())), preferred_element_type=F32)


def _sigmoid(x):
    return 1.0 / (1.0 + jnp.exp(-x))


def _cparams(sem, vmem=VMEM_LIMIT):
    return pltpu.CompilerParams(dimension_semantics=sem, vmem_limit_bytes=vmem)


def _matmul(a, b, *, mode, n_out=None, b_off=0, out_dtype=F32, res=None, alpha=1.0,
            tm=1024, tn=1024, tk=1024, comm=None, epi=None, name):
    if mode == "nn":
        m, k = a.shape
    elif mode == "nt":
        m, k = a.shape
    else:
        k, m = a.shape
    n = n_out if n_out is not None else (b.shape[0] if mode == "nt" else b.shape[1])
    tm, tn, tk = min(tm, m), min(tn, n), min(tk, k)
    assert m % tm == 0 and n % tn == 0 and k % tk == 0 and b_off % tn == 0
    grid = (m // tm, n // tn, k // tk)
    nk = grid[2]
    jo = b_off // tn
    dims = {"nn": NN, "nt": NT, "tn": TN}[mode]
    nc = len(comm[0]) if comm else 0
    if epi:
        assert res is None
        epi_fn, epi_rows, epi_vecs, epi_dtypes, n_sums = epi
        assert n_sums == 0 or grid[1] == 1
    else:
        epi_rows, epi_vecs, epi_dtypes, n_sums = [], [], [out_dtype], 0
    n_in = 2 + (res is not None) + len(epi_rows) + len(epi_vecs)
    n_res = len(epi_dtypes) + n_sums

    def body(*refs):
        a_ref, b_ref = refs[:2]
        r_ref = refs[2] if res is not None else None
        row_refs = refs[n_in - len(epi_rows) - len(epi_vecs):n_in - len(epi_vecs)]
        vec_refs = refs[n_in - len(epi_vecs):n_in]
        o_refs = refs[n_in + nc:n_in + nc + len(epi_dtypes)]
        sum_refs = refs[n_in + nc + len(epi_dtypes):n_in + nc + n_res]
        if comm:
            step = [pl.program_id(ax) for ax in range(3)]
            c_src = refs[n_in:n_in + nc]
            c_dst = refs[n_in + nc + n_res:n_in + 2 * nc + n_res]
            sems = refs[len(refs) - 3:]

            @pl.when((step[0] == 0) & (step[1] == 0) & (step[2] == 0))
            def _():
                _exchange_start(c_src, c_dst, comm[1], sems)

        part = _dot(a_ref[...].astype(BF16), b_ref[...].astype(BF16), dims)

        def finish(out):
            if not epi:
                if r_ref is not None:
                    out = out + alpha * r_ref[...]
                o_refs[0][...] = out.astype(out_dtype)
                return
            outs, sums = epi_fn(out, [r[...] for r in row_refs], [v[...] for v in vec_refs])
            for o_ref, o in zip(o_refs, outs):
                o_ref[...] = o.astype(o_ref.dtype)
            if sum_refs:
                first_rows = pl.program_id(0) == 0

                @pl.when(first_rows)
                def _():
                    for s_ref, part_sum in zip(sum_refs, sums):
                        s_ref[...] = part_sum

                @pl.when(jnp.logical_not(first_rows))
                def _():
                    for s_ref, part_sum in zip(sum_refs, sums):
                        s_ref[...] += part_sum

        if nk == 1:
            finish(part)
        else:
            acc_ref = refs[n_in + 2 * nc + n_res]
            kk = pl.program_id(2)

            @pl.when(kk == 0)
            def _():
                acc_ref[...] = part

            @pl.when(kk > 0)
            def _():
                acc_ref[...] += part

            @pl.when(kk == nk - 1)
            def _():
                finish(acc_ref[...])

        if comm:
            @pl.when((step[0] == grid[0] - 1) & (step[1] == grid[1] - 1) & (step[2] == grid[2] - 1))
            def _():
                _exchange_wait(c_src, c_dst, comm[1], sems)

    if mode == "tn":
        a_spec = pl.BlockSpec((tk, tm), lambda i, j, kk: (kk, i))
    else:
        a_spec = pl.BlockSpec((tm, tk), lambda i, j, kk: (i, kk))
    if mode == "nt":
        b_spec = pl.BlockSpec((tn, tk), lambda i, j, kk: (j + jo, kk))
    else:
        b_spec = pl.BlockSpec((tk, tn), lambda i, j, kk: (kk, j + jo))
    o_spec = pl.BlockSpec((tm, tn), lambda i, j, kk: (i, j))
    hbm = pl.BlockSpec(memory_space=pl.ANY)
    row_specs = [pl.BlockSpec((tm, tn), functools.partial(lambda i, j, kk, o: (i, j + o), o=off // tn))
                 for (_, off) in epi_rows]
    vec_specs = [pl.BlockSpec(v.shape, functools.partial(lambda i, j, kk, nd: (0,) * nd, nd=v.ndim)) for v in epi_vecs]
    in_specs = [a_spec, b_spec] + ([o_spec] if res is not None else []) + row_specs + vec_specs + [hbm] * nc
    args = ([a, b] + ([res] if res is not None else []) + [r for (r, _) in epi_rows] + list(epi_vecs)
            + (list(comm[0]) if comm else []))
    out_specs = [o_spec] * len(epi_dtypes) + [pl.BlockSpec((1, tn), lambda i, j, kk: (0, j))] * n_sums + [hbm] * nc
    out_shape = ([jax.ShapeDtypeStruct((m, n), dt) for dt in epi_dtypes] + [jax.ShapeDtypeStruct((1, n), F32)] * n_sums
                 + (_exchange_out_shapes(*comm) if comm else []))
    scratch = ([pltpu.VMEM((tm, tn), F32)] if nk > 1 else []) + (_exchange_sems(nc) if comm else [])
    if comm or n_sums:
        params = pltpu.CompilerParams(dimension_semantics=("arbitrary",) * 3, vmem_limit_bytes=VMEM_LIMIT,
                                      has_side_effects=bool(comm))
    else:
        params = _cparams(("parallel", "parallel", "arbitrary"))
    outs = pl.pallas_call(
        body, name=name, grid=grid, in_specs=in_specs, out_specs=out_specs, out_shape=out_shape,
        scratch_shapes=scratch, compiler_params=params,
    )(*args)
    return tuple(outs) if (comm or epi) else outs[0]


def _ln_stats(z):
    mu = jnp.mean(z, axis=-1, keepdims=True)
    zc = z - mu
    var = jnp.mean(zc * zc, axis=-1, keepdims=True)
    rstd = lax.rsqrt(var + LN_EPS)
    return zc * rstd, rstd


def _ln_bwd(dy, xhat, rstd, g):
    dxh = dy * g
    m1 = jnp.mean(dxh, axis=-1, keepdims=True)
    m2 = jnp.mean(dxh * xhat, axis=-1, keepdims=True)
    return rstd * (dxh - m1 - xhat * m2)


def _colsum(x):
    return jnp.sum(x, axis=0, keepdims=True)


def _epi_ln_fwd(acc, rows, vecs):
    z = ALPHA * rows[0] + acc
    xhat, _ = _ln_stats(z)
    x1 = xhat * vecs[0] + vecs[1]
    return [z, x1, x1], []


def _epi_ln_loss_bwd(acc, rows, vecs):
    inv_c = 1.0 / acc.shape[-1]
    z = ALPHA * rows[0] + acc
    xhat, rstd = _ln_stats(z)
    d = xhat * vecs[0] + vecs[1] - rows[1]
    dy = d * inv_c
    dz = _ln_bwd(dy, xhat, rstd, vecs[0])
    return [dz, dz], [_colsum(0.5 * d * d * inv_c), _colsum(dy * xhat), _colsum(dy)]


def _epi_ln_bwd(acc, rows, vecs):
    dy = acc + ALPHA * rows[0]
    xhat, rstd = _ln_stats(rows[1])
    dz = _ln_bwd(dy, xhat, rstd, vecs[0])
    return [dz, dz], [_colsum(dy * xhat), _colsum(dy)]


def _epi_gate_bwd(acc, rows, vecs):
    oo, gg = rows
    sg = _sigmoid(gg)
    return [acc * (gg * sg), acc * oo * (sg * (1.0 + gg * (1.0 - sg)))], []


LRU_T = 256
HALO = 8


def _log1p(y):
    u = 1.0 + y
    return jnp.where(u == 1.0, y, jnp.log(u) * (y / (u - 1.0)))


def _lru_gates(c, wa_ref, wx_ref, pv):
    c16 = c.astype(BF16)
    pre_r = jnp.concatenate(
        [_dot(c16[:, n * LRU_BLOCK:(n + 1) * LRU_BLOCK], wa_ref[n], NN) for n in range(LRU_BLOCKS)], axis=1)
    pre_i = jnp.concatenate(
        [_dot(c16[:, n * LRU_BLOCK:(n + 1) * LRU_BLOCK], wx_ref[n], NN) for n in range(LRU_BLOCKS)], axis=1)
    r = _sigmoid(pre_r + pv[5:6])
    ig = _sigmoid(pre_i + pv[6:7])
    lam = pv[7:8]
    ls = jnp.minimum(lam, 0.0) - _log1p(jnp.exp(-jnp.abs(lam)))
    la = LRU_C * r * ls
    a = jnp.exp(la)
    a2 = a * a
    m = jnp.sqrt(jnp.tanh(-la) * (a2 + 1.0))
    return c16, r, ig, ls, la, a, a2, m


def _conv(ext_ref, x, pv, t):
    return (pv[4:5] + pv[3:4] * x + pv[2:3] * ext_ref[pl.ds(HALO - 1, t), :]
            + pv[1:2] * ext_ref[pl.ds(HALO - 2, t), :] + pv[0:1] * ext_ref[pl.ds(HALO - 3, t), :])


def _rglru_fwd(h0a, pvec, wa16, wx16):
    s = h0a.shape[0]
    w = D_MODEL
    t = min(LRU_T, s)
    assert s % t == 0

    def body(ax_ref, ag_ref, pv_ref, wa_ref, wx_ref, ya_ref, h_ref, ext_ref, a_ref, u_ref, hc_ref):
        i = pl.program_id(0)

        @pl.when(i == 0)
        def _():
            ext_ref[pl.ds(0, HALO), :] = jnp.zeros((HALO, w), F32)
            hc_ref[...] = jnp.zeros((1, w), F32)

        pv = pv_ref[...]
        ax = ax_ref[...]
        ext_ref[pl.ds(HALO, t), :] = ax
        c = _conv(ext_ref, ax, pv, t)
        ext_ref[pl.ds(0, HALO), :] = ax[t - HALO:, :]
        _, _, ig, _, _, a, _, m = _lru_gates(c, wa_ref, wx_ref, pv)
        a_ref[...] = a
        u_ref[...] = m * (ig * c)

        def step(k, h):
            h = a_ref[pl.ds(k, 1), :] * h + u_ref[pl.ds(k, 1), :]
            h_ref[pl.ds(k, 1), :] = h
            return h

        hc_ref[...] = lax.fori_loop(0, t, step, hc_ref[...], unroll=8)
        ag = ag_ref[...]
        ya_ref[...] = (h_ref[...] * (ag * _sigmoid(ag))).astype(BF16)

    full = lambda shp: pl.BlockSpec(shp, lambda i: (0,) * len(shp))
    return pl.pallas_call(
        body, name="rglru_fwd", grid=(s // t,),
        in_specs=[pl.BlockSpec((t, w), lambda i: (i, 0)), pl.BlockSpec((t, w), lambda i: (i, 1)),
                  full((8, w)), full((LRU_BLOCKS, LRU_BLOCK, LRU_BLOCK)), full((LRU_BLOCKS, LRU_BLOCK, LRU_BLOCK))],
        out_specs=[pl.BlockSpec((t, w), lambda i: (i, 0)), pl.BlockSpec((t, w), lambda i: (i, 0))],
        out_shape=[jax.ShapeDtypeStruct((s, w), BF16), jax.ShapeDtypeStruct((s, w), F32)],
        scratch_shapes=[pltpu.VMEM((t + HALO, w), F32), pltpu.VMEM((t, w), F32), pltpu.VMEM((t, w), F32),
                        pltpu.VMEM((1, w), F32)],
        compiler_params=_cparams(("arbitrary",)),
    )(h0a, h0a, pvec, wa16, wx16)


def _rglru_bwd(h0a, h, dh, pvec, wa16, wx16):
    s = h0a.shape[0]
    w = D_MODEL
    t = min(LRU_T, s)
    nb = s // t
    hb = t // HALO

    def body(ax_ref, axh_ref, h_ref, hh_ref, dh_ref, pv_ref, wa_ref, wx_ref,
             dax_ref, dwa_ref, dwx_ref, dpv_ref, ext_ref, hext_ref, dcext_ref, a_ref, g_ref, gc_ref):
        i = pl.program_id(0)
        blk = nb - 1 - i

        @pl.when(i == 0)
        def _():
            dwa_ref[...] = jnp.zeros_like(dwa_ref)
            dwx_ref[...] = jnp.zeros_like(dwx_ref)
            dpv_ref[...] = jnp.zeros_like(dpv_ref)
            gc_ref[...] = jnp.zeros((1, w), F32)
            dcext_ref[pl.ds(t, HALO), :] = jnp.zeros((HALO, w), F32)

        pv = pv_ref[...]
        ax = ax_ref[...]
        keep = (blk > 0).astype(F32)
        ext_ref[pl.ds(0, HALO), :] = axh_ref[...] * keep
        ext_ref[pl.ds(HALO, t), :] = ax
        hext_ref[pl.ds(0, HALO), :] = hh_ref[...] * keep
        hext_ref[pl.ds(HALO, t), :] = h_ref[...]
        c = _conv(ext_ref, ax, pv, t)
        c16, r, ig, ls, _, a, a2, m = _lru_gates(c, wa_ref, wx_ref, pv)
        a_ref[...] = a

        def step(k, carry):
            row = t - 1 - k
            g = dh_ref[pl.ds(row, 1), :] + carry
            g_ref[pl.ds(row, 1), :] = g
            return a_ref[pl.ds(row, 1), :] * g

        gc_ref[...] = lax.fori_loop(0, t, step, gc_ref[...], unroll=8)
        g = g_ref[...]
        hprev = hext_ref[pl.ds(HALO - 1, t), :]
        gm = g * m
        d_la = g * hprev * a - (g * ig * c) * a2 / m
        d_pr = d_la * (LRU_C * ls) * r * (1.0 - r)
        d_pi = gm * c * ig * (1.0 - ig)
        dc = gm * ig
        dpr16 = d_pr.astype(BF16)
        dpi16 = d_pi.astype(BF16)
        dc_parts = []
        for n in range(LRU_BLOCKS):
            sl = slice(n * LRU_BLOCK, (n + 1) * LRU_BLOCK)
            dwa_ref[n] += _dot(c16[:, sl], dpr16[:, sl], TN)
            dwx_ref[n] += _dot(c16[:, sl], dpi16[:, sl], TN)
            dc_parts.append(_dot(dpr16[:, sl], wa_ref[n], NT) + _dot(dpi16[:, sl], wx_ref[n], NT))
        dc = dc + jnp.concatenate(dc_parts, axis=1)
        dcext_ref[pl.ds(0, t), :] = dc
        dax = (pv[3:4] * dc + pv[2:3] * dcext_ref[pl.ds(1, t), :] + pv[1:2] * dcext_ref[pl.ds(2, t), :]
               + pv[0:1] * dcext_ref[pl.ds(3, t), :])
        dax_ref[...] = dax.astype(BF16)
        dcext_ref[pl.ds(t, HALO), :] = dc[:HALO, :]
        sums = [_colsum(dc * ext_ref[pl.ds(HALO - 3 + kk, t), :]) for kk in range(4)]
        sums += [_colsum(dc), _colsum(d_pr), _colsum(d_pi), _colsum(d_la * (LRU_C * r))]
        for kk, part in enumerate(sums):
            dpv_ref[pl.ds(kk, 1), :] += part

        @pl.when(i == nb - 1)
        def _():
            lam = pv[7:8]
            dls = dpv_ref[pl.ds(7, 1), :]
            dpv_ref[pl.ds(7, 1), :] = dls * _sigmoid(-lam)

    full = lambda shp: pl.BlockSpec(shp, lambda i: (0,) * len(shp))
    rev = lambda cb: pl.BlockSpec((t, w), functools.partial(lambda i, cb: (nb - 1 - i, cb), cb=cb))
    halo = lambda cb: pl.BlockSpec(
        (HALO, w), functools.partial(lambda i, cb: (jnp.maximum((nb - 1 - i) * hb - 1, 0), cb), cb=cb))
    gw = (LRU_BLOCKS, LRU_BLOCK, LRU_BLOCK)
    return pl.pallas_call(
        body, name="rglru_bwd", grid=(nb,),
        in_specs=[rev(0), halo(0), rev(0), halo(0), rev(0), full((8, w)), full(gw), full(gw)],
        out_specs=[rev(0), full(gw), full(gw), full((8, w))],
        out_shape=[jax.ShapeDtypeStruct((s, w), BF16), jax.ShapeDtypeStruct(gw, F32),
                   jax.ShapeDtypeStruct(gw, F32), jax.ShapeDtypeStruct((8, w), F32)],
        scratch_shapes=[pltpu.VMEM((t + HALO, w), F32), pltpu.VMEM((t + HALO, w), F32),
                        pltpu.VMEM((t + HALO, w), F32), pltpu.VMEM((t, w), F32), pltpu.VMEM((t, w), F32),
                        pltpu.VMEM((1, w), F32)],
        compiler_params=_cparams(("arbitrary",)),
    )(h0a, h0a, h, h, dh, pvec, wa16, wx16)


SB_T = 256
SB_SUB = 2


def _split16(x):
    hi = x.astype(BF16)
    lo = (x - hi.astype(F32)).astype(BF16)
    return jnp.concatenate([hi, lo], axis=0)


def _sb_tile(q, k, scale, tri, run, causal):
    tq = q.shape[0]
    z = _dot(q, k, NT) * scale
    log1mb = -(jnp.maximum(z, 0.0) + jnp.log(1.0 + jnp.exp(-jnp.abs(z))))
    if causal is not None:
        log1mb = jnp.where(causal, log1mb, 0.0)
    cs = _dot(_split16(log1mb), tri, NN)
    cs = cs[:tq] + cs[tq:]
    wgt = jnp.exp(z + cs + run)
    if causal is not None:
        wgt = jnp.where(causal, wgt, 0.0)
    return z, wgt, run + cs[:, 0:1]


SB_DEAD = -105.0


def _sb_alive(run):
    return (jnp.max(run) > SB_DEAD).astype(jnp.int32)


def _sb_more(carry):
    return (carry[0] >= 0) & (carry[1] > 0)


def _sb_fwd(qkv, bg):
    s = qkv.shape[0]
    t = min(SB_T, s)
    nq = s // t
    dh = SB_HEAD_DIM
    scale = 1.0 / math.sqrt(dh)

    sub = SB_SUB if nq % SB_SUB == 0 else 1

    def body(q_ref, k_ref, v_ref, bg_ref, o_ref, y_ref):
        row = lax.broadcasted_iota(jnp.int32, (t, t), 0)
        col = lax.broadcasted_iota(jnp.int32, (t, t), 1)
        tri = jnp.where(row >= col, 1.0, 0.0).astype(BF16)
        causal = col < row
        for u in range(sub):
            rows = pl.ds(u * t, t)
            query_block(pl.program_id(1) * sub + u, q_ref[rows, :], tri, causal, k_ref, v_ref, bg_ref.at[rows, :],
                        o_ref.at[rows, :], y_ref.at[rows, :])

    def query_block(iq, q, tri, causal, k_ref, v_ref, bg_ref, o_ref, y_ref):
        def tile(kb, run, mask):
            off = pl.multiple_of(kb * t, t)
            _, wgt, run = _sb_tile(q, k_ref[pl.ds(off, t), :], scale, tri, run, mask)
            return run, _dot(wgt.astype(BF16), v_ref[pl.ds(off, t), :], NN)

        has_left = iq > 0
        run_d, acc_d = tile(iq, jnp.zeros((t, 1), F32), causal)
        run_l, acc_l = tile(jnp.maximum(iq - 1, 0), run_d, None)
        run = jnp.where(has_left, run_l, run_d)
        acc = acc_d + jnp.where(has_left, acc_l, 0.0)

        def loop(carry):
            kb, _, run, acc = carry
            run, part = tile(kb, run, None)
            return kb - 1, _sb_alive(run), run, acc + part

        _, _, run, acc = lax.while_loop(_sb_more, loop, (iq - 2, _sb_alive(run), run, acc))
        o_ref[...] = acc
        g = bg_ref[...]
        y_ref[...] = (acc * (g * _sigmoid(g))).astype(BF16)

    blk = lambda off: pl.BlockSpec((sub * t, dh), functools.partial(lambda h, i, off: (i, h + off), off=off))
    col = lambda off: pl.BlockSpec((s, dh), functools.partial(lambda h, i, off: (0, h + off), off=off))
    return pl.pallas_call(
        body, name="sb_fwd", grid=(SB_HEADS, nq // sub),
        in_specs=[blk(0), col(SB_HEADS), col(2 * SB_HEADS), blk(0)],
        out_specs=[blk(0), blk(0)],
        out_shape=[jax.ShapeDtypeStruct((s, SB_HEADS * dh), F32), jax.ShapeDtypeStruct((s, SB_HEADS * dh), BF16)],
        compiler_params=_cparams(("parallel", "arbitrary")),
    )(qkv, qkv, qkv, bg)


def _sb_bwd(qkv, dob):
    s = qkv.shape[0]
    t = min(SB_T, s)
    nq = s // t
    dh = SB_HEAD_DIM
    scale = 1.0 / math.sqrt(dh)

    sub = SB_SUB if nq % SB_SUB == 0 else 1

    def body(q_ref, k_ref, v_ref, do_ref, dq_ref, dk_ref, dv_ref, e_ref, b_ref):
        @pl.when(pl.program_id(1) == 0)
        def _():
            dk_ref[...] = jnp.zeros_like(dk_ref)
            dv_ref[...] = jnp.zeros_like(dv_ref)

        row = lax.broadcasted_iota(jnp.int32, (t, t), 0)
        col = lax.broadcasted_iota(jnp.int32, (t, t), 1)
        tri = jnp.where(row >= col, 1.0, 0.0).astype(BF16)
        tri_x = jnp.where(row < col, 1.0, 0.0).astype(BF16)
        for u in range(sub):
            rows = pl.ds(u * t, t)
            query_block(pl.program_id(1) * sub + u, q_ref[rows, :], do_ref[rows, :], tri, tri_x, col < row,
                        k_ref, v_ref, dq_ref.at[rows, :], dk_ref, dv_ref, e_ref, b_ref)

    def query_block(iq, q, do, tri, tri_x, causal, k_ref, v_ref, dq_ref, dk_ref, dv_ref, e_ref, b_ref):
        def sweep1(kb, slot, run, mask, live=None):
            off = pl.multiple_of(kb * t, t)
            v = v_ref[pl.ds(off, t), :]
            z, wgt, run = _sb_tile(q, k_ref[pl.ds(off, t), :], scale, tri, run, mask)
            if live is not None:
                wgt = jnp.where(live, wgt, 0.0)
            e_ref[slot] = wgt * _dot(do, v, NT)
            b_ref[slot] = _sigmoid(z)
            dv_ref[pl.ds(off, t), :] += _dot(wgt.astype(BF16), do, TN)
            return run

        has_left = iq > 0
        left = jnp.maximum(iq - 1, 0)
        left_slot = jnp.where(has_left, iq - 1, nq)
        run_d = sweep1(iq, iq, jnp.zeros((t, 1), F32), causal)
        run = jnp.where(has_left, sweep1(left, left_slot, run_d, None, has_left), run_d)

        def loop1(carry):
            run = sweep1(carry[0], carry[0], carry[2], None)
            return carry[0] - 1, _sb_alive(run), run

        first = lax.while_loop(_sb_more, loop1, (iq - 2, _sb_alive(run), run))[0] + 1

        def sweep2(kb, slot, pre, mask, live=None):
            off = pl.multiple_of(kb * t, t)
            e = e_ref[slot]
            beta = b_ref[slot]
            ps = _dot(_split16(e), tri_x, NN)
            ps = ps[:t] + ps[t:]
            dz = e * (1.0 - beta) - beta * (ps + pre)
            if mask is not None:
                dz = jnp.where(mask, dz, 0.0)
            if live is not None:
                dz = jnp.where(live, dz, 0.0)
            dz16 = (dz * scale).astype(BF16)
            dk_ref[pl.ds(off, t), :] += _dot(dz16, q, TN)
            return pre + ps[:, t - 1:t] + e[:, t - 1:t], _dot(dz16, k_ref[pl.ds(off, t), :], NN)

        def loop2(kb, carry):
            pre, part = sweep2(kb, kb, carry[0], None)
            return pre, carry[1] + part

        pre, dq = lax.fori_loop(first, iq - 1, loop2, (jnp.zeros((t, 1), F32), jnp.zeros((t, dh), F32)))
        pre, dq_l = sweep2(left, left_slot, pre, None, has_left)
        _, dq_d = sweep2(iq, iq, pre, causal)
        dq_ref[...] = (dq + dq_l + dq_d).astype(BF16)

    blk = lambda off: pl.BlockSpec((sub * t, dh), functools.partial(lambda h, i, off: (i, h + off), off=off))
    col = lambda off: pl.BlockSpec((s, dh), functools.partial(lambda h, i, off: (0, h + off), off=off))
    wide = SB_HEADS * dh
    return pl.pallas_call(
        body, name="sb_bwd", grid=(SB_HEADS, nq // sub),
        in_specs=[blk(0), col(SB_HEADS), col(2 * SB_HEADS), blk(0)],
        out_specs=[blk(0), col(0), col(0)],
        out_shape=[jax.ShapeDtypeStruct((s, wide), BF16), jax.ShapeDtypeStruct((s, wide), F32),
                   jax.ShapeDtypeStruct((s, wide), F32)],
        scratch_shapes=[pltpu.VMEM((nq + 1, t, t), F32), pltpu.VMEM((nq + 1, t, t), F32)],
        compiler_params=_cparams(("parallel", "arbitrary")),
    )(qkv, qkv, qkv, dob)


def _alibi_slope(h):
    return float(2.0 ** (-8.0 * (h + 1) / C_HEADS))


GROUP_ROWS = C_GROUP * WINDOW


def _swa_group_consts(n, c, sink_ref):
    qb = WINDOW
    head = lax.shift_right_logical(lax.broadcasted_iota(jnp.int32, (GROUP_ROWS, 1), 0), qb.bit_length() - 1)
    slope = jnp.zeros((GROUP_ROWS, 1), F32)
    sink = jnp.zeros((GROUP_ROWS, 1), F32)
    for hh in range(C_GROUP):
        slope = jnp.where(head == hh, _alibi_slope(c * C_GROUP + hh), slope)
        sink = jnp.where(head == hh, sink_ref[c * C_GROUP + hh], sink)
    i = lax.broadcasted_iota(jnp.int32, (GROUP_ROWS, 2 * qb), 0) & (qb - 1)
    j = lax.broadcasted_iota(jnp.int32, (GROUP_ROWS, 2 * qb), 1)
    d = i - j + qb
    valid = (d >= 0) & (d < WINDOW) & ((j >= qb) | (n > 0))
    return slope, sink, d.astype(F32), valid


def _swa_probs(qg, kw, sink, slope, dist, valid, scale):
    sc = _dot(qg, kw, NT) * scale - slope * dist
    sc = jnp.where(valid, sc, -1e30)
    m = jnp.maximum(jnp.max(sc, axis=-1, keepdims=True), sink)
    p = jnp.exp(sc - m)
    ps = jnp.exp(sink - m)
    inv = 1.0 / (jnp.sum(p, axis=-1, keepdims=True) + ps)
    return p * inv, ps * inv


def _stack_heads(x, c):
    hd = C_HEAD_DIM
    return jnp.concatenate([x[:, (c * C_GROUP + hh) * hd:(c * C_GROUP + hh + 1) * hd] for hh in range(C_GROUP)], axis=0)


def _swa_fwd(q16, gate, kvp, sinks):
    s = q16.shape[0]
    qb = WINDOW
    hd = C_HEAD_DIM
    scale = 1.0 / math.sqrt(hd)
    kvw = C_KV_HEADS * hd

    def body(sink_ref, q_ref, g_ref, kp_ref, ko_ref, o_ref, y_ref):
        n = pl.program_id(0)
        q = q_ref[...]
        kv = jnp.concatenate([kp_ref[...], ko_ref[...]], axis=0)
        for c in range(C_KV_HEADS):
            slope, sink, dist, valid = _swa_group_consts(n, c, sink_ref)
            kw = kv[:, c * hd:(c + 1) * hd]
            vw = kv[:, kvw + c * hd:kvw + (c + 1) * hd]
            p, _ = _swa_probs(_stack_heads(q, c), kw, sink, slope, dist, valid, scale)
            og = _dot(p.astype(BF16), vw, NN)
            for hh in range(C_GROUP):
                h = c * C_GROUP + hh
                o_ref[:, h * hd:(h + 1) * hd] = og[hh * qb:(hh + 1) * qb, :]
        g = g_ref[...]
        y_ref[...] = (o_ref[...] * (g * _sigmoid(g))).astype(BF16)

    wide = C_HEADS * hd
    return pl.pallas_call(
        body, name="swa_fwd", grid=(s // qb,),
        in_specs=[pl.BlockSpec(memory_space=pltpu.SMEM),
                  pl.BlockSpec((qb, wide), lambda n: (n, 0)), pl.BlockSpec((qb, wide), lambda n: (n, 0)),
                  pl.BlockSpec((qb, 2 * kvw), lambda n: (n, 0)), pl.BlockSpec((qb, 2 * kvw), lambda n: (n + 1, 0))],
        out_specs=[pl.BlockSpec((qb, wide), lambda n: (n, 0)), pl.BlockSpec((qb, wide), lambda n: (n, 0))],
        out_shape=[jax.ShapeDtypeStruct((s, wide), F32), jax.ShapeDtypeStruct((s, wide), BF16)],
        compiler_params=_cparams(("arbitrary",)),
    )(sinks, q16, gate, kvp, kvp)


def _swa_bwd(q16, do16, kvp, sinks):
    s = q16.shape[0]
    qb = WINDOW
    hd = C_HEAD_DIM
    scale = 1.0 / math.sqrt(hd)
    kvw = C_KV_HEADS * hd
    nblk = s // qb

    def body(sink_ref, q_ref, do_ref, kp_ref, ko_ref, dq_ref, dkv_ref, ds_ref, sacc_ref):
        n = pl.program_id(0)

        @pl.when(n == 0)
        def _():
            dkv_ref[...] = jnp.zeros_like(dkv_ref)
            sacc_ref[...] = jnp.zeros_like(sacc_ref)

        q = q_ref[...]
        do = do_ref[...]
        kv = jnp.concatenate([kp_ref[...], ko_ref[...]], axis=0)
        off = pl.multiple_of(n * qb, qb)
        for c in range(C_KV_HEADS):
            slope, sink, dist, valid = _swa_group_consts(n, c, sink_ref)
            kw = kv[:, c * hd:(c + 1) * hd]
            vw = kv[:, kvw + c * hd:kvw + (c + 1) * hd]
            qg = _stack_heads(q, c)
            dog = _stack_heads(do, c)
            p, ps = _swa_probs(qg, kw, sink, slope, dist, valid, scale)
            dp = _dot(dog, vw, NT)
            dd = jnp.sum(p * dp, axis=-1, keepdims=True)
            ds16 = (p * (dp - dd) * scale).astype(BF16)
            sacc_ref[c] += ps * dd
            dqg = _dot(ds16, kw, NN).astype(BF16)
            for hh in range(C_GROUP):
                h = c * C_GROUP + hh
                dq_ref[:, h * hd:(h + 1) * hd] = dqg[hh * qb:(hh + 1) * qb, :]
            dkv_ref[pl.ds(off, 2 * qb), c * hd:(c + 1) * hd] += _dot(ds16, qg, TN)
            dkv_ref[pl.ds(off, 2 * qb), kvw + c * hd:kvw + (c + 1) * hd] += _dot(p.astype(BF16), dog, TN)

        @pl.when(n == nblk - 1)
        def _():
            lane = lax.broadcasted_iota(jnp.int32, (1, 128), 1)
            row = jnp.zeros((1, 128), F32)
            for c in range(C_KV_HEADS):
                for hh in range(C_GROUP):
                    tot = jnp.sum(sacc_ref[c, pl.ds(hh * qb, qb), :], axis=0, keepdims=True)
                    row = jnp.where(lane == c * C_GROUP + hh, -tot, row)
            ds_ref[...] = row

    wide = C_HEADS * hd
    return pl.pallas_call(
        body, name="swa_bwd", grid=(nblk,),
        in_specs=[pl.BlockSpec(memory_space=pltpu.SMEM),
                  pl.BlockSpec((qb, wide), lambda n: (n, 0)), pl.BlockSpec((qb, wide), lambda n: (n, 0)),
                  pl.BlockSpec((qb, 2 * kvw), lambda n: (n, 0)), pl.BlockSpec((qb, 2 * kvw), lambda n: (n + 1, 0))],
        out_specs=[pl.BlockSpec((qb, wide), lambda n: (n, 0)),
                   pl.BlockSpec((s + qb, 2 * kvw), lambda n: (0, 0)), pl.BlockSpec((1, 128), lambda n: (0, 0))],
        out_shape=[jax.ShapeDtypeStruct((s, wide), BF16), jax.ShapeDtypeStruct((s + qb, 2 * kvw), F32),
                   jax.ShapeDtypeStruct((1, 128), F32)],
        scratch_shapes=[pltpu.VMEM((C_KV_HEADS, GROUP_ROWS, 1), F32)],
        compiler_params=_cparams(("arbitrary",)),
    )(sinks, q16, do16, kvp, kvp)


def _adamw(parts, w, m, v, *, name, tr=496):
    npart, r, c = parts.shape
    tr = min(tr, r)
    assert r % tr == 0 and tr % 16 == 0
    c1 = 1.0 / (1.0 - ADAM_B1 ** ADAM_STEP)
    c2 = 1.0 / (1.0 - ADAM_B2 ** ADAM_STEP)

    def body(p_ref, w_ref, m_ref, v_ref, g_ref, d_ref, nm_ref, nv_ref):
        g = p_ref[0].astype(F32)
        for j in range(1, npart):
            g = g + p_ref[j].astype(F32)
        nm = ADAM_B1 * m_ref[...] + (1.0 - ADAM_B1) * g
        nv = ADAM_B2 * v_ref[...] + (1.0 - ADAM_B2) * (g * g)
        g_ref[...] = g
        nm_ref[...] = nm
        nv_ref[...] = nv
        d_ref[...] = -ADAM_LR * ((nm * c1) / (jnp.sqrt(nv * c2) + ADAM_EPS) + ADAM_WD * w_ref[...])

    spec = pl.BlockSpec((tr, c), lambda i: (i, 0))
    return pl.pallas_call(
        body, name=name, grid=(r // tr,),
        in_specs=[pl.BlockSpec((npart, tr, c), lambda i: (0, i, 0)), spec, spec, spec],
        out_specs=[spec] * 4, out_shape=[jax.ShapeDtypeStruct((r, c), F32)] * 4,
        compiler_params=_cparams(("parallel",)),
    )(parts, w, m, v)


GATHER = "gather"
A2A = "a2a"
GATHER_COLS = "gather_cols"
A2A_COLS = "a2a_cols"


def _exchange(bufs, gather, *, name):
    nb = len(bufs)

    def body(*refs):
        _exchange_start(refs[:nb], refs[nb:2 * nb], gather, refs[2 * nb:])
        _exchange_wait(refs[:nb], refs[nb:2 * nb], gather, refs[2 * nb:])

    hbm = pl.BlockSpec(memory_space=pl.ANY)
    return pl.pallas_call(
        body, name=name, in_specs=[hbm] * nb, out_specs=[hbm] * nb, out_shape=_exchange_out_shapes(bufs, gather),
        scratch_shapes=_exchange_sems(nb),
        compiler_params=pltpu.CompilerParams(has_side_effects=True),
    )(*bufs)


def _gather_two_level(bufs, modes, *, name):
    nb = len(bufs)

    def body(*refs):
        src, dst = refs[:nb], refs[nb:2 * nb]
        send_sems, recv_sems, local_sems = refs[2 * nb:]
        x, y, c = lax.axis_index("x"), lax.axis_index("y"), lax.axis_index("c")
        me = 4 * x + 2 * y + c
        here, sibling = (x, y, c), (x, y, 1 - c)

        def copy(b, k, origin, to, source):
            return pltpu.make_async_remote_copy(
                src_ref=source, dst_ref=_slot(dst[b], modes[b], origin), send_sem=send_sems.at[b, k - 1],
                recv_sem=recv_sems.at[b, k - 1], device_id=to, device_id_type=pl.DeviceIdType.MESH)

        local = [pltpu.make_async_copy(src[b], _slot(dst[b], modes[b], me), local_sems.at[b]) for b in range(nb)]
        sends = [copy(b, k, me, (x ^ (k >> 2), y ^ ((k >> 1) & 1), c ^ (k & 1)), src[b])
                 for k in (1, 2, 4, 6) for b in range(nb)]
        for cp in local + sends:
            cp.start()
        for j in (2, 4, 6):
            for b in range(nb):
                copy(b, j, me ^ j, here, src[b]).wait_recv()
                passed = copy(b, j ^ 1, me ^ j, sibling, _slot(dst[b], modes[b], me ^ j))
                passed.start()
                sends.append(passed)
        for k in (1, 3, 5, 7):
            for b in range(nb):
                copy(b, k, me ^ k, here, src[b]).wait_recv()
        for cp in sends:
            cp.wait_send()
        for cp in local:
            cp.wait()

    hbm = pl.BlockSpec(memory_space=pl.ANY)
    return pl.pallas_call(
        body, name=name, in_specs=[hbm] * nb, out_specs=[hbm] * nb, out_shape=_exchange_out_shapes(bufs, modes),
        scratch_shapes=_exchange_sems(nb),
        compiler_params=pltpu.CompilerParams(has_side_effects=True),
    )(*bufs)


def _exchange_out_shapes(bufs, gather):
    shapes = {GATHER: lambda s: (N_DEV,) + s, A2A: lambda s: s, GATHER_COLS: lambda s: (s[0], N_DEV * s[1]),
              A2A_COLS: lambda s: (N_DEV, s[0], s[1] // N_DEV)}
    return [jax.ShapeDtypeStruct(shapes[g](tuple(b.shape)), b.dtype) for b, g in zip(bufs, gather)]


def _sent(src, mode, peer):
    if mode == A2A:
        return src.at[peer]
    if mode == A2A_COLS:
        w = src.shape[1] // N_DEV
        return src.at[:, pl.ds(pl.multiple_of(peer * w, 128), w)]
    return src


def _slot(dst, mode, dev):
    if mode == GATHER_COLS:
        w = dst.shape[1] // N_DEV
        return dst.at[:, pl.ds(pl.multiple_of(dev * w, 128), w)]
    return dst.at[dev]


def _exchange_sems(nb):
    return [pltpu.SemaphoreType.DMA((nb, N_DEV - 1)), pltpu.SemaphoreType.DMA((nb, N_DEV - 1)),
            pltpu.SemaphoreType.DMA((nb,))]


def _exchange_copies(src, dst, gather, sems):
    send_sems, recv_sems, local_sems = sems
    x, y, c = lax.axis_index("x"), lax.axis_index("y"), lax.axis_index("c")
    me = 4 * x + 2 * y + c
    local, sends, recvs = [], [], []
    for b in range(len(src)):
        mine = _sent(src[b], gather[b], me)
        local.append(pltpu.make_async_copy(mine, _slot(dst[b], gather[b], me), local_sems.at[b]))
        for k in range(1, N_DEV):
            px, py, pc = x ^ (k >> 2), y ^ ((k >> 1) & 1), c ^ (k & 1)
            peer = 4 * px + 2 * py + pc
            pair = dict(send_sem=send_sems.at[b, k - 1], recv_sem=recv_sems.at[b, k - 1],
                        device_id_type=pl.DeviceIdType.MESH)
            sends.append(pltpu.make_async_remote_copy(
                src_ref=_sent(src[b], gather[b], peer), dst_ref=_slot(dst[b], gather[b], me), device_id=(px, py, pc),
                **pair))
            recvs.append(pltpu.make_async_remote_copy(
                src_ref=mine, dst_ref=_slot(dst[b], gather[b], peer), device_id=(x, y, c), **pair))
    return local, sends, recvs


def _exchange_start(src, dst, gather, sems):
    local, sends, _ = _exchange_copies(src, dst, gather, sems)
    for cp in local + sends:
        cp.start()


def _exchange_wait(src, dst, gather, sems):
    local, sends, recvs = _exchange_copies(src, dst, gather, sems)
    for cp in sends:
        cp.wait_send()
    for cp in recvs:
        cp.wait_recv()
    for cp in local:
        cp.wait()


R_SMALL, R_REPL = 16, 48
REPL_ROWS = N_DEV * R_REPL
REPL_LEN = REPL_ROWS * D_MODEL


def _small_block(parts):
    flat = jnp.concatenate(parts, axis=-1)
    lead = flat.ndim - 1
    return jnp.pad(flat[..., None, :], [(0, 0)] * lead + [(0, R_SMALL - 1), (0, D_MODEL - flat.shape[-1])])


def _small_vectors(conv_w, ln_g, ln_b):
    block = _small_block([conv_w.reshape(512), ln_g.reshape(128), ln_b.reshape(128)])
    return jnp.concatenate([block, jnp.zeros((R_REPL, D_MODEL), F32)], axis=0)


REPL_SHAPES = ((1, 8, 128, 128), (1, 8, 128, 128), (1, 1024), (1, 1024), (1, 1024), (1, 1024), (1, 1024), (1, 1024),
               (1, 16))


def _pack_repl(parts):
    flat = jnp.concatenate([p.reshape(-1) for p in parts])
    return jnp.concatenate([flat, jnp.zeros((REPL_LEN - flat.shape[0],), F32)]).reshape(REPL_ROWS, D_MODEL)


def _unpack_repl(p):
    flat = p.reshape(-1)
    out, o = [], 0
    for shp in REPL_SHAPES:
        n = math.prod(shp)
        out.append(flat[o:o + n].reshape(shp))
        o += n
    return out


def kernel(x, e_w_in, e_conv_w, e_conv_b, e_w_gate_a, e_b_gate_a, e_w_gate_x, e_b_gate_x, e_lru_lambda, e_w_out, e_ln_g, e_ln_b, o_w_in, o_sinks, o_w_out, o_ln_g, o_ln_b, loss_target, m_e_w_in, m_e_conv_w, m_e_conv_b, m_e_w_gate_a, m_e_b_gate_a, m_e_w_gate_x, m_e_b_gate_x, m_e_lru_lambda, m_e_w_out, m_e_ln_g, m_e_ln_b, m_o_w_in, m_o_sinks, m_o_w_out, m_o_ln_g, m_o_ln_b, v_e_w_in, v_e_conv_w, v_e_conv_b, v_e_w_gate_a, v_e_b_gate_a, v_e_w_gate_x, v_e_b_gate_x, v_e_lru_lambda, v_e_w_out, v_e_ln_g, v_e_ln_b, v_o_w_in, v_o_sinks, v_o_w_out, v_o_ln_g, v_o_ln_b):
    d = D_MODEL
    x0 = x[0]
    target = loss_target[0]
    s = x0.shape[0]

    spack = _small_block([e_conv_w.reshape(512), o_ln_g.reshape(128), o_ln_b.reshape(128)])
    wi_e, sall = _gather_two_level([e_w_in[0].astype(BF16), spack], [GATHER_COLS, GATHER], name="gather_w_in")
    conv_w = sall[:, 0, 0:512].reshape(N_DEV, 4, 128).transpose(1, 0, 2).reshape(4, d)
    ln_g_o = sall[:, 0, 512:640].reshape(1, d)
    ln_b_o = sall[:, 0, 640:768].reshape(1, d)
    pvec = jnp.concatenate([conv_w, e_conv_b, e_b_gate_a, e_b_gate_x, e_lru_lambda], axis=0)
    wa16 = e_w_gate_a[0].astype(BF16)
    wx16 = e_w_gate_x[0].astype(BF16)
    sinks = o_sinks[0]

    x0_16 = x0.astype(BF16)
    h0a, wall_out_e = _matmul(x0_16, wi_e, mode="nn", n_out=2 * d, b_off=0, name="l0_in_a",
                              comm=([e_w_out[0].astype(BF16)], [GATHER]))
    qkv, wall_in_o, wall_out_o = _matmul(
        x0_16, wi_e, mode="nn", n_out=3 * d, b_off=2 * d, out_dtype=BF16, name="l0_in_qkv",
        comm=([o_w_in[0].astype(BF16), o_w_out[0].astype(BF16)], [GATHER, GATHER]))
    bg = _matmul(x0_16, wi_e, mode="nn", n_out=d, b_off=5 * d, name="l0_in_bg")
    wo_e = wall_out_e.reshape(2 * d, d)
    wi_o = wall_in_o.transpose(1, 0, 2).reshape(d, 2304)
    wi_o = jnp.concatenate([wi_o[:, :1024], wi_o[:, 1280:], wi_o[:, 1024:1280]], axis=1)
    wo_o = wall_out_o.reshape(d, d)
    ya, hst = _rglru_fwd(h0a, pvec, wa16, wx16)
    ob, yb = _sb_fwd(qkv, bg)
    yab = jnp.concatenate([ya, yb], axis=1)
    z0, x1, x1_16 = _matmul(yab, wo_e, mode="nn", tm=512, tk=2048, name="l0_out",
                            epi=(_epi_ln_fwd, [(x0, 0)], [e_ln_g, e_ln_b], [F32, F32, BF16], 0))

    q1 = _matmul(x1_16, wi_o, mode="nn", n_out=d, b_off=0, out_dtype=BF16, name="l1_in_q")
    g1 = _matmul(x1_16, wi_o, mode="nn", n_out=d, b_off=d, name="l1_in_g")
    kv1 = _matmul(x1_16, wi_o, mode="nn", n_out=256, b_off=2 * d, tn=256, out_dtype=BF16, name="l1_in_kv")
    kvp = jnp.concatenate([jnp.zeros((WINDOW, 256), BF16), kv1], axis=0)
    o1, y1 = _swa_fwd(q1, g1, kvp, sinks)
    dz1, dz1_16, loss_cols, dg_o, db_o = _matmul(
        y1, wo_o, mode="nn", tm=512, name="l1_out",
        epi=(_epi_ln_loss_bwd, [(x1, 0), (target, 0)], [ln_g_o, ln_b_o], [F32, BF16], 3))
    loss_hi = jnp.sum(loss_cols).astype(BF16).astype(F32)
    loss_terms = jnp.stack([loss_hi, jnp.sum(loss_cols) - loss_hi]).reshape(1, 2)

    dwo_o = _matmul(y1, dz1_16, mode="tn", out_dtype=BF16, name="l1_dwout")
    do1, dg1 = _matmul(dz1_16, wo_o, mode="nt", tm=512, name="l1_dy",
                       epi=(_epi_gate_bwd, [(o1, 0), (g1, 0)], [], [BF16, BF16], 0))
    dq1, dkvp, dsink = _swa_bwd(q1, do1, kvp, sinks)
    dh1 = jnp.concatenate([dq1, dg1, dkvp[WINDOW:].astype(BF16)], axis=1)
    dwi_o = _matmul(x1_16, dh1, mode="tn", tn=1152, out_dtype=BF16, name="l1_dwin")
    dwi_o = jnp.concatenate([dwi_o[:, :1024], dwi_o[:, 2048:], dwi_o[:, 1024:2048]], axis=1)
    dz0, dz0_16, dg_e, db_e, parts_in_o, parts_out_o = _matmul(
        dh1, wi_o, mode="nt", tm=512, tk=2304, name="l1_dx",
        comm=([dwi_o.reshape(d, N_DEV, 288).transpose(1, 0, 2), dwo_o.reshape(N_DEV, 128, d)], [A2A, A2A]),
        epi=(_epi_ln_bwd, [(dz1, 0), (z0, 0)], [e_ln_g], [F32, BF16], 2))

    dwo_e = _matmul(yab, dz0_16, mode="tn", out_dtype=BF16, name="l0_dwout")
    dhst, dag, parts_out_e = _matmul(
        dz0_16, wo_e, mode="nt", tm=512, n_out=d, b_off=0, name="l0_dy_a",
        comm=([dwo_e.reshape(N_DEV, 256, d)], [A2A]),
        epi=(_epi_gate_bwd, [(hst, 0), (h0a, d)], [], [F32, BF16], 0))
    dob, dbg = _matmul(dz0_16, wo_e, mode="nt", tm=512, n_out=d, b_off=d, name="l0_dy_b",
                       epi=(_epi_gate_bwd, [(ob, 0), (bg, 0)], [], [BF16, BF16], 0))
    dq0, dk0, dv0 = _sb_bwd(qkv, dob)
    dax, dwa, dwx, dpv = _rglru_bwd(h0a, hst, dhst, pvec, wa16, wx16)
    dh0 = jnp.concatenate([dax, dag, dq0, dk0.astype(BF16), dv0.astype(BF16), dbg], axis=1)
    dwi_e = _matmul(x0_16, dh0, mode="tn", out_dtype=BF16, name="l0_dwin")
    repl = _pack_repl([dwa, dwx, dpv[4:5], dpv[5:6], dpv[6:7], dpv[7:8], dg_e, db_e, dsink[:, :C_HEADS], loss_terms])
    small = jnp.concatenate([
        _small_block([dpv[0:4].reshape(4, N_DEV, 128).transpose(1, 0, 2).reshape(N_DEV, 512),
                      dg_o.reshape(N_DEV, 128), db_o.reshape(N_DEV, 128)]),
        repl.reshape(N_DEV, R_REPL, d)], axis=1).astype(BF16)
    grad_x, parts_in_e, parts_small = _matmul(dh0, wi_e, mode="nt", tk=2048, res=dz0, alpha=ALPHA, name="l0_dx",
                                              comm=([dwi_e, small], [A2A_COLS, A2A]))

    res_in_e = _adamw(parts_in_e, e_w_in[0], m_e_w_in[0], v_e_w_in[0], tr=512, name="adamw_w_in_e")
    res_out_e = _adamw(parts_out_e, e_w_out[0], m_e_w_out[0], v_e_w_out[0], name="adamw_w_out_e")
    res_in_o = _adamw(parts_in_o, o_w_in[0], m_o_w_in[0], v_o_w_in[0], tr=512, name="adamw_w_in_o")
    res_out_o = _adamw(parts_out_o, o_w_out[0], m_o_w_out[0], v_o_w_out[0], name="adamw_w_out_o")
    res_small = _adamw(parts_small, _small_vectors(e_conv_w, o_ln_g, o_ln_b),
                       _small_vectors(m_e_conv_w, m_o_ln_g, m_o_ln_b),
                       _small_vectors(v_e_conv_w, v_o_ln_g, v_o_ln_b), name="adamw_vectors")

    (g_repl,) = _exchange([res_small[0][R_SMALL:]], [GATHER], name="gather_repl_grads")
    g_repl = g_repl.reshape(1, REPL_ROWS, d)
    w_r = _pack_repl([e_w_gate_a, e_w_gate_x, e_conv_b, e_b_gate_a, e_b_gate_x, e_lru_lambda, e_ln_g, e_ln_b, o_sinks])
    m_r = _pack_repl([m_e_w_gate_a, m_e_w_gate_x, m_e_conv_b, m_e_b_gate_a, m_e_b_gate_x, m_e_lru_lambda, m_e_ln_g,
                      m_e_ln_b, m_o_sinks])
    v_r = _pack_repl([v_e_w_gate_a, v_e_w_gate_x, v_e_conv_b, v_e_b_gate_a, v_e_b_gate_x, v_e_lru_lambda, v_e_ln_g,
                      v_e_ln_b, v_o_sinks])
    g_r, d_r, nm_r, nv_r = _adamw(g_repl, w_r, m_r, v_r, name="adamw_replicated")
    loss_at = sum(math.prod(shp) for shp in REPL_SHAPES)
    loss = g_r[loss_at // d, loss_at % d] + g_r[(loss_at + 1) // d, (loss_at + 1) % d]

    def assemble(i, rp):
        vec = res_small[i][0]
        cw, lg_o, lb_o = vec[0:512].reshape(1, 4, 128), vec[512:640].reshape(1, 128), vec[640:768].reshape(1, 128)
        w_a, w_x, cb, b_a, b_x, lam, lg_e, lb_e, snk = _unpack_repl(rp)
        return [res_in_e[i][None], cw, cb, w_a, b_a, w_x, b_x, lam, res_out_e[i][None], lg_e, lb_e,
                res_in_o[i][None], snk, res_out_o[i][None], lg_o, lb_o]

    return (loss, grad_x[None], *assemble(0, g_r), *assemble(1, d_r), *assemble(2, nm_r), *assemble(3, nv_r))
```

```python
import functools
import math

import jax
import jax.numpy as jnp
from jax import lax
from jax.experimental import pallas as pl
from jax.experimental.pallas import tpu as pltpu

F32 = jnp.float32
BF16 = jnp.bfloat16

N_DEV = 8
D_MODEL = 1024
LRU_BLOCKS = 8
LRU_BLOCK = 128
LRU_C = 8.0
SB_HEADS = 8
SB_HEAD_DIM = 128
C_HEADS = 16
C_KV_HEADS = 2
C_GROUP = 8
C_HEAD_DIM = 64
WINDOW = 128
DEPTH = 2
ALPHA = float((2 * DEPTH) ** 0.25)
LN_EPS = 1e-5
ADAM_LR = 0.001
ADAM_B1 = 0.9
ADAM_B2 = 0.999
ADAM_EPS = 1e-08
ADAM_WD = 0.01
ADAM_STEP = 10

VMEM_LIMIT = 56 * 1024 * 1024

NN = ((1,), (0,))
NT = ((1,), (1,))
TN = ((0,), (0,))


def _dot(a, b, dims):
    return lax.dot_general(a, b, (dims, ((), ())), preferred_element_type=F32)


def _sigmoid(x):
    return 1.0 / (1.0 + jnp.exp(-x))


def _cparams(sem, vmem=VMEM_LIMIT):
    return pltpu.CompilerParams(dimension_semantics=sem, vmem_limit_bytes=vmem)


def _matmul(a, b, *, mode, n_out=None, b_off=0, out_dtype=F32, res=None, alpha=1.0,
            tm=1024, tn=1024, tk=1024, comm=None, epi=None, name):
    if mode == "nn":
        m, k = a.shape
    elif mode == "nt":
        m, k = a.shape
    else:
        k, m = a.shape
    n = n_out if n_out is not None else (b.shape[0] if mode == "nt" else b.shape[1])
    tm, tn, tk = min(tm, m), min(tn, n), min(tk, k)
    assert m % tm == 0 and n % tn == 0 and k % tk == 0 and b_off % tn == 0
    grid = (m // tm, n // tn, k // tk)
    nk = grid[2]
    jo = b_off // tn
    dims = {"nn": NN, "nt": NT, "tn": TN}[mode]
    nc = len(comm[0]) if comm else 0
    if epi:
        assert res is None
        epi_fn, epi_rows, epi_vecs, epi_dtypes, n_sums = epi
        assert n_sums == 0 or grid[1] == 1
    else:
        epi_rows, epi_vecs, epi_dtypes, n_sums = [], [], [out_dtype], 0
    n_in = 2 + (res is not None) + len(epi_rows) + len(epi_vecs)
    n_res = len(epi_dtypes) + n_sums

    def body(*refs):
        a_ref, b_ref = refs[:2]
        r_ref = refs[2] if res is not None else None
        row_refs = refs[n_in - len(epi_rows) - len(epi_vecs):n_in - len(epi_vecs)]
        vec_refs = refs[n_in - len(epi_vecs):n_in]
        o_refs = refs[n_in + nc:n_in + nc + len(epi_dtypes)]
        sum_refs = refs[n_in + nc + len(epi_dtypes):n_in + nc + n_res]
        if comm:
            step = [pl.program_id(ax) for ax in range(3)]
            c_src = refs[n_in:n_in + nc]
            c_dst = refs[n_in + nc + n_res:n_in + 2 * nc + n_res]
            sems = refs[len(refs) - 3:]

            @pl.when((step[0] == 0) & (step[1] == 0) & (step[2] == 0))
            def _():
                _exchange_start(c_src, c_dst, comm[1], sems)

        part = _dot(a_ref[...].astype(BF16), b_ref[...].astype(BF16), dims)

        def finish(out):
            if not epi:
                if r_ref is not None:
                    out = out + alpha * r_ref[...]
                o_refs[0][...] = out.astype(out_dtype)
                return
            outs, sums = epi_fn(out, [r[...] for r in row_refs], [v[...] for v in vec_refs])
            for o_ref, o in zip(o_refs, outs):
                o_ref[...] = o.astype(o_ref.dtype)
            if sum_refs:
                first_rows = pl.program_id(0) == 0

                @pl.when(first_rows)
                def _():
                    for s_ref, part_sum in zip(sum_refs, sums):
                        s_ref[...] = part_sum

                @pl.when(jnp.logical_not(first_rows))
                def _():
                    for s_ref, part_sum in zip(sum_refs, sums):
                        s_ref[...] += part_sum

        if nk == 1:
            finish(part)
        else:
            acc_ref = refs[n_in + 2 * nc + n_res]
            kk = pl.program_id(2)

            @pl.when(kk == 0)
            def _():
                acc_ref[...] = part

            @pl.when(kk > 0)
            def _():
                acc_ref[...] += part

            @pl.when(kk == nk - 1)
            def _():
                finish(acc_ref[...])

        if comm:
            @pl.when((step[0] == grid[0] - 1) & (step[1] == grid[1] - 1) & (step[2] == grid[2] - 1))
            def _():
                _exchange_wait(c_src, c_dst, comm[1], sems)

    if mode == "tn":
        a_spec = pl.BlockSpec((tk, tm), lambda i, j, kk: (kk, i))
    else:
        a_spec = pl.BlockSpec((tm, tk), lambda i, j, kk: (i, kk))
    if mode == "nt":
        b_spec = pl.BlockSpec((tn, tk), lambda i, j, kk: (j + jo, kk))
    else:
        b_spec = pl.BlockSpec((tk, tn), lambda i, j, kk: (kk, j + jo))
    o_spec = pl.BlockSpec((tm, tn), lambda i, j, kk: (i, j))
    hbm = pl.BlockSpec(memory_space=pl.ANY)
    row_specs = [pl.BlockSpec((tm, tn), functools.partial(lambda i, j, kk, o: (i, j + o), o=off // tn))
                 for (_, off) in epi_rows]
    vec_specs = [pl.BlockSpec(v.shape, functools.partial(lambda i, j, kk, nd: (0,) * nd, nd=v.ndim)) for v in epi_vecs]
    in_specs = [a_spec, b_spec] + ([o_spec] if res is not None else []) + row_specs + vec_specs + [hbm] * nc
    args = ([a, b] + ([res] if res is not None else []) + [r for (r, _) in epi_rows] + list(epi_vecs)
            + (list(comm[0]) if comm else []))
    out_specs = [o_spec] * len(epi_dtypes) + [pl.BlockSpec((1, tn), lambda i, j, kk: (0, j))] * n_sums + [hbm] * nc
    out_shape = ([jax.ShapeDtypeStruct((m, n), dt) for dt in epi_dtypes] + [jax.ShapeDtypeStruct((1, n), F32)] * n_sums
                 + (_exchange_out_shapes(*comm) if comm else []))
    scratch = ([pltpu.VMEM((tm, tn), F32)] if nk > 1 else []) + (_exchange_sems(nc) if comm else [])
    if comm or n_sums:
        params = pltpu.CompilerParams(dimension_semantics=("arbitrary",) * 3, vmem_limit_bytes=VMEM_LIMIT,
                                      has_side_effects=bool(comm))
    else:
        params = _cparams(("parallel", "parallel", "arbitrary"))
    outs = pl.pallas_call(
        body, name=name, grid=grid, in_specs=in_specs, out_specs=out_specs, out_shape=out_shape,
        scratch_shapes=scratch, compiler_params=params,
    )(*args)
    return tuple(outs) if (comm or epi) else outs[0]


def _ln_stats(z):
    mu = jnp.mean(z, axis=-1, keepdims=True)
    zc = z - mu
    var = jnp.mean(zc * zc, axis=-1, keepdims=True)
    rstd = lax.rsqrt(var + LN_EPS)
    return zc * rstd, rstd


def _ln_bwd(dy, xhat, rstd, g):
    dxh = dy * g
    m1 = jnp.mean(dxh, axis=-1, keepdims=True)
    m2 = jnp.mean(dxh * xhat, axis=-1, keepdims=True)
    return rstd * (dxh - m1 - xhat * m2)


def _colsum(x):
    return jnp.sum(x, axis=0, keepdims=True)


def _epi_ln_fwd(acc, rows, vecs):
    z = ALPHA * rows[0] + acc
    xhat, _ = _ln_stats(z)
    x1 = xhat * vecs[0] + vecs[1]
    return [z, x1, x1], []


def _epi_ln_loss_bwd(acc, rows, vecs):
    inv_c = 1.0 / acc.shape[-1]
    z = ALPHA * rows[0] + acc
    xhat, rstd = _ln_stats(z)
    d = xhat * vecs[0] + vecs[1] - rows[1]
    dy = d * inv_c
    dz = _ln_bwd(dy, xhat, rstd, vecs[0])
    return [dz, dz], [_colsum(0.5 * d * d * inv_c), _colsum(dy * xhat), _colsum(dy)]


def _epi_ln_bwd(acc, rows, vecs):
    dy = acc + ALPHA * rows[0]
    xhat, rstd = _ln_stats(rows[1])
    dz = _ln_bwd(dy, xhat, rstd, vecs[0])
    return [dz, dz], [_colsum(dy * xhat), _colsum(dy)]


def _epi_gate_bwd(acc, rows, vecs):
    oo, gg = rows
    sg = _sigmoid(gg)
    return [acc * (gg * sg), acc * oo * (sg * (1.0 + gg * (1.0 - sg)))], []


LRU_T = 256
HALO = 8


def _log1p(y):
    u = 1.0 + y
    return jnp.where(u == 1.0, y, jnp.log(u) * (y / (u - 1.0)))


def _lru_gates(c, wa_ref, wx_ref, pv):
    c16 = c.astype(BF16)
    pre_r = jnp.concatenate(
        [_dot(c16[:, n * LRU_BLOCK:(n + 1) * LRU_BLOCK], wa_ref[n], NN) for n in range(LRU_BLOCKS)], axis=1)
    pre_i = jnp.concatenate(
        [_dot(c16[:, n * LRU_BLOCK:(n + 1) * LRU_BLOCK], wx_ref[n], NN) for n in range(LRU_BLOCKS)], axis=1)
    r = _sigmoid(pre_r + pv[5:6])
    ig = _sigmoid(pre_i + pv[6:7])
    lam = pv[7:8]
    ls = jnp.minimum(lam, 0.0) - _log1p(jnp.exp(-jnp.abs(lam)))
    la = LRU_C * r * ls
    a = jnp.exp(la)
    a2 = a * a
    m = jnp.sqrt(jnp.tanh(-la) * (a2 + 1.0))
    return c16, r, ig, ls, la, a, a2, m


def _conv(ext_ref, x, pv, t):
    return (pv[4:5] + pv[3:4] * x + pv[2:3] * ext_ref[pl.ds(HALO - 1, t), :]
            + pv[1:2] * ext_ref[pl.ds(HALO - 2, t), :] + pv[0:1] * ext_ref[pl.ds(HALO - 3, t), :])


def _rglru_fwd(h0a, pvec, wa16, wx16):
    s = h0a.shape[0]
    w = D_MODEL
    t = min(LRU_T, s)
    assert s % t == 0

    def body(ax_ref, ag_ref, pv_ref, wa_ref, wx_ref, ya_ref, h_ref, ext_ref, a_ref, u_ref, hc_ref):
        i = pl.program_id(0)

        @pl.when(i == 0)
        def _():
            ext_ref[pl.ds(0, HALO), :] = jnp.zeros((HALO, w), F32)
            hc_ref[...] = jnp.zeros((1, w), F32)

        pv = pv_ref[...]
        ax = ax_ref[...]
        ext_ref[pl.ds(HALO, t), :] = ax
        c = _conv(ext_ref, ax, pv, t)
        ext_ref[pl.ds(0, HALO), :] = ax[t - HALO:, :]
        _, _, ig, _, _, a, _, m = _lru_gates(c, wa_ref, wx_ref, pv)
        a_ref[...] = a
        u_ref[...] = m * (ig * c)

        def steps(k, h):
            base = pl.multiple_of(k * HALO, HALO)
            for r in range(HALO):
                h = a_ref[pl.ds(base + r, 1), :] * h + u_ref[pl.ds(base + r, 1), :]
                h_ref[pl.ds(base + r, 1), :] = h
            return h

        hc_ref[...] = lax.fori_loop(0, t // HALO, steps, hc_ref[...])
        ag = ag_ref[...]
        ya_ref[...] = (h_ref[...] * (ag * _sigmoid(ag))).astype(BF16)

    full = lambda shp: pl.BlockSpec(shp, lambda i: (0,) * len(shp))
    return pl.pallas_call(
        body, name="rglru_fwd", grid=(s // t,),
        in_specs=[pl.BlockSpec((t, w), lambda i: (i, 0)), pl.BlockSpec((t, w), lambda i: (i, 1)),
                  full((8, w)), full((LRU_BLOCKS, LRU_BLOCK, LRU_BLOCK)), full((LRU_BLOCKS, LRU_BLOCK, LRU_BLOCK))],
        out_specs=[pl.BlockSpec((t, w), lambda i: (i, 0)), pl.BlockSpec((t, w), lambda i: (i, 0))],
        out_shape=[jax.ShapeDtypeStruct((s, w), BF16), jax.ShapeDtypeStruct((s, w), F32)],
        scratch_shapes=[pltpu.VMEM((t + HALO, w), F32), pltpu.VMEM((t, w), F32), pltpu.VMEM((t, w), F32),
                        pltpu.VMEM((1, w), F32)],
        compiler_params=_cparams(("arbitrary",)),
    )(h0a, h0a, pvec, wa16, wx16)


def _rglru_bwd(h0a, h, dh, pvec, wa16, wx16):
    s = h0a.shape[0]
    w = D_MODEL
    t = min(LRU_T, s)
    nb = s // t
    hb = t // HALO

    def body(ax_ref, axh_ref, h_ref, hh_ref, dh_ref, pv_ref, wa_ref, wx_ref,
             dax_ref, dwa_ref, dwx_ref, dpv_ref, ext_ref, hext_ref, dcext_ref, a_ref, g_ref, gc_ref):
        i = pl.program_id(0)
        blk = nb - 1 - i

        @pl.when(i == 0)
        def _():
            dwa_ref[...] = jnp.zeros_like(dwa_ref)
            dwx_ref[...] = jnp.zeros_like(dwx_ref)
            dpv_ref[...] = jnp.zeros_like(dpv_ref)
            gc_ref[...] = jnp.zeros((1, w), F32)
            dcext_ref[pl.ds(t, HALO), :] = jnp.zeros((HALO, w), F32)

        pv = pv_ref[...]
        ax = ax_ref[...]
        keep = (blk > 0).astype(F32)
        ext_ref[pl.ds(0, HALO), :] = axh_ref[...] * keep
        ext_ref[pl.ds(HALO, t), :] = ax
        hext_ref[pl.ds(0, HALO), :] = hh_ref[...] * keep
        hext_ref[pl.ds(HALO, t), :] = h_ref[...]
        c = _conv(ext_ref, ax, pv, t)
        c16, r, ig, ls, _, a, a2, m = _lru_gates(c, wa_ref, wx_ref, pv)
        a_ref[...] = a

        def steps(k, carry):
            base = pl.multiple_of(t - HALO - k * HALO, HALO)
            for r in reversed(range(HALO)):
                g = dh_ref[pl.ds(base + r, 1), :] + carry
                g_ref[pl.ds(base + r, 1), :] = g
                carry = a_ref[pl.ds(base + r, 1), :] * g
            return carry

        gc_ref[...] = lax.fori_loop(0, t // HALO, steps, gc_ref[...])
        g = g_ref[...]
        hprev = hext_ref[pl.ds(HALO - 1, t), :]
        gm = g * m
        d_la = g * hprev * a - (g * ig * c) * a2 / m
        d_pr = d_la * (LRU_C * ls) * r * (1.0 - r)
        d_pi = gm * c * ig * (1.0 - ig)
        dc = gm * ig
        dpr16 = d_pr.astype(BF16)
        dpi16 = d_pi.astype(BF16)
        dc_parts = []
        for n in range(LRU_BLOCKS):
            sl = slice(n * LRU_BLOCK, (n + 1) * LRU_BLOCK)
            dwa_ref[n] += _dot(c16[:, sl], dpr16[:, sl], TN)
            dwx_ref[n] += _dot(c16[:, sl], dpi16[:, sl], TN)
            dc_parts.append(_dot(dpr16[:, sl], wa_ref[n], NT) + _dot(dpi16[:, sl], wx_ref[n], NT))
        dc = dc + jnp.concatenate(dc_parts, axis=1)
        dcext_ref[pl.ds(0, t), :] = dc
        dax = (pv[3:4] * dc + pv[2:3] * dcext_ref[pl.ds(1, t), :] + pv[1:2] * dcext_ref[pl.ds(2, t), :]
               + pv[0:1] * dcext_ref[pl.ds(3, t), :])
        dax_ref[...] = dax.astype(BF16)
        dcext_ref[pl.ds(t, HALO), :] = dc[:HALO, :]
        sums = [_colsum(dc * ext_ref[pl.ds(HALO - 3 + kk, t), :]) for kk in range(4)]
        sums += [_colsum(dc), _colsum(d_pr), _colsum(d_pi), _colsum(d_la * (LRU_C * r))]
        for kk, part in enumerate(sums):
            dpv_ref[pl.ds(kk, 1), :] += part

        @pl.when(i == nb - 1)
        def _():
            lam = pv[7:8]
            dls = dpv_ref[pl.ds(7, 1), :]
            dpv_ref[pl.ds(7, 1), :] = dls * _sigmoid(-lam)

    full = lambda shp: pl.BlockSpec(shp, lambda i: (0,) * len(shp))
    rev = lambda cb: pl.BlockSpec((t, w), functools.partial(lambda i, cb: (nb - 1 - i, cb), cb=cb))
    halo = lambda cb: pl.BlockSpec(
        (HALO, w), functools.partial(lambda i, cb: (jnp.maximum((nb - 1 - i) * hb - 1, 0), cb), cb=cb))
    gw = (LRU_BLOCKS, LRU_BLOCK, LRU_BLOCK)
    return pl.pallas_call(
        body, name="rglru_bwd", grid=(nb,),
        in_specs=[rev(0), halo(0), rev(0), halo(0), rev(0), full((8, w)), full(gw), full(gw)],
        out_specs=[rev(0), full(gw), full(gw), full((8, w))],
        out_shape=[jax.ShapeDtypeStruct((s, w), BF16), jax.ShapeDtypeStruct(gw, F32),
                   jax.ShapeDtypeStruct(gw, F32), jax.ShapeDtypeStruct((8, w), F32)],
        scratch_shapes=[pltpu.VMEM((t + HALO, w), F32), pltpu.VMEM((t + HALO, w), F32),
                        pltpu.VMEM((t + HALO, w), F32), pltpu.VMEM((t, w), F32), pltpu.VMEM((t, w), F32),
                        pltpu.VMEM((1, w), F32)],
        compiler_params=_cparams(("arbitrary",)),
    )(h0a, h0a, h, h, dh, pvec, wa16, wx16)


SB_T = 256
SB_SUB = 4


def _split16(x):
    hi = x.astype(BF16)
    lo = (x - hi.astype(F32)).astype(BF16)
    return jnp.concatenate([hi, lo], axis=0)


def _sb_tile(q, k, scale, tri, run, causal):
    tq = q.shape[0]
    z = _dot(q, k, NT) * scale
    log1mb = -(jnp.maximum(z, 0.0) + jnp.log(1.0 + jnp.exp(-jnp.abs(z))))
    if causal is not None:
        log1mb = jnp.where(causal, log1mb, 0.0)
    cs = _dot(_split16(log1mb), tri, NN)
    cs = cs[:tq] + cs[tq:]
    wgt = jnp.exp(z + cs + run)
    if causal is not None:
        wgt = jnp.where(causal, wgt, 0.0)
    return z, wgt, run + cs[:, 0:1]


SB_DEAD = -105.0


def _sb_alive(run):
    return (jnp.max(run) > SB_DEAD).astype(jnp.int32)


def _sb_more(carry):
    return (carry[0] >= 0) & (carry[1] > 0)


def _sb_fwd(qkv, bg):
    s = qkv.shape[0]
    t = min(SB_T, s)
    nq = s // t
    dh = SB_HEAD_DIM
    scale = 1.0 / math.sqrt(dh)

    sub = SB_SUB if nq % SB_SUB == 0 else 1

    def body(q_ref, k_ref, v_ref, bg_ref, o_ref, y_ref):
        row = lax.broadcasted_iota(jnp.int32, (t, t), 0)
        col = lax.broadcasted_iota(jnp.int32, (t, t), 1)
        tri = jnp.where(row >= col, 1.0, 0.0).astype(BF16)
        causal = col < row
        for u in range(sub):
            rows = pl.ds(u * t, t)
            query_block(pl.program_id(1) * sub + u, q_ref[rows, :], tri, causal, k_ref, v_ref, bg_ref.at[rows, :],
                        o_ref.at[rows, :], y_ref.at[rows, :])

    def query_block(iq, q, tri, causal, k_ref, v_ref, bg_ref, o_ref, y_ref):
        def tile(kb, run, mask):
            off = pl.multiple_of(kb * t, t)
            _, wgt, run = _sb_tile(q, k_ref[pl.ds(off, t), :], scale, tri, run, mask)
            return run, _dot(wgt.astype(BF16), v_ref[pl.ds(off, t), :], NN)

        has_left = iq > 0
        run_d, acc_d = tile(iq, jnp.zeros((t, 1), F32), causal)
        run_l, acc_l = tile(jnp.maximum(iq - 1, 0), run_d, None)
        run = jnp.where(has_left, run_l, run_d)
        acc = acc_d + jnp.where(has_left, acc_l, 0.0)

        def loop(carry):
            kb, _, run, acc = carry
            run, part = tile(kb, run, None)
            return kb - 1, _sb_alive(run), run, acc + part

        _, _, run, acc = lax.while_loop(_sb_more, loop, (iq - 2, _sb_alive(run), run, acc))
        o_ref[...] = acc
        g = bg_ref[...]
        y_ref[...] = (acc * (g * _sigmoid(g))).astype(BF16)

    blk = lambda off: pl.BlockSpec((sub * t, dh), functools.partial(lambda h, i, off: (i, h + off), off=off))
    col = lambda off: pl.BlockSpec((s, dh), functools.partial(lambda h, i, off: (0, h + off), off=off))
    return pl.pallas_call(
        body, name="sb_fwd", grid=(SB_HEADS, nq // sub),
        in_specs=[blk(0), col(SB_HEADS), col(2 * SB_HEADS), blk(0)],
        out_specs=[blk(0), blk(0)],
        out_shape=[jax.ShapeDtypeStruct((s, SB_HEADS * dh), F32), jax.ShapeDtypeStruct((s, SB_HEADS * dh), BF16)],
        compiler_params=_cparams(("parallel", "arbitrary")),
    )(qkv, qkv, qkv, bg)


def _sb_bwd(qkv, dob):
    s = qkv.shape[0]
    t = min(SB_T, s)
    nq = s // t
    dh = SB_HEAD_DIM
    scale = 1.0 / math.sqrt(dh)

    sub = SB_SUB if nq % SB_SUB == 0 else 1

    def body(q_ref, k_ref, v_ref, do_ref, dq_ref, dk_ref, dv_ref, e_ref, b_ref):
        @pl.when(pl.program_id(1) == 0)
        def _():
            dk_ref[...] = jnp.zeros_like(dk_ref)
            dv_ref[...] = jnp.zeros_like(dv_ref)

        row = lax.broadcasted_iota(jnp.int32, (t, t), 0)
        col = lax.broadcasted_iota(jnp.int32, (t, t), 1)
        tri = jnp.where(row >= col, 1.0, 0.0).astype(BF16)
        tri_x = jnp.where(row < col, 1.0, 0.0).astype(BF16)
        for u in range(sub):
            rows = pl.ds(u * t, t)
            query_block(pl.program_id(1) * sub + u, q_ref[rows, :], do_ref[rows, :], tri, tri_x, col < row,
                        k_ref, v_ref, dq_ref.at[rows, :], dk_ref, dv_ref, e_ref, b_ref)

    def query_block(iq, q, do, tri, tri_x, causal, k_ref, v_ref, dq_ref, dk_ref, dv_ref, e_ref, b_ref):
        def sweep1(kb, slot, run, mask, live=None):
            off = pl.multiple_of(kb * t, t)
            v = v_ref[pl.ds(off, t), :]
            z, wgt, run = _sb_tile(q, k_ref[pl.ds(off, t), :], scale, tri, run, mask)
            if live is not None:
                wgt = jnp.where(live, wgt, 0.0)
            e_ref[slot] = wgt * _dot(do, v, NT)
            b_ref[slot] = _sigmoid(z)
            dv_ref[pl.ds(off, t), :] += _dot(wgt.astype(BF16), do, TN)
            return run

        has_left = iq > 0
        left = jnp.maximum(iq - 1, 0)
        left_slot = jnp.where(has_left, iq - 1, nq)
        run_d = sweep1(iq, iq, jnp.zeros((t, 1), F32), causal)
        run = jnp.where(has_left, sweep1(left, left_slot, run_d, None, has_left), run_d)

        def loop1(carry):
            run = sweep1(carry[0], carry[0], carry[2], None)
            return carry[0] - 1, _sb_alive(run), run

        first = lax.while_loop(_sb_more, loop1, (iq - 2, _sb_alive(run), run))[0] + 1

        def sweep2(kb, slot, pre, mask, live=None):
            off = pl.multiple_of(kb * t, t)
            e = e_ref[slot]
            beta = b_ref[slot]
            ps = _dot(_split16(e), tri_x, NN)
            ps = ps[:t] + ps[t:]
            dz = e * (1.0 - beta) - beta * (ps + pre)
            if mask is not None:
                dz = jnp.where(mask, dz, 0.0)
            if live is not None:
                dz = jnp.where(live, dz, 0.0)
            dz16 = (dz * scale).astype(BF16)
            dk_ref[pl.ds(off, t), :] += _dot(dz16, q, TN)
            return pre + ps[:, t - 1:t] + e[:, t - 1:t], _dot(dz16, k_ref[pl.ds(off, t), :], NN)

        def loop2(kb, carry):
            pre, part = sweep2(kb, kb, carry[0], None)
            return pre, carry[1] + part

        pre, dq = lax.fori_loop(first, iq - 1, loop2, (jnp.zeros((t, 1), F32), jnp.zeros((t, dh), F32)))
        pre, dq_l = sweep2(left, left_slot, pre, None, has_left)
        _, dq_d = sweep2(iq, iq, pre, causal)
        dq_ref[...] = (dq + dq_l + dq_d).astype(BF16)

    blk = lambda off: pl.BlockSpec((sub * t, dh), functools.partial(lambda h, i, off: (i, h + off), off=off))
    col = lambda off: pl.BlockSpec((s, dh), functools.partial(lambda h, i, off: (0, h + off), off=off))
    wide = SB_HEADS * dh
    return pl.pallas_call(
        body, name="sb_bwd", grid=(SB_HEADS, nq // sub),
        in_specs=[blk(0), col(SB_HEADS), col(2 * SB_HEADS), blk(0)],
        out_specs=[blk(0), col(0), col(0)],
        out_shape=[jax.ShapeDtypeStruct((s, wide), BF16), jax.ShapeDtypeStruct((s, wide), F32),
                   jax.ShapeDtypeStruct((s, wide), F32)],
        scratch_shapes=[pltpu.VMEM((nq + 1, t, t), F32), pltpu.VMEM((nq + 1, t, t), F32)],
        compiler_params=_cparams(("parallel", "arbitrary")),
    )(qkv, qkv, qkv, dob)


def _alibi_slope(h):
    return float(2.0 ** (-8.0 * (h + 1) / C_HEADS))


GROUP_ROWS = C_GROUP * WINDOW


def _swa_group_consts(n, c, sink_ref):
    qb = WINDOW
    head = lax.shift_right_logical(lax.broadcasted_iota(jnp.int32, (GROUP_ROWS, 1), 0), qb.bit_length() - 1)
    slope = jnp.zeros((GROUP_ROWS, 1), F32)
    sink = jnp.zeros((GROUP_ROWS, 1), F32)
    for hh in range(C_GROUP):
        slope = jnp.where(head == hh, _alibi_slope(c * C_GROUP + hh), slope)
        sink = jnp.where(head == hh, sink_ref[c * C_GROUP + hh], sink)
    i = lax.broadcasted_iota(jnp.int32, (GROUP_ROWS, 2 * qb), 0) & (qb - 1)
    j = lax.broadcasted_iota(jnp.int32, (GROUP_ROWS, 2 * qb), 1)
    d = i - j + qb
    valid = (d >= 0) & (d < WINDOW) & ((j >= qb) | (n > 0))
    return slope, sink, d.astype(F32), valid


def _swa_probs(qg, kw, sink, slope, dist, valid, scale):
    sc = _dot(qg, kw, NT) * scale - slope * dist
    sc = jnp.where(valid, sc, -1e30)
    m = jnp.maximum(jnp.max(sc, axis=-1, keepdims=True), sink)
    p = jnp.exp(sc - m)
    ps = jnp.exp(sink - m)
    inv = 1.0 / (jnp.sum(p, axis=-1, keepdims=True) + ps)
    return p * inv, ps * inv


def _stack_heads(x, c):
    hd = C_HEAD_DIM
    return jnp.concatenate([x[:, (c * C_GROUP + hh) * hd:(c * C_GROUP + hh + 1) * hd] for hh in range(C_GROUP)], axis=0)


def _swa_fwd(q16, gate, kvp, sinks):
    s = q16.shape[0]
    qb = WINDOW
    hd = C_HEAD_DIM
    scale = 1.0 / math.sqrt(hd)
    kvw = C_KV_HEADS * hd

    def body(sink_ref, q_ref, g_ref, kp_ref, ko_ref, o_ref, y_ref):
        n = pl.program_id(0)
        q = q_ref[...]
        kv = jnp.concatenate([kp_ref[...], ko_ref[...]], axis=0)
        for c in range(C_KV_HEADS):
            slope, sink, dist, valid = _swa_group_consts(n, c, sink_ref)
            kw = kv[:, c * hd:(c + 1) * hd]
            vw = kv[:, kvw + c * hd:kvw + (c + 1) * hd]
            p, _ = _swa_probs(_stack_heads(q, c), kw, sink, slope, dist, valid, scale)
            og = _dot(p.astype(BF16), vw, NN)
            for hh in range(C_GROUP):
                h = c * C_GROUP + hh
                o_ref[:, h * hd:(h + 1) * hd] = og[hh * qb:(hh + 1) * qb, :]
        g = g_ref[...]
        y_ref[...] = (o_ref[...] * (g * _sigmoid(g))).astype(BF16)

    wide = C_HEADS * hd
    return pl.pallas_call(
        body, name="swa_fwd", grid=(s // qb,),
        in_specs=[pl.BlockSpec(memory_space=pltpu.SMEM),
                  pl.BlockSpec((qb, wide), lambda n: (n, 0)), pl.BlockSpec((qb, wide), lambda n: (n, 0)),
                  pl.BlockSpec((qb, 2 * kvw), lambda n: (n, 0)), pl.BlockSpec((qb, 2 * kvw), lambda n: (n + 1, 0))],
        out_specs=[pl.BlockSpec((qb, wide), lambda n: (n, 0)), pl.BlockSpec((qb, wide), lambda n: (n, 0))],
        out_shape=[jax.ShapeDtypeStruct((s, wide), F32), jax.ShapeDtypeStruct((s, wide), BF16)],
        compiler_params=_cparams(("arbitrary",)),
    )(sinks, q16, gate, kvp, kvp)


def _swa_bwd(q16, do16, kvp, sinks):
    s = q16.shape[0]
    qb = WINDOW
    hd = C_HEAD_DIM
    scale = 1.0 / math.sqrt(hd)
    kvw = C_KV_HEADS * hd
    nblk = s // qb

    def body(sink_ref, q_ref, do_ref, kp_ref, ko_ref, dq_ref, dkv_ref, ds_ref, sacc_ref):
        n = pl.program_id(0)

        @pl.when(n == 0)
        def _():
            dkv_ref[...] = jnp.zeros_like(dkv_ref)
            sacc_ref[...] = jnp.zeros_like(sacc_ref)

        q = q_ref[...]
        do = do_ref[...]
        kv = jnp.concatenate([kp_ref[...], ko_ref[...]], axis=0)
        off = pl.multiple_of(n * qb, qb)
        for c in range(C_KV_HEADS):
            slope, sink, dist, valid = _swa_group_consts(n, c, sink_ref)
            kw = kv[:, c * hd:(c + 1) * hd]
            vw = kv[:, kvw + c * hd:kvw + (c + 1) * hd]
            qg = _stack_heads(q, c)
            dog = _stack_heads(do, c)
            p, ps = _swa_probs(qg, kw, sink, slope, dist, valid, scale)
            dp = _dot(dog, vw, NT)
            dd = jnp.sum(p * dp, axis=-1, keepdims=True)
            ds16 = (p * (dp - dd) * scale).astype(BF16)
            sacc_ref[c] += ps * dd
            dqg = _dot(ds16, kw, NN).astype(BF16)
            for hh in range(C_GROUP):
                h = c * C_GROUP + hh
                dq_ref[:, h * hd:(h + 1) * hd] = dqg[hh * qb:(hh + 1) * qb, :]
            dkv_ref[pl.ds(off, 2 * qb), c * hd:(c + 1) * hd] += _dot(ds16, qg, TN)
            dkv_ref[pl.ds(off, 2 * qb), kvw + c * hd:kvw + (c + 1) * hd] += _dot(p.astype(BF16), dog, TN)

        @pl.when(n == nblk - 1)
        def _():
            lane = lax.broadcasted_iota(jnp.int32, (1, 128), 1)
            row = jnp.zeros((1, 128), F32)
            for c in range(C_KV_HEADS):
                for hh in range(C_GROUP):
                    tot = jnp.sum(sacc_ref[c, pl.ds(hh * qb, qb), :], axis=0, keepdims=True)
                    row = jnp.where(lane == c * C_GROUP + hh, -tot, row)
            ds_ref[...] = row

    wide = C_HEADS * hd
    return pl.pallas_call(
        body, name="swa_bwd", grid=(nblk,),
        in_specs=[pl.BlockSpec(memory_space=pltpu.SMEM),
                  pl.BlockSpec((qb, wide), lambda n: (n, 0)), pl.BlockSpec((qb, wide), lambda n: (n, 0)),
                  pl.BlockSpec((qb, 2 * kvw), lambda n: (n, 0)), pl.BlockSpec((qb, 2 * kvw), lambda n: (n + 1, 0))],
        out_specs=[pl.BlockSpec((qb, wide), lambda n: (n, 0)),
                   pl.BlockSpec((s + qb, 2 * kvw), lambda n: (0, 0)), pl.BlockSpec((1, 128), lambda n: (0, 0))],
        out_shape=[jax.ShapeDtypeStruct((s, wide), BF16), jax.ShapeDtypeStruct((s + qb, 2 * kvw), F32),
                   jax.ShapeDtypeStruct((1, 128), F32)],
        scratch_shapes=[pltpu.VMEM((C_KV_HEADS, GROUP_ROWS, 1), F32)],
        compiler_params=_cparams(("arbitrary",)),
    )(sinks, q16, do16, kvp, kvp)


def _adamw(parts, w, m, v, *, name, tr=496):
    npart, r, c = parts.shape
    tr = min(tr, r)
    assert r % tr == 0 and tr % 16 == 0
    c1 = 1.0 / (1.0 - ADAM_B1 ** ADAM_STEP)
    c2 = 1.0 / (1.0 - ADAM_B2 ** ADAM_STEP)

    def body(p_ref, w_ref, m_ref, v_ref, g_ref, d_ref, nm_ref, nv_ref):
        g = p_ref[0].astype(F32)
        for j in range(1, npart):
            g = g + p_ref[j].astype(F32)
        nm = ADAM_B1 * m_ref[...] + (1.0 - ADAM_B1) * g
        nv = ADAM_B2 * v_ref[...] + (1.0 - ADAM_B2) * (g * g)
        g_ref[...] = g
        nm_ref[...] = nm
        nv_ref[...] = nv
        d_ref[...] = -ADAM_LR * ((nm * c1) / (jnp.sqrt(nv * c2) + ADAM_EPS) + ADAM_WD * w_ref[...])

    spec = pl.BlockSpec((tr, c), lambda i: (i, 0))
    return pl.pallas_call(
        body, name=name, grid=(r // tr,),
        in_specs=[pl.BlockSpec((npart, tr, c), lambda i: (0, i, 0)), spec, spec, spec],
        out_specs=[spec] * 4, out_shape=[jax.ShapeDtypeStruct((r, c), F32)] * 4,
        compiler_params=_cparams(("parallel",)),
    )(parts, w, m, v)


GATHER = "gather"
A2A = "a2a"
GATHER_COLS = "gather_cols"
A2A_COLS = "a2a_cols"


def _exchange(bufs, gather, *, name):
    nb = len(bufs)

    def body(*refs):
        _exchange_start(refs[:nb], refs[nb:2 * nb], gather, refs[2 * nb:])
        _exchange_wait(refs[:nb], refs[nb:2 * nb], gather, refs[2 * nb:])

    hbm = pl.BlockSpec(memory_space=pl.ANY)
    return pl.pallas_call(
        body, name=name, in_specs=[hbm] * nb, out_specs=[hbm] * nb, out_shape=_exchange_out_shapes(bufs, gather),
        scratch_shapes=_exchange_sems(nb),
        compiler_params=pltpu.CompilerParams(has_side_effects=True),
    )(*bufs)


def _gather_two_level(bufs, modes, *, name):
    nb = len(bufs)

    def body(*refs):
        src, dst = refs[:nb], refs[nb:2 * nb]
        send_sems, recv_sems, local_sems = refs[2 * nb:]
        x, y, c = lax.axis_index("x"), lax.axis_index("y"), lax.axis_index("c")
        me = 4 * x + 2 * y + c
        here, sibling = (x, y, c), (x, y, 1 - c)

        def copy(b, k, origin, to, source):
            return pltpu.make_async_remote_copy(
                src_ref=source, dst_ref=_slot(dst[b], modes[b], origin), send_sem=send_sems.at[b, k - 1],
                recv_sem=recv_sems.at[b, k - 1], device_id=to, device_id_type=pl.DeviceIdType.MESH)

        local = [pltpu.make_async_copy(src[b], _slot(dst[b], modes[b], me), local_sems.at[b]) for b in range(nb)]
        sends = [copy(b, k, me, (x ^ (k >> 2), y ^ ((k >> 1) & 1), c ^ (k & 1)), src[b])
                 for k in (1, 2, 4, 6) for b in range(nb)]
        for cp in local + sends:
            cp.start()
        for j in (2, 4, 6):
            for b in range(nb):
                copy(b, j, me ^ j, here, src[b]).wait_recv()
                passed = copy(b, j ^ 1, me ^ j, sibling, _slot(dst[b], modes[b], me ^ j))
                passed.start()
                sends.append(passed)
        for k in (1, 3, 5, 7):
            for b in range(nb):
                copy(b, k, me ^ k, here, src[b]).wait_recv()
        for cp in sends:
            cp.wait_send()
        for cp in local:
            cp.wait()

    hbm = pl.BlockSpec(memory_space=pl.ANY)
    return pl.pallas_call(
        body, name=name, in_specs=[hbm] * nb, out_specs=[hbm] * nb, out_shape=_exchange_out_shapes(bufs, modes),
        scratch_shapes=_exchange_sems(nb),
        compiler_params=pltpu.CompilerParams(has_side_effects=True),
    )(*bufs)


def _exchange_out_shapes(bufs, gather):
    shapes = {GATHER: lambda s: (N_DEV,) + s, A2A: lambda s: s, GATHER_COLS: lambda s: (s[0], N_DEV * s[1]),
              A2A_COLS: lambda s: (N_DEV, s[0], s[1] // N_DEV)}
    return [jax.ShapeDtypeStruct(shapes[g](tuple(b.shape)), b.dtype) for b, g in zip(bufs, gather)]


def _sent(src, mode, peer):
    if mode == A2A:
        return src.at[peer]
    if mode == A2A_COLS:
        w = src.shape[1] // N_DEV
        return src.at[:, pl.ds(pl.multiple_of(peer * w, 128), w)]
    return src


def _slot(dst, mode, dev):
    if mode == GATHER_COLS:
        w = dst.shape[1] // N_DEV
        return dst.at[:, pl.ds(pl.multiple_of(dev * w, 128), w)]
    return dst.at[dev]


def _exchange_sems(nb):
    return [pltpu.SemaphoreType.DMA((nb, N_DEV - 1)), pltpu.SemaphoreType.DMA((nb, N_DEV - 1)),
            pltpu.SemaphoreType.DMA((nb,))]


def _exchange_copies(src, dst, gather, sems):
    send_sems, recv_sems, local_sems = sems
    x, y, c = lax.axis_index("x"), lax.axis_index("y"), lax.axis_index("c")
    me = 4 * x + 2 * y + c
    local, sends, recvs = [], [], []
    for b in range(len(src)):
        mine = _sent(src[b], gather[b], me)
        local.append(pltpu.make_async_copy(mine, _slot(dst[b], gather[b], me), local_sems.at[b]))
        for k in range(1, N_DEV):
            px, py, pc = x ^ (k >> 2), y ^ ((k >> 1) & 1), c ^ (k & 1)
            peer = 4 * px + 2 * py + pc
            pair = dict(send_sem=send_sems.at[b, k - 1], recv_sem=recv_sems.at[b, k - 1],
                        device_id_type=pl.DeviceIdType.MESH)
            sends.append(pltpu.make_async_remote_copy(
                src_ref=_sent(src[b], gather[b], peer), dst_ref=_slot(dst[b], gather[b], me), device_id=(px, py, pc),
                **pair))
            recvs.append(pltpu.make_async_remote_copy(
                src_ref=mine, dst_ref=_slot(dst[b], gather[b], peer), device_id=(x, y, c), **pair))
    return local, sends, recvs


def _exchange_start(src, dst, gather, sems):
    local, sends, _ = _exchange_copies(src, dst, gather, sems)
    for cp in local + sends:
        cp.start()


def _exchange_wait(src, dst, gather, sems):
    local, sends, recvs = _exchange_copies(src, dst, gather, sems)
    for cp in sends:
        cp.wait_send()
    for cp in recvs:
        cp.wait_recv()
    for cp in local:
        cp.wait()


R_SMALL, R_REPL = 16, 48
REPL_ROWS = N_DEV * R_REPL
REPL_LEN = REPL_ROWS * D_MODEL


def _small_block(parts):
    flat = jnp.concatenate(parts, axis=-1)
    lead = flat.ndim - 1
    return jnp.pad(flat[..., None, :], [(0, 0)] * lead + [(0, R_SMALL - 1), (0, D_MODEL - flat.shape[-1])])


def _small_vectors(conv_w, ln_g, ln_b):
    block = _small_block([conv_w.reshape(512), ln_g.reshape(128), ln_b.reshape(128)])
    return jnp.concatenate([block, jnp.zeros((R_REPL, D_MODEL), F32)], axis=0)


REPL_SHAPES = ((1, 8, 128, 128), (1, 8, 128, 128), (1, 1024), (1, 1024), (1, 1024), (1, 1024), (1, 1024), (1, 1024),
               (1, 16))


def _pack_repl(parts):
    flat = jnp.concatenate([p.reshape(-1) for p in parts])
    return jnp.concatenate([flat, jnp.zeros((REPL_LEN - flat.shape[0],), F32)]).reshape(REPL_ROWS, D_MODEL)


def _unpack_repl(p):
    flat = p.reshape(-1)
    out, o = [], 0
    for shp in REPL_SHAPES:
        n = math.prod(shp)
        out.append(flat[o:o + n].reshape(shp))
        o += n
    return out


def kernel(x, e_w_in, e_conv_w, e_conv_b, e_w_gate_a, e_b_gate_a, e_w_gate_x, e_b_gate_x, e_lru_lambda, e_w_out, e_ln_g, e_ln_b, o_w_in, o_sinks, o_w_out, o_ln_g, o_ln_b, loss_target, m_e_w_in, m_e_conv_w, m_e_conv_b, m_e_w_gate_a, m_e_b_gate_a, m_e_w_gate_x, m_e_b_gate_x, m_e_lru_lambda, m_e_w_out, m_e_ln_g, m_e_ln_b, m_o_w_in, m_o_sinks, m_o_w_out, m_o_ln_g, m_o_ln_b, v_e_w_in, v_e_conv_w, v_e_conv_b, v_e_w_gate_a, v_e_b_gate_a, v_e_w_gate_x, v_e_b_gate_x, v_e_lru_lambda, v_e_w_out, v_e_ln_g, v_e_ln_b, v_o_w_in, v_o_sinks, v_o_w_out, v_o_ln_g, v_o_ln_b):
    d = D_MODEL
    x0 = x[0]
    target = loss_target[0]
    s = x0.shape[0]

    spack = _small_block([e_conv_w.reshape(512), o_ln_g.reshape(128), o_ln_b.reshape(128)])
    wi_e, sall = _gather_two_level([e_w_in[0].astype(BF16), spack], [GATHER_COLS, GATHER], name="gather_w_in")
    conv_w = sall[:, 0, 0:512].reshape(N_DEV, 4, 128).transpose(1, 0, 2).reshape(4, d)
    ln_g_o = sall[:, 0, 512:640].reshape(1, d)
    ln_b_o = sall[:, 0, 640:768].reshape(1, d)
    pvec = jnp.concatenate([conv_w, e_conv_b, e_b_gate_a, e_b_gate_x, e_lru_lambda], axis=0)
    wa16 = e_w_gate_a[0].astype(BF16)
    wx16 = e_w_gate_x[0].astype(BF16)
    sinks = o_sinks[0]

    x0_16 = x0.astype(BF16)
    h0a, wall_out_e = _matmul(x0_16, wi_e, mode="nn", n_out=2 * d, b_off=0, name="l0_in_a",
                              comm=([e_w_out[0].astype(BF16)], [GATHER]))
    qkv, wall_in_o = _matmul(x0_16, wi_e, mode="nn", n_out=3 * d, b_off=2 * d, out_dtype=BF16, name="l0_in_qkv",
                             comm=([o_w_in[0].astype(BF16)], [GATHER]))
    bg, wall_out_o = _matmul(x0_16, wi_e, mode="nn", n_out=d, b_off=5 * d, name="l0_in_bg",
                             comm=([o_w_out[0].astype(BF16)], [GATHER]))
    wo_e = wall_out_e.reshape(2 * d, d)
    wi_o = wall_in_o.transpose(1, 0, 2).reshape(d, 2304)
    wi_o = jnp.concatenate([wi_o[:, :1024], wi_o[:, 1280:], wi_o[:, 1024:1280]], axis=1)
    wo_o = wall_out_o.reshape(d, d)
    ya, hst = _rglru_fwd(h0a, pvec, wa16, wx16)
    ob, yb = _sb_fwd(qkv, bg)
    yab = jnp.concatenate([ya, yb], axis=1)
    z0, x1, x1_16 = _matmul(yab, wo_e, mode="nn", tm=512, tk=2048, name="l0_out",
                            epi=(_epi_ln_fwd, [(x0, 0)], [e_ln_g, e_ln_b], [F32, F32, BF16], 0))

    q1 = _matmul(x1_16, wi_o, mode="nn", n_out=d, b_off=0, out_dtype=BF16, name="l1_in_q")
    g1 = _matmul(x1_16, wi_o, mode="nn", n_out=d, b_off=d, name="l1_in_g")
    kv1 = _matmul(x1_16, wi_o, mode="nn", n_out=256, b_off=2 * d, tn=256, out_dtype=BF16, name="l1_in_kv")
    kvp = jnp.concatenate([jnp.zeros((WINDOW, 256), BF16), kv1], axis=0)
    o1, y1 = _swa_fwd(q1, g1, kvp, sinks)
    dz1, dz1_16, loss_cols, dg_o, db_o = _matmul(
        y1, wo_o, mode="nn", tm=512, name="l1_out",
        epi=(_epi_ln_loss_bwd, [(x1, 0), (target, 0)], [ln_g_o, ln_b_o], [F32, BF16], 3))
    loss_hi = jnp.sum(loss_cols).astype(BF16).astype(F32)
    loss_terms = jnp.stack([loss_hi, jnp.sum(loss_cols) - loss_hi]).reshape(1, 2)

    dwo_o = _matmul(y1, dz1_16, mode="tn", out_dtype=BF16, name="l1_dwout")
    do1, dg1, parts_out_o = _matmul(dz1_16, wo_o, mode="nt", tm=512, name="l1_dy",
                                    comm=([dwo_o.reshape(N_DEV, 128, d)], [A2A]),
                                    epi=(_epi_gate_bwd, [(o1, 0), (g1, 0)], [], [BF16, BF16], 0))
    dq1, dkvp, dsink = _swa_bwd(q1, do1, kvp, sinks)
    dh1 = jnp.concatenate([dq1, dg1, dkvp[WINDOW:].astype(BF16)], axis=1)
    dwi_o = _matmul(x1_16, dh1, mode="tn", tn=1152, out_dtype=BF16, name="l1_dwin")
    dwi_o = jnp.concatenate([dwi_o[:, :1024], dwi_o[:, 2048:], dwi_o[:, 1024:2048]], axis=1)
    dz0, dz0_16, dg_e, db_e, parts_in_o = _matmul(
        dh1, wi_o, mode="nt", tm=512, tk=2304, name="l1_dx",
        comm=([dwi_o.reshape(d, N_DEV, 288).transpose(1, 0, 2)], [A2A]),
        epi=(_epi_ln_bwd, [(dz1, 0), (z0, 0)], [e_ln_g], [F32, BF16], 2))

    dwo_e = _matmul(yab, dz0_16, mode="tn", out_dtype=BF16, name="l0_dwout")
    dhst, dag, parts_out_e = _matmul(
        dz0_16, wo_e, mode="nt", tm=512, n_out=d, b_off=0, name="l0_dy_a",
        comm=([dwo_e.reshape(N_DEV, 256, d)], [A2A]),
        epi=(_epi_gate_bwd, [(hst, 0), (h0a, d)], [], [F32, BF16], 0))
    dob, dbg = _matmul(dz0_16, wo_e, mode="nt", tm=512, n_out=d, b_off=d, name="l0_dy_b",
                       epi=(_epi_gate_bwd, [(ob, 0), (bg, 0)], [], [BF16, BF16], 0))
    dq0, dk0, dv0 = _sb_bwd(qkv, dob)
    dax, dwa, dwx, dpv = _rglru_bwd(h0a, hst, dhst, pvec, wa16, wx16)
    dh0 = jnp.concatenate([dax, dag, dq0, dk0.astype(BF16), dv0.astype(BF16), dbg], axis=1)
    dwi_e = _matmul(x0_16, dh0, mode="tn", out_dtype=BF16, name="l0_dwin")
    repl = _pack_repl([dwa, dwx, dpv[4:5], dpv[5:6], dpv[6:7], dpv[7:8], dg_e, db_e, dsink[:, :C_HEADS], loss_terms])
    small = jnp.concatenate([
        _small_block([dpv[0:4].reshape(4, N_DEV, 128).transpose(1, 0, 2).reshape(N_DEV, 512),
                      dg_o.reshape(N_DEV, 128), db_o.reshape(N_DEV, 128)]),
        repl.reshape(N_DEV, R_REPL, d)], axis=1).astype(BF16)
    grad_x, parts_in_e, parts_small = _matmul(dh0, wi_e, mode="nt", tk=2048, res=dz0, alpha=ALPHA, name="l0_dx",
                                              comm=([dwi_e, small], [A2A_COLS, A2A]))

    res_in_e = _adamw(parts_in_e, e_w_in[0], m_e_w_in[0], v_e_w_in[0], tr=512, name="adamw_w_in_e")
    res_out_e = _adamw(parts_out_e, e_w_out[0], m_e_w_out[0], v_e_w_out[0], name="adamw_w_out_e")
    res_in_o = _adamw(parts_in_o, o_w_in[0], m_o_w_in[0], v_o_w_in[0], tr=512, name="adamw_w_in_o")
    res_out_o = _adamw(parts_out_o, o_w_out[0], m_o_w_out[0], v_o_w_out[0], name="adamw_w_out_o")
    res_small = _adamw(parts_small, _small_vectors(e_conv_w, o_ln_g, o_ln_b),
                       _small_vectors(m_e_conv_w, m_o_ln_g, m_o_ln_b),
                       _small_vectors(v_e_conv_w, v_o_ln_g, v_o_ln_b), name="adamw_vectors")

    (g_repl,) = _exchange([res_small[0][R_SMALL:]], [GATHER], name="gather_repl_grads")
    g_repl = g_repl.reshape(1, REPL_ROWS, d)
    w_r = _pack_repl([e_w_gate_a, e_w_gate_x, e_conv_b, e_b_gate_a, e_b_gate_x, e_lru_lambda, e_ln_g, e_ln_b, o_sinks])
    m_r = _pack_repl([m_e_w_gate_a, m_e_w_gate_x, m_e_conv_b, m_e_b_gate_a, m_e_b_gate_x, m_e_lru_lambda, m_e_ln_g,
                      m_e_ln_b, m_o_sinks])
    v_r = _pack_repl([v_e_w_gate_a, v_e_w_gate_x, v_e_conv_b, v_e_b_gate_a, v_e_b_gate_x, v_e_lru_lambda, v_e_ln_g,
                      v_e_ln_b, v_o_sinks])
    g_r, d_r, nm_r, nv_r = _adamw(g_repl, w_r, m_r, v_r, name="adamw_replicated")
    loss_at = sum(math.prod(shp) for shp in REPL_SHAPES)
    loss = g_r[loss_at // d, loss_at % d] + g_r[(loss_at + 1) // d, (loss_at + 1) % d]

    def assemble(i, rp):
        vec = res_small[i][0]
        cw, lg_o, lb_o = vec[0:512].reshape(1, 4, 128), vec[512:640].reshape(1, 128), vec[640:768].reshape(1, 128)
        w_a, w_x, cb, b_a, b_x, lam, lg_e, lb_e, snk = _unpack_repl(rp)
        return [res_in_e[i][None], cw, cb, w_a, b_a, w_x, b_x, lam, res_out_e[i][None], lg_e, lb_e,
                res_in_o[i][None], snk, res_out_o[i][None], lg_o, lb_o]

    return (loss, grad_x[None], *assemble(0, g_r), *assemble(1, d_r), *assemble(2, nm_r), *assemble(3, nv_r))
```

```python
import functools
import math

import jax
import jax.numpy as jnp
from jax import lax
from jax.experimental import pallas as pl
from jax.experimental.pallas import tpu as pltpu

F32 = jnp.float32
BF16 = jnp.bfloat16

N_DEV = 8
D_MODEL = 1024
LRU_BLOCKS = 8
LRU_BLOCK = 128
LRU_C = 8.0
SB_HEADS = 8
SB_HEAD_DIM = 128
C_HEADS = 16
C_KV_HEADS = 2
C_GROUP = 8
C_HEAD_DIM = 64
WINDOW = 128
DEPTH = 2
ALPHA = float((2 * DEPTH) ** 0.25)
LN_EPS = 1e-5
ADAM_LR = 0.001
ADAM_B1 = 0.9
ADAM_B2 = 0.999
ADAM_EPS = 1e-08
ADAM_WD = 0.01
ADAM_STEP = 10

VMEM_LIMIT = 56 * 1024 * 1024

NN = ((1,), (0,))
NT = ((1,), (1,))
TN = ((0,), (0,))


def _dot(a, b, dims):
    return lax.dot_general(a, b, (dims, ((), ())), preferred_element_type=F32)


def _sigmoid(x):
    return 1.0 / (1.0 + jnp.exp(-x))


def _cparams(sem, vmem=VMEM_LIMIT):
    return pltpu.CompilerParams(dimension_semantics=sem, vmem_limit_bytes=vmem)


def _matmul(a, b, *, mode, n_out=None, b_off=0, out_dtype=F32, res=None, alpha=1.0,
            tm=1024, tn=1024, tk=1024, comm=None, epi=None, name):
    a_list = list(a) if isinstance(a, (list, tuple)) else [a]
    b_list = list(b) if isinstance(b, (list, tuple)) else [b]
    if mode == "tn":
        k = a_list[0].shape[0]
        m = sum(p.shape[1] for p in a_list)
        n = n_out if n_out is not None else sum(p.shape[1] for p in b_list)
        assert all(p.shape[1] == tm for p in a_list) or len(a_list) == 1
        assert all(p.shape[1] == tn for p in b_list) or len(b_list) == 1
    else:
        assert len(b_list) == 1
        m = a_list[0].shape[0]
        k = sum(p.shape[1] for p in a_list)
        n = n_out if n_out is not None else (b.shape[0] if mode == "nt" else b.shape[1])
        if len(a_list) > 1:
            tk = k
    tm, tn, tk = min(tm, m), min(tn, n), min(tk, k)
    assert m % tm == 0 and n % tn == 0 and k % tk == 0 and b_off % tn == 0
    grid = (m // tm, n // tn, k // tk)
    nk = grid[2]
    jo = b_off // tn
    dims = {"nn": NN, "nt": NT, "tn": TN}[mode]
    nc = len(comm[0]) if comm else 0
    if epi:
        assert res is None
        epi_fn, epi_rows, epi_vecs, epi_dtypes, n_sums = epi
        assert n_sums == 0 or grid[1] == 1
    else:
        epi_rows, epi_vecs, epi_dtypes, n_sums = [], [], [out_dtype], 0
    n_ab = len(a_list) + len(b_list)
    n_in = n_ab + (res is not None) + len(epi_rows) + len(epi_vecs)
    n_res = len(epi_dtypes) + n_sums

    def body(*refs):
        a_refs, b_refs = refs[:len(a_list)], refs[len(a_list):n_ab]
        r_ref = refs[n_ab] if res is not None else None
        row_refs = refs[n_in - len(epi_rows) - len(epi_vecs):n_in - len(epi_vecs)]
        vec_refs = refs[n_in - len(epi_vecs):n_in]
        o_refs = refs[n_in + nc:n_in + nc + len(epi_dtypes)]
        sum_refs = refs[n_in + nc + len(epi_dtypes):n_in + nc + n_res]
        if comm:
            step = [pl.program_id(ax) for ax in range(3)]
            c_src = refs[n_in:n_in + nc]
            c_dst = refs[n_in + nc + n_res:n_in + 2 * nc + n_res]
            sems = refs[len(refs) - 3:]

            @pl.when((step[0] == 0) & (step[1] == 0) & (step[2] == 0))
            def _():
                _exchange_start(c_src, c_dst, comm[1], sems)

        def finish(out):
            if not epi:
                if r_ref is not None:
                    out = out + alpha * r_ref[...]
                o_refs[0][...] = out.astype(out_dtype)
                return
            outs, sums = epi_fn(out, [r[...] for r in row_refs], [v[...] for v in vec_refs])
            for o_ref, o in zip(o_refs, outs):
                o_ref[...] = o.astype(o_ref.dtype)
            if sum_refs:
                first_rows = pl.program_id(0) == 0

                @pl.when(first_rows)
                def _():
                    for s_ref, part_sum in zip(sum_refs, sums):
                        s_ref[...] = part_sum

                @pl.when(jnp.logical_not(first_rows))
                def _():
                    for s_ref, part_sum in zip(sum_refs, sums):
                        s_ref[...] += part_sum

        def accumulate(part):
            if nk == 1:
                finish(part)
                return
            acc_ref = refs[n_in + 2 * nc + n_res]
            kk = pl.program_id(2)

            @pl.when(kk == 0)
            def _():
                acc_ref[...] = part

            @pl.when(kk > 0)
            def _():
                acc_ref[...] += part

            @pl.when(kk == nk - 1)
            def _():
                finish(acc_ref[...])

        def product(a_ref, b_val):
            return _dot(a_ref[...].astype(BF16), b_val.astype(BF16), dims)

        if mode != "tn" and len(a_list) > 1:
            part, off = None, 0
            for a_ref in a_refs:
                w = a_ref.shape[1]
                b_val = b_refs[0][off:off + w, :] if mode == "nn" else b_refs[0][:, off:off + w]
                part = product(a_ref, b_val) if part is None else part + product(a_ref, b_val)
                off += w
            accumulate(part)
        elif len(a_list) > 1 or len(b_list) > 1:
            for pa, a_ref in enumerate(a_refs):
                for pb, b_ref in enumerate(b_refs):
                    picked = ([pl.program_id(0) == pa] if len(a_list) > 1 else []) + (
                        [pl.program_id(1) == pb] if len(b_list) > 1 else [])
                    pl.when(functools.reduce(jnp.logical_and, picked))(
                        functools.partial(lambda a_ref, b_ref: accumulate(product(a_ref, b_ref[...])), a_ref, b_ref))
        else:
            accumulate(product(a_refs[0], b_refs[0][...]))

        if comm:
            @pl.when((step[0] == grid[0] - 1) & (step[1] == grid[1] - 1) & (step[2] == grid[2] - 1))
            def _():
                _exchange_wait(c_src, c_dst, comm[1], sems)

    def held(axis, p):
        def index(i, j, kk):
            return (jnp.where((i, j)[axis] == p, kk, 0), 0)
        return index

    if mode == "tn":
        if len(a_list) > 1:
            a_specs = [pl.BlockSpec((tk, tm), held(0, p)) for p in range(len(a_list))]
        else:
            a_specs = [pl.BlockSpec((tk, tm), lambda i, j, kk: (kk, i))]
    elif len(a_list) > 1:
        a_specs = [pl.BlockSpec((tm, p.shape[1]), lambda i, j, kk: (i, 0)) for p in a_list]
    else:
        a_specs = [pl.BlockSpec((tm, tk), lambda i, j, kk: (i, kk))]
    if mode == "nt":
        b_specs = [pl.BlockSpec((tn, tk), lambda i, j, kk: (j + jo, kk))]
    elif len(b_list) > 1:
        b_specs = [pl.BlockSpec((tk, tn), held(1, p)) for p in range(len(b_list))]
    else:
        b_specs = [pl.BlockSpec((tk, tn), lambda i, j, kk: (kk, j + jo))]
    o_spec = pl.BlockSpec((tm, tn), lambda i, j, kk: (i, j))
    hbm = pl.BlockSpec(memory_space=pl.ANY)
    row_specs = [pl.BlockSpec((tm, tn), functools.partial(lambda i, j, kk, o: (i, j + o), o=off // tn))
                 for (_, off) in epi_rows]
    vec_specs = [pl.BlockSpec(v.shape, functools.partial(lambda i, j, kk, nd: (0,) * nd, nd=v.ndim)) for v in epi_vecs]
    in_specs = a_specs + b_specs + ([o_spec] if res is not None else []) + row_specs + vec_specs + [hbm] * nc
    args = (a_list + b_list + ([res] if res is not None else []) + [r for (r, _) in epi_rows] + list(epi_vecs)
            + (list(comm[0]) if comm else []))
    out_specs = [o_spec] * len(epi_dtypes) + [pl.BlockSpec((1, tn), lambda i, j, kk: (0, j))] * n_sums + [hbm] * nc
    out_shape = ([jax.ShapeDtypeStruct((m, n), dt) for dt in epi_dtypes] + [jax.ShapeDtypeStruct((1, n), F32)] * n_sums
                 + (_exchange_out_shapes(*comm) if comm else []))
    scratch = ([pltpu.VMEM((tm, tn), F32)] if nk > 1 else []) + (_exchange_sems(nc) if comm else [])
    if comm or n_sums or n_ab > 2:
        params = pltpu.CompilerParams(dimension_semantics=("arbitrary",) * 3, vmem_limit_bytes=VMEM_LIMIT,
                                      has_side_effects=bool(comm))
    else:
        params = _cparams(("parallel", "parallel", "arbitrary"))
    outs = pl.pallas_call(
        body, name=name, grid=grid, in_specs=in_specs, out_specs=out_specs, out_shape=out_shape,
        scratch_shapes=scratch, compiler_params=params,
    )(*args)
    return tuple(outs) if (comm or epi) else outs[0]


def _ln_stats(z):
    mu = jnp.mean(z, axis=-1, keepdims=True)
    zc = z - mu
    var = jnp.mean(zc * zc, axis=-1, keepdims=True)
    rstd = lax.rsqrt(var + LN_EPS)
    return zc * rstd, rstd


def _ln_bwd(dy, xhat, rstd, g):
    dxh = dy * g
    m1 = jnp.mean(dxh, axis=-1, keepdims=True)
    m2 = jnp.mean(dxh * xhat, axis=-1, keepdims=True)
    return rstd * (dxh - m1 - xhat * m2)


def _colsum(x):
    return jnp.sum(x, axis=0, keepdims=True)


def _epi_ln_fwd(acc, rows, vecs):
    z = ALPHA * rows[0] + acc
    xhat, _ = _ln_stats(z)
    x1 = xhat * vecs[0] + vecs[1]
    return [z, x1, x1], []


def _epi_ln_loss_bwd(acc, rows, vecs):
    inv_c = 1.0 / acc.shape[-1]
    z = ALPHA * rows[0] + acc
    xhat, rstd = _ln_stats(z)
    d = xhat * vecs[0] + vecs[1] - rows[1]
    dy = d * inv_c
    dz = _ln_bwd(dy, xhat, rstd, vecs[0])
    return [dz, dz], [_colsum(0.5 * d * d * inv_c), _colsum(dy * xhat), _colsum(dy)]


def _epi_ln_bwd(acc, rows, vecs):
    dy = acc + ALPHA * rows[0]
    xhat, rstd = _ln_stats(rows[1])
    dz = _ln_bwd(dy, xhat, rstd, vecs[0])
    return [dz, dz], [_colsum(dy * xhat), _colsum(dy)]


def _epi_gate_bwd(acc, rows, vecs):
    oo, gg = rows
    sg = _sigmoid(gg)
    return [acc * (gg * sg), acc * oo * (sg * (1.0 + gg * (1.0 - sg)))], []


LRU_T = 256
HALO = 8


def _log1p(y):
    u = 1.0 + y
    return jnp.where(u == 1.0, y, jnp.log(u) * (y / (u - 1.0)))


def _lru_gates(c, wa_ref, wx_ref, pv):
    c16 = c.astype(BF16)
    pre_r = jnp.concatenate(
        [_dot(c16[:, n * LRU_BLOCK:(n + 1) * LRU_BLOCK], wa_ref[n], NN) for n in range(LRU_BLOCKS)], axis=1)
    pre_i = jnp.concatenate(
        [_dot(c16[:, n * LRU_BLOCK:(n + 1) * LRU_BLOCK], wx_ref[n], NN) for n in range(LRU_BLOCKS)], axis=1)
    r = _sigmoid(pre_r + pv[5:6])
    ig = _sigmoid(pre_i + pv[6:7])
    lam = pv[7:8]
    ls = jnp.minimum(lam, 0.0) - _log1p(jnp.exp(-jnp.abs(lam)))
    la = LRU_C * r * ls
    a = jnp.exp(la)
    a2 = a * a
    m = jnp.sqrt(jnp.tanh(-la) * (a2 + 1.0))
    return c16, r, ig, ls, la, a, a2, m


def _conv(ext_ref, x, pv, t):
    return (pv[4:5] + pv[3:4] * x + pv[2:3] * ext_ref[pl.ds(HALO - 1, t), :]
            + pv[1:2] * ext_ref[pl.ds(HALO - 2, t), :] + pv[0:1] * ext_ref[pl.ds(HALO - 3, t), :])


def _rglru_fwd(h0a, pvec, wa16, wx16):
    s = h0a.shape[0]
    w = D_MODEL
    t = min(LRU_T, s)
    assert s % t == 0

    def body(ax_ref, ag_ref, pv_ref, wa_ref, wx_ref, ya_ref, h_ref, ext_ref, a_ref, u_ref, hc_ref):
        i = pl.program_id(0)

        @pl.when(i == 0)
        def _():
            ext_ref[pl.ds(0, HALO), :] = jnp.zeros((HALO, w), F32)
            hc_ref[...] = jnp.zeros((1, w), F32)

        pv = pv_ref[...]
        ax = ax_ref[...]
        ext_ref[pl.ds(HALO, t), :] = ax
        c = _conv(ext_ref, ax, pv, t)
        ext_ref[pl.ds(0, HALO), :] = ax[t - HALO:, :]
        _, _, ig, _, _, a, _, m = _lru_gates(c, wa_ref, wx_ref, pv)
        a_ref[...] = a
        u_ref[...] = m * (ig * c)

        def steps(k, h):
            base = pl.multiple_of(k * HALO, HALO)
            for r in range(HALO):
                h = a_ref[pl.ds(base + r, 1), :] * h + u_ref[pl.ds(base + r, 1), :]
                h_ref[pl.ds(base + r, 1), :] = h
            return h

        hc_ref[...] = lax.fori_loop(0, t // HALO, steps, hc_ref[...])
        ag = ag_ref[...]
        ya_ref[...] = (h_ref[...] * (ag * _sigmoid(ag))).astype(BF16)

    full = lambda shp: pl.BlockSpec(shp, lambda i: (0,) * len(shp))
    return pl.pallas_call(
        body, name="rglru_fwd", grid=(s // t,),
        in_specs=[pl.BlockSpec((t, w), lambda i: (i, 0)), pl.BlockSpec((t, w), lambda i: (i, 1)),
                  full((8, w)), full((LRU_BLOCKS, LRU_BLOCK, LRU_BLOCK)), full((LRU_BLOCKS, LRU_BLOCK, LRU_BLOCK))],
        out_specs=[pl.BlockSpec((t, w), lambda i: (i, 0)), pl.BlockSpec((t, w), lambda i: (i, 0))],
        out_shape=[jax.ShapeDtypeStruct((s, w), BF16), jax.ShapeDtypeStruct((s, w), F32)],
        scratch_shapes=[pltpu.VMEM((t + HALO, w), F32), pltpu.VMEM((t, w), F32), pltpu.VMEM((t, w), F32),
                        pltpu.VMEM((1, w), F32)],
        compiler_params=_cparams(("arbitrary",)),
    )(h0a, h0a, pvec, wa16, wx16)


def _rglru_bwd(h0a, h, dh, pvec, wa16, wx16):
    s = h0a.shape[0]
    w = D_MODEL
    t = min(LRU_T, s)
    nb = s // t
    hb = t // HALO

    def body(ax_ref, axh_ref, h_ref, hh_ref, dh_ref, pv_ref, wa_ref, wx_ref,
             dax_ref, dwa_ref, dwx_ref, dpv_ref, ext_ref, hext_ref, dcext_ref, a_ref, g_ref, gc_ref):
        i = pl.program_id(0)
        blk = nb - 1 - i

        @pl.when(i == 0)
        def _():
            dwa_ref[...] = jnp.zeros_like(dwa_ref)
            dwx_ref[...] = jnp.zeros_like(dwx_ref)
            dpv_ref[...] = jnp.zeros_like(dpv_ref)
            gc_ref[...] = jnp.zeros((1, w), F32)
            dcext_ref[pl.ds(t, HALO), :] = jnp.zeros((HALO, w), F32)

        pv = pv_ref[...]
        ax = ax_ref[...]
        keep = (blk > 0).astype(F32)
        ext_ref[pl.ds(0, HALO), :] = axh_ref[...] * keep
        ext_ref[pl.ds(HALO, t), :] = ax
        hext_ref[pl.ds(0, HALO), :] = hh_ref[...] * keep
        hext_ref[pl.ds(HALO, t), :] = h_ref[...]
        c = _conv(ext_ref, ax, pv, t)
        c16, r, ig, ls, _, a, a2, m = _lru_gates(c, wa_ref, wx_ref, pv)
        a_ref[...] = a

        def steps(k, carry):
            base = pl.multiple_of(t - HALO - k * HALO, HALO)
            for r in reversed(range(HALO)):
                g = dh_ref[pl.ds(base + r, 1), :] + carry
                g_ref[pl.ds(base + r, 1), :] = g
                carry = a_ref[pl.ds(base + r, 1), :] * g
            return carry

        gc_ref[...] = lax.fori_loop(0, t // HALO, steps, gc_ref[...])
        g = g_ref[...]
        hprev = hext_ref[pl.ds(HALO - 1, t), :]
        gm = g * m
        d_la = g * hprev * a - (g * ig * c) * a2 / m
        d_pr = d_la * (LRU_C * ls) * r * (1.0 - r)
        d_pi = gm * c * ig * (1.0 - ig)
        dc = gm * ig
        dpr16 = d_pr.astype(BF16)
        dpi16 = d_pi.astype(BF16)
        dc_parts = []
        for n in range(LRU_BLOCKS):
            sl = slice(n * LRU_BLOCK, (n + 1) * LRU_BLOCK)
            dwa_ref[n] += _dot(c16[:, sl], dpr16[:, sl], TN)
            dwx_ref[n] += _dot(c16[:, sl], dpi16[:, sl], TN)
            dc_parts.append(_dot(dpr16[:, sl], wa_ref[n], NT) + _dot(dpi16[:, sl], wx_ref[n], NT))
        dc = dc + jnp.concatenate(dc_parts, axis=1)
        dcext_ref[pl.ds(0, t), :] = dc
        dax = (pv[3:4] * dc + pv[2:3] * dcext_ref[pl.ds(1, t), :] + pv[1:2] * dcext_ref[pl.ds(2, t), :]
               + pv[0:1] * dcext_ref[pl.ds(3, t), :])
        dax_ref[...] = dax.astype(BF16)
        dcext_ref[pl.ds(t, HALO), :] = dc[:HALO, :]
        sums = [_colsum(dc * ext_ref[pl.ds(HALO - 3 + kk, t), :]) for kk in range(4)]
        sums += [_colsum(dc), _colsum(d_pr), _colsum(d_pi), _colsum(d_la * (LRU_C * r))]
        for kk, part in enumerate(sums):
            dpv_ref[pl.ds(kk, 1), :] += part

        @pl.when(i == nb - 1)
        def _():
            lam = pv[7:8]
            dls = dpv_ref[pl.ds(7, 1), :]
            dpv_ref[pl.ds(7, 1), :] = dls * _sigmoid(-lam)

    full = lambda shp: pl.BlockSpec(shp, lambda i: (0,) * len(shp))
    rev = lambda cb: pl.BlockSpec((t, w), functools.partial(lambda i, cb: (nb - 1 - i, cb), cb=cb))
    halo = lambda cb: pl.BlockSpec(
        (HALO, w), functools.partial(lambda i, cb: (jnp.maximum((nb - 1 - i) * hb - 1, 0), cb), cb=cb))
    gw = (LRU_BLOCKS, LRU_BLOCK, LRU_BLOCK)
    return pl.pallas_call(
        body, name="rglru_bwd", grid=(nb,),
        in_specs=[rev(0), halo(0), rev(0), halo(0), rev(0), full((8, w)), full(gw), full(gw)],
        out_specs=[rev(0), full(gw), full(gw), full((8, w))],
        out_shape=[jax.ShapeDtypeStruct((s, w), BF16), jax.ShapeDtypeStruct(gw, F32),
                   jax.ShapeDtypeStruct(gw, F32), jax.ShapeDtypeStruct((8, w), F32)],
        scratch_shapes=[pltpu.VMEM((t + HALO, w), F32), pltpu.VMEM((t + HALO, w), F32),
                        pltpu.VMEM((t + HALO, w), F32), pltpu.VMEM((t, w), F32), pltpu.VMEM((t, w), F32),
                        pltpu.VMEM((1, w), F32)],
        compiler_params=_cparams(("arbitrary",)),
    )(h0a, h0a, h, h, dh, pvec, wa16, wx16)


SB_T = 256
SB_SUB = 4


def _split16(x):
    hi = x.astype(BF16)
    lo = (x - hi.astype(F32)).astype(BF16)
    return jnp.concatenate([hi, lo], axis=0)


def _sb_tile(q, k, scale, tri, run, causal):
    tq = q.shape[0]
    z = _dot(q, k, NT) * scale
    log1mb = -(jnp.maximum(z, 0.0) + jnp.log(1.0 + jnp.exp(-jnp.abs(z))))
    if causal is not None:
        log1mb = jnp.where(causal, log1mb, 0.0)
    cs = _dot(_split16(log1mb), tri, NN)
    cs = cs[:tq] + cs[tq:]
    wgt = jnp.exp(z + cs + run)
    if causal is not None:
        wgt = jnp.where(causal, wgt, 0.0)
    return z, wgt, run + cs[:, 0:1]


SB_DEAD = -105.0


def _sb_alive(run):
    return (jnp.max(run) > SB_DEAD).astype(jnp.int32)


def _sb_more(carry):
    return (carry[0] >= 0) & (carry[1] > 0)


def _sb_fwd(qkv, bg):
    s = qkv.shape[0]
    t = min(SB_T, s)
    nq = s // t
    dh = SB_HEAD_DIM
    scale = 1.0 / math.sqrt(dh)

    sub = SB_SUB if nq % SB_SUB == 0 else 1

    def body(q_ref, k_ref, v_ref, bg_ref, o_ref, y_ref):
        row = lax.broadcasted_iota(jnp.int32, (t, t), 0)
        col = lax.broadcasted_iota(jnp.int32, (t, t), 1)
        tri = jnp.where(row >= col, 1.0, 0.0).astype(BF16)
        causal = col < row
        for u in range(sub):
            rows = pl.ds(u * t, t)
            query_block(pl.program_id(1) * sub + u, q_ref[rows, :], tri, causal, k_ref, v_ref, bg_ref.at[rows, :],
                        o_ref.at[rows, :], y_ref.at[rows, :])

    def query_block(iq, q, tri, causal, k_ref, v_ref, bg_ref, o_ref, y_ref):
        def tile(kb, run, mask):
            off = pl.multiple_of(kb * t, t)
            _, wgt, run = _sb_tile(q, k_ref[pl.ds(off, t), :], scale, tri, run, mask)
            return run, _dot(wgt.astype(BF16), v_ref[pl.ds(off, t), :], NN)

        has_left = iq > 0
        run_d, acc_d = tile(iq, jnp.zeros((t, 1), F32), causal)
        run_l, acc_l = tile(jnp.maximum(iq - 1, 0), run_d, None)
        run = jnp.where(has_left, run_l, run_d)
        acc = acc_d + jnp.where(has_left, acc_l, 0.0)

        def loop(carry):
            kb, _, run, acc = carry
            run, part = tile(kb, run, None)
            return kb - 1, _sb_alive(run), run, acc + part

        _, _, run, acc = lax.while_loop(_sb_more, loop, (iq - 2, _sb_alive(run), run, acc))
        o_ref[...] = acc
        g = bg_ref[...]
        y_ref[...] = (acc * (g * _sigmoid(g))).astype(BF16)

    blk = lambda off: pl.BlockSpec((sub * t, dh), functools.partial(lambda h, i, off: (i, h + off), off=off))
    col = lambda off: pl.BlockSpec((s, dh), functools.partial(lambda h, i, off: (0, h + off), off=off))
    return pl.pallas_call(
        body, name="sb_fwd", grid=(SB_HEADS, nq // sub),
        in_specs=[blk(0), col(SB_HEADS), col(2 * SB_HEADS), blk(0)],
        out_specs=[blk(0), blk(0)],
        out_shape=[jax.ShapeDtypeStruct((s, SB_HEADS * dh), F32), jax.ShapeDtypeStruct((s, SB_HEADS * dh), BF16)],
        compiler_params=_cparams(("parallel", "arbitrary")),
    )(qkv, qkv, qkv, bg)


def _sb_bwd(qkv, dob):
    s = qkv.shape[0]
    t = min(SB_T, s)
    nq = s // t
    dh = SB_HEAD_DIM
    scale = 1.0 / math.sqrt(dh)

    sub = SB_SUB if nq % SB_SUB == 0 else 1

    def body(q_ref, k_ref, v_ref, do_ref, dq_ref, dk_out, dv_out, e_ref, b_ref, dk_ref, dv_ref):
        @pl.when(pl.program_id(1) == 0)
        def _():
            dk_ref[...] = jnp.zeros_like(dk_ref)
            dv_ref[...] = jnp.zeros_like(dv_ref)

        row = lax.broadcasted_iota(jnp.int32, (t, t), 0)
        col = lax.broadcasted_iota(jnp.int32, (t, t), 1)
        tri = jnp.where(row >= col, 1.0, 0.0).astype(BF16)
        tri_x = jnp.where(row < col, 1.0, 0.0).astype(BF16)
        for u in range(sub):
            rows = pl.ds(u * t, t)
            query_block(pl.program_id(1) * sub + u, q_ref[rows, :], do_ref[rows, :], tri, tri_x, col < row,
                        k_ref, v_ref, dq_ref.at[rows, :], dk_ref, dv_ref, e_ref, b_ref)

        @pl.when(pl.program_id(1) == nq // sub - 1)
        def _():
            dk_out[...] = dk_ref[...].astype(BF16)
            dv_out[...] = dv_ref[...].astype(BF16)

    def query_block(iq, q, do, tri, tri_x, causal, k_ref, v_ref, dq_ref, dk_ref, dv_ref, e_ref, b_ref):
        def sweep1(kb, slot, run, mask, live=None):
            off = pl.multiple_of(kb * t, t)
            v = v_ref[pl.ds(off, t), :]
            z, wgt, run = _sb_tile(q, k_ref[pl.ds(off, t), :], scale, tri, run, mask)
            if live is not None:
                wgt = jnp.where(live, wgt, 0.0)
            e_ref[slot] = wgt * _dot(do, v, NT)
            b_ref[slot] = _sigmoid(z)
            dv_ref[pl.ds(off, t), :] += _dot(wgt.astype(BF16), do, TN)
            return run

        has_left = iq > 0
        left = jnp.maximum(iq - 1, 0)
        left_slot = jnp.where(has_left, iq - 1, nq)
        run_d = sweep1(iq, iq, jnp.zeros((t, 1), F32), causal)
        run = jnp.where(has_left, sweep1(left, left_slot, run_d, None, has_left), run_d)

        def loop1(carry):
            run = sweep1(carry[0], carry[0], carry[2], None)
            return carry[0] - 1, _sb_alive(run), run

        first = lax.while_loop(_sb_more, loop1, (iq - 2, _sb_alive(run), run))[0] + 1

        def sweep2(kb, slot, pre, mask, live=None):
            off = pl.multiple_of(kb * t, t)
            e = e_ref[slot]
            beta = b_ref[slot]
            ps = _dot(_split16(e), tri_x, NN)
            ps = ps[:t] + ps[t:]
            dz = e * (1.0 - beta) - beta * (ps + pre)
            if mask is not None:
                dz = jnp.where(mask, dz, 0.0)
            if live is not None:
                dz = jnp.where(live, dz, 0.0)
            dz16 = (dz * scale).astype(BF16)
            dk_ref[pl.ds(off, t), :] += _dot(dz16, q, TN)
            return pre + ps[:, t - 1:t] + e[:, t - 1:t], _dot(dz16, k_ref[pl.ds(off, t), :], NN)

        def loop2(kb, carry):
            pre, part = sweep2(kb, kb, carry[0], None)
            return pre, carry[1] + part

        pre, dq = lax.fori_loop(first, iq - 1, loop2, (jnp.zeros((t, 1), F32), jnp.zeros((t, dh), F32)))
        pre, dq_l = sweep2(left, left_slot, pre, None, has_left)
        _, dq_d = sweep2(iq, iq, pre, causal)
        dq_ref[...] = (dq + dq_l + dq_d).astype(BF16)

    blk = lambda off: pl.BlockSpec((sub * t, dh), functools.partial(lambda h, i, off: (i, h + off), off=off))
    col = lambda off: pl.BlockSpec((s, dh), functools.partial(lambda h, i, off: (0, h + off), off=off))
    wide = SB_HEADS * dh
    return pl.pallas_call(
        body, name="sb_bwd", grid=(SB_HEADS, nq // sub),
        in_specs=[blk(0), col(SB_HEADS), col(2 * SB_HEADS), blk(0)],
        out_specs=[blk(0), col(0), col(0)],
        out_shape=[jax.ShapeDtypeStruct((s, wide), BF16)] * 3,
        scratch_shapes=[pltpu.VMEM((nq + 1, t, t), F32), pltpu.VMEM((nq + 1, t, t), F32),
                        pltpu.VMEM((s, dh), F32), pltpu.VMEM((s, dh), F32)],
        compiler_params=_cparams(("parallel", "arbitrary")),
    )(qkv, qkv, qkv, dob)


def _alibi_slope(h):
    return float(2.0 ** (-8.0 * (h + 1) / C_HEADS))


GROUP_ROWS = C_GROUP * WINDOW


def _swa_group_consts(n, c, sink_ref):
    qb = WINDOW
    head = lax.shift_right_logical(lax.broadcasted_iota(jnp.int32, (GROUP_ROWS, 1), 0), qb.bit_length() - 1)
    slope = jnp.zeros((GROUP_ROWS, 1), F32)
    sink = jnp.zeros((GROUP_ROWS, 1), F32)
    for hh in range(C_GROUP):
        slope = jnp.where(head == hh, _alibi_slope(c * C_GROUP + hh), slope)
        sink = jnp.where(head == hh, sink_ref[c * C_GROUP + hh], sink)
    i = lax.broadcasted_iota(jnp.int32, (GROUP_ROWS, 2 * qb), 0) & (qb - 1)
    j = lax.broadcasted_iota(jnp.int32, (GROUP_ROWS, 2 * qb), 1)
    d = i - j + qb
    valid = (d >= 0) & (d < WINDOW) & ((j >= qb) | (n > 0))
    return slope, sink, d.astype(F32), valid


def _swa_probs(qg, kw, sink, slope, dist, valid, scale):
    sc = _dot(qg, kw, NT) * scale - slope * dist
    sc = jnp.where(valid, sc, -1e30)
    m = jnp.maximum(jnp.max(sc, axis=-1, keepdims=True), sink)
    p = jnp.exp(sc - m)
    ps = jnp.exp(sink - m)
    inv = 1.0 / (jnp.sum(p, axis=-1, keepdims=True) + ps)
    return p * inv, ps * inv


def _stack_heads(x, c):
    hd = C_HEAD_DIM
    return jnp.concatenate([x[:, (c * C_GROUP + hh) * hd:(c * C_GROUP + hh + 1) * hd] for hh in range(C_GROUP)], axis=0)


def _swa_fwd(q16, gate, kvp, sinks):
    s = q16.shape[0]
    qb = WINDOW
    hd = C_HEAD_DIM
    scale = 1.0 / math.sqrt(hd)
    kvw = C_KV_HEADS * hd

    def body(sink_ref, q_ref, g_ref, kp_ref, ko_ref, o_ref, y_ref):
        n = pl.program_id(0)
        q = q_ref[...]
        kv = jnp.concatenate([kp_ref[...], ko_ref[...]], axis=0)
        for c in range(C_KV_HEADS):
            slope, sink, dist, valid = _swa_group_consts(n, c, sink_ref)
            kw = kv[:, c * hd:(c + 1) * hd]
            vw = kv[:, kvw + c * hd:kvw + (c + 1) * hd]
            p, _ = _swa_probs(_stack_heads(q, c), kw, sink, slope, dist, valid, scale)
            og = _dot(p.astype(BF16), vw, NN)
            for hh in range(C_GROUP):
                h = c * C_GROUP + hh
                o_ref[:, h * hd:(h + 1) * hd] = og[hh * qb:(hh + 1) * qb, :]
        g = g_ref[...]
        y_ref[...] = (o_ref[...] * (g * _sigmoid(g))).astype(BF16)

    wide = C_HEADS * hd
    return pl.pallas_call(
        body, name="swa_fwd", grid=(s // qb,),
        in_specs=[pl.BlockSpec(memory_space=pltpu.SMEM),
                  pl.BlockSpec((qb, wide), lambda n: (n, 0)), pl.BlockSpec((qb, wide), lambda n: (n, 0)),
                  pl.BlockSpec((qb, 2 * kvw), lambda n: (n, 0)), pl.BlockSpec((qb, 2 * kvw), lambda n: (n + 1, 0))],
        out_specs=[pl.BlockSpec((qb, wide), lambda n: (n, 0)), pl.BlockSpec((qb, wide), lambda n: (n, 0))],
        out_shape=[jax.ShapeDtypeStruct((s, wide), F32), jax.ShapeDtypeStruct((s, wide), BF16)],
        compiler_params=_cparams(("arbitrary",)),
    )(sinks, q16, gate, kvp, kvp)


def _swa_bwd(q16, do16, kvp, sinks):
    s = q16.shape[0]
    qb = WINDOW
    hd = C_HEAD_DIM
    scale = 1.0 / math.sqrt(hd)
    kvw = C_KV_HEADS * hd
    nblk = s // qb

    def body(sink_ref, q_ref, do_ref, kp_ref, ko_ref, dq_ref, dkv_ref, ds_ref, sacc_ref):
        n = pl.program_id(0)

        @pl.when(n == 0)
        def _():
            dkv_ref[...] = jnp.zeros_like(dkv_ref)
            sacc_ref[...] = jnp.zeros_like(sacc_ref)

        q = q_ref[...]
        do = do_ref[...]
        kv = jnp.concatenate([kp_ref[...], ko_ref[...]], axis=0)
        off = pl.multiple_of(n * qb, qb)
        for c in range(C_KV_HEADS):
            slope, sink, dist, valid = _swa_group_consts(n, c, sink_ref)
            kw = kv[:, c * hd:(c + 1) * hd]
            vw = kv[:, kvw + c * hd:kvw + (c + 1) * hd]
            qg = _stack_heads(q, c)
            dog = _stack_heads(do, c)
            p, ps = _swa_probs(qg, kw, sink, slope, dist, valid, scale)
            dp = _dot(dog, vw, NT)
            dd = jnp.sum(p * dp, axis=-1, keepdims=True)
            ds16 = (p * (dp - dd) * scale).astype(BF16)
            sacc_ref[c] += ps * dd
            dqg = _dot(ds16, kw, NN).astype(BF16)
            for hh in range(C_GROUP):
                h = c * C_GROUP + hh
                dq_ref[:, h * hd:(h + 1) * hd] = dqg[hh * qb:(hh + 1) * qb, :]
            dkv_ref[pl.ds(off, 2 * qb), c * hd:(c + 1) * hd] += _dot(ds16, qg, TN)
            dkv_ref[pl.ds(off, 2 * qb), kvw + c * hd:kvw + (c + 1) * hd] += _dot(p.astype(BF16), dog, TN)

        @pl.when(n == nblk - 1)
        def _():
            lane = lax.broadcasted_iota(jnp.int32, (1, 128), 1)
            row = jnp.zeros((1, 128), F32)
            for c in range(C_KV_HEADS):
                for hh in range(C_GROUP):
                    tot = jnp.sum(sacc_ref[c, pl.ds(hh * qb, qb), :], axis=0, keepdims=True)
                    row = jnp.where(lane == c * C_GROUP + hh, -tot, row)
            ds_ref[...] = row

    wide = C_HEADS * hd
    return pl.pallas_call(
        body, name="swa_bwd", grid=(nblk,),
        in_specs=[pl.BlockSpec(memory_space=pltpu.SMEM),
                  pl.BlockSpec((qb, wide), lambda n: (n, 0)), pl.BlockSpec((qb, wide), lambda n: (n, 0)),
                  pl.BlockSpec((qb, 2 * kvw), lambda n: (n, 0)), pl.BlockSpec((qb, 2 * kvw), lambda n: (n + 1, 0))],
        out_specs=[pl.BlockSpec((qb, wide), lambda n: (n, 0)),
                   pl.BlockSpec((s + qb, 2 * kvw), lambda n: (0, 0)), pl.BlockSpec((1, 128), lambda n: (0, 0))],
        out_shape=[jax.ShapeDtypeStruct((s, wide), BF16), jax.ShapeDtypeStruct((s + qb, 2 * kvw), F32),
                   jax.ShapeDtypeStruct((1, 128), F32)],
        scratch_shapes=[pltpu.VMEM((C_KV_HEADS, GROUP_ROWS, 1), F32)],
        compiler_params=_cparams(("arbitrary",)),
    )(sinks, q16, do16, kvp, kvp)


def _adamw(parts, w, m, v, *, name, tr=496):
    npart, r, c = parts.shape
    tr = min(tr, r)
    assert r % tr == 0 and tr % 16 == 0
    c1 = 1.0 / (1.0 - ADAM_B1 ** ADAM_STEP)
    c2 = 1.0 / (1.0 - ADAM_B2 ** ADAM_STEP)

    def body(p_ref, w_ref, m_ref, v_ref, g_ref, d_ref, nm_ref, nv_ref):
        g = p_ref[0].astype(F32)
        for j in range(1, npart):
            g = g + p_ref[j].astype(F32)
        nm = ADAM_B1 * m_ref[...] + (1.0 - ADAM_B1) * g
        nv = ADAM_B2 * v_ref[...] + (1.0 - ADAM_B2) * (g * g)
        g_ref[...] = g
        nm_ref[...] = nm
        nv_ref[...] = nv
        d_ref[...] = -ADAM_LR * ((nm * c1) / (jnp.sqrt(nv * c2) + ADAM_EPS) + ADAM_WD * w_ref[...])

    spec = pl.BlockSpec((tr, c), lambda i: (i, 0))
    return pl.pallas_call(
        body, name=name, grid=(r // tr,),
        in_specs=[pl.BlockSpec((npart, tr, c), lambda i: (0, i, 0)), spec, spec, spec],
        out_specs=[spec] * 4, out_shape=[jax.ShapeDtypeStruct((r, c), F32)] * 4,
        compiler_params=_cparams(("parallel",)),
    )(parts, w, m, v)


GATHER = "gather"
A2A = "a2a"
GATHER_COLS = "gather_cols"
A2A_COLS = "a2a_cols"


def _exchange(bufs, gather, *, name):
    nb = len(bufs)

    def body(*refs):
        _exchange_start(refs[:nb], refs[nb:2 * nb], gather, refs[2 * nb:])
        _exchange_wait(refs[:nb], refs[nb:2 * nb], gather, refs[2 * nb:])

    hbm = pl.BlockSpec(memory_space=pl.ANY)
    return pl.pallas_call(
        body, name=name, in_specs=[hbm] * nb, out_specs=[hbm] * nb, out_shape=_exchange_out_shapes(bufs, gather),
        scratch_shapes=_exchange_sems(nb),
        compiler_params=pltpu.CompilerParams(has_side_effects=True),
    )(*bufs)


def _gather_two_level(bufs, modes, *, name):
    nb = len(bufs)

    def body(*refs):
        src, dst = refs[:nb], refs[nb:2 * nb]
        send_sems, recv_sems, local_sems = refs[2 * nb:]
        x, y, c = lax.axis_index("x"), lax.axis_index("y"), lax.axis_index("c")
        me = 4 * x + 2 * y + c
        here, sibling = (x, y, c), (x, y, 1 - c)

        def copy(b, k, origin, to, source):
            return pltpu.make_async_remote_copy(
                src_ref=source, dst_ref=_slot(dst[b], modes[b], origin), send_sem=send_sems.at[b, k - 1],
                recv_sem=recv_sems.at[b, k - 1], device_id=to, device_id_type=pl.DeviceIdType.MESH)

        local = [pltpu.make_async_copy(src[b], _slot(dst[b], modes[b], me), local_sems.at[b]) for b in range(nb)]
        sends = [copy(b, k, me, (x ^ (k >> 2), y ^ ((k >> 1) & 1), c ^ (k & 1)), src[b])
                 for k in (1, 2, 4, 6) for b in range(nb)]
        for cp in local + sends:
            cp.start()
        for j in (2, 4, 6):
            for b in range(nb):
                copy(b, j, me ^ j, here, src[b]).wait_recv()
                passed = copy(b, j ^ 1, me ^ j, sibling, _slot(dst[b], modes[b], me ^ j))
                passed.start()
                sends.append(passed)
        for k in (1, 3, 5, 7):
            for b in range(nb):
                copy(b, k, me ^ k, here, src[b]).wait_recv()
        for cp in sends:
            cp.wait_send()
        for cp in local:
            cp.wait()

    hbm = pl.BlockSpec(memory_space=pl.ANY)
    return pl.pallas_call(
        body, name=name, in_specs=[hbm] * nb, out_specs=[hbm] * nb, out_shape=_exchange_out_shapes(bufs, modes),
        scratch_shapes=_exchange_sems(nb),
        compiler_params=pltpu.CompilerParams(has_side_effects=True),
    )(*bufs)


def _exchange_out_shapes(bufs, gather):
    shapes = {GATHER: lambda s: (N_DEV,) + s, A2A: lambda s: s, GATHER_COLS: lambda s: (s[0], N_DEV * s[1]),
              A2A_COLS: lambda s: (N_DEV, s[0], s[1] // N_DEV)}
    return [jax.ShapeDtypeStruct(shapes[g](tuple(b.shape)), b.dtype) for b, g in zip(bufs, gather)]


def _sent(src, mode, peer):
    if mode == A2A:
        return src.at[peer]
    if mode == A2A_COLS:
        w = src.shape[1] // N_DEV
        return src.at[:, pl.ds(pl.multiple_of(peer * w, 128), w)]
    return src


def _slot(dst, mode, dev):
    if mode == GATHER_COLS:
        w = dst.shape[1] // N_DEV
        return dst.at[:, pl.ds(pl.multiple_of(dev * w, 128), w)]
    return dst.at[dev]


def _exchange_sems(nb):
    return [pltpu.SemaphoreType.DMA((nb, N_DEV - 1)), pltpu.SemaphoreType.DMA((nb, N_DEV - 1)),
            pltpu.SemaphoreType.DMA((nb,))]


def _exchange_copies(src, dst, gather, sems):
    send_sems, recv_sems, local_sems = sems
    x, y, c = lax.axis_index("x"), lax.axis_index("y"), lax.axis_index("c")
    me = 4 * x + 2 * y + c
    local, sends, recvs = [], [], []
    for b in range(len(src)):
        mine = _sent(src[b], gather[b], me)
        local.append(pltpu.make_async_copy(mine, _slot(dst[b], gather[b], me), local_sems.at[b]))
        for k in range(1, N_DEV):
            px, py, pc = x ^ (k >> 2), y ^ ((k >> 1) & 1), c ^ (k & 1)
            peer = 4 * px + 2 * py + pc
            pair = dict(send_sem=send_sems.at[b, k - 1], recv_sem=recv_sems.at[b, k - 1],
                        device_id_type=pl.DeviceIdType.MESH)
            sends.append(pltpu.make_async_remote_copy(
                src_ref=_sent(src[b], gather[b], peer), dst_ref=_slot(dst[b], gather[b], me), device_id=(px, py, pc),
                **pair))
            recvs.append(pltpu.make_async_remote_copy(
                src_ref=mine, dst_ref=_slot(dst[b], gather[b], peer), device_id=(x, y, c), **pair))
    return local, sends, recvs


def _exchange_start(src, dst, gather, sems):
    local, sends, _ = _exchange_copies(src, dst, gather, sems)
    for cp in local + sends:
        cp.start()


def _exchange_wait(src, dst, gather, sems):
    local, sends, recvs = _exchange_copies(src, dst, gather, sems)
    for cp in sends:
        cp.wait_send()
    for cp in recvs:
        cp.wait_recv()
    for cp in local:
        cp.wait()


R_SMALL, R_REPL = 16, 48
REPL_ROWS = N_DEV * R_REPL
REPL_LEN = REPL_ROWS * D_MODEL


def _small_block(parts):
    flat = jnp.concatenate(parts, axis=-1)
    lead = flat.ndim - 1
    return jnp.pad(flat[..., None, :], [(0, 0)] * lead + [(0, R_SMALL - 1), (0, D_MODEL - flat.shape[-1])])


def _small_vectors(conv_w, ln_g, ln_b):
    block = _small_block([conv_w.reshape(512), ln_g.reshape(128), ln_b.reshape(128)])
    return jnp.concatenate([block, jnp.zeros((R_REPL, D_MODEL), F32)], axis=0)


REPL_SHAPES = ((1, 8, 128, 128), (1, 8, 128, 128), (1, 1024), (1, 1024), (1, 1024), (1, 1024), (1, 1024), (1, 1024),
               (1, 16))


def _pack_repl(parts):
    flat = jnp.concatenate([p.reshape(-1) for p in parts])
    return jnp.concatenate([flat, jnp.zeros((REPL_LEN - flat.shape[0],), F32)]).reshape(REPL_ROWS, D_MODEL)


def _unpack_repl(p):
    flat = p.reshape(-1)
    out, o = [], 0
    for shp in REPL_SHAPES:
        n = math.prod(shp)
        out.append(flat[o:o + n].reshape(shp))
        o += n
    return out


def kernel(x, e_w_in, e_conv_w, e_conv_b, e_w_gate_a, e_b_gate_a, e_w_gate_x, e_b_gate_x, e_lru_lambda, e_w_out, e_ln_g, e_ln_b, o_w_in, o_sinks, o_w_out, o_ln_g, o_ln_b, loss_target, m_e_w_in, m_e_conv_w, m_e_conv_b, m_e_w_gate_a, m_e_b_gate_a, m_e_w_gate_x, m_e_b_gate_x, m_e_lru_lambda, m_e_w_out, m_e_ln_g, m_e_ln_b, m_o_w_in, m_o_sinks, m_o_w_out, m_o_ln_g, m_o_ln_b, v_e_w_in, v_e_conv_w, v_e_conv_b, v_e_w_gate_a, v_e_b_gate_a, v_e_w_gate_x, v_e_b_gate_x, v_e_lru_lambda, v_e_w_out, v_e_ln_g, v_e_ln_b, v_o_w_in, v_o_sinks, v_o_w_out, v_o_ln_g, v_o_ln_b):
    d = D_MODEL
    x0 = x[0]
    target = loss_target[0]
    s = x0.shape[0]

    spack = _small_block([e_conv_w.reshape(512), o_ln_g.reshape(128), o_ln_b.reshape(128)])
    wi_e, sall = _gather_two_level([e_w_in[0].astype(BF16), spack], [GATHER_COLS, GATHER], name="gather_w_in")
    conv_w = sall[:, 0, 0:512].reshape(N_DEV, 4, 128).transpose(1, 0, 2).reshape(4, d)
    ln_g_o = sall[:, 0, 512:640].reshape(1, d)
    ln_b_o = sall[:, 0, 640:768].reshape(1, d)
    pvec = jnp.concatenate([conv_w, e_conv_b, e_b_gate_a, e_b_gate_x, e_lru_lambda], axis=0)
    wa16 = e_w_gate_a[0].astype(BF16)
    wx16 = e_w_gate_x[0].astype(BF16)
    sinks = o_sinks[0]

    x0_16 = x0.astype(BF16)
    h0a, wall_out_e = _matmul(x0_16, wi_e, mode="nn", n_out=2 * d, b_off=0, name="l0_in_a",
                              comm=([e_w_out[0].astype(BF16)], [GATHER]))
    qkv, wall_in_o = _matmul(x0_16, wi_e, mode="nn", n_out=3 * d, b_off=2 * d, out_dtype=BF16, name="l0_in_qkv",
                             comm=([o_w_in[0].astype(BF16)], [GATHER]))
    bg, wall_out_o = _matmul(x0_16, wi_e, mode="nn", n_out=d, b_off=5 * d, name="l0_in_bg",
                             comm=([o_w_out[0].astype(BF16)], [GATHER]))
    wo_e = wall_out_e.reshape(2 * d, d)
    wi_o = wall_in_o.transpose(1, 0, 2).reshape(d, 2304)
    wi_o = jnp.concatenate([wi_o[:, :1024], wi_o[:, 1280:], wi_o[:, 1024:1280]], axis=1)
    wo_o = wall_out_o.reshape(d, d)
    ya, hst = _rglru_fwd(h0a, pvec, wa16, wx16)
    ob, yb = _sb_fwd(qkv, bg)
    z0, x1, x1_16 = _matmul([ya, yb], wo_e, mode="nn", tm=512, name="l0_out",
                            epi=(_epi_ln_fwd, [(x0, 0)], [e_ln_g, e_ln_b], [F32, F32, BF16], 0))

    q1 = _matmul(x1_16, wi_o, mode="nn", n_out=d, b_off=0, out_dtype=BF16, name="l1_in_q")
    g1 = _matmul(x1_16, wi_o, mode="nn", n_out=d, b_off=d, name="l1_in_g")
    kv1 = _matmul(x1_16, wi_o, mode="nn", n_out=256, b_off=2 * d, tn=256, out_dtype=BF16, name="l1_in_kv")
    kvp = jnp.concatenate([jnp.zeros((WINDOW, 256), BF16), kv1], axis=0)
    o1, y1 = _swa_fwd(q1, g1, kvp, sinks)
    dz1, dz1_16, loss_cols, dg_o, db_o = _matmul(
        y1, wo_o, mode="nn", tm=512, name="l1_out",
        epi=(_epi_ln_loss_bwd, [(x1, 0), (target, 0)], [ln_g_o, ln_b_o], [F32, BF16], 3))
    loss_hi = jnp.sum(loss_cols).astype(BF16).astype(F32)
    loss_terms = jnp.stack([loss_hi, jnp.sum(loss_cols) - loss_hi]).reshape(1, 2)

    dwo_o = _matmul(y1, dz1_16, mode="tn", out_dtype=BF16, name="l1_dwout")
    do1, dg1, parts_out_o = _matmul(dz1_16, wo_o, mode="nt", tm=512, name="l1_dy",
                                    comm=([dwo_o.reshape(N_DEV, 128, d)], [A2A]),
                                    epi=(_epi_gate_bwd, [(o1, 0), (g1, 0)], [], [BF16, BF16], 0))
    dq1, dkvp, dsink = _swa_bwd(q1, do1, kvp, sinks)
    dh1 = jnp.concatenate([dq1, dg1, dkvp[WINDOW:].astype(BF16)], axis=1)
    dwi_o = _matmul(x1_16, dh1, mode="tn", tn=1152, out_dtype=BF16, name="l1_dwin")
    dwi_o = jnp.concatenate([dwi_o[:, :1024], dwi_o[:, 2048:], dwi_o[:, 1024:2048]], axis=1)
    dz0, dz0_16, dg_e, db_e, parts_in_o = _matmul(
        dh1, wi_o, mode="nt", tm=512, tk=2304, name="l1_dx",
        comm=([dwi_o.reshape(d, N_DEV, 288).transpose(1, 0, 2)], [A2A]),
        epi=(_epi_ln_bwd, [(dz1, 0), (z0, 0)], [e_ln_g], [F32, BF16], 2))

    dwo_e = _matmul([ya, yb], dz0_16, mode="tn", out_dtype=BF16, name="l0_dwout")
    dhst, dag, parts_out_e = _matmul(
        dz0_16, wo_e, mode="nt", tm=512, n_out=d, b_off=0, name="l0_dy_a",
        comm=([dwo_e.reshape(N_DEV, 256, d)], [A2A]),
        epi=(_epi_gate_bwd, [(hst, 0), (h0a, d)], [], [F32, BF16], 0))
    dob, dbg = _matmul(dz0_16, wo_e, mode="nt", tm=512, n_out=d, b_off=d, name="l0_dy_b",
                       epi=(_epi_gate_bwd, [(ob, 0), (bg, 0)], [], [BF16, BF16], 0))
    dq0, dk0, dv0 = _sb_bwd(qkv, dob)
    dax, dwa, dwx, dpv = _rglru_bwd(h0a, hst, dhst, pvec, wa16, wx16)
    dh0 = [dax, dag, dq0, dk0, dv0, dbg]
    dwi_e = _matmul(x0_16, dh0, mode="tn", out_dtype=BF16, name="l0_dwin")
    repl = _pack_repl([dwa, dwx, dpv[4:5], dpv[5:6], dpv[6:7], dpv[7:8], dg_e, db_e, dsink[:, :C_HEADS], loss_terms])
    small = jnp.concatenate([
        _small_block([dpv[0:4].reshape(4, N_DEV, 128).transpose(1, 0, 2).reshape(N_DEV, 512),
                      dg_o.reshape(N_DEV, 128), db_o.reshape(N_DEV, 128)]),
        repl.reshape(N_DEV, R_REPL, d)], axis=1).astype(BF16)
    grad_x, parts_in_e, parts_small = _matmul(dh0, wi_e, mode="nt", tm=512, res=dz0, alpha=ALPHA, name="l0_dx",
                                              comm=([dwi_e, small], [A2A_COLS, A2A]))

    res_in_e = _adamw(parts_in_e, e_w_in[0], m_e_w_in[0], v_e_w_in[0], tr=512, name="adamw_w_in_e")
    res_out_e = _adamw(parts_out_e, e_w_out[0], m_e_w_out[0], v_e_w_out[0], name="adamw_w_out_e")
    res_in_o = _adamw(parts_in_o, o_w_in[0], m_o_w_in[0], v_o_w_in[0], tr=512, name="adamw_w_in_o")
    res_out_o = _adamw(parts_out_o, o_w_out[0], m_o_w_out[0], v_o_w_out[0], name="adamw_w_out_o")
    res_small = _adamw(parts_small, _small_vectors(e_conv_w, o_ln_g, o_ln_b),
                       _small_vectors(m_e_conv_w, m_o_ln_g, m_o_ln_b),
                       _small_vectors(v_e_conv_w, v_o_ln_g, v_o_ln_b), name="adamw_vectors")

    (g_repl,) = _exchange([res_small[0][R_SMALL:]], [GATHER], name="gather_repl_grads")
    g_repl = g_repl.reshape(1, REPL_ROWS, d)
    w_r = _pack_repl([e_w_gate_a, e_w_gate_x, e_conv_b, e_b_gate_a, e_b_gate_x, e_lru_lambda, e_ln_g, e_ln_b, o_sinks])
    m_r = _pack_repl([m_e_w_gate_a, m_e_w_gate_x, m_e_conv_b, m_e_b_gate_a, m_e_b_gate_x, m_e_lru_lambda, m_e_ln_g,
                      m_e_ln_b, m_o_sinks])
    v_r = _pack_repl([v_e_w_gate_a, v_e_w_gate_x, v_e_conv_b, v_e_b_gate_a, v_e_b_gate_x, v_e_lru_lambda, v_e_ln_g,
                      v_e_ln_b, v_o_sinks])
    g_r, d_r, nm_r, nv_r = _adamw(g_repl, w_r, m_r, v_r, name="adamw_replicated")
    loss_at = sum(math.prod(shp) for shp in REPL_SHAPES)
    loss = g_r[loss_at // d, loss_at % d] + g_r[(loss_at + 1) // d, (loss_at + 1) % d]

    def assemble(i, rp):
        vec = res_small[i][0]
        cw, lg_o, lb_o = vec[0:512].reshape(1, 4, 128), vec[512:640].reshape(1, 128), vec[640:768].reshape(1, 128)
        w_a, w_x, cb, b_a, b_x, lam, lg_e, lb_e, snk = _unpack_repl(rp)
        return [res_in_e[i][None], cw, cb, w_a, b_a, w_x, b_x, lam, res_out_e[i][None], lg_e, lb_e,
                res_in_o[i][None], snk, res_out_o[i][None], lg_o, lb_o]

    return (loss, grad_x[None], *assemble(0, g_r), *assemble(1, d_r), *assemble(2, nm_r), *assemble(3, nv_r))
```

```python
import functools
import math

import jax
import jax.numpy as jnp
from jax import lax
from jax.experimental import pallas as pl
from jax.experimental.pallas import tpu as pltpu

F32 = jnp.float32
BF16 = jnp.bfloat16

N_DEV = 8
D_MODEL = 1024
LRU_BLOCKS = 8
LRU_BLOCK = 128
LRU_C = 8.0
SB_HEADS = 8
SB_HEAD_DIM = 128
C_HEADS = 16
C_KV_HEADS = 2
C_GROUP = 8
C_HEAD_DIM = 64
WINDOW = 128
DEPTH = 2
ALPHA = float((2 * DEPTH) ** 0.25)
LN_EPS = 1e-5
ADAM_LR = 0.001
ADAM_B1 = 0.9
ADAM_B2 = 0.999
ADAM_EPS = 1e-08
ADAM_WD = 0.01
ADAM_STEP = 10

VMEM_LIMIT = 56 * 1024 * 1024

NN = ((1,), (0,))
NT = ((1,), (1,))
TN = ((0,), (0,))


def _dot(a, b, dims):
    return lax.dot_general(a, b, (dims, ((), ())), preferred_element_type=F32)


def _sigmoid(x):
    return 1.0 / (1.0 + jnp.exp(-x))


def _cparams(sem, vmem=VMEM_LIMIT):
    return pltpu.CompilerParams(dimension_semantics=sem, vmem_limit_bytes=vmem)


def _matmul(a, b, *, mode, n_out=None, b_off=0, out_dtype=F32, res=None, alpha=1.0,
            tm=1024, tn=1024, tk=1024, comm=None, epi=None, name):
    a_list = list(a) if isinstance(a, (list, tuple)) else [a]
    b_list = list(b) if isinstance(b, (list, tuple)) else [b]
    if mode == "tn":
        k = a_list[0].shape[0]
        m = sum(p.shape[1] for p in a_list)
        n = n_out if n_out is not None else sum(p.shape[1] for p in b_list)
        assert all(p.shape[1] == tm for p in a_list) or len(a_list) == 1
        assert all(p.shape[1] == tn for p in b_list) or len(b_list) == 1
    else:
        assert len(b_list) == 1
        m = a_list[0].shape[0]
        k = sum(p.shape[1] for p in a_list)
        n = n_out if n_out is not None else (b.shape[0] if mode == "nt" else b.shape[1])
        if len(a_list) > 1:
            tk = k
    tm, tn, tk = min(tm, m), min(tn, n), min(tk, k)
    assert m % tm == 0 and n % tn == 0 and k % tk == 0 and b_off % tn == 0
    grid = (m // tm, n // tn, k // tk)
    nk = grid[2]
    jo = b_off // tn
    dims = {"nn": NN, "nt": NT, "tn": TN}[mode]
    nc = len(comm[0]) if comm else 0
    if epi:
        assert res is None
        epi_fn, epi_rows, epi_vecs, epi_dtypes, n_sums = epi
        assert n_sums == 0 or grid[1] == 1
    else:
        epi_rows, epi_vecs, epi_dtypes, n_sums = [], [], [out_dtype], 0
    n_ab = len(a_list) + len(b_list)
    n_in = n_ab + (res is not None) + len(epi_rows) + len(epi_vecs)
    n_res = len(epi_dtypes) + n_sums

    def body(*refs):
        a_refs, b_refs = refs[:len(a_list)], refs[len(a_list):n_ab]
        r_ref = refs[n_ab] if res is not None else None
        row_refs = refs[n_in - len(epi_rows) - len(epi_vecs):n_in - len(epi_vecs)]
        vec_refs = refs[n_in - len(epi_vecs):n_in]
        o_refs = refs[n_in + nc:n_in + nc + len(epi_dtypes)]
        sum_refs = refs[n_in + nc + len(epi_dtypes):n_in + nc + n_res]
        if comm:
            step = [pl.program_id(ax) for ax in range(3)]
            c_src = refs[n_in:n_in + nc]
            c_dst = refs[n_in + nc + n_res:n_in + 2 * nc + n_res]
            sems = refs[len(refs) - 3:]

            @pl.when((step[0] == 0) & (step[1] == 0) & (step[2] == 0))
            def _():
                _exchange_start(c_src, c_dst, comm[1], sems)

        def finish(out):
            if not epi:
                if r_ref is not None:
                    out = out + alpha * r_ref[...]
                o_refs[0][...] = out.astype(out_dtype)
                return
            outs, sums = epi_fn(out, [r[...] for r in row_refs], [v[...] for v in vec_refs])
            for o_ref, o in zip(o_refs, outs):
                o_ref[...] = o.astype(o_ref.dtype)
            if sum_refs:
                first_rows = pl.program_id(0) == 0

                @pl.when(first_rows)
                def _():
                    for s_ref, part_sum in zip(sum_refs, sums):
                        s_ref[...] = part_sum

                @pl.when(jnp.logical_not(first_rows))
                def _():
                    for s_ref, part_sum in zip(sum_refs, sums):
                        s_ref[...] += part_sum

        def accumulate(part):
            if nk == 1:
                finish(part)
                return
            acc_ref = refs[n_in + 2 * nc + n_res]
            kk = pl.program_id(2)

            @pl.when(kk == 0)
            def _():
                acc_ref[...] = part

            @pl.when(kk > 0)
            def _():
                acc_ref[...] += part

            @pl.when(kk == nk - 1)
            def _():
                finish(acc_ref[...])

        def product(a_ref, b_val):
            return _dot(a_ref[...].astype(BF16), b_val.astype(BF16), dims)

        if mode != "tn" and len(a_list) > 1:
            part, off = None, 0
            for a_ref in a_refs:
                w = a_ref.shape[1]
                b_val = b_refs[0][off:off + w, :] if mode == "nn" else b_refs[0][:, off:off + w]
                part = product(a_ref, b_val) if part is None else part + product(a_ref, b_val)
                off += w
            accumulate(part)
        elif len(a_list) > 1 or len(b_list) > 1:
            for pa, a_ref in enumerate(a_refs):
                for pb, b_ref in enumerate(b_refs):
                    picked = ([pl.program_id(0) == pa] if len(a_list) > 1 else []) + (
                        [pl.program_id(1) == pb] if len(b_list) > 1 else [])
                    pl.when(functools.reduce(jnp.logical_and, picked))(
                        functools.partial(lambda a_ref, b_ref: accumulate(product(a_ref, b_ref[...])), a_ref, b_ref))
        else:
            accumulate(product(a_refs[0], b_refs[0][...]))

        if comm:
            @pl.when((step[0] == grid[0] - 1) & (step[1] == grid[1] - 1) & (step[2] == grid[2] - 1))
            def _():
                _exchange_wait(c_src, c_dst, comm[1], sems)

    def held(axis, p):
        def index(i, j, kk):
            return (jnp.where((i, j)[axis] == p, kk, 0), 0)
        return index

    if mode == "tn":
        if len(a_list) > 1:
            a_specs = [pl.BlockSpec((tk, tm), held(0, p)) for p in range(len(a_list))]
        else:
            a_specs = [pl.BlockSpec((tk, tm), lambda i, j, kk: (kk, i))]
    elif len(a_list) > 1:
        a_specs = [pl.BlockSpec((tm, p.shape[1]), lambda i, j, kk: (i, 0)) for p in a_list]
    else:
        a_specs = [pl.BlockSpec((tm, tk), lambda i, j, kk: (i, kk))]
    if mode == "nt":
        b_specs = [pl.BlockSpec((tn, tk), lambda i, j, kk: (j + jo, kk))]
    elif len(b_list) > 1:
        b_specs = [pl.BlockSpec((tk, tn), held(1, p)) for p in range(len(b_list))]
    else:
        b_specs = [pl.BlockSpec((tk, tn), lambda i, j, kk: (kk, j + jo))]
    o_spec = pl.BlockSpec((tm, tn), lambda i, j, kk: (i, j))
    hbm = pl.BlockSpec(memory_space=pl.ANY)
    row_specs = [pl.BlockSpec((tm, tn), functools.partial(lambda i, j, kk, o: (i, j + o), o=off // tn))
                 for (_, off) in epi_rows]
    vec_specs = [pl.BlockSpec(v.shape, functools.partial(lambda i, j, kk, nd: (0,) * nd, nd=v.ndim)) for v in epi_vecs]
    in_specs = a_specs + b_specs + ([o_spec] if res is not None else []) + row_specs + vec_specs + [hbm] * nc
    args = (a_list + b_list + ([res] if res is not None else []) + [r for (r, _) in epi_rows] + list(epi_vecs)
            + (list(comm[0]) if comm else []))
    out_specs = [o_spec] * len(epi_dtypes) + [pl.BlockSpec((1, tn), lambda i, j, kk: (0, j))] * n_sums + [hbm] * nc
    out_shape = ([jax.ShapeDtypeStruct((m, n), dt) for dt in epi_dtypes] + [jax.ShapeDtypeStruct((1, n), F32)] * n_sums
                 + (_exchange_out_shapes(*comm) if comm else []))
    scratch = ([pltpu.VMEM((tm, tn), F32)] if nk > 1 else []) + (_exchange_sems(nc) if comm else [])
    if comm or n_sums or n_ab > 2:
        params = pltpu.CompilerParams(dimension_semantics=("arbitrary",) * 3, vmem_limit_bytes=VMEM_LIMIT,
                                      has_side_effects=bool(comm))
    else:
        params = _cparams(("parallel", "parallel", "arbitrary"))
    outs = pl.pallas_call(
        body, name=name, grid=grid, in_specs=in_specs, out_specs=out_specs, out_shape=out_shape,
        scratch_shapes=scratch, compiler_params=params,
    )(*args)
    return tuple(outs) if (comm or epi) else outs[0]


def _ln_stats(z):
    mu = jnp.mean(z, axis=-1, keepdims=True)
    zc = z - mu
    var = jnp.mean(zc * zc, axis=-1, keepdims=True)
    rstd = lax.rsqrt(var + LN_EPS)
    return zc * rstd, rstd


def _ln_bwd(dy, xhat, rstd, g):
    dxh = dy * g
    m1 = jnp.mean(dxh, axis=-1, keepdims=True)
    m2 = jnp.mean(dxh * xhat, axis=-1, keepdims=True)
    return rstd * (dxh - m1 - xhat * m2)


def _colsum(x):
    return jnp.sum(x, axis=0, keepdims=True)


def _epi_ln_fwd(acc, rows, vecs):
    z = ALPHA * rows[0] + acc
    xhat, _ = _ln_stats(z)
    x1 = xhat * vecs[0] + vecs[1]
    return [z, x1, x1], []


def _epi_ln_loss_bwd(acc, rows, vecs):
    inv_c = 1.0 / acc.shape[-1]
    z = ALPHA * rows[0] + acc
    xhat, rstd = _ln_stats(z)
    d = xhat * vecs[0] + vecs[1] - rows[1]
    dy = d * inv_c
    dz = _ln_bwd(dy, xhat, rstd, vecs[0])
    return [dz, dz], [_colsum(0.5 * d * d * inv_c), _colsum(dy * xhat), _colsum(dy)]


def _epi_ln_bwd(acc, rows, vecs):
    dy = acc + ALPHA * rows[0]
    xhat, rstd = _ln_stats(rows[1])
    dz = _ln_bwd(dy, xhat, rstd, vecs[0])
    return [dz, dz], [_colsum(dy * xhat), _colsum(dy)]


def _epi_gate_bwd(acc, rows, vecs):
    oo, gg = rows
    sg = _sigmoid(gg)
    return [acc * (gg * sg), acc * oo * (sg * (1.0 + gg * (1.0 - sg)))], []


LRU_T = 256
HALO = 8


def _log1p(y):
    u = 1.0 + y
    return jnp.where(u == 1.0, y, jnp.log(u) * (y / (u - 1.0)))


def _lru_gates(c, wa_ref, wx_ref, pv):
    c16 = c.astype(BF16)
    pre_r = jnp.concatenate(
        [_dot(c16[:, n * LRU_BLOCK:(n + 1) * LRU_BLOCK], wa_ref[n], NN) for n in range(LRU_BLOCKS)], axis=1)
    pre_i = jnp.concatenate(
        [_dot(c16[:, n * LRU_BLOCK:(n + 1) * LRU_BLOCK], wx_ref[n], NN) for n in range(LRU_BLOCKS)], axis=1)
    r = _sigmoid(pre_r + pv[5:6])
    ig = _sigmoid(pre_i + pv[6:7])
    lam = pv[7:8]
    ls = jnp.minimum(lam, 0.0) - _log1p(jnp.exp(-jnp.abs(lam)))
    la = LRU_C * r * ls
    a = jnp.exp(la)
    a2 = a * a
    m = jnp.sqrt(jnp.tanh(-la) * (a2 + 1.0))
    return c16, r, ig, ls, la, a, a2, m


def _conv(ext_ref, x, pv, t):
    return (pv[4:5] + pv[3:4] * x + pv[2:3] * ext_ref[pl.ds(HALO - 1, t), :]
            + pv[1:2] * ext_ref[pl.ds(HALO - 2, t), :] + pv[0:1] * ext_ref[pl.ds(HALO - 3, t), :])


def _rglru_fwd(h0a, pvec, wa16, wx16):
    s = h0a.shape[0]
    w = D_MODEL
    t = min(LRU_T, s)
    assert s % t == 0

    def body(ax_ref, ag_ref, pv_ref, wa_ref, wx_ref, ya_ref, h_ref, ext_ref, a_ref, u_ref, hc_ref):
        i = pl.program_id(0)

        @pl.when(i == 0)
        def _():
            ext_ref[pl.ds(0, HALO), :] = jnp.zeros((HALO, w), F32)
            hc_ref[...] = jnp.zeros((1, w), F32)

        pv = pv_ref[...]
        ax = ax_ref[...]
        ext_ref[pl.ds(HALO, t), :] = ax
        c = _conv(ext_ref, ax, pv, t)
        ext_ref[pl.ds(0, HALO), :] = ax[t - HALO:, :]
        _, _, ig, _, _, a, _, m = _lru_gates(c, wa_ref, wx_ref, pv)
        a_ref[...] = a
        u_ref[...] = m * (ig * c)

        def steps(k, h):
            base = pl.multiple_of(k * HALO, HALO)
            for r in range(HALO):
                h = a_ref[pl.ds(base + r, 1), :] * h + u_ref[pl.ds(base + r, 1), :]
                h_ref[pl.ds(base + r, 1), :] = h
            return h

        hc_ref[...] = lax.fori_loop(0, t // HALO, steps, hc_ref[...])
        ag = ag_ref[...]
        ya_ref[...] = (h_ref[...] * (ag * _sigmoid(ag))).astype(BF16)

    full = lambda shp: pl.BlockSpec(shp, lambda i: (0,) * len(shp))
    return pl.pallas_call(
        body, name="rglru_fwd", grid=(s // t,),
        in_specs=[pl.BlockSpec((t, w), lambda i: (i, 0)), pl.BlockSpec((t, w), lambda i: (i, 1)),
                  full((8, w)), full((LRU_BLOCKS, LRU_BLOCK, LRU_BLOCK)), full((LRU_BLOCKS, LRU_BLOCK, LRU_BLOCK))],
        out_specs=[pl.BlockSpec((t, w), lambda i: (i, 0)), pl.BlockSpec((t, w), lambda i: (i, 0))],
        out_shape=[jax.ShapeDtypeStruct((s, w), BF16), jax.ShapeDtypeStruct((s, w), F32)],
        scratch_shapes=[pltpu.VMEM((t + HALO, w), F32), pltpu.VMEM((t, w), F32), pltpu.VMEM((t, w), F32),
                        pltpu.VMEM((1, w), F32)],
        compiler_params=_cparams(("arbitrary",)),
    )(h0a, h0a, pvec, wa16, wx16)


def _rglru_bwd(h0a, h, dh, pvec, wa16, wx16):
    s = h0a.shape[0]
    w = D_MODEL
    t = min(LRU_T, s)
    nb = s // t
    hb = t // HALO

    def body(ax_ref, axh_ref, h_ref, hh_ref, dh_ref, pv_ref, wa_ref, wx_ref,
             dax_ref, dwa_ref, dwx_ref, dpv_ref, ext_ref, hext_ref, dcext_ref, a_ref, g_ref, gc_ref):
        i = pl.program_id(0)
        blk = nb - 1 - i

        @pl.when(i == 0)
        def _():
            dwa_ref[...] = jnp.zeros_like(dwa_ref)
            dwx_ref[...] = jnp.zeros_like(dwx_ref)
            dpv_ref[...] = jnp.zeros_like(dpv_ref)
            gc_ref[...] = jnp.zeros((1, w), F32)
            dcext_ref[pl.ds(t, HALO), :] = jnp.zeros((HALO, w), F32)

        pv = pv_ref[...]
        ax = ax_ref[...]
        keep = (blk > 0).astype(F32)
        ext_ref[pl.ds(0, HALO), :] = axh_ref[...] * keep
        ext_ref[pl.ds(HALO, t), :] = ax
        hext_ref[pl.ds(0, HALO), :] = hh_ref[...] * keep
        hext_ref[pl.ds(HALO, t), :] = h_ref[...]
        c = _conv(ext_ref, ax, pv, t)
        c16, r, ig, ls, _, a, a2, m = _lru_gates(c, wa_ref, wx_ref, pv)
        a_ref[...] = a

        def steps(k, carry):
            base = pl.multiple_of(t - HALO - k * HALO, HALO)
            for r in reversed(range(HALO)):
                g = dh_ref[pl.ds(base + r, 1), :] + carry
                g_ref[pl.ds(base + r, 1), :] = g
                carry = a_ref[pl.ds(base + r, 1), :] * g
            return carry

        gc_ref[...] = lax.fori_loop(0, t // HALO, steps, gc_ref[...])
        g = g_ref[...]
        hprev = hext_ref[pl.ds(HALO - 1, t), :]
        gm = g * m
        d_la = g * hprev * a - (g * ig * c) * a2 / m
        d_pr = d_la * (LRU_C * ls) * r * (1.0 - r)
        d_pi = gm * c * ig * (1.0 - ig)
        dc = gm * ig
        dpr16 = d_pr.astype(BF16)
        dpi16 = d_pi.astype(BF16)
        dc_parts = []
        for n in range(LRU_BLOCKS):
            sl = slice(n * LRU_BLOCK, (n + 1) * LRU_BLOCK)
            dwa_ref[n] += _dot(c16[:, sl], dpr16[:, sl], TN)
            dwx_ref[n] += _dot(c16[:, sl], dpi16[:, sl], TN)
            dc_parts.append(_dot(dpr16[:, sl], wa_ref[n], NT) + _dot(dpi16[:, sl], wx_ref[n], NT))
        dc = dc + jnp.concatenate(dc_parts, axis=1)
        dcext_ref[pl.ds(0, t), :] = dc
        dax = (pv[3:4] * dc + pv[2:3] * dcext_ref[pl.ds(1, t), :] + pv[1:2] * dcext_ref[pl.ds(2, t), :]
               + pv[0:1] * dcext_ref[pl.ds(3, t), :])
        dax_ref[...] = dax.astype(BF16)
        dcext_ref[pl.ds(t, HALO), :] = dc[:HALO, :]
        sums = [_colsum(dc * ext_ref[pl.ds(HALO - 3 + kk, t), :]) for kk in range(4)]
        sums += [_colsum(dc), _colsum(d_pr), _colsum(d_pi), _colsum(d_la * (LRU_C * r))]
        for kk, part in enumerate(sums):
            dpv_ref[pl.ds(kk, 1), :] += part

        @pl.when(i == nb - 1)
        def _():
            lam = pv[7:8]
            dls = dpv_ref[pl.ds(7, 1), :]
            dpv_ref[pl.ds(7, 1), :] = dls * _sigmoid(-lam)

    full = lambda shp: pl.BlockSpec(shp, lambda i: (0,) * len(shp))
    rev = lambda cb: pl.BlockSpec((t, w), functools.partial(lambda i, cb: (nb - 1 - i, cb), cb=cb))
    halo = lambda cb: pl.BlockSpec(
        (HALO, w), functools.partial(lambda i, cb: (jnp.maximum((nb - 1 - i) * hb - 1, 0), cb), cb=cb))
    gw = (LRU_BLOCKS, LRU_BLOCK, LRU_BLOCK)
    return pl.pallas_call(
        body, name="rglru_bwd", grid=(nb,),
        in_specs=[rev(0), halo(0), rev(0), halo(0), rev(0), full((8, w)), full(gw), full(gw)],
        out_specs=[rev(0), full(gw), full(gw), full((8, w))],
        out_shape=[jax.ShapeDtypeStruct((s, w), BF16), jax.ShapeDtypeStruct(gw, F32),
                   jax.ShapeDtypeStruct(gw, F32), jax.ShapeDtypeStruct((8, w), F32)],
        scratch_shapes=[pltpu.VMEM((t + HALO, w), F32), pltpu.VMEM((t + HALO, w), F32),
                        pltpu.VMEM((t + HALO, w), F32), pltpu.VMEM((t, w), F32), pltpu.VMEM((t, w), F32),
                        pltpu.VMEM((1, w), F32)],
        compiler_params=_cparams(("arbitrary",)),
    )(h0a, h0a, h, h, dh, pvec, wa16, wx16)


SB_T = 256
SB_SUB = 4


def _split16(x):
    hi = x.astype(BF16)
    lo = (x - hi.astype(F32)).astype(BF16)
    return jnp.concatenate([hi, lo], axis=0)


def _sb_tile(q, k, scale, tri, run, causal):
    tq = q.shape[0]
    z = _dot(q, k, NT) * scale
    log1mb = -(jnp.maximum(z, 0.0) + jnp.log(1.0 + jnp.exp(-jnp.abs(z))))
    if causal is not None:
        log1mb = jnp.where(causal, log1mb, 0.0)
    cs = _dot(_split16(log1mb), tri, NN)
    cs = cs[:tq] + cs[tq:]
    wgt = jnp.exp(z + cs + run)
    if causal is not None:
        wgt = jnp.where(causal, wgt, 0.0)
    return z, wgt, run + cs[:, 0:1]


SB_DEAD = -105.0


def _sb_alive(run):
    return (jnp.max(run) > SB_DEAD).astype(jnp.int32)


def _sb_more(carry):
    return (carry[0] >= 0) & (carry[1] > 0)


def _sb_fwd(qkv, bg):
    s = qkv.shape[0]
    t = min(SB_T, s)
    nq = s // t
    dh = SB_HEAD_DIM
    scale = 1.0 / math.sqrt(dh)

    sub = SB_SUB if nq % SB_SUB == 0 else 1

    def body(q_ref, k_ref, v_ref, bg_ref, o_ref, y_ref):
        row = lax.broadcasted_iota(jnp.int32, (t, t), 0)
        col = lax.broadcasted_iota(jnp.int32, (t, t), 1)
        tri = jnp.where(row >= col, 1.0, 0.0).astype(BF16)
        causal = col < row
        for u in range(sub):
            rows = pl.ds(u * t, t)
            query_block(pl.program_id(1) * sub + u, q_ref[rows, :], tri, causal, k_ref, v_ref, bg_ref.at[rows, :],
                        o_ref.at[rows, :], y_ref.at[rows, :])

    def query_block(iq, q, tri, causal, k_ref, v_ref, bg_ref, o_ref, y_ref):
        def tile(kb, run, mask):
            off = pl.multiple_of(kb * t, t)
            _, wgt, run = _sb_tile(q, k_ref[pl.ds(off, t), :], scale, tri, run, mask)
            return run, _dot(wgt.astype(BF16), v_ref[pl.ds(off, t), :], NN)

        has_left = iq > 0
        run_d, acc_d = tile(iq, jnp.zeros((t, 1), F32), causal)
        run_l, acc_l = tile(jnp.maximum(iq - 1, 0), run_d, None)
        run = jnp.where(has_left, run_l, run_d)
        acc = acc_d + jnp.where(has_left, acc_l, 0.0)

        def loop(carry):
            kb, _, run, acc = carry
            run, part = tile(kb, run, None)
            return kb - 1, _sb_alive(run), run, acc + part

        _, _, run, acc = lax.while_loop(_sb_more, loop, (iq - 2, _sb_alive(run), run, acc))
        o_ref[...] = acc
        g = bg_ref[...]
        y_ref[...] = (acc * (g * _sigmoid(g))).astype(BF16)

    blk = lambda off: pl.BlockSpec((sub * t, dh), functools.partial(lambda h, i, off: (i, h + off), off=off))
    col = lambda off: pl.BlockSpec((s, dh), functools.partial(lambda h, i, off: (0, h + off), off=off))
    return pl.pallas_call(
        body, name="sb_fwd", grid=(SB_HEADS, nq // sub),
        in_specs=[blk(0), col(SB_HEADS), col(2 * SB_HEADS), blk(0)],
        out_specs=[blk(0), blk(0)],
        out_shape=[jax.ShapeDtypeStruct((s, SB_HEADS * dh), F32), jax.ShapeDtypeStruct((s, SB_HEADS * dh), BF16)],
        compiler_params=_cparams(("parallel", "arbitrary")),
    )(qkv, qkv, qkv, bg)


def _sb_bwd(qkv, dob):
    s = qkv.shape[0]
    t = min(SB_T, s)
    nq = s // t
    dh = SB_HEAD_DIM
    scale = 1.0 / math.sqrt(dh)

    sub = SB_SUB if nq % SB_SUB == 0 else 1

    def body(q_ref, k_ref, v_ref, do_ref, dq_ref, dk_out, dv_out, e_ref, b_ref, dk_ref, dv_ref):
        @pl.when(pl.program_id(1) == 0)
        def _():
            dk_ref[...] = jnp.zeros_like(dk_ref)
            dv_ref[...] = jnp.zeros_like(dv_ref)

        row = lax.broadcasted_iota(jnp.int32, (t, t), 0)
        col = lax.broadcasted_iota(jnp.int32, (t, t), 1)
        tri = jnp.where(row >= col, 1.0, 0.0).astype(BF16)
        tri_x = jnp.where(row < col, 1.0, 0.0).astype(BF16)
        for u in range(sub):
            rows = pl.ds(u * t, t)
            query_block(pl.program_id(1) * sub + u, q_ref[rows, :], do_ref[rows, :], tri, tri_x, col < row,
                        k_ref, v_ref, dq_ref.at[rows, :], dk_ref, dv_ref, e_ref, b_ref)

        @pl.when(pl.program_id(1) == nq // sub - 1)
        def _():
            dk_out[...] = dk_ref[...].astype(BF16)
            dv_out[...] = dv_ref[...].astype(BF16)

    def query_block(iq, q, do, tri, tri_x, causal, k_ref, v_ref, dq_ref, dk_ref, dv_ref, e_ref, b_ref):
        def sweep1(kb, slot, run, mask, live=None):
            off = pl.multiple_of(kb * t, t)
            v = v_ref[pl.ds(off, t), :]
            z, wgt, run = _sb_tile(q, k_ref[pl.ds(off, t), :], scale, tri, run, mask)
            if live is not None:
                wgt = jnp.where(live, wgt, 0.0)
            e_ref[slot] = wgt * _dot(do, v, NT)
            b_ref[slot] = _sigmoid(z)
            dv_ref[pl.ds(off, t), :] += _dot(wgt.astype(BF16), do, TN)
            return run

        has_left = iq > 0
        left = jnp.maximum(iq - 1, 0)
        left_slot = jnp.where(has_left, iq - 1, nq)
        run_d = sweep1(iq, iq, jnp.zeros((t, 1), F32), causal)
        run = jnp.where(has_left, sweep1(left, left_slot, run_d, None, has_left), run_d)

        def loop1(carry):
            run = sweep1(carry[0], carry[0], carry[2], None)
            return carry[0] - 1, _sb_alive(run), run

        first = lax.while_loop(_sb_more, loop1, (iq - 2, _sb_alive(run), run))[0] + 1

        def sweep2(kb, slot, pre, mask, live=None):
            off = pl.multiple_of(kb * t, t)
            e = e_ref[slot]
            beta = b_ref[slot]
            ps = _dot(_split16(e), tri_x, NN)
            ps = ps[:t] + ps[t:]
            dz = e * (1.0 - beta) - beta * (ps + pre)
            if mask is not None:
                dz = jnp.where(mask, dz, 0.0)
            if live is not None:
                dz = jnp.where(live, dz, 0.0)
            dz16 = (dz * scale).astype(BF16)
            dk_ref[pl.ds(off, t), :] += _dot(dz16, q, TN)
            return pre + ps[:, t - 1:t] + e[:, t - 1:t], _dot(dz16, k_ref[pl.ds(off, t), :], NN)

        def loop2(kb, carry):
            pre, part = sweep2(kb, kb, carry[0], None)
            return pre, carry[1] + part

        pre, dq = lax.fori_loop(first, iq - 1, loop2, (jnp.zeros((t, 1), F32), jnp.zeros((t, dh), F32)))
        pre, dq_l = sweep2(left, left_slot, pre, None, has_left)
        _, dq_d = sweep2(iq, iq, pre, causal)
        dq_ref[...] = (dq + dq_l + dq_d).astype(BF16)

    blk = lambda off: pl.BlockSpec((sub * t, dh), functools.partial(lambda h, i, off: (i, h + off), off=off))
    col = lambda off: pl.BlockSpec((s, dh), functools.partial(lambda h, i, off: (0, h + off), off=off))
    wide = SB_HEADS * dh
    return pl.pallas_call(
        body, name="sb_bwd", grid=(SB_HEADS, nq // sub),
        in_specs=[blk(0), col(SB_HEADS), col(2 * SB_HEADS), blk(0)],
        out_specs=[blk(0), col(0), col(0)],
        out_shape=[jax.ShapeDtypeStruct((s, wide), BF16)] * 3,
        scratch_shapes=[pltpu.VMEM((nq + 1, t, t), F32), pltpu.VMEM((nq + 1, t, t), F32),
                        pltpu.VMEM((s, dh), F32), pltpu.VMEM((s, dh), F32)],
        compiler_params=_cparams(("parallel", "arbitrary")),
    )(qkv, qkv, qkv, dob)


def _alibi_slope(h):
    return float(2.0 ** (-8.0 * (h + 1) / C_HEADS))


GROUP_ROWS = C_GROUP * WINDOW


def _swa_window(n):
    qb = WINDOW
    i = lax.broadcasted_iota(jnp.int32, (qb, 2 * qb), 0)
    j = lax.broadcasted_iota(jnp.int32, (qb, 2 * qb), 1)
    d = i - j + qb
    valid = (d >= 0) & (d < WINDOW) & ((j >= qb) | (n > 0))
    stacked = lambda x: jnp.concatenate([x] * C_GROUP, axis=0)
    return stacked(d.astype(F32)), stacked(jnp.where(valid, 0.0, -1e30))


def _swa_group_cols(c, sink_ref):
    head = lax.shift_right_logical(lax.broadcasted_iota(jnp.int32, (GROUP_ROWS, 1), 0), WINDOW.bit_length() - 1)
    slope = jnp.zeros((GROUP_ROWS, 1), F32)
    sink = jnp.zeros((GROUP_ROWS, 1), F32)
    for hh in range(C_GROUP):
        slope = jnp.where(head == hh, _alibi_slope(c * C_GROUP + hh), slope)
        sink = jnp.where(head == hh, sink_ref[c * C_GROUP + hh], sink)
    return slope, sink


def _swa_probs(qg, kw, sink, slope, dist, bias, scale):
    sc = _dot(qg, kw, NT) * scale - slope * dist + bias
    m = jnp.maximum(jnp.max(sc, axis=-1, keepdims=True), sink)
    p = jnp.exp(sc - m)
    ps = jnp.exp(sink - m)
    inv = 1.0 / (jnp.sum(p, axis=-1, keepdims=True) + ps)
    return p * inv, ps * inv


def _stack_heads(x, c):
    hd = C_HEAD_DIM
    return jnp.concatenate([x[:, (c * C_GROUP + hh) * hd:(c * C_GROUP + hh + 1) * hd] for hh in range(C_GROUP)], axis=0)


def _swa_fwd(q16, gate, kvp, sinks):
    s = q16.shape[0]
    qb = WINDOW
    hd = C_HEAD_DIM
    scale = 1.0 / math.sqrt(hd)
    kvw = C_KV_HEADS * hd

    def body(sink_ref, q_ref, g_ref, kp_ref, ko_ref, o_ref, y_ref):
        n = pl.program_id(0)
        q = q_ref[...]
        kv = jnp.concatenate([kp_ref[...], ko_ref[...]], axis=0)
        dist, bias = _swa_window(n)
        for c in range(C_KV_HEADS):
            slope, sink = _swa_group_cols(c, sink_ref)
            kw = kv[:, c * hd:(c + 1) * hd]
            vw = kv[:, kvw + c * hd:kvw + (c + 1) * hd]
            p, _ = _swa_probs(_stack_heads(q, c), kw, sink, slope, dist, bias, scale)
            og = _dot(p.astype(BF16), vw, NN)
            for hh in range(C_GROUP):
                h = c * C_GROUP + hh
                o_ref[:, h * hd:(h + 1) * hd] = og[hh * qb:(hh + 1) * qb, :]
        g = g_ref[...]
        y_ref[...] = (o_ref[...] * (g * _sigmoid(g))).astype(BF16)

    wide = C_HEADS * hd
    return pl.pallas_call(
        body, name="swa_fwd", grid=(s // qb,),
        in_specs=[pl.BlockSpec(memory_space=pltpu.SMEM),
                  pl.BlockSpec((qb, wide), lambda n: (n, 0)), pl.BlockSpec((qb, wide), lambda n: (n, 0)),
                  pl.BlockSpec((qb, 2 * kvw), lambda n: (n, 0)), pl.BlockSpec((qb, 2 * kvw), lambda n: (n + 1, 0))],
        out_specs=[pl.BlockSpec((qb, wide), lambda n: (n, 0)), pl.BlockSpec((qb, wide), lambda n: (n, 0))],
        out_shape=[jax.ShapeDtypeStruct((s, wide), F32), jax.ShapeDtypeStruct((s, wide), BF16)],
        compiler_params=_cparams(("arbitrary",)),
    )(sinks, q16, gate, kvp, kvp)


def _swa_bwd(q16, do16, kvp, sinks):
    s = q16.shape[0]
    qb = WINDOW
    hd = C_HEAD_DIM
    scale = 1.0 / math.sqrt(hd)
    kvw = C_KV_HEADS * hd
    nblk = s // qb

    def body(sink_ref, q_ref, do_ref, kp_ref, ko_ref, dq_ref, dkv_ref, ds_ref, sacc_ref):
        n = pl.program_id(0)

        @pl.when(n == 0)
        def _():
            dkv_ref[...] = jnp.zeros_like(dkv_ref)
            sacc_ref[...] = jnp.zeros_like(sacc_ref)

        q = q_ref[...]
        do = do_ref[...]
        kv = jnp.concatenate([kp_ref[...], ko_ref[...]], axis=0)
        off = pl.multiple_of(n * qb, qb)
        dist, bias = _swa_window(n)
        for c in range(C_KV_HEADS):
            slope, sink = _swa_group_cols(c, sink_ref)
            kw = kv[:, c * hd:(c + 1) * hd]
            vw = kv[:, kvw + c * hd:kvw + (c + 1) * hd]
            qg = _stack_heads(q, c)
            dog = _stack_heads(do, c)
            p, ps = _swa_probs(qg, kw, sink, slope, dist, bias, scale)
            dp = _dot(dog, vw, NT)
            dd = jnp.sum(p * dp, axis=-1, keepdims=True)
            ds16 = (p * (dp - dd) * scale).astype(BF16)
            sacc_ref[c] += ps * dd
            dqg = _dot(ds16, kw, NN).astype(BF16)
            for hh in range(C_GROUP):
                h = c * C_GROUP + hh
                dq_ref[:, h * hd:(h + 1) * hd] = dqg[hh * qb:(hh + 1) * qb, :]
            dkv_ref[pl.ds(off, 2 * qb), c * hd:(c + 1) * hd] += _dot(ds16, qg, TN)
            dkv_ref[pl.ds(off, 2 * qb), kvw + c * hd:kvw + (c + 1) * hd] += _dot(p.astype(BF16), dog, TN)

        @pl.when(n == nblk - 1)
        def _():
            lane = lax.broadcasted_iota(jnp.int32, (1, 128), 1)
            row = jnp.zeros((1, 128), F32)
            for c in range(C_KV_HEADS):
                for hh in range(C_GROUP):
                    tot = jnp.sum(sacc_ref[c, pl.ds(hh * qb, qb), :], axis=0, keepdims=True)
                    row = jnp.where(lane == c * C_GROUP + hh, -tot, row)
            ds_ref[...] = row

    wide = C_HEADS * hd
    return pl.pallas_call(
        body, name="swa_bwd", grid=(nblk,),
        in_specs=[pl.BlockSpec(memory_space=pltpu.SMEM),
                  pl.BlockSpec((qb, wide), lambda n: (n, 0)), pl.BlockSpec((qb, wide), lambda n: (n, 0)),
                  pl.BlockSpec((qb, 2 * kvw), lambda n: (n, 0)), pl.BlockSpec((qb, 2 * kvw), lambda n: (n + 1, 0))],
        out_specs=[pl.BlockSpec((qb, wide), lambda n: (n, 0)),
                   pl.BlockSpec((s + qb, 2 * kvw), lambda n: (0, 0)), pl.BlockSpec((1, 128), lambda n: (0, 0))],
        out_shape=[jax.ShapeDtypeStruct((s, wide), BF16), jax.ShapeDtypeStruct((s + qb, 2 * kvw), F32),
                   jax.ShapeDtypeStruct((1, 128), F32)],
        scratch_shapes=[pltpu.VMEM((C_KV_HEADS, GROUP_ROWS, 1), F32)],
        compiler_params=_cparams(("arbitrary",)),
    )(sinks, q16, do16, kvp, kvp)


def _adamw(parts, w, m, v, *, name, tr=496):
    npart, r, c = parts.shape
    tr = min(tr, r)
    assert r % tr == 0 and tr % 16 == 0
    c1 = 1.0 / (1.0 - ADAM_B1 ** ADAM_STEP)
    c2 = 1.0 / (1.0 - ADAM_B2 ** ADAM_STEP)

    def body(p_ref, w_ref, m_ref, v_ref, g_ref, d_ref, nm_ref, nv_ref):
        g = p_ref[0].astype(F32)
        for j in range(1, npart):
            g = g + p_ref[j].astype(F32)
        nm = ADAM_B1 * m_ref[...] + (1.0 - ADAM_B1) * g
        nv = ADAM_B2 * v_ref[...] + (1.0 - ADAM_B2) * (g * g)
        g_ref[...] = g
        nm_ref[...] = nm
        nv_ref[...] = nv
        d_ref[...] = -ADAM_LR * ((nm * c1) / (jnp.sqrt(nv * c2) + ADAM_EPS) + ADAM_WD * w_ref[...])

    spec = pl.BlockSpec((tr, c), lambda i: (i, 0))
    return pl.pallas_call(
        body, name=name, grid=(r // tr,),
        in_specs=[pl.BlockSpec((npart, tr, c), lambda i: (0, i, 0)), spec, spec, spec],
        out_specs=[spec] * 4, out_shape=[jax.ShapeDtypeStruct((r, c), F32)] * 4,
        compiler_params=_cparams(("parallel",)),
    )(parts, w, m, v)


GATHER = "gather"
A2A = "a2a"
GATHER_COLS = "gather_cols"
A2A_COLS = "a2a_cols"


def _exchange(bufs, gather, *, name):
    nb = len(bufs)

    def body(*refs):
        _exchange_start(refs[:nb], refs[nb:2 * nb], gather, refs[2 * nb:])
        _exchange_wait(refs[:nb], refs[nb:2 * nb], gather, refs[2 * nb:])

    hbm = pl.BlockSpec(memory_space=pl.ANY)
    return pl.pallas_call(
        body, name=name, in_specs=[hbm] * nb, out_specs=[hbm] * nb, out_shape=_exchange_out_shapes(bufs, gather),
        scratch_shapes=_exchange_sems(nb),
        compiler_params=pltpu.CompilerParams(has_side_effects=True),
    )(*bufs)


def _gather_two_level(bufs, modes, *, name):
    nb = len(bufs)

    def body(*refs):
        src, dst = refs[:nb], refs[nb:2 * nb]
        send_sems, recv_sems, local_sems = refs[2 * nb:]
        x, y, c = lax.axis_index("x"), lax.axis_index("y"), lax.axis_index("c")
        me = 4 * x + 2 * y + c
        here, sibling = (x, y, c), (x, y, 1 - c)

        def copy(b, k, origin, to, source):
            return pltpu.make_async_remote_copy(
                src_ref=source, dst_ref=_slot(dst[b], modes[b], origin), send_sem=send_sems.at[b, k - 1],
                recv_sem=recv_sems.at[b, k - 1], device_id=to, device_id_type=pl.DeviceIdType.MESH)

        local = [pltpu.make_async_copy(src[b], _slot(dst[b], modes[b], me), local_sems.at[b]) for b in range(nb)]
        sends = [copy(b, k, me, (x ^ (k >> 2), y ^ ((k >> 1) & 1), c ^ (k & 1)), src[b])
                 for k in (1, 2, 4, 6) for b in range(nb)]
        for cp in local + sends:
            cp.start()
        for j in (2, 4, 6):
            for b in range(nb):
                copy(b, j, me ^ j, here, src[b]).wait_recv()
                passed = copy(b, j ^ 1, me ^ j, sibling, _slot(dst[b], modes[b], me ^ j))
                passed.start()
                sends.append(passed)
        for k in (1, 3, 5, 7):
            for b in range(nb):
                copy(b, k, me ^ k, here, src[b]).wait_recv()
        for cp in sends:
            cp.wait_send()
        for cp in local:
            cp.wait()

    hbm = pl.BlockSpec(memory_space=pl.ANY)
    return pl.pallas_call(
        body, name=name, in_specs=[hbm] * nb, out_specs=[hbm] * nb, out_shape=_exchange_out_shapes(bufs, modes),
        scratch_shapes=_exchange_sems(nb),
        compiler_params=pltpu.CompilerParams(has_side_effects=True),
    )(*bufs)


def _exchange_out_shapes(bufs, gather):
    shapes = {GATHER: lambda s: (N_DEV,) + s, A2A: lambda s: s, GATHER_COLS: lambda s: (s[0], N_DEV * s[1]),
              A2A_COLS: lambda s: (N_DEV, s[0], s[1] // N_DEV)}
    return [jax.ShapeDtypeStruct(shapes[g](tuple(b.shape)), b.dtype) for b, g in zip(bufs, gather)]


def _sent(src, mode, peer):
    if mode == A2A:
        return src.at[peer]
    if mode == A2A_COLS:
        w = src.shape[1] // N_DEV
        return src.at[:, pl.ds(pl.multiple_of(peer * w, 128), w)]
    return src


def _slot(dst, mode, dev):
    if mode == GATHER_COLS:
        w = dst.shape[1] // N_DEV
        return dst.at[:, pl.ds(pl.multiple_of(dev * w, 128), w)]
    return dst.at[dev]


def _exchange_sems(nb):
    return [pltpu.SemaphoreType.DMA((nb, N_DEV - 1)), pltpu.SemaphoreType.DMA((nb, N_DEV - 1)),
            pltpu.SemaphoreType.DMA((nb,))]


def _exchange_copies(src, dst, gather, sems):
    send_sems, recv_sems, local_sems = sems
    x, y, c = lax.axis_index("x"), lax.axis_index("y"), lax.axis_index("c")
    me = 4 * x + 2 * y + c
    local, sends, recvs = [], [], []
    for b in range(len(src)):
        mine = _sent(src[b], gather[b], me)
        local.append(pltpu.make_async_copy(mine, _slot(dst[b], gather[b], me), local_sems.at[b]))
        for k in range(1, N_DEV):
            px, py, pc = x ^ (k >> 2), y ^ ((k >> 1) & 1), c ^ (k & 1)
            peer = 4 * px + 2 * py + pc
            pair = dict(send_sem=send_sems.at[b, k - 1], recv_sem=recv_sems.at[b, k - 1],
                        device_id_type=pl.DeviceIdType.MESH)
            sends.append(pltpu.make_async_remote_copy(
                src_ref=_sent(src[b], gather[b], peer), dst_ref=_slot(dst[b], gather[b], me), device_id=(px, py, pc),
                **pair))
            recvs.append(pltpu.make_async_remote_copy(
                src_ref=mine, dst_ref=_slot(dst[b], gather[b], peer), device_id=(x, y, c), **pair))
    return local, sends, recvs


def _exchange_start(src, dst, gather, sems):
    local, sends, _ = _exchange_copies(src, dst, gather, sems)
    for cp in local + sends:
        cp.start()


def _exchange_wait(src, dst, gather, sems):
    local, sends, recvs = _exchange_copies(src, dst, gather, sems)
    for cp in sends:
        cp.wait_send()
    for cp in recvs:
        cp.wait_recv()
    for cp in local:
        cp.wait()


R_SMALL, R_REPL = 16, 48
REPL_ROWS = N_DEV * R_REPL
REPL_LEN = REPL_ROWS * D_MODEL


def _small_block(parts):
    flat = jnp.concatenate(parts, axis=-1)
    lead = flat.ndim - 1
    return jnp.pad(flat[..., None, :], [(0, 0)] * lead + [(0, R_SMALL - 1), (0, D_MODEL - flat.shape[-1])])


def _small_vectors(conv_w, ln_g, ln_b):
    block = _small_block([conv_w.reshape(512), ln_g.reshape(128), ln_b.reshape(128)])
    return jnp.concatenate([block, jnp.zeros((R_REPL, D_MODEL), F32)], axis=0)


REPL_SHAPES = ((1, 8, 128, 128), (1, 8, 128, 128), (1, 1024), (1, 1024), (1, 1024), (1, 1024), (1, 1024), (1, 1024),
               (1, 16))


def _pack_repl(parts):
    flat = jnp.concatenate([p.reshape(-1) for p in parts])
    return jnp.concatenate([flat, jnp.zeros((REPL_LEN - flat.shape[0],), F32)]).reshape(REPL_ROWS, D_MODEL)


def _unpack_repl(p):
    flat = p.reshape(-1)
    out, o = [], 0
    for shp in REPL_SHAPES:
        n = math.prod(shp)
        out.append(flat[o:o + n].reshape(shp))
        o += n
    return out


def kernel(x, e_w_in, e_conv_w, e_conv_b, e_w_gate_a, e_b_gate_a, e_w_gate_x, e_b_gate_x, e_lru_lambda, e_w_out, e_ln_g, e_ln_b, o_w_in, o_sinks, o_w_out, o_ln_g, o_ln_b, loss_target, m_e_w_in, m_e_conv_w, m_e_conv_b, m_e_w_gate_a, m_e_b_gate_a, m_e_w_gate_x, m_e_b_gate_x, m_e_lru_lambda, m_e_w_out, m_e_ln_g, m_e_ln_b, m_o_w_in, m_o_sinks, m_o_w_out, m_o_ln_g, m_o_ln_b, v_e_w_in, v_e_conv_w, v_e_conv_b, v_e_w_gate_a, v_e_b_gate_a, v_e_w_gate_x, v_e_b_gate_x, v_e_lru_lambda, v_e_w_out, v_e_ln_g, v_e_ln_b, v_o_w_in, v_o_sinks, v_o_w_out, v_o_ln_g, v_o_ln_b):
    d = D_MODEL
    x0 = x[0]
    target = loss_target[0]
    s = x0.shape[0]

    spack = _small_block([e_conv_w.reshape(512), o_ln_g.reshape(128), o_ln_b.reshape(128)])
    wi_e, sall = _gather_two_level([e_w_in[0].astype(BF16), spack], [GATHER_COLS, GATHER], name="gather_w_in")
    conv_w = sall[:, 0, 0:512].reshape(N_DEV, 4, 128).transpose(1, 0, 2).reshape(4, d)
    ln_g_o = sall[:, 0, 512:640].reshape(1, d)
    ln_b_o = sall[:, 0, 640:768].reshape(1, d)
    pvec = jnp.concatenate([conv_w, e_conv_b, e_b_gate_a, e_b_gate_x, e_lru_lambda], axis=0)
    wa16 = e_w_gate_a[0].astype(BF16)
    wx16 = e_w_gate_x[0].astype(BF16)
    sinks = o_sinks[0]

    x0_16 = x0.astype(BF16)
    h0a, wall_out_e = _matmul(x0_16, wi_e, mode="nn", n_out=2 * d, b_off=0, name="l0_in_a",
                              comm=([e_w_out[0].astype(BF16)], [GATHER]))
    qkv, wall_in_o = _matmul(x0_16, wi_e, mode="nn", n_out=3 * d, b_off=2 * d, out_dtype=BF16, name="l0_in_qkv",
                             comm=([o_w_in[0].astype(BF16)], [GATHER]))
    bg, wall_out_o = _matmul(x0_16, wi_e, mode="nn", n_out=d, b_off=5 * d, name="l0_in_bg",
                             comm=([o_w_out[0].astype(BF16)], [GATHER]))
    wo_e = wall_out_e.reshape(2 * d, d)
    wi_o = wall_in_o.transpose(1, 0, 2).reshape(d, 2304)
    wi_o = jnp.concatenate([wi_o[:, :1024], wi_o[:, 1280:], wi_o[:, 1024:1280]], axis=1)
    wo_o = wall_out_o.reshape(d, d)
    ya, hst = _rglru_fwd(h0a, pvec, wa16, wx16)
    ob, yb = _sb_fwd(qkv, bg)
    z0, x1, x1_16 = _matmul([ya, yb], wo_e, mode="nn", tm=512, name="l0_out",
                            epi=(_epi_ln_fwd, [(x0, 0)], [e_ln_g, e_ln_b], [F32, F32, BF16], 0))

    q1 = _matmul(x1_16, wi_o, mode="nn", n_out=d, b_off=0, out_dtype=BF16, name="l1_in_q")
    g1 = _matmul(x1_16, wi_o, mode="nn", n_out=d, b_off=d, name="l1_in_g")
    kv1 = _matmul(x1_16, wi_o, mode="nn", n_out=256, b_off=2 * d, tn=256, out_dtype=BF16, name="l1_in_kv")
    kvp = jnp.concatenate([jnp.zeros((WINDOW, 256), BF16), kv1], axis=0)
    o1, y1 = _swa_fwd(q1, g1, kvp, sinks)
    dz1, dz1_16, loss_cols, dg_o, db_o = _matmul(
        y1, wo_o, mode="nn", tm=512, name="l1_out",
        epi=(_epi_ln_loss_bwd, [(x1, 0), (target, 0)], [ln_g_o, ln_b_o], [F32, BF16], 3))
    loss_hi = jnp.sum(loss_cols).astype(BF16).astype(F32)
    loss_terms = jnp.stack([loss_hi, jnp.sum(loss_cols) - loss_hi]).reshape(1, 2)

    dwo_o = _matmul(y1, dz1_16, mode="tn", out_dtype=BF16, name="l1_dwout")
    do1, dg1, parts_out_o = _matmul(dz1_16, wo_o, mode="nt", tm=512, name="l1_dy",
                                    comm=([dwo_o.reshape(N_DEV, 128, d)], [A2A]),
                                    epi=(_epi_gate_bwd, [(o1, 0), (g1, 0)], [], [BF16, BF16], 0))
    dq1, dkvp, dsink = _swa_bwd(q1, do1, kvp, sinks)
    dkv1 = dkvp[WINDOW:].astype(BF16)
    dh1 = [dq1, dg1, dkv1]
    dw_qg = _matmul(x1_16, [dq1, dg1], mode="tn", out_dtype=BF16, name="l1_dwin_qg")
    dw_kv = _matmul(x1_16, dkv1, mode="tn", out_dtype=BF16, name="l1_dwin_kv")
    dwi_o = jnp.concatenate([dw_qg[:, :1024], dw_kv, dw_qg[:, 1024:]], axis=1)
    dz0, dz0_16, dg_e, db_e, parts_in_o = _matmul(
        dh1, wi_o, mode="nt", tm=512, tk=2304, name="l1_dx",
        comm=([dwi_o.reshape(d, N_DEV, 288).transpose(1, 0, 2)], [A2A]),
        epi=(_epi_ln_bwd, [(dz1, 0), (z0, 0)], [e_ln_g], [F32, BF16], 2))

    dwo_e = _matmul([ya, yb], dz0_16, mode="tn", out_dtype=BF16, name="l0_dwout")
    dhst, dag, parts_out_e = _matmul(
        dz0_16, wo_e, mode="nt", tm=512, n_out=d, b_off=0, name="l0_dy_a",
        comm=([dwo_e.reshape(N_DEV, 256, d)], [A2A]),
        epi=(_epi_gate_bwd, [(hst, 0), (h0a, d)], [], [F32, BF16], 0))
    dob, dbg = _matmul(dz0_16, wo_e, mode="nt", tm=512, n_out=d, b_off=d, name="l0_dy_b",
                       epi=(_epi_gate_bwd, [(ob, 0), (bg, 0)], [], [BF16, BF16], 0))
    dq0, dk0, dv0 = _sb_bwd(qkv, dob)
    dax, dwa, dwx, dpv = _rglru_bwd(h0a, hst, dhst, pvec, wa16, wx16)
    dh0 = [dax, dag, dq0, dk0, dv0, dbg]
    dwi_e = _matmul(x0_16, dh0, mode="tn", out_dtype=BF16, name="l0_dwin")
    repl = _pack_repl([dwa, dwx, dpv[4:5], dpv[5:6], dpv[6:7], dpv[7:8], dg_e, db_e, dsink[:, :C_HEADS], loss_terms])
    small = jnp.concatenate([
        _small_block([dpv[0:4].reshape(4, N_DEV, 128).transpose(1, 0, 2).reshape(N_DEV, 512),
                      dg_o.reshape(N_DEV, 128), db_o.reshape(N_DEV, 128)]),
        repl.reshape(N_DEV, R_REPL, d)], axis=1).astype(BF16)
    grad_x, parts_in_e, parts_small = _matmul(dh0, wi_e, mode="nt", tm=512, res=dz0, alpha=ALPHA, name="l0_dx",
                                              comm=([dwi_e, small], [A2A_COLS, A2A]))

    res_in_e = _adamw(parts_in_e, e_w_in[0], m_e_w_in[0], v_e_w_in[0], tr=512, name="adamw_w_in_e")
    res_out_e = _adamw(parts_out_e, e_w_out[0], m_e_w_out[0], v_e_w_out[0], name="adamw_w_out_e")
    res_in_o = _adamw(parts_in_o, o_w_in[0], m_o_w_in[0], v_o_w_in[0], tr=512, name="adamw_w_in_o")
    res_out_o = _adamw(parts_out_o, o_w_out[0], m_o_w_out[0], v_o_w_out[0], name="adamw_w_out_o")
    res_small = _adamw(parts_small, _small_vectors(e_conv_w, o_ln_g, o_ln_b),
                       _small_vectors(m_e_conv_w, m_o_ln_g, m_o_ln_b),
                       _small_vectors(v_e_conv_w, v_o_ln_g, v_o_ln_b), name="adamw_vectors")

    (g_repl,) = _exchange([res_small[0][R_SMALL:]], [GATHER], name="gather_repl_grads")
    g_repl = g_repl.reshape(1, REPL_ROWS, d)
    w_r = _pack_repl([e_w_gate_a, e_w_gate_x, e_conv_b, e_b_gate_a, e_b_gate_x, e_lru_lambda, e_ln_g, e_ln_b, o_sinks])
    m_r = _pack_repl([m_e_w_gate_a, m_e_w_gate_x, m_e_conv_b, m_e_b_gate_a, m_e_b_gate_x, m_e_lru_lambda, m_e_ln_g,
                      m_e_ln_b, m_o_sinks])
    v_r = _pack_repl([v_e_w_gate_a, v_e_w_gate_x, v_e_conv_b, v_e_b_gate_a, v_e_b_gate_x, v_e_lru_lambda, v_e_ln_g,
                      v_e_ln_b, v_o_sinks])
    g_r, d_r, nm_r, nv_r = _adamw(g_repl, w_r, m_r, v_r, name="adamw_replicated")
    loss_at = sum(math.prod(shp) for shp in REPL_SHAPES)
    loss = g_r[loss_at // d, loss_at % d] + g_r[(loss_at + 1) // d, (loss_at + 1) % d]

    def assemble(i, rp):
        vec = res_small[i][0]
        cw, lg_o, lb_o = vec[0:512].reshape(1, 4, 128), vec[512:640].reshape(1, 128), vec[640:768].reshape(1, 128)
        w_a, w_x, cb, b_a, b_x, lam, lg_e, lb_e, snk = _unpack_repl(rp)
        return [res_in_e[i][None], cw, cb, w_a, b_a, w_x, b_x, lam, res_out_e[i][None], lg_e, lb_e,
                res_in_o[i][None], snk, res_out_o[i][None], lg_o, lb_o]

    return (loss, grad_x[None], *assemble(0, g_r), *assemble(1, d_r), *assemble(2, nm_r), *assemble(3, nv_r))
```

```python
import functools
import math

import jax
import jax.numpy as jnp
from jax import lax
from jax.experimental import pallas as pl
from jax.experimental.pallas import tpu as pltpu

F32 = jnp.float32
BF16 = jnp.bfloat16

N_DEV = 8
D_MODEL = 1024
LRU_BLOCKS = 8
LRU_BLOCK = 128
LRU_C = 8.0
SB_HEADS = 8
SB_HEAD_DIM = 128
C_HEADS = 16
C_KV_HEADS = 2
C_GROUP = 8
C_HEAD_DIM = 64
WINDOW = 128
DEPTH = 2
ALPHA = float((2 * DEPTH) ** 0.25)
LN_EPS = 1e-5
ADAM_LR = 0.001
ADAM_B1 = 0.9
ADAM_B2 = 0.999
ADAM_EPS = 1e-08
ADAM_WD = 0.01
ADAM_STEP = 10

VMEM_LIMIT = 56 * 1024 * 1024

NN = ((1,), (0,))
NT = ((1,), (1,))
TN = ((0,), (0,))


def _dot(a, b, dims):
    return lax.dot_general(a, b, (dims, ((), ())), preferred_element_type=F32)


def _sigmoid(x):
    return 1.0 / (1.0 + jnp.exp(-x))


def _cparams(sem, vmem=VMEM_LIMIT):
    return pltpu.CompilerParams(dimension_semantics=sem, vmem_limit_bytes=vmem)


def _matmul(a, b, *, mode, n_out=None, b_off=0, out_dtype=F32, res=None, alpha=1.0,
            tm=1024, tn=1024, tk=1024, comm=None, epi=None, name):
    a_list = list(a) if isinstance(a, (list, tuple)) else [a]
    b_list = list(b) if isinstance(b, (list, tuple)) else [b]
    if mode == "tn":
        k = a_list[0].shape[0]
        m = sum(p.shape[1] for p in a_list)
        n = n_out if n_out is not None else sum(p.shape[1] for p in b_list)
        assert all(p.shape[1] == tm for p in a_list) or len(a_list) == 1
        assert all(p.shape[1] == tn for p in b_list) or len(b_list) == 1
    else:
        assert len(b_list) == 1
        m = a_list[0].shape[0]
        k = sum(p.shape[1] for p in a_list)
        n = n_out if n_out is not None else (b.shape[0] if mode == "nt" else b.shape[1])
        if len(a_list) > 1:
            tk = k
    tm, tn, tk = min(tm, m), min(tn, n), min(tk, k)
    assert m % tm == 0 and n % tn == 0 and k % tk == 0 and b_off % tn == 0
    grid = (m // tm, n // tn, k // tk)
    nk = grid[2]
    jo = b_off // tn
    dims = {"nn": NN, "nt": NT, "tn": TN}[mode]
    nc = len(comm[0]) if comm else 0
    if epi:
        assert res is None
        epi_fn, epi_rows, epi_vecs, epi_dtypes, n_sums = epi
        assert n_sums == 0 or grid[1] == 1
    else:
        epi_rows, epi_vecs, epi_dtypes, n_sums = [], [], [out_dtype], 0
    n_ab = len(a_list) + len(b_list)
    n_in = n_ab + (res is not None) + len(epi_rows) + len(epi_vecs)
    n_res = len(epi_dtypes) + n_sums

    def body(*refs):
        a_refs, b_refs = refs[:len(a_list)], refs[len(a_list):n_ab]
        r_ref = refs[n_ab] if res is not None else None
        row_refs = refs[n_in - len(epi_rows) - len(epi_vecs):n_in - len(epi_vecs)]
        vec_refs = refs[n_in - len(epi_vecs):n_in]
        o_refs = refs[n_in + nc:n_in + nc + len(epi_dtypes)]
        sum_refs = refs[n_in + nc + len(epi_dtypes):n_in + nc + n_res]
        if comm:
            step = [pl.program_id(ax) for ax in range(3)]
            c_src = refs[n_in:n_in + nc]
            c_dst = refs[n_in + nc + n_res:n_in + 2 * nc + n_res]
            sems = refs[len(refs) - 3:]

            @pl.when((step[0] == 0) & (step[1] == 0) & (step[2] == 0))
            def _():
                _exchange_start(c_src, c_dst, comm[1], sems)

        def finish(out):
            if not epi:
                if r_ref is not None:
                    out = out + alpha * r_ref[...]
                o_refs[0][...] = out.astype(out_dtype)
                return
            outs, sums = epi_fn(out, [r[...] for r in row_refs], [v[...] for v in vec_refs])
            for o_ref, o in zip(o_refs, outs):
                o_ref[...] = o.astype(o_ref.dtype)
            if sum_refs:
                first_rows = pl.program_id(0) == 0

                @pl.when(first_rows)
                def _():
                    for s_ref, part_sum in zip(sum_refs, sums):
                        s_ref[...] = part_sum

                @pl.when(jnp.logical_not(first_rows))
                def _():
                    for s_ref, part_sum in zip(sum_refs, sums):
                        s_ref[...] += part_sum

        def accumulate(part):
            if nk == 1:
                finish(part)
                return
            acc_ref = refs[n_in + 2 * nc + n_res]
            kk = pl.program_id(2)

            @pl.when(kk == 0)
            def _():
                acc_ref[...] = part

            @pl.when(kk > 0)
            def _():
                acc_ref[...] += part

            @pl.when(kk == nk - 1)
            def _():
                finish(acc_ref[...])

        def product(a_ref, b_val):
            return _dot(a_ref[...].astype(BF16), b_val.astype(BF16), dims)

        if mode != "tn" and len(a_list) > 1:
            part, off = None, 0
            for a_ref in a_refs:
                w = a_ref.shape[1]
                b_val = b_refs[0][off:off + w, :] if mode == "nn" else b_refs[0][:, off:off + w]
                part = product(a_ref, b_val) if part is None else part + product(a_ref, b_val)
                off += w
            accumulate(part)
        elif len(a_list) > 1 or len(b_list) > 1:
            for pa, a_ref in enumerate(a_refs):
                for pb, b_ref in enumerate(b_refs):
                    picked = ([pl.program_id(0) == pa] if len(a_list) > 1 else []) + (
                        [pl.program_id(1) == pb] if len(b_list) > 1 else [])
                    pl.when(functools.reduce(jnp.logical_and, picked))(
                        functools.partial(lambda a_ref, b_ref: accumulate(product(a_ref, b_ref[...])), a_ref, b_ref))
        else:
            accumulate(product(a_refs[0], b_refs[0][...]))

        if comm:
            @pl.when((step[0] == grid[0] - 1) & (step[1] == grid[1] - 1) & (step[2] == grid[2] - 1))
            def _():
                _exchange_wait(c_src, c_dst, comm[1], sems)

    def held(axis, p):
        def index(i, j, kk):
            return (jnp.where((i, j)[axis] == p, kk, 0), 0)
        return index

    if mode == "tn":
        if len(a_list) > 1:
            a_specs = [pl.BlockSpec((tk, tm), held(0, p)) for p in range(len(a_list))]
        else:
            a_specs = [pl.BlockSpec((tk, tm), lambda i, j, kk: (kk, i))]
    elif len(a_list) > 1:
        a_specs = [pl.BlockSpec((tm, p.shape[1]), lambda i, j, kk: (i, 0)) for p in a_list]
    else:
        a_specs = [pl.BlockSpec((tm, tk), lambda i, j, kk: (i, kk))]
    if mode == "nt":
        b_specs = [pl.BlockSpec((tn, tk), lambda i, j, kk: (j + jo, kk))]
    elif len(b_list) > 1:
        b_specs = [pl.BlockSpec((tk, tn), held(1, p)) for p in range(len(b_list))]
    else:
        b_specs = [pl.BlockSpec((tk, tn), lambda i, j, kk: (kk, j + jo))]
    o_spec = pl.BlockSpec((tm, tn), lambda i, j, kk: (i, j))
    hbm = pl.BlockSpec(memory_space=pl.ANY)
    row_specs = [pl.BlockSpec((tm, tn), functools.partial(lambda i, j, kk, o: (i, j + o), o=off // tn))
                 for (_, off) in epi_rows]
    vec_specs = [pl.BlockSpec(v.shape, functools.partial(lambda i, j, kk, nd: (0,) * nd, nd=v.ndim)) for v in epi_vecs]
    in_specs = a_specs + b_specs + ([o_spec] if res is not None else []) + row_specs + vec_specs + [hbm] * nc
    args = (a_list + b_list + ([res] if res is not None else []) + [r for (r, _) in epi_rows] + list(epi_vecs)
            + (list(comm[0]) if comm else []))
    out_specs = [o_spec] * len(epi_dtypes) + [pl.BlockSpec((1, tn), lambda i, j, kk: (0, j))] * n_sums + [hbm] * nc
    out_shape = ([jax.ShapeDtypeStruct((m, n), dt) for dt in epi_dtypes] + [jax.ShapeDtypeStruct((1, n), F32)] * n_sums
                 + (_exchange_out_shapes(*comm) if comm else []))
    scratch = ([pltpu.VMEM((tm, tn), F32)] if nk > 1 else []) + (_exchange_sems(nc) if comm else [])
    if comm or n_sums or n_ab > 2:
        params = pltpu.CompilerParams(dimension_semantics=("arbitrary",) * 3, vmem_limit_bytes=VMEM_LIMIT,
                                      has_side_effects=bool(comm))
    else:
        params = _cparams(("parallel", "parallel", "arbitrary"))
    outs = pl.pallas_call(
        body, name=name, grid=grid, in_specs=in_specs, out_specs=out_specs, out_shape=out_shape,
        scratch_shapes=scratch, compiler_params=params,
    )(*args)
    return tuple(outs) if (comm or epi) else outs[0]


def _ln_stats(z):
    mu = jnp.mean(z, axis=-1, keepdims=True)
    zc = z - mu
    var = jnp.mean(zc * zc, axis=-1, keepdims=True)
    rstd = lax.rsqrt(var + LN_EPS)
    return zc * rstd, rstd


def _ln_bwd(dy, xhat, rstd, g):
    dxh = dy * g
    m1 = jnp.mean(dxh, axis=-1, keepdims=True)
    m2 = jnp.mean(dxh * xhat, axis=-1, keepdims=True)
    return rstd * (dxh - m1 - xhat * m2)


def _colsum(x):
    return jnp.sum(x, axis=0, keepdims=True)


def _epi_ln_fwd(acc, rows, vecs):
    z = ALPHA * rows[0] + acc
    xhat, _ = _ln_stats(z)
    x1 = xhat * vecs[0] + vecs[1]
    return [z, x1, x1], []


def _epi_ln_loss_bwd(acc, rows, vecs):
    inv_c = 1.0 / acc.shape[-1]
    z = ALPHA * rows[0] + acc
    xhat, rstd = _ln_stats(z)
    d = xhat * vecs[0] + vecs[1] - rows[1]
    dy = d * inv_c
    dz = _ln_bwd(dy, xhat, rstd, vecs[0])
    return [dz, dz], [_colsum(0.5 * d * d * inv_c), _colsum(dy * xhat), _colsum(dy)]


def _epi_ln_bwd(acc, rows, vecs):
    dy = acc + ALPHA * rows[0]
    xhat, rstd = _ln_stats(rows[1])
    dz = _ln_bwd(dy, xhat, rstd, vecs[0])
    return [dz, dz], [_colsum(dy * xhat), _colsum(dy)]


def _epi_gate_bwd(acc, rows, vecs):
    oo, gg = rows
    sg = _sigmoid(gg)
    return [acc * (gg * sg), acc * oo * (sg * (1.0 + gg * (1.0 - sg)))], []


LRU_T = 256
HALO = 8


def _log1p(y):
    u = 1.0 + y
    return jnp.where(u == 1.0, y, jnp.log(u) * (y / (u - 1.0)))


def _lru_gates(c, wa_ref, wx_ref, pv):
    c16 = c.astype(BF16)
    pre_r = jnp.concatenate(
        [_dot(c16[:, n * LRU_BLOCK:(n + 1) * LRU_BLOCK], wa_ref[n], NN) for n in range(LRU_BLOCKS)], axis=1)
    pre_i = jnp.concatenate(
        [_dot(c16[:, n * LRU_BLOCK:(n + 1) * LRU_BLOCK], wx_ref[n], NN) for n in range(LRU_BLOCKS)], axis=1)
    r = _sigmoid(pre_r + pv[5:6])
    ig = _sigmoid(pre_i + pv[6:7])
    lam = pv[7:8]
    ls = jnp.minimum(lam, 0.0) - _log1p(jnp.exp(-jnp.abs(lam)))
    la = LRU_C * r * ls
    a = jnp.exp(la)
    a2 = a * a
    m = jnp.sqrt(jnp.tanh(-la) * (a2 + 1.0))
    return c16, r, ig, ls, la, a, a2, m


def _conv(ext_ref, x, pv, t):
    return (pv[4:5] + pv[3:4] * x + pv[2:3] * ext_ref[pl.ds(HALO - 1, t), :]
            + pv[1:2] * ext_ref[pl.ds(HALO - 2, t), :] + pv[0:1] * ext_ref[pl.ds(HALO - 3, t), :])


def _rglru_fwd(h0a, pvec, wa16, wx16):
    s = h0a.shape[0]
    w = D_MODEL
    t = min(LRU_T, s)
    assert s % t == 0

    def body(ax_ref, ag_ref, pv_ref, wa_ref, wx_ref, ya_ref, h_ref, ext_ref, a_ref, u_ref, hc_ref):
        i = pl.program_id(0)

        @pl.when(i == 0)
        def _():
            ext_ref[pl.ds(0, HALO), :] = jnp.zeros((HALO, w), F32)
            hc_ref[...] = jnp.zeros((1, w), F32)

        pv = pv_ref[...]
        ax = ax_ref[...]
        ext_ref[pl.ds(HALO, t), :] = ax
        c = _conv(ext_ref, ax, pv, t)
        ext_ref[pl.ds(0, HALO), :] = ax[t - HALO:, :]
        _, _, ig, _, _, a, _, m = _lru_gates(c, wa_ref, wx_ref, pv)
        a_ref[...] = a
        u_ref[...] = m * (ig * c)

        def steps(k, h):
            base = pl.multiple_of(k * HALO, HALO)
            for r in range(HALO):
                h = a_ref[pl.ds(base + r, 1), :] * h + u_ref[pl.ds(base + r, 1), :]
                h_ref[pl.ds(base + r, 1), :] = h
            return h

        hc_ref[...] = lax.fori_loop(0, t // HALO, steps, hc_ref[...])
        ag = ag_ref[...]
        ya_ref[...] = (h_ref[...] * (ag * _sigmoid(ag))).astype(BF16)

    full = lambda shp: pl.BlockSpec(shp, lambda i: (0,) * len(shp))
    return pl.pallas_call(
        body, name="rglru_fwd", grid=(s // t,),
        in_specs=[pl.BlockSpec((t, w), lambda i: (i, 0)), pl.BlockSpec((t, w), lambda i: (i, 1)),
                  full((8, w)), full((LRU_BLOCKS, LRU_BLOCK, LRU_BLOCK)), full((LRU_BLOCKS, LRU_BLOCK, LRU_BLOCK))],
        out_specs=[pl.BlockSpec((t, w), lambda i: (i, 0)), pl.BlockSpec((t, w), lambda i: (i, 0))],
        out_shape=[jax.ShapeDtypeStruct((s, w), BF16), jax.ShapeDtypeStruct((s, w), F32)],
        scratch_shapes=[pltpu.VMEM((t + HALO, w), F32), pltpu.VMEM((t, w), F32), pltpu.VMEM((t, w), F32),
                        pltpu.VMEM((1, w), F32)],
        compiler_params=_cparams(("arbitrary",)),
    )(h0a, h0a, pvec, wa16, wx16)


def _rglru_bwd(h0a, h, dh, pvec, wa16, wx16):
    s = h0a.shape[0]
    w = D_MODEL
    t = min(LRU_T, s)
    nb = s // t
    hb = t // HALO

    def body(ax_ref, axh_ref, h_ref, hh_ref, dh_ref, pv_ref, wa_ref, wx_ref,
             dax_ref, dwa_ref, dwx_ref, dpv_ref, ext_ref, hext_ref, dcext_ref, a_ref, g_ref, gc_ref):
        i = pl.program_id(0)
        blk = nb - 1 - i

        @pl.when(i == 0)
        def _():
            dwa_ref[...] = jnp.zeros_like(dwa_ref)
            dwx_ref[...] = jnp.zeros_like(dwx_ref)
            dpv_ref[...] = jnp.zeros_like(dpv_ref)
            gc_ref[...] = jnp.zeros((1, w), F32)
            dcext_ref[pl.ds(t, HALO), :] = jnp.zeros((HALO, w), F32)

        pv = pv_ref[...]
        ax = ax_ref[...]
        keep = (blk > 0).astype(F32)
        ext_ref[pl.ds(0, HALO), :] = axh_ref[...] * keep
        ext_ref[pl.ds(HALO, t), :] = ax
        hext_ref[pl.ds(0, HALO), :] = hh_ref[...] * keep
        hext_ref[pl.ds(HALO, t), :] = h_ref[...]
        c = _conv(ext_ref, ax, pv, t)
        c16, r, ig, ls, _, a, a2, m = _lru_gates(c, wa_ref, wx_ref, pv)
        a_ref[...] = a

        def steps(k, carry):
            base = pl.multiple_of(t - HALO - k * HALO, HALO)
            for r in reversed(range(HALO)):
                g = dh_ref[pl.ds(base + r, 1), :] + carry
                g_ref[pl.ds(base + r, 1), :] = g
                carry = a_ref[pl.ds(base + r, 1), :] * g
            return carry

        gc_ref[...] = lax.fori_loop(0, t // HALO, steps, gc_ref[...])
        g = g_ref[...]
        hprev = hext_ref[pl.ds(HALO - 1, t), :]
        gm = g * m
        d_la = g * hprev * a - (g * ig * c) * a2 / m
        d_pr = d_la * (LRU_C * ls) * r * (1.0 - r)
        d_pi = gm * c * ig * (1.0 - ig)
        dc = gm * ig
        dpr16 = d_pr.astype(BF16)
        dpi16 = d_pi.astype(BF16)
        dc_parts = []
        for n in range(LRU_BLOCKS):
            sl = slice(n * LRU_BLOCK, (n + 1) * LRU_BLOCK)
            dwa_ref[n] += _dot(c16[:, sl], dpr16[:, sl], TN)
            dwx_ref[n] += _dot(c16[:, sl], dpi16[:, sl], TN)
            dc_parts.append(_dot(dpr16[:, sl], wa_ref[n], NT) + _dot(dpi16[:, sl], wx_ref[n], NT))
        dc = dc + jnp.concatenate(dc_parts, axis=1)
        dcext_ref[pl.ds(0, t), :] = dc
        dax = (pv[3:4] * dc + pv[2:3] * dcext_ref[pl.ds(1, t), :] + pv[1:2] * dcext_ref[pl.ds(2, t), :]
               + pv[0:1] * dcext_ref[pl.ds(3, t), :])
        dax_ref[...] = dax.astype(BF16)
        dcext_ref[pl.ds(t, HALO), :] = dc[:HALO, :]
        sums = [_colsum(dc * ext_ref[pl.ds(HALO - 3 + kk, t), :]) for kk in range(4)]
        sums += [_colsum(dc), _colsum(d_pr), _colsum(d_pi), _colsum(d_la * (LRU_C * r))]
        for kk, part in enumerate(sums):
            dpv_ref[pl.ds(kk, 1), :] += part

        @pl.when(i == nb - 1)
        def _():
            lam = pv[7:8]
            dls = dpv_ref[pl.ds(7, 1), :]
            dpv_ref[pl.ds(7, 1), :] = dls * _sigmoid(-lam)

    full = lambda shp: pl.BlockSpec(shp, lambda i: (0,) * len(shp))
    rev = lambda cb: pl.BlockSpec((t, w), functools.partial(lambda i, cb: (nb - 1 - i, cb), cb=cb))
    halo = lambda cb: pl.BlockSpec(
        (HALO, w), functools.partial(lambda i, cb: (jnp.maximum((nb - 1 - i) * hb - 1, 0), cb), cb=cb))
    gw = (LRU_BLOCKS, LRU_BLOCK, LRU_BLOCK)
    return pl.pallas_call(
        body, name="rglru_bwd", grid=(nb,),
        in_specs=[rev(0), halo(0), rev(0), halo(0), rev(0), full((8, w)), full(gw), full(gw)],
        out_specs=[rev(0), full(gw), full(gw), full((8, w))],
        out_shape=[jax.ShapeDtypeStruct((s, w), BF16), jax.ShapeDtypeStruct(gw, F32),
                   jax.ShapeDtypeStruct(gw, F32), jax.ShapeDtypeStruct((8, w), F32)],
        scratch_shapes=[pltpu.VMEM((t + HALO, w), F32), pltpu.VMEM((t + HALO, w), F32),
                        pltpu.VMEM((t + HALO, w), F32), pltpu.VMEM((t, w), F32), pltpu.VMEM((t, w), F32),
                        pltpu.VMEM((1, w), F32)],
        compiler_params=_cparams(("arbitrary",)),
    )(h0a, h0a, h, h, dh, pvec, wa16, wx16)


SB_T = 256
SB_SUB = 4


def _split16(x):
    hi = x.astype(BF16)
    lo = (x - hi.astype(F32)).astype(BF16)
    return jnp.concatenate([hi, lo], axis=0)


def _sb_tile(q, k, scale, tri, run, causal):
    tq = q.shape[0]
    z = _dot(q, k, NT) * scale
    log1mb = -(jnp.maximum(z, 0.0) + jnp.log(1.0 + jnp.exp(-jnp.abs(z))))
    if causal is not None:
        log1mb = jnp.where(causal, log1mb, 0.0)
    cs = _dot(_split16(log1mb), tri, NN)
    cs = cs[:tq] + cs[tq:]
    wgt = jnp.exp(z + cs + run)
    if causal is not None:
        wgt = jnp.where(causal, wgt, 0.0)
    return z, wgt, run + cs[:, 0:1]


SB_DEAD = -105.0


def _sb_alive(run):
    return (jnp.max(run) > SB_DEAD).astype(jnp.int32)


def _sb_more(carry):
    return (carry[0] >= 0) & (carry[1] > 0)


def _sb_fwd(qkv, bg):
    s = qkv.shape[0]
    t = min(SB_T, s)
    nq = s // t
    dh = SB_HEAD_DIM
    scale = 1.0 / math.sqrt(dh)

    sub = SB_SUB if nq % SB_SUB == 0 else 1

    def body(q_ref, k_ref, v_ref, bg_ref, o_ref, y_ref):
        row = lax.broadcasted_iota(jnp.int32, (t, t), 0)
        col = lax.broadcasted_iota(jnp.int32, (t, t), 1)
        tri = jnp.where(row >= col, 1.0, 0.0).astype(BF16)
        causal = col < row
        for u in range(sub):
            rows = pl.ds(u * t, t)
            query_block(pl.program_id(1) * sub + u, q_ref[rows, :], tri, causal, k_ref, v_ref, bg_ref.at[rows, :],
                        o_ref.at[rows, :], y_ref.at[rows, :])

    def query_block(iq, q, tri, causal, k_ref, v_ref, bg_ref, o_ref, y_ref):
        def tile(kb, run, mask):
            off = pl.multiple_of(kb * t, t)
            _, wgt, run = _sb_tile(q, k_ref[pl.ds(off, t), :], scale, tri, run, mask)
            return run, _dot(wgt.astype(BF16), v_ref[pl.ds(off, t), :], NN)

        has_left = iq > 0
        run_d, acc_d = tile(iq, jnp.zeros((t, 1), F32), causal)
        run_l, acc_l = tile(jnp.maximum(iq - 1, 0), run_d, None)
        run = jnp.where(has_left, run_l, run_d)
        acc = acc_d + jnp.where(has_left, acc_l, 0.0)

        def loop(carry):
            kb, _, run, acc = carry
            run, part = tile(kb, run, None)
            return kb - 1, _sb_alive(run), run, acc + part

        _, _, run, acc = lax.while_loop(_sb_more, loop, (iq - 2, _sb_alive(run), run, acc))
        o_ref[...] = acc
        g = bg_ref[...]
        y_ref[...] = (acc * (g * _sigmoid(g))).astype(BF16)

    blk = lambda off: pl.BlockSpec((sub * t, dh), functools.partial(lambda h, i, off: (i, h + off), off=off))
    col = lambda off: pl.BlockSpec((s, dh), functools.partial(lambda h, i, off: (0, h + off), off=off))
    return pl.pallas_call(
        body, name="sb_fwd", grid=(SB_HEADS, nq // sub),
        in_specs=[blk(0), col(SB_HEADS), col(2 * SB_HEADS), blk(0)],
        out_specs=[blk(0), blk(0)],
        out_shape=[jax.ShapeDtypeStruct((s, SB_HEADS * dh), F32), jax.ShapeDtypeStruct((s, SB_HEADS * dh), BF16)],
        compiler_params=_cparams(("parallel", "arbitrary")),
    )(qkv, qkv, qkv, bg)


def _sb_bwd(qkv, dob):
    s = qkv.shape[0]
    t = min(SB_T, s)
    nq = s // t
    dh = SB_HEAD_DIM
    scale = 1.0 / math.sqrt(dh)

    sub = SB_SUB if nq % SB_SUB == 0 else 1

    def body(q_ref, k_ref, v_ref, do_ref, dq_ref, dk_out, dv_out, e_ref, b_ref, dk_ref, dv_ref):
        @pl.when(pl.program_id(1) == 0)
        def _():
            dk_ref[...] = jnp.zeros_like(dk_ref)
            dv_ref[...] = jnp.zeros_like(dv_ref)

        row = lax.broadcasted_iota(jnp.int32, (t, t), 0)
        col = lax.broadcasted_iota(jnp.int32, (t, t), 1)
        tri = jnp.where(row >= col, 1.0, 0.0).astype(BF16)
        tri_x = jnp.where(row < col, 1.0, 0.0).astype(BF16)
        for u in range(sub):
            rows = pl.ds(u * t, t)
            query_block(pl.program_id(1) * sub + u, q_ref[rows, :], do_ref[rows, :], tri, tri_x, col < row,
                        k_ref, v_ref, dq_ref.at[rows, :], dk_ref, dv_ref, e_ref, b_ref)

        @pl.when(pl.program_id(1) == nq // sub - 1)
        def _():
            dk_out[...] = dk_ref[...].astype(BF16)
            dv_out[...] = dv_ref[...].astype(BF16)

    def query_block(iq, q, do, tri, tri_x, causal, k_ref, v_ref, dq_ref, dk_ref, dv_ref, e_ref, b_ref):
        def sweep1(kb, slot, run, mask, live=None):
            off = pl.multiple_of(kb * t, t)
            v = v_ref[pl.ds(off, t), :]
            z, wgt, run = _sb_tile(q, k_ref[pl.ds(off, t), :], scale, tri, run, mask)
            if live is not None:
                wgt = jnp.where(live, wgt, 0.0)
            e_ref[slot] = wgt * _dot(do, v, NT)
            b_ref[slot] = _sigmoid(z)
            dv_ref[pl.ds(off, t), :] += _dot(wgt.astype(BF16), do, TN)
            return run

        has_left = iq > 0
        left = jnp.maximum(iq - 1, 0)
        left_slot = jnp.where(has_left, iq - 1, nq)
        run_d = sweep1(iq, iq, jnp.zeros((t, 1), F32), causal)
        run = jnp.where(has_left, sweep1(left, left_slot, run_d, None, has_left), run_d)

        def loop1(carry):
            run = sweep1(carry[0], carry[0], carry[2], None)
            return carry[0] - 1, _sb_alive(run), run

        first = lax.while_loop(_sb_more, loop1, (iq - 2, _sb_alive(run), run))[0] + 1

        def sweep2(kb, slot, pre, mask, live=None):
            off = pl.multiple_of(kb * t, t)
            e = e_ref[slot]
            beta = b_ref[slot]
            ps = _dot(_split16(e), tri_x, NN)
            ps = ps[:t] + ps[t:]
            dz = e * (1.0 - beta) - beta * (ps + pre)
            if mask is not None:
                dz = jnp.where(mask, dz, 0.0)
            if live is not None:
                dz = jnp.where(live, dz, 0.0)
            dz16 = (dz * scale).astype(BF16)
            dk_ref[pl.ds(off, t), :] += _dot(dz16, q, TN)
            return pre + ps[:, t - 1:t] + e[:, t - 1:t], _dot(dz16, k_ref[pl.ds(off, t), :], NN)

        def loop2(kb, carry):
            pre, part = sweep2(kb, kb, carry[0], None)
            return pre, carry[1] + part

        pre, dq = lax.fori_loop(first, iq - 1, loop2, (jnp.zeros((t, 1), F32), jnp.zeros((t, dh), F32)))
        pre, dq_l = sweep2(left, left_slot, pre, None, has_left)
        _, dq_d = sweep2(iq, iq, pre, causal)
        dq_ref[...] = (dq + dq_l + dq_d).astype(BF16)

    blk = lambda off: pl.BlockSpec((sub * t, dh), functools.partial(lambda h, i, off: (i, h + off), off=off))
    col = lambda off: pl.BlockSpec((s, dh), functools.partial(lambda h, i, off: (0, h + off), off=off))
    wide = SB_HEADS * dh
    return pl.pallas_call(
        body, name="sb_bwd", grid=(SB_HEADS, nq // sub),
        in_specs=[blk(0), col(SB_HEADS), col(2 * SB_HEADS), blk(0)],
        out_specs=[blk(0), col(0), col(0)],
        out_shape=[jax.ShapeDtypeStruct((s, wide), BF16)] * 3,
        scratch_shapes=[pltpu.VMEM((nq + 1, t, t), F32), pltpu.VMEM((nq + 1, t, t), F32),
                        pltpu.VMEM((s, dh), F32), pltpu.VMEM((s, dh), F32)],
        compiler_params=_cparams(("parallel", "arbitrary")),
    )(qkv, qkv, qkv, dob)


def _alibi_slope(h):
    return float(2.0 ** (-8.0 * (h + 1) / C_HEADS))


GROUP_ROWS = C_GROUP * WINDOW


def _swa_window(n):
    qb = WINDOW
    i = lax.broadcasted_iota(jnp.int32, (qb, 2 * qb), 0)
    j = lax.broadcasted_iota(jnp.int32, (qb, 2 * qb), 1)
    d = i - j + qb
    valid = (d >= 0) & (d < WINDOW) & ((j >= qb) | (n > 0))
    stacked = lambda x: jnp.concatenate([x] * C_GROUP, axis=0)
    return stacked(d.astype(F32)), stacked(jnp.where(valid, 0.0, -1e30))


def _swa_group_cols(c, sink_ref):
    head = lax.shift_right_logical(lax.broadcasted_iota(jnp.int32, (GROUP_ROWS, 1), 0), WINDOW.bit_length() - 1)
    slope = jnp.zeros((GROUP_ROWS, 1), F32)
    sink = jnp.zeros((GROUP_ROWS, 1), F32)
    for hh in range(C_GROUP):
        slope = jnp.where(head == hh, _alibi_slope(c * C_GROUP + hh), slope)
        sink = jnp.where(head == hh, sink_ref[c * C_GROUP + hh], sink)
    return slope, sink


def _swa_probs(qg, kw, sink, slope, dist, bias, scale):
    sc = _dot(qg, kw, NT) * scale - slope * dist + bias
    m = jnp.maximum(jnp.max(sc, axis=-1, keepdims=True), sink)
    p = jnp.exp(sc - m)
    ps = jnp.exp(sink - m)
    inv = 1.0 / (jnp.sum(p, axis=-1, keepdims=True) + ps)
    return p * inv, ps * inv


def _stack_heads(x, c):
    hd = C_HEAD_DIM
    return jnp.concatenate([x[:, (c * C_GROUP + hh) * hd:(c * C_GROUP + hh + 1) * hd] for hh in range(C_GROUP)], axis=0)


SWA_SUB = 2


def _swa_kv_specs(sub, kvw):
    return [pl.BlockSpec((WINDOW, 2 * kvw), functools.partial(lambda n, w: (n * sub + w, 0), w=w)) for w in range(sub + 1)]


def _swa_fwd(q16, gate, kvp, sinks):
    s = q16.shape[0]
    qb = WINDOW
    hd = C_HEAD_DIM
    scale = 1.0 / math.sqrt(hd)
    kvw = C_KV_HEADS * hd
    sub = SWA_SUB if (s // qb) % SWA_SUB == 0 else 1

    def body(sink_ref, q_ref, g_ref, *refs):
        kv_refs, (o_ref, y_ref) = refs[:sub + 1], refs[sub + 1:]
        for u in range(sub):
            rows = slice(u * qb, (u + 1) * qb)
            q = q_ref[rows, :]
            kv = jnp.concatenate([kv_refs[u][...], kv_refs[u + 1][...]], axis=0)
            dist, bias = _swa_window(pl.program_id(0) * sub + u)
            for c in range(C_KV_HEADS):
                slope, sink = _swa_group_cols(c, sink_ref)
                kw = kv[:, c * hd:(c + 1) * hd]
                vw = kv[:, kvw + c * hd:kvw + (c + 1) * hd]
                p, _ = _swa_probs(_stack_heads(q, c), kw, sink, slope, dist, bias, scale)
                og = _dot(p.astype(BF16), vw, NN)
                for hh in range(C_GROUP):
                    h = c * C_GROUP + hh
                    o_ref[rows, h * hd:(h + 1) * hd] = og[hh * qb:(hh + 1) * qb, :]
        g = g_ref[...]
        y_ref[...] = (o_ref[...] * (g * _sigmoid(g))).astype(BF16)

    wide = C_HEADS * hd
    big = pl.BlockSpec((sub * qb, wide), lambda n: (n, 0))
    return pl.pallas_call(
        body, name="swa_fwd", grid=(s // (sub * qb),),
        in_specs=[pl.BlockSpec(memory_space=pltpu.SMEM), big, big] + _swa_kv_specs(sub, kvw),
        out_specs=[big, big],
        out_shape=[jax.ShapeDtypeStruct((s, wide), F32), jax.ShapeDtypeStruct((s, wide), BF16)],
        compiler_params=_cparams(("arbitrary",)),
    )(sinks, q16, gate, *([kvp] * (sub + 1)))


def _swa_bwd(q16, do16, kvp, sinks):
    s = q16.shape[0]
    qb = WINDOW
    hd = C_HEAD_DIM
    scale = 1.0 / math.sqrt(hd)
    kvw = C_KV_HEADS * hd
    sub = SWA_SUB if (s // qb) % SWA_SUB == 0 else 1
    nstep = s // (sub * qb)

    def body(sink_ref, q_ref, do_ref, *refs):
        kv_refs, (dq_ref, dkv_ref, ds_ref, sacc_ref) = refs[:sub + 1], refs[sub + 1:]

        @pl.when(pl.program_id(0) == 0)
        def _():
            dkv_ref[...] = jnp.zeros_like(dkv_ref)
            sacc_ref[...] = jnp.zeros_like(sacc_ref)

        for u in range(sub):
            rows = slice(u * qb, (u + 1) * qb)
            n = pl.program_id(0) * sub + u
            q = q_ref[rows, :]
            do = do_ref[rows, :]
            kv = jnp.concatenate([kv_refs[u][...], kv_refs[u + 1][...]], axis=0)
            off = pl.multiple_of(n * qb, qb)
            dist, bias = _swa_window(n)
            for c in range(C_KV_HEADS):
                slope, sink = _swa_group_cols(c, sink_ref)
                kw = kv[:, c * hd:(c + 1) * hd]
                vw = kv[:, kvw + c * hd:kvw + (c + 1) * hd]
                qg = _stack_heads(q, c)
                dog = _stack_heads(do, c)
                p, ps = _swa_probs(qg, kw, sink, slope, dist, bias, scale)
                dp = _dot(dog, vw, NT)
                dd = jnp.sum(p * dp, axis=-1, keepdims=True)
                ds16 = (p * (dp - dd) * scale).astype(BF16)
                sacc_ref[c] += ps * dd
                dqg = _dot(ds16, kw, NN).astype(BF16)
                for hh in range(C_GROUP):
                    h = c * C_GROUP + hh
                    dq_ref[rows, h * hd:(h + 1) * hd] = dqg[hh * qb:(hh + 1) * qb, :]
                dkv_ref[pl.ds(off, 2 * qb), c * hd:(c + 1) * hd] += _dot(ds16, qg, TN)
                dkv_ref[pl.ds(off, 2 * qb), kvw + c * hd:kvw + (c + 1) * hd] += _dot(p.astype(BF16), dog, TN)

        @pl.when(pl.program_id(0) == nstep - 1)
        def _():
            lane = lax.broadcasted_iota(jnp.int32, (1, 128), 1)
            row = jnp.zeros((1, 128), F32)
            for c in range(C_KV_HEADS):
                for hh in range(C_GROUP):
                    tot = jnp.sum(sacc_ref[c, pl.ds(hh * qb, qb), :], axis=0, keepdims=True)
                    row = jnp.where(lane == c * C_GROUP + hh, -tot, row)
            ds_ref[...] = row

    wide = C_HEADS * hd
    big = pl.BlockSpec((sub * qb, wide), lambda n: (n, 0))
    return pl.pallas_call(
        body, name="swa_bwd", grid=(nstep,),
        in_specs=[pl.BlockSpec(memory_space=pltpu.SMEM), big, big] + _swa_kv_specs(sub, kvw),
        out_specs=[big,
                   pl.BlockSpec((s + qb, 2 * kvw), lambda n: (0, 0)), pl.BlockSpec((1, 128), lambda n: (0, 0))],
        out_shape=[jax.ShapeDtypeStruct((s, wide), BF16), jax.ShapeDtypeStruct((s + qb, 2 * kvw), F32),
                   jax.ShapeDtypeStruct((1, 128), F32)],
        scratch_shapes=[pltpu.VMEM((C_KV_HEADS, GROUP_ROWS, 1), F32)],
        compiler_params=_cparams(("arbitrary",)),
    )(sinks, q16, do16, *([kvp] * (sub + 1)))


def _adamw(parts, w, m, v, *, name, tr=496):
    npart, r, c = parts.shape
    tr = min(tr, r)
    assert r % tr == 0 and tr % 16 == 0
    c1 = 1.0 / (1.0 - ADAM_B1 ** ADAM_STEP)
    c2 = 1.0 / (1.0 - ADAM_B2 ** ADAM_STEP)

    def body(p_ref, w_ref, m_ref, v_ref, g_ref, d_ref, nm_ref, nv_ref):
        g = p_ref[0].astype(F32)
        for j in range(1, npart):
            g = g + p_ref[j].astype(F32)
        nm = ADAM_B1 * m_ref[...] + (1.0 - ADAM_B1) * g
        nv = ADAM_B2 * v_ref[...] + (1.0 - ADAM_B2) * (g * g)
        g_ref[...] = g
        nm_ref[...] = nm
        nv_ref[...] = nv
        d_ref[...] = -ADAM_LR * ((nm * c1) / (jnp.sqrt(nv * c2) + ADAM_EPS) + ADAM_WD * w_ref[...])

    spec = pl.BlockSpec((tr, c), lambda i: (i, 0))
    return pl.pallas_call(
        body, name=name, grid=(r // tr,),
        in_specs=[pl.BlockSpec((npart, tr, c), lambda i: (0, i, 0)), spec, spec, spec],
        out_specs=[spec] * 4, out_shape=[jax.ShapeDtypeStruct((r, c), F32)] * 4,
        compiler_params=_cparams(("parallel",)),
    )(parts, w, m, v)


GATHER = "gather"
A2A = "a2a"
GATHER_COLS = "gather_cols"
A2A_COLS = "a2a_cols"


def _exchange(bufs, gather, *, name):
    nb = len(bufs)

    def body(*refs):
        _exchange_start(refs[:nb], refs[nb:2 * nb], gather, refs[2 * nb:])
        _exchange_wait(refs[:nb], refs[nb:2 * nb], gather, refs[2 * nb:])

    hbm = pl.BlockSpec(memory_space=pl.ANY)
    return pl.pallas_call(
        body, name=name, in_specs=[hbm] * nb, out_specs=[hbm] * nb, out_shape=_exchange_out_shapes(bufs, gather),
        scratch_shapes=_exchange_sems(nb),
        compiler_params=pltpu.CompilerParams(has_side_effects=True),
    )(*bufs)


def _gather_two_level(bufs, modes, *, name):
    nb = len(bufs)

    def body(*refs):
        src, dst = refs[:nb], refs[nb:2 * nb]
        send_sems, recv_sems, local_sems = refs[2 * nb:]
        x, y, c = lax.axis_index("x"), lax.axis_index("y"), lax.axis_index("c")
        me = 4 * x + 2 * y + c
        here, sibling = (x, y, c), (x, y, 1 - c)

        def copy(b, k, origin, to, source):
            return pltpu.make_async_remote_copy(
                src_ref=source, dst_ref=_slot(dst[b], modes[b], origin), send_sem=send_sems.at[b, k - 1],
                recv_sem=recv_sems.at[b, k - 1], device_id=to, device_id_type=pl.DeviceIdType.MESH)

        local = [pltpu.make_async_copy(src[b], _slot(dst[b], modes[b], me), local_sems.at[b]) for b in range(nb)]
        sends = [copy(b, k, me, (x ^ (k >> 2), y ^ ((k >> 1) & 1), c ^ (k & 1)), src[b])
                 for k in (1, 2, 4, 6) for b in range(nb)]
        for cp in local + sends:
            cp.start()
        for j in (2, 4, 6):
            for b in range(nb):
                copy(b, j, me ^ j, here, src[b]).wait_recv()
                passed = copy(b, j ^ 1, me ^ j, sibling, _slot(dst[b], modes[b], me ^ j))
                passed.start()
                sends.append(passed)
        for k in (1, 3, 5, 7):
            for b in range(nb):
                copy(b, k, me ^ k, here, src[b]).wait_recv()
        for cp in sends:
            cp.wait_send()
        for cp in local:
            cp.wait()

    hbm = pl.BlockSpec(memory_space=pl.ANY)
    return pl.pallas_call(
        body, name=name, in_specs=[hbm] * nb, out_specs=[hbm] * nb, out_shape=_exchange_out_shapes(bufs, modes),
        scratch_shapes=_exchange_sems(nb),
        compiler_params=pltpu.CompilerParams(has_side_effects=True),
    )(*bufs)


def _exchange_out_shapes(bufs, gather):
    shapes = {GATHER: lambda s: (N_DEV,) + s, A2A: lambda s: s, GATHER_COLS: lambda s: (s[0], N_DEV * s[1]),
              A2A_COLS: lambda s: (N_DEV, s[0], s[1] // N_DEV)}
    return [jax.ShapeDtypeStruct(shapes[g](tuple(b.shape)), b.dtype) for b, g in zip(bufs, gather)]


def _sent(src, mode, peer):
    if mode == A2A:
        return src.at[peer]
    if mode == A2A_COLS:
        w = src.shape[1] // N_DEV
        return src.at[:, pl.ds(pl.multiple_of(peer * w, 128), w)]
    return src


def _slot(dst, mode, dev):
    if mode == GATHER_COLS:
        w = dst.shape[1] // N_DEV
        return dst.at[:, pl.ds(pl.multiple_of(dev * w, 128), w)]
    return dst.at[dev]


def _exchange_sems(nb):
    return [pltpu.SemaphoreType.DMA((nb, N_DEV - 1)), pltpu.SemaphoreType.DMA((nb, N_DEV - 1)),
            pltpu.SemaphoreType.DMA((nb,))]


def _exchange_copies(src, dst, gather, sems):
    send_sems, recv_sems, local_sems = sems
    x, y, c = lax.axis_index("x"), lax.axis_index("y"), lax.axis_index("c")
    me = 4 * x + 2 * y + c
    local, sends, recvs = [], [], []
    for b in range(len(src)):
        mine = _sent(src[b], gather[b], me)
        local.append(pltpu.make_async_copy(mine, _slot(dst[b], gather[b], me), local_sems.at[b]))
        for k in range(1, N_DEV):
            px, py, pc = x ^ (k >> 2), y ^ ((k >> 1) & 1), c ^ (k & 1)
            peer = 4 * px + 2 * py + pc
            pair = dict(send_sem=send_sems.at[b, k - 1], recv_sem=recv_sems.at[b, k - 1],
                        device_id_type=pl.DeviceIdType.MESH)
            sends.append(pltpu.make_async_remote_copy(
                src_ref=_sent(src[b], gather[b], peer), dst_ref=_slot(dst[b], gather[b], me), device_id=(px, py, pc),
                **pair))
            recvs.append(pltpu.make_async_remote_copy(
                src_ref=mine, dst_ref=_slot(dst[b], gather[b], peer), device_id=(x, y, c), **pair))
    return local, sends, recvs


def _exchange_start(src, dst, gather, sems):
    local, sends, _ = _exchange_copies(src, dst, gather, sems)
    for cp in local + sends:
        cp.start()


def _exchange_wait(src, dst, gather, sems):
    local, sends, recvs = _exchange_copies(src, dst, gather, sems)
    for cp in sends:
        cp.wait_send()
    for cp in recvs:
        cp.wait_recv()
    for cp in local:
        cp.wait()


R_SMALL = 16
REPL_ROWS = 272
REPL_LEN = REPL_ROWS * D_MODEL


def _small_block(parts):
    flat = jnp.concatenate(parts, axis=-1)
    lead = flat.ndim - 1
    return jnp.pad(flat[..., None, :], [(0, 0)] * lead + [(0, R_SMALL - 1), (0, D_MODEL - flat.shape[-1])])


def _small_vectors(conv_w, ln_g, ln_b):
    return _small_block([conv_w.reshape(512), ln_g.reshape(128), ln_b.reshape(128)])


REPL_SHAPES = ((1, 8, 128, 128), (1, 8, 128, 128), (1, 1024), (1, 1024), (1, 1024), (1, 1024), (1, 1024), (1, 1024),
               (1, 16))


def _pack_repl(parts):
    flat = jnp.concatenate([p.reshape(-1) for p in parts])
    return jnp.concatenate([flat, jnp.zeros((REPL_LEN - flat.shape[0],), F32)]).reshape(REPL_ROWS, D_MODEL)


def _unpack_repl(p):
    flat = p.reshape(-1)
    out, o = [], 0
    for shp in REPL_SHAPES:
        n = math.prod(shp)
        out.append(flat[o:o + n].reshape(shp))
        o += n
    return out


def kernel(x, e_w_in, e_conv_w, e_conv_b, e_w_gate_a, e_b_gate_a, e_w_gate_x, e_b_gate_x, e_lru_lambda, e_w_out, e_ln_g, e_ln_b, o_w_in, o_sinks, o_w_out, o_ln_g, o_ln_b, loss_target, m_e_w_in, m_e_conv_w, m_e_conv_b, m_e_w_gate_a, m_e_b_gate_a, m_e_w_gate_x, m_e_b_gate_x, m_e_lru_lambda, m_e_w_out, m_e_ln_g, m_e_ln_b, m_o_w_in, m_o_sinks, m_o_w_out, m_o_ln_g, m_o_ln_b, v_e_w_in, v_e_conv_w, v_e_conv_b, v_e_w_gate_a, v_e_b_gate_a, v_e_w_gate_x, v_e_b_gate_x, v_e_lru_lambda, v_e_w_out, v_e_ln_g, v_e_ln_b, v_o_w_in, v_o_sinks, v_o_w_out, v_o_ln_g, v_o_ln_b):
    d = D_MODEL
    x0 = x[0]
    target = loss_target[0]
    s = x0.shape[0]

    spack = _small_block([e_conv_w.reshape(512), o_ln_g.reshape(128), o_ln_b.reshape(128)])
    wi_e, sall = _gather_two_level([e_w_in[0].astype(BF16), spack], [GATHER_COLS, GATHER], name="gather_w_in")
    conv_w = sall[:, 0, 0:512].reshape(N_DEV, 4, 128).transpose(1, 0, 2).reshape(4, d)
    ln_g_o = sall[:, 0, 512:640].reshape(1, d)
    ln_b_o = sall[:, 0, 640:768].reshape(1, d)
    pvec = jnp.concatenate([conv_w, e_conv_b, e_b_gate_a, e_b_gate_x, e_lru_lambda], axis=0)
    wa16 = e_w_gate_a[0].astype(BF16)
    wx16 = e_w_gate_x[0].astype(BF16)
    sinks = o_sinks[0]

    x0_16 = x0.astype(BF16)
    h0a, wall_out_e = _matmul(x0_16, wi_e, mode="nn", n_out=2 * d, b_off=0, name="l0_in_a",
                              comm=([e_w_out[0].astype(BF16)], [GATHER]))
    qkv, wall_in_o = _matmul(x0_16, wi_e, mode="nn", n_out=3 * d, b_off=2 * d, out_dtype=BF16, name="l0_in_qkv",
                             comm=([o_w_in[0].astype(BF16)], [GATHER]))
    bg, wall_out_o = _matmul(x0_16, wi_e, mode="nn", n_out=d, b_off=5 * d, name="l0_in_bg",
                             comm=([o_w_out[0].astype(BF16)], [GATHER]))
    wo_e = wall_out_e.reshape(2 * d, d)
    wi_o = wall_in_o.transpose(1, 0, 2).reshape(d, 2304)
    wi_o = jnp.concatenate([wi_o[:, :1024], wi_o[:, 1280:], wi_o[:, 1024:1280]], axis=1)
    wo_o = wall_out_o.reshape(d, d)
    ya, hst = _rglru_fwd(h0a, pvec, wa16, wx16)
    ob, yb = _sb_fwd(qkv, bg)
    z0, x1, x1_16 = _matmul([ya, yb], wo_e, mode="nn", tm=512, name="l0_out",
                            epi=(_epi_ln_fwd, [(x0, 0)], [e_ln_g, e_ln_b], [F32, F32, BF16], 0))

    q1 = _matmul(x1_16, wi_o, mode="nn", n_out=d, b_off=0, out_dtype=BF16, name="l1_in_q")
    g1 = _matmul(x1_16, wi_o, mode="nn", n_out=d, b_off=d, name="l1_in_g")
    kv1 = _matmul(x1_16, wi_o, mode="nn", n_out=256, b_off=2 * d, tn=256, out_dtype=BF16, name="l1_in_kv")
    kvp = jnp.concatenate([jnp.zeros((WINDOW, 256), BF16), kv1], axis=0)
    o1, y1 = _swa_fwd(q1, g1, kvp, sinks)
    dz1, dz1_16, loss_cols, dg_o, db_o = _matmul(
        y1, wo_o, mode="nn", tm=512, name="l1_out",
        epi=(_epi_ln_loss_bwd, [(x1, 0), (target, 0)], [ln_g_o, ln_b_o], [F32, BF16], 3))
    loss_hi = jnp.sum(loss_cols).astype(BF16).astype(F32)
    loss_terms = jnp.stack([loss_hi, jnp.sum(loss_cols) - loss_hi]).reshape(1, 2)

    dwo_o = _matmul(y1, dz1_16, mode="tn", out_dtype=BF16, name="l1_dwout")
    do1, dg1, parts_out_o = _matmul(dz1_16, wo_o, mode="nt", tm=512, name="l1_dy",
                                    comm=([dwo_o.reshape(N_DEV, 128, d)], [A2A]),
                                    epi=(_epi_gate_bwd, [(o1, 0), (g1, 0)], [], [BF16, BF16], 0))
    dq1, dkvp, dsink = _swa_bwd(q1, do1, kvp, sinks)
    dkv1 = dkvp[WINDOW:].astype(BF16)
    dh1 = [dq1, dg1, dkv1]
    dw_qg = _matmul(x1_16, [dq1, dg1], mode="tn", out_dtype=BF16, name="l1_dwin_qg")
    dw_kv = _matmul(x1_16, dkv1, mode="tn", out_dtype=BF16, name="l1_dwin_kv")
    dwi_o = jnp.concatenate([dw_qg[:, :1024], dw_kv, dw_qg[:, 1024:]], axis=1)
    dz0, dz0_16, dg_e, db_e, parts_in_o = _matmul(
        dh1, wi_o, mode="nt", tm=512, tk=2304, name="l1_dx",
        comm=([dwi_o.reshape(d, N_DEV, 288).transpose(1, 0, 2)], [A2A]),
        epi=(_epi_ln_bwd, [(dz1, 0), (z0, 0)], [e_ln_g], [F32, BF16], 2))

    dwo_e = _matmul([ya, yb], dz0_16, mode="tn", out_dtype=BF16, name="l0_dwout")
    dhst, dag, parts_out_e = _matmul(
        dz0_16, wo_e, mode="nt", tm=512, n_out=d, b_off=0, name="l0_dy_a",
        comm=([dwo_e.reshape(N_DEV, 256, d)], [A2A]),
        epi=(_epi_gate_bwd, [(hst, 0), (h0a, d)], [], [F32, BF16], 0))
    dob, dbg = _matmul(dz0_16, wo_e, mode="nt", tm=512, n_out=d, b_off=d, name="l0_dy_b",
                       epi=(_epi_gate_bwd, [(ob, 0), (bg, 0)], [], [BF16, BF16], 0))
    dq0, dk0, dv0 = _sb_bwd(qkv, dob)
    dax, dwa, dwx, dpv = _rglru_bwd(h0a, hst, dhst, pvec, wa16, wx16)
    dh0 = [dax, dag, dq0, dk0, dv0, dbg]
    repl = _pack_repl([dwa, dwx, dpv[4:5], dpv[5:6], dpv[6:7], dpv[7:8], dg_e, db_e, dsink[:, :C_HEADS], loss_terms])
    dwi_e, parts_repl = _matmul(x0_16, dh0, mode="tn", out_dtype=BF16, name="l0_dwin",
                                comm=([repl.astype(BF16)], [GATHER]))
    small = _small_block([dpv[0:4].reshape(4, N_DEV, 128).transpose(1, 0, 2).reshape(N_DEV, 512),
                          dg_o.reshape(N_DEV, 128), db_o.reshape(N_DEV, 128)]).astype(BF16)
    grad_x, parts_in_e, parts_small = _matmul(dh0, wi_e, mode="nt", tm=512, res=dz0, alpha=ALPHA, name="l0_dx",
                                              comm=([dwi_e, small], [A2A_COLS, A2A]))

    res_in_e = _adamw(parts_in_e, e_w_in[0], m_e_w_in[0], v_e_w_in[0], tr=512, name="adamw_w_in_e")
    res_out_e = _adamw(parts_out_e, e_w_out[0], m_e_w_out[0], v_e_w_out[0], name="adamw_w_out_e")
    res_in_o = _adamw(parts_in_o, o_w_in[0], m_o_w_in[0], v_o_w_in[0], tr=512, name="adamw_w_in_o")
    res_out_o = _adamw(parts_out_o, o_w_out[0], m_o_w_out[0], v_o_w_out[0], name="adamw_w_out_o")
    res_small = _adamw(parts_small, _small_vectors(e_conv_w, o_ln_g, o_ln_b),
                       _small_vectors(m_e_conv_w, m_o_ln_g, m_o_ln_b),
                       _small_vectors(v_e_conv_w, v_o_ln_g, v_o_ln_b), name="adamw_vectors")

    w_r = _pack_repl([e_w_gate_a, e_w_gate_x, e_conv_b, e_b_gate_a, e_b_gate_x, e_lru_lambda, e_ln_g, e_ln_b, o_sinks])
    m_r = _pack_repl([m_e_w_gate_a, m_e_w_gate_x, m_e_conv_b, m_e_b_gate_a, m_e_b_gate_x, m_e_lru_lambda, m_e_ln_g,
                      m_e_ln_b, m_o_sinks])
    v_r = _pack_repl([v_e_w_gate_a, v_e_w_gate_x, v_e_conv_b, v_e_b_gate_a, v_e_b_gate_x, v_e_lru_lambda, v_e_ln_g,
                      v_e_ln_b, v_o_sinks])
    g_r, d_r, nm_r, nv_r = _adamw(parts_repl, w_r, m_r, v_r, name="adamw_replicated")
    loss_at = sum(math.prod(shp) for shp in REPL_SHAPES)
    loss = g_r[loss_at // d, loss_at % d] + g_r[(loss_at + 1) // d, (loss_at + 1) % d]

    def assemble(i, rp):
        vec = res_small[i][0]
        cw, lg_o, lb_o = vec[0:512].reshape(1, 4, 128), vec[512:640].reshape(1, 128), vec[640:768].reshape(1, 128)
        w_a, w_x, cb, b_a, b_x, lam, lg_e, lb_e, snk = _unpack_repl(rp)
        return [res_in_e[i][None], cw, cb, w_a, b_a, w_x, b_x, lam, res_out_e[i][None], lg_e, lb_e,
                res_in_o[i][None], snk, res_out_o[i][None], lg_o, lb_o]

    return (loss, grad_x[None], *assemble(0, g_r), *assemble(1, d_r), *assemble(2, nm_r), *assemble(3, nv_r))
```

```python
import functools
import math

import jax
import jax.numpy as jnp
from jax import lax
from jax.experimental import pallas as pl
from jax.experimental.pallas import tpu as pltpu

F32 = jnp.float32
BF16 = jnp.bfloat16

N_DEV = 8
D_MODEL = 1024
LRU_BLOCKS = 8
LRU_BLOCK = 128
LRU_C = 8.0
SB_HEADS = 8
SB_HEAD_DIM = 128
C_HEADS = 16
C_KV_HEADS = 2
C_GROUP = 8
C_HEAD_DIM = 64
WINDOW = 128
DEPTH = 2
ALPHA = float((2 * DEPTH) ** 0.25)
LN_EPS = 1e-5
ADAM_LR = 0.001
ADAM_B1 = 0.9
ADAM_B2 = 0.999
ADAM_EPS = 1e-08
ADAM_WD = 0.01
ADAM_STEP = 10

VMEM_LIMIT = 56 * 1024 * 1024

NN = ((1,), (0,))
NT = ((1,), (1,))
TN = ((0,), (0,))


def _dot(a, b, dims):
    return lax.dot_general(a, b, (dims, ((), ())), preferred_element_type=F32)


def _sigmoid(x):
    return 1.0 / (1.0 + jnp.exp(-x))


def _cparams(sem, vmem=VMEM_LIMIT):
    return pltpu.CompilerParams(dimension_semantics=sem, vmem_limit_bytes=vmem)


def _matmul(a, b, *, mode, n_out=None, b_off=0, out_dtype=F32, res=None, alpha=1.0,
            tm=1024, tn=1024, tk=1024, comm=None, epi=None, name):
    a_list = list(a) if isinstance(a, (list, tuple)) else [a]
    b_list = list(b) if isinstance(b, (list, tuple)) else [b]
    if mode == "tn":
        k = a_list[0].shape[0]
        m = sum(p.shape[1] for p in a_list)
        n = n_out if n_out is not None else sum(p.shape[1] for p in b_list)
        assert all(p.shape[1] == tm for p in a_list) or len(a_list) == 1
        assert all(p.shape[1] == tn for p in b_list) or len(b_list) == 1
    else:
        assert len(b_list) == 1
        m = a_list[0].shape[0]
        k = sum(p.shape[1] for p in a_list)
        n = n_out if n_out is not None else (b.shape[0] if mode == "nt" else b.shape[1])
        if len(a_list) > 1:
            tk = k
    tm, tn, tk = min(tm, m), min(tn, n), min(tk, k)
    assert m % tm == 0 and n % tn == 0 and k % tk == 0 and b_off % tn == 0
    grid = (m // tm, n // tn, k // tk)
    nk = grid[2]
    jo = b_off // tn
    dims = {"nn": NN, "nt": NT, "tn": TN}[mode]
    nc = len(comm[0]) if comm else 0
    if epi:
        assert res is None
        epi_fn, epi_rows, epi_vecs, epi_dtypes, n_sums = epi
        assert n_sums == 0 or grid[1] == 1
    else:
        epi_rows, epi_vecs, epi_dtypes, n_sums = [], [], [out_dtype], 0
    n_ab = len(a_list) + len(b_list)
    n_in = n_ab + (res is not None) + len(epi_rows) + len(epi_vecs)
    n_res = len(epi_dtypes) + n_sums

    def body(*refs):
        a_refs, b_refs = refs[:len(a_list)], refs[len(a_list):n_ab]
        r_ref = refs[n_ab] if res is not None else None
        row_refs = refs[n_in - len(epi_rows) - len(epi_vecs):n_in - len(epi_vecs)]
        vec_refs = refs[n_in - len(epi_vecs):n_in]
        o_refs = refs[n_in + nc:n_in + nc + len(epi_dtypes)]
        sum_refs = refs[n_in + nc + len(epi_dtypes):n_in + nc + n_res]
        if comm:
            step = [pl.program_id(ax) for ax in range(3)]
            c_src = refs[n_in:n_in + nc]
            c_dst = refs[n_in + nc + n_res:n_in + 2 * nc + n_res]
            sems = refs[len(refs) - 3:]

            @pl.when((step[0] == 0) & (step[1] == 0) & (step[2] == 0))
            def _():
                _exchange_start(c_src, c_dst, comm[1], sems)

        def finish(out):
            if not epi:
                if r_ref is not None:
                    out = out + alpha * r_ref[...]
                o_refs[0][...] = out.astype(out_dtype)
                return
            outs, sums = epi_fn(out, [r[...] for r in row_refs], [v[...] for v in vec_refs])
            for o_ref, o in zip(o_refs, outs):
                o_ref[...] = o.astype(o_ref.dtype)
            if sum_refs:
                first_rows = pl.program_id(0) == 0

                @pl.when(first_rows)
                def _():
                    for s_ref, part_sum in zip(sum_refs, sums):
                        s_ref[...] = part_sum

                @pl.when(jnp.logical_not(first_rows))
                def _():
                    for s_ref, part_sum in zip(sum_refs, sums):
                        s_ref[...] += part_sum

        def accumulate(part):
            if nk == 1:
                finish(part)
                return
            acc_ref = refs[n_in + 2 * nc + n_res]
            kk = pl.program_id(2)

            @pl.when(kk == 0)
            def _():
                acc_ref[...] = part

            @pl.when(kk > 0)
            def _():
                acc_ref[...] += part

            @pl.when(kk == nk - 1)
            def _():
                finish(acc_ref[...])

        def product(a_ref, b_val):
            return _dot(a_ref[...].astype(BF16), b_val.astype(BF16), dims)

        if mode != "tn" and len(a_list) > 1:
            part, off = None, 0
            for a_ref in a_refs:
                w = a_ref.shape[1]
                b_val = b_refs[0][off:off + w, :] if mode == "nn" else b_refs[0][:, off:off + w]
                part = product(a_ref, b_val) if part is None else part + product(a_ref, b_val)
                off += w
            accumulate(part)
        elif len(a_list) > 1 or len(b_list) > 1:
            for pa, a_ref in enumerate(a_refs):
                for pb, b_ref in enumerate(b_refs):
                    picked = ([pl.program_id(0) == pa] if len(a_list) > 1 else []) + (
                        [pl.program_id(1) == pb] if len(b_list) > 1 else [])
                    pl.when(functools.reduce(jnp.logical_and, picked))(
                        functools.partial(lambda a_ref, b_ref: accumulate(product(a_ref, b_ref[...])), a_ref, b_ref))
        else:
            accumulate(product(a_refs[0], b_refs[0][...]))

        if comm:
            @pl.when((step[0] == grid[0] - 1) & (step[1] == grid[1] - 1) & (step[2] == grid[2] - 1))
            def _():
                _exchange_wait(c_src, c_dst, comm[1], sems)

    def held(axis, p):
        def index(i, j, kk):
            return (jnp.where((i, j)[axis] == p, kk, 0), 0)
        return index

    if mode == "tn":
        if len(a_list) > 1:
            a_specs = [pl.BlockSpec((tk, tm), held(0, p)) for p in range(len(a_list))]
        else:
            a_specs = [pl.BlockSpec((tk, tm), lambda i, j, kk: (kk, i))]
    elif len(a_list) > 1:
        a_specs = [pl.BlockSpec((tm, p.shape[1]), lambda i, j, kk: (i, 0)) for p in a_list]
    else:
        a_specs = [pl.BlockSpec((tm, tk), lambda i, j, kk: (i, kk))]
    if mode == "nt":
        b_specs = [pl.BlockSpec((tn, tk), lambda i, j, kk: (j + jo, kk))]
    elif len(b_list) > 1:
        b_specs = [pl.BlockSpec((tk, tn), held(1, p)) for p in range(len(b_list))]
    else:
        b_specs = [pl.BlockSpec((tk, tn), lambda i, j, kk: (kk, j + jo))]
    o_spec = pl.BlockSpec((tm, tn), lambda i, j, kk: (i, j))
    hbm = pl.BlockSpec(memory_space=pl.ANY)
    row_specs = [pl.BlockSpec((tm, tn), functools.partial(lambda i, j, kk, o: (i, j + o), o=off // tn))
                 for (_, off) in epi_rows]
    vec_specs = [pl.BlockSpec(v.shape, functools.partial(lambda i, j, kk, nd: (0,) * nd, nd=v.ndim)) for v in epi_vecs]
    in_specs = a_specs + b_specs + ([o_spec] if res is not None else []) + row_specs + vec_specs + [hbm] * nc
    args = (a_list + b_list + ([res] if res is not None else []) + [r for (r, _) in epi_rows] + list(epi_vecs)
            + (list(comm[0]) if comm else []))
    out_specs = [o_spec] * len(epi_dtypes) + [pl.BlockSpec((1, tn), lambda i, j, kk: (0, j))] * n_sums + [hbm] * nc
    out_shape = ([jax.ShapeDtypeStruct((m, n), dt) for dt in epi_dtypes] + [jax.ShapeDtypeStruct((1, n), F32)] * n_sums
                 + (_exchange_out_shapes(*comm) if comm else []))
    scratch = ([pltpu.VMEM((tm, tn), F32)] if nk > 1 else []) + (_exchange_sems(nc) if comm else [])
    if comm or n_sums or n_ab > 2:
        params = pltpu.CompilerParams(dimension_semantics=("arbitrary",) * 3, vmem_limit_bytes=VMEM_LIMIT,
                                      has_side_effects=bool(comm))
    else:
        params = _cparams(("parallel", "parallel", "arbitrary"))
    outs = pl.pallas_call(
        body, name=name, grid=grid, in_specs=in_specs, out_specs=out_specs, out_shape=out_shape,
        scratch_shapes=scratch, compiler_params=params,
    )(*args)
    return tuple(outs) if (comm or epi) else outs[0]


def _ln_stats(z):
    mu = jnp.mean(z, axis=-1, keepdims=True)
    zc = z - mu
    var = jnp.mean(zc * zc, axis=-1, keepdims=True)
    rstd = lax.rsqrt(var + LN_EPS)
    return zc * rstd, rstd


def _ln_bwd(dy, xhat, rstd, g):
    dxh = dy * g
    m1 = jnp.mean(dxh, axis=-1, keepdims=True)
    m2 = jnp.mean(dxh * xhat, axis=-1, keepdims=True)
    return rstd * (dxh - m1 - xhat * m2)


def _colsum(x):
    return jnp.sum(x, axis=0, keepdims=True)


def _epi_ln_fwd(acc, rows, vecs):
    z = ALPHA * rows[0] + acc
    xhat, _ = _ln_stats(z)
    x1 = xhat * vecs[0] + vecs[1]
    return [z, x1, x1], []


def _epi_ln_loss_bwd(acc, rows, vecs):
    inv_c = 1.0 / acc.shape[-1]
    z = ALPHA * rows[0] + acc
    xhat, rstd = _ln_stats(z)
    d = xhat * vecs[0] + vecs[1] - rows[1]
    dy = d * inv_c
    dz = _ln_bwd(dy, xhat, rstd, vecs[0])
    return [dz, dz], [_colsum(0.5 * d * d * inv_c), _colsum(dy * xhat), _colsum(dy)]


def _epi_ln_bwd(acc, rows, vecs):
    dy = acc + ALPHA * rows[0]
    xhat, rstd = _ln_stats(rows[1])
    dz = _ln_bwd(dy, xhat, rstd, vecs[0])
    return [dz, dz], [_colsum(dy * xhat), _colsum(dy)]


def _epi_gate_bwd(acc, rows, vecs):
    oo, gg = rows
    sg = _sigmoid(gg)
    return [acc * (gg * sg), acc * oo * (sg * (1.0 + gg * (1.0 - sg)))], []


LRU_T = 256
HALO = 8


def _log1p(y):
    u = 1.0 + y
    return jnp.where(u == 1.0, y, jnp.log(u) * (y / (u - 1.0)))


def _lru_gates(c, wa_ref, wx_ref, pv):
    c16 = c.astype(BF16)
    pre_r = jnp.concatenate(
        [_dot(c16[:, n * LRU_BLOCK:(n + 1) * LRU_BLOCK], wa_ref[n], NN) for n in range(LRU_BLOCKS)], axis=1)
    pre_i = jnp.concatenate(
        [_dot(c16[:, n * LRU_BLOCK:(n + 1) * LRU_BLOCK], wx_ref[n], NN) for n in range(LRU_BLOCKS)], axis=1)
    r = _sigmoid(pre_r + pv[5:6])
    ig = _sigmoid(pre_i + pv[6:7])
    lam = pv[7:8]
    ls = jnp.minimum(lam, 0.0) - _log1p(jnp.exp(-jnp.abs(lam)))
    la = LRU_C * r * ls
    a = jnp.exp(la)
    a2 = a * a
    m = jnp.sqrt(jnp.tanh(-la) * (a2 + 1.0))
    return c16, r, ig, ls, la, a, a2, m


def _conv(ext_ref, x, pv, t):
    return (pv[4:5] + pv[3:4] * x + pv[2:3] * ext_ref[pl.ds(HALO - 1, t), :]
            + pv[1:2] * ext_ref[pl.ds(HALO - 2, t), :] + pv[0:1] * ext_ref[pl.ds(HALO - 3, t), :])


def _rglru_fwd(h0a, pvec, wa16, wx16):
    s = h0a.shape[0]
    w = D_MODEL
    t = min(LRU_T, s)
    assert s % t == 0

    def body(ax_ref, ag_ref, pv_ref, wa_ref, wx_ref, ya_ref, h_ref, ext_ref, a_ref, u_ref, hc_ref):
        i = pl.program_id(0)

        @pl.when(i == 0)
        def _():
            ext_ref[pl.ds(0, HALO), :] = jnp.zeros((HALO, w), F32)
            hc_ref[...] = jnp.zeros((1, w), F32)

        pv = pv_ref[...]
        ax = ax_ref[...]
        ext_ref[pl.ds(HALO, t), :] = ax
        c = _conv(ext_ref, ax, pv, t)
        ext_ref[pl.ds(0, HALO), :] = ax[t - HALO:, :]
        _, _, ig, _, _, a, _, m = _lru_gates(c, wa_ref, wx_ref, pv)
        a_ref[...] = a
        u_ref[...] = m * (ig * c)

        def steps(k, h):
            base = pl.multiple_of(k * HALO, HALO)
            for r in range(HALO):
                h = a_ref[pl.ds(base + r, 1), :] * h + u_ref[pl.ds(base + r, 1), :]
                h_ref[pl.ds(base + r, 1), :] = h
            return h

        hc_ref[...] = lax.fori_loop(0, t // HALO, steps, hc_ref[...])
        ag = ag_ref[...]
        ya_ref[...] = (h_ref[...] * (ag * _sigmoid(ag))).astype(BF16)

    full = lambda shp: pl.BlockSpec(shp, lambda i: (0,) * len(shp))
    return pl.pallas_call(
        body, name="rglru_fwd", grid=(s // t,),
        in_specs=[pl.BlockSpec((t, w), lambda i: (i, 0)), pl.BlockSpec((t, w), lambda i: (i, 1)),
                  full((8, w)), full((LRU_BLOCKS, LRU_BLOCK, LRU_BLOCK)), full((LRU_BLOCKS, LRU_BLOCK, LRU_BLOCK))],
        out_specs=[pl.BlockSpec((t, w), lambda i: (i, 0)), pl.BlockSpec((t, w), lambda i: (i, 0))],
        out_shape=[jax.ShapeDtypeStruct((s, w), BF16), jax.ShapeDtypeStruct((s, w), F32)],
        scratch_shapes=[pltpu.VMEM((t + HALO, w), F32), pltpu.VMEM((t, w), F32), pltpu.VMEM((t, w), F32),
                        pltpu.VMEM((1, w), F32)],
        compiler_params=_cparams(("arbitrary",)),
    )(h0a, h0a, pvec, wa16, wx16)


def _rglru_bwd(h0a, h, dh, pvec, wa16, wx16):
    s = h0a.shape[0]
    w = D_MODEL
    t = min(LRU_T, s)
    nb = s // t
    hb = t // HALO

    def body(ax_ref, axh_ref, h_ref, hh_ref, dh_ref, pv_ref, wa_ref, wx_ref,
             dax_ref, dwa_ref, dwx_ref, dpv_ref, ext_ref, hext_ref, dcext_ref, a_ref, g_ref, gc_ref):
        i = pl.program_id(0)
        blk = nb - 1 - i

        @pl.when(i == 0)
        def _():
            dwa_ref[...] = jnp.zeros_like(dwa_ref)
            dwx_ref[...] = jnp.zeros_like(dwx_ref)
            dpv_ref[...] = jnp.zeros_like(dpv_ref)
            gc_ref[...] = jnp.zeros((1, w), F32)
            dcext_ref[pl.ds(t, HALO), :] = jnp.zeros((HALO, w), F32)

        pv = pv_ref[...]
        ax = ax_ref[...]
        keep = (blk > 0).astype(F32)
        ext_ref[pl.ds(0, HALO), :] = axh_ref[...] * keep
        ext_ref[pl.ds(HALO, t), :] = ax
        hext_ref[pl.ds(0, HALO), :] = hh_ref[...] * keep
        hext_ref[pl.ds(HALO, t), :] = h_ref[...]
        c = _conv(ext_ref, ax, pv, t)
        c16, r, ig, ls, _, a, a2, m = _lru_gates(c, wa_ref, wx_ref, pv)
        a_ref[...] = a

        def steps(k, carry):
            base = pl.multiple_of(t - HALO - k * HALO, HALO)
            for r in reversed(range(HALO)):
                g = dh_ref[pl.ds(base + r, 1), :] + carry
                g_ref[pl.ds(base + r, 1), :] = g
                carry = a_ref[pl.ds(base + r, 1), :] * g
            return carry

        gc_ref[...] = lax.fori_loop(0, t // HALO, steps, gc_ref[...])
        g = g_ref[...]
        hprev = hext_ref[pl.ds(HALO - 1, t), :]
        gm = g * m
        d_la = g * hprev * a - (g * ig * c) * a2 / m
        d_pr = d_la * (LRU_C * ls) * r * (1.0 - r)
        d_pi = gm * c * ig * (1.0 - ig)
        dc = gm * ig
        dpr16 = d_pr.astype(BF16)
        dpi16 = d_pi.astype(BF16)
        dc_parts = []
        for n in range(LRU_BLOCKS):
            sl = slice(n * LRU_BLOCK, (n + 1) * LRU_BLOCK)
            dwa_ref[n] += _dot(c16[:, sl], dpr16[:, sl], TN)
            dwx_ref[n] += _dot(c16[:, sl], dpi16[:, sl], TN)
            dc_parts.append(_dot(dpr16[:, sl], wa_ref[n], NT) + _dot(dpi16[:, sl], wx_ref[n], NT))
        dc = dc + jnp.concatenate(dc_parts, axis=1)
        dcext_ref[pl.ds(0, t), :] = dc
        dax = (pv[3:4] * dc + pv[2:3] * dcext_ref[pl.ds(1, t), :] + pv[1:2] * dcext_ref[pl.ds(2, t), :]
               + pv[0:1] * dcext_ref[pl.ds(3, t), :])
        dax_ref[...] = dax.astype(BF16)
        dcext_ref[pl.ds(t, HALO), :] = dc[:HALO, :]
        sums = [_colsum(dc * ext_ref[pl.ds(HALO - 3 + kk, t), :]) for kk in range(4)]
        sums += [_colsum(dc), _colsum(d_pr), _colsum(d_pi), _colsum(d_la * (LRU_C * r))]
        for kk, part in enumerate(sums):
            dpv_ref[pl.ds(kk, 1), :] += part

        @pl.when(i == nb - 1)
        def _():
            lam = pv[7:8]
            dls = dpv_ref[pl.ds(7, 1), :]
            dpv_ref[pl.ds(7, 1), :] = dls * _sigmoid(-lam)

    full = lambda shp: pl.BlockSpec(shp, lambda i: (0,) * len(shp))
    rev = lambda cb: pl.BlockSpec((t, w), functools.partial(lambda i, cb: (nb - 1 - i, cb), cb=cb))
    halo = lambda cb: pl.BlockSpec(
        (HALO, w), functools.partial(lambda i, cb: (jnp.maximum((nb - 1 - i) * hb - 1, 0), cb), cb=cb))
    gw = (LRU_BLOCKS, LRU_BLOCK, LRU_BLOCK)
    return pl.pallas_call(
        body, name="rglru_bwd", grid=(nb,),
        in_specs=[rev(0), halo(0), rev(0), halo(0), rev(0), full((8, w)), full(gw), full(gw)],
        out_specs=[rev(0), full(gw), full(gw), full((8, w))],
        out_shape=[jax.ShapeDtypeStruct((s, w), BF16), jax.ShapeDtypeStruct(gw, F32),
                   jax.ShapeDtypeStruct(gw, F32), jax.ShapeDtypeStruct((8, w), F32)],
        scratch_shapes=[pltpu.VMEM((t + HALO, w), F32), pltpu.VMEM((t + HALO, w), F32),
                        pltpu.VMEM((t + HALO, w), F32), pltpu.VMEM((t, w), F32), pltpu.VMEM((t, w), F32),
                        pltpu.VMEM((1, w), F32)],
        compiler_params=_cparams(("arbitrary",)),
    )(h0a, h0a, h, h, dh, pvec, wa16, wx16)


SB_T = 256
SB_SUB = 4


def _split16(x):
    hi = x.astype(BF16)
    lo = (x - hi.astype(F32)).astype(BF16)
    return jnp.concatenate([hi, lo], axis=0)


def _sb_tile(q, k, scale, tri, run, causal):
    tq = q.shape[0]
    z = _dot(q, k, NT) * scale
    log1mb = -(jnp.maximum(z, 0.0) + jnp.log(1.0 + jnp.exp(-jnp.abs(z))))
    if causal is not None:
        log1mb = jnp.where(causal, log1mb, 0.0)
    cs = _dot(_split16(log1mb), tri, NN)
    cs = cs[:tq] + cs[tq:]
    wgt = jnp.exp(z + cs + run)
    if causal is not None:
        wgt = jnp.where(causal, wgt, 0.0)
    return z, wgt, run + cs[:, 0:1]


SB_DEAD = -105.0


def _sb_alive(run):
    return (jnp.max(run) > SB_DEAD).astype(jnp.int32)


def _sb_more(carry):
    return (carry[0] >= 0) & (carry[1] > 0)


def _sb_fwd(qkv, bg):
    s = qkv.shape[0]
    t = min(SB_T, s)
    nq = s // t
    dh = SB_HEAD_DIM
    scale = 1.0 / math.sqrt(dh)

    sub = SB_SUB if nq % SB_SUB == 0 else 1

    def body(q_ref, k_ref, v_ref, bg_ref, o_ref, y_ref):
        row = lax.broadcasted_iota(jnp.int32, (t, t), 0)
        col = lax.broadcasted_iota(jnp.int32, (t, t), 1)
        tri = jnp.where(row >= col, 1.0, 0.0).astype(BF16)
        causal = col < row
        def tile(q, kb, run, mask):
            off = pl.multiple_of(kb * t, t)
            _, wgt, run = _sb_tile(q, k_ref[pl.ds(off, t), :], scale, tri, run, mask)
            return run, _dot(wgt.astype(BF16), v_ref[pl.ds(off, t), :], NN)

        blocks = []
        for u in range(sub):
            iq = pl.program_id(1) * sub + u
            q = q_ref[pl.ds(u * t, t), :]
            run_d, acc_d = tile(q, iq, jnp.zeros((t, 1), F32), causal)
            run_l, acc_l = tile(q, jnp.maximum(iq - 1, 0), run_d, None)
            blocks.append((iq, q, jnp.where(iq > 0, run_l, run_d), acc_d + jnp.where(iq > 0, acc_l, 0.0)))
        for u, (iq, q, run, acc) in enumerate(blocks):
            def loop(carry, q=q):
                kb, _, run, acc = carry
                run, part = tile(q, kb, run, None)
                return kb - 1, _sb_alive(run), run, acc + part

            _, _, run, acc = lax.while_loop(_sb_more, loop, (iq - 2, _sb_alive(run), run, acc))
            rows = pl.ds(u * t, t)
            o_ref[rows, :] = acc
            g = bg_ref[rows, :]
            y_ref[rows, :] = (acc * (g * _sigmoid(g))).astype(BF16)

    blk = lambda off: pl.BlockSpec((sub * t, dh), functools.partial(lambda h, i, off: (i, h + off), off=off))
    col = lambda off: pl.BlockSpec((s, dh), functools.partial(lambda h, i, off: (0, h + off), off=off))
    return pl.pallas_call(
        body, name="sb_fwd", grid=(SB_HEADS, nq // sub),
        in_specs=[blk(0), col(SB_HEADS), col(2 * SB_HEADS), blk(0)],
        out_specs=[blk(0), blk(0)],
        out_shape=[jax.ShapeDtypeStruct((s, SB_HEADS * dh), F32), jax.ShapeDtypeStruct((s, SB_HEADS * dh), BF16)],
        compiler_params=_cparams(("parallel", "arbitrary")),
    )(qkv, qkv, qkv, bg)


def _sb_bwd(qkv, dob):
    s = qkv.shape[0]
    t = min(SB_T, s)
    nq = s // t
    dh = SB_HEAD_DIM
    scale = 1.0 / math.sqrt(dh)

    sub = SB_SUB if nq % SB_SUB == 0 else 1

    def body(q_ref, k_ref, v_ref, do_ref, dq_ref, dk_out, dv_out, e_ref, b_ref, pe_ref, pb_ref, dk_ref, dv_ref):
        @pl.when(pl.program_id(1) == 0)
        def _():
            dk_ref[...] = jnp.zeros_like(dk_ref)
            dv_ref[...] = jnp.zeros_like(dv_ref)

        row = lax.broadcasted_iota(jnp.int32, (t, t), 0)
        col = lax.broadcasted_iota(jnp.int32, (t, t), 1)
        tri = jnp.where(row >= col, 1.0, 0.0).astype(BF16)
        tri_x = jnp.where(row < col, 1.0, 0.0).astype(BF16)
        causal = col < row

        def sweep1(q, do, kb, e_dst, b_dst, run, mask, live=None):
            off = pl.multiple_of(kb * t, t)
            v = v_ref[pl.ds(off, t), :]
            z, wgt, run = _sb_tile(q, k_ref[pl.ds(off, t), :], scale, tri, run, mask)
            if live is not None:
                wgt = jnp.where(live, wgt, 0.0)
            e_dst[...] = wgt * _dot(do, v, NT)
            b_dst[...] = _sigmoid(z)
            dv_ref[pl.ds(off, t), :] += _dot(wgt.astype(BF16), do, TN)
            return run

        def sweep2(q, kb, e_src, b_src, pre, mask, live=None):
            off = pl.multiple_of(kb * t, t)
            e = e_src[...]
            beta = b_src[...]
            ps = _dot(_split16(e), tri_x, NN)
            ps = ps[:t] + ps[t:]
            dz = e * (1.0 - beta) - beta * (ps + pre)
            if mask is not None:
                dz = jnp.where(mask, dz, 0.0)
            if live is not None:
                dz = jnp.where(live, dz, 0.0)
            dz16 = (dz * scale).astype(BF16)
            dk_ref[pl.ds(off, t), :] += _dot(dz16, q, TN)
            return pre + ps[:, t - 1:t] + e[:, t - 1:t], _dot(dz16, k_ref[pl.ds(off, t), :], NN)

        blocks = []
        for u in range(sub):
            iq = pl.program_id(1) * sub + u
            q, do = q_ref[pl.ds(u * t, t), :], do_ref[pl.ds(u * t, t), :]
            left = jnp.maximum(iq - 1, 0)
            run_d = sweep1(q, do, iq, pe_ref.at[u, 0], pb_ref.at[u, 0], jnp.zeros((t, 1), F32), causal)
            run_l = sweep1(q, do, left, pe_ref.at[u, 1], pb_ref.at[u, 1], run_d, None, iq > 0)
            blocks.append((iq, q, do, left, jnp.where(iq > 0, run_l, run_d)))
        carried = []
        for iq, q, do, left, run in blocks:
            def loop1(carry, q=q, do=do):
                run = sweep1(q, do, carry[0], e_ref.at[carry[0]], b_ref.at[carry[0]], carry[2], None)
                return carry[0] - 1, _sb_alive(run), run

            first = lax.while_loop(_sb_more, loop1, (iq - 2, _sb_alive(run), run))[0] + 1

            def loop2(kb, carry, q=q):
                pre, part = sweep2(q, kb, e_ref.at[kb], b_ref.at[kb], carry[0], None)
                return pre, carry[1] + part

            carried.append(lax.fori_loop(first, iq - 1, loop2, (jnp.zeros((t, 1), F32), jnp.zeros((t, dh), F32))))
        for u, ((iq, q, do, left, _), (pre, dq)) in enumerate(zip(blocks, carried)):
            pre, dq_l = sweep2(q, left, pe_ref.at[u, 1], pb_ref.at[u, 1], pre, None, iq > 0)
            _, dq_d = sweep2(q, iq, pe_ref.at[u, 0], pb_ref.at[u, 0], pre, causal)
            dq_ref[pl.ds(u * t, t), :] = (dq + dq_l + dq_d).astype(BF16)

        @pl.when(pl.program_id(1) == nq // sub - 1)
        def _():
            dk_out[...] = dk_ref[...].astype(BF16)
            dv_out[...] = dv_ref[...].astype(BF16)

    blk = lambda off: pl.BlockSpec((sub * t, dh), functools.partial(lambda h, i, off: (i, h + off), off=off))
    col = lambda off: pl.BlockSpec((s, dh), functools.partial(lambda h, i, off: (0, h + off), off=off))
    wide = SB_HEADS * dh
    return pl.pallas_call(
        body, name="sb_bwd", grid=(SB_HEADS, nq // sub),
        in_specs=[blk(0), col(SB_HEADS), col(2 * SB_HEADS), blk(0)],
        out_specs=[blk(0), col(0), col(0)],
        out_shape=[jax.ShapeDtypeStruct((s, wide), BF16)] * 3,
        scratch_shapes=[pltpu.VMEM((nq, t, t), F32), pltpu.VMEM((nq, t, t), F32),
                        pltpu.VMEM((sub, 2, t, t), F32), pltpu.VMEM((sub, 2, t, t), F32),
                        pltpu.VMEM((s, dh), F32), pltpu.VMEM((s, dh), F32)],
        compiler_params=_cparams(("parallel", "arbitrary")),
    )(qkv, qkv, qkv, dob)


def _alibi_slope(h):
    return float(2.0 ** (-8.0 * (h + 1) / C_HEADS))


GROUP_ROWS = C_GROUP * WINDOW


def _swa_window(n):
    qb = WINDOW
    i = lax.broadcasted_iota(jnp.int32, (qb, 2 * qb), 0)
    j = lax.broadcasted_iota(jnp.int32, (qb, 2 * qb), 1)
    d = i - j + qb
    valid = (d >= 0) & (d < WINDOW) & ((j >= qb) | (n > 0))
    stacked = lambda x: jnp.concatenate([x] * C_GROUP, axis=0)
    return stacked(d.astype(F32)), stacked(jnp.where(valid, 0.0, -1e30))


def _swa_group_cols(c, sink_ref):
    head = lax.shift_right_logical(lax.broadcasted_iota(jnp.int32, (GROUP_ROWS, 1), 0), WINDOW.bit_length() - 1)
    slope = jnp.zeros((GROUP_ROWS, 1), F32)
    sink = jnp.zeros((GROUP_ROWS, 1), F32)
    for hh in range(C_GROUP):
        slope = jnp.where(head == hh, _alibi_slope(c * C_GROUP + hh), slope)
        sink = jnp.where(head == hh, sink_ref[c * C_GROUP + hh], sink)
    return slope, sink


def _swa_probs(qg, kw, sink, slope, dist, bias, scale):
    sc = _dot(qg, kw, NT) * scale - slope * dist + bias
    m = jnp.maximum(jnp.max(sc, axis=-1, keepdims=True), sink)
    p = jnp.exp(sc - m)
    ps = jnp.exp(sink - m)
    inv = 1.0 / (jnp.sum(p, axis=-1, keepdims=True) + ps)
    return p * inv, ps * inv


def _stack_heads(x, c):
    hd = C_HEAD_DIM
    return jnp.concatenate([x[:, (c * C_GROUP + hh) * hd:(c * C_GROUP + hh + 1) * hd] for hh in range(C_GROUP)], axis=0)


SWA_SUB = 2


def _swa_kv_specs(sub, kvw):
    return [pl.BlockSpec((WINDOW, 2 * kvw), functools.partial(lambda n, w: (n * sub + w, 0), w=w)) for w in range(sub + 1)]


def _swa_fwd(q16, gate, kvp, sinks):
    s = q16.shape[0]
    qb = WINDOW
    hd = C_HEAD_DIM
    scale = 1.0 / math.sqrt(hd)
    kvw = C_KV_HEADS * hd
    sub = SWA_SUB if (s // qb) % SWA_SUB == 0 else 1

    def body(sink_ref, q_ref, g_ref, *refs):
        kv_refs, (o_ref, y_ref) = refs[:sub + 1], refs[sub + 1:]
        for u in range(sub):
            rows = slice(u * qb, (u + 1) * qb)
            q = q_ref[rows, :]
            kv = jnp.concatenate([kv_refs[u][...], kv_refs[u + 1][...]], axis=0)
            dist, bias = _swa_window(pl.program_id(0) * sub + u)
            for c in range(C_KV_HEADS):
                slope, sink = _swa_group_cols(c, sink_ref)
                kw = kv[:, c * hd:(c + 1) * hd]
                vw = kv[:, kvw + c * hd:kvw + (c + 1) * hd]
                p, _ = _swa_probs(_stack_heads(q, c), kw, sink, slope, dist, bias, scale)
                og = _dot(p.astype(BF16), vw, NN)
                for hh in range(C_GROUP):
                    h = c * C_GROUP + hh
                    o_ref[rows, h * hd:(h + 1) * hd] = og[hh * qb:(hh + 1) * qb, :]
        g = g_ref[...]
        y_ref[...] = (o_ref[...] * (g * _sigmoid(g))).astype(BF16)

    wide = C_HEADS * hd
    big = pl.BlockSpec((sub * qb, wide), lambda n: (n, 0))
    return pl.pallas_call(
        body, name="swa_fwd", grid=(s // (sub * qb),),
        in_specs=[pl.BlockSpec(memory_space=pltpu.SMEM), big, big] + _swa_kv_specs(sub, kvw),
        out_specs=[big, big],
        out_shape=[jax.ShapeDtypeStruct((s, wide), F32), jax.ShapeDtypeStruct((s, wide), BF16)],
        compiler_params=_cparams(("arbitrary",)),
    )(sinks, q16, gate, *([kvp] * (sub + 1)))


def _swa_bwd(q16, do16, kvp, sinks):
    s = q16.shape[0]
    qb = WINDOW
    hd = C_HEAD_DIM
    scale = 1.0 / math.sqrt(hd)
    kvw = C_KV_HEADS * hd
    sub = 1
    nstep = s // (sub * qb)

    def body(sink_ref, q_ref, do_ref, *refs):
        kv_refs, (dq_ref, dkv_ref, ds_ref, sacc_ref) = refs[:sub + 1], refs[sub + 1:]

        @pl.when(pl.program_id(0) == 0)
        def _():
            dkv_ref[...] = jnp.zeros_like(dkv_ref)
            sacc_ref[...] = jnp.zeros_like(sacc_ref)

        for u in range(sub):
            rows = slice(u * qb, (u + 1) * qb)
            n = pl.program_id(0) * sub + u
            q = q_ref[rows, :]
            do = do_ref[rows, :]
            kv = jnp.concatenate([kv_refs[u][...], kv_refs[u + 1][...]], axis=0)
            off = pl.multiple_of(n * qb, qb)
            dist, bias = _swa_window(n)
            for c in range(C_KV_HEADS):
                slope, sink = _swa_group_cols(c, sink_ref)
                kw = kv[:, c * hd:(c + 1) * hd]
                vw = kv[:, kvw + c * hd:kvw + (c + 1) * hd]
                qg = _stack_heads(q, c)
                dog = _stack_heads(do, c)
                p, ps = _swa_probs(qg, kw, sink, slope, dist, bias, scale)
                dp = _dot(dog, vw, NT)
                dd = jnp.sum(p * dp, axis=-1, keepdims=True)
                ds16 = (p * (dp - dd) * scale).astype(BF16)
                sacc_ref[c] += ps * dd
                dqg = _dot(ds16, kw, NN).astype(BF16)
                for hh in range(C_GROUP):
                    h = c * C_GROUP + hh
                    dq_ref[rows, h * hd:(h + 1) * hd] = dqg[hh * qb:(hh + 1) * qb, :]
                dkv_ref[pl.ds(off, 2 * qb), c * hd:(c + 1) * hd] += _dot(ds16, qg, TN)
                dkv_ref[pl.ds(off, 2 * qb), kvw + c * hd:kvw + (c + 1) * hd] += _dot(p.astype(BF16), dog, TN)

        @pl.when(pl.program_id(0) == nstep - 1)
        def _():
            lane = lax.broadcasted_iota(jnp.int32, (1, 128), 1)
            row = jnp.zeros((1, 128), F32)
            for c in range(C_KV_HEADS):
                for hh in range(C_GROUP):
                    tot = jnp.sum(sacc_ref[c, pl.ds(hh * qb, qb), :], axis=0, keepdims=True)
                    row = jnp.where(lane == c * C_GROUP + hh, -tot, row)
            ds_ref[...] = row

    wide = C_HEADS * hd
    big = pl.BlockSpec((sub * qb, wide), lambda n: (n, 0))
    return pl.pallas_call(
        body, name="swa_bwd", grid=(nstep,),
        in_specs=[pl.BlockSpec(memory_space=pltpu.SMEM), big, big] + _swa_kv_specs(sub, kvw),
        out_specs=[big,
                   pl.BlockSpec((s + qb, 2 * kvw), lambda n: (0, 0)), pl.BlockSpec((1, 128), lambda n: (0, 0))],
        out_shape=[jax.ShapeDtypeStruct((s, wide), BF16), jax.ShapeDtypeStruct((s + qb, 2 * kvw), F32),
                   jax.ShapeDtypeStruct((1, 128), F32)],
        scratch_shapes=[pltpu.VMEM((C_KV_HEADS, GROUP_ROWS, 1), F32)],
        compiler_params=_cparams(("arbitrary",)),
    )(sinks, q16, do16, *([kvp] * (sub + 1)))


def _adamw(parts, w, m, v, *, name, tr=496):
    npart, r, c = parts.shape
    tr = min(tr, r)
    assert r % tr == 0 and tr % 16 == 0
    c1 = 1.0 / (1.0 - ADAM_B1 ** ADAM_STEP)
    c2 = 1.0 / (1.0 - ADAM_B2 ** ADAM_STEP)

    def body(p_ref, w_ref, m_ref, v_ref, g_ref, d_ref, nm_ref, nv_ref):
        g = p_ref[0].astype(F32)
        for j in range(1, npart):
            g = g + p_ref[j].astype(F32)
        nm = ADAM_B1 * m_ref[...] + (1.0 - ADAM_B1) * g
        nv = ADAM_B2 * v_ref[...] + (1.0 - ADAM_B2) * (g * g)
        g_ref[...] = g
        nm_ref[...] = nm
        nv_ref[...] = nv
        d_ref[...] = -ADAM_LR * ((nm * c1) / (jnp.sqrt(nv * c2) + ADAM_EPS) + ADAM_WD * w_ref[...])

    spec = pl.BlockSpec((tr, c), lambda i: (i, 0))
    return pl.pallas_call(
        body, name=name, grid=(r // tr,),
        in_specs=[pl.BlockSpec((npart, tr, c), lambda i: (0, i, 0)), spec, spec, spec],
        out_specs=[spec] * 4, out_shape=[jax.ShapeDtypeStruct((r, c), F32)] * 4,
        compiler_params=_cparams(("parallel",)),
    )(parts, w, m, v)


GATHER = "gather"
A2A = "a2a"
GATHER_COLS = "gather_cols"
A2A_COLS = "a2a_cols"


def _exchange(bufs, gather, *, name):
    nb = len(bufs)

    def body(*refs):
        _exchange_start(refs[:nb], refs[nb:2 * nb], gather, refs[2 * nb:])
        _exchange_wait(refs[:nb], refs[nb:2 * nb], gather, refs[2 * nb:])

    hbm = pl.BlockSpec(memory_space=pl.ANY)
    return pl.pallas_call(
        body, name=name, in_specs=[hbm] * nb, out_specs=[hbm] * nb, out_shape=_exchange_out_shapes(bufs, gather),
        scratch_shapes=_exchange_sems(nb),
        compiler_params=pltpu.CompilerParams(has_side_effects=True),
    )(*bufs)


def _gather_two_level(bufs, modes, *, name):
    nb = len(bufs)

    def body(*refs):
        src, dst = refs[:nb], refs[nb:2 * nb]
        send_sems, recv_sems, local_sems = refs[2 * nb:]
        x, y, c = lax.axis_index("x"), lax.axis_index("y"), lax.axis_index("c")
        me = 4 * x + 2 * y + c
        here, sibling = (x, y, c), (x, y, 1 - c)

        def copy(b, k, origin, to, source):
            return pltpu.make_async_remote_copy(
                src_ref=source, dst_ref=_slot(dst[b], modes[b], origin), send_sem=send_sems.at[b, k - 1],
                recv_sem=recv_sems.at[b, k - 1], device_id=to, device_id_type=pl.DeviceIdType.MESH)

        local = [pltpu.make_async_copy(src[b], _slot(dst[b], modes[b], me), local_sems.at[b]) for b in range(nb)]
        sends = [copy(b, k, me, (x ^ (k >> 2), y ^ ((k >> 1) & 1), c ^ (k & 1)), src[b])
                 for k in (1, 2, 4, 6) for b in range(nb)]
        for cp in local + sends:
            cp.start()
        for j in (2, 4, 6):
            for b in range(nb):
                copy(b, j, me ^ j, here, src[b]).wait_recv()
                passed = copy(b, j ^ 1, me ^ j, sibling, _slot(dst[b], modes[b], me ^ j))
                passed.start()
                sends.append(passed)
        for k in (1, 3, 5, 7):
            for b in range(nb):
                copy(b, k, me ^ k, here, src[b]).wait_recv()
        for cp in sends:
            cp.wait_send()
        for cp in local:
            cp.wait()

    hbm = pl.BlockSpec(memory_space=pl.ANY)
    return pl.pallas_call(
        body, name=name, in_specs=[hbm] * nb, out_specs=[hbm] * nb, out_shape=_exchange_out_shapes(bufs, modes),
        scratch_shapes=_exchange_sems(nb),
        compiler_params=pltpu.CompilerParams(has_side_effects=True),
    )(*bufs)


def _exchange_out_shapes(bufs, gather):
    shapes = {GATHER: lambda s: (N_DEV,) + s, A2A: lambda s: s, GATHER_COLS: lambda s: (s[0], N_DEV * s[1]),
              A2A_COLS: lambda s: (N_DEV, s[0], s[1] // N_DEV)}
    return [jax.ShapeDtypeStruct(shapes[g](tuple(b.shape)), b.dtype) for b, g in zip(bufs, gather)]


def _sent(src, mode, peer):
    if mode == A2A:
        return src.at[peer]
    if mode == A2A_COLS:
        w = src.shape[1] // N_DEV
        return src.at[:, pl.ds(pl.multiple_of(peer * w, 128), w)]
    return src


def _slot(dst, mode, dev):
    if mode == GATHER_COLS:
        w = dst.shape[1] // N_DEV
        return dst.at[:, pl.ds(pl.multiple_of(dev * w, 128), w)]
    return dst.at[dev]


def _exchange_sems(nb):
    return [pltpu.SemaphoreType.DMA((nb, N_DEV - 1)), pltpu.SemaphoreType.DMA((nb, N_DEV - 1)),
            pltpu.SemaphoreType.DMA((nb,))]


def _exchange_copies(src, dst, gather, sems):
    send_sems, recv_sems, local_sems = sems
    x, y, c = lax.axis_index("x"), lax.axis_index("y"), lax.axis_index("c")
    me = 4 * x + 2 * y + c
    local, sends, recvs = [], [], []
    for b in range(len(src)):
        mine = _sent(src[b], gather[b], me)
        local.append(pltpu.make_async_copy(mine, _slot(dst[b], gather[b], me), local_sems.at[b]))
        for k in range(1, N_DEV):
            px, py, pc = x ^ (k >> 2), y ^ ((k >> 1) & 1), c ^ (k & 1)
            peer = 4 * px + 2 * py + pc
            pair = dict(send_sem=send_sems.at[b, k - 1], recv_sem=recv_sems.at[b, k - 1],
                        device_id_type=pl.DeviceIdType.MESH)
            sends.append(pltpu.make_async_remote_copy(
                src_ref=_sent(src[b], gather[b], peer), dst_ref=_slot(dst[b], gather[b], me), device_id=(px, py, pc),
                **pair))
            recvs.append(pltpu.make_async_remote_copy(
                src_ref=mine, dst_ref=_slot(dst[b], gather[b], peer), device_id=(x, y, c), **pair))
    return local, sends, recvs


def _exchange_start(src, dst, gather, sems):
    local, sends, _ = _exchange_copies(src, dst, gather, sems)
    for cp in local + sends:
        cp.start()


def _exchange_wait(src, dst, gather, sems):
    local, sends, recvs = _exchange_copies(src, dst, gather, sems)
    for cp in sends:
        cp.wait_send()
    for cp in recvs:
        cp.wait_recv()
    for cp in local:
        cp.wait()


R_SMALL = 16
REPL_ROWS = 272
REPL_LEN = REPL_ROWS * D_MODEL


def _small_block(parts):
    flat = jnp.concatenate(parts, axis=-1)
    lead = flat.ndim - 1
    return jnp.pad(flat[..., None, :], [(0, 0)] * lead + [(0, R_SMALL - 1), (0, D_MODEL - flat.shape[-1])])


def _small_vectors(conv_w, ln_g, ln_b):
    return _small_block([conv_w.reshape(512), ln_g.reshape(128), ln_b.reshape(128)])


REPL_SHAPES = ((1, 8, 128, 128), (1, 8, 128, 128), (1, 1024), (1, 1024), (1, 1024), (1, 1024), (1, 1024), (1, 1024),
               (1, 16))


def _pack_repl(parts):
    flat = jnp.concatenate([p.reshape(-1) for p in parts])
    return jnp.concatenate([flat, jnp.zeros((REPL_LEN - flat.shape[0],), F32)]).reshape(REPL_ROWS, D_MODEL)


def _unpack_repl(p):
    flat = p.reshape(-1)
    out, o = [], 0
    for shp in REPL_SHAPES:
        n = math.prod(shp)
        out.append(flat[o:o + n].reshape(shp))
        o += n
    return out


def kernel(x, e_w_in, e_conv_w, e_conv_b, e_w_gate_a, e_b_gate_a, e_w_gate_x, e_b_gate_x, e_lru_lambda, e_w_out, e_ln_g, e_ln_b, o_w_in, o_sinks, o_w_out, o_ln_g, o_ln_b, loss_target, m_e_w_in, m_e_conv_w, m_e_conv_b, m_e_w_gate_a, m_e_b_gate_a, m_e_w_gate_x, m_e_b_gate_x, m_e_lru_lambda, m_e_w_out, m_e_ln_g, m_e_ln_b, m_o_w_in, m_o_sinks, m_o_w_out, m_o_ln_g, m_o_ln_b, v_e_w_in, v_e_conv_w, v_e_conv_b, v_e_w_gate_a, v_e_b_gate_a, v_e_w_gate_x, v_e_b_gate_x, v_e_lru_lambda, v_e_w_out, v_e_ln_g, v_e_ln_b, v_o_w_in, v_o_sinks, v_o_w_out, v_o_ln_g, v_o_ln_b):
    d = D_MODEL
    x0 = x[0]
    target = loss_target[0]
    s = x0.shape[0]

    spack = _small_block([e_conv_w.reshape(512), o_ln_g.reshape(128), o_ln_b.reshape(128)])
    wi_e, sall = _gather_two_level([e_w_in[0].astype(BF16), spack], [GATHER_COLS, GATHER], name="gather_w_in")
    conv_w = sall[:, 0, 0:512].reshape(N_DEV, 4, 128).transpose(1, 0, 2).reshape(4, d)
    ln_g_o = sall[:, 0, 512:640].reshape(1, d)
    ln_b_o = sall[:, 0, 640:768].reshape(1, d)
    pvec = jnp.concatenate([conv_w, e_conv_b, e_b_gate_a, e_b_gate_x, e_lru_lambda], axis=0)
    wa16 = e_w_gate_a[0].astype(BF16)
    wx16 = e_w_gate_x[0].astype(BF16)
    sinks = o_sinks[0]

    x0_16 = x0.astype(BF16)
    h0a, wall_out_e = _matmul(x0_16, wi_e, mode="nn", n_out=2 * d, b_off=0, name="l0_in_a",
                              comm=([e_w_out[0].astype(BF16)], [GATHER]))
    qkv, wall_in_o = _matmul(x0_16, wi_e, mode="nn", n_out=3 * d, b_off=2 * d, out_dtype=BF16, name="l0_in_qkv",
                             comm=([o_w_in[0].astype(BF16)], [GATHER]))
    bg, wall_out_o = _matmul(x0_16, wi_e, mode="nn", n_out=d, b_off=5 * d, name="l0_in_bg",
                             comm=([o_w_out[0].astype(BF16)], [GATHER]))
    wo_e = wall_out_e.reshape(2 * d, d)
    wi_o = wall_in_o.transpose(1, 0, 2).reshape(d, 2304)
    wi_o = jnp.concatenate([wi_o[:, :1024], wi_o[:, 1280:], wi_o[:, 1024:1280]], axis=1)
    wo_o = wall_out_o.reshape(d, d)
    ya, hst = _rglru_fwd(h0a, pvec, wa16, wx16)
    ob, yb = _sb_fwd(qkv, bg)
    z0, x1, x1_16 = _matmul([ya, yb], wo_e, mode="nn", tm=512, name="l0_out",
                            epi=(_epi_ln_fwd, [(x0, 0)], [e_ln_g, e_ln_b], [F32, F32, BF16], 0))

    q1 = _matmul(x1_16, wi_o, mode="nn", n_out=d, b_off=0, out_dtype=BF16, name="l1_in_q")
    g1 = _matmul(x1_16, wi_o, mode="nn", n_out=d, b_off=d, name="l1_in_g")
    kv1 = _matmul(x1_16, wi_o, mode="nn", n_out=256, b_off=2 * d, tn=256, out_dtype=BF16, name="l1_in_kv")
    kvp = jnp.concatenate([jnp.zeros((WINDOW, 256), BF16), kv1], axis=0)
    o1, y1 = _swa_fwd(q1, g1, kvp, sinks)
    dz1, dz1_16, loss_cols, dg_o, db_o = _matmul(
        y1, wo_o, mode="nn", tm=512, name="l1_out",
        epi=(_epi_ln_loss_bwd, [(x1, 0), (target, 0)], [ln_g_o, ln_b_o], [F32, BF16], 3))
    loss_hi = jnp.sum(loss_cols).astype(BF16).astype(F32)
    loss_terms = jnp.stack([loss_hi, jnp.sum(loss_cols) - loss_hi]).reshape(1, 2)

    dwo_o = _matmul(y1, dz1_16, mode="tn", out_dtype=BF16, name="l1_dwout")
    do1, dg1, parts_out_o = _matmul(dz1_16, wo_o, mode="nt", tm=512, name="l1_dy",
                                    comm=([dwo_o.reshape(N_DEV, 128, d)], [A2A]),
                                    epi=(_epi_gate_bwd, [(o1, 0), (g1, 0)], [], [BF16, BF16], 0))
    dq1, dkvp, dsink = _swa_bwd(q1, do1, kvp, sinks)
    dkv1 = dkvp[WINDOW:].astype(BF16)
    dh1 = [dq1, dg1, dkv1]
    dw_qg = _matmul(x1_16, [dq1, dg1], mode="tn", out_dtype=BF16, name="l1_dwin_qg")
    dw_kv = _matmul(x1_16, dkv1, mode="tn", out_dtype=BF16, name="l1_dwin_kv")
    dwi_o = jnp.concatenate([dw_qg[:, :1024], dw_kv, dw_qg[:, 1024:]], axis=1)
    dz0, dz0_16, dg_e, db_e, parts_in_o = _matmul(
        dh1, wi_o, mode="nt", tm=512, tk=2304, name="l1_dx",
        comm=([dwi_o.reshape(d, N_DEV, 288).transpose(1, 0, 2)], [A2A]),
        epi=(_epi_ln_bwd, [(dz1, 0), (z0, 0)], [e_ln_g], [F32, BF16], 2))

    dwo_e = _matmul([ya, yb], dz0_16, mode="tn", out_dtype=BF16, name="l0_dwout")
    dhst, dag, parts_out_e = _matmul(
        dz0_16, wo_e, mode="nt", tm=512, n_out=d, b_off=0, name="l0_dy_a",
        comm=([dwo_e.reshape(N_DEV, 256, d)], [A2A]),
        epi=(_epi_gate_bwd, [(hst, 0), (h0a, d)], [], [F32, BF16], 0))
    dob, dbg = _matmul(dz0_16, wo_e, mode="nt", tm=512, n_out=d, b_off=d, name="l0_dy_b",
                       epi=(_epi_gate_bwd, [(ob, 0), (bg, 0)], [], [BF16, BF16], 0))
    dq0, dk0, dv0 = _sb_bwd(qkv, dob)
    dax, dwa, dwx, dpv = _rglru_bwd(h0a, hst, dhst, pvec, wa16, wx16)
    dh0 = [dax, dag, dq0, dk0, dv0, dbg]
    repl = _pack_repl([dwa, dwx, dpv[4:5], dpv[5:6], dpv[6:7], dpv[7:8], dg_e, db_e, dsink[:, :C_HEADS], loss_terms])
    dwi_e, parts_repl = _matmul(x0_16, dh0, mode="tn", out_dtype=BF16, name="l0_dwin",
                                comm=([repl.astype(BF16)], [GATHER]))
    small = _small_block([dpv[0:4].reshape(4, N_DEV, 128).transpose(1, 0, 2).reshape(N_DEV, 512),
                          dg_o.reshape(N_DEV, 128), db_o.reshape(N_DEV, 128)]).astype(BF16)
    grad_x, parts_in_e, parts_small = _matmul(dh0, wi_e, mode="nt", tm=512, res=dz0, alpha=ALPHA, name="l0_dx",
                                              comm=([dwi_e, small], [A2A_COLS, A2A]))

    res_in_e = _adamw(parts_in_e, e_w_in[0], m_e_w_in[0], v_e_w_in[0], tr=512, name="adamw_w_in_e")
    res_out_e = _adamw(parts_out_e, e_w_out[0], m_e_w_out[0], v_e_w_out[0], name="adamw_w_out_e")
    res_in_o = _adamw(parts_in_o, o_w_in[0], m_o_w_in[0], v_o_w_in[0], tr=512, name="adamw_w_in_o")
    res_out_o = _adamw(parts_out_o, o_w_out[0], m_o_w_out[0], v_o_w_out[0], name="adamw_w_out_o")
    res_small = _adamw(parts_small, _small_vectors(e_conv_w, o_ln_g, o_ln_b),
                       _small_vectors(m_e_conv_w, m_o_ln_g, m_o_ln_b),
                       _small_vectors(v_e_conv_w, v_o_ln_g, v_o_ln_b), name="adamw_vectors")

    w_r = _pack_repl([e_w_gate_a, e_w_gate_x, e_conv_b, e_b_gate_a, e_b_gate_x, e_lru_lambda, e_ln_g, e_ln_b, o_sinks])
    m_r = _pack_repl([m_e_w_gate_a, m_e_w_gate_x, m_e_conv_b, m_e_b_gate_a, m_e_b_gate_x, m_e_lru_lambda, m_e_ln_g,
                      m_e_ln_b, m_o_sinks])
    v_r = _pack_repl([v_e_w_gate_a, v_e_w_gate_x, v_e_conv_b, v_e_b_gate_a, v_e_b_gate_x, v_e_lru_lambda, v_e_ln_g,
                      v_e_ln_b, v_o_sinks])
    g_r, d_r, nm_r, nv_r = _adamw(parts_repl, w_r, m_r, v_r, name="adamw_replicated")
    loss_at = sum(math.prod(shp) for shp in REPL_SHAPES)
    loss = g_r[loss_at // d, loss_at % d] + g_r[(loss_at + 1) // d, (loss_at + 1) % d]

    def assemble(i, rp):
        vec = res_small[i][0]
        cw, lg_o, lb_o = vec[0:512].reshape(1, 4, 128), vec[512:640].reshape(1, 128), vec[640:768].reshape(1, 128)
        w_a, w_x, cb, b_a, b_x, lam, lg_e, lb_e, snk = _unpack_repl(rp)
        return [res_in_e[i][None], cw, cb, w_a, b_a, w_x, b_x, lam, res_out_e[i][None], lg_e, lb_e,
                res_in_o[i][None], snk, res_out_o[i][None], lg_o, lb_o]

    return (loss, grad_x[None], *assemble(0, g_r), *assemble(1, d_r), *assemble(2, nm_r), *assemble(3, nv_r))
```

```python
import functools
import math

import jax
import jax.numpy as jnp
from jax import lax
from jax.experimental import pallas as pl
from jax.experimental.pallas import tpu as pltpu

F32 = jnp.float32
BF16 = jnp.bfloat16

N_DEV = 8
D_MODEL = 1024
LRU_BLOCKS = 8
LRU_BLOCK = 128
LRU_C = 8.0
SB_HEADS = 8
SB_HEAD_DIM = 128
C_HEADS = 16
C_KV_HEADS = 2
C_GROUP = 8
C_HEAD_DIM = 64
WINDOW = 128
DEPTH = 2
ALPHA = float((2 * DEPTH) ** 0.25)
LN_EPS = 1e-5
ADAM_LR = 0.001
ADAM_B1 = 0.9
ADAM_B2 = 0.999
ADAM_EPS = 1e-08
ADAM_WD = 0.01
ADAM_STEP = 10

VMEM_LIMIT = 56 * 1024 * 1024

NN = ((1,), (0,))
NT = ((1,), (1,))
TN = ((0,), (0,))


def _dot(a, b, dims):
    return lax.dot_general(a, b, (dims, ((), ())), preferred_element_type=F32)


def _sigmoid(x):
    return 0.5 * jnp.tanh(0.5 * x) + 0.5


def _cparams(sem, vmem=VMEM_LIMIT):
    return pltpu.CompilerParams(dimension_semantics=sem, vmem_limit_bytes=vmem)


def _matmul(a, b, *, mode, n_out=None, b_off=0, out_dtype=F32, res=None, alpha=1.0,
            tm=1024, tn=1024, tk=1024, comm=None, epi=None, name):
    a_list = list(a) if isinstance(a, (list, tuple)) else [a]
    b_list = list(b) if isinstance(b, (list, tuple)) else [b]
    if mode == "tn":
        k = a_list[0].shape[0]
        m = sum(p.shape[1] for p in a_list)
        n = n_out if n_out is not None else sum(p.shape[1] for p in b_list)
        assert all(p.shape[1] == tm for p in a_list) or len(a_list) == 1
        assert all(p.shape[1] == tn for p in b_list) or len(b_list) == 1
    else:
        assert len(b_list) == 1
        m = a_list[0].shape[0]
        k = sum(p.shape[1] for p in a_list)
        n = n_out if n_out is not None else (b.shape[0] if mode == "nt" else b.shape[1])
        if len(a_list) > 1:
            tk = k
    tm, tn, tk = min(tm, m), min(tn, n), min(tk, k)
    assert m % tm == 0 and n % tn == 0 and k % tk == 0 and b_off % tn == 0
    grid = (m // tm, n // tn, k // tk)
    nk = grid[2]
    jo = b_off // tn
    dims = {"nn": NN, "nt": NT, "tn": TN}[mode]
    nc = len(comm[0]) if comm else 0
    if epi:
        assert res is None
        epi_fn, epi_rows, epi_vecs, epi_dtypes, n_sums = epi
        assert n_sums == 0 or grid[1] == 1
    else:
        epi_rows, epi_vecs, epi_dtypes, n_sums = [], [], [out_dtype], 0
    n_ab = len(a_list) + len(b_list)
    n_in = n_ab + (res is not None) + len(epi_rows) + len(epi_vecs)
    n_res = len(epi_dtypes) + n_sums

    def body(*refs):
        a_refs, b_refs = refs[:len(a_list)], refs[len(a_list):n_ab]
        r_ref = refs[n_ab] if res is not None else None
        row_refs = refs[n_in - len(epi_rows) - len(epi_vecs):n_in - len(epi_vecs)]
        vec_refs = refs[n_in - len(epi_vecs):n_in]
        o_refs = refs[n_in + nc:n_in + nc + len(epi_dtypes)]
        sum_refs = refs[n_in + nc + len(epi_dtypes):n_in + nc + n_res]
        if comm:
            step = [pl.program_id(ax) for ax in range(3)]
            c_src = refs[n_in:n_in + nc]
            c_dst = refs[n_in + nc + n_res:n_in + 2 * nc + n_res]
            sems = refs[len(refs) - 3:]

            @pl.when((step[0] == 0) & (step[1] == 0) & (step[2] == 0))
            def _():
                _exchange_start(c_src, c_dst, comm[1], sems)

        def finish(out):
            if not epi:
                if r_ref is not None:
                    out = out + alpha * r_ref[...]
                o_refs[0][...] = out.astype(out_dtype)
                return
            outs, sums = epi_fn(out, [r[...] for r in row_refs], [v[...] for v in vec_refs])
            for o_ref, o in zip(o_refs, outs):
                o_ref[...] = o.astype(o_ref.dtype)
            if sum_refs:
                first_rows = pl.program_id(0) == 0

                @pl.when(first_rows)
                def _():
                    for s_ref, part_sum in zip(sum_refs, sums):
                        s_ref[...] = part_sum

                @pl.when(jnp.logical_not(first_rows))
                def _():
                    for s_ref, part_sum in zip(sum_refs, sums):
                        s_ref[...] += part_sum

        def accumulate(part):
            if nk == 1:
                finish(part)
                return
            acc_ref = refs[n_in + 2 * nc + n_res]
            kk = pl.program_id(2)

            @pl.when(kk == 0)
            def _():
                acc_ref[...] = part

            @pl.when(kk > 0)
            def _():
                acc_ref[...] += part

            @pl.when(kk == nk - 1)
            def _():
                finish(acc_ref[...])

        def product(a_ref, b_val):
            return _dot(a_ref[...].astype(BF16), b_val.astype(BF16), dims)

        if mode != "tn" and len(a_list) > 1:
            part, off = None, 0
            for a_ref in a_refs:
                w = a_ref.shape[1]
                b_val = b_refs[0][off:off + w, :] if mode == "nn" else b_refs[0][:, off:off + w]
                part = product(a_ref, b_val) if part is None else part + product(a_ref, b_val)
                off += w
            accumulate(part)
        elif len(a_list) > 1 or len(b_list) > 1:
            for pa, a_ref in enumerate(a_refs):
                for pb, b_ref in enumerate(b_refs):
                    picked = ([pl.program_id(0) == pa] if len(a_list) > 1 else []) + (
                        [pl.program_id(1) == pb] if len(b_list) > 1 else [])
                    pl.when(functools.reduce(jnp.logical_and, picked))(
                        functools.partial(lambda a_ref, b_ref: accumulate(product(a_ref, b_ref[...])), a_ref, b_ref))
        else:
            accumulate(product(a_refs[0], b_refs[0][...]))

        if comm:
            @pl.when((step[0] == grid[0] - 1) & (step[1] == grid[1] - 1) & (step[2] == grid[2] - 1))
            def _():
                _exchange_wait(c_src, c_dst, comm[1], sems)

    def held(axis, p):
        def index(i, j, kk):
            return (jnp.where((i, j)[axis] == p, kk, 0), 0)
        return index

    if mode == "tn":
        if len(a_list) > 1:
            a_specs = [pl.BlockSpec((tk, tm), held(0, p)) for p in range(len(a_list))]
        else:
            a_specs = [pl.BlockSpec((tk, tm), lambda i, j, kk: (kk, i))]
    elif len(a_list) > 1:
        a_specs = [pl.BlockSpec((tm, p.shape[1]), lambda i, j, kk: (i, 0)) for p in a_list]
    else:
        a_specs = [pl.BlockSpec((tm, tk), lambda i, j, kk: (i, kk))]
    if mode == "nt":
        b_specs = [pl.BlockSpec((tn, tk), lambda i, j, kk: (j + jo, kk))]
    elif len(b_list) > 1:
        b_specs = [pl.BlockSpec((tk, tn), held(1, p)) for p in range(len(b_list))]
    else:
        b_specs = [pl.BlockSpec((tk, tn), lambda i, j, kk: (kk, j + jo))]
    o_spec = pl.BlockSpec((tm, tn), lambda i, j, kk: (i, j))
    hbm = pl.BlockSpec(memory_space=pl.ANY)
    row_specs = [pl.BlockSpec((tm, tn), functools.partial(lambda i, j, kk, o: (i, j + o), o=off // tn))
                 for (_, off) in epi_rows]
    vec_specs = [pl.BlockSpec(v.shape, functools.partial(lambda i, j, kk, nd: (0,) * nd, nd=v.ndim)) for v in epi_vecs]
    in_specs = a_specs + b_specs + ([o_spec] if res is not None else []) + row_specs + vec_specs + [hbm] * nc
    args = (a_list + b_list + ([res] if res is not None else []) + [r for (r, _) in epi_rows] + list(epi_vecs)
            + (list(comm[0]) if comm else []))
    out_specs = [o_spec] * len(epi_dtypes) + [pl.BlockSpec((1, tn), lambda i, j, kk: (0, j))] * n_sums + [hbm] * nc
    out_shape = ([jax.ShapeDtypeStruct((m, n), dt) for dt in epi_dtypes] + [jax.ShapeDtypeStruct((1, n), F32)] * n_sums
                 + (_exchange_out_shapes(*comm) if comm else []))
    scratch = ([pltpu.VMEM((tm, tn), F32)] if nk > 1 else []) + (_exchange_sems(nc) if comm else [])
    if comm or n_sums or n_ab > 2:
        params = pltpu.CompilerParams(dimension_semantics=("arbitrary",) * 3, vmem_limit_bytes=VMEM_LIMIT,
                                      has_side_effects=bool(comm))
    else:
        params = _cparams(("parallel", "parallel", "arbitrary"))
    outs = pl.pallas_call(
        body, name=name, grid=grid, in_specs=in_specs, out_specs=out_specs, out_shape=out_shape,
        scratch_shapes=scratch, compiler_params=params,
    )(*args)
    return tuple(outs) if (comm or epi) else outs[0]


def _ln_stats(z):
    mu = jnp.mean(z, axis=-1, keepdims=True)
    zc = z - mu
    var = jnp.mean(zc * zc, axis=-1, keepdims=True)
    rstd = lax.rsqrt(var + LN_EPS)
    return zc * rstd, rstd


def _ln_bwd(dy, xhat, rstd, g):
    dxh = dy * g
    m1 = jnp.mean(dxh, axis=-1, keepdims=True)
    m2 = jnp.mean(dxh * xhat, axis=-1, keepdims=True)
    return rstd * (dxh - m1 - xhat * m2)


def _colsum(x):
    return jnp.sum(x, axis=0, keepdims=True)


def _epi_ln_fwd(acc, rows, vecs):
    z = ALPHA * rows[0] + acc
    xhat, _ = _ln_stats(z)
    x1 = xhat * vecs[0] + vecs[1]
    return [z, x1, x1], []


def _epi_ln_loss_bwd(acc, rows, vecs):
    inv_c = 1.0 / acc.shape[-1]
    z = ALPHA * rows[0] + acc
    xhat, rstd = _ln_stats(z)
    d = xhat * vecs[0] + vecs[1] - rows[1]
    dy = d * inv_c
    dz = _ln_bwd(dy, xhat, rstd, vecs[0])
    return [dz, dz], [_colsum(0.5 * d * d * inv_c), _colsum(dy * xhat), _colsum(dy)]


def _epi_ln_bwd(acc, rows, vecs):
    dy = acc + ALPHA * rows[0]
    xhat, rstd = _ln_stats(rows[1])
    dz = _ln_bwd(dy, xhat, rstd, vecs[0])
    return [dz, dz], [_colsum(dy * xhat), _colsum(dy)]


def _epi_gate_bwd(acc, rows, vecs):
    oo, gg = rows
    sg = _sigmoid(gg)
    return [acc * (gg * sg), acc * oo * (sg * (1.0 + gg * (1.0 - sg)))], []


LRU_T = 256
HALO = 8


def _log1p(y):
    u = 1.0 + y
    return jnp.where(u == 1.0, y, jnp.log(u) * (y / (u - 1.0)))


def _lru_gates(c, wa_ref, wx_ref, pv):
    c16 = c.astype(BF16)
    pre_r = jnp.concatenate(
        [_dot(c16[:, n * LRU_BLOCK:(n + 1) * LRU_BLOCK], wa_ref[n], NN) for n in range(LRU_BLOCKS)], axis=1)
    pre_i = jnp.concatenate(
        [_dot(c16[:, n * LRU_BLOCK:(n + 1) * LRU_BLOCK], wx_ref[n], NN) for n in range(LRU_BLOCKS)], axis=1)
    r = _sigmoid(pre_r + pv[5:6])
    ig = _sigmoid(pre_i + pv[6:7])
    lam = pv[7:8]
    ls = jnp.minimum(lam, 0.0) - _log1p(jnp.exp(-jnp.abs(lam)))
    la = LRU_C * r * ls
    a = jnp.exp(la)
    a2 = a * a
    m = jnp.sqrt(jnp.tanh(-la) * (a2 + 1.0))
    return c16, r, ig, ls, la, a, a2, m


def _conv(ext_ref, x, pv, t):
    return (pv[4:5] + pv[3:4] * x + pv[2:3] * ext_ref[pl.ds(HALO - 1, t), :]
            + pv[1:2] * ext_ref[pl.ds(HALO - 2, t), :] + pv[0:1] * ext_ref[pl.ds(HALO - 3, t), :])


def _rglru_fwd(h0a, pvec, wa16, wx16):
    s = h0a.shape[0]
    w = D_MODEL
    t = min(LRU_T, s)
    assert s % t == 0

    def body(ax_ref, ag_ref, pv_ref, wa_ref, wx_ref, ya_ref, h_ref, ext_ref, a_ref, u_ref, hc_ref):
        i = pl.program_id(0)

        @pl.when(i == 0)
        def _():
            ext_ref[pl.ds(0, HALO), :] = jnp.zeros((HALO, w), F32)
            hc_ref[...] = jnp.zeros((1, w), F32)

        pv = pv_ref[...]
        ax = ax_ref[...]
        ext_ref[pl.ds(HALO, t), :] = ax
        c = _conv(ext_ref, ax, pv, t)
        ext_ref[pl.ds(0, HALO), :] = ax[t - HALO:, :]
        _, _, ig, _, _, a, _, m = _lru_gates(c, wa_ref, wx_ref, pv)
        a_ref[...] = a
        u_ref[...] = m * (ig * c)

        def steps(k, h):
            base = pl.multiple_of(k * HALO, HALO)
            for r in range(HALO):
                h = a_ref[pl.ds(base + r, 1), :] * h + u_ref[pl.ds(base + r, 1), :]
                h_ref[pl.ds(base + r, 1), :] = h
            return h

        hc_ref[...] = lax.fori_loop(0, t // HALO, steps, hc_ref[...])
        ag = ag_ref[...]
        ya_ref[...] = (h_ref[...] * (ag * _sigmoid(ag))).astype(BF16)

    full = lambda shp: pl.BlockSpec(shp, lambda i: (0,) * len(shp))
    return pl.pallas_call(
        body, name="rglru_fwd", grid=(s // t,),
        in_specs=[pl.BlockSpec((t, w), lambda i: (i, 0)), pl.BlockSpec((t, w), lambda i: (i, 1)),
                  full((8, w)), full((LRU_BLOCKS, LRU_BLOCK, LRU_BLOCK)), full((LRU_BLOCKS, LRU_BLOCK, LRU_BLOCK))],
        out_specs=[pl.BlockSpec((t, w), lambda i: (i, 0)), pl.BlockSpec((t, w), lambda i: (i, 0))],
        out_shape=[jax.ShapeDtypeStruct((s, w), BF16), jax.ShapeDtypeStruct((s, w), F32)],
        scratch_shapes=[pltpu.VMEM((t + HALO, w), F32), pltpu.VMEM((t, w), F32), pltpu.VMEM((t, w), F32),
                        pltpu.VMEM((1, w), F32)],
        compiler_params=_cparams(("arbitrary",)),
    )(h0a, h0a, pvec, wa16, wx16)


def _rglru_bwd(h0a, h, dh, pvec, wa16, wx16):
    s = h0a.shape[0]
    w = D_MODEL
    t = min(LRU_T, s)
    nb = s // t
    hb = t // HALO

    def body(ax_ref, axh_ref, h_ref, hh_ref, dh_ref, pv_ref, wa_ref, wx_ref,
             dax_ref, dwa_ref, dwx_ref, dpv_ref, ext_ref, hext_ref, dcext_ref, a_ref, g_ref, gc_ref):
        i = pl.program_id(0)
        blk = nb - 1 - i

        @pl.when(i == 0)
        def _():
            dwa_ref[...] = jnp.zeros_like(dwa_ref)
            dwx_ref[...] = jnp.zeros_like(dwx_ref)
            dpv_ref[...] = jnp.zeros_like(dpv_ref)
            gc_ref[...] = jnp.zeros((1, w), F32)
            dcext_ref[pl.ds(t, HALO), :] = jnp.zeros((HALO, w), F32)

        pv = pv_ref[...]
        ax = ax_ref[...]
        keep = (blk > 0).astype(F32)
        ext_ref[pl.ds(0, HALO), :] = axh_ref[...] * keep
        ext_ref[pl.ds(HALO, t), :] = ax
        hext_ref[pl.ds(0, HALO), :] = hh_ref[...] * keep
        hext_ref[pl.ds(HALO, t), :] = h_ref[...]
        c = _conv(ext_ref, ax, pv, t)
        c16, r, ig, ls, _, a, a2, m = _lru_gates(c, wa_ref, wx_ref, pv)
        a_ref[...] = a

        def steps(k, carry):
            base = pl.multiple_of(t - HALO - k * HALO, HALO)
            for r in reversed(range(HALO)):
                g = dh_ref[pl.ds(base + r, 1), :] + carry
                g_ref[pl.ds(base + r, 1), :] = g
                carry = a_ref[pl.ds(base + r, 1), :] * g
            return carry

        gc_ref[...] = lax.fori_loop(0, t // HALO, steps, gc_ref[...])
        g = g_ref[...]
        hprev = hext_ref[pl.ds(HALO - 1, t), :]
        gm = g * m
        d_la = g * hprev * a - (g * ig * c) * a2 / m
        d_pr = d_la * (LRU_C * ls) * r * (1.0 - r)
        d_pi = gm * c * ig * (1.0 - ig)
        dc = gm * ig
        dpr16 = d_pr.astype(BF16)
        dpi16 = d_pi.astype(BF16)
        dc_parts = []
        for n in range(LRU_BLOCKS):
            sl = slice(n * LRU_BLOCK, (n + 1) * LRU_BLOCK)
            dwa_ref[n] += _dot(c16[:, sl], dpr16[:, sl], TN)
            dwx_ref[n] += _dot(c16[:, sl], dpi16[:, sl], TN)
            dc_parts.append(_dot(dpr16[:, sl], wa_ref[n], NT) + _dot(dpi16[:, sl], wx_ref[n], NT))
        dc = dc + jnp.concatenate(dc_parts, axis=1)
        dcext_ref[pl.ds(0, t), :] = dc
        dax = (pv[3:4] * dc + pv[2:3] * dcext_ref[pl.ds(1, t), :] + pv[1:2] * dcext_ref[pl.ds(2, t), :]
               + pv[0:1] * dcext_ref[pl.ds(3, t), :])
        dax_ref[...] = dax.astype(BF16)
        dcext_ref[pl.ds(t, HALO), :] = dc[:HALO, :]
        sums = [_colsum(dc * ext_ref[pl.ds(HALO - 3 + kk, t), :]) for kk in range(4)]
        sums += [_colsum(dc), _colsum(d_pr), _colsum(d_pi), _colsum(d_la * (LRU_C * r))]
        for kk, part in enumerate(sums):
            dpv_ref[pl.ds(kk, 1), :] += part

        @pl.when(i == nb - 1)
        def _():
            lam = pv[7:8]
            dls = dpv_ref[pl.ds(7, 1), :]
            dpv_ref[pl.ds(7, 1), :] = dls * _sigmoid(-lam)

    full = lambda shp: pl.BlockSpec(shp, lambda i: (0,) * len(shp))
    rev = lambda cb: pl.BlockSpec((t, w), functools.partial(lambda i, cb: (nb - 1 - i, cb), cb=cb))
    halo = lambda cb: pl.BlockSpec(
        (HALO, w), functools.partial(lambda i, cb: (jnp.maximum((nb - 1 - i) * hb - 1, 0), cb), cb=cb))
    gw = (LRU_BLOCKS, LRU_BLOCK, LRU_BLOCK)
    return pl.pallas_call(
        body, name="rglru_bwd", grid=(nb,),
        in_specs=[rev(0), halo(0), rev(0), halo(0), rev(0), full((8, w)), full(gw), full(gw)],
        out_specs=[rev(0), full(gw), full(gw), full((8, w))],
        out_shape=[jax.ShapeDtypeStruct((s, w), BF16), jax.ShapeDtypeStruct(gw, F32),
                   jax.ShapeDtypeStruct(gw, F32), jax.ShapeDtypeStruct((8, w), F32)],
        scratch_shapes=[pltpu.VMEM((t + HALO, w), F32), pltpu.VMEM((t + HALO, w), F32),
                        pltpu.VMEM((t + HALO, w), F32), pltpu.VMEM((t, w), F32), pltpu.VMEM((t, w), F32),
                        pltpu.VMEM((1, w), F32)],
        compiler_params=_cparams(("arbitrary",)),
    )(h0a, h0a, h, h, dh, pvec, wa16, wx16)


SB_T = 256
SB_SUB = 4


def _split16(x):
    hi = x.astype(BF16)
    lo = (x - hi.astype(F32)).astype(BF16)
    return jnp.concatenate([hi, lo], axis=0)


def _sb_tile(q, k, scale, tri, run, causal):
    tq = q.shape[0]
    z = _dot(q, k, NT) * scale
    log1mb = -(jnp.maximum(z, 0.0) + jnp.log(1.0 + jnp.exp(-jnp.abs(z))))
    if causal is not None:
        log1mb = jnp.where(causal, log1mb, 0.0)
    cs = _dot(_split16(log1mb), tri, NN)
    cs = cs[:tq] + cs[tq:]
    wgt = jnp.exp(z + cs + run)
    if causal is not None:
        wgt = jnp.where(causal, wgt, 0.0)
    return z, wgt, run + cs[:, 0:1]


SB_DEAD = -105.0


def _sb_alive(run):
    return (jnp.max(run) > SB_DEAD).astype(jnp.int32)


def _sb_more(carry):
    return (carry[0] >= 0) & (carry[1] > 0)


def _sb_fwd(qkv, bg):
    s = qkv.shape[0]
    t = min(SB_T, s)
    nq = s // t
    dh = SB_HEAD_DIM
    scale = 1.0 / math.sqrt(dh)

    sub = SB_SUB if nq % SB_SUB == 0 else 1

    def body(q_ref, k_ref, v_ref, bg_ref, o_ref, y_ref):
        row = lax.broadcasted_iota(jnp.int32, (t, t), 0)
        col = lax.broadcasted_iota(jnp.int32, (t, t), 1)
        tri = jnp.where(row >= col, 1.0, 0.0).astype(BF16)
        causal = col < row
        def tile(q, kb, run, mask):
            off = pl.multiple_of(kb * t, t)
            _, wgt, run = _sb_tile(q, k_ref[pl.ds(off, t), :], scale, tri, run, mask)
            return run, _dot(wgt.astype(BF16), v_ref[pl.ds(off, t), :], NN)

        blocks = []
        for u in range(sub):
            iq = pl.program_id(1) * sub + u
            q = q_ref[pl.ds(u * t, t), :]
            run_d, acc_d = tile(q, iq, jnp.zeros((t, 1), F32), causal)
            run_l, acc_l = tile(q, jnp.maximum(iq - 1, 0), run_d, None)
            blocks.append((iq, q, jnp.where(iq > 0, run_l, run_d), acc_d + jnp.where(iq > 0, acc_l, 0.0)))
        for u, (iq, q, run, acc) in enumerate(blocks):
            def loop(carry, q=q):
                kb, _, run, acc = carry
                run, part = tile(q, kb, run, None)
                return kb - 1, _sb_alive(run), run, acc + part

            _, _, run, acc = lax.while_loop(_sb_more, loop, (iq - 2, _sb_alive(run), run, acc))
            rows = pl.ds(u * t, t)
            o_ref[rows, :] = acc
            g = bg_ref[rows, :]
            y_ref[rows, :] = (acc * (g * _sigmoid(g))).astype(BF16)

    blk = lambda off: pl.BlockSpec((sub * t, dh), functools.partial(lambda h, i, off: (i, h + off), off=off))
    col = lambda off: pl.BlockSpec((s, dh), functools.partial(lambda h, i, off: (0, h + off), off=off))
    return pl.pallas_call(
        body, name="sb_fwd", grid=(SB_HEADS, nq // sub),
        in_specs=[blk(0), col(SB_HEADS), col(2 * SB_HEADS), blk(0)],
        out_specs=[blk(0), blk(0)],
        out_shape=[jax.ShapeDtypeStruct((s, SB_HEADS * dh), F32), jax.ShapeDtypeStruct((s, SB_HEADS * dh), BF16)],
        compiler_params=_cparams(("parallel", "arbitrary")),
    )(qkv, qkv, qkv, bg)


def _sb_bwd(qkv, dob):
    s = qkv.shape[0]
    t = min(SB_T, s)
    nq = s // t
    dh = SB_HEAD_DIM
    scale = 1.0 / math.sqrt(dh)

    sub = SB_SUB if nq % SB_SUB == 0 else 1

    def body(q_ref, k_ref, v_ref, do_ref, dq_ref, dk_out, dv_out, e_ref, b_ref, pe_ref, pb_ref, dk_ref, dv_ref):
        @pl.when(pl.program_id(1) == 0)
        def _():
            dk_ref[...] = jnp.zeros_like(dk_ref)
            dv_ref[...] = jnp.zeros_like(dv_ref)

        row = lax.broadcasted_iota(jnp.int32, (t, t), 0)
        col = lax.broadcasted_iota(jnp.int32, (t, t), 1)
        tri = jnp.where(row >= col, 1.0, 0.0).astype(BF16)
        tri_x = jnp.where(row < col, 1.0, 0.0).astype(BF16)
        causal = col < row

        def sweep1(q, do, kb, e_dst, b_dst, run, mask, live=None):
            off = pl.multiple_of(kb * t, t)
            v = v_ref[pl.ds(off, t), :]
            z, wgt, run = _sb_tile(q, k_ref[pl.ds(off, t), :], scale, tri, run, mask)
            if live is not None:
                wgt = jnp.where(live, wgt, 0.0)
            e_dst[...] = wgt * _dot(do, v, NT)
            b_dst[...] = _sigmoid(z)
            dv_ref[pl.ds(off, t), :] += _dot(wgt.astype(BF16), do, TN)
            return run

        def sweep2(q, kb, e_src, b_src, pre, mask, live=None):
            off = pl.multiple_of(kb * t, t)
            e = e_src[...]
            beta = b_src[...]
            ps = _dot(_split16(e), tri_x, NN)
            ps = ps[:t] + ps[t:]
            dz = e * (1.0 - beta) - beta * (ps + pre)
            if mask is not None:
                dz = jnp.where(mask, dz, 0.0)
            if live is not None:
                dz = jnp.where(live, dz, 0.0)
            dz16 = (dz * scale).astype(BF16)
            dk_ref[pl.ds(off, t), :] += _dot(dz16, q, TN)
            return pre + ps[:, t - 1:t] + e[:, t - 1:t], _dot(dz16, k_ref[pl.ds(off, t), :], NN)

        blocks = []
        for u in range(sub):
            iq = pl.program_id(1) * sub + u
            q, do = q_ref[pl.ds(u * t, t), :], do_ref[pl.ds(u * t, t), :]
            left = jnp.maximum(iq - 1, 0)
            run_d = sweep1(q, do, iq, pe_ref.at[u, 0], pb_ref.at[u, 0], jnp.zeros((t, 1), F32), causal)
            run_l = sweep1(q, do, left, pe_ref.at[u, 1], pb_ref.at[u, 1], run_d, None, iq > 0)
            blocks.append((iq, q, do, left, jnp.where(iq > 0, run_l, run_d)))
        carried = []
        for iq, q, do, left, run in blocks:
            def loop1(carry, q=q, do=do):
                run = sweep1(q, do, carry[0], e_ref.at[carry[0]], b_ref.at[carry[0]], carry[2], None)
                return carry[0] - 1, _sb_alive(run), run

            first = lax.while_loop(_sb_more, loop1, (iq - 2, _sb_alive(run), run))[0] + 1

            def loop2(kb, carry, q=q):
                pre, part = sweep2(q, kb, e_ref.at[kb], b_ref.at[kb], carry[0], None)
                return pre, carry[1] + part

            carried.append(lax.fori_loop(first, iq - 1, loop2, (jnp.zeros((t, 1), F32), jnp.zeros((t, dh), F32))))
        for u, ((iq, q, do, left, _), (pre, dq)) in enumerate(zip(blocks, carried)):
            pre, dq_l = sweep2(q, left, pe_ref.at[u, 1], pb_ref.at[u, 1], pre, None, iq > 0)
            _, dq_d = sweep2(q, iq, pe_ref.at[u, 0], pb_ref.at[u, 0], pre, causal)
            dq_ref[pl.ds(u * t, t), :] = (dq + dq_l + dq_d).astype(BF16)

        @pl.when(pl.program_id(1) == nq // sub - 1)
        def _():
            dk_out[...] = dk_ref[...].astype(BF16)
            dv_out[...] = dv_ref[...].astype(BF16)

    blk = lambda off: pl.BlockSpec((sub * t, dh), functools.partial(lambda h, i, off: (i, h + off), off=off))
    col = lambda off: pl.BlockSpec((s, dh), functools.partial(lambda h, i, off: (0, h + off), off=off))
    wide = SB_HEADS * dh
    return pl.pallas_call(
        body, name="sb_bwd", grid=(SB_HEADS, nq // sub),
        in_specs=[blk(0), col(SB_HEADS), col(2 * SB_HEADS), blk(0)],
        out_specs=[blk(0), col(0), col(0)],
        out_shape=[jax.ShapeDtypeStruct((s, wide), BF16)] * 3,
        scratch_shapes=[pltpu.VMEM((nq, t, t), F32), pltpu.VMEM((nq, t, t), F32),
                        pltpu.VMEM((sub, 2, t, t), F32), pltpu.VMEM((sub, 2, t, t), F32),
                        pltpu.VMEM((s, dh), F32), pltpu.VMEM((s, dh), F32)],
        compiler_params=_cparams(("parallel", "arbitrary")),
    )(qkv, qkv, qkv, dob)


def _alibi_slope(h):
    return float(2.0 ** (-8.0 * (h + 1) / C_HEADS))


GROUP_ROWS = C_GROUP * WINDOW


def _swa_window(n):
    qb = WINDOW
    i = lax.broadcasted_iota(jnp.int32, (qb, 2 * qb), 0)
    j = lax.broadcasted_iota(jnp.int32, (qb, 2 * qb), 1)
    d = i - j + qb
    valid = (d >= 0) & (d < WINDOW) & ((j >= qb) | (n > 0))
    stacked = lambda x: jnp.concatenate([x] * C_GROUP, axis=0)
    return stacked(d.astype(F32)), stacked(jnp.where(valid, 0.0, -1e30))


def _swa_group_cols(c, sink_ref):
    head = lax.shift_right_logical(lax.broadcasted_iota(jnp.int32, (GROUP_ROWS, 1), 0), WINDOW.bit_length() - 1)
    slope = jnp.zeros((GROUP_ROWS, 1), F32)
    sink = jnp.zeros((GROUP_ROWS, 1), F32)
    for hh in range(C_GROUP):
        slope = jnp.where(head == hh, _alibi_slope(c * C_GROUP + hh), slope)
        sink = jnp.where(head == hh, sink_ref[c * C_GROUP + hh], sink)
    return slope, sink


def _swa_probs(qg, kw, sink, slope, dist, bias, scale):
    sc = _dot(qg, kw, NT) * scale - slope * dist + bias
    m = jnp.maximum(jnp.max(sc, axis=-1, keepdims=True), sink)
    p = jnp.exp(sc - m)
    ps = jnp.exp(sink - m)
    inv = 1.0 / (jnp.sum(p, axis=-1, keepdims=True) + ps)
    return p * inv, ps * inv


def _stack_heads(x, c):
    hd = C_HEAD_DIM
    return jnp.concatenate([x[:, (c * C_GROUP + hh) * hd:(c * C_GROUP + hh + 1) * hd] for hh in range(C_GROUP)], axis=0)


SWA_SUB = 2


def _swa_kv_specs(sub, kvw):
    return [pl.BlockSpec((WINDOW, 2 * kvw), functools.partial(lambda n, w: (n * sub + w, 0), w=w)) for w in range(sub + 1)]


def _swa_fwd(q16, gate, kvp, sinks):
    s = q16.shape[0]
    qb = WINDOW
    hd = C_HEAD_DIM
    scale = 1.0 / math.sqrt(hd)
    kvw = C_KV_HEADS * hd
    sub = SWA_SUB if (s // qb) % SWA_SUB == 0 else 1

    def body(sink_ref, q_ref, g_ref, *refs):
        kv_refs, (o_ref, y_ref) = refs[:sub + 1], refs[sub + 1:]
        for u in range(sub):
            rows = slice(u * qb, (u + 1) * qb)
            q = q_ref[rows, :]
            kv = jnp.concatenate([kv_refs[u][...], kv_refs[u + 1][...]], axis=0)
            dist, bias = _swa_window(pl.program_id(0) * sub + u)
            for c in range(C_KV_HEADS):
                slope, sink = _swa_group_cols(c, sink_ref)
                kw = kv[:, c * hd:(c + 1) * hd]
                vw = kv[:, kvw + c * hd:kvw + (c + 1) * hd]
                p, _ = _swa_probs(_stack_heads(q, c), kw, sink, slope, dist, bias, scale)
                og = _dot(p.astype(BF16), vw, NN)
                for hh in range(C_GROUP):
                    h = c * C_GROUP + hh
                    o_ref[rows, h * hd:(h + 1) * hd] = og[hh * qb:(hh + 1) * qb, :]
        g = g_ref[...]
        y_ref[...] = (o_ref[...] * (g * _sigmoid(g))).astype(BF16)

    wide = C_HEADS * hd
    big = pl.BlockSpec((sub * qb, wide), lambda n: (n, 0))
    return pl.pallas_call(
        body, name="swa_fwd", grid=(s // (sub * qb),),
        in_specs=[pl.BlockSpec(memory_space=pltpu.SMEM), big, big] + _swa_kv_specs(sub, kvw),
        out_specs=[big, big],
        out_shape=[jax.ShapeDtypeStruct((s, wide), F32), jax.ShapeDtypeStruct((s, wide), BF16)],
        compiler_params=_cparams(("arbitrary",)),
    )(sinks, q16, gate, *([kvp] * (sub + 1)))


def _swa_bwd(q16, do16, kvp, sinks):
    s = q16.shape[0]
    qb = WINDOW
    hd = C_HEAD_DIM
    scale = 1.0 / math.sqrt(hd)
    kvw = C_KV_HEADS * hd
    sub = 1
    nstep = s // (sub * qb)

    def body(sink_ref, q_ref, do_ref, *refs):
        kv_refs, (dq_ref, dkv_ref, ds_ref, sacc_ref) = refs[:sub + 1], refs[sub + 1:]

        @pl.when(pl.program_id(0) == 0)
        def _():
            dkv_ref[...] = jnp.zeros_like(dkv_ref)
            sacc_ref[...] = jnp.zeros_like(sacc_ref)

        for u in range(sub):
            rows = slice(u * qb, (u + 1) * qb)
            n = pl.program_id(0) * sub + u
            q = q_ref[rows, :]
            do = do_ref[rows, :]
            kv = jnp.concatenate([kv_refs[u][...], kv_refs[u + 1][...]], axis=0)
            off = pl.multiple_of(n * qb, qb)
            dist, bias = _swa_window(n)
            for c in range(C_KV_HEADS):
                slope, sink = _swa_group_cols(c, sink_ref)
                kw = kv[:, c * hd:(c + 1) * hd]
                vw = kv[:, kvw + c * hd:kvw + (c + 1) * hd]
                qg = _stack_heads(q, c)
                dog = _stack_heads(do, c)
                p, ps = _swa_probs(qg, kw, sink, slope, dist, bias, scale)
                dp = _dot(dog, vw, NT)
                dd = jnp.sum(p * dp, axis=-1, keepdims=True)
                ds16 = (p * (dp - dd) * scale).astype(BF16)
                sacc_ref[c] += ps * dd
                dqg = _dot(ds16, kw, NN).astype(BF16)
                for hh in range(C_GROUP):
                    h = c * C_GROUP + hh
                    dq_ref[rows, h * hd:(h + 1) * hd] = dqg[hh * qb:(hh + 1) * qb, :]
                dkv_ref[pl.ds(off, 2 * qb), c * hd:(c + 1) * hd] += _dot(ds16, qg, TN)
                dkv_ref[pl.ds(off, 2 * qb), kvw + c * hd:kvw + (c + 1) * hd] += _dot(p.astype(BF16), dog, TN)

        @pl.when(pl.program_id(0) == nstep - 1)
        def _():
            lane = lax.broadcasted_iota(jnp.int32, (1, 128), 1)
            row = jnp.zeros((1, 128), F32)
            for c in range(C_KV_HEADS):
                for hh in range(C_GROUP):
                    tot = jnp.sum(sacc_ref[c, pl.ds(hh * qb, qb), :], axis=0, keepdims=True)
                    row = jnp.where(lane == c * C_GROUP + hh, -tot, row)
            ds_ref[...] = row

    wide = C_HEADS * hd
    big = pl.BlockSpec((sub * qb, wide), lambda n: (n, 0))
    return pl.pallas_call(
        body, name="swa_bwd", grid=(nstep,),
        in_specs=[pl.BlockSpec(memory_space=pltpu.SMEM), big, big] + _swa_kv_specs(sub, kvw),
        out_specs=[big,
                   pl.BlockSpec((s + qb, 2 * kvw), lambda n: (0, 0)), pl.BlockSpec((1, 128), lambda n: (0, 0))],
        out_shape=[jax.ShapeDtypeStruct((s, wide), BF16), jax.ShapeDtypeStruct((s + qb, 2 * kvw), F32),
                   jax.ShapeDtypeStruct((1, 128), F32)],
        scratch_shapes=[pltpu.VMEM((C_KV_HEADS, GROUP_ROWS, 1), F32)],
        compiler_params=_cparams(("arbitrary",)),
    )(sinks, q16, do16, *([kvp] * (sub + 1)))


def _adamw(parts, w, m, v, *, name, tr=496):
    npart, r, c = parts.shape
    tr = min(tr, r)
    assert r % tr == 0 and tr % 16 == 0
    c1 = 1.0 / (1.0 - ADAM_B1 ** ADAM_STEP)
    c2 = 1.0 / (1.0 - ADAM_B2 ** ADAM_STEP)

    def body(p_ref, w_ref, m_ref, v_ref, g_ref, d_ref, nm_ref, nv_ref):
        g = p_ref[0].astype(F32)
        for j in range(1, npart):
            g = g + p_ref[j].astype(F32)
        nm = ADAM_B1 * m_ref[...] + (1.0 - ADAM_B1) * g
        nv = ADAM_B2 * v_ref[...] + (1.0 - ADAM_B2) * (g * g)
        g_ref[...] = g
        nm_ref[...] = nm
        nv_ref[...] = nv
        d_ref[...] = -ADAM_LR * ((nm * c1) / (jnp.sqrt(nv * c2) + ADAM_EPS) + ADAM_WD * w_ref[...])

    spec = pl.BlockSpec((tr, c), lambda i: (i, 0))
    return pl.pallas_call(
        body, name=name, grid=(r // tr,),
        in_specs=[pl.BlockSpec((npart, tr, c), lambda i: (0, i, 0)), spec, spec, spec],
        out_specs=[spec] * 4, out_shape=[jax.ShapeDtypeStruct((r, c), F32)] * 4,
        compiler_params=_cparams(("parallel",)),
    )(parts, w, m, v)


GATHER = "gather"
A2A = "a2a"
GATHER_COLS = "gather_cols"
A2A_COLS = "a2a_cols"


def _exchange(bufs, gather, *, name):
    nb = len(bufs)

    def body(*refs):
        _exchange_start(refs[:nb], refs[nb:2 * nb], gather, refs[2 * nb:])
        _exchange_wait(refs[:nb], refs[nb:2 * nb], gather, refs[2 * nb:])

    hbm = pl.BlockSpec(memory_space=pl.ANY)
    return pl.pallas_call(
        body, name=name, in_specs=[hbm] * nb, out_specs=[hbm] * nb, out_shape=_exchange_out_shapes(bufs, gather),
        scratch_shapes=_exchange_sems(nb),
        compiler_params=pltpu.CompilerParams(has_side_effects=True),
    )(*bufs)


def _gather_two_level(bufs, modes, x, *, tr=1024, name):
    nb = len(bufs)
    rows, cols = x.shape
    tr = min(tr, rows)
    steps = rows // tr

    def body(*refs):
        x_ref, src, x16_ref, dst = refs[0], refs[1:1 + nb], refs[1 + nb], refs[2 + nb:2 + 2 * nb]
        send_sems, recv_sems, local_sems = refs[2 + 2 * nb:]
        cx, cy, cc = lax.axis_index("x"), lax.axis_index("y"), lax.axis_index("c")
        me = 4 * cx + 2 * cy + cc
        here, sibling = (cx, cy, cc), (cx, cy, 1 - cc)

        def copy(b, k, origin, to, source):
            return pltpu.make_async_remote_copy(
                src_ref=source, dst_ref=_slot(dst[b], modes[b], origin), send_sem=send_sems.at[b, k - 1],
                recv_sem=recv_sems.at[b, k - 1], device_id=to, device_id_type=pl.DeviceIdType.MESH)

        def first_copies():
            local = [pltpu.make_async_copy(src[b], _slot(dst[b], modes[b], me), local_sems.at[b]) for b in range(nb)]
            sends = [copy(b, k, me, (cx ^ (k >> 2), cy ^ ((k >> 1) & 1), cc ^ (k & 1)), src[b])
                     for k in (1, 2, 4, 6) for b in range(nb)]
            return local, sends

        @pl.when(pl.program_id(0) == 0)
        def _():
            local, sends = first_copies()
            for cp in local + sends:
                cp.start()

        x16_ref[...] = x_ref[...].astype(BF16)

        @pl.when(pl.program_id(0) == steps - 1)
        def _():
            local, sends = first_copies()
            for j in (2, 4, 6):
                for b in range(nb):
                    copy(b, j, me ^ j, here, src[b]).wait_recv()
                    passed = copy(b, j ^ 1, me ^ j, sibling, _slot(dst[b], modes[b], me ^ j))
                    passed.start()
                    sends.append(passed)
            for k in (1, 3, 5, 7):
                for b in range(nb):
                    copy(b, k, me ^ k, here, src[b]).wait_recv()
            for cp in sends:
                cp.wait_send()
            for cp in local:
                cp.wait()

    hbm = pl.BlockSpec(memory_space=pl.ANY)
    tile = pl.BlockSpec((tr, cols), lambda i: (i, 0))
    return pl.pallas_call(
        body, name=name, grid=(steps,), in_specs=[tile] + [hbm] * nb, out_specs=[tile] + [hbm] * nb,
        out_shape=[jax.ShapeDtypeStruct(x.shape, BF16)] + _exchange_out_shapes(bufs, modes),
        scratch_shapes=_exchange_sems(nb),
        compiler_params=pltpu.CompilerParams(dimension_semantics=("arbitrary",), vmem_limit_bytes=VMEM_LIMIT,
                                             has_side_effects=True),
    )(x, *bufs)


def _exchange_out_shapes(bufs, gather):
    shapes = {GATHER: lambda s: (N_DEV,) + s, A2A: lambda s: s, GATHER_COLS: lambda s: (s[0], N_DEV * s[1]),
              A2A_COLS: lambda s: (N_DEV, s[0], s[1] // N_DEV)}
    return [jax.ShapeDtypeStruct(shapes[g](tuple(b.shape)), b.dtype) for b, g in zip(bufs, gather)]


def _sent(src, mode, peer):
    if mode == A2A:
        return src.at[peer]
    if mode == A2A_COLS:
        w = src.shape[1] // N_DEV
        return src.at[:, pl.ds(pl.multiple_of(peer * w, 128), w)]
    return src


def _slot(dst, mode, dev):
    if mode == GATHER_COLS:
        w = dst.shape[1] // N_DEV
        return dst.at[:, pl.ds(pl.multiple_of(dev * w, 128), w)]
    return dst.at[dev]


def _exchange_sems(nb):
    return [pltpu.SemaphoreType.DMA((nb, N_DEV - 1)), pltpu.SemaphoreType.DMA((nb, N_DEV - 1)),
            pltpu.SemaphoreType.DMA((nb,))]


def _exchange_copies(src, dst, gather, sems):
    send_sems, recv_sems, local_sems = sems
    x, y, c = lax.axis_index("x"), lax.axis_index("y"), lax.axis_index("c")
    me = 4 * x + 2 * y + c
    local, sends, recvs = [], [], []
    for b in range(len(src)):
        mine = _sent(src[b], gather[b], me)
        local.append(pltpu.make_async_copy(mine, _slot(dst[b], gather[b], me), local_sems.at[b]))
        for k in range(1, N_DEV):
            px, py, pc = x ^ (k >> 2), y ^ ((k >> 1) & 1), c ^ (k & 1)
            peer = 4 * px + 2 * py + pc
            pair = dict(send_sem=send_sems.at[b, k - 1], recv_sem=recv_sems.at[b, k - 1],
                        device_id_type=pl.DeviceIdType.MESH)
            sends.append(pltpu.make_async_remote_copy(
                src_ref=_sent(src[b], gather[b], peer), dst_ref=_slot(dst[b], gather[b], me), device_id=(px, py, pc),
                **pair))
            recvs.append(pltpu.make_async_remote_copy(
                src_ref=mine, dst_ref=_slot(dst[b], gather[b], peer), device_id=(x, y, c), **pair))
    return local, sends, recvs


def _exchange_start(src, dst, gather, sems):
    local, sends, _ = _exchange_copies(src, dst, gather, sems)
    for cp in local + sends:
        cp.start()


def _exchange_wait(src, dst, gather, sems):
    local, sends, recvs = _exchange_copies(src, dst, gather, sems)
    for cp in sends:
        cp.wait_send()
    for cp in recvs:
        cp.wait_recv()
    for cp in local:
        cp.wait()


R_SMALL = 16
REPL_ROWS = 272
REPL_LEN = REPL_ROWS * D_MODEL


def _small_block(parts):
    flat = jnp.concatenate(parts, axis=-1)
    lead = flat.ndim - 1
    return jnp.pad(flat[..., None, :], [(0, 0)] * lead + [(0, R_SMALL - 1), (0, D_MODEL - flat.shape[-1])])


def _small_vectors(conv_w, ln_g, ln_b):
    return _small_block([conv_w.reshape(512), ln_g.reshape(128), ln_b.reshape(128)])


REPL_SHAPES = ((1, 8, 128, 128), (1, 8, 128, 128), (1, 1024), (1, 1024), (1, 1024), (1, 1024), (1, 1024), (1, 1024),
               (1, 16))


def _pack_repl(parts):
    flat = jnp.concatenate([p.reshape(-1) for p in parts])
    return jnp.concatenate([flat, jnp.zeros((REPL_LEN - flat.shape[0],), F32)]).reshape(REPL_ROWS, D_MODEL)


def _unpack_repl(p):
    flat = p.reshape(-1)
    out, o = [], 0
    for shp in REPL_SHAPES:
        n = math.prod(shp)
        out.append(flat[o:o + n].reshape(shp))
        o += n
    return out


def kernel(x, e_w_in, e_conv_w, e_conv_b, e_w_gate_a, e_b_gate_a, e_w_gate_x, e_b_gate_x, e_lru_lambda, e_w_out, e_ln_g, e_ln_b, o_w_in, o_sinks, o_w_out, o_ln_g, o_ln_b, loss_target, m_e_w_in, m_e_conv_w, m_e_conv_b, m_e_w_gate_a, m_e_b_gate_a, m_e_w_gate_x, m_e_b_gate_x, m_e_lru_lambda, m_e_w_out, m_e_ln_g, m_e_ln_b, m_o_w_in, m_o_sinks, m_o_w_out, m_o_ln_g, m_o_ln_b, v_e_w_in, v_e_conv_w, v_e_conv_b, v_e_w_gate_a, v_e_b_gate_a, v_e_w_gate_x, v_e_b_gate_x, v_e_lru_lambda, v_e_w_out, v_e_ln_g, v_e_ln_b, v_o_w_in, v_o_sinks, v_o_w_out, v_o_ln_g, v_o_ln_b):
    d = D_MODEL
    x0 = x[0]
    target = loss_target[0]
    s = x0.shape[0]

    spack = _small_block([e_conv_w.reshape(512), o_ln_g.reshape(128), o_ln_b.reshape(128)])
    x0_16, wi_e, sall = _gather_two_level([e_w_in[0].astype(BF16), spack], [GATHER_COLS, GATHER], x0,
                                          name="gather_w_in")
    conv_w = sall[:, 0, 0:512].reshape(N_DEV, 4, 128).transpose(1, 0, 2).reshape(4, d)
    ln_g_o = sall[:, 0, 512:640].reshape(1, d)
    ln_b_o = sall[:, 0, 640:768].reshape(1, d)
    pvec = jnp.concatenate([conv_w, e_conv_b, e_b_gate_a, e_b_gate_x, e_lru_lambda], axis=0)
    wa16 = e_w_gate_a[0].astype(BF16)
    wx16 = e_w_gate_x[0].astype(BF16)
    sinks = o_sinks[0]

    h0a, wall_out_e = _matmul(x0_16, wi_e, mode="nn", n_out=2 * d, b_off=0, name="l0_in_a",
                              comm=([e_w_out[0].astype(BF16)], [GATHER]))
    qkv, wall_in_o = _matmul(x0_16, wi_e, mode="nn", n_out=3 * d, b_off=2 * d, out_dtype=BF16, name="l0_in_qkv",
                             comm=([o_w_in[0].astype(BF16)], [GATHER]))
    bg, wall_out_o = _matmul(x0_16, wi_e, mode="nn", n_out=d, b_off=5 * d, name="l0_in_bg",
                             comm=([o_w_out[0].astype(BF16)], [GATHER]))
    wo_e = wall_out_e.reshape(2 * d, d)
    wi_o = wall_in_o.transpose(1, 0, 2).reshape(d, 2304)
    wi_o = jnp.concatenate([wi_o[:, :1024], wi_o[:, 1280:], wi_o[:, 1024:1280]], axis=1)
    wo_o = wall_out_o.reshape(d, d)
    ya, hst = _rglru_fwd(h0a, pvec, wa16, wx16)
    ob, yb = _sb_fwd(qkv, bg)
    z0, x1, x1_16 = _matmul([ya, yb], wo_e, mode="nn", tm=512, name="l0_out",
                            epi=(_epi_ln_fwd, [(x0, 0)], [e_ln_g, e_ln_b], [F32, F32, BF16], 0))

    q1 = _matmul(x1_16, wi_o, mode="nn", n_out=d, b_off=0, out_dtype=BF16, name="l1_in_q")
    g1 = _matmul(x1_16, wi_o, mode="nn", n_out=d, b_off=d, name="l1_in_g")
    kv1 = _matmul(x1_16, wi_o, mode="nn", n_out=256, b_off=2 * d, tn=256, out_dtype=BF16, name="l1_in_kv")
    kvp = jnp.concatenate([jnp.zeros((WINDOW, 256), BF16), kv1], axis=0)
    o1, y1 = _swa_fwd(q1, g1, kvp, sinks)
    dz1, dz1_16, loss_cols, dg_o, db_o = _matmul(
        y1, wo_o, mode="nn", tm=512, name="l1_out",
        epi=(_epi_ln_loss_bwd, [(x1, 0), (target, 0)], [ln_g_o, ln_b_o], [F32, BF16], 3))
    loss_hi = jnp.sum(loss_cols).astype(BF16).astype(F32)
    loss_terms = jnp.stack([loss_hi, jnp.sum(loss_cols) - loss_hi]).reshape(1, 2)

    dwo_o = _matmul(y1, dz1_16, mode="tn", out_dtype=BF16, name="l1_dwout")
    do1, dg1, parts_out_o = _matmul(dz1_16, wo_o, mode="nt", tm=512, name="l1_dy",
                                    comm=([dwo_o.reshape(N_DEV, 128, d)], [A2A]),
                                    epi=(_epi_gate_bwd, [(o1, 0), (g1, 0)], [], [BF16, BF16], 0))
    dq1, dkvp, dsink = _swa_bwd(q1, do1, kvp, sinks)
    dkv1 = dkvp[WINDOW:].astype(BF16)
    dh1 = [dq1, dg1, dkv1]
    dw_qg = _matmul(x1_16, [dq1, dg1], mode="tn", out_dtype=BF16, name="l1_dwin_qg")
    dw_kv = _matmul(x1_16, dkv1, mode="tn", out_dtype=BF16, name="l1_dwin_kv")
    dwi_o = jnp.concatenate([dw_qg[:, :1024], dw_kv, dw_qg[:, 1024:]], axis=1)
    dz0, dz0_16, dg_e, db_e, parts_in_o = _matmul(
        dh1, wi_o, mode="nt", tm=512, tk=2304, name="l1_dx",
        comm=([dwi_o.reshape(d, N_DEV, 288).transpose(1, 0, 2)], [A2A]),
        epi=(_epi_ln_bwd, [(dz1, 0), (z0, 0)], [e_ln_g], [F32, BF16], 2))

    dwo_e = _matmul([ya, yb], dz0_16, mode="tn", out_dtype=BF16, name="l0_dwout")
    dhst, dag, parts_out_e = _matmul(
        dz0_16, wo_e, mode="nt", tm=512, n_out=d, b_off=0, name="l0_dy_a",
        comm=([dwo_e.reshape(N_DEV, 256, d)], [A2A]),
        epi=(_epi_gate_bwd, [(hst, 0), (h0a, d)], [], [F32, BF16], 0))
    dob, dbg = _matmul(dz0_16, wo_e, mode="nt", tm=512, n_out=d, b_off=d, name="l0_dy_b",
                       epi=(_epi_gate_bwd, [(ob, 0), (bg, 0)], [], [BF16, BF16], 0))
    dq0, dk0, dv0 = _sb_bwd(qkv, dob)
    dax, dwa, dwx, dpv = _rglru_bwd(h0a, hst, dhst, pvec, wa16, wx16)
    dh0 = [dax, dag, dq0, dk0, dv0, dbg]
    repl = _pack_repl([dwa, dwx, dpv[4:5], dpv[5:6], dpv[6:7], dpv[7:8], dg_e, db_e, dsink[:, :C_HEADS], loss_terms])
    dwi_e, parts_repl = _matmul(x0_16, dh0, mode="tn", out_dtype=BF16, name="l0_dwin",
                                comm=([repl.astype(BF16)], [GATHER]))
    small = _small_block([dpv[0:4].reshape(4, N_DEV, 128).transpose(1, 0, 2).reshape(N_DEV, 512),
                          dg_o.reshape(N_DEV, 128), db_o.reshape(N_DEV, 128)]).astype(BF16)
    grad_x, parts_in_e, parts_small = _matmul(dh0, wi_e, mode="nt", tm=512, res=dz0, alpha=ALPHA, name="l0_dx",
                                              comm=([dwi_e, small], [A2A_COLS, A2A]))

    res_in_e = _adamw(parts_in_e, e_w_in[0], m_e_w_in[0], v_e_w_in[0], tr=512, name="adamw_w_in_e")
    res_out_e = _adamw(parts_out_e, e_w_out[0], m_e_w_out[0], v_e_w_out[0], name="adamw_w_out_e")
    res_in_o = _adamw(parts_in_o, o_w_in[0], m_o_w_in[0], v_o_w_in[0], tr=512, name="adamw_w_in_o")
    res_out_o = _adamw(parts_out_o, o_w_out[0], m_o_w_out[0], v_o_w_out[0], name="adamw_w_out_o")
    res_small = _adamw(parts_small, _small_vectors(e_conv_w, o_ln_g, o_ln_b),
                       _small_vectors(m_e_conv_w, m_o_ln_g, m_o_ln_b),
                       _small_vectors(v_e_conv_w, v_o_ln_g, v_o_ln_b), name="adamw_vectors")

    w_r = _pack_repl([e_w_gate_a, e_w_gate_x, e_conv_b, e_b_gate_a, e_b_gate_x, e_lru_lambda, e_ln_g, e_ln_b, o_sinks])
    m_r = _pack_repl([m_e_w_gate_a, m_e_w_gate_x, m_e_conv_b, m_e_b_gate_a, m_e_b_gate_x, m_e_lru_lambda, m_e_ln_g,
                      m_e_ln_b, m_o_sinks])
    v_r = _pack_repl([v_e_w_gate_a, v_e_w_gate_x, v_e_conv_b, v_e_b_gate_a, v_e_b_gate_x, v_e_lru_lambda, v_e_ln_g,
                      v_e_ln_b, v_o_sinks])
    g_r, d_r, nm_r, nv_r = _adamw(parts_repl, w_r, m_r, v_r, name="adamw_replicated")
    loss_at = sum(math.prod(shp) for shp in REPL_SHAPES)
    loss = g_r[loss_at // d, loss_at % d] + g_r[(loss_at + 1) // d, (loss_at + 1) % d]

    def assemble(i, rp):
        vec = res_small[i][0]
        cw, lg_o, lb_o = vec[0:512].reshape(1, 4, 128), vec[512:640].reshape(1, 128), vec[640:768].reshape(1, 128)
        w_a, w_x, cb, b_a, b_x, lam, lg_e, lb_e, snk = _unpack_repl(rp)
        return [res_in_e[i][None], cw, cb, w_a, b_a, w_x, b_x, lam, res_out_e[i][None], lg_e, lb_e,
                res_in_o[i][None], snk, res_out_o[i][None], lg_o, lb_o]

    return (loss, grad_x[None], *assemble(0, g_r), *assemble(1, d_r), *assemble(2, nm_r), *assemble(3, nv_r))
```

```python
import functools
import math

import jax
import jax.numpy as jnp
from jax import lax
from jax.experimental import pallas as pl
from jax.experimental.pallas import tpu as pltpu

F32 = jnp.float32
BF16 = jnp.bfloat16

N_DEV = 8
D_MODEL = 1024
LRU_BLOCKS = 8
LRU_BLOCK = 128
LRU_C = 8.0
SB_HEADS = 8
SB_HEAD_DIM = 128
C_HEADS = 16
C_KV_HEADS = 2
C_GROUP = 8
C_HEAD_DIM = 64
WINDOW = 128
DEPTH = 2
ALPHA = float((2 * DEPTH) ** 0.25)
LN_EPS = 1e-5
ADAM_LR = 0.001
ADAM_B1 = 0.9
ADAM_B2 = 0.999
ADAM_EPS = 1e-08
ADAM_WD = 0.01
ADAM_STEP = 10

VMEM_LIMIT = 56 * 1024 * 1024

NN = ((1,), (0,))
NT = ((1,), (1,))
TN = ((0,), (0,))


def _dot(a, b, dims):
    return lax.dot_general(a, b, (dims, ((), ())), preferred_element_type=F32)


def _sigmoid(x):
    return 0.5 * jnp.tanh(0.5 * x) + 0.5


def _cparams(sem, vmem=VMEM_LIMIT):
    return pltpu.CompilerParams(dimension_semantics=sem, vmem_limit_bytes=vmem)


def _matmul(a, b, *, mode, n_out=None, b_off=0, out_dtype=F32, res=None, alpha=1.0,
            tm=1024, tn=1024, tk=1024, comm=None, epi=None, name):
    a_list = list(a) if isinstance(a, (list, tuple)) else [a]
    b_list = list(b) if isinstance(b, (list, tuple)) else [b]
    if mode == "tn":
        k = a_list[0].shape[0]
        m = sum(p.shape[1] for p in a_list)
        n = n_out if n_out is not None else sum(p.shape[1] for p in b_list)
        assert all(p.shape[1] == tm for p in a_list) or len(a_list) == 1
        assert all(p.shape[1] == tn for p in b_list) or len(b_list) == 1
    else:
        assert len(b_list) == 1
        m = a_list[0].shape[0]
        k = sum(p.shape[1] for p in a_list)
        n = n_out if n_out is not None else (b.shape[0] if mode == "nt" else b.shape[1])
        if len(a_list) > 1:
            tk = k
    tm, tn, tk = min(tm, m), min(tn, n), min(tk, k)
    assert m % tm == 0 and n % tn == 0 and k % tk == 0 and b_off % tn == 0
    grid = (m // tm, n // tn, k // tk)
    nk = grid[2]
    jo = b_off // tn
    dims = {"nn": NN, "nt": NT, "tn": TN}[mode]
    nc = len(comm[0]) if comm else 0
    if epi:
        assert res is None
        epi_fn, epi_rows, epi_vecs, epi_dtypes, n_sums = epi
        assert n_sums == 0 or grid[1] == 1
    else:
        epi_rows, epi_vecs, epi_dtypes, n_sums = [], [], [out_dtype], 0
    n_ab = len(a_list) + len(b_list)
    n_in = n_ab + (res is not None) + len(epi_rows) + len(epi_vecs)
    n_res = len(epi_dtypes) + n_sums

    def body(*refs):
        a_refs, b_refs = refs[:len(a_list)], refs[len(a_list):n_ab]
        r_ref = refs[n_ab] if res is not None else None
        row_refs = refs[n_in - len(epi_rows) - len(epi_vecs):n_in - len(epi_vecs)]
        vec_refs = refs[n_in - len(epi_vecs):n_in]
        o_refs = refs[n_in + nc:n_in + nc + len(epi_dtypes)]
        sum_refs = refs[n_in + nc + len(epi_dtypes):n_in + nc + n_res]
        if comm:
            step = [pl.program_id(ax) for ax in range(3)]
            c_src = refs[n_in:n_in + nc]
            c_dst = refs[n_in + nc + n_res:n_in + 2 * nc + n_res]
            sems = refs[len(refs) - 3:]

            @pl.when((step[0] == 0) & (step[1] == 0) & (step[2] == 0))
            def _():
                _exchange_start(c_src, c_dst, comm[1], sems)

        def finish(out):
            if not epi:
                if r_ref is not None:
                    out = out + alpha * r_ref[...]
                o_refs[0][...] = out.astype(out_dtype)
                return
            outs, sums = epi_fn(out, [r[...] for r in row_refs], [v[...] for v in vec_refs])
            for o_ref, o in zip(o_refs, outs):
                o_ref[...] = o.astype(o_ref.dtype)
            if sum_refs:
                first_rows = pl.program_id(0) == 0

                @pl.when(first_rows)
                def _():
                    for s_ref, part_sum in zip(sum_refs, sums):
                        s_ref[...] = part_sum

                @pl.when(jnp.logical_not(first_rows))
                def _():
                    for s_ref, part_sum in zip(sum_refs, sums):
                        s_ref[...] += part_sum

        def accumulate(part):
            if nk == 1:
                finish(part)
                return
            acc_ref = refs[n_in + 2 * nc + n_res]
            kk = pl.program_id(2)

            @pl.when(kk == 0)
            def _():
                acc_ref[...] = part

            @pl.when(kk > 0)
            def _():
                acc_ref[...] += part

            @pl.when(kk == nk - 1)
            def _():
                finish(acc_ref[...])

        def product(a_ref, b_val):
            return _dot(a_ref[...].astype(BF16), b_val.astype(BF16), dims)

        if mode != "tn" and len(a_list) > 1:
            part, off = None, 0
            for a_ref in a_refs:
                w = a_ref.shape[1]
                b_val = b_refs[0][off:off + w, :] if mode == "nn" else b_refs[0][:, off:off + w]
                part = product(a_ref, b_val) if part is None else part + product(a_ref, b_val)
                off += w
            accumulate(part)
        elif len(a_list) > 1 or len(b_list) > 1:
            for pa, a_ref in enumerate(a_refs):
                for pb, b_ref in enumerate(b_refs):
                    picked = ([pl.program_id(0) == pa] if len(a_list) > 1 else []) + (
                        [pl.program_id(1) == pb] if len(b_list) > 1 else [])
                    pl.when(functools.reduce(jnp.logical_and, picked))(
                        functools.partial(lambda a_ref, b_ref: accumulate(product(a_ref, b_ref[...])), a_ref, b_ref))
        else:
            accumulate(product(a_refs[0], b_refs[0][...]))

        if comm:
            @pl.when((step[0] == grid[0] - 1) & (step[1] == grid[1] - 1) & (step[2] == grid[2] - 1))
            def _():
                _exchange_wait(c_src, c_dst, comm[1], sems)

    def held(axis, p):
        def index(i, j, kk):
            return (jnp.where((i, j)[axis] == p, kk, 0), 0)
        return index

    if mode == "tn":
        if len(a_list) > 1:
            a_specs = [pl.BlockSpec((tk, tm), held(0, p)) for p in range(len(a_list))]
        else:
            a_specs = [pl.BlockSpec((tk, tm), lambda i, j, kk: (kk, i))]
    elif len(a_list) > 1:
        a_specs = [pl.BlockSpec((tm, p.shape[1]), lambda i, j, kk: (i, 0)) for p in a_list]
    else:
        a_specs = [pl.BlockSpec((tm, tk), lambda i, j, kk: (i, kk))]
    if mode == "nt":
        b_specs = [pl.BlockSpec((tn, tk), lambda i, j, kk: (j + jo, kk))]
    elif len(b_list) > 1:
        b_specs = [pl.BlockSpec((tk, tn), held(1, p)) for p in range(len(b_list))]
    else:
        b_specs = [pl.BlockSpec((tk, tn), lambda i, j, kk: (kk, j + jo))]
    o_spec = pl.BlockSpec((tm, tn), lambda i, j, kk: (i, j))
    hbm = pl.BlockSpec(memory_space=pl.ANY)
    row_specs = [pl.BlockSpec((tm, tn), functools.partial(lambda i, j, kk, o: (i, j + o), o=off // tn))
                 for (_, off) in epi_rows]
    vec_specs = [pl.BlockSpec(v.shape, functools.partial(lambda i, j, kk, nd: (0,) * nd, nd=v.ndim)) for v in epi_vecs]
    in_specs = a_specs + b_specs + ([o_spec] if res is not None else []) + row_specs + vec_specs + [hbm] * nc
    args = (a_list + b_list + ([res] if res is not None else []) + [r for (r, _) in epi_rows] + list(epi_vecs)
            + (list(comm[0]) if comm else []))
    out_specs = [o_spec] * len(epi_dtypes) + [pl.BlockSpec((1, tn), lambda i, j, kk: (0, j))] * n_sums + [hbm] * nc
    out_shape = ([jax.ShapeDtypeStruct((m, n), dt) for dt in epi_dtypes] + [jax.ShapeDtypeStruct((1, n), F32)] * n_sums
                 + (_exchange_out_shapes(*comm) if comm else []))
    scratch = ([pltpu.VMEM((tm, tn), F32)] if nk > 1 else []) + (_exchange_sems(nc) if comm else [])
    if comm or n_sums or n_ab > 2:
        params = pltpu.CompilerParams(dimension_semantics=("arbitrary",) * 3, vmem_limit_bytes=VMEM_LIMIT,
                                      has_side_effects=bool(comm))
    else:
        params = _cparams(("parallel", "parallel", "arbitrary"))
    outs = pl.pallas_call(
        body, name=name, grid=grid, in_specs=in_specs, out_specs=out_specs, out_shape=out_shape,
        scratch_shapes=scratch, compiler_params=params,
    )(*args)
    return tuple(outs) if (comm or epi) else outs[0]


def _ln_stats(z):
    mu = jnp.mean(z, axis=-1, keepdims=True)
    zc = z - mu
    var = jnp.mean(zc * zc, axis=-1, keepdims=True)
    rstd = lax.rsqrt(var + LN_EPS)
    return zc * rstd, rstd


def _ln_bwd(dy, xhat, rstd, g):
    dxh = dy * g
    m1 = jnp.mean(dxh, axis=-1, keepdims=True)
    m2 = jnp.mean(dxh * xhat, axis=-1, keepdims=True)
    return rstd * (dxh - m1 - xhat * m2)


def _colsum(x):
    return jnp.sum(x, axis=0, keepdims=True)


def _epi_ln_fwd(acc, rows, vecs):
    z = ALPHA * rows[0] + acc
    xhat, _ = _ln_stats(z)
    x1 = xhat * vecs[0] + vecs[1]
    return [z, x1, x1], []


def _epi_ln_loss_bwd(acc, rows, vecs):
    inv_c = 1.0 / acc.shape[-1]
    z = ALPHA * rows[0] + acc
    xhat, rstd = _ln_stats(z)
    d = xhat * vecs[0] + vecs[1] - rows[1]
    dy = d * inv_c
    dz = _ln_bwd(dy, xhat, rstd, vecs[0])
    return [dz, dz], [_colsum(0.5 * d * d * inv_c), _colsum(dy * xhat), _colsum(dy)]


def _epi_ln_bwd(acc, rows, vecs):
    dy = acc + ALPHA * rows[0]
    xhat, rstd = _ln_stats(rows[1])
    dz = _ln_bwd(dy, xhat, rstd, vecs[0])
    return [dz, dz], [_colsum(dy * xhat), _colsum(dy)]


def _epi_gate_bwd(acc, rows, vecs):
    oo, gg = rows
    sg = _sigmoid(gg)
    return [acc * (gg * sg), acc * oo * (sg * (1.0 + gg * (1.0 - sg)))], []


LRU_T = 256
HALO = 8


def _log1p(y):
    u = 1.0 + y
    return jnp.where(u == 1.0, y, jnp.log(u) * (y / (u - 1.0)))


def _lru_gates(c, wa_ref, wx_ref, pv):
    c16 = c.astype(BF16)
    pre_r = jnp.concatenate(
        [_dot(c16[:, n * LRU_BLOCK:(n + 1) * LRU_BLOCK], wa_ref[n], NN) for n in range(LRU_BLOCKS)], axis=1)
    pre_i = jnp.concatenate(
        [_dot(c16[:, n * LRU_BLOCK:(n + 1) * LRU_BLOCK], wx_ref[n], NN) for n in range(LRU_BLOCKS)], axis=1)
    r = _sigmoid(pre_r + pv[5:6])
    ig = _sigmoid(pre_i + pv[6:7])
    lam = pv[7:8]
    ls = jnp.minimum(lam, 0.0) - _log1p(jnp.exp(-jnp.abs(lam)))
    la = LRU_C * r * ls
    a = jnp.exp(la)
    a2 = a * a
    m = jnp.sqrt(jnp.tanh(-la) * (a2 + 1.0))
    return c16, r, ig, ls, la, a, a2, m


def _conv(ext_ref, x, pv, t):
    return (pv[4:5] + pv[3:4] * x + pv[2:3] * ext_ref[pl.ds(HALO - 1, t), :]
            + pv[1:2] * ext_ref[pl.ds(HALO - 2, t), :] + pv[0:1] * ext_ref[pl.ds(HALO - 3, t), :])


def _rglru_fwd(h0a, pvec, wa16, wx16):
    s = h0a.shape[0]
    w = D_MODEL
    t = min(LRU_T, s)
    assert s % t == 0

    def body(ax_ref, ag_ref, pv_ref, wa_ref, wx_ref, ya_ref, h_ref, ext_ref, a_ref, u_ref, hc_ref):
        i = pl.program_id(0)

        @pl.when(i == 0)
        def _():
            ext_ref[pl.ds(0, HALO), :] = jnp.zeros((HALO, w), F32)
            hc_ref[...] = jnp.zeros((1, w), F32)

        pv = pv_ref[...]
        ax = ax_ref[...]
        ext_ref[pl.ds(HALO, t), :] = ax
        c = _conv(ext_ref, ax, pv, t)
        ext_ref[pl.ds(0, HALO), :] = ax[t - HALO:, :]
        _, _, ig, _, _, a, _, m = _lru_gates(c, wa_ref, wx_ref, pv)
        a_ref[...] = a
        u_ref[...] = m * (ig * c)

        def steps(k, h):
            base = pl.multiple_of(k * HALO, HALO)
            for r in range(HALO):
                h = a_ref[pl.ds(base + r, 1), :] * h + u_ref[pl.ds(base + r, 1), :]
                h_ref[pl.ds(base + r, 1), :] = h
            return h

        hc_ref[...] = lax.fori_loop(0, t // HALO, steps, hc_ref[...])
        ag = ag_ref[...]
        ya_ref[...] = (h_ref[...] * (ag * _sigmoid(ag))).astype(BF16)

    full = lambda shp: pl.BlockSpec(shp, lambda i: (0,) * len(shp))
    return pl.pallas_call(
        body, name="rglru_fwd", grid=(s // t,),
        in_specs=[pl.BlockSpec((t, w), lambda i: (i, 0)), pl.BlockSpec((t, w), lambda i: (i, 1)),
                  full((8, w)), full((LRU_BLOCKS, LRU_BLOCK, LRU_BLOCK)), full((LRU_BLOCKS, LRU_BLOCK, LRU_BLOCK))],
        out_specs=[pl.BlockSpec((t, w), lambda i: (i, 0)), pl.BlockSpec((t, w), lambda i: (i, 0))],
        out_shape=[jax.ShapeDtypeStruct((s, w), BF16), jax.ShapeDtypeStruct((s, w), F32)],
        scratch_shapes=[pltpu.VMEM((t + HALO, w), F32), pltpu.VMEM((t, w), F32), pltpu.VMEM((t, w), F32),
                        pltpu.VMEM((1, w), F32)],
        compiler_params=_cparams(("arbitrary",)),
    )(h0a, h0a, pvec, wa16, wx16)


def _rglru_bwd(h0a, h, dh, pvec, wa16, wx16):
    s = h0a.shape[0]
    w = D_MODEL
    t = min(LRU_T, s)
    nb = s // t
    hb = t // HALO

    def body(ax_ref, axh_ref, h_ref, hh_ref, dh_ref, pv_ref, wa_ref, wx_ref,
             dax_ref, dwa_ref, dwx_ref, dpv_ref, ext_ref, hext_ref, dcext_ref, a_ref, g_ref, gc_ref):
        i = pl.program_id(0)
        blk = nb - 1 - i

        @pl.when(i == 0)
        def _():
            dwa_ref[...] = jnp.zeros_like(dwa_ref)
            dwx_ref[...] = jnp.zeros_like(dwx_ref)
            dpv_ref[...] = jnp.zeros_like(dpv_ref)
            gc_ref[...] = jnp.zeros((1, w), F32)
            dcext_ref[pl.ds(t, HALO), :] = jnp.zeros((HALO, w), F32)

        pv = pv_ref[...]
        ax = ax_ref[...]
        keep = (blk > 0).astype(F32)
        ext_ref[pl.ds(0, HALO), :] = axh_ref[...] * keep
        ext_ref[pl.ds(HALO, t), :] = ax
        hext_ref[pl.ds(0, HALO), :] = hh_ref[...] * keep
        hext_ref[pl.ds(HALO, t), :] = h_ref[...]
        c = _conv(ext_ref, ax, pv, t)
        c16, r, ig, ls, _, a, a2, m = _lru_gates(c, wa_ref, wx_ref, pv)
        a_ref[...] = a

        def steps(k, carry):
            base = pl.multiple_of(t - HALO - k * HALO, HALO)
            for r in reversed(range(HALO)):
                g = dh_ref[pl.ds(base + r, 1), :] + carry
                g_ref[pl.ds(base + r, 1), :] = g
                carry = a_ref[pl.ds(base + r, 1), :] * g
            return carry

        gc_ref[...] = lax.fori_loop(0, t // HALO, steps, gc_ref[...])
        g = g_ref[...]
        hprev = hext_ref[pl.ds(HALO - 1, t), :]
        gm = g * m
        d_la = g * hprev * a - (g * ig * c) * a2 / m
        d_pr = d_la * (LRU_C * ls) * r * (1.0 - r)
        d_pi = gm * c * ig * (1.0 - ig)
        dc = gm * ig
        dpr16 = d_pr.astype(BF16)
        dpi16 = d_pi.astype(BF16)
        dc_parts = []
        for n in range(LRU_BLOCKS):
            sl = slice(n * LRU_BLOCK, (n + 1) * LRU_BLOCK)
            dwa_ref[n] += _dot(c16[:, sl], dpr16[:, sl], TN)
            dwx_ref[n] += _dot(c16[:, sl], dpi16[:, sl], TN)
            dc_parts.append(_dot(dpr16[:, sl], wa_ref[n], NT) + _dot(dpi16[:, sl], wx_ref[n], NT))
        dc = dc + jnp.concatenate(dc_parts, axis=1)
        dcext_ref[pl.ds(0, t), :] = dc
        dax = (pv[3:4] * dc + pv[2:3] * dcext_ref[pl.ds(1, t), :] + pv[1:2] * dcext_ref[pl.ds(2, t), :]
               + pv[0:1] * dcext_ref[pl.ds(3, t), :])
        dax_ref[...] = dax.astype(BF16)
        dcext_ref[pl.ds(t, HALO), :] = dc[:HALO, :]
        sums = [_colsum(dc * ext_ref[pl.ds(HALO - 3 + kk, t), :]) for kk in range(4)]
        sums += [_colsum(dc), _colsum(d_pr), _colsum(d_pi), _colsum(d_la * (LRU_C * r))]
        for kk, part in enumerate(sums):
            dpv_ref[pl.ds(kk, 1), :] += part

        @pl.when(i == nb - 1)
        def _():
            lam = pv[7:8]
            dls = dpv_ref[pl.ds(7, 1), :]
            dpv_ref[pl.ds(7, 1), :] = dls * _sigmoid(-lam)

    full = lambda shp: pl.BlockSpec(shp, lambda i: (0,) * len(shp))
    rev = lambda cb: pl.BlockSpec((t, w), functools.partial(lambda i, cb: (nb - 1 - i, cb), cb=cb))
    halo = lambda cb: pl.BlockSpec(
        (HALO, w), functools.partial(lambda i, cb: (jnp.maximum((nb - 1 - i) * hb - 1, 0), cb), cb=cb))
    gw = (LRU_BLOCKS, LRU_BLOCK, LRU_BLOCK)
    return pl.pallas_call(
        body, name="rglru_bwd", grid=(nb,),
        in_specs=[rev(0), halo(0), rev(0), halo(0), rev(0), full((8, w)), full(gw), full(gw)],
        out_specs=[rev(0), full(gw), full(gw), full((8, w))],
        out_shape=[jax.ShapeDtypeStruct((s, w), BF16), jax.ShapeDtypeStruct(gw, F32),
                   jax.ShapeDtypeStruct(gw, F32), jax.ShapeDtypeStruct((8, w), F32)],
        scratch_shapes=[pltpu.VMEM((t + HALO, w), F32), pltpu.VMEM((t + HALO, w), F32),
                        pltpu.VMEM((t + HALO, w), F32), pltpu.VMEM((t, w), F32), pltpu.VMEM((t, w), F32),
                        pltpu.VMEM((1, w), F32)],
        compiler_params=_cparams(("arbitrary",)),
    )(h0a, h0a, h, h, dh, pvec, wa16, wx16)


SB_T = 256
SB_SUB = 4


def _split16(x):
    hi = x.astype(BF16)
    lo = (x - hi.astype(F32)).astype(BF16)
    return jnp.concatenate([hi, lo], axis=0)


def _sb_tile(q, k, scale, tri, run, causal):
    tq = q.shape[0]
    z = _dot(q, k, NT) * scale
    log1mb = -(jnp.maximum(z, 0.0) + jnp.log(1.0 + jnp.exp(-jnp.abs(z))))
    if causal is not None:
        log1mb = jnp.where(causal, log1mb, 0.0)
    cs = _dot(_split16(log1mb), tri, NN)
    cs = cs[:tq] + cs[tq:]
    wgt = jnp.exp(z + cs + run)
    if causal is not None:
        wgt = jnp.where(causal, wgt, 0.0)
    return z, wgt, run + cs[:, 0:1]


SB_DEAD = -105.0


def _sb_alive(run):
    return (jnp.max(run) > SB_DEAD).astype(jnp.int32)


def _sb_more(carry):
    return (carry[0] >= 0) & (carry[1] > 0)


def _sb_fwd(qkv, bg):
    s = qkv.shape[0]
    t = min(SB_T, s)
    nq = s // t
    dh = SB_HEAD_DIM
    scale = 1.0 / math.sqrt(dh)

    sub = SB_SUB if nq % SB_SUB == 0 else 1

    def body(q_ref, k_ref, v_ref, bg_ref, o_ref, y_ref):
        row = lax.broadcasted_iota(jnp.int32, (t, t), 0)
        col = lax.broadcasted_iota(jnp.int32, (t, t), 1)
        tri = jnp.where(row >= col, 1.0, 0.0).astype(BF16)
        causal = col < row
        def tile(q, kb, run, mask):
            off = pl.multiple_of(kb * t, t)
            _, wgt, run = _sb_tile(q, k_ref[pl.ds(off, t), :], scale, tri, run, mask)
            return run, _dot(wgt.astype(BF16), v_ref[pl.ds(off, t), :], NN)

        blocks = []
        for u in range(sub):
            iq = pl.program_id(1) * sub + u
            q = q_ref[pl.ds(u * t, t), :]
            run_d, acc_d = tile(q, iq, jnp.zeros((t, 1), F32), causal)
            run_l, acc_l = tile(q, jnp.maximum(iq - 1, 0), run_d, None)
            blocks.append((iq, q, jnp.where(iq > 0, run_l, run_d), acc_d + jnp.where(iq > 0, acc_l, 0.0)))
        for u, (iq, q, run, acc) in enumerate(blocks):
            def loop(carry, q=q):
                kb, _, run, acc = carry
                run, part = tile(q, kb, run, None)
                return kb - 1, _sb_alive(run), run, acc + part

            _, _, run, acc = lax.while_loop(_sb_more, loop, (iq - 2, _sb_alive(run), run, acc))
            rows = pl.ds(u * t, t)
            o_ref[rows, :] = acc.astype(BF16)
            g = bg_ref[rows, :]
            y_ref[rows, :] = (acc * (g * _sigmoid(g))).astype(BF16)

    blk = lambda off: pl.BlockSpec((sub * t, dh), functools.partial(lambda h, i, off: (i, h + off), off=off))
    col = lambda off: pl.BlockSpec((s, dh), functools.partial(lambda h, i, off: (0, h + off), off=off))
    return pl.pallas_call(
        body, name="sb_fwd", grid=(SB_HEADS, nq // sub),
        in_specs=[blk(0), col(SB_HEADS), col(2 * SB_HEADS), blk(0)],
        out_specs=[blk(0), blk(0)],
        out_shape=[jax.ShapeDtypeStruct((s, SB_HEADS * dh), BF16)] * 2,
        compiler_params=_cparams(("parallel", "arbitrary")),
    )(qkv, qkv, qkv, bg)


def _sb_bwd(qkv, dob):
    s = qkv.shape[0]
    t = min(SB_T, s)
    nq = s // t
    dh = SB_HEAD_DIM
    scale = 1.0 / math.sqrt(dh)

    sub = SB_SUB if nq % SB_SUB == 0 else 1

    def body(q_ref, k_ref, v_ref, do_ref, dq_ref, dk_out, dv_out, e_ref, b_ref, pe_ref, pb_ref, dk_ref, dv_ref):
        @pl.when(pl.program_id(1) == 0)
        def _():
            dk_ref[...] = jnp.zeros_like(dk_ref)
            dv_ref[...] = jnp.zeros_like(dv_ref)

        row = lax.broadcasted_iota(jnp.int32, (t, t), 0)
        col = lax.broadcasted_iota(jnp.int32, (t, t), 1)
        tri = jnp.where(row >= col, 1.0, 0.0).astype(BF16)
        tri_x = jnp.where(row < col, 1.0, 0.0).astype(BF16)
        causal = col < row

        def sweep1(q, do, kb, e_dst, b_dst, run, mask, live=None):
            off = pl.multiple_of(kb * t, t)
            v = v_ref[pl.ds(off, t), :]
            z, wgt, run = _sb_tile(q, k_ref[pl.ds(off, t), :], scale, tri, run, mask)
            if live is not None:
                wgt = jnp.where(live, wgt, 0.0)
            e_dst[...] = wgt * _dot(do, v, NT)
            b_dst[...] = _sigmoid(z)
            dv_ref[pl.ds(off, t), :] += _dot(wgt.astype(BF16), do, TN)
            return run

        def sweep2(q, kb, e_src, b_src, pre, mask, live=None):
            off = pl.multiple_of(kb * t, t)
            e = e_src[...]
            beta = b_src[...]
            ps = _dot(_split16(e), tri_x, NN)
            ps = ps[:t] + ps[t:]
            dz = e * (1.0 - beta) - beta * (ps + pre)
            if mask is not None:
                dz = jnp.where(mask, dz, 0.0)
            if live is not None:
                dz = jnp.where(live, dz, 0.0)
            dz16 = (dz * scale).astype(BF16)
            dk_ref[pl.ds(off, t), :] += _dot(dz16, q, TN)
            return pre + ps[:, t - 1:t] + e[:, t - 1:t], _dot(dz16, k_ref[pl.ds(off, t), :], NN)

        blocks = []
        for u in range(sub):
            iq = pl.program_id(1) * sub + u
            q, do = q_ref[pl.ds(u * t, t), :], do_ref[pl.ds(u * t, t), :]
            left = jnp.maximum(iq - 1, 0)
            run_d = sweep1(q, do, iq, pe_ref.at[u, 0], pb_ref.at[u, 0], jnp.zeros((t, 1), F32), causal)
            run_l = sweep1(q, do, left, pe_ref.at[u, 1], pb_ref.at[u, 1], run_d, None, iq > 0)
            blocks.append((iq, q, do, left, jnp.where(iq > 0, run_l, run_d)))
        carried = []
        for iq, q, do, left, run in blocks:
            def loop1(carry, q=q, do=do):
                run = sweep1(q, do, carry[0], e_ref.at[carry[0]], b_ref.at[carry[0]], carry[2], None)
                return carry[0] - 1, _sb_alive(run), run

            first = lax.while_loop(_sb_more, loop1, (iq - 2, _sb_alive(run), run))[0] + 1

            def loop2(kb, carry, q=q):
                pre, part = sweep2(q, kb, e_ref.at[kb], b_ref.at[kb], carry[0], None)
                return pre, carry[1] + part

            carried.append(lax.fori_loop(first, iq - 1, loop2, (jnp.zeros((t, 1), F32), jnp.zeros((t, dh), F32))))
        for u, ((iq, q, do, left, _), (pre, dq)) in enumerate(zip(blocks, carried)):
            pre, dq_l = sweep2(q, left, pe_ref.at[u, 1], pb_ref.at[u, 1], pre, None, iq > 0)
            _, dq_d = sweep2(q, iq, pe_ref.at[u, 0], pb_ref.at[u, 0], pre, causal)
            dq_ref[pl.ds(u * t, t), :] = (dq + dq_l + dq_d).astype(BF16)

        @pl.when(pl.program_id(1) == nq // sub - 1)
        def _():
            dk_out[...] = dk_ref[...].astype(BF16)
            dv_out[...] = dv_ref[...].astype(BF16)

    blk = lambda off: pl.BlockSpec((sub * t, dh), functools.partial(lambda h, i, off: (i, h + off), off=off))
    col = lambda off: pl.BlockSpec((s, dh), functools.partial(lambda h, i, off: (0, h + off), off=off))
    wide = SB_HEADS * dh
    return pl.pallas_call(
        body, name="sb_bwd", grid=(SB_HEADS, nq // sub),
        in_specs=[blk(0), col(SB_HEADS), col(2 * SB_HEADS), blk(0)],
        out_specs=[blk(0), col(0), col(0)],
        out_shape=[jax.ShapeDtypeStruct((s, wide), BF16)] * 3,
        scratch_shapes=[pltpu.VMEM((nq, t, t), F32), pltpu.VMEM((nq, t, t), F32),
                        pltpu.VMEM((sub, 2, t, t), F32), pltpu.VMEM((sub, 2, t, t), F32),
                        pltpu.VMEM((s, dh), F32), pltpu.VMEM((s, dh), F32)],
        compiler_params=_cparams(("parallel", "arbitrary")),
    )(qkv, qkv, qkv, dob)


def _alibi_slope(h):
    return float(2.0 ** (-8.0 * (h + 1) / C_HEADS))


GROUP_ROWS = C_GROUP * WINDOW


def _swa_window(n):
    qb = WINDOW
    i = lax.broadcasted_iota(jnp.int32, (qb, 2 * qb), 0)
    j = lax.broadcasted_iota(jnp.int32, (qb, 2 * qb), 1)
    d = i - j + qb
    valid = (d >= 0) & (d < WINDOW) & ((j >= qb) | (n > 0))
    stacked = lambda x: jnp.concatenate([x] * C_GROUP, axis=0)
    return stacked(d.astype(F32)), stacked(jnp.where(valid, 0.0, -1e30))


def _swa_group_cols(c, sink_ref):
    head = lax.shift_right_logical(lax.broadcasted_iota(jnp.int32, (GROUP_ROWS, 1), 0), WINDOW.bit_length() - 1)
    slope = jnp.zeros((GROUP_ROWS, 1), F32)
    sink = jnp.zeros((GROUP_ROWS, 1), F32)
    for hh in range(C_GROUP):
        slope = jnp.where(head == hh, _alibi_slope(c * C_GROUP + hh), slope)
        sink = jnp.where(head == hh, sink_ref[c * C_GROUP + hh], sink)
    return slope, sink


def _swa_probs(qg, kw, sink, slope, dist, bias, scale):
    sc = _dot(qg, kw, NT) * scale - slope * dist + bias
    m = jnp.maximum(jnp.max(sc, axis=-1, keepdims=True), sink)
    p = jnp.exp(sc - m)
    ps = jnp.exp(sink - m)
    inv = 1.0 / (jnp.sum(p, axis=-1, keepdims=True) + ps)
    return p * inv, ps * inv


def _stack_heads(x, c):
    hd = C_HEAD_DIM
    return jnp.concatenate([x[:, (c * C_GROUP + hh) * hd:(c * C_GROUP + hh + 1) * hd] for hh in range(C_GROUP)], axis=0)


SWA_SUB = 2


def _swa_kv_specs(sub, kvw):
    return [pl.BlockSpec((WINDOW, 2 * kvw), functools.partial(lambda n, w: (n * sub + w, 0), w=w)) for w in range(sub + 1)]


def _swa_fwd(q16, gate, kvp, sinks):
    s = q16.shape[0]
    qb = WINDOW
    hd = C_HEAD_DIM
    scale = 1.0 / math.sqrt(hd)
    kvw = C_KV_HEADS * hd
    sub = SWA_SUB if (s // qb) % SWA_SUB == 0 else 1

    def body(sink_ref, q_ref, g_ref, *refs):
        kv_refs, (o16_ref, y_ref, o_ref) = refs[:sub + 1], refs[sub + 1:]
        for u in range(sub):
            rows = slice(u * qb, (u + 1) * qb)
            q = q_ref[rows, :]
            kv = jnp.concatenate([kv_refs[u][...], kv_refs[u + 1][...]], axis=0)
            dist, bias = _swa_window(pl.program_id(0) * sub + u)
            for c in range(C_KV_HEADS):
                slope, sink = _swa_group_cols(c, sink_ref)
                kw = kv[:, c * hd:(c + 1) * hd]
                vw = kv[:, kvw + c * hd:kvw + (c + 1) * hd]
                p, _ = _swa_probs(_stack_heads(q, c), kw, sink, slope, dist, bias, scale)
                og = _dot(p.astype(BF16), vw, NN)
                for hh in range(C_GROUP):
                    h = c * C_GROUP + hh
                    o_ref[rows, h * hd:(h + 1) * hd] = og[hh * qb:(hh + 1) * qb, :]
        g = g_ref[...]
        o = o_ref[...]
        o16_ref[...] = o.astype(BF16)
        y_ref[...] = (o * (g * _sigmoid(g))).astype(BF16)

    wide = C_HEADS * hd
    big = pl.BlockSpec((sub * qb, wide), lambda n: (n, 0))
    return pl.pallas_call(
        body, name="swa_fwd", grid=(s // (sub * qb),),
        in_specs=[pl.BlockSpec(memory_space=pltpu.SMEM), big, big] + _swa_kv_specs(sub, kvw),
        out_specs=[big, big],
        out_shape=[jax.ShapeDtypeStruct((s, wide), BF16), jax.ShapeDtypeStruct((s, wide), BF16)],
        scratch_shapes=[pltpu.VMEM((sub * qb, wide), F32)],
        compiler_params=_cparams(("arbitrary",)),
    )(sinks, q16, gate, *([kvp] * (sub + 1)))


def _swa_bwd(q16, do16, kvp, sinks):
    s = q16.shape[0]
    qb = WINDOW
    hd = C_HEAD_DIM
    scale = 1.0 / math.sqrt(hd)
    kvw = C_KV_HEADS * hd
    sub = 1
    nstep = s // (sub * qb)

    def body(sink_ref, q_ref, do_ref, *refs):
        kv_refs, (dq_ref, dkv_ref, ds_ref, sacc_ref) = refs[:sub + 1], refs[sub + 1:]

        @pl.when(pl.program_id(0) == 0)
        def _():
            dkv_ref[...] = jnp.zeros_like(dkv_ref)
            sacc_ref[...] = jnp.zeros_like(sacc_ref)

        for u in range(sub):
            rows = slice(u * qb, (u + 1) * qb)
            n = pl.program_id(0) * sub + u
            q = q_ref[rows, :]
            do = do_ref[rows, :]
            kv = jnp.concatenate([kv_refs[u][...], kv_refs[u + 1][...]], axis=0)
            off = pl.multiple_of(n * qb, qb)
            dist, bias = _swa_window(n)
            for c in range(C_KV_HEADS):
                slope, sink = _swa_group_cols(c, sink_ref)
                kw = kv[:, c * hd:(c + 1) * hd]
                vw = kv[:, kvw + c * hd:kvw + (c + 1) * hd]
                qg = _stack_heads(q, c)
                dog = _stack_heads(do, c)
                p, ps = _swa_probs(qg, kw, sink, slope, dist, bias, scale)
                dp = _dot(dog, vw, NT)
                dd = jnp.sum(p * dp, axis=-1, keepdims=True)
                ds16 = (p * (dp - dd) * scale).astype(BF16)
                sacc_ref[c] += ps * dd
                dqg = _dot(ds16, kw, NN).astype(BF16)
                for hh in range(C_GROUP):
                    h = c * C_GROUP + hh
                    dq_ref[rows, h * hd:(h + 1) * hd] = dqg[hh * qb:(hh + 1) * qb, :]
                dkv_ref[pl.ds(off, 2 * qb), c * hd:(c + 1) * hd] += _dot(ds16, qg, TN)
                dkv_ref[pl.ds(off, 2 * qb), kvw + c * hd:kvw + (c + 1) * hd] += _dot(p.astype(BF16), dog, TN)

        @pl.when(pl.program_id(0) == nstep - 1)
        def _():
            lane = lax.broadcasted_iota(jnp.int32, (1, 128), 1)
            row = jnp.zeros((1, 128), F32)
            for c in range(C_KV_HEADS):
                for hh in range(C_GROUP):
                    tot = jnp.sum(sacc_ref[c, pl.ds(hh * qb, qb), :], axis=0, keepdims=True)
                    row = jnp.where(lane == c * C_GROUP + hh, -tot, row)
            ds_ref[...] = row

    wide = C_HEADS * hd
    big = pl.BlockSpec((sub * qb, wide), lambda n: (n, 0))
    return pl.pallas_call(
        body, name="swa_bwd", grid=(nstep,),
        in_specs=[pl.BlockSpec(memory_space=pltpu.SMEM), big, big] + _swa_kv_specs(sub, kvw),
        out_specs=[big,
                   pl.BlockSpec((s + qb, 2 * kvw), lambda n: (0, 0)), pl.BlockSpec((1, 128), lambda n: (0, 0))],
        out_shape=[jax.ShapeDtypeStruct((s, wide), BF16), jax.ShapeDtypeStruct((s + qb, 2 * kvw), F32),
                   jax.ShapeDtypeStruct((1, 128), F32)],
        scratch_shapes=[pltpu.VMEM((C_KV_HEADS, GROUP_ROWS, 1), F32)],
        compiler_params=_cparams(("arbitrary",)),
    )(sinks, q16, do16, *([kvp] * (sub + 1)))


def _adamw(parts, w, m, v, *, name, tr=496):
    npart, r, c = parts.shape
    tr = min(tr, r)
    assert r % tr == 0 and tr % 16 == 0
    c1 = 1.0 / (1.0 - ADAM_B1 ** ADAM_STEP)
    c2 = 1.0 / (1.0 - ADAM_B2 ** ADAM_STEP)

    def body(p_ref, w_ref, m_ref, v_ref, g_ref, d_ref, nm_ref, nv_ref):
        g = p_ref[0].astype(F32)
        for j in range(1, npart):
            g = g + p_ref[j].astype(F32)
        nm = ADAM_B1 * m_ref[...] + (1.0 - ADAM_B1) * g
        nv = ADAM_B2 * v_ref[...] + (1.0 - ADAM_B2) * (g * g)
        g_ref[...] = g
        nm_ref[...] = nm
        nv_ref[...] = nv
        d_ref[...] = -ADAM_LR * ((nm * c1) / (jnp.sqrt(nv * c2) + ADAM_EPS) + ADAM_WD * w_ref[...])

    spec = pl.BlockSpec((tr, c), lambda i: (i, 0))
    return pl.pallas_call(
        body, name=name, grid=(r // tr,),
        in_specs=[pl.BlockSpec((npart, tr, c), lambda i: (0, i, 0)), spec, spec, spec],
        out_specs=[spec] * 4, out_shape=[jax.ShapeDtypeStruct((r, c), F32)] * 4,
        compiler_params=_cparams(("parallel",)),
    )(parts, w, m, v)


GATHER = "gather"
A2A = "a2a"
GATHER_COLS = "gather_cols"
A2A_COLS = "a2a_cols"


def _exchange(bufs, gather, *, name):
    nb = len(bufs)

    def body(*refs):
        _exchange_start(refs[:nb], refs[nb:2 * nb], gather, refs[2 * nb:])
        _exchange_wait(refs[:nb], refs[nb:2 * nb], gather, refs[2 * nb:])

    hbm = pl.BlockSpec(memory_space=pl.ANY)
    return pl.pallas_call(
        body, name=name, in_specs=[hbm] * nb, out_specs=[hbm] * nb, out_shape=_exchange_out_shapes(bufs, gather),
        scratch_shapes=_exchange_sems(nb),
        compiler_params=pltpu.CompilerParams(has_side_effects=True),
    )(*bufs)


def _gather_two_level(bufs, modes, x, *, tr=1024, name):
    nb = len(bufs)
    rows, cols = x.shape
    tr = min(tr, rows)
    steps = rows // tr

    def body(*refs):
        x_ref, src, x16_ref, dst = refs[0], refs[1:1 + nb], refs[1 + nb], refs[2 + nb:2 + 2 * nb]
        send_sems, recv_sems, local_sems = refs[2 + 2 * nb:]
        cx, cy, cc = lax.axis_index("x"), lax.axis_index("y"), lax.axis_index("c")
        me = 4 * cx + 2 * cy + cc
        here, sibling = (cx, cy, cc), (cx, cy, 1 - cc)

        def copy(b, k, origin, to, source):
            return pltpu.make_async_remote_copy(
                src_ref=source, dst_ref=_slot(dst[b], modes[b], origin), send_sem=send_sems.at[b, k - 1],
                recv_sem=recv_sems.at[b, k - 1], device_id=to, device_id_type=pl.DeviceIdType.MESH)

        def first_copies():
            local = [pltpu.make_async_copy(src[b], _slot(dst[b], modes[b], me), local_sems.at[b]) for b in range(nb)]
            sends = [copy(b, k, me, (cx ^ (k >> 2), cy ^ ((k >> 1) & 1), cc ^ (k & 1)), src[b])
                     for k in (1, 2, 4, 6) for b in range(nb)]
            return local, sends

        @pl.when(pl.program_id(0) == 0)
        def _():
            local, sends = first_copies()
            for cp in local + sends:
                cp.start()

        x16_ref[...] = x_ref[...].astype(BF16)

        @pl.when(pl.program_id(0) == steps - 1)
        def _():
            local, sends = first_copies()
            for j in (2, 4, 6):
                for b in range(nb):
                    copy(b, j, me ^ j, here, src[b]).wait_recv()
                    passed = copy(b, j ^ 1, me ^ j, sibling, _slot(dst[b], modes[b], me ^ j))
                    passed.start()
                    sends.append(passed)
            for k in (1, 3, 5, 7):
                for b in range(nb):
                    copy(b, k, me ^ k, here, src[b]).wait_recv()
            for cp in sends:
                cp.wait_send()
            for cp in local:
                cp.wait()

    hbm = pl.BlockSpec(memory_space=pl.ANY)
    tile = pl.BlockSpec((tr, cols), lambda i: (i, 0))
    return pl.pallas_call(
        body, name=name, grid=(steps,), in_specs=[tile] + [hbm] * nb, out_specs=[tile] + [hbm] * nb,
        out_shape=[jax.ShapeDtypeStruct(x.shape, BF16)] + _exchange_out_shapes(bufs, modes),
        scratch_shapes=_exchange_sems(nb),
        compiler_params=pltpu.CompilerParams(dimension_semantics=("arbitrary",), vmem_limit_bytes=VMEM_LIMIT,
                                             has_side_effects=True),
    )(x, *bufs)


def _exchange_out_shapes(bufs, gather):
    shapes = {GATHER: lambda s: (N_DEV,) + s, A2A: lambda s: s, GATHER_COLS: lambda s: (s[0], N_DEV * s[1]),
              A2A_COLS: lambda s: (N_DEV, s[0], s[1] // N_DEV)}
    return [jax.ShapeDtypeStruct(shapes[g](tuple(b.shape)), b.dtype) for b, g in zip(bufs, gather)]


def _sent(src, mode, peer):
    if mode == A2A:
        return src.at[peer]
    if mode == A2A_COLS:
        w = src.shape[1] // N_DEV
        return src.at[:, pl.ds(pl.multiple_of(peer * w, 128), w)]
    return src


def _slot(dst, mode, dev):
    if mode == GATHER_COLS:
        w = dst.shape[1] // N_DEV
        return dst.at[:, pl.ds(pl.multiple_of(dev * w, 128), w)]
    return dst.at[dev]


def _exchange_sems(nb):
    return [pltpu.SemaphoreType.DMA((nb, N_DEV - 1)), pltpu.SemaphoreType.DMA((nb, N_DEV - 1)),
            pltpu.SemaphoreType.DMA((nb,))]


def _exchange_copies(src, dst, gather, sems):
    send_sems, recv_sems, local_sems = sems
    x, y, c = lax.axis_index("x"), lax.axis_index("y"), lax.axis_index("c")
    me = 4 * x + 2 * y + c
    local, sends, recvs = [], [], []
    for b in range(len(src)):
        mine = _sent(src[b], gather[b], me)
        local.append(pltpu.make_async_copy(mine, _slot(dst[b], gather[b], me), local_sems.at[b]))
        for k in range(1, N_DEV):
            px, py, pc = x ^ (k >> 2), y ^ ((k >> 1) & 1), c ^ (k & 1)
            peer = 4 * px + 2 * py + pc
            pair = dict(send_sem=send_sems.at[b, k - 1], recv_sem=recv_sems.at[b, k - 1],
                        device_id_type=pl.DeviceIdType.MESH)
            sends.append(pltpu.make_async_remote_copy(
                src_ref=_sent(src[b], gather[b], peer), dst_ref=_slot(dst[b], gather[b], me), device_id=(px, py, pc),
                **pair))
            recvs.append(pltpu.make_async_remote_copy(
                src_ref=mine, dst_ref=_slot(dst[b], gather[b], peer), device_id=(x, y, c), **pair))
    return local, sends, recvs


def _exchange_start(src, dst, gather, sems):
    local, sends, _ = _exchange_copies(src, dst, gather, sems)
    for cp in local + sends:
        cp.start()


def _exchange_wait(src, dst, gather, sems):
    local, sends, recvs = _exchange_copies(src, dst, gather, sems)
    for cp in sends:
        cp.wait_send()
    for cp in recvs:
        cp.wait_recv()
    for cp in local:
        cp.wait()


R_SMALL = 16
REPL_ROWS = 272
REPL_LEN = REPL_ROWS * D_MODEL


def _small_block(parts):
    flat = jnp.concatenate(parts, axis=-1)
    lead = flat.ndim - 1
    return jnp.pad(flat[..., None, :], [(0, 0)] * lead + [(0, R_SMALL - 1), (0, D_MODEL - flat.shape[-1])])


def _small_vectors(conv_w, ln_g, ln_b):
    return _small_block([conv_w.reshape(512), ln_g.reshape(128), ln_b.reshape(128)])


REPL_SHAPES = ((1, 8, 128, 128), (1, 8, 128, 128), (1, 1024), (1, 1024), (1, 1024), (1, 1024), (1, 1024), (1, 1024),
               (1, 16))


def _pack_repl(parts):
    flat = jnp.concatenate([p.reshape(-1) for p in parts])
    return jnp.concatenate([flat, jnp.zeros((REPL_LEN - flat.shape[0],), F32)]).reshape(REPL_ROWS, D_MODEL)


def _unpack_repl(p):
    flat = p.reshape(-1)
    out, o = [], 0
    for shp in REPL_SHAPES:
        n = math.prod(shp)
        out.append(flat[o:o + n].reshape(shp))
        o += n
    return out


def kernel(x, e_w_in, e_conv_w, e_conv_b, e_w_gate_a, e_b_gate_a, e_w_gate_x, e_b_gate_x, e_lru_lambda, e_w_out, e_ln_g, e_ln_b, o_w_in, o_sinks, o_w_out, o_ln_g, o_ln_b, loss_target, m_e_w_in, m_e_conv_w, m_e_conv_b, m_e_w_gate_a, m_e_b_gate_a, m_e_w_gate_x, m_e_b_gate_x, m_e_lru_lambda, m_e_w_out, m_e_ln_g, m_e_ln_b, m_o_w_in, m_o_sinks, m_o_w_out, m_o_ln_g, m_o_ln_b, v_e_w_in, v_e_conv_w, v_e_conv_b, v_e_w_gate_a, v_e_b_gate_a, v_e_w_gate_x, v_e_b_gate_x, v_e_lru_lambda, v_e_w_out, v_e_ln_g, v_e_ln_b, v_o_w_in, v_o_sinks, v_o_w_out, v_o_ln_g, v_o_ln_b):
    d = D_MODEL
    x0 = x[0]
    target = loss_target[0]
    s = x0.shape[0]

    spack = _small_block([e_conv_w.reshape(512), o_ln_g.reshape(128), o_ln_b.reshape(128)])
    x0_16, wi_e, sall = _gather_two_level([e_w_in[0].astype(BF16), spack], [GATHER_COLS, GATHER], x0,
                                          name="gather_w_in")
    conv_w = sall[:, 0, 0:512].reshape(N_DEV, 4, 128).transpose(1, 0, 2).reshape(4, d)
    ln_g_o = sall[:, 0, 512:640].reshape(1, d)
    ln_b_o = sall[:, 0, 640:768].reshape(1, d)
    pvec = jnp.concatenate([conv_w, e_conv_b, e_b_gate_a, e_b_gate_x, e_lru_lambda], axis=0)
    wa16 = e_w_gate_a[0].astype(BF16)
    wx16 = e_w_gate_x[0].astype(BF16)
    sinks = o_sinks[0]

    h0a, wall_out_e = _matmul(x0_16, wi_e, mode="nn", n_out=2 * d, b_off=0, name="l0_in_a",
                              comm=([e_w_out[0].astype(BF16)], [GATHER]))
    qkv, wall_in_o = _matmul(x0_16, wi_e, mode="nn", n_out=3 * d, b_off=2 * d, out_dtype=BF16, name="l0_in_qkv",
                             comm=([o_w_in[0].astype(BF16)], [GATHER]))
    bg, wall_out_o = _matmul(x0_16, wi_e, mode="nn", n_out=d, b_off=5 * d, name="l0_in_bg",
                             comm=([o_w_out[0].astype(BF16)], [GATHER]))
    wo_e = wall_out_e.reshape(2 * d, d)
    wi_o = wall_in_o.transpose(1, 0, 2).reshape(d, 2304)
    wi_o = jnp.concatenate([wi_o[:, :1024], wi_o[:, 1280:], wi_o[:, 1024:1280]], axis=1)
    wo_o = wall_out_o.reshape(d, d)
    ya, hst = _rglru_fwd(h0a, pvec, wa16, wx16)
    ob, yb = _sb_fwd(qkv, bg)
    z0, x1, x1_16 = _matmul([ya, yb], wo_e, mode="nn", tm=512, name="l0_out",
                            epi=(_epi_ln_fwd, [(x0, 0)], [e_ln_g, e_ln_b], [F32, F32, BF16], 0))

    q1 = _matmul(x1_16, wi_o, mode="nn", n_out=d, b_off=0, out_dtype=BF16, name="l1_in_q")
    g1 = _matmul(x1_16, wi_o, mode="nn", n_out=d, b_off=d, name="l1_in_g")
    kv1 = _matmul(x1_16, wi_o, mode="nn", n_out=256, b_off=2 * d, tn=256, out_dtype=BF16, name="l1_in_kv")
    kvp = jnp.concatenate([jnp.zeros((WINDOW, 256), BF16), kv1], axis=0)
    o1, y1 = _swa_fwd(q1, g1, kvp, sinks)
    dz1, dz1_16, loss_cols, dg_o, db_o = _matmul(
        y1, wo_o, mode="nn", tm=512, name="l1_out",
        epi=(_epi_ln_loss_bwd, [(x1, 0), (target, 0)], [ln_g_o, ln_b_o], [F32, BF16], 3))
    loss_hi = jnp.sum(loss_cols).astype(BF16).astype(F32)
    loss_terms = jnp.stack([loss_hi, jnp.sum(loss_cols) - loss_hi]).reshape(1, 2)

    dwo_o = _matmul(y1, dz1_16, mode="tn", out_dtype=BF16, name="l1_dwout")
    do1, dg1, parts_out_o = _matmul(dz1_16, wo_o, mode="nt", tm=512, name="l1_dy",
                                    comm=([dwo_o.reshape(N_DEV, 128, d)], [A2A]),
                                    epi=(_epi_gate_bwd, [(o1, 0), (g1, 0)], [], [BF16, BF16], 0))
    dq1, dkvp, dsink = _swa_bwd(q1, do1, kvp, sinks)
    dkv1 = dkvp[WINDOW:].astype(BF16)
    dh1 = [dq1, dg1, dkv1]
    dw_qg = _matmul(x1_16, [dq1, dg1], mode="tn", out_dtype=BF16, name="l1_dwin_qg")
    dw_kv = _matmul(x1_16, dkv1, mode="tn", out_dtype=BF16, name="l1_dwin_kv")
    dwi_o = jnp.concatenate([dw_qg[:, :1024], dw_kv, dw_qg[:, 1024:]], axis=1)
    dz0, dz0_16, dg_e, db_e, parts_in_o = _matmul(
        dh1, wi_o, mode="nt", tm=512, tk=2304, name="l1_dx",
        comm=([dwi_o.reshape(d, N_DEV, 288).transpose(1, 0, 2)], [A2A]),
        epi=(_epi_ln_bwd, [(dz1, 0), (z0, 0)], [e_ln_g], [F32, BF16], 2))

    dwo_e = _matmul([ya, yb], dz0_16, mode="tn", out_dtype=BF16, name="l0_dwout")
    dhst, dag, parts_out_e = _matmul(
        dz0_16, wo_e, mode="nt", tm=512, n_out=d, b_off=0, name="l0_dy_a",
        comm=([dwo_e.reshape(N_DEV, 256, d)], [A2A]),
        epi=(_epi_gate_bwd, [(hst, 0), (h0a, d)], [], [F32, BF16], 0))
    dob, dbg = _matmul(dz0_16, wo_e, mode="nt", tm=512, n_out=d, b_off=d, name="l0_dy_b",
                       epi=(_epi_gate_bwd, [(ob, 0), (bg, 0)], [], [BF16, BF16], 0))
    dq0, dk0, dv0 = _sb_bwd(qkv, dob)
    dax, dwa, dwx, dpv = _rglru_bwd(h0a, hst, dhst, pvec, wa16, wx16)
    dh0 = [dax, dag, dq0, dk0, dv0, dbg]
    repl = _pack_repl([dwa, dwx, dpv[4:5], dpv[5:6], dpv[6:7], dpv[7:8], dg_e, db_e, dsink[:, :C_HEADS], loss_terms])
    dwi_e, parts_repl = _matmul(x0_16, dh0, mode="tn", out_dtype=BF16, name="l0_dwin",
                                comm=([repl.astype(BF16)], [GATHER]))
    small = _small_block([dpv[0:4].reshape(4, N_DEV, 128).transpose(1, 0, 2).reshape(N_DEV, 512),
                          dg_o.reshape(N_DEV, 128), db_o.reshape(N_DEV, 128)]).astype(BF16)
    grad_x, parts_in_e, parts_small = _matmul(dh0, wi_e, mode="nt", tm=512, res=dz0, alpha=ALPHA, name="l0_dx",
                                              comm=([dwi_e, small], [A2A_COLS, A2A]))

    res_in_e = _adamw(parts_in_e, e_w_in[0], m_e_w_in[0], v_e_w_in[0], tr=512, name="adamw_w_in_e")
    res_out_e = _adamw(parts_out_e, e_w_out[0], m_e_w_out[0], v_e_w_out[0], name="adamw_w_out_e")
    res_in_o = _adamw(parts_in_o, o_w_in[0], m_o_w_in[0], v_o_w_in[0], tr=512, name="adamw_w_in_o")
    res_out_o = _adamw(parts_out_o, o_w_out[0], m_o_w_out[0], v_o_w_out[0], name="adamw_w_out_o")
    res_small = _adamw(parts_small, _small_vectors(e_conv_w, o_ln_g, o_ln_b),
                       _small_vectors(m_e_conv_w, m_o_ln_g, m_o_ln_b),
                       _small_vectors(v_e_conv_w, v_o_ln_g, v_o_ln_b), name="adamw_vectors")

    w_r = _pack_repl([e_w_gate_a, e_w_gate_x, e_conv_b, e_b_gate_a, e_b_gate_x, e_lru_lambda, e_ln_g, e_ln_b, o_sinks])
    m_r = _pack_repl([m_e_w_gate_a, m_e_w_gate_x, m_e_conv_b, m_e_b_gate_a, m_e_b_gate_x, m_e_lru_lambda, m_e_ln_g,
                      m_e_ln_b, m_o_sinks])
    v_r = _pack_repl([v_e_w_gate_a, v_e_w_gate_x, v_e_conv_b, v_e_b_gate_a, v_e_b_gate_x, v_e_lru_lambda, v_e_ln_g,
                      v_e_ln_b, v_o_sinks])
    g_r, d_r, nm_r, nv_r = _adamw(parts_repl, w_r, m_r, v_r, name="adamw_replicated")
    loss_at = sum(math.prod(shp) for shp in REPL_SHAPES)
    loss = g_r[loss_at // d, loss_at % d] + g_r[(loss_at + 1) // d, (loss_at + 1) % d]

    def assemble(i, rp):
        vec = res_small[i][0]
        cw, lg_o, lb_o = vec[0:512].reshape(1, 4, 128), vec[512:640].reshape(1, 128), vec[640:768].reshape(1, 128)
        w_a, w_x, cb, b_a, b_x, lam, lg_e, lb_e, snk = _unpack_repl(rp)
        return [res_in_e[i][None], cw, cb, w_a, b_a, w_x, b_x, lam, res_out_e[i][None], lg_e, lb_e,
                res_in_o[i][None], snk, res_out_o[i][None], lg_o, lb_o]

    return (loss, grad_x[None], *assemble(0, g_r), *assemble(1, d_r), *assemble(2, nm_r), *assemble(3, nv_r))
```

```python
import functools
import math

import jax
import jax.numpy as jnp
from jax import lax
from jax.experimental import pallas as pl
from jax.experimental.pallas import tpu as pltpu

F32 = jnp.float32
BF16 = jnp.bfloat16

N_DEV = 8
D_MODEL = 1024
LRU_BLOCKS = 8
LRU_BLOCK = 128
LRU_C = 8.0
SB_HEADS = 8
SB_HEAD_DIM = 128
C_HEADS = 16
C_KV_HEADS = 2
C_GROUP = 8
C_HEAD_DIM = 64
WINDOW = 128
DEPTH = 2
ALPHA = float((2 * DEPTH) ** 0.25)
LN_EPS = 1e-5
ADAM_LR = 0.001
ADAM_B1 = 0.9
ADAM_B2 = 0.999
ADAM_EPS = 1e-08
ADAM_WD = 0.01
ADAM_STEP = 10

VMEM_LIMIT = 56 * 1024 * 1024

NN = ((1,), (0,))
NT = ((1,), (1,))
TN = ((0,), (0,))


def _dot(a, b, dims):
    return lax.dot_general(a, b, (dims, ((), ())), preferred_element_type=F32)


def _sigmoid(x):
    return 0.5 * jnp.tanh(0.5 * x) + 0.5


def _cparams(sem, vmem=VMEM_LIMIT):
    return pltpu.CompilerParams(dimension_semantics=sem, vmem_limit_bytes=vmem)


def _matmul(a, b, *, mode, n_out=None, b_off=0, out_dtype=F32, res=None, alpha=1.0,
            tm=1024, tn=1024, tk=1024, comm=None, epi=None, name):
    a_list = list(a) if isinstance(a, (list, tuple)) else [a]
    b_list = list(b) if isinstance(b, (list, tuple)) else [b]
    if mode == "tn":
        k = a_list[0].shape[0]
        m = sum(p.shape[1] for p in a_list)
        n = n_out if n_out is not None else sum(p.shape[1] for p in b_list)
        assert all(p.shape[1] == tm for p in a_list) or len(a_list) == 1
        assert all(p.shape[1] == tn for p in b_list) or len(b_list) == 1
    else:
        assert len(b_list) == 1
        m = a_list[0].shape[0]
        k = sum(p.shape[1] for p in a_list)
        n = n_out if n_out is not None else (b.shape[0] if mode == "nt" else b.shape[1])
        if len(a_list) > 1:
            tk = k
    tm, tn, tk = min(tm, m), min(tn, n), min(tk, k)
    assert m % tm == 0 and n % tn == 0 and k % tk == 0 and b_off % tn == 0
    grid = (m // tm, n // tn, k // tk)
    nk = grid[2]
    jo = b_off // tn
    dims = {"nn": NN, "nt": NT, "tn": TN}[mode]
    nc = len(comm[0]) if comm else 0
    if epi:
        assert res is None
        epi_fn, epi_rows, epi_vecs, epi_dtypes, n_sums = epi
        assert n_sums == 0 or grid[1] == 1
    else:
        epi_rows, epi_vecs, epi_dtypes, n_sums = [], [], [out_dtype], 0
    n_ab = len(a_list) + len(b_list)
    n_in = n_ab + (res is not None) + len(epi_rows) + len(epi_vecs)
    n_res = len(epi_dtypes) + n_sums

    def body(*refs):
        a_refs, b_refs = refs[:len(a_list)], refs[len(a_list):n_ab]
        r_ref = refs[n_ab] if res is not None else None
        row_refs = refs[n_in - len(epi_rows) - len(epi_vecs):n_in - len(epi_vecs)]
        vec_refs = refs[n_in - len(epi_vecs):n_in]
        o_refs = refs[n_in + nc:n_in + nc + len(epi_dtypes)]
        sum_refs = refs[n_in + nc + len(epi_dtypes):n_in + nc + n_res]
        if comm:
            step = [pl.program_id(ax) for ax in range(3)]
            c_src = refs[n_in:n_in + nc]
            c_dst = refs[n_in + nc + n_res:n_in + 2 * nc + n_res]
            sems = refs[len(refs) - 3:]

            @pl.when((step[0] == 0) & (step[1] == 0) & (step[2] == 0))
            def _():
                _exchange_start(c_src, c_dst, comm[1], sems)

        def finish(out):
            if not epi:
                if r_ref is not None:
                    out = out + alpha * r_ref[...]
                o_refs[0][...] = out.astype(out_dtype)
                return
            outs, sums = epi_fn(out, [r[...] for r in row_refs], [v[...] for v in vec_refs])
            for o_ref, o in zip(o_refs, outs):
                o_ref[...] = o.astype(o_ref.dtype)
            if sum_refs:
                first_rows = pl.program_id(0) == 0

                @pl.when(first_rows)
                def _():
                    for s_ref, part_sum in zip(sum_refs, sums):
                        s_ref[...] = part_sum

                @pl.when(jnp.logical_not(first_rows))
                def _():
                    for s_ref, part_sum in zip(sum_refs, sums):
                        s_ref[...] += part_sum

        def accumulate(part):
            if nk == 1:
                finish(part)
                return
            acc_ref = refs[n_in + 2 * nc + n_res]
            kk = pl.program_id(2)

            @pl.when(kk == 0)
            def _():
                acc_ref[...] = part

            @pl.when(kk > 0)
            def _():
                acc_ref[...] += part

            @pl.when(kk == nk - 1)
            def _():
                finish(acc_ref[...])

        def product(a_ref, b_val):
            return _dot(a_ref[...].astype(BF16), b_val.astype(BF16), dims)

        if mode != "tn" and len(a_list) > 1:
            part, off = None, 0
            for a_ref in a_refs:
                w = a_ref.shape[1]
                b_val = b_refs[0][off:off + w, :] if mode == "nn" else b_refs[0][:, off:off + w]
                part = product(a_ref, b_val) if part is None else part + product(a_ref, b_val)
                off += w
            accumulate(part)
        elif len(a_list) > 1 or len(b_list) > 1:
            for pa, a_ref in enumerate(a_refs):
                for pb, b_ref in enumerate(b_refs):
                    picked = ([pl.program_id(0) == pa] if len(a_list) > 1 else []) + (
                        [pl.program_id(1) == pb] if len(b_list) > 1 else [])
                    pl.when(functools.reduce(jnp.logical_and, picked))(
                        functools.partial(lambda a_ref, b_ref: accumulate(product(a_ref, b_ref[...])), a_ref, b_ref))
        else:
            accumulate(product(a_refs[0], b_refs[0][...]))

        if comm:
            @pl.when((step[0] == grid[0] - 1) & (step[1] == grid[1] - 1) & (step[2] == grid[2] - 1))
            def _():
                _exchange_wait(c_src, c_dst, comm[1], sems)

    def held(axis, p):
        def index(i, j, kk):
            return (jnp.where((i, j)[axis] == p, kk, 0), 0)
        return index

    if mode == "tn":
        if len(a_list) > 1:
            a_specs = [pl.BlockSpec((tk, tm), held(0, p)) for p in range(len(a_list))]
        else:
            a_specs = [pl.BlockSpec((tk, tm), lambda i, j, kk: (kk, i))]
    elif len(a_list) > 1:
        a_specs = [pl.BlockSpec((tm, p.shape[1]), lambda i, j, kk: (i, 0)) for p in a_list]
    else:
        a_specs = [pl.BlockSpec((tm, tk), lambda i, j, kk: (i, kk))]
    if mode == "nt":
        b_specs = [pl.BlockSpec((tn, tk), lambda i, j, kk: (j + jo, kk))]
    elif len(b_list) > 1:
        b_specs = [pl.BlockSpec((tk, tn), held(1, p)) for p in range(len(b_list))]
    else:
        b_specs = [pl.BlockSpec((tk, tn), lambda i, j, kk: (kk, j + jo))]
    o_spec = pl.BlockSpec((tm, tn), lambda i, j, kk: (i, j))
    hbm = pl.BlockSpec(memory_space=pl.ANY)
    row_specs = [pl.BlockSpec((tm, tn), functools.partial(lambda i, j, kk, o: (i, j + o), o=off // tn))
                 for (_, off) in epi_rows]
    vec_specs = [pl.BlockSpec(v.shape, functools.partial(lambda i, j, kk, nd: (0,) * nd, nd=v.ndim)) for v in epi_vecs]
    in_specs = a_specs + b_specs + ([o_spec] if res is not None else []) + row_specs + vec_specs + [hbm] * nc
    args = (a_list + b_list + ([res] if res is not None else []) + [r for (r, _) in epi_rows] + list(epi_vecs)
            + (list(comm[0]) if comm else []))
    out_specs = [o_spec] * len(epi_dtypes) + [pl.BlockSpec((1, tn), lambda i, j, kk: (0, j))] * n_sums + [hbm] * nc
    out_shape = ([jax.ShapeDtypeStruct((m, n), dt) for dt in epi_dtypes] + [jax.ShapeDtypeStruct((1, n), F32)] * n_sums
                 + (_exchange_out_shapes(*comm) if comm else []))
    scratch = ([pltpu.VMEM((tm, tn), F32)] if nk > 1 else []) + (_exchange_sems(nc) if comm else [])
    if comm or n_sums or n_ab > 2:
        params = pltpu.CompilerParams(dimension_semantics=("arbitrary",) * 3, vmem_limit_bytes=VMEM_LIMIT,
                                      has_side_effects=bool(comm))
    else:
        params = _cparams(("parallel", "parallel", "arbitrary"))
    outs = pl.pallas_call(
        body, name=name, grid=grid, in_specs=in_specs, out_specs=out_specs, out_shape=out_shape,
        scratch_shapes=scratch, compiler_params=params,
    )(*args)
    return tuple(outs) if (comm or epi) else outs[0]


def _ln_stats(z):
    mu = jnp.mean(z, axis=-1, keepdims=True)
    zc = z - mu
    var = jnp.mean(zc * zc, axis=-1, keepdims=True)
    rstd = lax.rsqrt(var + LN_EPS)
    return zc * rstd, rstd


def _ln_bwd(dy, xhat, rstd, g):
    dxh = dy * g
    m1 = jnp.mean(dxh, axis=-1, keepdims=True)
    m2 = jnp.mean(dxh * xhat, axis=-1, keepdims=True)
    return rstd * (dxh - m1 - xhat * m2)


def _colsum(x):
    return jnp.sum(x, axis=0, keepdims=True)


def _epi_ln_fwd(acc, rows, vecs):
    z = ALPHA * rows[0] + acc
    xhat, _ = _ln_stats(z)
    x1 = xhat * vecs[0] + vecs[1]
    return [z, x1, x1], []


def _epi_ln_loss_bwd(acc, rows, vecs):
    inv_c = 1.0 / acc.shape[-1]
    z = ALPHA * rows[0] + acc
    xhat, rstd = _ln_stats(z)
    d = xhat * vecs[0] + vecs[1] - rows[1]
    dy = d * inv_c
    dz = _ln_bwd(dy, xhat, rstd, vecs[0])
    return [dz, dz], [_colsum(0.5 * d * d * inv_c), _colsum(dy * xhat), _colsum(dy)]


def _epi_ln_bwd(acc, rows, vecs):
    dy = acc + ALPHA * rows[0]
    xhat, rstd = _ln_stats(rows[1])
    dz = _ln_bwd(dy, xhat, rstd, vecs[0])
    return [dz, dz], [_colsum(dy * xhat), _colsum(dy)]


def _epi_gate_bwd(acc, rows, vecs):
    oo, gg = rows
    sg = _sigmoid(gg)
    return [acc * (gg * sg), acc * oo * (sg * (1.0 + gg * (1.0 - sg)))], []


LRU_T = 256
HALO = 8


def _log1p(y):
    u = 1.0 + y
    return jnp.where(u == 1.0, y, jnp.log(u) * (y / (u - 1.0)))


def _lru_gates(c, wa_ref, wx_ref, pv):
    c16 = c.astype(BF16)
    pre_r = jnp.concatenate(
        [_dot(c16[:, n * LRU_BLOCK:(n + 1) * LRU_BLOCK], wa_ref[n], NN) for n in range(LRU_BLOCKS)], axis=1)
    pre_i = jnp.concatenate(
        [_dot(c16[:, n * LRU_BLOCK:(n + 1) * LRU_BLOCK], wx_ref[n], NN) for n in range(LRU_BLOCKS)], axis=1)
    r = _sigmoid(pre_r + pv[5:6])
    ig = _sigmoid(pre_i + pv[6:7])
    lam = pv[7:8]
    ls = jnp.minimum(lam, 0.0) - _log1p(jnp.exp(-jnp.abs(lam)))
    la = LRU_C * r * ls
    a = jnp.exp(la)
    a2 = a * a
    m = jnp.sqrt(jnp.tanh(-la) * (a2 + 1.0))
    return c16, r, ig, ls, la, a, a2, m


def _conv(ext_ref, x, pv, t):
    return (pv[4:5] + pv[3:4] * x + pv[2:3] * ext_ref[pl.ds(HALO - 1, t), :]
            + pv[1:2] * ext_ref[pl.ds(HALO - 2, t), :] + pv[0:1] * ext_ref[pl.ds(HALO - 3, t), :])


def _rglru_fwd(h0a, pvec, wa16, wx16):
    s = h0a.shape[0]
    w = D_MODEL
    t = min(LRU_T, s)
    assert s % t == 0

    def body(ax_ref, ag_ref, pv_ref, wa_ref, wx_ref, ya_ref, h_ref, ext_ref, a_ref, u_ref, hc_ref):
        i = pl.program_id(0)

        @pl.when(i == 0)
        def _():
            ext_ref[pl.ds(0, HALO), :] = jnp.zeros((HALO, w), F32)
            hc_ref[...] = jnp.zeros((1, w), F32)

        pv = pv_ref[...]
        ax = ax_ref[...]
        ext_ref[pl.ds(HALO, t), :] = ax
        c = _conv(ext_ref, ax, pv, t)
        ext_ref[pl.ds(0, HALO), :] = ax[t - HALO:, :]
        _, _, ig, _, _, a, _, m = _lru_gates(c, wa_ref, wx_ref, pv)
        a_ref[...] = a
        u_ref[...] = m * (ig * c)

        def steps(k, h):
            base = pl.multiple_of(k * HALO, HALO)
            for r in range(HALO):
                h = a_ref[pl.ds(base + r, 1), :] * h + u_ref[pl.ds(base + r, 1), :]
                h_ref[pl.ds(base + r, 1), :] = h
            return h

        hc_ref[...] = lax.fori_loop(0, t // HALO, steps, hc_ref[...])
        ag = ag_ref[...]
        ya_ref[...] = (h_ref[...] * (ag * _sigmoid(ag))).astype(BF16)

    full = lambda shp: pl.BlockSpec(shp, lambda i: (0,) * len(shp))
    return pl.pallas_call(
        body, name="rglru_fwd", grid=(s // t,),
        in_specs=[pl.BlockSpec((t, w), lambda i: (i, 0)), pl.BlockSpec((t, w), lambda i: (i, 1)),
                  full((8, w)), full((LRU_BLOCKS, LRU_BLOCK, LRU_BLOCK)), full((LRU_BLOCKS, LRU_BLOCK, LRU_BLOCK))],
        out_specs=[pl.BlockSpec((t, w), lambda i: (i, 0)), pl.BlockSpec((t, w), lambda i: (i, 0))],
        out_shape=[jax.ShapeDtypeStruct((s, w), BF16), jax.ShapeDtypeStruct((s, w), F32)],
        scratch_shapes=[pltpu.VMEM((t + HALO, w), F32), pltpu.VMEM((t, w), F32), pltpu.VMEM((t, w), F32),
                        pltpu.VMEM((1, w), F32)],
        compiler_params=_cparams(("arbitrary",)),
    )(h0a, h0a, pvec, wa16, wx16)


def _rglru_bwd(h0a, h, dh, pvec, wa16, wx16):
    s = h0a.shape[0]
    w = D_MODEL
    t = min(LRU_T, s)
    nb = s // t
    hb = t // HALO

    def body(ax_ref, axh_ref, h_ref, hh_ref, dh_ref, pv_ref, wa_ref, wx_ref,
             dax_ref, dwa_ref, dwx_ref, dpv_ref, ext_ref, hext_ref, dcext_ref, a_ref, g_ref, gc_ref):
        i = pl.program_id(0)
        blk = nb - 1 - i

        @pl.when(i == 0)
        def _():
            dwa_ref[...] = jnp.zeros_like(dwa_ref)
            dwx_ref[...] = jnp.zeros_like(dwx_ref)
            dpv_ref[...] = jnp.zeros_like(dpv_ref)
            gc_ref[...] = jnp.zeros((1, w), F32)
            dcext_ref[pl.ds(t, HALO), :] = jnp.zeros((HALO, w), F32)

        pv = pv_ref[...]
        ax = ax_ref[...]
        keep = (blk > 0).astype(F32)
        ext_ref[pl.ds(0, HALO), :] = axh_ref[...] * keep
        ext_ref[pl.ds(HALO, t), :] = ax
        hext_ref[pl.ds(0, HALO), :] = hh_ref[...] * keep
        hext_ref[pl.ds(HALO, t), :] = h_ref[...]
        c = _conv(ext_ref, ax, pv, t)
        c16, r, ig, ls, _, a, a2, m = _lru_gates(c, wa_ref, wx_ref, pv)
        a_ref[...] = a

        def steps(k, carry):
            base = pl.multiple_of(t - HALO - k * HALO, HALO)
            for r in reversed(range(HALO)):
                g = dh_ref[pl.ds(base + r, 1), :] + carry
                g_ref[pl.ds(base + r, 1), :] = g
                carry = a_ref[pl.ds(base + r, 1), :] * g
            return carry

        gc_ref[...] = lax.fori_loop(0, t // HALO, steps, gc_ref[...])
        g = g_ref[...]
        hprev = hext_ref[pl.ds(HALO - 1, t), :]
        gm = g * m
        d_la = g * hprev * a - (g * ig * c) * a2 / m
        d_pr = d_la * (LRU_C * ls) * r * (1.0 - r)
        d_pi = gm * c * ig * (1.0 - ig)
        dc = gm * ig
        dpr16 = d_pr.astype(BF16)
        dpi16 = d_pi.astype(BF16)
        dc_parts = []
        for n in range(LRU_BLOCKS):
            sl = slice(n * LRU_BLOCK, (n + 1) * LRU_BLOCK)
            dwa_ref[n] += _dot(c16[:, sl], dpr16[:, sl], TN)
            dwx_ref[n] += _dot(c16[:, sl], dpi16[:, sl], TN)
            dc_parts.append(_dot(dpr16[:, sl], wa_ref[n], NT) + _dot(dpi16[:, sl], wx_ref[n], NT))
        dc = dc + jnp.concatenate(dc_parts, axis=1)
        dcext_ref[pl.ds(0, t), :] = dc
        dax = (pv[3:4] * dc + pv[2:3] * dcext_ref[pl.ds(1, t), :] + pv[1:2] * dcext_ref[pl.ds(2, t), :]
               + pv[0:1] * dcext_ref[pl.ds(3, t), :])
        dax_ref[...] = dax.astype(BF16)
        dcext_ref[pl.ds(t, HALO), :] = dc[:HALO, :]
        sums = [_colsum(dc * ext_ref[pl.ds(HALO - 3 + kk, t), :]) for kk in range(4)]
        sums += [_colsum(dc), _colsum(d_pr), _colsum(d_pi), _colsum(d_la * (LRU_C * r))]
        for kk, part in enumerate(sums):
            dpv_ref[pl.ds(kk, 1), :] += part

        @pl.when(i == nb - 1)
        def _():
            lam = pv[7:8]
            dls = dpv_ref[pl.ds(7, 1), :]
            dpv_ref[pl.ds(7, 1), :] = dls * _sigmoid(-lam)

    full = lambda shp: pl.BlockSpec(shp, lambda i: (0,) * len(shp))
    rev = lambda cb: pl.BlockSpec((t, w), functools.partial(lambda i, cb: (nb - 1 - i, cb), cb=cb))
    halo = lambda cb: pl.BlockSpec(
        (HALO, w), functools.partial(lambda i, cb: (jnp.maximum((nb - 1 - i) * hb - 1, 0), cb), cb=cb))
    gw = (LRU_BLOCKS, LRU_BLOCK, LRU_BLOCK)
    return pl.pallas_call(
        body, name="rglru_bwd", grid=(nb,),
        in_specs=[rev(0), halo(0), rev(0), halo(0), rev(0), full((8, w)), full(gw), full(gw)],
        out_specs=[rev(0), full(gw), full(gw), full((8, w))],
        out_shape=[jax.ShapeDtypeStruct((s, w), BF16), jax.ShapeDtypeStruct(gw, F32),
                   jax.ShapeDtypeStruct(gw, F32), jax.ShapeDtypeStruct((8, w), F32)],
        scratch_shapes=[pltpu.VMEM((t + HALO, w), F32), pltpu.VMEM((t + HALO, w), F32),
                        pltpu.VMEM((t + HALO, w), F32), pltpu.VMEM((t, w), F32), pltpu.VMEM((t, w), F32),
                        pltpu.VMEM((1, w), F32)],
        compiler_params=_cparams(("arbitrary",)),
    )(h0a, h0a, h, h, dh, pvec, wa16, wx16)


SB_T = 256
SB_SUB = 8


def _split16(x):
    hi = x.astype(BF16)
    lo = (x - hi.astype(F32)).astype(BF16)
    return jnp.concatenate([hi, lo], axis=0)


def _sb_tile(q, k, scale, tri, run, causal):
    tq = q.shape[0]
    z = _dot(q, k, NT) * scale
    log1mb = -(jnp.maximum(z, 0.0) + jnp.log(1.0 + jnp.exp(-jnp.abs(z))))
    if causal is not None:
        log1mb = jnp.where(causal, log1mb, 0.0)
    cs = _dot(_split16(log1mb), tri, NN)
    cs = cs[:tq] + cs[tq:]
    wgt = jnp.exp(z + cs + run)
    if causal is not None:
        wgt = jnp.where(causal, wgt, 0.0)
    return z, wgt, run + cs[:, 0:1]


SB_DEAD = -105.0


def _sb_alive(run):
    return (jnp.max(run) > SB_DEAD).astype(jnp.int32)


def _sb_more(carry):
    return (carry[0] >= 0) & (carry[1] > 0)


def _sb_fwd(qkv, bg):
    s = qkv.shape[0]
    t = min(SB_T, s)
    nq = s // t
    dh = SB_HEAD_DIM
    scale = 1.0 / math.sqrt(dh)

    sub = SB_SUB if nq % SB_SUB == 0 else 1

    def body(q_ref, k_ref, v_ref, bg_ref, o_ref, y_ref):
        row = lax.broadcasted_iota(jnp.int32, (t, t), 0)
        col = lax.broadcasted_iota(jnp.int32, (t, t), 1)
        tri = jnp.where(row >= col, 1.0, 0.0).astype(BF16)
        causal = col < row
        def tile(q, kb, run, mask):
            off = pl.multiple_of(kb * t, t)
            _, wgt, run = _sb_tile(q, k_ref[pl.ds(off, t), :], scale, tri, run, mask)
            return run, _dot(wgt.astype(BF16), v_ref[pl.ds(off, t), :], NN)

        blocks = []
        for u in range(sub):
            iq = pl.program_id(1) * sub + u
            q = q_ref[pl.ds(u * t, t), :]
            run_d, acc_d = tile(q, iq, jnp.zeros((t, 1), F32), causal)
            run_l, acc_l = tile(q, jnp.maximum(iq - 1, 0), run_d, None)
            blocks.append((iq, q, jnp.where(iq > 0, run_l, run_d), acc_d + jnp.where(iq > 0, acc_l, 0.0)))
        for u, (iq, q, run, acc) in enumerate(blocks):
            def loop(carry, q=q):
                kb, _, run, acc = carry
                run, part = tile(q, kb, run, None)
                return kb - 1, _sb_alive(run), run, acc + part

            _, _, run, acc = lax.while_loop(_sb_more, loop, (iq - 2, _sb_alive(run), run, acc))
            rows = pl.ds(u * t, t)
            o_ref[rows, :] = acc
            g = bg_ref[rows, :]
            y_ref[rows, :] = (acc * (g * _sigmoid(g))).astype(BF16)

    blk = lambda off: pl.BlockSpec((sub * t, dh), functools.partial(lambda h, i, off: (i, h + off), off=off))
    col = lambda off: pl.BlockSpec((s, dh), functools.partial(lambda h, i, off: (0, h + off), off=off))
    return pl.pallas_call(
        body, name="sb_fwd", grid=(SB_HEADS, nq // sub),
        in_specs=[blk(0), col(SB_HEADS), col(2 * SB_HEADS), blk(0)],
        out_specs=[blk(0), blk(0)],
        out_shape=[jax.ShapeDtypeStruct((s, SB_HEADS * dh), F32), jax.ShapeDtypeStruct((s, SB_HEADS * dh), BF16)],
        compiler_params=_cparams(("parallel", "arbitrary")),
    )(qkv, qkv, qkv, bg)


def _sb_bwd(qkv, dob):
    s = qkv.shape[0]
    t = min(SB_T, s)
    nq = s // t
    dh = SB_HEAD_DIM
    scale = 1.0 / math.sqrt(dh)

    sub = SB_SUB if nq % SB_SUB == 0 else 1

    def body(q_ref, k_ref, v_ref, do_ref, dq_ref, dk_out, dv_out, e_ref, b_ref, pe_ref, pb_ref, dk_ref, dv_ref):
        @pl.when(pl.program_id(1) == 0)
        def _():
            dk_ref[...] = jnp.zeros_like(dk_ref)
            dv_ref[...] = jnp.zeros_like(dv_ref)

        row = lax.broadcasted_iota(jnp.int32, (t, t), 0)
        col = lax.broadcasted_iota(jnp.int32, (t, t), 1)
        tri = jnp.where(row >= col, 1.0, 0.0).astype(BF16)
        tri_x = jnp.where(row < col, 1.0, 0.0).astype(BF16)
        causal = col < row

        def sweep1(q, do, kb, e_dst, b_dst, run, mask, live=None):
            off = pl.multiple_of(kb * t, t)
            v = v_ref[pl.ds(off, t), :]
            z, wgt, run = _sb_tile(q, k_ref[pl.ds(off, t), :], scale, tri, run, mask)
            if live is not None:
                wgt = jnp.where(live, wgt, 0.0)
            e_dst[...] = wgt * _dot(do, v, NT)
            b_dst[...] = _sigmoid(z)
            dv_ref[pl.ds(off, t), :] += _dot(wgt.astype(BF16), do, TN)
            return run

        def sweep2(q, kb, e_src, b_src, pre, mask, live=None):
            off = pl.multiple_of(kb * t, t)
            e = e_src[...]
            beta = b_src[...]
            ps = _dot(_split16(e), tri_x, NN)
            ps = ps[:t] + ps[t:]
            dz = e * (1.0 - beta) - beta * (ps + pre)
            if mask is not None:
                dz = jnp.where(mask, dz, 0.0)
            if live is not None:
                dz = jnp.where(live, dz, 0.0)
            dz16 = (dz * scale).astype(BF16)
            dk_ref[pl.ds(off, t), :] += _dot(dz16, q, TN)
            return pre + ps[:, t - 1:t] + e[:, t - 1:t], _dot(dz16, k_ref[pl.ds(off, t), :], NN)

        blocks = []
        for u in range(sub):
            iq = pl.program_id(1) * sub + u
            q, do = q_ref[pl.ds(u * t, t), :], do_ref[pl.ds(u * t, t), :]
            left = jnp.maximum(iq - 1, 0)
            run_d = sweep1(q, do, iq, pe_ref.at[u, 0], pb_ref.at[u, 0], jnp.zeros((t, 1), F32), causal)
            run_l = sweep1(q, do, left, pe_ref.at[u, 1], pb_ref.at[u, 1], run_d, None, iq > 0)
            blocks.append((iq, q, do, left, jnp.where(iq > 0, run_l, run_d)))
        carried = []
        for iq, q, do, left, run in blocks:
            def loop1(carry, q=q, do=do):
                run = sweep1(q, do, carry[0], e_ref.at[carry[0]], b_ref.at[carry[0]], carry[2], None)
                return carry[0] - 1, _sb_alive(run), run

            first = lax.while_loop(_sb_more, loop1, (iq - 2, _sb_alive(run), run))[0] + 1

            def loop2(kb, carry, q=q):
                pre, part = sweep2(q, kb, e_ref.at[kb], b_ref.at[kb], carry[0], None)
                return pre, carry[1] + part

            carried.append(lax.fori_loop(first, iq - 1, loop2, (jnp.zeros((t, 1), F32), jnp.zeros((t, dh), F32))))
        for u, ((iq, q, do, left, _), (pre, dq)) in enumerate(zip(blocks, carried)):
            pre, dq_l = sweep2(q, left, pe_ref.at[u, 1], pb_ref.at[u, 1], pre, None, iq > 0)
            _, dq_d = sweep2(q, iq, pe_ref.at[u, 0], pb_ref.at[u, 0], pre, causal)
            dq_ref[pl.ds(u * t, t), :] = (dq + dq_l + dq_d).astype(BF16)

        @pl.when(pl.program_id(1) == nq // sub - 1)
        def _():
            dk_out[...] = dk_ref[...].astype(BF16)
            dv_out[...] = dv_ref[...].astype(BF16)

    blk = lambda off: pl.BlockSpec((sub * t, dh), functools.partial(lambda h, i, off: (i, h + off), off=off))
    col = lambda off: pl.BlockSpec((s, dh), functools.partial(lambda h, i, off: (0, h + off), off=off))
    wide = SB_HEADS * dh
    return pl.pallas_call(
        body, name="sb_bwd", grid=(SB_HEADS, nq // sub),
        in_specs=[blk(0), col(SB_HEADS), col(2 * SB_HEADS), blk(0)],
        out_specs=[blk(0), col(0), col(0)],
        out_shape=[jax.ShapeDtypeStruct((s, wide), BF16)] * 3,
        scratch_shapes=[pltpu.VMEM((nq, t, t), F32), pltpu.VMEM((nq, t, t), F32),
                        pltpu.VMEM((sub, 2, t, t), F32), pltpu.VMEM((sub, 2, t, t), F32),
                        pltpu.VMEM((s, dh), F32), pltpu.VMEM((s, dh), F32)],
        compiler_params=_cparams(("parallel", "arbitrary")),
    )(qkv, qkv, qkv, dob)


def _alibi_slope(h):
    return float(2.0 ** (-8.0 * (h + 1) / C_HEADS))


GROUP_ROWS = C_GROUP * WINDOW


def _swa_window(n):
    qb = WINDOW
    i = lax.broadcasted_iota(jnp.int32, (qb, 2 * qb), 0)
    j = lax.broadcasted_iota(jnp.int32, (qb, 2 * qb), 1)
    d = i - j + qb
    valid = (d >= 0) & (d < WINDOW) & ((j >= qb) | (n > 0))
    stacked = lambda x: jnp.concatenate([x] * C_GROUP, axis=0)
    return stacked(d.astype(F32)), stacked(jnp.where(valid, 0.0, -1e30))


def _swa_group_cols(c, sink_ref):
    head = lax.shift_right_logical(lax.broadcasted_iota(jnp.int32, (GROUP_ROWS, 1), 0), WINDOW.bit_length() - 1)
    slope = jnp.zeros((GROUP_ROWS, 1), F32)
    sink = jnp.zeros((GROUP_ROWS, 1), F32)
    for hh in range(C_GROUP):
        slope = jnp.where(head == hh, _alibi_slope(c * C_GROUP + hh), slope)
        sink = jnp.where(head == hh, sink_ref[c * C_GROUP + hh], sink)
    return slope, sink


def _swa_probs(qg, kw, sink, slope, dist, bias, scale):
    sc = _dot(qg, kw, NT) * scale - slope * dist + bias
    m = jnp.maximum(jnp.max(sc, axis=-1, keepdims=True), sink)
    p = jnp.exp(sc - m)
    ps = jnp.exp(sink - m)
    inv = 1.0 / (jnp.sum(p, axis=-1, keepdims=True) + ps)
    return p * inv, ps * inv


def _stack_heads(x, c):
    hd = C_HEAD_DIM
    return jnp.concatenate([x[:, (c * C_GROUP + hh) * hd:(c * C_GROUP + hh + 1) * hd] for hh in range(C_GROUP)], axis=0)


SWA_SUB = 2


def _swa_kv_specs(sub, kvw):
    return [pl.BlockSpec((WINDOW, 2 * kvw), functools.partial(lambda n, w: (n * sub + w, 0), w=w)) for w in range(sub + 1)]


def _swa_fwd(q16, gate, kvp, sinks):
    s = q16.shape[0]
    qb = WINDOW
    hd = C_HEAD_DIM
    scale = 1.0 / math.sqrt(hd)
    kvw = C_KV_HEADS * hd
    sub = SWA_SUB if (s // qb) % SWA_SUB == 0 else 1

    def body(sink_ref, q_ref, g_ref, *refs):
        kv_refs, (o_ref, y_ref) = refs[:sub + 1], refs[sub + 1:]
        for u in range(sub):
            rows = slice(u * qb, (u + 1) * qb)
            q = q_ref[rows, :]
            kv = jnp.concatenate([kv_refs[u][...], kv_refs[u + 1][...]], axis=0)
            dist, bias = _swa_window(pl.program_id(0) * sub + u)
            for c in range(C_KV_HEADS):
                slope, sink = _swa_group_cols(c, sink_ref)
                kw = kv[:, c * hd:(c + 1) * hd]
                vw = kv[:, kvw + c * hd:kvw + (c + 1) * hd]
                p, _ = _swa_probs(_stack_heads(q, c), kw, sink, slope, dist, bias, scale)
                og = _dot(p.astype(BF16), vw, NN)
                for hh in range(C_GROUP):
                    h = c * C_GROUP + hh
                    o_ref[rows, h * hd:(h + 1) * hd] = og[hh * qb:(hh + 1) * qb, :]
        g = g_ref[...]
        y_ref[...] = (o_ref[...] * (g * _sigmoid(g))).astype(BF16)

    wide = C_HEADS * hd
    big = pl.BlockSpec((sub * qb, wide), lambda n: (n, 0))
    return pl.pallas_call(
        body, name="swa_fwd", grid=(s // (sub * qb),),
        in_specs=[pl.BlockSpec(memory_space=pltpu.SMEM), big, big] + _swa_kv_specs(sub, kvw),
        out_specs=[big, big],
        out_shape=[jax.ShapeDtypeStruct((s, wide), F32), jax.ShapeDtypeStruct((s, wide), BF16)],
        compiler_params=_cparams(("arbitrary",)),
    )(sinks, q16, gate, *([kvp] * (sub + 1)))


def _swa_bwd(q16, do16, kvp, sinks):
    s = q16.shape[0]
    qb = WINDOW
    hd = C_HEAD_DIM
    scale = 1.0 / math.sqrt(hd)
    kvw = C_KV_HEADS * hd
    sub = 1
    nstep = s // (sub * qb)

    def body(sink_ref, q_ref, do_ref, *refs):
        kv_refs, (dq_ref, dkv_ref, ds_ref, sacc_ref) = refs[:sub + 1], refs[sub + 1:]

        @pl.when(pl.program_id(0) == 0)
        def _():
            dkv_ref[...] = jnp.zeros_like(dkv_ref)
            sacc_ref[...] = jnp.zeros_like(sacc_ref)

        for u in range(sub):
            rows = slice(u * qb, (u + 1) * qb)
            n = pl.program_id(0) * sub + u
            q = q_ref[rows, :]
            do = do_ref[rows, :]
            kv = jnp.concatenate([kv_refs[u][...], kv_refs[u + 1][...]], axis=0)
            off = pl.multiple_of(n * qb, qb)
            dist, bias = _swa_window(n)
            for c in range(C_KV_HEADS):
                slope, sink = _swa_group_cols(c, sink_ref)
                kw = kv[:, c * hd:(c + 1) * hd]
                vw = kv[:, kvw + c * hd:kvw + (c + 1) * hd]
                qg = _stack_heads(q, c)
                dog = _stack_heads(do, c)
                p, ps = _swa_probs(qg, kw, sink, slope, dist, bias, scale)
                dp = _dot(dog, vw, NT)
                dd = jnp.sum(p * dp, axis=-1, keepdims=True)
                ds16 = (p * (dp - dd) * scale).astype(BF16)
                sacc_ref[c] += ps * dd
                dqg = _dot(ds16, kw, NN).astype(BF16)
                for hh in range(C_GROUP):
                    h = c * C_GROUP + hh
                    dq_ref[rows, h * hd:(h + 1) * hd] = dqg[hh * qb:(hh + 1) * qb, :]
                dkv_ref[pl.ds(off, 2 * qb), c * hd:(c + 1) * hd] += _dot(ds16, qg, TN)
                dkv_ref[pl.ds(off, 2 * qb), kvw + c * hd:kvw + (c + 1) * hd] += _dot(p.astype(BF16), dog, TN)

        @pl.when(pl.program_id(0) == nstep - 1)
        def _():
            lane = lax.broadcasted_iota(jnp.int32, (1, 128), 1)
            row = jnp.zeros((1, 128), F32)
            for c in range(C_KV_HEADS):
                for hh in range(C_GROUP):
                    tot = jnp.sum(sacc_ref[c, pl.ds(hh * qb, qb), :], axis=0, keepdims=True)
                    row = jnp.where(lane == c * C_GROUP + hh, -tot, row)
            ds_ref[...] = row

    wide = C_HEADS * hd
    big = pl.BlockSpec((sub * qb, wide), lambda n: (n, 0))
    return pl.pallas_call(
        body, name="swa_bwd", grid=(nstep,),
        in_specs=[pl.BlockSpec(memory_space=pltpu.SMEM), big, big] + _swa_kv_specs(sub, kvw),
        out_specs=[big,
                   pl.BlockSpec((s + qb, 2 * kvw), lambda n: (0, 0)), pl.BlockSpec((1, 128), lambda n: (0, 0))],
        out_shape=[jax.ShapeDtypeStruct((s, wide), BF16), jax.ShapeDtypeStruct((s + qb, 2 * kvw), F32),
                   jax.ShapeDtypeStruct((1, 128), F32)],
        scratch_shapes=[pltpu.VMEM((C_KV_HEADS, GROUP_ROWS, 1), F32)],
        compiler_params=_cparams(("arbitrary",)),
    )(sinks, q16, do16, *([kvp] * (sub + 1)))


def _adamw(parts, w, m, v, *, name, tr=496):
    npart, r, c = parts.shape
    tr = min(tr, r)
    assert r % tr == 0 and tr % 16 == 0
    c1 = 1.0 / (1.0 - ADAM_B1 ** ADAM_STEP)
    c2 = 1.0 / (1.0 - ADAM_B2 ** ADAM_STEP)

    def body(p_ref, w_ref, m_ref, v_ref, g_ref, d_ref, nm_ref, nv_ref):
        g = p_ref[0].astype(F32)
        for j in range(1, npart):
            g = g + p_ref[j].astype(F32)
        nm = ADAM_B1 * m_ref[...] + (1.0 - ADAM_B1) * g
        nv = ADAM_B2 * v_ref[...] + (1.0 - ADAM_B2) * (g * g)
        g_ref[...] = g
        nm_ref[...] = nm
        nv_ref[...] = nv
        d_ref[...] = -ADAM_LR * ((nm * c1) / (jnp.sqrt(nv * c2) + ADAM_EPS) + ADAM_WD * w_ref[...])

    spec = pl.BlockSpec((tr, c), lambda i: (i, 0))
    return pl.pallas_call(
        body, name=name, grid=(r // tr,),
        in_specs=[pl.BlockSpec((npart, tr, c), lambda i: (0, i, 0)), spec, spec, spec],
        out_specs=[spec] * 4, out_shape=[jax.ShapeDtypeStruct((r, c), F32)] * 4,
        compiler_params=_cparams(("parallel",)),
    )(parts, w, m, v)


GATHER = "gather"
A2A = "a2a"
GATHER_COLS = "gather_cols"
A2A_COLS = "a2a_cols"


def _gather_two_level(bufs, modes, x, *, tr=1024, name):
    nb = len(bufs)
    rows, cols = x.shape
    tr = min(tr, rows)
    steps = rows // tr

    def body(*refs):
        x_ref, src, x16_ref, dst = refs[0], refs[1:1 + nb], refs[1 + nb], refs[2 + nb:2 + 2 * nb]
        send_sems, recv_sems, local_sems = refs[2 + 2 * nb:]
        cx, cy, cc = lax.axis_index("x"), lax.axis_index("y"), lax.axis_index("c")
        me = 4 * cx + 2 * cy + cc
        here, sibling = (cx, cy, cc), (cx, cy, 1 - cc)

        def copy(b, k, origin, to, source):
            return pltpu.make_async_remote_copy(
                src_ref=source, dst_ref=_slot(dst[b], modes[b], origin), send_sem=send_sems.at[b, k - 1],
                recv_sem=recv_sems.at[b, k - 1], device_id=to, device_id_type=pl.DeviceIdType.MESH)

        def first_copies():
            local = [pltpu.make_async_copy(src[b], _slot(dst[b], modes[b], me), local_sems.at[b]) for b in range(nb)]
            sends = [copy(b, k, me, (cx ^ (k >> 2), cy ^ ((k >> 1) & 1), cc ^ (k & 1)), src[b])
                     for k in (1, 2, 4, 6) for b in range(nb)]
            return local, sends

        @pl.when(pl.program_id(0) == 0)
        def _():
            local, sends = first_copies()
            for cp in local + sends:
                cp.start()

        x16_ref[...] = x_ref[...].astype(BF16)

        @pl.when(pl.program_id(0) == steps - 1)
        def _():
            local, sends = first_copies()
            for j in (2, 4, 6):
                for b in range(nb):
                    copy(b, j, me ^ j, here, src[b]).wait_recv()
                    passed = copy(b, j ^ 1, me ^ j, sibling, _slot(dst[b], modes[b], me ^ j))
                    passed.start()
                    sends.append(passed)
            for k in (1, 3, 5, 7):
                for b in range(nb):
                    copy(b, k, me ^ k, here, src[b]).wait_recv()
            for cp in sends:
                cp.wait_send()
            for cp in local:
                cp.wait()

    hbm = pl.BlockSpec(memory_space=pl.ANY)
    tile = pl.BlockSpec((tr, cols), lambda i: (i, 0))
    return pl.pallas_call(
        body, name=name, grid=(steps,), in_specs=[tile] + [hbm] * nb, out_specs=[tile] + [hbm] * nb,
        out_shape=[jax.ShapeDtypeStruct(x.shape, BF16)] + _exchange_out_shapes(bufs, modes),
        scratch_shapes=_exchange_sems(nb),
        compiler_params=pltpu.CompilerParams(dimension_semantics=("arbitrary",), vmem_limit_bytes=VMEM_LIMIT,
                                             has_side_effects=True),
    )(x, *bufs)


def _exchange_out_shapes(bufs, gather):
    shapes = {GATHER: lambda s: (N_DEV,) + s, A2A: lambda s: s, GATHER_COLS: lambda s: (s[0], N_DEV * s[1]),
              A2A_COLS: lambda s: (N_DEV, s[0], s[1] // N_DEV)}
    return [jax.ShapeDtypeStruct(shapes[g](tuple(b.shape)), b.dtype) for b, g in zip(bufs, gather)]


def _sent(src, mode, peer):
    if mode == A2A:
        return src.at[peer]
    if mode == A2A_COLS:
        w = src.shape[1] // N_DEV
        return src.at[:, pl.ds(pl.multiple_of(peer * w, 128), w)]
    return src


def _slot(dst, mode, dev):
    if mode == GATHER_COLS:
        w = dst.shape[1] // N_DEV
        return dst.at[:, pl.ds(pl.multiple_of(dev * w, 128), w)]
    return dst.at[dev]


def _exchange_sems(nb):
    return [pltpu.SemaphoreType.DMA((nb, N_DEV - 1)), pltpu.SemaphoreType.DMA((nb, N_DEV - 1)),
            pltpu.SemaphoreType.DMA((nb,))]


def _exchange_copies(src, dst, gather, sems):
    send_sems, recv_sems, local_sems = sems
    x, y, c = lax.axis_index("x"), lax.axis_index("y"), lax.axis_index("c")
    me = 4 * x + 2 * y + c
    local, sends, recvs = [], [], []
    for b in range(len(src)):
        mine = _sent(src[b], gather[b], me)
        local.append(pltpu.make_async_copy(mine, _slot(dst[b], gather[b], me), local_sems.at[b]))
        for k in range(1, N_DEV):
            px, py, pc = x ^ (k >> 2), y ^ ((k >> 1) & 1), c ^ (k & 1)
            peer = 4 * px + 2 * py + pc
            pair = dict(send_sem=send_sems.at[b, k - 1], recv_sem=recv_sems.at[b, k - 1],
                        device_id_type=pl.DeviceIdType.MESH)
            sends.append(pltpu.make_async_remote_copy(
                src_ref=_sent(src[b], gather[b], peer), dst_ref=_slot(dst[b], gather[b], me), device_id=(px, py, pc),
                **pair))
            recvs.append(pltpu.make_async_remote_copy(
                src_ref=mine, dst_ref=_slot(dst[b], gather[b], peer), device_id=(x, y, c), **pair))
    return local, sends, recvs


def _exchange_start(src, dst, gather, sems):
    local, sends, _ = _exchange_copies(src, dst, gather, sems)
    for cp in local + sends:
        cp.start()


def _exchange_wait(src, dst, gather, sems):
    local, sends, recvs = _exchange_copies(src, dst, gather, sems)
    for cp in sends:
        cp.wait_send()
    for cp in recvs:
        cp.wait_recv()
    for cp in local:
        cp.wait()


R_SMALL = 16
REPL_ROWS = 272
REPL_LEN = REPL_ROWS * D_MODEL


def _small_block(parts):
    flat = jnp.concatenate(parts, axis=-1)
    lead = flat.ndim - 1
    return jnp.pad(flat[..., None, :], [(0, 0)] * lead + [(0, R_SMALL - 1), (0, D_MODEL - flat.shape[-1])])


def _small_vectors(conv_w, ln_g, ln_b):
    return _small_block([conv_w.reshape(512), ln_g.reshape(128), ln_b.reshape(128)])


REPL_SHAPES = ((1, 8, 128, 128), (1, 8, 128, 128), (1, 1024), (1, 1024), (1, 1024), (1, 1024), (1, 1024), (1, 1024),
               (1, 16))


def _pack_repl(parts):
    flat = jnp.concatenate([p.reshape(-1) for p in parts])
    return jnp.concatenate([flat, jnp.zeros((REPL_LEN - flat.shape[0],), F32)]).reshape(REPL_ROWS, D_MODEL)


def _unpack_repl(p):
    flat = p.reshape(-1)
    out, o = [], 0
    for shp in REPL_SHAPES:
        n = math.prod(shp)
        out.append(flat[o:o + n].reshape(shp))
        o += n
    return out


def kernel(x, e_w_in, e_conv_w, e_conv_b, e_w_gate_a, e_b_gate_a, e_w_gate_x, e_b_gate_x, e_lru_lambda, e_w_out, e_ln_g, e_ln_b, o_w_in, o_sinks, o_w_out, o_ln_g, o_ln_b, loss_target, m_e_w_in, m_e_conv_w, m_e_conv_b, m_e_w_gate_a, m_e_b_gate_a, m_e_w_gate_x, m_e_b_gate_x, m_e_lru_lambda, m_e_w_out, m_e_ln_g, m_e_ln_b, m_o_w_in, m_o_sinks, m_o_w_out, m_o_ln_g, m_o_ln_b, v_e_w_in, v_e_conv_w, v_e_conv_b, v_e_w_gate_a, v_e_b_gate_a, v_e_w_gate_x, v_e_b_gate_x, v_e_lru_lambda, v_e_w_out, v_e_ln_g, v_e_ln_b, v_o_w_in, v_o_sinks, v_o_w_out, v_o_ln_g, v_o_ln_b):
    d = D_MODEL
    x0 = x[0]
    target = loss_target[0]
    s = x0.shape[0]

    spack = _small_block([e_conv_w.reshape(512), o_ln_g.reshape(128), o_ln_b.reshape(128)])
    x0_16, wi_e, sall = _gather_two_level([e_w_in[0].astype(BF16), spack], [GATHER_COLS, GATHER], x0,
                                          name="gather_w_in")
    conv_w = sall[:, 0, 0:512].reshape(N_DEV, 4, 128).transpose(1, 0, 2).reshape(4, d)
    ln_g_o = sall[:, 0, 512:640].reshape(1, d)
    ln_b_o = sall[:, 0, 640:768].reshape(1, d)
    pvec = jnp.concatenate([conv_w, e_conv_b, e_b_gate_a, e_b_gate_x, e_lru_lambda], axis=0)
    wa16 = e_w_gate_a[0].astype(BF16)
    wx16 = e_w_gate_x[0].astype(BF16)
    sinks = o_sinks[0]

    h0a, wall_out_e = _matmul(x0_16, wi_e, mode="nn", n_out=2 * d, b_off=0, name="l0_in_a",
                              comm=([e_w_out[0].astype(BF16)], [GATHER]))
    qkv, wall_in_o = _matmul(x0_16, wi_e, mode="nn", n_out=3 * d, b_off=2 * d, out_dtype=BF16, name="l0_in_qkv",
                             comm=([o_w_in[0].astype(BF16)], [GATHER]))
    bg, wall_out_o = _matmul(x0_16, wi_e, mode="nn", n_out=d, b_off=5 * d, name="l0_in_bg",
                             comm=([o_w_out[0].astype(BF16)], [GATHER]))
    wo_e = wall_out_e.reshape(2 * d, d)
    wi_o = wall_in_o.transpose(1, 0, 2).reshape(d, 2304)
    wi_o = jnp.concatenate([wi_o[:, :1024], wi_o[:, 1280:], wi_o[:, 1024:1280]], axis=1)
    wo_o = wall_out_o.reshape(d, d)
    ya, hst = _rglru_fwd(h0a, pvec, wa16, wx16)
    ob, yb = _sb_fwd(qkv, bg)
    z0, x1, x1_16 = _matmul([ya, yb], wo_e, mode="nn", tm=512, name="l0_out",
                            epi=(_epi_ln_fwd, [(x0, 0)], [e_ln_g, e_ln_b], [F32, F32, BF16], 0))

    q1 = _matmul(x1_16, wi_o, mode="nn", n_out=d, b_off=0, out_dtype=BF16, name="l1_in_q")
    g1 = _matmul(x1_16, wi_o, mode="nn", n_out=d, b_off=d, name="l1_in_g")
    kv1 = _matmul(x1_16, wi_o, mode="nn", n_out=256, b_off=2 * d, tn=256, out_dtype=BF16, name="l1_in_kv")
    kvp = jnp.concatenate([jnp.zeros((WINDOW, 256), BF16), kv1], axis=0)
    o1, y1 = _swa_fwd(q1, g1, kvp, sinks)
    dz1, dz1_16, loss_cols, dg_o, db_o = _matmul(
        y1, wo_o, mode="nn", tm=512, name="l1_out",
        epi=(_epi_ln_loss_bwd, [(x1, 0), (target, 0)], [ln_g_o, ln_b_o], [F32, BF16], 3))
    loss_hi = jnp.sum(loss_cols).astype(BF16).astype(F32)
    loss_terms = jnp.stack([loss_hi, jnp.sum(loss_cols) - loss_hi]).reshape(1, 2)

    dwo_o = _matmul(y1, dz1_16, mode="tn", out_dtype=BF16, name="l1_dwout")
    do1, dg1, parts_out_o = _matmul(dz1_16, wo_o, mode="nt", tm=512, name="l1_dy",
                                    comm=([dwo_o.reshape(N_DEV, 128, d)], [A2A]),
                                    epi=(_epi_gate_bwd, [(o1, 0), (g1, 0)], [], [BF16, BF16], 0))
    dq1, dkvp, dsink = _swa_bwd(q1, do1, kvp, sinks)
    dkv1 = dkvp[WINDOW:].astype(BF16)
    dh1 = [dq1, dg1, dkv1]
    dw_qg = _matmul(x1_16, [dq1, dg1], mode="tn", out_dtype=BF16, name="l1_dwin_qg")
    dw_kv = _matmul(x1_16, dkv1, mode="tn", out_dtype=BF16, name="l1_dwin_kv")
    dwi_o = jnp.concatenate([dw_qg[:, :1024], dw_kv, dw_qg[:, 1024:]], axis=1)
    dz0, dz0_16, dg_e, db_e, parts_in_o = _matmul(
        dh1, wi_o, mode="nt", tm=512, tk=2304, name="l1_dx",
        comm=([dwi_o.reshape(d, N_DEV, 288).transpose(1, 0, 2)], [A2A]),
        epi=(_epi_ln_bwd, [(dz1, 0), (z0, 0)], [e_ln_g], [F32, BF16], 2))

    dwo_e = _matmul([ya, yb], dz0_16, mode="tn", out_dtype=BF16, name="l0_dwout")
    dhst, dag, parts_out_e = _matmul(
        dz0_16, wo_e, mode="nt", tm=512, n_out=d, b_off=0, name="l0_dy_a",
        comm=([dwo_e.reshape(N_DEV, 256, d)], [A2A]),
        epi=(_epi_gate_bwd, [(hst, 0), (h0a, d)], [], [F32, BF16], 0))
    dob, dbg = _matmul(dz0_16, wo_e, mode="nt", tm=512, n_out=d, b_off=d, name="l0_dy_b",
                       epi=(_epi_gate_bwd, [(ob, 0), (bg, 0)], [], [BF16, BF16], 0))
    dq0, dk0, dv0 = _sb_bwd(qkv, dob)
    dax, dwa, dwx, dpv = _rglru_bwd(h0a, hst, dhst, pvec, wa16, wx16)
    dh0 = [dax, dag, dq0, dk0, dv0, dbg]
    repl = _pack_repl([dwa, dwx, dpv[4:5], dpv[5:6], dpv[6:7], dpv[7:8], dg_e, db_e, dsink[:, :C_HEADS], loss_terms])
    dwi_e, parts_repl = _matmul(x0_16, dh0, mode="tn", out_dtype=BF16, name="l0_dwin",
                                comm=([repl.astype(BF16)], [GATHER]))
    small = _small_block([dpv[0:4].reshape(4, N_DEV, 128).transpose(1, 0, 2).reshape(N_DEV, 512),
                          dg_o.reshape(N_DEV, 128), db_o.reshape(N_DEV, 128)]).astype(BF16)
    grad_x, parts_in_e, parts_small = _matmul(dh0, wi_e, mode="nt", tm=512, res=dz0, alpha=ALPHA, name="l0_dx",
                                              comm=([dwi_e, small], [A2A_COLS, A2A]))

    res_in_e = _adamw(parts_in_e, e_w_in[0], m_e_w_in[0], v_e_w_in[0], tr=512, name="adamw_w_in_e")
    res_out_e = _adamw(parts_out_e, e_w_out[0], m_e_w_out[0], v_e_w_out[0], name="adamw_w_out_e")
    res_in_o = _adamw(parts_in_o, o_w_in[0], m_o_w_in[0], v_o_w_in[0], tr=512, name="adamw_w_in_o")
    res_out_o = _adamw(parts_out_o, o_w_out[0], m_o_w_out[0], v_o_w_out[0], name="adamw_w_out_o")
    res_small = _adamw(parts_small, _small_vectors(e_conv_w, o_ln_g, o_ln_b),
                       _small_vectors(m_e_conv_w, m_o_ln_g, m_o_ln_b),
                       _small_vectors(v_e_conv_w, v_o_ln_g, v_o_ln_b), name="adamw_vectors")

    w_r = _pack_repl([e_w_gate_a, e_w_gate_x, e_conv_b, e_b_gate_a, e_b_gate_x, e_lru_lambda, e_ln_g, e_ln_b, o_sinks])
    m_r = _pack_repl([m_e_w_gate_a, m_e_w_gate_x, m_e_conv_b, m_e_b_gate_a, m_e_b_gate_x, m_e_lru_lambda, m_e_ln_g,
                      m_e_ln_b, m_o_sinks])
    v_r = _pack_repl([v_e_w_gate_a, v_e_w_gate_x, v_e_conv_b, v_e_b_gate_a, v_e_b_gate_x, v_e_lru_lambda, v_e_ln_g,
                      v_e_ln_b, v_o_sinks])
    g_r, d_r, nm_r, nv_r = _adamw(parts_repl, w_r, m_r, v_r, name="adamw_replicated")
    loss_at = sum(math.prod(shp) for shp in REPL_SHAPES)
    loss = g_r[loss_at // d, loss_at % d] + g_r[(loss_at + 1) // d, (loss_at + 1) % d]

    def assemble(i, rp):
        vec = res_small[i][0]
        cw, lg_o, lb_o = vec[0:512].reshape(1, 4, 128), vec[512:640].reshape(1, 128), vec[640:768].reshape(1, 128)
        w_a, w_x, cb, b_a, b_x, lam, lg_e, lb_e, snk = _unpack_repl(rp)
        return [res_in_e[i][None], cw, cb, w_a, b_a, w_x, b_x, lam, res_out_e[i][None], lg_e, lb_e,
                res_in_o[i][None], snk, res_out_o[i][None], lg_o, lb_o]

    return (loss, grad_x[None], *assemble(0, g_r), *assemble(1, d_r), *assemble(2, nm_r), *assemble(3, nv_r))
```

```python
import functools
import math

import jax
import jax.numpy as jnp
from jax import lax
from jax.experimental import pallas as pl
from jax.experimental.pallas import tpu as pltpu

F32 = jnp.float32
BF16 = jnp.bfloat16

N_DEV = 8
D_MODEL = 1024
LRU_BLOCKS = 8
LRU_BLOCK = 128
LRU_C = 8.0
SB_HEADS = 8
SB_HEAD_DIM = 128
C_HEADS = 16
C_KV_HEADS = 2
C_GROUP = 8
C_HEAD_DIM = 64
WINDOW = 128
DEPTH = 2
ALPHA = float((2 * DEPTH) ** 0.25)
LN_EPS = 1e-5
ADAM_LR = 0.001
ADAM_B1 = 0.9
ADAM_B2 = 0.999
ADAM_EPS = 1e-08
ADAM_WD = 0.01
ADAM_STEP = 10

VMEM_LIMIT = 56 * 1024 * 1024

NN = ((1,), (0,))
NT = ((1,), (1,))
TN = ((0,), (0,))


def _dot(a, b, dims):
    return lax.dot_general(a, b, (dims, ((), ())), preferred_element_type=F32)


def _sigmoid(x):
    return 0.5 * jnp.tanh(0.5 * x) + 0.5


def _cparams(sem, vmem=VMEM_LIMIT):
    return pltpu.CompilerParams(dimension_semantics=sem, vmem_limit_bytes=vmem)


def _matmul(a, b, *, mode, n_out=None, b_off=0, out_dtype=F32, res=None, alpha=1.0,
            tm=1024, tn=1024, tk=1024, comm=None, epi=None, name):
    a_list = list(a) if isinstance(a, (list, tuple)) else [a]
    b_list = list(b) if isinstance(b, (list, tuple)) else [b]
    if mode == "tn":
        k = a_list[0].shape[0]
        m = sum(p.shape[1] for p in a_list)
        n = n_out if n_out is not None else sum(p.shape[1] for p in b_list)
        assert all(p.shape[1] == tm for p in a_list) or len(a_list) == 1
        assert all(p.shape[1] == tn for p in b_list) or len(b_list) == 1
    else:
        assert len(b_list) == 1
        m = a_list[0].shape[0]
        k = sum(p.shape[1] for p in a_list)
        n = n_out if n_out is not None else (b.shape[0] if mode == "nt" else b.shape[1])
        if len(a_list) > 1:
            tk = k
    tm, tn, tk = min(tm, m), min(tn, n), min(tk, k)
    assert m % tm == 0 and n % tn == 0 and k % tk == 0 and b_off % tn == 0
    grid = (m // tm, n // tn, k // tk)
    nk = grid[2]
    jo = b_off // tn
    dims = {"nn": NN, "nt": NT, "tn": TN}[mode]
    nc = len(comm[0]) if comm else 0
    if epi:
        assert res is None
        epi_fn, epi_rows, epi_vecs, epi_dtypes, n_sums = epi
        assert n_sums == 0 or grid[1] == 1
    else:
        epi_rows, epi_vecs, epi_dtypes, n_sums = [], [], [out_dtype], 0
    n_ab = len(a_list) + len(b_list)
    n_in = n_ab + (res is not None) + len(epi_rows) + len(epi_vecs)
    n_res = len(epi_dtypes) + n_sums

    def body(*refs):
        a_refs, b_refs = refs[:len(a_list)], refs[len(a_list):n_ab]
        r_ref = refs[n_ab] if res is not None else None
        row_refs = refs[n_in - len(epi_rows) - len(epi_vecs):n_in - len(epi_vecs)]
        vec_refs = refs[n_in - len(epi_vecs):n_in]
        o_refs = refs[n_in + nc:n_in + nc + len(epi_dtypes)]
        sum_refs = refs[n_in + nc + len(epi_dtypes):n_in + nc + n_res]
        if comm:
            step = [pl.program_id(ax) for ax in range(3)]
            c_src = refs[n_in:n_in + nc]
            c_dst = refs[n_in + nc + n_res:n_in + 2 * nc + n_res]
            sems = refs[len(refs) - 3:]

            @pl.when((step[0] == 0) & (step[1] == 0) & (step[2] == 0))
            def _():
                _exchange_start(c_src, c_dst, comm[1], sems)

        def finish(out):
            if not epi:
                if r_ref is not None:
                    out = out + alpha * r_ref[...]
                o_refs[0][...] = out.astype(out_dtype)
                return
            outs, sums = epi_fn(out, [r[...] for r in row_refs], [v[...] for v in vec_refs])
            for o_ref, o in zip(o_refs, outs):
                o_ref[...] = o.astype(o_ref.dtype)
            if sum_refs:
                first_rows = pl.program_id(0) == 0

                @pl.when(first_rows)
                def _():
                    for s_ref, part_sum in zip(sum_refs, sums):
                        s_ref[...] = part_sum

                @pl.when(jnp.logical_not(first_rows))
                def _():
                    for s_ref, part_sum in zip(sum_refs, sums):
                        s_ref[...] += part_sum

        def accumulate(part):
            if nk == 1:
                finish(part)
                return
            acc_ref = refs[n_in + 2 * nc + n_res]
            kk = pl.program_id(2)

            @pl.when(kk == 0)
            def _():
                acc_ref[...] = part

            @pl.when(kk > 0)
            def _():
                acc_ref[...] += part

            @pl.when(kk == nk - 1)
            def _():
                finish(acc_ref[...])

        def product(a_ref, b_val):
            return _dot(a_ref[...].astype(BF16), b_val.astype(BF16), dims)

        if mode != "tn" and len(a_list) > 1:
            part, off = None, 0
            for a_ref in a_refs:
                w = a_ref.shape[1]
                b_val = b_refs[0][off:off + w, :] if mode == "nn" else b_refs[0][:, off:off + w]
                part = product(a_ref, b_val) if part is None else part + product(a_ref, b_val)
                off += w
            accumulate(part)
        elif len(a_list) > 1 or len(b_list) > 1:
            for pa, a_ref in enumerate(a_refs):
                for pb, b_ref in enumerate(b_refs):
                    picked = ([pl.program_id(0) == pa] if len(a_list) > 1 else []) + (
                        [pl.program_id(1) == pb] if len(b_list) > 1 else [])
                    pl.when(functools.reduce(jnp.logical_and, picked))(
                        functools.partial(lambda a_ref, b_ref: accumulate(product(a_ref, b_ref[...])), a_ref, b_ref))
        else:
            accumulate(product(a_refs[0], b_refs[0][...]))

        if comm:
            @pl.when((step[0] == grid[0] - 1) & (step[1] == grid[1] - 1) & (step[2] == grid[2] - 1))
            def _():
                _exchange_wait(c_src, c_dst, comm[1], sems)

    def held(axis, p):
        def index(i, j, kk):
            return (jnp.where((i, j)[axis] == p, kk, 0), 0)
        return index

    if mode == "tn":
        if len(a_list) > 1:
            a_specs = [pl.BlockSpec((tk, tm), held(0, p)) for p in range(len(a_list))]
        else:
            a_specs = [pl.BlockSpec((tk, tm), lambda i, j, kk: (kk, i))]
    elif len(a_list) > 1:
        a_specs = [pl.BlockSpec((tm, p.shape[1]), lambda i, j, kk: (i, 0)) for p in a_list]
    else:
        a_specs = [pl.BlockSpec((tm, tk), lambda i, j, kk: (i, kk))]
    if mode == "nt":
        b_specs = [pl.BlockSpec((tn, tk), lambda i, j, kk: (j + jo, kk))]
    elif len(b_list) > 1:
        b_specs = [pl.BlockSpec((tk, tn), held(1, p)) for p in range(len(b_list))]
    else:
        b_specs = [pl.BlockSpec((tk, tn), lambda i, j, kk: (kk, j + jo))]
    o_spec = pl.BlockSpec((tm, tn), lambda i, j, kk: (i, j))
    hbm = pl.BlockSpec(memory_space=pl.ANY)
    row_specs = [pl.BlockSpec((tm, tn), functools.partial(lambda i, j, kk, o: (i, j + o), o=off // tn))
                 for (_, off) in epi_rows]
    vec_specs = [pl.BlockSpec(v.shape, functools.partial(lambda i, j, kk, nd: (0,) * nd, nd=v.ndim)) for v in epi_vecs]
    in_specs = a_specs + b_specs + ([o_spec] if res is not None else []) + row_specs + vec_specs + [hbm] * nc
    args = (a_list + b_list + ([res] if res is not None else []) + [r for (r, _) in epi_rows] + list(epi_vecs)
            + (list(comm[0]) if comm else []))
    out_specs = [o_spec] * len(epi_dtypes) + [pl.BlockSpec((1, tn), lambda i, j, kk: (0, j))] * n_sums + [hbm] * nc
    out_shape = ([jax.ShapeDtypeStruct((m, n), dt) for dt in epi_dtypes] + [jax.ShapeDtypeStruct((1, n), F32)] * n_sums
                 + (_exchange_out_shapes(*comm) if comm else []))
    scratch = ([pltpu.VMEM((tm, tn), F32)] if nk > 1 else []) + (_exchange_sems(nc) if comm else [])
    if comm or n_sums or n_ab > 2:
        params = pltpu.CompilerParams(dimension_semantics=("arbitrary",) * 3, vmem_limit_bytes=VMEM_LIMIT,
                                      has_side_effects=bool(comm))
    else:
        params = _cparams(("parallel", "parallel", "arbitrary"))
    outs = pl.pallas_call(
        body, name=name, grid=grid, in_specs=in_specs, out_specs=out_specs, out_shape=out_shape,
        scratch_shapes=scratch, compiler_params=params,
    )(*args)
    return tuple(outs) if (comm or epi) else outs[0]


def _ln_stats(z):
    mu = jnp.mean(z, axis=-1, keepdims=True)
    zc = z - mu
    var = jnp.mean(zc * zc, axis=-1, keepdims=True)
    rstd = lax.rsqrt(var + LN_EPS)
    return zc * rstd, rstd


def _ln_bwd(dy, xhat, rstd, g):
    dxh = dy * g
    m1 = jnp.mean(dxh, axis=-1, keepdims=True)
    m2 = jnp.mean(dxh * xhat, axis=-1, keepdims=True)
    return rstd * (dxh - m1 - xhat * m2)


def _colsum(x):
    return jnp.sum(x, axis=0, keepdims=True)


def _epi_ln_fwd(acc, rows, vecs):
    z = ALPHA * rows[0] + acc
    xhat, _ = _ln_stats(z)
    x1 = xhat * vecs[0] + vecs[1]
    return [z, x1, x1], []


def _epi_ln_loss_bwd(acc, rows, vecs):
    inv_c = 1.0 / acc.shape[-1]
    z = ALPHA * rows[0] + acc
    xhat, rstd = _ln_stats(z)
    d = xhat * vecs[0] + vecs[1] - rows[1]
    dy = d * inv_c
    dz = _ln_bwd(dy, xhat, rstd, vecs[0])
    return [dz, dz], [_colsum(0.5 * d * d * inv_c), _colsum(dy * xhat), _colsum(dy)]


def _epi_ln_bwd(acc, rows, vecs):
    dy = acc + ALPHA * rows[0]
    xhat, rstd = _ln_stats(rows[1])
    dz = _ln_bwd(dy, xhat, rstd, vecs[0])
    return [dz, dz], [_colsum(dy * xhat), _colsum(dy)]


def _epi_gate_bwd(acc, rows, vecs):
    oo, gg = rows
    sg = _sigmoid(gg)
    return [acc * (gg * sg), acc * oo * (sg * (1.0 + gg * (1.0 - sg)))], []


LRU_T = 256
HALO = 8


def _log1p(y):
    u = 1.0 + y
    return jnp.where(u == 1.0, y, jnp.log(u) * (y / (u - 1.0)))


def _lru_gates(c, wa_ref, wx_ref, pv):
    c16 = c.astype(BF16)
    pre_r = jnp.concatenate(
        [_dot(c16[:, n * LRU_BLOCK:(n + 1) * LRU_BLOCK], wa_ref[n], NN) for n in range(LRU_BLOCKS)], axis=1)
    pre_i = jnp.concatenate(
        [_dot(c16[:, n * LRU_BLOCK:(n + 1) * LRU_BLOCK], wx_ref[n], NN) for n in range(LRU_BLOCKS)], axis=1)
    r = _sigmoid(pre_r + pv[5:6])
    ig = _sigmoid(pre_i + pv[6:7])
    lam = pv[7:8]
    ls = jnp.minimum(lam, 0.0) - _log1p(jnp.exp(-jnp.abs(lam)))
    la = LRU_C * r * ls
    a = jnp.exp(la)
    a2 = a * a
    m = jnp.sqrt(jnp.tanh(-la) * (a2 + 1.0))
    return c16, r, ig, ls, la, a, a2, m


def _conv(ext_ref, x, pv, t):
    return (pv[4:5] + pv[3:4] * x + pv[2:3] * ext_ref[pl.ds(HALO - 1, t), :]
            + pv[1:2] * ext_ref[pl.ds(HALO - 2, t), :] + pv[0:1] * ext_ref[pl.ds(HALO - 3, t), :])


def _rglru_fwd(h0a, pvec, wa16, wx16):
    s = h0a.shape[0]
    w = D_MODEL
    t = min(LRU_T, s)
    assert s % t == 0

    def body(ax_ref, ag_ref, pv_ref, wa_ref, wx_ref, ya_ref, h_ref, ext_ref, a_ref, u_ref, hc_ref):
        i = pl.program_id(0)

        @pl.when(i == 0)
        def _():
            ext_ref[pl.ds(0, HALO), :] = jnp.zeros((HALO, w), F32)
            hc_ref[...] = jnp.zeros((1, w), F32)

        pv = pv_ref[...]
        ax = ax_ref[...]
        ext_ref[pl.ds(HALO, t), :] = ax
        c = _conv(ext_ref, ax, pv, t)
        ext_ref[pl.ds(0, HALO), :] = ax[t - HALO:, :]
        _, _, ig, _, _, a, _, m = _lru_gates(c, wa_ref, wx_ref, pv)
        a_ref[...] = a
        u_ref[...] = m * (ig * c)

        def steps(k, h):
            base = pl.multiple_of(k * HALO, HALO)
            for r in range(HALO):
                h = a_ref[pl.ds(base + r, 1), :] * h + u_ref[pl.ds(base + r, 1), :]
                h_ref[pl.ds(base + r, 1), :] = h
            return h

        hc_ref[...] = lax.fori_loop(0, t // HALO, steps, hc_ref[...])
        ag = ag_ref[...]
        ya_ref[...] = (h_ref[...] * (ag * _sigmoid(ag))).astype(BF16)

    full = lambda shp: pl.BlockSpec(shp, lambda i: (0,) * len(shp))
    return pl.pallas_call(
        body, name="rglru_fwd", grid=(s // t,),
        in_specs=[pl.BlockSpec((t, w), lambda i: (i, 0)), pl.BlockSpec((t, w), lambda i: (i, 1)),
                  full((8, w)), full((LRU_BLOCKS, LRU_BLOCK, LRU_BLOCK)), full((LRU_BLOCKS, LRU_BLOCK, LRU_BLOCK))],
        out_specs=[pl.BlockSpec((t, w), lambda i: (i, 0)), pl.BlockSpec((t, w), lambda i: (i, 0))],
        out_shape=[jax.ShapeDtypeStruct((s, w), BF16), jax.ShapeDtypeStruct((s, w), F32)],
        scratch_shapes=[pltpu.VMEM((t + HALO, w), F32), pltpu.VMEM((t, w), F32), pltpu.VMEM((t, w), F32),
                        pltpu.VMEM((1, w), F32)],
        compiler_params=_cparams(("arbitrary",)),
    )(h0a, h0a, pvec, wa16, wx16)


def _rglru_bwd(h0a, h, dh, pvec, wa16, wx16):
    s = h0a.shape[0]
    w = D_MODEL
    t = min(LRU_T, s)
    nb = s // t
    hb = t // HALO

    def body(ax_ref, axh_ref, h_ref, hh_ref, dh_ref, pv_ref, wa_ref, wx_ref,
             dax_ref, dwa_ref, dwx_ref, dpv_ref, ext_ref, hext_ref, dcext_ref, a_ref, g_ref, gc_ref):
        i = pl.program_id(0)
        blk = nb - 1 - i

        @pl.when(i == 0)
        def _():
            dwa_ref[...] = jnp.zeros_like(dwa_ref)
            dwx_ref[...] = jnp.zeros_like(dwx_ref)
            dpv_ref[...] = jnp.zeros_like(dpv_ref)
            gc_ref[...] = jnp.zeros((1, w), F32)
            dcext_ref[pl.ds(t, HALO), :] = jnp.zeros((HALO, w), F32)

        pv = pv_ref[...]
        ax = ax_ref[...]
        keep = (blk > 0).astype(F32)
        ext_ref[pl.ds(0, HALO), :] = axh_ref[...] * keep
        ext_ref[pl.ds(HALO, t), :] = ax
        hext_ref[pl.ds(0, HALO), :] = hh_ref[...] * keep
        hext_ref[pl.ds(HALO, t), :] = h_ref[...]
        c = _conv(ext_ref, ax, pv, t)
        c16, r, ig, ls, _, a, a2, m = _lru_gates(c, wa_ref, wx_ref, pv)
        a_ref[...] = a

        def steps(k, carry):
            base = pl.multiple_of(t - HALO - k * HALO, HALO)
            for r in reversed(range(HALO)):
                g = dh_ref[pl.ds(base + r, 1), :] + carry
                g_ref[pl.ds(base + r, 1), :] = g
                carry = a_ref[pl.ds(base + r, 1), :] * g
            return carry

        gc_ref[...] = lax.fori_loop(0, t // HALO, steps, gc_ref[...])
        g = g_ref[...]
        hprev = hext_ref[pl.ds(HALO - 1, t), :]
        gm = g * m
        d_la = g * hprev * a - (g * ig * c) * a2 / m
        d_pr = d_la * (LRU_C * ls) * r * (1.0 - r)
        d_pi = gm * c * ig * (1.0 - ig)
        dc = gm * ig
        dpr16 = d_pr.astype(BF16)
        dpi16 = d_pi.astype(BF16)
        dc_parts = []
        for n in range(LRU_BLOCKS):
            sl = slice(n * LRU_BLOCK, (n + 1) * LRU_BLOCK)
            dwa_ref[n] += _dot(c16[:, sl], dpr16[:, sl], TN)
            dwx_ref[n] += _dot(c16[:, sl], dpi16[:, sl], TN)
            dc_parts.append(_dot(dpr16[:, sl], wa_ref[n], NT) + _dot(dpi16[:, sl], wx_ref[n], NT))
        dc = dc + jnp.concatenate(dc_parts, axis=1)
        dcext_ref[pl.ds(0, t), :] = dc
        dax = (pv[3:4] * dc + pv[2:3] * dcext_ref[pl.ds(1, t), :] + pv[1:2] * dcext_ref[pl.ds(2, t), :]
               + pv[0:1] * dcext_ref[pl.ds(3, t), :])
        dax_ref[...] = dax.astype(BF16)
        dcext_ref[pl.ds(t, HALO), :] = dc[:HALO, :]
        sums = [_colsum(dc * ext_ref[pl.ds(HALO - 3 + kk, t), :]) for kk in range(4)]
        sums += [_colsum(dc), _colsum(d_pr), _colsum(d_pi), _colsum(d_la * (LRU_C * r))]
        for kk, part in enumerate(sums):
            dpv_ref[pl.ds(kk, 1), :] += part

        @pl.when(i == nb - 1)
        def _():
            lam = pv[7:8]
            dls = dpv_ref[pl.ds(7, 1), :]
            dpv_ref[pl.ds(7, 1), :] = dls * _sigmoid(-lam)

    full = lambda shp: pl.BlockSpec(shp, lambda i: (0,) * len(shp))
    rev = lambda cb: pl.BlockSpec((t, w), functools.partial(lambda i, cb: (nb - 1 - i, cb), cb=cb))
    halo = lambda cb: pl.BlockSpec(
        (HALO, w), functools.partial(lambda i, cb: (jnp.maximum((nb - 1 - i) * hb - 1, 0), cb), cb=cb))
    gw = (LRU_BLOCKS, LRU_BLOCK, LRU_BLOCK)
    return pl.pallas_call(
        body, name="rglru_bwd", grid=(nb,),
        in_specs=[rev(0), halo(0), rev(0), halo(0), rev(0), full((8, w)), full(gw), full(gw)],
        out_specs=[rev(0), full(gw), full(gw), full((8, w))],
        out_shape=[jax.ShapeDtypeStruct((s, w), BF16), jax.ShapeDtypeStruct(gw, F32),
                   jax.ShapeDtypeStruct(gw, F32), jax.ShapeDtypeStruct((8, w), F32)],
        scratch_shapes=[pltpu.VMEM((t + HALO, w), F32), pltpu.VMEM((t + HALO, w), F32),
                        pltpu.VMEM((t + HALO, w), F32), pltpu.VMEM((t, w), F32), pltpu.VMEM((t, w), F32),
                        pltpu.VMEM((1, w), F32)],
        compiler_params=_cparams(("arbitrary",)),
    )(h0a, h0a, h, h, dh, pvec, wa16, wx16)


SB_T = 256
SB_SUB = 8


def _split16(x):
    hi = x.astype(BF16)
    lo = (x - hi.astype(F32)).astype(BF16)
    return jnp.concatenate([hi, lo], axis=0)


def _sb_tile(q, k, scale, tri, run, causal):
    tq = q.shape[0]
    z = _dot(q, k, NT) * scale
    log1mb = -(jnp.maximum(z, 0.0) + jnp.log(1.0 + jnp.exp(-jnp.abs(z))))
    if causal is not None:
        log1mb = jnp.where(causal, log1mb, 0.0)
    cs = _dot(_split16(log1mb), tri, NN)
    cs = cs[:tq] + cs[tq:]
    wgt = jnp.exp(z + cs + run)
    if causal is not None:
        wgt = jnp.where(causal, wgt, 0.0)
    return z, wgt, run + cs[:, 0:1]


SB_DEAD = -105.0


def _sb_alive(run):
    return (jnp.max(run) > SB_DEAD).astype(jnp.int32)


def _sb_more(carry):
    return (carry[0] >= 0) & (carry[1] > 0)


def _sb_fwd(qkv, bg):
    s = qkv.shape[0]
    t = min(SB_T, s)
    nq = s // t
    dh = SB_HEAD_DIM
    scale = 1.0 / math.sqrt(dh)

    sub = SB_SUB if nq % SB_SUB == 0 else 1

    def body(q_ref, k_ref, v_ref, bg_ref, o_ref, y_ref):
        row = lax.broadcasted_iota(jnp.int32, (t, t), 0)
        col = lax.broadcasted_iota(jnp.int32, (t, t), 1)
        tri = jnp.where(row >= col, 1.0, 0.0).astype(BF16)
        causal = col < row
        def tile(q, kb, run, mask):
            off = pl.multiple_of(kb * t, t)
            _, wgt, run = _sb_tile(q, k_ref[pl.ds(off, t), :], scale, tri, run, mask)
            return run, _dot(wgt.astype(BF16), v_ref[pl.ds(off, t), :], NN)

        blocks = []
        for u in range(sub):
            iq = pl.program_id(1) * sub + u
            q = q_ref[pl.ds(u * t, t), :]
            run_d, acc_d = tile(q, iq, jnp.zeros((t, 1), F32), causal)
            run_l, acc_l = tile(q, jnp.maximum(iq - 1, 0), run_d, None)
            blocks.append((iq, q, jnp.where(iq > 0, run_l, run_d), acc_d + jnp.where(iq > 0, acc_l, 0.0)))
        for u, (iq, q, run, acc) in enumerate(blocks):
            def loop(carry, q=q):
                kb, _, run, acc = carry
                run, part = tile(q, kb, run, None)
                return kb - 1, _sb_alive(run), run, acc + part

            _, _, run, acc = lax.while_loop(_sb_more, loop, (iq - 2, _sb_alive(run), run, acc))
            rows = pl.ds(u * t, t)
            o_ref[rows, :] = acc
            g = bg_ref[rows, :]
            y_ref[rows, :] = (acc * (g * _sigmoid(g))).astype(BF16)

    blk = lambda off: pl.BlockSpec((sub * t, dh), functools.partial(lambda h, i, off: (i, h + off), off=off))
    col = lambda off: pl.BlockSpec((s, dh), functools.partial(lambda h, i, off: (0, h + off), off=off))
    return pl.pallas_call(
        body, name="sb_fwd", grid=(SB_HEADS, nq // sub),
        in_specs=[blk(0), col(SB_HEADS), col(2 * SB_HEADS), blk(0)],
        out_specs=[blk(0), blk(0)],
        out_shape=[jax.ShapeDtypeStruct((s, SB_HEADS * dh), F32), jax.ShapeDtypeStruct((s, SB_HEADS * dh), BF16)],
        compiler_params=_cparams(("parallel", "arbitrary")),
    )(qkv, qkv, qkv, bg)


def _sb_bwd(qkv, dob):
    s = qkv.shape[0]
    t = min(SB_T, s)
    nq = s // t
    dh = SB_HEAD_DIM
    scale = 1.0 / math.sqrt(dh)

    sub = SB_SUB if nq % SB_SUB == 0 else 1

    def body(q_ref, k_ref, v_ref, do_ref, dq_ref, dk_out, dv_out, e_ref, b_ref, pe_ref, pb_ref, dk_ref, dv_ref):
        @pl.when(pl.program_id(1) == 0)
        def _():
            dk_ref[...] = jnp.zeros_like(dk_ref)
            dv_ref[...] = jnp.zeros_like(dv_ref)

        row = lax.broadcasted_iota(jnp.int32, (t, t), 0)
        col = lax.broadcasted_iota(jnp.int32, (t, t), 1)
        tri = jnp.where(row >= col, 1.0, 0.0).astype(BF16)
        tri_x = jnp.where(row < col, 1.0, 0.0).astype(BF16)
        causal = col < row

        def sweep1(q, do, kb, e_dst, b_dst, run, mask, live=None):
            off = pl.multiple_of(kb * t, t)
            v = v_ref[pl.ds(off, t), :]
            z, wgt, run = _sb_tile(q, k_ref[pl.ds(off, t), :], scale, tri, run, mask)
            if live is not None:
                wgt = jnp.where(live, wgt, 0.0)
            e_dst[...] = wgt * _dot(do, v, NT)
            b_dst[...] = _sigmoid(z)
            dv_ref[pl.ds(off, t), :] += _dot(wgt.astype(BF16), do, TN)
            return run

        def sweep2(q, kb, e_src, b_src, pre, mask, live=None):
            off = pl.multiple_of(kb * t, t)
            e = e_src[...]
            beta = b_src[...]
            ps = _dot(_split16(e), tri_x, NN)
            ps = ps[:t] + ps[t:]
            dz = e * (1.0 - beta) - beta * (ps + pre)
            if mask is not None:
                dz = jnp.where(mask, dz, 0.0)
            if live is not None:
                dz = jnp.where(live, dz, 0.0)
            dz16 = (dz * scale).astype(BF16)
            dk_ref[pl.ds(off, t), :] += _dot(dz16, q, TN)
            return pre + ps[:, t - 1:t] + e[:, t - 1:t], _dot(dz16, k_ref[pl.ds(off, t), :], NN)

        blocks = []
        for u in range(sub):
            iq = pl.program_id(1) * sub + u
            q, do = q_ref[pl.ds(u * t, t), :], do_ref[pl.ds(u * t, t), :]
            left = jnp.maximum(iq - 1, 0)
            run_d = sweep1(q, do, iq, pe_ref.at[u, 0], pb_ref.at[u, 0], jnp.zeros((t, 1), F32), causal)
            run_l = sweep1(q, do, left, pe_ref.at[u, 1], pb_ref.at[u, 1], run_d, None, iq > 0)
            blocks.append((iq, q, do, left, jnp.where(iq > 0, run_l, run_d)))
        carried = []
        for iq, q, do, left, run in blocks:
            def loop1(carry, q=q, do=do):
                run = sweep1(q, do, carry[0], e_ref.at[carry[0]], b_ref.at[carry[0]], carry[2], None)
                return carry[0] - 1, _sb_alive(run), run

            first = lax.while_loop(_sb_more, loop1, (iq - 2, _sb_alive(run), run))[0] + 1

            def loop2(kb, carry, q=q):
                pre, part = sweep2(q, kb, e_ref.at[kb], b_ref.at[kb], carry[0], None)
                return pre, carry[1] + part

            carried.append(lax.fori_loop(first, iq - 1, loop2, (jnp.zeros((t, 1), F32), jnp.zeros((t, dh), F32))))
        for u, ((iq, q, do, left, _), (pre, dq)) in enumerate(zip(blocks, carried)):
            pre, dq_l = sweep2(q, left, pe_ref.at[u, 1], pb_ref.at[u, 1], pre, None, iq > 0)
            _, dq_d = sweep2(q, iq, pe_ref.at[u, 0], pb_ref.at[u, 0], pre, causal)
            dq_ref[pl.ds(u * t, t), :] = (dq + dq_l + dq_d).astype(BF16)

        @pl.when(pl.program_id(1) == nq // sub - 1)
        def _():
            dk_out[...] = dk_ref[...].astype(BF16)
            dv_out[...] = dv_ref[...].astype(BF16)

    blk = lambda off: pl.BlockSpec((sub * t, dh), functools.partial(lambda h, i, off: (i, h + off), off=off))
    col = lambda off: pl.BlockSpec((s, dh), functools.partial(lambda h, i, off: (0, h + off), off=off))
    wide = SB_HEADS * dh
    return pl.pallas_call(
        body, name="sb_bwd", grid=(SB_HEADS, nq // sub),
        in_specs=[blk(0), col(SB_HEADS), col(2 * SB_HEADS), blk(0)],
        out_specs=[blk(0), col(0), col(0)],
        out_shape=[jax.ShapeDtypeStruct((s, wide), BF16)] * 3,
        scratch_shapes=[pltpu.VMEM((nq, t, t), F32), pltpu.VMEM((nq, t, t), F32),
                        pltpu.VMEM((sub, 2, t, t), F32), pltpu.VMEM((sub, 2, t, t), F32),
                        pltpu.VMEM((s, dh), F32), pltpu.VMEM((s, dh), F32)],
        compiler_params=_cparams(("parallel", "arbitrary")),
    )(qkv, qkv, qkv, dob)


def _alibi_slope(h):
    return float(2.0 ** (-8.0 * (h + 1) / C_HEADS))


GROUP_ROWS = C_GROUP * WINDOW


def _swa_window(n):
    qb = WINDOW
    i = lax.broadcasted_iota(jnp.int32, (qb, 2 * qb), 0)
    j = lax.broadcasted_iota(jnp.int32, (qb, 2 * qb), 1)
    d = i - j + qb
    valid = (d >= 0) & (d < WINDOW) & ((j >= qb) | (n > 0))
    stacked = lambda x: jnp.concatenate([x] * C_GROUP, axis=0)
    return stacked(d.astype(F32)), stacked(jnp.where(valid, 0.0, -1e30))


def _swa_group_cols(c, sink_ref):
    head = lax.shift_right_logical(lax.broadcasted_iota(jnp.int32, (GROUP_ROWS, 1), 0), WINDOW.bit_length() - 1)
    slope = jnp.zeros((GROUP_ROWS, 1), F32)
    sink = jnp.zeros((GROUP_ROWS, 1), F32)
    for hh in range(C_GROUP):
        slope = jnp.where(head == hh, _alibi_slope(c * C_GROUP + hh), slope)
        sink = jnp.where(head == hh, sink_ref[c * C_GROUP + hh], sink)
    return slope, sink


def _swa_probs(qg, kw, sink, slope, dist, bias, scale):
    sc = _dot(qg, kw, NT) * scale - slope * dist + bias
    m = jnp.maximum(jnp.max(sc, axis=-1, keepdims=True), sink)
    p = jnp.exp(sc - m)
    ps = jnp.exp(sink - m)
    inv = 1.0 / (jnp.sum(p, axis=-1, keepdims=True) + ps)
    return p * inv, ps * inv


def _stack_heads(x, c):
    hd = C_HEAD_DIM
    return jnp.concatenate([x[:, (c * C_GROUP + hh) * hd:(c * C_GROUP + hh + 1) * hd] for hh in range(C_GROUP)], axis=0)


SWA_SUB = 2


def _swa_kv_specs(sub, kvw):
    return [pl.BlockSpec((WINDOW, 2 * kvw), functools.partial(lambda n, w: (n * sub + w, 0), w=w)) for w in range(sub + 1)]


def _swa_fwd(q16, gate, kvp, sinks):
    s = q16.shape[0]
    qb = WINDOW
    hd = C_HEAD_DIM
    scale = 1.0 / math.sqrt(hd)
    kvw = C_KV_HEADS * hd
    sub = SWA_SUB if (s // qb) % SWA_SUB == 0 else 1

    def body(sink_ref, q_ref, g_ref, *refs):
        kv_refs, (o_ref, y_ref) = refs[:sub + 1], refs[sub + 1:]
        for u in range(sub):
            rows = slice(u * qb, (u + 1) * qb)
            q = q_ref[rows, :]
            kv = jnp.concatenate([kv_refs[u][...], kv_refs[u + 1][...]], axis=0)
            dist, bias = _swa_window(pl.program_id(0) * sub + u)
            for c in range(C_KV_HEADS):
                slope, sink = _swa_group_cols(c, sink_ref)
                kw = kv[:, c * hd:(c + 1) * hd]
                vw = kv[:, kvw + c * hd:kvw + (c + 1) * hd]
                p, _ = _swa_probs(_stack_heads(q, c), kw, sink, slope, dist, bias, scale)
                og = _dot(p.astype(BF16), vw, NN)
                for hh in range(C_GROUP):
                    h = c * C_GROUP + hh
                    o_ref[rows, h * hd:(h + 1) * hd] = og[hh * qb:(hh + 1) * qb, :]
        g = g_ref[...]
        y_ref[...] = (o_ref[...] * (g * _sigmoid(g))).astype(BF16)

    wide = C_HEADS * hd
    big = pl.BlockSpec((sub * qb, wide), lambda n: (n, 0))
    return pl.pallas_call(
        body, name="swa_fwd", grid=(s // (sub * qb),),
        in_specs=[pl.BlockSpec(memory_space=pltpu.SMEM), big, big] + _swa_kv_specs(sub, kvw),
        out_specs=[big, big],
        out_shape=[jax.ShapeDtypeStruct((s, wide), F32), jax.ShapeDtypeStruct((s, wide), BF16)],
        compiler_params=_cparams(("arbitrary",)),
    )(sinks, q16, gate, *([kvp] * (sub + 1)))


def _swa_bwd(q16, do16, kvp, sinks):
    s = q16.shape[0]
    qb = WINDOW
    hd = C_HEAD_DIM
    scale = 1.0 / math.sqrt(hd)
    kvw = C_KV_HEADS * hd
    sub = 1
    nstep = s // (sub * qb)

    def body(sink_ref, q_ref, do_ref, *refs):
        kv_refs, (dq_ref, dkv_ref, ds_ref, sacc_ref) = refs[:sub + 1], refs[sub + 1:]

        @pl.when(pl.program_id(0) == 0)
        def _():
            dkv_ref[...] = jnp.zeros_like(dkv_ref)
            sacc_ref[...] = jnp.zeros_like(sacc_ref)

        for u in range(sub):
            rows = slice(u * qb, (u + 1) * qb)
            n = pl.program_id(0) * sub + u
            q = q_ref[rows, :]
            do = do_ref[rows, :]
            kv = jnp.concatenate([kv_refs[u][...], kv_refs[u + 1][...]], axis=0)
            off = pl.multiple_of(n * qb, qb)
            dist, bias = _swa_window(n)
            for c in range(C_KV_HEADS):
                slope, sink = _swa_group_cols(c, sink_ref)
                kw = kv[:, c * hd:(c + 1) * hd]
                vw = kv[:, kvw + c * hd:kvw + (c + 1) * hd]
                qg = _stack_heads(q, c)
                dog = _stack_heads(do, c)
                p, ps = _swa_probs(qg, kw, sink, slope, dist, bias, scale)
                dp = _dot(dog, vw, NT)
                dd = jnp.sum(p * dp, axis=-1, keepdims=True)
                ds16 = (p * (dp - dd) * scale).astype(BF16)
                sacc_ref[c] += ps * dd
                dqg = _dot(ds16, kw, NN).astype(BF16)
                for hh in range(C_GROUP):
                    h = c * C_GROUP + hh
                    dq_ref[rows, h * hd:(h + 1) * hd] = dqg[hh * qb:(hh + 1) * qb, :]
                dkv_ref[pl.ds(off, 2 * qb), c * hd:(c + 1) * hd] += _dot(ds16, qg, TN)
                dkv_ref[pl.ds(off, 2 * qb), kvw + c * hd:kvw + (c + 1) * hd] += _dot(p.astype(BF16), dog, TN)

        @pl.when(pl.program_id(0) == nstep - 1)
        def _():
            lane = lax.broadcasted_iota(jnp.int32, (1, 128), 1)
            row = jnp.zeros((1, 128), F32)
            for c in range(C_KV_HEADS):
                for hh in range(C_GROUP):
                    tot = jnp.sum(sacc_ref[c, pl.ds(hh * qb, qb), :], axis=0, keepdims=True)
                    row = jnp.where(lane == c * C_GROUP + hh, -tot, row)
            ds_ref[...] = row

    wide = C_HEADS * hd
    big = pl.BlockSpec((sub * qb, wide), lambda n: (n, 0))
    return pl.pallas_call(
        body, name="swa_bwd", grid=(nstep,),
        in_specs=[pl.BlockSpec(memory_space=pltpu.SMEM), big, big] + _swa_kv_specs(sub, kvw),
        out_specs=[big,
                   pl.BlockSpec((s + qb, 2 * kvw), lambda n: (0, 0)), pl.BlockSpec((1, 128), lambda n: (0, 0))],
        out_shape=[jax.ShapeDtypeStruct((s, wide), BF16), jax.ShapeDtypeStruct((s + qb, 2 * kvw), F32),
                   jax.ShapeDtypeStruct((1, 128), F32)],
        scratch_shapes=[pltpu.VMEM((C_KV_HEADS, GROUP_ROWS, 1), F32)],
        compiler_params=_cparams(("arbitrary",)),
    )(sinks, q16, do16, *([kvp] * (sub + 1)))


def _adamw(parts, w, m, v, *, name, tr=496):
    npart, r, c = parts.shape
    tr = min(tr, r)
    assert r % tr == 0 and tr % 16 == 0
    c1 = 1.0 / (1.0 - ADAM_B1 ** ADAM_STEP)
    c2 = 1.0 / (1.0 - ADAM_B2 ** ADAM_STEP)

    def body(p_ref, w_ref, m_ref, v_ref, g_ref, d_ref, nm_ref, nv_ref):
        g = p_ref[0].astype(F32)
        for j in range(1, npart):
            g = g + p_ref[j].astype(F32)
        nm = ADAM_B1 * m_ref[...] + (1.0 - ADAM_B1) * g
        nv = ADAM_B2 * v_ref[...] + (1.0 - ADAM_B2) * (g * g)
        g_ref[...] = g
        nm_ref[...] = nm
        nv_ref[...] = nv
        d_ref[...] = -ADAM_LR * ((nm * c1) / (jnp.sqrt(nv * c2) + ADAM_EPS) + ADAM_WD * w_ref[...])

    spec = pl.BlockSpec((tr, c), lambda i: (i, 0))
    return pl.pallas_call(
        body, name=name, grid=(r // tr,),
        in_specs=[pl.BlockSpec((npart, tr, c), lambda i: (0, i, 0)), spec, spec, spec],
        out_specs=[spec] * 4, out_shape=[jax.ShapeDtypeStruct((r, c), F32)] * 4,
        compiler_params=_cparams(("parallel",)),
    )(parts, w, m, v)


GATHER = "gather"
A2A = "a2a"
GATHER_COLS = "gather_cols"
A2A_COLS = "a2a_cols"


def _gather_two_level(bufs, modes, x, *, tr=1024, name):
    nb = len(bufs)
    rows, cols = x.shape
    tr = min(tr, rows)
    steps = rows // tr

    def body(*refs):
        x_ref, src, x16_ref, dst = refs[0], refs[1:1 + nb], refs[1 + nb], refs[2 + nb:2 + 2 * nb]
        send_sems, recv_sems, local_sems = refs[2 + 2 * nb:]
        cx, cy, cc = lax.axis_index("x"), lax.axis_index("y"), lax.axis_index("c")
        me = 4 * cx + 2 * cy + cc
        here, sibling = (cx, cy, cc), (cx, cy, 1 - cc)

        def copy(b, k, origin, to, source):
            return pltpu.make_async_remote_copy(
                src_ref=source, dst_ref=_slot(dst[b], modes[b], origin), send_sem=send_sems.at[b, k - 1],
                recv_sem=recv_sems.at[b, k - 1], device_id=to, device_id_type=pl.DeviceIdType.MESH)

        def first_copies():
            local = [pltpu.make_async_copy(src[b], _slot(dst[b], modes[b], me), local_sems.at[b]) for b in range(nb)]
            sends = [copy(b, k, me, (cx ^ (k >> 2), cy ^ ((k >> 1) & 1), cc ^ (k & 1)), src[b])
                     for k in (1, 2, 4, 6) for b in range(nb)]
            return local, sends

        @pl.when(pl.program_id(0) == 0)
        def _():
            local, sends = first_copies()
            for cp in local + sends:
                cp.start()

        x16_ref[...] = x_ref[...].astype(BF16)

        @pl.when(pl.program_id(0) == steps - 1)
        def _():
            local, sends = first_copies()
            for j in (2, 4, 6):
                for b in range(nb):
                    copy(b, j, me ^ j, here, src[b]).wait_recv()
                    passed = copy(b, j ^ 1, me ^ j, sibling, _slot(dst[b], modes[b], me ^ j))
                    passed.start()
                    sends.append(passed)
            for k in (1, 3, 5, 7):
                for b in range(nb):
                    copy(b, k, me ^ k, here, src[b]).wait_recv()
            for cp in sends:
                cp.wait_send()
            for cp in local:
                cp.wait()

    hbm = pl.BlockSpec(memory_space=pl.ANY)
    tile = pl.BlockSpec((tr, cols), lambda i: (i, 0))
    return pl.pallas_call(
        body, name=name, grid=(steps,), in_specs=[tile] + [hbm] * nb, out_specs=[tile] + [hbm] * nb,
        out_shape=[jax.ShapeDtypeStruct(x.shape, BF16)] + _exchange_out_shapes(bufs, modes),
        scratch_shapes=_exchange_sems(nb),
        compiler_params=pltpu.CompilerParams(dimension_semantics=("arbitrary",), vmem_limit_bytes=VMEM_LIMIT,
                                             has_side_effects=True),
    )(x, *bufs)


def _exchange_out_shapes(bufs, gather):
    shapes = {GATHER: lambda s: (N_DEV,) + s, A2A: lambda s: s, GATHER_COLS: lambda s: (s[0], N_DEV * s[1]),
              A2A_COLS: lambda s: (N_DEV, s[0], s[1] // N_DEV)}
    return [jax.ShapeDtypeStruct(shapes[g](tuple(b.shape)), b.dtype) for b, g in zip(bufs, gather)]


def _sent(src, mode, peer):
    if mode == A2A:
        return src.at[peer]
    if mode == A2A_COLS:
        w = src.shape[1] // N_DEV
        return src.at[:, pl.ds(pl.multiple_of(peer * w, 128), w)]
    return src


def _slot(dst, mode, dev):
    if mode == GATHER_COLS:
        w = dst.shape[1] // N_DEV
        return dst.at[:, pl.ds(pl.multiple_of(dev * w, 128), w)]
    return dst.at[dev]


def _exchange_sems(nb):
    return [pltpu.SemaphoreType.DMA((nb, N_DEV - 1)), pltpu.SemaphoreType.DMA((nb, N_DEV - 1)),
            pltpu.SemaphoreType.DMA((nb,))]


def _exchange_copies(src, dst, gather, sems):
    send_sems, recv_sems, local_sems = sems
    x, y, c = lax.axis_index("x"), lax.axis_index("y"), lax.axis_index("c")
    me = 4 * x + 2 * y + c
    local, sends, recvs = [], [], []
    for b in range(len(src)):
        mine = _sent(src[b], gather[b], me)
        local.append(pltpu.make_async_copy(mine, _slot(dst[b], gather[b], me), local_sems.at[b]))
        for k in range(1, N_DEV):
            px, py, pc = x ^ (k >> 2), y ^ ((k >> 1) & 1), c ^ (k & 1)
            peer = 4 * px + 2 * py + pc
            pair = dict(send_sem=send_sems.at[b, k - 1], recv_sem=recv_sems.at[b, k - 1],
                        device_id_type=pl.DeviceIdType.MESH)
            sends.append(pltpu.make_async_remote_copy(
                src_ref=_sent(src[b], gather[b], peer), dst_ref=_slot(dst[b], gather[b], me), device_id=(px, py, pc),
                **pair))
            recvs.append(pltpu.make_async_remote_copy(
                src_ref=mine, dst_ref=_slot(dst[b], gather[b], peer), device_id=(x, y, c), **pair))
    return local, sends, recvs


def _exchange_start(src, dst, gather, sems):
    local, sends, _ = _exchange_copies(src, dst, gather, sems)
    for cp in local + sends:
        cp.start()


def _exchange_wait(src, dst, gather, sems):
    local, sends, recvs = _exchange_copies(src, dst, gather, sems)
    for cp in sends:
        cp.wait_send()
    for cp in recvs:
        cp.wait_recv()
    for cp in local:
        cp.wait()


R_SMALL = 16
REPL_ROWS = 272
REPL_LEN = REPL_ROWS * D_MODEL


def _small_block(parts):
    flat = jnp.concatenate(parts, axis=-1)
    lead = flat.ndim - 1
    return jnp.pad(flat[..., None, :], [(0, 0)] * lead + [(0, R_SMALL - 1), (0, D_MODEL - flat.shape[-1])])


def _small_vectors(conv_w, ln_g, ln_b):
    return _small_block([conv_w.reshape(512), ln_g.reshape(128), ln_b.reshape(128)])


REPL_SHAPES = ((1, 8, 128, 128), (1, 8, 128, 128), (1, 1024), (1, 1024), (1, 1024), (1, 1024), (1, 1024), (1, 1024),
               (1, 16))


def _pack_repl(parts):
    flat = jnp.concatenate([p.reshape(-1) for p in parts])
    return jnp.concatenate([flat, jnp.zeros((REPL_LEN - flat.shape[0],), F32)]).reshape(REPL_ROWS, D_MODEL)


def _unpack_repl(p):
    flat = p.reshape(-1)
    out, o = [], 0
    for shp in REPL_SHAPES:
        n = math.prod(shp)
        out.append(flat[o:o + n].reshape(shp))
        o += n
    return out


def kernel(x, e_w_in, e_conv_w, e_conv_b, e_w_gate_a, e_b_gate_a, e_w_gate_x, e_b_gate_x, e_lru_lambda, e_w_out, e_ln_g, e_ln_b, o_w_in, o_sinks, o_w_out, o_ln_g, o_ln_b, loss_target, m_e_w_in, m_e_conv_w, m_e_conv_b, m_e_w_gate_a, m_e_b_gate_a, m_e_w_gate_x, m_e_b_gate_x, m_e_lru_lambda, m_e_w_out, m_e_ln_g, m_e_ln_b, m_o_w_in, m_o_sinks, m_o_w_out, m_o_ln_g, m_o_ln_b, v_e_w_in, v_e_conv_w, v_e_conv_b, v_e_w_gate_a, v_e_b_gate_a, v_e_w_gate_x, v_e_b_gate_x, v_e_lru_lambda, v_e_w_out, v_e_ln_g, v_e_ln_b, v_o_w_in, v_o_sinks, v_o_w_out, v_o_ln_g, v_o_ln_b):
    d = D_MODEL
    x0 = x[0]
    target = loss_target[0]
    s = x0.shape[0]

    spack = _small_block([e_conv_w.reshape(512), o_ln_g.reshape(128), o_ln_b.reshape(128)])
    x0_16, wi_e, sall = _gather_two_level([e_w_in[0].astype(BF16), spack], [GATHER_COLS, GATHER], x0,
                                          name="gather_w_in")
    conv_w = sall[:, 0, 0:512].reshape(N_DEV, 4, 128).transpose(1, 0, 2).reshape(4, d)
    ln_g_o = sall[:, 0, 512:640].reshape(1, d)
    ln_b_o = sall[:, 0, 640:768].reshape(1, d)
    pvec = jnp.concatenate([conv_w, e_conv_b, e_b_gate_a, e_b_gate_x, e_lru_lambda], axis=0)
    wa16 = e_w_gate_a[0].astype(BF16)
    wx16 = e_w_gate_x[0].astype(BF16)
    sinks = o_sinks[0]

    h0a, wall_out_e = _matmul(x0_16, wi_e, mode="nn", n_out=2 * d, b_off=0, name="l0_in_a",
                              comm=([e_w_out[0].astype(BF16)], [GATHER]))
    qkv, wall_in_o = _matmul(x0_16, wi_e, mode="nn", n_out=3 * d, b_off=2 * d, out_dtype=BF16, name="l0_in_qkv",
                             comm=([o_w_in[0].astype(BF16)], [GATHER]))
    bg, wall_out_o = _matmul(x0_16, wi_e, mode="nn", n_out=d, b_off=5 * d, name="l0_in_bg",
                             comm=([o_w_out[0].astype(BF16)], [GATHER]))
    wo_e = wall_out_e.reshape(2 * d, d)
    wi_o = wall_in_o.transpose(1, 0, 2).reshape(d, 2304)
    wi_o = jnp.concatenate([wi_o[:, :1024], wi_o[:, 1280:], wi_o[:, 1024:1280]], axis=1)
    wo_o = wall_out_o.reshape(d, d)
    ya, hst = _rglru_fwd(h0a, pvec, wa16, wx16)
    ob, yb = _sb_fwd(qkv, bg)
    z0, x1, x1_16 = _matmul([ya, yb], wo_e, mode="nn", tm=512, name="l0_out",
                            epi=(_epi_ln_fwd, [(x0, 0)], [e_ln_g, e_ln_b], [F32, F32, BF16], 0))

    q1 = _matmul(x1_16, wi_o, mode="nn", n_out=d, b_off=0, out_dtype=BF16, name="l1_in_q")
    g1 = _matmul(x1_16, wi_o, mode="nn", n_out=d, b_off=d, name="l1_in_g")
    kv1 = _matmul(x1_16, wi_o, mode="nn", n_out=256, b_off=2 * d, tn=256, out_dtype=BF16, name="l1_in_kv")
    kvp = jnp.concatenate([jnp.zeros((WINDOW, 256), BF16), kv1], axis=0)
    o1, y1 = _swa_fwd(q1, g1, kvp, sinks)
    dz1, dz1_16, loss_cols, dg_o, db_o = _matmul(
        y1, wo_o, mode="nn", tm=512, name="l1_out",
        epi=(_epi_ln_loss_bwd, [(x1, 0), (target, 0)], [ln_g_o, ln_b_o], [F32, BF16], 3))
    loss_hi = jnp.sum(loss_cols).astype(BF16).astype(F32)
    loss_terms = jnp.stack([loss_hi, jnp.sum(loss_cols) - loss_hi]).reshape(1, 2)

    dwo_o = _matmul(y1, dz1_16, mode="tn", out_dtype=BF16, name="l1_dwout")
    do1, dg1, parts_out_o = _matmul(dz1_16, wo_o, mode="nt", tm=512, name="l1_dy",
                                    comm=([dwo_o.reshape(N_DEV, 128, d)], [A2A]),
                                    epi=(_epi_gate_bwd, [(o1, 0), (g1, 0)], [], [BF16, BF16], 0))
    dq1, dkvp, dsink = _swa_bwd(q1, do1, kvp, sinks)
    dkv1 = dkvp[WINDOW:].astype(BF16)
    dh1 = [dq1, dg1, dkv1]
    dw_qg = _matmul(x1_16, [dq1, dg1], mode="tn", out_dtype=BF16, name="l1_dwin_qg")
    dw_kv = _matmul(x1_16, dkv1, mode="tn", out_dtype=BF16, name="l1_dwin_kv")
    dwi_o = jnp.concatenate([dw_qg[:, :1024], dw_kv, dw_qg[:, 1024:]], axis=1)
    dz0, dz0_16, dg_e, db_e, parts_in_o = _matmul(
        dh1, wi_o, mode="nt", tm=512, tk=2304, name="l1_dx",
        comm=([dwi_o.reshape(d, N_DEV, 288).transpose(1, 0, 2)], [A2A]),
        epi=(_epi_ln_bwd, [(dz1, 0), (z0, 0)], [e_ln_g], [F32, BF16], 2))

    dwo_e = _matmul([ya, yb], dz0_16, mode="tn", out_dtype=BF16, name="l0_dwout")
    dhst, dag, parts_out_e = _matmul(
        dz0_16, wo_e, mode="nt", tm=512, n_out=d, b_off=0, name="l0_dy_a",
        comm=([dwo_e.reshape(N_DEV, 256, d)], [A2A]),
        epi=(_epi_gate_bwd, [(hst, 0), (h0a, d)], [], [F32, BF16], 0))
    dax, dwa, dwx, dpv = _rglru_bwd(h0a, hst, dhst, pvec, wa16, wx16)
    repl = _pack_repl([dwa, dwx, dpv[4:5], dpv[5:6], dpv[6:7], dpv[7:8], dg_e, db_e, dsink[:, :C_HEADS], loss_terms])
    dob, dbg, parts_repl = _matmul(dz0_16, wo_e, mode="nt", tm=512, n_out=d, b_off=d, name="l0_dy_b",
                                   comm=([repl.astype(BF16)], [GATHER]),
                                   epi=(_epi_gate_bwd, [(ob, 0), (bg, 0)], [], [BF16, BF16], 0))
    dq0, dk0, dv0 = _sb_bwd(qkv, dob)
    dh0 = [dax, dag, dq0, dk0, dv0, dbg]
    dwi_e = _matmul(x0_16, dh0, mode="tn", out_dtype=BF16, name="l0_dwin")
    small = _small_block([dpv[0:4].reshape(4, N_DEV, 128).transpose(1, 0, 2).reshape(N_DEV, 512),
                          dg_o.reshape(N_DEV, 128), db_o.reshape(N_DEV, 128)]).astype(BF16)
    grad_x, parts_in_e, parts_small = _matmul(dh0, wi_e, mode="nt", tm=512, res=dz0, alpha=ALPHA, name="l0_dx",
                                              comm=([dwi_e, small], [A2A_COLS, A2A]))

    res_in_e = _adamw(parts_in_e, e_w_in[0], m_e_w_in[0], v_e_w_in[0], tr=512, name="adamw_w_in_e")
    res_out_e = _adamw(parts_out_e, e_w_out[0], m_e_w_out[0], v_e_w_out[0], name="adamw_w_out_e")
    res_in_o = _adamw(parts_in_o, o_w_in[0], m_o_w_in[0], v_o_w_in[0], tr=512, name="adamw_w_in_o")
    res_out_o = _adamw(parts_out_o, o_w_out[0], m_o_w_out[0], v_o_w_out[0], name="adamw_w_out_o")
    res_small = _adamw(parts_small, _small_vectors(e_conv_w, o_ln_g, o_ln_b),
                       _small_vectors(m_e_conv_w, m_o_ln_g, m_o_ln_b),
                       _small_vectors(v_e_conv_w, v_o_ln_g, v_o_ln_b), name="adamw_vectors")

    w_r = _pack_repl([e_w_gate_a, e_w_gate_x, e_conv_b, e_b_gate_a, e_b_gate_x, e_lru_lambda, e_ln_g, e_ln_b, o_sinks])
    m_r = _pack_repl([m_e_w_gate_a, m_e_w_gate_x, m_e_conv_b, m_e_b_gate_a, m_e_b_gate_x, m_e_lru_lambda, m_e_ln_g,
                      m_e_ln_b, m_o_sinks])
    v_r = _pack_repl([v_e_w_gate_a, v_e_w_gate_x, v_e_conv_b, v_e_b_gate_a, v_e_b_gate_x, v_e_lru_lambda, v_e_ln_g,
                      v_e_ln_b, v_o_sinks])
    g_r, d_r, nm_r, nv_r = _adamw(parts_repl, w_r, m_r, v_r, name="adamw_replicated")
    loss_at = sum(math.prod(shp) for shp in REPL_SHAPES)
    loss = g_r[loss_at // d, loss_at % d] + g_r[(loss_at + 1) // d, (loss_at + 1) % d]

    def assemble(i, rp):
        vec = res_small[i][0]
        cw, lg_o, lb_o = vec[0:512].reshape(1, 4, 128), vec[512:640].reshape(1, 128), vec[640:768].reshape(1, 128)
        w_a, w_x, cb, b_a, b_x, lam, lg_e, lb_e, snk = _unpack_repl(rp)
        return [res_in_e[i][None], cw, cb, w_a, b_a, w_x, b_x, lam, res_out_e[i][None], lg_e, lb_e,
                res_in_o[i][None], snk, res_out_o[i][None], lg_o, lb_o]

    return (loss, grad_x[None], *assemble(0, g_r), *assemble(1, d_r), *assemble(2, nm_r), *assemble(3, nv_r))
```

```python
import functools
import math

import jax
import jax.numpy as jnp
from jax import lax
from jax.experimental import pallas as pl
from jax.experimental.pallas import tpu as pltpu

F32 = jnp.float32
BF16 = jnp.bfloat16

N_DEV = 8
D_MODEL = 1024
LRU_BLOCKS = 8
LRU_BLOCK = 128
LRU_C = 8.0
SB_HEADS = 8
SB_HEAD_DIM = 128
C_HEADS = 16
C_KV_HEADS = 2
C_GROUP = 8
C_HEAD_DIM = 64
WINDOW = 128
DEPTH = 2
ALPHA = float((2 * DEPTH) ** 0.25)
LN_EPS = 1e-5
ADAM_LR = 0.001
ADAM_B1 = 0.9
ADAM_B2 = 0.999
ADAM_EPS = 1e-08
ADAM_WD = 0.01
ADAM_STEP = 10

VMEM_LIMIT = 56 * 1024 * 1024

NN = ((1,), (0,))
NT = ((1,), (1,))
TN = ((0,), (0,))


def _dot(a, b, dims):
    return lax.dot_general(a, b, (dims, ((), ())), preferred_element_type=F32)


def _sigmoid(x):
    return 1.0 / (1.0 + jnp.exp(-x))


def _cparams(sem, vmem=VMEM_LIMIT):
    return pltpu.CompilerParams(dimension_semantics=sem, vmem_limit_bytes=vmem)


def _matmul(a, b, *, mode, n_out=None, b_off=0, out_dtype=F32, res=None, alpha=1.0,
            tm=1024, tn=1024, tk=1024, comm=None, epi=None, name):
    a_list = list(a) if isinstance(a, (list, tuple)) else [a]
    b_list = list(b) if isinstance(b, (list, tuple)) else [b]
    if mode == "tn":
        k = a_list[0].shape[0]
        m = sum(p.shape[1] for p in a_list)
        n = n_out if n_out is not None else sum(p.shape[1] for p in b_list)
        assert all(p.shape[1] == tm for p in a_list) or len(a_list) == 1
        assert all(p.shape[1] == tn for p in b_list) or len(b_list) == 1
    else:
        assert len(b_list) == 1
        m = a_list[0].shape[0]
        k = sum(p.shape[1] for p in a_list)
        n = n_out if n_out is not None else (b.shape[0] if mode == "nt" else b.shape[1])
        if len(a_list) > 1:
            tk = k
    tm, tn, tk = min(tm, m), min(tn, n), min(tk, k)
    assert m % tm == 0 and n % tn == 0 and k % tk == 0 and b_off % tn == 0
    grid = (m // tm, n // tn, k // tk)
    nk = grid[2]
    jo = b_off // tn
    dims = {"nn": NN, "nt": NT, "tn": TN}[mode]
    nc = len(comm[0]) if comm else 0
    if epi:
        assert res is None
        epi_fn, epi_rows, epi_vecs, epi_dtypes, n_sums = epi
        assert n_sums == 0 or grid[1] == 1
    else:
        epi_rows, epi_vecs, epi_dtypes, n_sums = [], [], [out_dtype], 0
    n_ab = len(a_list) + len(b_list)
    n_in = n_ab + (res is not None) + len(epi_rows) + len(epi_vecs)
    n_res = len(epi_dtypes) + n_sums

    def body(*refs):
        a_refs, b_refs = refs[:len(a_list)], refs[len(a_list):n_ab]
        r_ref = refs[n_ab] if res is not None else None
        row_refs = refs[n_in - len(epi_rows) - len(epi_vecs):n_in - len(epi_vecs)]
        vec_refs = refs[n_in - len(epi_vecs):n_in]
        o_refs = refs[n_in + nc:n_in + nc + len(epi_dtypes)]
        sum_refs = refs[n_in + nc + len(epi_dtypes):n_in + nc + n_res]
        if comm:
            step = [pl.program_id(ax) for ax in range(3)]
            c_src = refs[n_in:n_in + nc]
            c_dst = refs[n_in + nc + n_res:n_in + 2 * nc + n_res]
            sems = refs[len(refs) - 3:]

            @pl.when((step[0] == 0) & (step[1] == 0) & (step[2] == 0))
            def _():
                _exchange_start(c_src, c_dst, comm[1], sems)

        def finish(out):
            if not epi:
                if r_ref is not None:
                    out = out + alpha * r_ref[...]
                o_refs[0][...] = out.astype(out_dtype)
                return
            outs, sums = epi_fn(out, [r[...] for r in row_refs], [v[...] for v in vec_refs])
            for o_ref, o in zip(o_refs, outs):
                o_ref[...] = o.astype(o_ref.dtype)
            if sum_refs:
                first_rows = pl.program_id(0) == 0

                @pl.when(first_rows)
                def _():
                    for s_ref, part_sum in zip(sum_refs, sums):
                        s_ref[...] = part_sum

                @pl.when(jnp.logical_not(first_rows))
                def _():
                    for s_ref, part_sum in zip(sum_refs, sums):
                        s_ref[...] += part_sum

        def accumulate(part):
            if nk == 1:
                finish(part)
                return
            acc_ref = refs[n_in + 2 * nc + n_res]
            kk = pl.program_id(2)

            @pl.when(kk == 0)
            def _():
                acc_ref[...] = part

            @pl.when(kk > 0)
            def _():
                acc_ref[...] += part

            @pl.when(kk == nk - 1)
            def _():
                finish(acc_ref[...])

        def product(a_ref, b_val):
            return _dot(a_ref[...].astype(BF16), b_val.astype(BF16), dims)

        if mode != "tn" and len(a_list) > 1:
            part, off = None, 0
            for a_ref in a_refs:
                w = a_ref.shape[1]
                b_val = b_refs[0][off:off + w, :] if mode == "nn" else b_refs[0][:, off:off + w]
                part = product(a_ref, b_val) if part is None else part + product(a_ref, b_val)
                off += w
            accumulate(part)
        elif len(a_list) > 1 or len(b_list) > 1:
            for pa, a_ref in enumerate(a_refs):
                for pb, b_ref in enumerate(b_refs):
                    picked = ([pl.program_id(0) == pa] if len(a_list) > 1 else []) + (
                        [pl.program_id(1) == pb] if len(b_list) > 1 else [])
                    pl.when(functools.reduce(jnp.logical_and, picked))(
                        functools.partial(lambda a_ref, b_ref: accumulate(product(a_ref, b_ref[...])), a_ref, b_ref))
        else:
            accumulate(product(a_refs[0], b_refs[0][...]))

        if comm:
            @pl.when((step[0] == grid[0] - 1) & (step[1] == grid[1] - 1) & (step[2] == grid[2] - 1))
            def _():
                _exchange_wait(c_src, c_dst, comm[1], sems)

    def held(axis, p):
        def index(i, j, kk):
            return (jnp.where((i, j)[axis] == p, kk, 0), 0)
        return index

    if mode == "tn":
        if len(a_list) > 1:
            a_specs = [pl.BlockSpec((tk, tm), held(0, p)) for p in range(len(a_list))]
        else:
            a_specs = [pl.BlockSpec((tk, tm), lambda i, j, kk: (kk, i))]
    elif len(a_list) > 1:
        a_specs = [pl.BlockSpec((tm, p.shape[1]), lambda i, j, kk: (i, 0)) for p in a_list]
    else:
        a_specs = [pl.BlockSpec((tm, tk), lambda i, j, kk: (i, kk))]
    if mode == "nt":
        b_specs = [pl.BlockSpec((tn, tk), lambda i, j, kk: (j + jo, kk))]
    elif len(b_list) > 1:
        b_specs = [pl.BlockSpec((tk, tn), held(1, p)) for p in range(len(b_list))]
    else:
        b_specs = [pl.BlockSpec((tk, tn), lambda i, j, kk: (kk, j + jo))]
    o_spec = pl.BlockSpec((tm, tn), lambda i, j, kk: (i, j))
    hbm = pl.BlockSpec(memory_space=pl.ANY)
    row_specs = [pl.BlockSpec((tm, tn), functools.partial(lambda i, j, kk, o: (i, j + o), o=off // tn))
                 for (_, off) in epi_rows]
    vec_specs = [pl.BlockSpec(v.shape, functools.partial(lambda i, j, kk, nd: (0,) * nd, nd=v.ndim)) for v in epi_vecs]
    in_specs = a_specs + b_specs + ([o_spec] if res is not None else []) + row_specs + vec_specs + [hbm] * nc
    args = (a_list + b_list + ([res] if res is not None else []) + [r for (r, _) in epi_rows] + list(epi_vecs)
            + (list(comm[0]) if comm else []))
    out_specs = [o_spec] * len(epi_dtypes) + [pl.BlockSpec((1, tn), lambda i, j, kk: (0, j))] * n_sums + [hbm] * nc
    out_shape = ([jax.ShapeDtypeStruct((m, n), dt) for dt in epi_dtypes] + [jax.ShapeDtypeStruct((1, n), F32)] * n_sums
                 + (_exchange_out_shapes(*comm) if comm else []))
    scratch = ([pltpu.VMEM((tm, tn), F32)] if nk > 1 else []) + (_exchange_sems(nc) if comm else [])
    if comm or n_sums or n_ab > 2:
        params = pltpu.CompilerParams(dimension_semantics=("arbitrary",) * 3, vmem_limit_bytes=VMEM_LIMIT,
                                      has_side_effects=bool(comm))
    else:
        params = _cparams(("parallel", "parallel", "arbitrary"))
    outs = pl.pallas_call(
        body, name=name, grid=grid, in_specs=in_specs, out_specs=out_specs, out_shape=out_shape,
        scratch_shapes=scratch, compiler_params=params,
    )(*args)
    return tuple(outs) if (comm or epi) else outs[0]


def _ln_stats(z):
    mu = jnp.mean(z, axis=-1, keepdims=True)
    zc = z - mu
    var = jnp.mean(zc * zc, axis=-1, keepdims=True)
    rstd = lax.rsqrt(var + LN_EPS)
    return zc * rstd, rstd


def _ln_bwd(dy, xhat, rstd, g):
    dxh = dy * g
    m1 = jnp.mean(dxh, axis=-1, keepdims=True)
    m2 = jnp.mean(dxh * xhat, axis=-1, keepdims=True)
    return rstd * (dxh - m1 - xhat * m2)


def _colsum(x):
    return jnp.sum(x, axis=0, keepdims=True)


def _epi_ln_fwd(acc, rows, vecs):
    z = ALPHA * rows[0] + acc
    xhat, _ = _ln_stats(z)
    x1 = xhat * vecs[0] + vecs[1]
    return [z, x1, x1], []


def _epi_ln_loss_bwd(acc, rows, vecs):
    inv_c = 1.0 / acc.shape[-1]
    z = ALPHA * rows[0] + acc
    xhat, rstd = _ln_stats(z)
    d = xhat * vecs[0] + vecs[1] - rows[1]
    dy = d * inv_c
    dz = _ln_bwd(dy, xhat, rstd, vecs[0])
    return [dz, dz], [_colsum(0.5 * d * d * inv_c), _colsum(dy * xhat), _colsum(dy)]


def _epi_ln_bwd(acc, rows, vecs):
    dy = acc + ALPHA * rows[0]
    xhat, rstd = _ln_stats(rows[1])
    dz = _ln_bwd(dy, xhat, rstd, vecs[0])
    return [dz, dz], [_colsum(dy * xhat), _colsum(dy)]


def _epi_gate_bwd(acc, rows, vecs):
    oo, gg = rows
    sg = _sigmoid(gg)
    return [acc * (gg * sg), acc * oo * (sg * (1.0 + gg * (1.0 - sg)))], []


LRU_T = 256
HALO = 8


def _log1p(y):
    u = 1.0 + y
    return jnp.where(u == 1.0, y, jnp.log(u) * (y / (u - 1.0)))


def _lru_gates(c, wa_ref, wx_ref, pv):
    c16 = c.astype(BF16)
    pre_r = jnp.concatenate(
        [_dot(c16[:, n * LRU_BLOCK:(n + 1) * LRU_BLOCK], wa_ref[n], NN) for n in range(LRU_BLOCKS)], axis=1)
    pre_i = jnp.concatenate(
        [_dot(c16[:, n * LRU_BLOCK:(n + 1) * LRU_BLOCK], wx_ref[n], NN) for n in range(LRU_BLOCKS)], axis=1)
    r = _sigmoid(pre_r + pv[5:6])
    ig = _sigmoid(pre_i + pv[6:7])
    lam = pv[7:8]
    ls = jnp.minimum(lam, 0.0) - _log1p(jnp.exp(-jnp.abs(lam)))
    la = LRU_C * r * ls
    a = jnp.exp(la)
    a2 = a * a
    m = jnp.sqrt(jnp.tanh(-la) * (a2 + 1.0))
    return c16, r, ig, ls, la, a, a2, m


def _conv(ext_ref, x, pv, t):
    return (pv[4:5] + pv[3:4] * x + pv[2:3] * ext_ref[pl.ds(HALO - 1, t), :]
            + pv[1:2] * ext_ref[pl.ds(HALO - 2, t), :] + pv[0:1] * ext_ref[pl.ds(HALO - 3, t), :])


def _rglru_fwd(h0a, pvec, wa16, wx16):
    s = h0a.shape[0]
    w = D_MODEL
    t = min(LRU_T, s)
    assert s % t == 0

    def body(ax_ref, ag_ref, pv_ref, wa_ref, wx_ref, ya_ref, h_ref, ext_ref, a_ref, u_ref, hc_ref):
        i = pl.program_id(0)

        @pl.when(i == 0)
        def _():
            ext_ref[pl.ds(0, HALO), :] = jnp.zeros((HALO, w), F32)
            hc_ref[...] = jnp.zeros((1, w), F32)

        pv = pv_ref[...]
        ax = ax_ref[...]
        ext_ref[pl.ds(HALO, t), :] = ax
        c = _conv(ext_ref, ax, pv, t)
        ext_ref[pl.ds(0, HALO), :] = ax[t - HALO:, :]
        _, _, ig, _, _, a, _, m = _lru_gates(c, wa_ref, wx_ref, pv)
        a_ref[...] = a
        u_ref[...] = m * (ig * c)

        def steps(k, h):
            base = pl.multiple_of(k * HALO, HALO)
            for r in range(HALO):
                h = a_ref[pl.ds(base + r, 1), :] * h + u_ref[pl.ds(base + r, 1), :]
                h_ref[pl.ds(base + r, 1), :] = h
            return h

        hc_ref[...] = lax.fori_loop(0, t // HALO, steps, hc_ref[...])
        ag = ag_ref[...]
        ya_ref[...] = (h_ref[...] * (ag * _sigmoid(ag))).astype(BF16)

    full = lambda shp: pl.BlockSpec(shp, lambda i: (0,) * len(shp))
    return pl.pallas_call(
        body, name="rglru_fwd", grid=(s // t,),
        in_specs=[pl.BlockSpec((t, w), lambda i: (i, 0)), pl.BlockSpec((t, w), lambda i: (i, 1)),
                  full((8, w)), full((LRU_BLOCKS, LRU_BLOCK, LRU_BLOCK)), full((LRU_BLOCKS, LRU_BLOCK, LRU_BLOCK))],
        out_specs=[pl.BlockSpec((t, w), lambda i: (i, 0)), pl.BlockSpec((t, w), lambda i: (i, 0))],
        out_shape=[jax.ShapeDtypeStruct((s, w), BF16), jax.ShapeDtypeStruct((s, w), F32)],
        scratch_shapes=[pltpu.VMEM((t + HALO, w), F32), pltpu.VMEM((t, w), F32), pltpu.VMEM((t, w), F32),
                        pltpu.VMEM((1, w), F32)],
        compiler_params=_cparams(("arbitrary",)),
    )(h0a, h0a, pvec, wa16, wx16)


def _rglru_bwd(h0a, h, dh, pvec, wa16, wx16):
    s = h0a.shape[0]
    w = D_MODEL
    t = min(LRU_T, s)
    nb = s // t
    hb = t // HALO

    def body(ax_ref, axh_ref, h_ref, hh_ref, dh_ref, pv_ref, wa_ref, wx_ref,
             dax_ref, dwa_ref, dwx_ref, dpv_ref, ext_ref, hext_ref, dcext_ref, a_ref, g_ref, gc_ref):
        i = pl.program_id(0)
        blk = nb - 1 - i

        @pl.when(i == 0)
        def _():
            dwa_ref[...] = jnp.zeros_like(dwa_ref)
            dwx_ref[...] = jnp.zeros_like(dwx_ref)
            dpv_ref[...] = jnp.zeros_like(dpv_ref)
            gc_ref[...] = jnp.zeros((1, w), F32)
            dcext_ref[pl.ds(t, HALO), :] = jnp.zeros((HALO, w), F32)

        pv = pv_ref[...]
        ax = ax_ref[...]
        keep = (blk > 0).astype(F32)
        ext_ref[pl.ds(0, HALO), :] = axh_ref[...] * keep
        ext_ref[pl.ds(HALO, t), :] = ax
        hext_ref[pl.ds(0, HALO), :] = hh_ref[...] * keep
        hext_ref[pl.ds(HALO, t), :] = h_ref[...]
        c = _conv(ext_ref, ax, pv, t)
        c16, r, ig, ls, _, a, a2, m = _lru_gates(c, wa_ref, wx_ref, pv)
        a_ref[...] = a

        def steps(k, carry):
            base = pl.multiple_of(t - HALO - k * HALO, HALO)
            for r in reversed(range(HALO)):
                g = dh_ref[pl.ds(base + r, 1), :] + carry
                g_ref[pl.ds(base + r, 1), :] = g
                carry = a_ref[pl.ds(base + r, 1), :] * g
            return carry

        gc_ref[...] = lax.fori_loop(0, t // HALO, steps, gc_ref[...])
        g = g_ref[...]
        hprev = hext_ref[pl.ds(HALO - 1, t), :]
        gm = g * m
        d_la = g * hprev * a - (g * ig * c) * a2 / m
        d_pr = d_la * (LRU_C * ls) * r * (1.0 - r)
        d_pi = gm * c * ig * (1.0 - ig)
        dc = gm * ig
        dpr16 = d_pr.astype(BF16)
        dpi16 = d_pi.astype(BF16)
        dc_parts = []
        for n in range(LRU_BLOCKS):
            sl = slice(n * LRU_BLOCK, (n + 1) * LRU_BLOCK)
            dwa_ref[n] += _dot(c16[:, sl], dpr16[:, sl], TN)
            dwx_ref[n] += _dot(c16[:, sl], dpi16[:, sl], TN)
            dc_parts.append(_dot(dpr16[:, sl], wa_ref[n], NT) + _dot(dpi16[:, sl], wx_ref[n], NT))
        dc = dc + jnp.concatenate(dc_parts, axis=1)
        dcext_ref[pl.ds(0, t), :] = dc
        dax = (pv[3:4] * dc + pv[2:3] * dcext_ref[pl.ds(1, t), :] + pv[1:2] * dcext_ref[pl.ds(2, t), :]
               + pv[0:1] * dcext_ref[pl.ds(3, t), :])
        dax_ref[...] = dax.astype(BF16)
        dcext_ref[pl.ds(t, HALO), :] = dc[:HALO, :]
        sums = [_colsum(dc * ext_ref[pl.ds(HALO - 3 + kk, t), :]) for kk in range(4)]
        sums += [_colsum(dc), _colsum(d_pr), _colsum(d_pi), _colsum(d_la * (LRU_C * r))]
        for kk, part in enumerate(sums):
            dpv_ref[pl.ds(kk, 1), :] += part

        @pl.when(i == nb - 1)
        def _():
            lam = pv[7:8]
            dls = dpv_ref[pl.ds(7, 1), :]
            dpv_ref[pl.ds(7, 1), :] = dls * _sigmoid(-lam)

    full = lambda shp: pl.BlockSpec(shp, lambda i: (0,) * len(shp))
    rev = lambda cb: pl.BlockSpec((t, w), functools.partial(lambda i, cb: (nb - 1 - i, cb), cb=cb))
    halo = lambda cb: pl.BlockSpec(
        (HALO, w), functools.partial(lambda i, cb: (jnp.maximum((nb - 1 - i) * hb - 1, 0), cb), cb=cb))
    gw = (LRU_BLOCKS, LRU_BLOCK, LRU_BLOCK)
    return pl.pallas_call(
        body, name="rglru_bwd", grid=(nb,),
        in_specs=[rev(0), halo(0), rev(0), halo(0), rev(0), full((8, w)), full(gw), full(gw)],
        out_specs=[rev(0), full(gw), full(gw), full((8, w))],
        out_shape=[jax.ShapeDtypeStruct((s, w), BF16), jax.ShapeDtypeStruct(gw, F32),
                   jax.ShapeDtypeStruct(gw, F32), jax.ShapeDtypeStruct((8, w), F32)],
        scratch_shapes=[pltpu.VMEM((t + HALO, w), F32), pltpu.VMEM((t + HALO, w), F32),
                        pltpu.VMEM((t + HALO, w), F32), pltpu.VMEM((t, w), F32), pltpu.VMEM((t, w), F32),
                        pltpu.VMEM((1, w), F32)],
        compiler_params=_cparams(("arbitrary",)),
    )(h0a, h0a, h, h, dh, pvec, wa16, wx16)


SB_T = 256
SB_SUB = 8


def _split16(x):
    hi = x.astype(BF16)
    lo = (x - hi.astype(F32)).astype(BF16)
    return jnp.concatenate([hi, lo], axis=0)


def _sb_tile(q, k, scale, tri, run, causal):
    tq = q.shape[0]
    z = _dot(q, k, NT) * scale
    log1mb = -(jnp.maximum(z, 0.0) + jnp.log(1.0 + jnp.exp(-jnp.abs(z))))
    if causal is not None:
        log1mb = jnp.where(causal, log1mb, 0.0)
    cs = _dot(_split16(log1mb), tri, NN)
    cs = cs[:tq] + cs[tq:]
    wgt = jnp.exp(z + cs + run)
    if causal is not None:
        wgt = jnp.where(causal, wgt, 0.0)
    return z, wgt, run + cs[:, 0:1]


SB_DEAD = -105.0


def _sb_alive(run):
    return (jnp.max(run) > SB_DEAD).astype(jnp.int32)


def _sb_more(carry):
    return (carry[0] >= 0) & (carry[1] > 0)


def _sb_fwd(qkv, bg):
    s = qkv.shape[0]
    t = min(SB_T, s)
    nq = s // t
    dh = SB_HEAD_DIM
    scale = 1.0 / math.sqrt(dh)

    sub = SB_SUB if nq % SB_SUB == 0 else 1

    def body(q_ref, k_ref, v_ref, bg_ref, o_ref, y_ref):
        row = lax.broadcasted_iota(jnp.int32, (t, t), 0)
        col = lax.broadcasted_iota(jnp.int32, (t, t), 1)
        tri = jnp.where(row >= col, 1.0, 0.0).astype(BF16)
        causal = col < row
        def tile(q, kb, run, mask):
            off = pl.multiple_of(kb * t, t)
            _, wgt, run = _sb_tile(q, k_ref[pl.ds(off, t), :], scale, tri, run, mask)
            return run, _dot(wgt.astype(BF16), v_ref[pl.ds(off, t), :], NN)

        blocks = []
        for u in range(sub):
            iq = pl.program_id(1) * sub + u
            q = q_ref[pl.ds(u * t, t), :]
            run_d, acc_d = tile(q, iq, jnp.zeros((t, 1), F32), causal)
            run_l, acc_l = tile(q, jnp.maximum(iq - 1, 0), run_d, None)
            blocks.append((iq, q, jnp.where(iq > 0, run_l, run_d), acc_d + jnp.where(iq > 0, acc_l, 0.0)))
        for u, (iq, q, run, acc) in enumerate(blocks):
            def loop(carry, q=q):
                kb, _, run, acc = carry
                run, part = tile(q, kb, run, None)
                return kb - 1, _sb_alive(run), run, acc + part

            _, _, run, acc = lax.while_loop(_sb_more, loop, (iq - 2, _sb_alive(run), run, acc))
            rows = pl.ds(u * t, t)
            o_ref[rows, :] = acc
            g = bg_ref[rows, :]
            y_ref[rows, :] = (acc * (g * _sigmoid(g))).astype(BF16)

    blk = lambda off: pl.BlockSpec((sub * t, dh), functools.partial(lambda h, i, off: (i, h + off), off=off))
    col = lambda off: pl.BlockSpec((s, dh), functools.partial(lambda h, i, off: (0, h + off), off=off))
    return pl.pallas_call(
        body, name="sb_fwd", grid=(SB_HEADS, nq // sub),
        in_specs=[blk(0), col(SB_HEADS), col(2 * SB_HEADS), blk(0)],
        out_specs=[blk(0), blk(0)],
        out_shape=[jax.ShapeDtypeStruct((s, SB_HEADS * dh), F32), jax.ShapeDtypeStruct((s, SB_HEADS * dh), BF16)],
        compiler_params=_cparams(("parallel", "arbitrary")),
    )(qkv, qkv, qkv, bg)


def _sb_bwd(qkv, dob):
    s = qkv.shape[0]
    t = min(SB_T, s)
    nq = s // t
    dh = SB_HEAD_DIM
    scale = 1.0 / math.sqrt(dh)

    sub = SB_SUB if nq % SB_SUB == 0 else 1

    def body(q_ref, k_ref, v_ref, do_ref, dq_ref, dk_out, dv_out, e_ref, b_ref, pe_ref, pb_ref, dk_ref, dv_ref):
        @pl.when(pl.program_id(1) == 0)
        def _():
            dk_ref[...] = jnp.zeros_like(dk_ref)
            dv_ref[...] = jnp.zeros_like(dv_ref)

        row = lax.broadcasted_iota(jnp.int32, (t, t), 0)
        col = lax.broadcasted_iota(jnp.int32, (t, t), 1)
        tri = jnp.where(row >= col, 1.0, 0.0).astype(BF16)
        tri_x = jnp.where(row < col, 1.0, 0.0).astype(BF16)
        causal = col < row

        def sweep1(q, do, kb, e_dst, b_dst, run, mask, live=None):
            off = pl.multiple_of(kb * t, t)
            v = v_ref[pl.ds(off, t), :]
            z, wgt, run = _sb_tile(q, k_ref[pl.ds(off, t), :], scale, tri, run, mask)
            if live is not None:
                wgt = jnp.where(live, wgt, 0.0)
            e_dst[...] = wgt * _dot(do, v, NT)
            b_dst[...] = _sigmoid(z)
            dv_ref[pl.ds(off, t), :] += _dot(wgt.astype(BF16), do, TN)
            return run

        def sweep2(q, kb, e_src, b_src, pre, mask, live=None):
            off = pl.multiple_of(kb * t, t)
            e = e_src[...]
            beta = b_src[...]
            ps = _dot(_split16(e), tri_x, NN)
            ps = ps[:t] + ps[t:]
            dz = e * (1.0 - beta) - beta * (ps + pre)
            if mask is not None:
                dz = jnp.where(mask, dz, 0.0)
            if live is not None:
                dz = jnp.where(live, dz, 0.0)
            dz16 = (dz * scale).astype(BF16)
            dk_ref[pl.ds(off, t), :] += _dot(dz16, q, TN)
            return pre + ps[:, t - 1:t] + e[:, t - 1:t], _dot(dz16, k_ref[pl.ds(off, t), :], NN)

        blocks = []
        for u in range(sub):
            iq = pl.program_id(1) * sub + u
            q, do = q_ref[pl.ds(u * t, t), :], do_ref[pl.ds(u * t, t), :]
            left = jnp.maximum(iq - 1, 0)
            run_d = sweep1(q, do, iq, pe_ref.at[u, 0], pb_ref.at[u, 0], jnp.zeros((t, 1), F32), causal)
            run_l = sweep1(q, do, left, pe_ref.at[u, 1], pb_ref.at[u, 1], run_d, None, iq > 0)
            blocks.append((iq, q, do, left, jnp.where(iq > 0, run_l, run_d)))
        carried = []
        for iq, q, do, left, run in blocks:
            def loop1(carry, q=q, do=do):
                run = sweep1(q, do, carry[0], e_ref.at[carry[0]], b_ref.at[carry[0]], carry[2], None)
                return carry[0] - 1, _sb_alive(run), run

            first = lax.while_loop(_sb_more, loop1, (iq - 2, _sb_alive(run), run))[0] + 1

            def loop2(kb, carry, q=q):
                pre, part = sweep2(q, kb, e_ref.at[kb], b_ref.at[kb], carry[0], None)
                return pre, carry[1] + part

            carried.append(lax.fori_loop(first, iq - 1, loop2, (jnp.zeros((t, 1), F32), jnp.zeros((t, dh), F32))))
        for u, ((iq, q, do, left, _), (pre, dq)) in enumerate(zip(blocks, carried)):
            pre, dq_l = sweep2(q, left, pe_ref.at[u, 1], pb_ref.at[u, 1], pre, None, iq > 0)
            _, dq_d = sweep2(q, iq, pe_ref.at[u, 0], pb_ref.at[u, 0], pre, causal)
            dq_ref[pl.ds(u * t, t), :] = (dq + dq_l + dq_d).astype(BF16)

        @pl.when(pl.program_id(1) == nq // sub - 1)
        def _():
            dk_out[...] = dk_ref[...].astype(BF16)
            dv_out[...] = dv_ref[...].astype(BF16)

    blk = lambda off: pl.BlockSpec((sub * t, dh), functools.partial(lambda h, i, off: (i, h + off), off=off))
    col = lambda off: pl.BlockSpec((s, dh), functools.partial(lambda h, i, off: (0, h + off), off=off))
    wide = SB_HEADS * dh
    return pl.pallas_call(
        body, name="sb_bwd", grid=(SB_HEADS, nq // sub),
        in_specs=[blk(0), col(SB_HEADS), col(2 * SB_HEADS), blk(0)],
        out_specs=[blk(0), col(0), col(0)],
        out_shape=[jax.ShapeDtypeStruct((s, wide), BF16)] * 3,
        scratch_shapes=[pltpu.VMEM((nq, t, t), F32), pltpu.VMEM((nq, t, t), F32),
                        pltpu.VMEM((sub, 2, t, t), F32), pltpu.VMEM((sub, 2, t, t), F32),
                        pltpu.VMEM((s, dh), F32), pltpu.VMEM((s, dh), F32)],
        compiler_params=_cparams(("parallel", "arbitrary")),
    )(qkv, qkv, qkv, dob)


def _alibi_slope(h):
    return float(2.0 ** (-8.0 * (h + 1) / C_HEADS))


GROUP_ROWS = C_GROUP * WINDOW


def _swa_window(n):
    qb = WINDOW
    i = lax.broadcasted_iota(jnp.int32, (qb, 2 * qb), 0)
    j = lax.broadcasted_iota(jnp.int32, (qb, 2 * qb), 1)
    d = i - j + qb
    valid = (d >= 0) & (d < WINDOW) & ((j >= qb) | (n > 0))
    stacked = lambda x: jnp.concatenate([x] * C_GROUP, axis=0)
    return stacked(d.astype(F32)), stacked(jnp.where(valid, 0.0, -1e30))


def _swa_group_cols(c, sink_ref):
    head = lax.shift_right_logical(lax.broadcasted_iota(jnp.int32, (GROUP_ROWS, 1), 0), WINDOW.bit_length() - 1)
    slope = jnp.zeros((GROUP_ROWS, 1), F32)
    sink = jnp.zeros((GROUP_ROWS, 1), F32)
    for hh in range(C_GROUP):
        slope = jnp.where(head == hh, _alibi_slope(c * C_GROUP + hh), slope)
        sink = jnp.where(head == hh, sink_ref[c * C_GROUP + hh], sink)
    return slope, sink


def _swa_probs(qg, kw, sink, slope, dist, bias, scale):
    sc = _dot(qg, kw, NT) * scale - slope * dist + bias
    m = jnp.maximum(jnp.max(sc, axis=-1, keepdims=True), sink)
    p = jnp.exp(sc - m)
    ps = jnp.exp(sink - m)
    inv = 1.0 / (jnp.sum(p, axis=-1, keepdims=True) + ps)
    return p * inv, ps * inv


def _stack_heads(x, c):
    hd = C_HEAD_DIM
    return jnp.concatenate([x[:, (c * C_GROUP + hh) * hd:(c * C_GROUP + hh + 1) * hd] for hh in range(C_GROUP)], axis=0)


SWA_SUB = 2


def _swa_kv_specs(sub, kvw):
    return [pl.BlockSpec((WINDOW, 2 * kvw), functools.partial(lambda n, w: (n * sub + w, 0), w=w)) for w in range(sub + 1)]


def _swa_fwd(q16, gate, kvp, sinks):
    s = q16.shape[0]
    qb = WINDOW
    hd = C_HEAD_DIM
    scale = 1.0 / math.sqrt(hd)
    kvw = C_KV_HEADS * hd
    sub = SWA_SUB if (s // qb) % SWA_SUB == 0 else 1

    def body(sink_ref, q_ref, g_ref, *refs):
        kv_refs, (o_ref, y_ref) = refs[:sub + 1], refs[sub + 1:]
        for u in range(sub):
            rows = slice(u * qb, (u + 1) * qb)
            q = q_ref[rows, :]
            kv = jnp.concatenate([kv_refs[u][...], kv_refs[u + 1][...]], axis=0)
            dist, bias = _swa_window(pl.program_id(0) * sub + u)
            for c in range(C_KV_HEADS):
                slope, sink = _swa_group_cols(c, sink_ref)
                kw = kv[:, c * hd:(c + 1) * hd]
                vw = kv[:, kvw + c * hd:kvw + (c + 1) * hd]
                p, _ = _swa_probs(_stack_heads(q, c), kw, sink, slope, dist, bias, scale)
                og = _dot(p.astype(BF16), vw, NN)
                for hh in range(C_GROUP):
                    h = c * C_GROUP + hh
                    o_ref[rows, h * hd:(h + 1) * hd] = og[hh * qb:(hh + 1) * qb, :]
        g = g_ref[...]
        y_ref[...] = (o_ref[...] * (g * _sigmoid(g))).astype(BF16)

    wide = C_HEADS * hd
    big = pl.BlockSpec((sub * qb, wide), lambda n: (n, 0))
    return pl.pallas_call(
        body, name="swa_fwd", grid=(s // (sub * qb),),
        in_specs=[pl.BlockSpec(memory_space=pltpu.SMEM), big, big] + _swa_kv_specs(sub, kvw),
        out_specs=[big, big],
        out_shape=[jax.ShapeDtypeStruct((s, wide), F32), jax.ShapeDtypeStruct((s, wide), BF16)],
        compiler_params=_cparams(("arbitrary",)),
    )(sinks, q16, gate, *([kvp] * (sub + 1)))


def _swa_bwd(q16, do16, kvp, sinks):
    s = q16.shape[0]
    qb = WINDOW
    hd = C_HEAD_DIM
    scale = 1.0 / math.sqrt(hd)
    kvw = C_KV_HEADS * hd
    sub = 1
    nstep = s // (sub * qb)

    def body(sink_ref, q_ref, do_ref, *refs):
        kv_refs, (dq_ref, dkv_ref, ds_ref, sacc_ref) = refs[:sub + 1], refs[sub + 1:]

        @pl.when(pl.program_id(0) == 0)
        def _():
            dkv_ref[...] = jnp.zeros_like(dkv_ref)
            sacc_ref[...] = jnp.zeros_like(sacc_ref)

        for u in range(sub):
            rows = slice(u * qb, (u + 1) * qb)
            n = pl.program_id(0) * sub + u
            q = q_ref[rows, :]
            do = do_ref[rows, :]
            kv = jnp.concatenate([kv_refs[u][...], kv_refs[u + 1][...]], axis=0)
            off = pl.multiple_of(n * qb, qb)
            dist, bias = _swa_window(n)
            for c in range(C_KV_HEADS):
                slope, sink = _swa_group_cols(c, sink_ref)
                kw = kv[:, c * hd:(c + 1) * hd]
                vw = kv[:, kvw + c * hd:kvw + (c + 1) * hd]
                qg = _stack_heads(q, c)
                dog = _stack_heads(do, c)
                p, ps = _swa_probs(qg, kw, sink, slope, dist, bias, scale)
                dp = _dot(dog, vw, NT)
                dd = jnp.sum(p * dp, axis=-1, keepdims=True)
                ds16 = (p * (dp - dd) * scale).astype(BF16)
                sacc_ref[c] += ps * dd
                dqg = _dot(ds16, kw, NN).astype(BF16)
                for hh in range(C_GROUP):
                    h = c * C_GROUP + hh
                    dq_ref[rows, h * hd:(h + 1) * hd] = dqg[hh * qb:(hh + 1) * qb, :]
                dkv_ref[pl.ds(off, 2 * qb), c * hd:(c + 1) * hd] += _dot(ds16, qg, TN)
                dkv_ref[pl.ds(off, 2 * qb), kvw + c * hd:kvw + (c + 1) * hd] += _dot(p.astype(BF16), dog, TN)

        @pl.when(pl.program_id(0) == nstep - 1)
        def _():
            lane = lax.broadcasted_iota(jnp.int32, (1, 128), 1)
            row = jnp.zeros((1, 128), F32)
            for c in range(C_KV_HEADS):
                for hh in range(C_GROUP):
                    tot = jnp.sum(sacc_ref[c, pl.ds(hh * qb, qb), :], axis=0, keepdims=True)
                    row = jnp.where(lane == c * C_GROUP + hh, -tot, row)
            ds_ref[...] = row

    wide = C_HEADS * hd
    big = pl.BlockSpec((sub * qb, wide), lambda n: (n, 0))
    return pl.pallas_call(
        body, name="swa_bwd", grid=(nstep,),
        in_specs=[pl.BlockSpec(memory_space=pltpu.SMEM), big, big] + _swa_kv_specs(sub, kvw),
        out_specs=[big,
                   pl.BlockSpec((s + qb, 2 * kvw), lambda n: (0, 0)), pl.BlockSpec((1, 128), lambda n: (0, 0))],
        out_shape=[jax.ShapeDtypeStruct((s, wide), BF16), jax.ShapeDtypeStruct((s + qb, 2 * kvw), F32),
                   jax.ShapeDtypeStruct((1, 128), F32)],
        scratch_shapes=[pltpu.VMEM((C_KV_HEADS, GROUP_ROWS, 1), F32)],
        compiler_params=_cparams(("arbitrary",)),
    )(sinks, q16, do16, *([kvp] * (sub + 1)))


def _adamw(parts, w, m, v, *, name, tr=496):
    npart, r, c = parts.shape
    tr = min(tr, r)
    assert r % tr == 0 and tr % 16 == 0
    c1 = 1.0 / (1.0 - ADAM_B1 ** ADAM_STEP)
    c2 = 1.0 / (1.0 - ADAM_B2 ** ADAM_STEP)

    def body(p_ref, w_ref, m_ref, v_ref, g_ref, d_ref, nm_ref, nv_ref):
        g = p_ref[0].astype(F32)
        for j in range(1, npart):
            g = g + p_ref[j].astype(F32)
        nm = ADAM_B1 * m_ref[...] + (1.0 - ADAM_B1) * g
        nv = ADAM_B2 * v_ref[...] + (1.0 - ADAM_B2) * (g * g)
        g_ref[...] = g
        nm_ref[...] = nm
        nv_ref[...] = nv
        d_ref[...] = -ADAM_LR * ((nm * c1) / (jnp.sqrt(nv * c2) + ADAM_EPS) + ADAM_WD * w_ref[...])

    spec = pl.BlockSpec((tr, c), lambda i: (i, 0))
    return pl.pallas_call(
        body, name=name, grid=(r // tr,),
        in_specs=[pl.BlockSpec((npart, tr, c), lambda i: (0, i, 0)), spec, spec, spec],
        out_specs=[spec] * 4, out_shape=[jax.ShapeDtypeStruct((r, c), F32)] * 4,
        compiler_params=_cparams(("parallel",)),
    )(parts, w, m, v)


GATHER = "gather"
A2A = "a2a"
GATHER_COLS = "gather_cols"
A2A_COLS = "a2a_cols"


def _gather_two_level(bufs, modes, x, *, tr=1024, name):
    nb = len(bufs)
    rows, cols = x.shape
    tr = min(tr, rows)
    steps = rows // tr

    def body(*refs):
        x_ref, src, x16_ref, dst = refs[0], refs[1:1 + nb], refs[1 + nb], refs[2 + nb:2 + 2 * nb]
        send_sems, recv_sems, local_sems = refs[2 + 2 * nb:]
        cx, cy, cc = lax.axis_index("x"), lax.axis_index("y"), lax.axis_index("c")
        me = 4 * cx + 2 * cy + cc
        here, sibling = (cx, cy, cc), (cx, cy, 1 - cc)

        def copy(b, k, origin, to, source):
            return pltpu.make_async_remote_copy(
                src_ref=source, dst_ref=_slot(dst[b], modes[b], origin), send_sem=send_sems.at[b, k - 1],
                recv_sem=recv_sems.at[b, k - 1], device_id=to, device_id_type=pl.DeviceIdType.MESH)

        def first_copies():
            local = [pltpu.make_async_copy(src[b], _slot(dst[b], modes[b], me), local_sems.at[b]) for b in range(nb)]
            sends = [copy(b, k, me, (cx ^ (k >> 2), cy ^ ((k >> 1) & 1), cc ^ (k & 1)), src[b])
                     for k in (1, 2, 4, 6) for b in range(nb)]
            return local, sends

        @pl.when(pl.program_id(0) == 0)
        def _():
            local, sends = first_copies()
            for cp in local + sends:
                cp.start()

        x16_ref[...] = x_ref[...].astype(BF16)

        @pl.when(pl.program_id(0) == steps - 1)
        def _():
            local, sends = first_copies()
            for j in (2, 4, 6):
                for b in range(nb):
                    copy(b, j, me ^ j, here, src[b]).wait_recv()
                    passed = copy(b, j ^ 1, me ^ j, sibling, _slot(dst[b], modes[b], me ^ j))
                    passed.start()
                    sends.append(passed)
            for k in (1, 3, 5, 7):
                for b in range(nb):
                    copy(b, k, me ^ k, here, src[b]).wait_recv()
            for cp in sends:
                cp.wait_send()
            for cp in local:
                cp.wait()

    hbm = pl.BlockSpec(memory_space=pl.ANY)
    tile = pl.BlockSpec((tr, cols), lambda i: (i, 0))
    return pl.pallas_call(
        body, name=name, grid=(steps,), in_specs=[tile] + [hbm] * nb, out_specs=[tile] + [hbm] * nb,
        out_shape=[jax.ShapeDtypeStruct(x.shape, BF16)] + _exchange_out_shapes(bufs, modes),
        scratch_shapes=_exchange_sems(nb),
        compiler_params=pltpu.CompilerParams(dimension_semantics=("arbitrary",), vmem_limit_bytes=VMEM_LIMIT,
                                             has_side_effects=True),
    )(x, *bufs)


def _exchange_out_shapes(bufs, gather):
    shapes = {GATHER: lambda s: (N_DEV,) + s, A2A: lambda s: s, GATHER_COLS: lambda s: (s[0], N_DEV * s[1]),
              A2A_COLS: lambda s: (N_DEV, s[0], s[1] // N_DEV)}
    return [jax.ShapeDtypeStruct(shapes[g](tuple(b.shape)), b.dtype) for b, g in zip(bufs, gather)]


def _sent(src, mode, peer):
    if mode == A2A:
        return src.at[peer]
    if mode == A2A_COLS:
        w = src.shape[1] // N_DEV
        return src.at[:, pl.ds(pl.multiple_of(peer * w, 128), w)]
    return src


def _slot(dst, mode, dev):
    if mode == GATHER_COLS:
        w = dst.shape[1] // N_DEV
        return dst.at[:, pl.ds(pl.multiple_of(dev * w, 128), w)]
    return dst.at[dev]


def _exchange_sems(nb):
    return [pltpu.SemaphoreType.DMA((nb, N_DEV - 1)), pltpu.SemaphoreType.DMA((nb, N_DEV - 1)),
            pltpu.SemaphoreType.DMA((nb,))]


def _exchange_copies(src, dst, gather, sems):
    send_sems, recv_sems, local_sems = sems
    x, y, c = lax.axis_index("x"), lax.axis_index("y"), lax.axis_index("c")
    me = 4 * x + 2 * y + c
    local, sends, recvs = [], [], []
    for b in range(len(src)):
        mine = _sent(src[b], gather[b], me)
        local.append(pltpu.make_async_copy(mine, _slot(dst[b], gather[b], me), local_sems.at[b]))
        for k in range(1, N_DEV):
            px, py, pc = x ^ (k >> 2), y ^ ((k >> 1) & 1), c ^ (k & 1)
            peer = 4 * px + 2 * py + pc
            pair = dict(send_sem=send_sems.at[b, k - 1], recv_sem=recv_sems.at[b, k - 1],
                        device_id_type=pl.DeviceIdType.MESH)
            sends.append(pltpu.make_async_remote_copy(
                src_ref=_sent(src[b], gather[b], peer), dst_ref=_slot(dst[b], gather[b], me), device_id=(px, py, pc),
                **pair))
            recvs.append(pltpu.make_async_remote_copy(
                src_ref=mine, dst_ref=_slot(dst[b], gather[b], peer), device_id=(x, y, c), **pair))
    return local, sends, recvs


def _exchange_start(src, dst, gather, sems):
    local, sends, _ = _exchange_copies(src, dst, gather, sems)
    for cp in local + sends:
        cp.start()


def _exchange_wait(src, dst, gather, sems):
    local, sends, recvs = _exchange_copies(src, dst, gather, sems)
    for cp in sends:
        cp.wait_send()
    for cp in recvs:
        cp.wait_recv()
    for cp in local:
        cp.wait()


R_SMALL = 16
REPL_ROWS = 272
REPL_LEN = REPL_ROWS * D_MODEL


def _small_block(parts):
    flat = jnp.concatenate(parts, axis=-1)
    lead = flat.ndim - 1
    return jnp.pad(flat[..., None, :], [(0, 0)] * lead + [(0, R_SMALL - 1), (0, D_MODEL - flat.shape[-1])])


def _small_vectors(conv_w, ln_g, ln_b):
    return _small_block([conv_w.reshape(512), ln_g.reshape(128), ln_b.reshape(128)])


REPL_SHAPES = ((1, 8, 128, 128), (1, 8, 128, 128), (1, 1024), (1, 1024), (1, 1024), (1, 1024), (1, 1024), (1, 1024),
               (1, 16))


def _pack_repl(parts):
    flat = jnp.concatenate([p.reshape(-1) for p in parts])
    return jnp.concatenate([flat, jnp.zeros((REPL_LEN - flat.shape[0],), F32)]).reshape(REPL_ROWS, D_MODEL)


def _unpack_repl(p):
    flat = p.reshape(-1)
    out, o = [], 0
    for shp in REPL_SHAPES:
        n = math.prod(shp)
        out.append(flat[o:o + n].reshape(shp))
        o += n
    return out


def kernel(x, e_w_in, e_conv_w, e_conv_b, e_w_gate_a, e_b_gate_a, e_w_gate_x, e_b_gate_x, e_lru_lambda, e_w_out, e_ln_g, e_ln_b, o_w_in, o_sinks, o_w_out, o_ln_g, o_ln_b, loss_target, m_e_w_in, m_e_conv_w, m_e_conv_b, m_e_w_gate_a, m_e_b_gate_a, m_e_w_gate_x, m_e_b_gate_x, m_e_lru_lambda, m_e_w_out, m_e_ln_g, m_e_ln_b, m_o_w_in, m_o_sinks, m_o_w_out, m_o_ln_g, m_o_ln_b, v_e_w_in, v_e_conv_w, v_e_conv_b, v_e_w_gate_a, v_e_b_gate_a, v_e_w_gate_x, v_e_b_gate_x, v_e_lru_lambda, v_e_w_out, v_e_ln_g, v_e_ln_b, v_o_w_in, v_o_sinks, v_o_w_out, v_o_ln_g, v_o_ln_b):
    d = D_MODEL
    x0 = x[0]
    target = loss_target[0]
    s = x0.shape[0]

    spack = _small_block([e_conv_w.reshape(512), o_ln_g.reshape(128), o_ln_b.reshape(128)])
    x0_16, wi_e, sall = _gather_two_level([e_w_in[0].astype(BF16), spack], [GATHER_COLS, GATHER], x0,
                                          name="gather_w_in")
    conv_w = sall[:, 0, 0:512].reshape(N_DEV, 4, 128).transpose(1, 0, 2).reshape(4, d)
    ln_g_o = sall[:, 0, 512:640].reshape(1, d)
    ln_b_o = sall[:, 0, 640:768].reshape(1, d)
    pvec = jnp.concatenate([conv_w, e_conv_b, e_b_gate_a, e_b_gate_x, e_lru_lambda], axis=0)
    wa16 = e_w_gate_a[0].astype(BF16)
    wx16 = e_w_gate_x[0].astype(BF16)
    sinks = o_sinks[0]

    h0a, wall_out_e = _matmul(x0_16, wi_e, mode="nn", n_out=2 * d, b_off=0, name="l0_in_a",
                              comm=([e_w_out[0].astype(BF16)], [GATHER]))
    qkv, wall_in_o = _matmul(x0_16, wi_e, mode="nn", n_out=3 * d, b_off=2 * d, out_dtype=BF16, name="l0_in_qkv",
                             comm=([o_w_in[0].astype(BF16)], [GATHER]))
    bg, wall_out_o = _matmul(x0_16, wi_e, mode="nn", n_out=d, b_off=5 * d, name="l0_in_bg",
                             comm=([o_w_out[0].astype(BF16)], [GATHER]))
    wo_e = wall_out_e.reshape(2 * d, d)
    wi_o = wall_in_o.transpose(1, 0, 2).reshape(d, 2304)
    wi_o = jnp.concatenate([wi_o[:, :1024], wi_o[:, 1280:], wi_o[:, 1024:1280]], axis=1)
    wo_o = wall_out_o.reshape(d, d)
    ya, hst = _rglru_fwd(h0a, pvec, wa16, wx16)
    ob, yb = _sb_fwd(qkv, bg)
    z0, x1, x1_16 = _matmul([ya, yb], wo_e, mode="nn", tm=512, name="l0_out",
                            epi=(_epi_ln_fwd, [(x0, 0)], [e_ln_g, e_ln_b], [F32, F32, BF16], 0))

    q1 = _matmul(x1_16, wi_o, mode="nn", n_out=d, b_off=0, out_dtype=BF16, name="l1_in_q")
    g1 = _matmul(x1_16, wi_o, mode="nn", n_out=d, b_off=d, name="l1_in_g")
    kv1 = _matmul(x1_16, wi_o, mode="nn", n_out=256, b_off=2 * d, tn=256, out_dtype=BF16, name="l1_in_kv")
    kvp = jnp.concatenate([jnp.zeros((WINDOW, 256), BF16), kv1], axis=0)
    o1, y1 = _swa_fwd(q1, g1, kvp, sinks)
    dz1, dz1_16, loss_cols, dg_o, db_o = _matmul(
        y1, wo_o, mode="nn", tm=512, name="l1_out",
        epi=(_epi_ln_loss_bwd, [(x1, 0), (target, 0)], [ln_g_o, ln_b_o], [F32, BF16], 3))
    loss_hi = jnp.sum(loss_cols).astype(BF16).astype(F32)
    loss_terms = jnp.stack([loss_hi, jnp.sum(loss_cols) - loss_hi]).reshape(1, 2)

    dwo_o = _matmul(y1, dz1_16, mode="tn", out_dtype=BF16, name="l1_dwout")
    do1, dg1, parts_out_o = _matmul(dz1_16, wo_o, mode="nt", tm=512, name="l1_dy",
                                    comm=([dwo_o.reshape(N_DEV, 128, d)], [A2A]),
                                    epi=(_epi_gate_bwd, [(o1, 0), (g1, 0)], [], [BF16, BF16], 0))
    dq1, dkvp, dsink = _swa_bwd(q1, do1, kvp, sinks)
    dkv1 = dkvp[WINDOW:].astype(BF16)
    dh1 = [dq1, dg1, dkv1]
    dw_qg = _matmul(x1_16, [dq1, dg1], mode="tn", out_dtype=BF16, name="l1_dwin_qg")
    dw_kv = _matmul(x1_16, dkv1, mode="tn", out_dtype=BF16, name="l1_dwin_kv")
    dwi_o = jnp.concatenate([dw_qg[:, :1024], dw_kv, dw_qg[:, 1024:]], axis=1)
    dz0, dz0_16, dg_e, db_e, parts_in_o = _matmul(
        dh1, wi_o, mode="nt", tm=512, tk=2304, name="l1_dx",
        comm=([dwi_o.reshape(d, N_DEV, 288).transpose(1, 0, 2)], [A2A]),
        epi=(_epi_ln_bwd, [(dz1, 0), (z0, 0)], [e_ln_g], [F32, BF16], 2))

    dwo_e = _matmul([ya, yb], dz0_16, mode="tn", out_dtype=BF16, name="l0_dwout")
    dhst, dag, parts_out_e = _matmul(
        dz0_16, wo_e, mode="nt", tm=512, n_out=d, b_off=0, name="l0_dy_a",
        comm=([dwo_e.reshape(N_DEV, 256, d)], [A2A]),
        epi=(_epi_gate_bwd, [(hst, 0), (h0a, d)], [], [F32, BF16], 0))
    dob, dbg = _matmul(dz0_16, wo_e, mode="nt", tm=512, n_out=d, b_off=d, name="l0_dy_b",
                       epi=(_epi_gate_bwd, [(ob, 0), (bg, 0)], [], [BF16, BF16], 0))
    dq0, dk0, dv0 = _sb_bwd(qkv, dob)
    dax, dwa, dwx, dpv = _rglru_bwd(h0a, hst, dhst, pvec, wa16, wx16)
    dh0 = [dax, dag, dq0, dk0, dv0, dbg]
    repl = _pack_repl([dwa, dwx, dpv[4:5], dpv[5:6], dpv[6:7], dpv[7:8], dg_e, db_e, dsink[:, :C_HEADS], loss_terms])
    dwi_e, parts_repl = _matmul(x0_16, dh0, mode="tn", out_dtype=BF16, name="l0_dwin",
                                comm=([repl.astype(BF16)], [GATHER]))
    small = _small_block([dpv[0:4].reshape(4, N_DEV, 128).transpose(1, 0, 2).reshape(N_DEV, 512),
                          dg_o.reshape(N_DEV, 128), db_o.reshape(N_DEV, 128)]).astype(BF16)
    grad_x, parts_in_e, parts_small = _matmul(dh0, wi_e, mode="nt", tm=512, res=dz0, alpha=ALPHA, name="l0_dx",
                                              comm=([dwi_e, small], [A2A_COLS, A2A]))

    res_in_e = _adamw(parts_in_e, e_w_in[0], m_e_w_in[0], v_e_w_in[0], tr=512, name="adamw_w_in_e")
    res_out_e = _adamw(parts_out_e, e_w_out[0], m_e_w_out[0], v_e_w_out[0], name="adamw_w_out_e")
    res_in_o = _adamw(parts_in_o, o_w_in[0], m_o_w_in[0], v_o_w_in[0], tr=512, name="adamw_w_in_o")
    res_out_o = _adamw(parts_out_o, o_w_out[0], m_o_w_out[0], v_o_w_out[0], name="adamw_w_out_o")
    res_small = _adamw(parts_small, _small_vectors(e_conv_w, o_ln_g, o_ln_b),
                       _small_vectors(m_e_conv_w, m_o_ln_g, m_o_ln_b),
                       _small_vectors(v_e_conv_w, v_o_ln_g, v_o_ln_b), name="adamw_vectors")

    w_r = _pack_repl([e_w_gate_a, e_w_gate_x, e_conv_b, e_b_gate_a, e_b_gate_x, e_lru_lambda, e_ln_g, e_ln_b, o_sinks])
    m_r = _pack_repl([m_e_w_gate_a, m_e_w_gate_x, m_e_conv_b, m_e_b_gate_a, m_e_b_gate_x, m_e_lru_lambda, m_e_ln_g,
                      m_e_ln_b, m_o_sinks])
    v_r = _pack_repl([v_e_w_gate_a, v_e_w_gate_x, v_e_conv_b, v_e_b_gate_a, v_e_b_gate_x, v_e_lru_lambda, v_e_ln_g,
                      v_e_ln_b, v_o_sinks])
    g_r, d_r, nm_r, nv_r = _adamw(parts_repl, w_r, m_r, v_r, name="adamw_replicated")
    loss_at = sum(math.prod(shp) for shp in REPL_SHAPES)
    loss = g_r[loss_at // d, loss_at % d] + g_r[(loss_at + 1) // d, (loss_at + 1) % d]

    def assemble(i, rp):
        vec = res_small[i][0]
        cw, lg_o, lb_o = vec[0:512].reshape(1, 4, 128), vec[512:640].reshape(1, 128), vec[640:768].reshape(1, 128)
        w_a, w_x, cb, b_a, b_x, lam, lg_e, lb_e, snk = _unpack_repl(rp)
        return [res_in_e[i][None], cw, cb, w_a, b_a, w_x, b_x, lam, res_out_e[i][None], lg_e, lb_e,
                res_in_o[i][None], snk, res_out_o[i][None], lg_o, lb_o]

    return (loss, grad_x[None], *assemble(0, g_r), *assemble(1, d_r), *assemble(2, nm_r), *assemble(3, nv_r))
```

```python
import functools
import math

import jax
import jax.numpy as jnp
from jax import lax
from jax.experimental import pallas as pl
from jax.experimental.pallas import tpu as pltpu

F32 = jnp.float32
BF16 = jnp.bfloat16

N_DEV = 8
D_MODEL = 1024
LRU_BLOCKS = 8
LRU_BLOCK = 128
LRU_C = 8.0
SB_HEADS = 8
SB_HEAD_DIM = 128
C_HEADS = 16
C_KV_HEADS = 2
C_GROUP = 8
C_HEAD_DIM = 64
WINDOW = 128
DEPTH = 2
ALPHA = float((2 * DEPTH) ** 0.25)
LN_EPS = 1e-5
ADAM_LR = 0.001
ADAM_B1 = 0.9
ADAM_B2 = 0.999
ADAM_EPS = 1e-08
ADAM_WD = 0.01
ADAM_STEP = 10

VMEM_LIMIT = 56 * 1024 * 1024

NN = ((1,), (0,))
NT = ((1,), (1,))
TN = ((0,), (0,))


def _dot(a, b, dims):
    return lax.dot_general(a, b, (dims, ((), ())), preferred_element_type=F32)


def _sigmoid(x):
    return 1.0 / (1.0 + jnp.exp(-x))


def _cparams(sem, vmem=VMEM_LIMIT):
    return pltpu.CompilerParams(dimension_semantics=sem, vmem_limit_bytes=vmem)


def _matmul(a, b, *, mode, n_out=None, b_off=0, out_dtype=F32, res=None, alpha=1.0,
            tm=1024, tn=1024, tk=1024, comm=None, epi=None, name):
    a_list = list(a) if isinstance(a, (list, tuple)) else [a]
    b_list = list(b) if isinstance(b, (list, tuple)) else [b]
    if mode == "tn":
        k = a_list[0].shape[0]
        m = sum(p.shape[1] for p in a_list)
        n = n_out if n_out is not None else sum(p.shape[1] for p in b_list)
        assert all(p.shape[1] == tm for p in a_list) or len(a_list) == 1
        assert all(p.shape[1] == tn for p in b_list) or len(b_list) == 1
    else:
        assert len(b_list) == 1
        m = a_list[0].shape[0]
        k = sum(p.shape[1] for p in a_list)
        n = n_out if n_out is not None else (b.shape[0] if mode == "nt" else b.shape[1])
        if len(a_list) > 1:
            tk = k
    tm, tn, tk = min(tm, m), min(tn, n), min(tk, k)
    assert m % tm == 0 and n % tn == 0 and k % tk == 0 and b_off % tn == 0
    grid = (m // tm, n // tn, k // tk)
    nk = grid[2]
    jo = b_off // tn
    dims = {"nn": NN, "nt": NT, "tn": TN}[mode]
    nc = len(comm[0]) if comm else 0
    if epi:
        assert res is None
        epi_fn, epi_rows, epi_vecs, epi_dtypes, n_sums = epi
        assert n_sums == 0 or grid[1] == 1
    else:
        epi_rows, epi_vecs, epi_dtypes, n_sums = [], [], [out_dtype], 0
    n_ab = len(a_list) + len(b_list)
    n_in = n_ab + (res is not None) + len(epi_rows) + len(epi_vecs)
    n_res = len(epi_dtypes) + n_sums

    def body(*refs):
        a_refs, b_refs = refs[:len(a_list)], refs[len(a_list):n_ab]
        r_ref = refs[n_ab] if res is not None else None
        row_refs = refs[n_in - len(epi_rows) - len(epi_vecs):n_in - len(epi_vecs)]
        vec_refs = refs[n_in - len(epi_vecs):n_in]
        o_refs = refs[n_in + nc:n_in + nc + len(epi_dtypes)]
        sum_refs = refs[n_in + nc + len(epi_dtypes):n_in + nc + n_res]
        if comm:
            step = [pl.program_id(ax) for ax in range(3)]
            c_src = refs[n_in:n_in + nc]
            c_dst = refs[n_in + nc + n_res:n_in + 2 * nc + n_res]
            sems = refs[len(refs) - 3:]

            @pl.when((step[0] == 0) & (step[1] == 0) & (step[2] == 0))
            def _():
                _exchange_start(c_src, c_dst, comm[1], sems)

        def finish(out):
            if not epi:
                if r_ref is not None:
                    out = out + alpha * r_ref[...]
                o_refs[0][...] = out.astype(out_dtype)
                return
            outs, sums = epi_fn(out, [r[...] for r in row_refs], [v[...] for v in vec_refs])
            for o_ref, o in zip(o_refs, outs):
                o_ref[...] = o.astype(o_ref.dtype)
            if sum_refs:
                first_rows = pl.program_id(0) == 0

                @pl.when(first_rows)
                def _():
                    for s_ref, part_sum in zip(sum_refs, sums):
                        s_ref[...] = part_sum

                @pl.when(jnp.logical_not(first_rows))
                def _():
                    for s_ref, part_sum in zip(sum_refs, sums):
                        s_ref[...] += part_sum

        def accumulate(part):
            if nk == 1:
                finish(part)
                return
            acc_ref = refs[n_in + 2 * nc + n_res]
            kk = pl.program_id(2)

            @pl.when(kk == 0)
            def _():
                acc_ref[...] = part

            @pl.when(kk > 0)
            def _():
                acc_ref[...] += part

            @pl.when(kk == nk - 1)
            def _():
                finish(acc_ref[...])

        def product(a_ref, b_val):
            return _dot(a_ref[...].astype(BF16), b_val.astype(BF16), dims)

        if mode != "tn" and len(a_list) > 1:
            part, off = None, 0
            for a_ref in a_refs:
                w = a_ref.shape[1]
                b_val = b_refs[0][off:off + w, :] if mode == "nn" else b_refs[0][:, off:off + w]
                part = product(a_ref, b_val) if part is None else part + product(a_ref, b_val)
                off += w
            accumulate(part)
        elif len(a_list) > 1 or len(b_list) > 1:
            for pa, a_ref in enumerate(a_refs):
                for pb, b_ref in enumerate(b_refs):
                    picked = ([pl.program_id(0) == pa] if len(a_list) > 1 else []) + (
                        [pl.program_id(1) == pb] if len(b_list) > 1 else [])
                    pl.when(functools.reduce(jnp.logical_and, picked))(
                        functools.partial(lambda a_ref, b_ref: accumulate(product(a_ref, b_ref[...])), a_ref, b_ref))
        else:
            accumulate(product(a_refs[0], b_refs[0][...]))

        if comm:
            @pl.when((step[0] == grid[0] - 1) & (step[1] == grid[1] - 1) & (step[2] == grid[2] - 1))
            def _():
                _exchange_wait(c_src, c_dst, comm[1], sems)

    def held(axis, p):
        def index(i, j, kk):
            return (jnp.where((i, j)[axis] == p, kk, 0), 0)
        return index

    if mode == "tn":
        if len(a_list) > 1:
            a_specs = [pl.BlockSpec((tk, tm), held(0, p)) for p in range(len(a_list))]
        else:
            a_specs = [pl.BlockSpec((tk, tm), lambda i, j, kk: (kk, i))]
    elif len(a_list) > 1:
        a_specs = [pl.BlockSpec((tm, p.shape[1]), lambda i, j, kk: (i, 0)) for p in a_list]
    else:
        a_specs = [pl.BlockSpec((tm, tk), lambda i, j, kk: (i, kk))]
    if mode == "nt":
        b_specs = [pl.BlockSpec((tn, tk), lambda i, j, kk: (j + jo, kk))]
    elif len(b_list) > 1:
        b_specs = [pl.BlockSpec((tk, tn), held(1, p)) for p in range(len(b_list))]
    else:
        b_specs = [pl.BlockSpec((tk, tn), lambda i, j, kk: (kk, j + jo))]
    o_spec = pl.BlockSpec((tm, tn), lambda i, j, kk: (i, j))
    hbm = pl.BlockSpec(memory_space=pl.ANY)
    row_specs = [pl.BlockSpec((tm, tn), functools.partial(lambda i, j, kk, o: (i, j + o), o=off // tn))
                 for (_, off) in epi_rows]
    vec_specs = [pl.BlockSpec(v.shape, functools.partial(lambda i, j, kk, nd: (0,) * nd, nd=v.ndim)) for v in epi_vecs]
    in_specs = a_specs + b_specs + ([o_spec] if res is not None else []) + row_specs + vec_specs + [hbm] * nc
    args = (a_list + b_list + ([res] if res is not None else []) + [r for (r, _) in epi_rows] + list(epi_vecs)
            + (list(comm[0]) if comm else []))
    out_specs = [o_spec] * len(epi_dtypes) + [pl.BlockSpec((1, tn), lambda i, j, kk: (0, j))] * n_sums + [hbm] * nc
    out_shape = ([jax.ShapeDtypeStruct((m, n), dt) for dt in epi_dtypes] + [jax.ShapeDtypeStruct((1, n), F32)] * n_sums
                 + (_exchange_out_shapes(*comm) if comm else []))
    scratch = ([pltpu.VMEM((tm, tn), F32)] if nk > 1 else []) + (_exchange_sems(nc) if comm else [])
    if comm or n_sums or n_ab > 2:
        params = pltpu.CompilerParams(dimension_semantics=("arbitrary",) * 3, vmem_limit_bytes=VMEM_LIMIT,
                                      has_side_effects=bool(comm))
    else:
        params = _cparams(("parallel", "parallel", "arbitrary"))
    outs = pl.pallas_call(
        body, name=name, grid=grid, in_specs=in_specs, out_specs=out_specs, out_shape=out_shape,
        scratch_shapes=scratch, compiler_params=params,
    )(*args)
    return tuple(outs) if (comm or epi) else outs[0]


def _ln_stats(z):
    mu = jnp.mean(z, axis=-1, keepdims=True)
    zc = z - mu
    var = jnp.mean(zc * zc, axis=-1, keepdims=True)
    rstd = lax.rsqrt(var + LN_EPS)
    return zc * rstd, rstd


def _ln_bwd(dy, xhat, rstd, g):
    dxh = dy * g
    m1 = jnp.mean(dxh, axis=-1, keepdims=True)
    m2 = jnp.mean(dxh * xhat, axis=-1, keepdims=True)
    return rstd * (dxh - m1 - xhat * m2)


def _colsum(x):
    return jnp.sum(x, axis=0, keepdims=True)


def _epi_ln_fwd(acc, rows, vecs):
    z = ALPHA * rows[0] + acc
    xhat, _ = _ln_stats(z)
    x1 = xhat * vecs[0] + vecs[1]
    return [z, x1, x1], []


def _epi_ln_loss_bwd(acc, rows, vecs):
    inv_c = 1.0 / acc.shape[-1]
    z = ALPHA * rows[0] + acc
    xhat, rstd = _ln_stats(z)
    d = xhat * vecs[0] + vecs[1] - rows[1]
    dy = d * inv_c
    dz = _ln_bwd(dy, xhat, rstd, vecs[0])
    return [dz, dz], [_colsum(0.5 * d * d * inv_c), _colsum(dy * xhat), _colsum(dy)]


def _epi_ln_bwd(acc, rows, vecs):
    dy = acc + ALPHA * rows[0]
    xhat, rstd = _ln_stats(rows[1])
    dz = _ln_bwd(dy, xhat, rstd, vecs[0])
    return [dz, dz], [_colsum(dy * xhat), _colsum(dy)]


def _epi_gate_bwd(acc, rows, vecs):
    oo, gg = rows
    sg = _sigmoid(gg)
    return [acc * (gg * sg), acc * oo * (sg * (1.0 + gg * (1.0 - sg)))], []


LRU_T = 512
HALO = 8


def _log1p(y):
    u = 1.0 + y
    return jnp.where(u == 1.0, y, jnp.log(u) * (y / (u - 1.0)))


def _lru_gates(c, wa_ref, wx_ref, pv):
    c16 = c.astype(BF16)
    pre_r = jnp.concatenate(
        [_dot(c16[:, n * LRU_BLOCK:(n + 1) * LRU_BLOCK], wa_ref[n], NN) for n in range(LRU_BLOCKS)], axis=1)
    pre_i = jnp.concatenate(
        [_dot(c16[:, n * LRU_BLOCK:(n + 1) * LRU_BLOCK], wx_ref[n], NN) for n in range(LRU_BLOCKS)], axis=1)
    r = _sigmoid(pre_r + pv[5:6])
    ig = _sigmoid(pre_i + pv[6:7])
    lam = pv[7:8]
    ls = jnp.minimum(lam, 0.0) - _log1p(jnp.exp(-jnp.abs(lam)))
    la = LRU_C * r * ls
    a = jnp.exp(la)
    a2 = a * a
    m = jnp.sqrt(jnp.tanh(-la) * (a2 + 1.0))
    return c16, r, ig, ls, la, a, a2, m


def _conv(ext_ref, x, pv, t):
    return (pv[4:5] + pv[3:4] * x + pv[2:3] * ext_ref[pl.ds(HALO - 1, t), :]
            + pv[1:2] * ext_ref[pl.ds(HALO - 2, t), :] + pv[0:1] * ext_ref[pl.ds(HALO - 3, t), :])


def _rglru_fwd(h0a, pvec, wa16, wx16):
    s = h0a.shape[0]
    w = D_MODEL
    t = min(LRU_T, s)
    assert s % t == 0

    def body(ax_ref, ag_ref, pv_ref, wa_ref, wx_ref, ya_ref, h_ref, ext_ref, a_ref, u_ref, hc_ref):
        i = pl.program_id(0)

        @pl.when(i == 0)
        def _():
            ext_ref[pl.ds(0, HALO), :] = jnp.zeros((HALO, w), F32)
            hc_ref[...] = jnp.zeros((1, w), F32)

        pv = pv_ref[...]
        ax = ax_ref[...]
        ext_ref[pl.ds(HALO, t), :] = ax
        c = _conv(ext_ref, ax, pv, t)
        ext_ref[pl.ds(0, HALO), :] = ax[t - HALO:, :]
        _, _, ig, _, _, a, _, m = _lru_gates(c, wa_ref, wx_ref, pv)
        a_ref[...] = a
        u_ref[...] = m * (ig * c)

        def steps(k, h):
            base = pl.multiple_of(k * HALO, HALO)
            for r in range(HALO):
                h = a_ref[pl.ds(base + r, 1), :] * h + u_ref[pl.ds(base + r, 1), :]
                h_ref[pl.ds(base + r, 1), :] = h
            return h

        hc_ref[...] = lax.fori_loop(0, t // HALO, steps, hc_ref[...])
        ag = ag_ref[...]
        ya_ref[...] = (h_ref[...] * (ag * _sigmoid(ag))).astype(BF16)

    full = lambda shp: pl.BlockSpec(shp, lambda i: (0,) * len(shp))
    return pl.pallas_call(
        body, name="rglru_fwd", grid=(s // t,),
        in_specs=[pl.BlockSpec((t, w), lambda i: (i, 0)), pl.BlockSpec((t, w), lambda i: (i, 1)),
                  full((8, w)), full((LRU_BLOCKS, LRU_BLOCK, LRU_BLOCK)), full((LRU_BLOCKS, LRU_BLOCK, LRU_BLOCK))],
        out_specs=[pl.BlockSpec((t, w), lambda i: (i, 0)), pl.BlockSpec((t, w), lambda i: (i, 0))],
        out_shape=[jax.ShapeDtypeStruct((s, w), BF16), jax.ShapeDtypeStruct((s, w), F32)],
        scratch_shapes=[pltpu.VMEM((t + HALO, w), F32), pltpu.VMEM((t, w), F32), pltpu.VMEM((t, w), F32),
                        pltpu.VMEM((1, w), F32)],
        compiler_params=_cparams(("arbitrary",)),
    )(h0a, h0a, pvec, wa16, wx16)


def _rglru_bwd(h0a, h, dh, pvec, wa16, wx16):
    s = h0a.shape[0]
    w = D_MODEL
    t = min(LRU_T, s)
    nb = s // t
    hb = t // HALO

    def body(ax_ref, axh_ref, h_ref, hh_ref, dh_ref, pv_ref, wa_ref, wx_ref,
             dax_ref, dwa_ref, dwx_ref, dpv_ref, ext_ref, hext_ref, dcext_ref, a_ref, g_ref, gc_ref):
        i = pl.program_id(0)
        blk = nb - 1 - i

        @pl.when(i == 0)
        def _():
            dwa_ref[...] = jnp.zeros_like(dwa_ref)
            dwx_ref[...] = jnp.zeros_like(dwx_ref)
            dpv_ref[...] = jnp.zeros_like(dpv_ref)
            gc_ref[...] = jnp.zeros((1, w), F32)
            dcext_ref[pl.ds(t, HALO), :] = jnp.zeros((HALO, w), F32)

        pv = pv_ref[...]
        ax = ax_ref[...]
        keep = (blk > 0).astype(F32)
        ext_ref[pl.ds(0, HALO), :] = axh_ref[...] * keep
        ext_ref[pl.ds(HALO, t), :] = ax
        hext_ref[pl.ds(0, HALO), :] = hh_ref[...] * keep
        hext_ref[pl.ds(HALO, t), :] = h_ref[...]
        c = _conv(ext_ref, ax, pv, t)
        c16, r, ig, ls, _, a, a2, m = _lru_gates(c, wa_ref, wx_ref, pv)
        a_ref[...] = a

        def steps(k, carry):
            base = pl.multiple_of(t - HALO - k * HALO, HALO)
            for r in reversed(range(HALO)):
                g = dh_ref[pl.ds(base + r, 1), :] + carry
                g_ref[pl.ds(base + r, 1), :] = g
                carry = a_ref[pl.ds(base + r, 1), :] * g
            return carry

        gc_ref[...] = lax.fori_loop(0, t // HALO, steps, gc_ref[...])
        g = g_ref[...]
        hprev = hext_ref[pl.ds(HALO - 1, t), :]
        gm = g * m
        d_la = g * hprev * a - (g * ig * c) * a2 / m
        d_pr = d_la * (LRU_C * ls) * r * (1.0 - r)
        d_pi = gm * c * ig * (1.0 - ig)
        dc = gm * ig
        dpr16 = d_pr.astype(BF16)
        dpi16 = d_pi.astype(BF16)
        dc_parts = []
        for n in range(LRU_BLOCKS):
            sl = slice(n * LRU_BLOCK, (n + 1) * LRU_BLOCK)
            dwa_ref[n] += _dot(c16[:, sl], dpr16[:, sl], TN)
            dwx_ref[n] += _dot(c16[:, sl], dpi16[:, sl], TN)
            dc_parts.append(_dot(dpr16[:, sl], wa_ref[n], NT) + _dot(dpi16[:, sl], wx_ref[n], NT))
        dc = dc + jnp.concatenate(dc_parts, axis=1)
        dcext_ref[pl.ds(0, t), :] = dc
        dax = (pv[3:4] * dc + pv[2:3] * dcext_ref[pl.ds(1, t), :] + pv[1:2] * dcext_ref[pl.ds(2, t), :]
               + pv[0:1] * dcext_ref[pl.ds(3, t), :])
        dax_ref[...] = dax.astype(BF16)
        dcext_ref[pl.ds(t, HALO), :] = dc[:HALO, :]
        sums = [_colsum(dc * ext_ref[pl.ds(HALO - 3 + kk, t), :]) for kk in range(4)]
        sums += [_colsum(dc), _colsum(d_pr), _colsum(d_pi), _colsum(d_la * (LRU_C * r))]
        for kk, part in enumerate(sums):
            dpv_ref[pl.ds(kk, 1), :] += part

        @pl.when(i == nb - 1)
        def _():
            lam = pv[7:8]
            dls = dpv_ref[pl.ds(7, 1), :]
            dpv_ref[pl.ds(7, 1), :] = dls * _sigmoid(-lam)

    full = lambda shp: pl.BlockSpec(shp, lambda i: (0,) * len(shp))
    rev = lambda cb: pl.BlockSpec((t, w), functools.partial(lambda i, cb: (nb - 1 - i, cb), cb=cb))
    halo = lambda cb: pl.BlockSpec(
        (HALO, w), functools.partial(lambda i, cb: (jnp.maximum((nb - 1 - i) * hb - 1, 0), cb), cb=cb))
    gw = (LRU_BLOCKS, LRU_BLOCK, LRU_BLOCK)
    return pl.pallas_call(
        body, name="rglru_bwd", grid=(nb,),
        in_specs=[rev(0), halo(0), rev(0), halo(0), rev(0), full((8, w)), full(gw), full(gw)],
        out_specs=[rev(0), full(gw), full(gw), full((8, w))],
        out_shape=[jax.ShapeDtypeStruct((s, w), BF16), jax.ShapeDtypeStruct(gw, F32),
                   jax.ShapeDtypeStruct(gw, F32), jax.ShapeDtypeStruct((8, w), F32)],
        scratch_shapes=[pltpu.VMEM((t + HALO, w), F32), pltpu.VMEM((t + HALO, w), F32),
                        pltpu.VMEM((t + HALO, w), F32), pltpu.VMEM((t, w), F32), pltpu.VMEM((t, w), F32),
                        pltpu.VMEM((1, w), F32)],
        compiler_params=_cparams(("arbitrary",)),
    )(h0a, h0a, h, h, dh, pvec, wa16, wx16)


SB_T = 256
SB_SUB = 8


def _split16(x):
    hi = x.astype(BF16)
    lo = (x - hi.astype(F32)).astype(BF16)
    return jnp.concatenate([hi, lo], axis=0)


def _sb_tile(q, k, scale, tri, run, causal):
    tq = q.shape[0]
    z = _dot(q, k, NT) * scale
    log1mb = -(jnp.maximum(z, 0.0) + jnp.log(1.0 + jnp.exp(-jnp.abs(z))))
    if causal is not None:
        log1mb = jnp.where(causal, log1mb, 0.0)
    cs = _dot(_split16(log1mb), tri, NN)
    cs = cs[:tq] + cs[tq:]
    wgt = jnp.exp(z + cs + run)
    if causal is not None:
        wgt = jnp.where(causal, wgt, 0.0)
    return z, wgt, run + cs[:, 0:1]


SB_DEAD = -105.0


def _sb_alive(run):
    return (jnp.max(run) > SB_DEAD).astype(jnp.int32)


def _sb_more(carry):
    return (carry[0] >= 0) & (carry[1] > 0)


def _sb_fwd(qkv, bg):
    s = qkv.shape[0]
    t = min(SB_T, s)
    nq = s // t
    dh = SB_HEAD_DIM
    scale = 1.0 / math.sqrt(dh)

    sub = SB_SUB if nq % SB_SUB == 0 else 1

    def body(q_ref, k_ref, v_ref, bg_ref, o_ref, y_ref):
        row = lax.broadcasted_iota(jnp.int32, (t, t), 0)
        col = lax.broadcasted_iota(jnp.int32, (t, t), 1)
        tri = jnp.where(row >= col, 1.0, 0.0).astype(BF16)
        causal = col < row
        def tile(q, kb, run, mask):
            off = pl.multiple_of(kb * t, t)
            _, wgt, run = _sb_tile(q, k_ref[pl.ds(off, t), :], scale, tri, run, mask)
            return run, _dot(wgt.astype(BF16), v_ref[pl.ds(off, t), :], NN)

        blocks = []
        for u in range(sub):
            iq = pl.program_id(1) * sub + u
            q = q_ref[pl.ds(u * t, t), :]
            run_d, acc_d = tile(q, iq, jnp.zeros((t, 1), F32), causal)
            run_l, acc_l = tile(q, jnp.maximum(iq - 1, 0), run_d, None)
            blocks.append((iq, q, jnp.where(iq > 0, run_l, run_d), acc_d + jnp.where(iq > 0, acc_l, 0.0)))
        for u, (iq, q, run, acc) in enumerate(blocks):
            def loop(carry, q=q):
                kb, _, run, acc = carry
                run, part = tile(q, kb, run, None)
                return kb - 1, _sb_alive(run), run, acc + part

            _, _, run, acc = lax.while_loop(_sb_more, loop, (iq - 2, _sb_alive(run), run, acc))
            rows = pl.ds(u * t, t)
            o_ref[rows, :] = acc
            g = bg_ref[rows, :]
            y_ref[rows, :] = (acc * (g * _sigmoid(g))).astype(BF16)

    blk = lambda off: pl.BlockSpec((sub * t, dh), functools.partial(lambda h, i, off: (i, h + off), off=off))
    col = lambda off: pl.BlockSpec((s, dh), functools.partial(lambda h, i, off: (0, h + off), off=off))
    return pl.pallas_call(
        body, name="sb_fwd", grid=(SB_HEADS, nq // sub),
        in_specs=[blk(0), col(SB_HEADS), col(2 * SB_HEADS), blk(0)],
        out_specs=[blk(0), blk(0)],
        out_shape=[jax.ShapeDtypeStruct((s, SB_HEADS * dh), F32), jax.ShapeDtypeStruct((s, SB_HEADS * dh), BF16)],
        compiler_params=_cparams(("parallel", "arbitrary")),
    )(qkv, qkv, qkv, bg)


def _sb_bwd(qkv, dob):
    s = qkv.shape[0]
    t = min(SB_T, s)
    nq = s // t
    dh = SB_HEAD_DIM
    scale = 1.0 / math.sqrt(dh)

    sub = SB_SUB if nq % SB_SUB == 0 else 1

    def body(q_ref, k_ref, v_ref, do_ref, dq_ref, dk_out, dv_out, e_ref, b_ref, pe_ref, pb_ref, dk_ref, dv_ref):
        @pl.when(pl.program_id(1) == 0)
        def _():
            dk_ref[...] = jnp.zeros_like(dk_ref)
            dv_ref[...] = jnp.zeros_like(dv_ref)

        row = lax.broadcasted_iota(jnp.int32, (t, t), 0)
        col = lax.broadcasted_iota(jnp.int32, (t, t), 1)
        tri = jnp.where(row >= col, 1.0, 0.0).astype(BF16)
        tri_x = jnp.where(row < col, 1.0, 0.0).astype(BF16)
        causal = col < row

        def sweep1(q, do, kb, e_dst, b_dst, run, mask, live=None):
            off = pl.multiple_of(kb * t, t)
            v = v_ref[pl.ds(off, t), :]
            z, wgt, run = _sb_tile(q, k_ref[pl.ds(off, t), :], scale, tri, run, mask)
            if live is not None:
                wgt = jnp.where(live, wgt, 0.0)
            e_dst[...] = wgt * _dot(do, v, NT)
            b_dst[...] = _sigmoid(z)
            dv_ref[pl.ds(off, t), :] += _dot(wgt.astype(BF16), do, TN)
            return run

        def sweep2(q, kb, e_src, b_src, pre, mask, live=None):
            off = pl.multiple_of(kb * t, t)
            e = e_src[...]
            beta = b_src[...]
            ps = _dot(_split16(e), tri_x, NN)
            ps = ps[:t] + ps[t:]
            dz = e * (1.0 - beta) - beta * (ps + pre)
            if mask is not None:
                dz = jnp.where(mask, dz, 0.0)
            if live is not None:
                dz = jnp.where(live, dz, 0.0)
            dz16 = (dz * scale).astype(BF16)
            dk_ref[pl.ds(off, t), :] += _dot(dz16, q, TN)
            return pre + ps[:, t - 1:t] + e[:, t - 1:t], _dot(dz16, k_ref[pl.ds(off, t), :], NN)

        blocks = []
        for u in range(sub):
            iq = pl.program_id(1) * sub + u
            q, do = q_ref[pl.ds(u * t, t), :], do_ref[pl.ds(u * t, t), :]
            left = jnp.maximum(iq - 1, 0)
            run_d = sweep1(q, do, iq, pe_ref.at[u, 0], pb_ref.at[u, 0], jnp.zeros((t, 1), F32), causal)
            run_l = sweep1(q, do, left, pe_ref.at[u, 1], pb_ref.at[u, 1], run_d, None, iq > 0)
            blocks.append((iq, q, do, left, jnp.where(iq > 0, run_l, run_d)))
        carried = []
        for iq, q, do, left, run in blocks:
            def loop1(carry, q=q, do=do):
                run = sweep1(q, do, carry[0], e_ref.at[carry[0]], b_ref.at[carry[0]], carry[2], None)
                return carry[0] - 1, _sb_alive(run), run

            first = lax.while_loop(_sb_more, loop1, (iq - 2, _sb_alive(run), run))[0] + 1

            def loop2(kb, carry, q=q):
                pre, part = sweep2(q, kb, e_ref.at[kb], b_ref.at[kb], carry[0], None)
                return pre, carry[1] + part

            carried.append(lax.fori_loop(first, iq - 1, loop2, (jnp.zeros((t, 1), F32), jnp.zeros((t, dh), F32))))
        for u, ((iq, q, do, left, _), (pre, dq)) in enumerate(zip(blocks, carried)):
            pre, dq_l = sweep2(q, left, pe_ref.at[u, 1], pb_ref.at[u, 1], pre, None, iq > 0)
            _, dq_d = sweep2(q, iq, pe_ref.at[u, 0], pb_ref.at[u, 0], pre, causal)
            dq_ref[pl.ds(u * t, t), :] = (dq + dq_l + dq_d).astype(BF16)

        @pl.when(pl.program_id(1) == nq // sub - 1)
        def _():
            dk_out[...] = dk_ref[...].astype(BF16)
            dv_out[...] = dv_ref[...].astype(BF16)

    blk = lambda off: pl.BlockSpec((sub * t, dh), functools.partial(lambda h, i, off: (i, h + off), off=off))
    col = lambda off: pl.BlockSpec((s, dh), functools.partial(lambda h, i, off: (0, h + off), off=off))
    wide = SB_HEADS * dh
    return pl.pallas_call(
        body, name="sb_bwd", grid=(SB_HEADS, nq // sub),
        in_specs=[blk(0), col(SB_HEADS), col(2 * SB_HEADS), blk(0)],
        out_specs=[blk(0), col(0), col(0)],
        out_shape=[jax.ShapeDtypeStruct((s, wide), BF16)] * 3,
        scratch_shapes=[pltpu.VMEM((nq, t, t), F32), pltpu.VMEM((nq, t, t), F32),
                        pltpu.VMEM((sub, 2, t, t), F32), pltpu.VMEM((sub, 2, t, t), F32),
                        pltpu.VMEM((s, dh), F32), pltpu.VMEM((s, dh), F32)],
        compiler_params=_cparams(("parallel", "arbitrary")),
    )(qkv, qkv, qkv, dob)


def _alibi_slope(h):
    return float(2.0 ** (-8.0 * (h + 1) / C_HEADS))


GROUP_ROWS = C_GROUP * WINDOW


def _swa_window(n):
    qb = WINDOW
    i = lax.broadcasted_iota(jnp.int32, (qb, 2 * qb), 0)
    j = lax.broadcasted_iota(jnp.int32, (qb, 2 * qb), 1)
    d = i - j + qb
    valid = (d >= 0) & (d < WINDOW) & ((j >= qb) | (n > 0))
    stacked = lambda x: jnp.concatenate([x] * C_GROUP, axis=0)
    return stacked(d.astype(F32)), stacked(jnp.where(valid, 0.0, -1e30))


def _swa_group_cols(c, sink_ref):
    head = lax.shift_right_logical(lax.broadcasted_iota(jnp.int32, (GROUP_ROWS, 1), 0), WINDOW.bit_length() - 1)
    slope = jnp.zeros((GROUP_ROWS, 1), F32)
    sink = jnp.zeros((GROUP_ROWS, 1), F32)
    for hh in range(C_GROUP):
        slope = jnp.where(head == hh, _alibi_slope(c * C_GROUP + hh), slope)
        sink = jnp.where(head == hh, sink_ref[c * C_GROUP + hh], sink)
    return slope, sink


def _swa_probs(qg, kw, sink, slope, dist, bias, scale):
    sc = _dot(qg, kw, NT) * scale - slope * dist + bias
    m = jnp.maximum(jnp.max(sc, axis=-1, keepdims=True), sink)
    p = jnp.exp(sc - m)
    ps = jnp.exp(sink - m)
    inv = 1.0 / (jnp.sum(p, axis=-1, keepdims=True) + ps)
    return p * inv, ps * inv


def _stack_heads(x, c):
    hd = C_HEAD_DIM
    return jnp.concatenate([x[:, (c * C_GROUP + hh) * hd:(c * C_GROUP + hh + 1) * hd] for hh in range(C_GROUP)], axis=0)


SWA_SUB = 2


def _swa_kv_specs(sub, kvw):
    return [pl.BlockSpec((WINDOW, 2 * kvw), functools.partial(lambda n, w: (n * sub + w, 0), w=w)) for w in range(sub + 1)]


def _swa_fwd(q16, gate, kvp, sinks):
    s = q16.shape[0]
    qb = WINDOW
    hd = C_HEAD_DIM
    scale = 1.0 / math.sqrt(hd)
    kvw = C_KV_HEADS * hd
    sub = SWA_SUB if (s // qb) % SWA_SUB == 0 else 1

    def body(sink_ref, q_ref, g_ref, *refs):
        kv_refs, (o_ref, y_ref) = refs[:sub + 1], refs[sub + 1:]
        for u in range(sub):
            rows = slice(u * qb, (u + 1) * qb)
            q = q_ref[rows, :]
            kv = jnp.concatenate([kv_refs[u][...], kv_refs[u + 1][...]], axis=0)
            dist, bias = _swa_window(pl.program_id(0) * sub + u)
            for c in range(C_KV_HEADS):
                slope, sink = _swa_group_cols(c, sink_ref)
                kw = kv[:, c * hd:(c + 1) * hd]
                vw = kv[:, kvw + c * hd:kvw + (c + 1) * hd]
                p, _ = _swa_probs(_stack_heads(q, c), kw, sink, slope, dist, bias, scale)
                og = _dot(p.astype(BF16), vw, NN)
                for hh in range(C_GROUP):
                    h = c * C_GROUP + hh
                    o_ref[rows, h * hd:(h + 1) * hd] = og[hh * qb:(hh + 1) * qb, :]
        g = g_ref[...]
        y_ref[...] = (o_ref[...] * (g * _sigmoid(g))).astype(BF16)

    wide = C_HEADS * hd
    big = pl.BlockSpec((sub * qb, wide), lambda n: (n, 0))
    return pl.pallas_call(
        body, name="swa_fwd", grid=(s // (sub * qb),),
        in_specs=[pl.BlockSpec(memory_space=pltpu.SMEM), big, big] + _swa_kv_specs(sub, kvw),
        out_specs=[big, big],
        out_shape=[jax.ShapeDtypeStruct((s, wide), F32), jax.ShapeDtypeStruct((s, wide), BF16)],
        compiler_params=_cparams(("arbitrary",)),
    )(sinks, q16, gate, *([kvp] * (sub + 1)))


def _swa_bwd(q16, do16, kvp, sinks):
    s = q16.shape[0]
    qb = WINDOW
    hd = C_HEAD_DIM
    scale = 1.0 / math.sqrt(hd)
    kvw = C_KV_HEADS * hd
    sub = 1
    nstep = s // (sub * qb)

    def body(sink_ref, q_ref, do_ref, *refs):
        kv_refs, (dq_ref, dkv_ref, ds_ref, sacc_ref) = refs[:sub + 1], refs[sub + 1:]

        @pl.when(pl.program_id(0) == 0)
        def _():
            dkv_ref[...] = jnp.zeros_like(dkv_ref)
            sacc_ref[...] = jnp.zeros_like(sacc_ref)

        for u in range(sub):
            rows = slice(u * qb, (u + 1) * qb)
            n = pl.program_id(0) * sub + u
            q = q_ref[rows, :]
            do = do_ref[rows, :]
            kv = jnp.concatenate([kv_refs[u][...], kv_refs[u + 1][...]], axis=0)
            off = pl.multiple_of(n * qb, qb)
            dist, bias = _swa_window(n)
            for c in range(C_KV_HEADS):
                slope, sink = _swa_group_cols(c, sink_ref)
                kw = kv[:, c * hd:(c + 1) * hd]
                vw = kv[:, kvw + c * hd:kvw + (c + 1) * hd]
                qg = _stack_heads(q, c)
                dog = _stack_heads(do, c)
                p, ps = _swa_probs(qg, kw, sink, slope, dist, bias, scale)
                dp = _dot(dog, vw, NT)
                dd = jnp.sum(p * dp, axis=-1, keepdims=True)
                ds16 = (p * (dp - dd) * scale).astype(BF16)
                sacc_ref[c] += ps * dd
                dqg = _dot(ds16, kw, NN).astype(BF16)
                for hh in range(C_GROUP):
                    h = c * C_GROUP + hh
                    dq_ref[rows, h * hd:(h + 1) * hd] = dqg[hh * qb:(hh + 1) * qb, :]
                dkv_ref[pl.ds(off, 2 * qb), c * hd:(c + 1) * hd] += _dot(ds16, qg, TN)
                dkv_ref[pl.ds(off, 2 * qb), kvw + c * hd:kvw + (c + 1) * hd] += _dot(p.astype(BF16), dog, TN)

        @pl.when(pl.program_id(0) == nstep - 1)
        def _():
            lane = lax.broadcasted_iota(jnp.int32, (1, 128), 1)
            row = jnp.zeros((1, 128), F32)
            for c in range(C_KV_HEADS):
                for hh in range(C_GROUP):
                    tot = jnp.sum(sacc_ref[c, pl.ds(hh * qb, qb), :], axis=0, keepdims=True)
                    row = jnp.where(lane == c * C_GROUP + hh, -tot, row)
            ds_ref[...] = row

    wide = C_HEADS * hd
    big = pl.BlockSpec((sub * qb, wide), lambda n: (n, 0))
    return pl.pallas_call(
        body, name="swa_bwd", grid=(nstep,),
        in_specs=[pl.BlockSpec(memory_space=pltpu.SMEM), big, big] + _swa_kv_specs(sub, kvw),
        out_specs=[big,
                   pl.BlockSpec((s + qb, 2 * kvw), lambda n: (0, 0)), pl.BlockSpec((1, 128), lambda n: (0, 0))],
        out_shape=[jax.ShapeDtypeStruct((s, wide), BF16), jax.ShapeDtypeStruct((s + qb, 2 * kvw), F32),
                   jax.ShapeDtypeStruct((1, 128), F32)],
        scratch_shapes=[pltpu.VMEM((C_KV_HEADS, GROUP_ROWS, 1), F32)],
        compiler_params=_cparams(("arbitrary",)),
    )(sinks, q16, do16, *([kvp] * (sub + 1)))


def _adamw(parts, w, m, v, *, name, tr=496):
    npart, r, c = parts.shape
    tr = min(tr, r)
    assert r % tr == 0 and tr % 16 == 0
    c1 = 1.0 / (1.0 - ADAM_B1 ** ADAM_STEP)
    c2 = 1.0 / (1.0 - ADAM_B2 ** ADAM_STEP)

    def body(p_ref, w_ref, m_ref, v_ref, g_ref, d_ref, nm_ref, nv_ref):
        g = p_ref[0].astype(F32)
        for j in range(1, npart):
            g = g + p_ref[j].astype(F32)
        nm = ADAM_B1 * m_ref[...] + (1.0 - ADAM_B1) * g
        nv = ADAM_B2 * v_ref[...] + (1.0 - ADAM_B2) * (g * g)
        g_ref[...] = g
        nm_ref[...] = nm
        nv_ref[...] = nv
        d_ref[...] = -ADAM_LR * ((nm * c1) / (jnp.sqrt(nv * c2) + ADAM_EPS) + ADAM_WD * w_ref[...])

    spec = pl.BlockSpec((tr, c), lambda i: (i, 0))
    return pl.pallas_call(
        body, name=name, grid=(r // tr,),
        in_specs=[pl.BlockSpec((npart, tr, c), lambda i: (0, i, 0)), spec, spec, spec],
        out_specs=[spec] * 4, out_shape=[jax.ShapeDtypeStruct((r, c), F32)] * 4,
        compiler_params=_cparams(("parallel",)),
    )(parts, w, m, v)


GATHER = "gather"
A2A = "a2a"
GATHER_COLS = "gather_cols"
A2A_COLS = "a2a_cols"


def _gather_two_level(bufs, modes, x, *, tr=1024, name):
    nb = len(bufs)
    rows, cols = x.shape
    tr = min(tr, rows)
    steps = rows // tr

    def body(*refs):
        x_ref, src, x16_ref, dst = refs[0], refs[1:1 + nb], refs[1 + nb], refs[2 + nb:2 + 2 * nb]
        send_sems, recv_sems, local_sems = refs[2 + 2 * nb:]
        cx, cy, cc = lax.axis_index("x"), lax.axis_index("y"), lax.axis_index("c")
        me = 4 * cx + 2 * cy + cc
        here, sibling = (cx, cy, cc), (cx, cy, 1 - cc)

        def copy(b, k, origin, to, source):
            return pltpu.make_async_remote_copy(
                src_ref=source, dst_ref=_slot(dst[b], modes[b], origin), send_sem=send_sems.at[b, k - 1],
                recv_sem=recv_sems.at[b, k - 1], device_id=to, device_id_type=pl.DeviceIdType.MESH)

        def first_copies():
            local = [pltpu.make_async_copy(src[b], _slot(dst[b], modes[b], me), local_sems.at[b]) for b in range(nb)]
            sends = [copy(b, k, me, (cx ^ (k >> 2), cy ^ ((k >> 1) & 1), cc ^ (k & 1)), src[b])
                     for k in (1, 2, 4, 6) for b in range(nb)]
            return local, sends

        @pl.when(pl.program_id(0) == 0)
        def _():
            local, sends = first_copies()
            for cp in local + sends:
                cp.start()

        x16_ref[...] = x_ref[...].astype(BF16)

        @pl.when(pl.program_id(0) == steps - 1)
        def _():
            local, sends = first_copies()
            for j in (2, 4, 6):
                for b in range(nb):
                    copy(b, j, me ^ j, here, src[b]).wait_recv()
                    passed = copy(b, j ^ 1, me ^ j, sibling, _slot(dst[b], modes[b], me ^ j))
                    passed.start()
                    sends.append(passed)
            for k in (1, 3, 5, 7):
                for b in range(nb):
                    copy(b, k, me ^ k, here, src[b]).wait_recv()
            for cp in sends:
                cp.wait_send()
            for cp in local:
                cp.wait()

    hbm = pl.BlockSpec(memory_space=pl.ANY)
    tile = pl.BlockSpec((tr, cols), lambda i: (i, 0))
    return pl.pallas_call(
        body, name=name, grid=(steps,), in_specs=[tile] + [hbm] * nb, out_specs=[tile] + [hbm] * nb,
        out_shape=[jax.ShapeDtypeStruct(x.shape, BF16)] + _exchange_out_shapes(bufs, modes),
        scratch_shapes=_exchange_sems(nb),
        compiler_params=pltpu.CompilerParams(dimension_semantics=("arbitrary",), vmem_limit_bytes=VMEM_LIMIT,
                                             has_side_effects=True),
    )(x, *bufs)


def _exchange_out_shapes(bufs, gather):
    shapes = {GATHER: lambda s: (N_DEV,) + s, A2A: lambda s: s, GATHER_COLS: lambda s: (s[0], N_DEV * s[1]),
              A2A_COLS: lambda s: (N_DEV, s[0], s[1] // N_DEV)}
    return [jax.ShapeDtypeStruct(shapes[g](tuple(b.shape)), b.dtype) for b, g in zip(bufs, gather)]


def _sent(src, mode, peer):
    if mode == A2A:
        return src.at[peer]
    if mode == A2A_COLS:
        w = src.shape[1] // N_DEV
        return src.at[:, pl.ds(pl.multiple_of(peer * w, 128), w)]
    return src


def _slot(dst, mode, dev):
    if mode == GATHER_COLS:
        w = dst.shape[1] // N_DEV
        return dst.at[:, pl.ds(pl.multiple_of(dev * w, 128), w)]
    return dst.at[dev]


def _exchange_sems(nb):
    return [pltpu.SemaphoreType.DMA((nb, N_DEV - 1)), pltpu.SemaphoreType.DMA((nb, N_DEV - 1)),
            pltpu.SemaphoreType.DMA((nb,))]


def _exchange_copies(src, dst, gather, sems):
    send_sems, recv_sems, local_sems = sems
    x, y, c = lax.axis_index("x"), lax.axis_index("y"), lax.axis_index("c")
    me = 4 * x + 2 * y + c
    local, sends, recvs = [], [], []
    for b in range(len(src)):
        mine = _sent(src[b], gather[b], me)
        local.append(pltpu.make_async_copy(mine, _slot(dst[b], gather[b], me), local_sems.at[b]))
        for k in range(1, N_DEV):
            px, py, pc = x ^ (k >> 2), y ^ ((k >> 1) & 1), c ^ (k & 1)
            peer = 4 * px + 2 * py + pc
            pair = dict(send_sem=send_sems.at[b, k - 1], recv_sem=recv_sems.at[b, k - 1],
                        device_id_type=pl.DeviceIdType.MESH)
            sends.append(pltpu.make_async_remote_copy(
                src_ref=_sent(src[b], gather[b], peer), dst_ref=_slot(dst[b], gather[b], me), device_id=(px, py, pc),
                **pair))
            recvs.append(pltpu.make_async_remote_copy(
                src_ref=mine, dst_ref=_slot(dst[b], gather[b], peer), device_id=(x, y, c), **pair))
    return local, sends, recvs


def _exchange_start(src, dst, gather, sems):
    local, sends, _ = _exchange_copies(src, dst, gather, sems)
    for cp in local + sends:
        cp.start()


def _exchange_wait(src, dst, gather, sems):
    local, sends, recvs = _exchange_copies(src, dst, gather, sems)
    for cp in sends:
        cp.wait_send()
    for cp in recvs:
        cp.wait_recv()
    for cp in local:
        cp.wait()


R_SMALL = 16
REPL_ROWS = 272
REPL_LEN = REPL_ROWS * D_MODEL


def _small_block(parts):
    flat = jnp.concatenate(parts, axis=-1)
    lead = flat.ndim - 1
    return jnp.pad(flat[..., None, :], [(0, 0)] * lead + [(0, R_SMALL - 1), (0, D_MODEL - flat.shape[-1])])


def _small_vectors(conv_w, ln_g, ln_b):
    return _small_block([conv_w.reshape(512), ln_g.reshape(128), ln_b.reshape(128)])


REPL_SHAPES = ((1, 8, 128, 128), (1, 8, 128, 128), (1, 1024), (1, 1024), (1, 1024), (1, 1024), (1, 1024), (1, 1024),
               (1, 16))


def _pack_repl(parts):
    flat = jnp.concatenate([p.reshape(-1) for p in parts])
    return jnp.concatenate([flat, jnp.zeros((REPL_LEN - flat.shape[0],), F32)]).reshape(REPL_ROWS, D_MODEL)


def _unpack_repl(p):
    flat = p.reshape(-1)
    out, o = [], 0
    for shp in REPL_SHAPES:
        n = math.prod(shp)
        out.append(flat[o:o + n].reshape(shp))
        o += n
    return out


def kernel(x, e_w_in, e_conv_w, e_conv_b, e_w_gate_a, e_b_gate_a, e_w_gate_x, e_b_gate_x, e_lru_lambda, e_w_out, e_ln_g, e_ln_b, o_w_in, o_sinks, o_w_out, o_ln_g, o_ln_b, loss_target, m_e_w_in, m_e_conv_w, m_e_conv_b, m_e_w_gate_a, m_e_b_gate_a, m_e_w_gate_x, m_e_b_gate_x, m_e_lru_lambda, m_e_w_out, m_e_ln_g, m_e_ln_b, m_o_w_in, m_o_sinks, m_o_w_out, m_o_ln_g, m_o_ln_b, v_e_w_in, v_e_conv_w, v_e_conv_b, v_e_w_gate_a, v_e_b_gate_a, v_e_w_gate_x, v_e_b_gate_x, v_e_lru_lambda, v_e_w_out, v_e_ln_g, v_e_ln_b, v_o_w_in, v_o_sinks, v_o_w_out, v_o_ln_g, v_o_ln_b):
    d = D_MODEL
    x0 = x[0]
    target = loss_target[0]
    s = x0.shape[0]

    spack = _small_block([e_conv_w.reshape(512), o_ln_g.reshape(128), o_ln_b.reshape(128)])
    x0_16, wi_e, sall = _gather_two_level([e_w_in[0].astype(BF16), spack], [GATHER_COLS, GATHER], x0,
                                          name="gather_w_in")
    conv_w = sall[:, 0, 0:512].reshape(N_DEV, 4, 128).transpose(1, 0, 2).reshape(4, d)
    ln_g_o = sall[:, 0, 512:640].reshape(1, d)
    ln_b_o = sall[:, 0, 640:768].reshape(1, d)
    pvec = jnp.concatenate([conv_w, e_conv_b, e_b_gate_a, e_b_gate_x, e_lru_lambda], axis=0)
    wa16 = e_w_gate_a[0].astype(BF16)
    wx16 = e_w_gate_x[0].astype(BF16)
    sinks = o_sinks[0]

    h0a, wall_out_e = _matmul(x0_16, wi_e, mode="nn", n_out=2 * d, b_off=0, name="l0_in_a",
                              comm=([e_w_out[0].astype(BF16)], [GATHER]))
    qkv, wall_in_o = _matmul(x0_16, wi_e, mode="nn", n_out=3 * d, b_off=2 * d, out_dtype=BF16, name="l0_in_qkv",
                             comm=([o_w_in[0].astype(BF16)], [GATHER]))
    bg, wall_out_o = _matmul(x0_16, wi_e, mode="nn", n_out=d, b_off=5 * d, name="l0_in_bg",
                             comm=([o_w_out[0].astype(BF16)], [GATHER]))
    wo_e = wall_out_e.reshape(2 * d, d)
    wi_o = wall_in_o.transpose(1, 0, 2).reshape(d, 2304)
    wi_o = jnp.concatenate([wi_o[:, :1024], wi_o[:, 1280:], wi_o[:, 1024:1280]], axis=1)
    wo_o = wall_out_o.reshape(d, d)
    ya, hst = _rglru_fwd(h0a, pvec, wa16, wx16)
    ob, yb = _sb_fwd(qkv, bg)
    z0, x1, x1_16 = _matmul([ya, yb], wo_e, mode="nn", tm=512, name="l0_out",
                            epi=(_epi_ln_fwd, [(x0, 0)], [e_ln_g, e_ln_b], [F32, F32, BF16], 0))

    q1 = _matmul(x1_16, wi_o, mode="nn", n_out=d, b_off=0, out_dtype=BF16, name="l1_in_q")
    g1 = _matmul(x1_16, wi_o, mode="nn", n_out=d, b_off=d, name="l1_in_g")
    kv1 = _matmul(x1_16, wi_o, mode="nn", n_out=256, b_off=2 * d, tn=256, out_dtype=BF16, name="l1_in_kv")
    kvp = jnp.concatenate([jnp.zeros((WINDOW, 256), BF16), kv1], axis=0)
    o1, y1 = _swa_fwd(q1, g1, kvp, sinks)
    dz1, dz1_16, loss_cols, dg_o, db_o = _matmul(
        y1, wo_o, mode="nn", tm=512, name="l1_out",
        epi=(_epi_ln_loss_bwd, [(x1, 0), (target, 0)], [ln_g_o, ln_b_o], [F32, BF16], 3))
    loss_hi = jnp.sum(loss_cols).astype(BF16).astype(F32)
    loss_terms = jnp.stack([loss_hi, jnp.sum(loss_cols) - loss_hi]).reshape(1, 2)

    dwo_o = _matmul(y1, dz1_16, mode="tn", out_dtype=BF16, name="l1_dwout")
    do1, dg1, parts_out_o = _matmul(dz1_16, wo_o, mode="nt", tm=512, name="l1_dy",
                                    comm=([dwo_o.reshape(N_DEV, 128, d)], [A2A]),
                                    epi=(_epi_gate_bwd, [(o1, 0), (g1, 0)], [], [BF16, BF16], 0))
    dq1, dkvp, dsink = _swa_bwd(q1, do1, kvp, sinks)
    dkv1 = dkvp[WINDOW:].astype(BF16)
    dh1 = [dq1, dg1, dkv1]
    dw_qg = _matmul(x1_16, [dq1, dg1], mode="tn", out_dtype=BF16, name="l1_dwin_qg")
    dw_kv = _matmul(x1_16, dkv1, mode="tn", out_dtype=BF16, name="l1_dwin_kv")
    dwi_o = jnp.concatenate([dw_qg[:, :1024], dw_kv, dw_qg[:, 1024:]], axis=1)
    dz0, dz0_16, dg_e, db_e, parts_in_o = _matmul(
        dh1, wi_o, mode="nt", tm=512, tk=2304, name="l1_dx",
        comm=([dwi_o.reshape(d, N_DEV, 288).transpose(1, 0, 2)], [A2A]),
        epi=(_epi_ln_bwd, [(dz1, 0), (z0, 0)], [e_ln_g], [F32, BF16], 2))

    dwo_e = _matmul([ya, yb], dz0_16, mode="tn", out_dtype=BF16, name="l0_dwout")
    dhst, dag, parts_out_e = _matmul(
        dz0_16, wo_e, mode="nt", tm=512, n_out=d, b_off=0, name="l0_dy_a",
        comm=([dwo_e.reshape(N_DEV, 256, d)], [A2A]),
        epi=(_epi_gate_bwd, [(hst, 0), (h0a, d)], [], [F32, BF16], 0))
    dob, dbg = _matmul(dz0_16, wo_e, mode="nt", tm=512, n_out=d, b_off=d, name="l0_dy_b",
                       epi=(_epi_gate_bwd, [(ob, 0), (bg, 0)], [], [BF16, BF16], 0))
    dq0, dk0, dv0 = _sb_bwd(qkv, dob)
    dax, dwa, dwx, dpv = _rglru_bwd(h0a, hst, dhst, pvec, wa16, wx16)
    dh0 = [dax, dag, dq0, dk0, dv0, dbg]
    repl = _pack_repl([dwa, dwx, dpv[4:5], dpv[5:6], dpv[6:7], dpv[7:8], dg_e, db_e, dsink[:, :C_HEADS], loss_terms])
    dwi_e, parts_repl = _matmul(x0_16, dh0, mode="tn", out_dtype=BF16, name="l0_dwin",
                                comm=([repl.astype(BF16)], [GATHER]))
    small = _small_block([dpv[0:4].reshape(4, N_DEV, 128).transpose(1, 0, 2).reshape(N_DEV, 512),
                          dg_o.reshape(N_DEV, 128), db_o.reshape(N_DEV, 128)]).astype(BF16)
    grad_x, parts_in_e, parts_small = _matmul(dh0, wi_e, mode="nt", tm=512, res=dz0, alpha=ALPHA, name="l0_dx",
                                              comm=([dwi_e, small], [A2A_COLS, A2A]))

    res_in_e = _adamw(parts_in_e, e_w_in[0], m_e_w_in[0], v_e_w_in[0], tr=512, name="adamw_w_in_e")
    res_out_e = _adamw(parts_out_e, e_w_out[0], m_e_w_out[0], v_e_w_out[0], name="adamw_w_out_e")
    res_in_o = _adamw(parts_in_o, o_w_in[0], m_o_w_in[0], v_o_w_in[0], tr=512, name="adamw_w_in_o")
    res_out_o = _adamw(parts_out_o, o_w_out[0], m_o_w_out[0], v_o_w_out[0], name="adamw_w_out_o")
    res_small = _adamw(parts_small, _small_vectors(e_conv_w, o_ln_g, o_ln_b),
                       _small_vectors(m_e_conv_w, m_o_ln_g, m_o_ln_b),
                       _small_vectors(v_e_conv_w, v_o_ln_g, v_o_ln_b), name="adamw_vectors")

    w_r = _pack_repl([e_w_gate_a, e_w_gate_x, e_conv_b, e_b_gate_a, e_b_gate_x, e_lru_lambda, e_ln_g, e_ln_b, o_sinks])
    m_r = _pack_repl([m_e_w_gate_a, m_e_w_gate_x, m_e_conv_b, m_e_b_gate_a, m_e_b_gate_x, m_e_lru_lambda, m_e_ln_g,
                      m_e_ln_b, m_o_sinks])
    v_r = _pack_repl([v_e_w_gate_a, v_e_w_gate_x, v_e_conv_b, v_e_b_gate_a, v_e_b_gate_x, v_e_lru_lambda, v_e_ln_g,
                      v_e_ln_b, v_o_sinks])
    g_r, d_r, nm_r, nv_r = _adamw(parts_repl, w_r, m_r, v_r, name="adamw_replicated")
    loss_at = sum(math.prod(shp) for shp in REPL_SHAPES)
    loss = g_r[loss_at // d, loss_at % d] + g_r[(loss_at + 1) // d, (loss_at + 1) % d]

    def assemble(i, rp):
        vec = res_small[i][0]
        cw, lg_o, lb_o = vec[0:512].reshape(1, 4, 128), vec[512:640].reshape(1, 128), vec[640:768].reshape(1, 128)
        w_a, w_x, cb, b_a, b_x, lam, lg_e, lb_e, snk = _unpack_repl(rp)
        return [res_in_e[i][None], cw, cb, w_a, b_a, w_x, b_x, lam, res_out_e[i][None], lg_e, lb_e,
                res_in_o[i][None], snk, res_out_o[i][None], lg_o, lb_o]

    return (loss, grad_x[None], *assemble(0, g_r), *assemble(1, d_r), *assemble(2, nm_r), *assemble(3, nv_r))
```

```python
import functools
import math

import jax
import jax.numpy as jnp
from jax import lax
from jax.experimental import pallas as pl
from jax.experimental.pallas import tpu as pltpu

F32 = jnp.float32
BF16 = jnp.bfloat16

N_DEV = 8
D_MODEL = 1024
LRU_BLOCKS = 8
LRU_BLOCK = 128
LRU_C = 8.0
SB_HEADS = 8
SB_HEAD_DIM = 128
C_HEADS = 16
C_KV_HEADS = 2
C_GROUP = 8
C_HEAD_DIM = 64
WINDOW = 128
DEPTH = 2
ALPHA = float((2 * DEPTH) ** 0.25)
LN_EPS = 1e-5
ADAM_LR = 0.001
ADAM_B1 = 0.9
ADAM_B2 = 0.999
ADAM_EPS = 1e-08
ADAM_WD = 0.01
ADAM_STEP = 10

VMEM_LIMIT = 56 * 1024 * 1024

NN = ((1,), (0,))
NT = ((1,), (1,))
TN = ((0,), (0,))


def _dot(a, b, dims):
    return lax.dot_general(a, b, (dims, ((), ())), preferred_element_type=F32)


def _sigmoid(x):
    return 1.0 / (1.0 + jnp.exp(-x))


def _cparams(sem, vmem=VMEM_LIMIT):
    return pltpu.CompilerParams(dimension_semantics=sem, vmem_limit_bytes=vmem)


def _matmul(a, b, *, mode, n_out=None, b_off=0, out_dtype=F32, res=None, alpha=1.0,
            tm=1024, tn=1024, tk=1024, comm=None, epi=None, name):
    a_list = list(a) if isinstance(a, (list, tuple)) else [a]
    b_list = list(b) if isinstance(b, (list, tuple)) else [b]
    if mode == "tn":
        k = a_list[0].shape[0]
        m = sum(p.shape[1] for p in a_list)
        n = n_out if n_out is not None else sum(p.shape[1] for p in b_list)
        assert all(p.shape[1] == tm for p in a_list) or len(a_list) == 1
        assert all(p.shape[1] == tn for p in b_list) or len(b_list) == 1
    else:
        assert len(b_list) == 1
        m = a_list[0].shape[0]
        k = sum(p.shape[1] for p in a_list)
        n = n_out if n_out is not None else (b.shape[0] if mode == "nt" else b.shape[1])
        if len(a_list) > 1:
            tk = k
    tm, tn, tk = min(tm, m), min(tn, n), min(tk, k)
    assert m % tm == 0 and n % tn == 0 and k % tk == 0 and b_off % tn == 0
    grid = (m // tm, n // tn, k // tk)
    nk = grid[2]
    jo = b_off // tn
    dims = {"nn": NN, "nt": NT, "tn": TN}[mode]
    nc = len(comm[0]) if comm else 0
    if epi:
        assert res is None
        epi_fn, epi_rows, epi_vecs, epi_dtypes, n_sums = epi
        assert n_sums == 0 or grid[1] == 1
    else:
        epi_rows, epi_vecs, epi_dtypes, n_sums = [], [], [out_dtype], 0
    n_ab = len(a_list) + len(b_list)
    n_in = n_ab + (res is not None) + len(epi_rows) + len(epi_vecs)
    n_res = len(epi_dtypes) + n_sums

    def body(*refs):
        a_refs, b_refs = refs[:len(a_list)], refs[len(a_list):n_ab]
        r_ref = refs[n_ab] if res is not None else None
        row_refs = refs[n_in - len(epi_rows) - len(epi_vecs):n_in - len(epi_vecs)]
        vec_refs = refs[n_in - len(epi_vecs):n_in]
        o_refs = refs[n_in + nc:n_in + nc + len(epi_dtypes)]
        sum_refs = refs[n_in + nc + len(epi_dtypes):n_in + nc + n_res]
        if comm:
            step = [pl.program_id(ax) for ax in range(3)]
            c_src = refs[n_in:n_in + nc]
            c_dst = refs[n_in + nc + n_res:n_in + 2 * nc + n_res]
            sems = refs[len(refs) - 3:]

            @pl.when((step[0] == 0) & (step[1] == 0) & (step[2] == 0))
            def _():
                _exchange_start(c_src, c_dst, comm[1], sems)

        def finish(out):
            if not epi:
                if r_ref is not None:
                    out = out + alpha * r_ref[...]
                o_refs[0][...] = out.astype(out_dtype)
                return
            outs, sums = epi_fn(out, [r[...] for r in row_refs], [v[...] for v in vec_refs])
            for o_ref, o in zip(o_refs, outs):
                o_ref[...] = o.astype(o_ref.dtype)
            if sum_refs:
                first_rows = pl.program_id(0) == 0

                @pl.when(first_rows)
                def _():
                    for s_ref, part_sum in zip(sum_refs, sums):
                        s_ref[...] = part_sum

                @pl.when(jnp.logical_not(first_rows))
                def _():
                    for s_ref, part_sum in zip(sum_refs, sums):
                        s_ref[...] += part_sum

        def accumulate(part):
            if nk == 1:
                finish(part)
                return
            acc_ref = refs[n_in + 2 * nc + n_res]
            kk = pl.program_id(2)

            @pl.when(kk == 0)
            def _():
                acc_ref[...] = part

            @pl.when(kk > 0)
            def _():
                acc_ref[...] += part

            @pl.when(kk == nk - 1)
            def _():
                finish(acc_ref[...])

        def product(a_ref, b_val):
            return _dot(a_ref[...].astype(BF16), b_val.astype(BF16), dims)

        if mode != "tn" and len(a_list) > 1:
            part, off = None, 0
            for a_ref in a_refs:
                w = a_ref.shape[1]
                b_val = b_refs[0][off:off + w, :] if mode == "nn" else b_refs[0][:, off:off + w]
                part = product(a_ref, b_val) if part is None else part + product(a_ref, b_val)
                off += w
            accumulate(part)
        elif len(a_list) > 1 or len(b_list) > 1:
            for pa, a_ref in enumerate(a_refs):
                for pb, b_ref in enumerate(b_refs):
                    picked = ([pl.program_id(0) == pa] if len(a_list) > 1 else []) + (
                        [pl.program_id(1) == pb] if len(b_list) > 1 else [])
                    pl.when(functools.reduce(jnp.logical_and, picked))(
                        functools.partial(lambda a_ref, b_ref: accumulate(product(a_ref, b_ref[...])), a_ref, b_ref))
        else:
            accumulate(product(a_refs[0], b_refs[0][...]))

        if comm:
            @pl.when((step[0] == grid[0] - 1) & (step[1] == grid[1] - 1) & (step[2] == grid[2] - 1))
            def _():
                _exchange_wait(c_src, c_dst, comm[1], sems)

    def held(axis, p):
        def index(i, j, kk):
            return (jnp.where((i, j)[axis] == p, kk, 0), 0)
        return index

    if mode == "tn":
        if len(a_list) > 1:
            a_specs = [pl.BlockSpec((tk, tm), held(0, p)) for p in range(len(a_list))]
        else:
            a_specs = [pl.BlockSpec((tk, tm), lambda i, j, kk: (kk, i))]
    elif len(a_list) > 1:
        a_specs = [pl.BlockSpec((tm, p.shape[1]), lambda i, j, kk: (i, 0)) for p in a_list]
    else:
        a_specs = [pl.BlockSpec((tm, tk), lambda i, j, kk: (i, kk))]
    if mode == "nt":
        b_specs = [pl.BlockSpec((tn, tk), lambda i, j, kk: (j + jo, kk))]
    elif len(b_list) > 1:
        b_specs = [pl.BlockSpec((tk, tn), held(1, p)) for p in range(len(b_list))]
    else:
        b_specs = [pl.BlockSpec((tk, tn), lambda i, j, kk: (kk, j + jo))]
    o_spec = pl.BlockSpec((tm, tn), lambda i, j, kk: (i, j))
    hbm = pl.BlockSpec(memory_space=pl.ANY)
    row_specs = [pl.BlockSpec((tm, tn), functools.partial(lambda i, j, kk, o: (i, j + o), o=off // tn))
                 for (_, off) in epi_rows]
    vec_specs = [pl.BlockSpec(v.shape, functools.partial(lambda i, j, kk, nd: (0,) * nd, nd=v.ndim)) for v in epi_vecs]
    in_specs = a_specs + b_specs + ([o_spec] if res is not None else []) + row_specs + vec_specs + [hbm] * nc
    args = (a_list + b_list + ([res] if res is not None else []) + [r for (r, _) in epi_rows] + list(epi_vecs)
            + (list(comm[0]) if comm else []))
    out_specs = [o_spec] * len(epi_dtypes) + [pl.BlockSpec((1, tn), lambda i, j, kk: (0, j))] * n_sums + [hbm] * nc
    out_shape = ([jax.ShapeDtypeStruct((m, n), dt) for dt in epi_dtypes] + [jax.ShapeDtypeStruct((1, n), F32)] * n_sums
                 + (_exchange_out_shapes(*comm) if comm else []))
    scratch = ([pltpu.VMEM((tm, tn), F32)] if nk > 1 else []) + (_exchange_sems(nc) if comm else [])
    if comm or n_sums or n_ab > 2:
        params = pltpu.CompilerParams(dimension_semantics=("arbitrary",) * 3, vmem_limit_bytes=VMEM_LIMIT,
                                      has_side_effects=bool(comm))
    else:
        params = _cparams(("parallel", "parallel", "arbitrary"))
    outs = pl.pallas_call(
        body, name=name, grid=grid, in_specs=in_specs, out_specs=out_specs, out_shape=out_shape,
        scratch_shapes=scratch, compiler_params=params,
    )(*args)
    return tuple(outs) if (comm or epi) else outs[0]


def _ln_stats(z):
    mu = jnp.mean(z, axis=-1, keepdims=True)
    zc = z - mu
    var = jnp.mean(zc * zc, axis=-1, keepdims=True)
    rstd = lax.rsqrt(var + LN_EPS)
    return zc * rstd, rstd


def _ln_bwd(dy, xhat, rstd, g):
    dxh = dy * g
    m1 = jnp.mean(dxh, axis=-1, keepdims=True)
    m2 = jnp.mean(dxh * xhat, axis=-1, keepdims=True)
    return rstd * (dxh - m1 - xhat * m2)


def _colsum(x):
    return jnp.sum(x, axis=0, keepdims=True)


def _epi_ln_fwd(acc, rows, vecs):
    z = ALPHA * rows[0] + acc
    xhat, _ = _ln_stats(z)
    x1 = xhat * vecs[0] + vecs[1]
    return [z, x1, x1], []


def _epi_ln_loss_bwd(acc, rows, vecs):
    inv_c = 1.0 / acc.shape[-1]
    z = ALPHA * rows[0] + acc
    xhat, rstd = _ln_stats(z)
    d = xhat * vecs[0] + vecs[1] - rows[1]
    dy = d * inv_c
    dz = _ln_bwd(dy, xhat, rstd, vecs[0])
    return [dz, dz], [_colsum(0.5 * d * d * inv_c), _colsum(dy * xhat), _colsum(dy)]


def _epi_ln_bwd(acc, rows, vecs):
    dy = acc + ALPHA * rows[0]
    xhat, rstd = _ln_stats(rows[1])
    dz = _ln_bwd(dy, xhat, rstd, vecs[0])
    return [dz, dz], [_colsum(dy * xhat), _colsum(dy)]


def _epi_gate_bwd(acc, rows, vecs):
    oo, gg = rows
    sg = _sigmoid(gg)
    return [acc * (gg * sg), acc * oo * (sg * (1.0 + gg * (1.0 - sg)))], []


LRU_T = 256
HALO = 8


def _log1p(y):
    u = 1.0 + y
    return jnp.where(u == 1.0, y, jnp.log(u) * (y / (u - 1.0)))


def _lru_gates(c, wa_ref, wx_ref, pv):
    c16 = c.astype(BF16)
    pre_r = jnp.concatenate(
        [_dot(c16[:, n * LRU_BLOCK:(n + 1) * LRU_BLOCK], wa_ref[n], NN) for n in range(LRU_BLOCKS)], axis=1)
    pre_i = jnp.concatenate(
        [_dot(c16[:, n * LRU_BLOCK:(n + 1) * LRU_BLOCK], wx_ref[n], NN) for n in range(LRU_BLOCKS)], axis=1)
    r = _sigmoid(pre_r + pv[5:6])
    ig = _sigmoid(pre_i + pv[6:7])
    lam = pv[7:8]
    ls = jnp.minimum(lam, 0.0) - _log1p(jnp.exp(-jnp.abs(lam)))
    la = LRU_C * r * ls
    a = jnp.exp(la)
    a2 = a * a
    m = jnp.sqrt(jnp.tanh(-la) * (a2 + 1.0))
    return c16, r, ig, ls, la, a, a2, m


def _conv(ext_ref, x, pv, t):
    return (pv[4:5] + pv[3:4] * x + pv[2:3] * ext_ref[pl.ds(HALO - 1, t), :]
            + pv[1:2] * ext_ref[pl.ds(HALO - 2, t), :] + pv[0:1] * ext_ref[pl.ds(HALO - 3, t), :])


def _rglru_fwd(h0a, pvec, wa16, wx16):
    s = h0a.shape[0]
    w = D_MODEL
    t = min(LRU_T, s)
    assert s % t == 0

    def body(ax_ref, ag_ref, pv_ref, wa_ref, wx_ref, ya_ref, h_ref, ext_ref, a_ref, u_ref, hc_ref):
        i = pl.program_id(0)

        @pl.when(i == 0)
        def _():
            ext_ref[pl.ds(0, HALO), :] = jnp.zeros((HALO, w), F32)
            hc_ref[...] = jnp.zeros((1, w), F32)

        pv = pv_ref[...]
        ax = ax_ref[...]
        ext_ref[pl.ds(HALO, t), :] = ax
        c = _conv(ext_ref, ax, pv, t)
        ext_ref[pl.ds(0, HALO), :] = ax[t - HALO:, :]
        _, _, ig, _, _, a, _, m = _lru_gates(c, wa_ref, wx_ref, pv)
        a_ref[...] = a
        u_ref[...] = m * (ig * c)

        def steps(k, h):
            base = pl.multiple_of(k * HALO, HALO)
            for r in range(HALO):
                h = a_ref[pl.ds(base + r, 1), :] * h + u_ref[pl.ds(base + r, 1), :]
                h_ref[pl.ds(base + r, 1), :] = h
            return h

        hc_ref[...] = lax.fori_loop(0, t // HALO, steps, hc_ref[...])
        ag = ag_ref[...]
        ya_ref[...] = (h_ref[...] * (ag * _sigmoid(ag))).astype(BF16)

    full = lambda shp: pl.BlockSpec(shp, lambda i: (0,) * len(shp))
    return pl.pallas_call(
        body, name="rglru_fwd", grid=(s // t,),
        in_specs=[pl.BlockSpec((t, w), lambda i: (i, 0)), pl.BlockSpec((t, w), lambda i: (i, 1)),
                  full((8, w)), full((LRU_BLOCKS, LRU_BLOCK, LRU_BLOCK)), full((LRU_BLOCKS, LRU_BLOCK, LRU_BLOCK))],
        out_specs=[pl.BlockSpec((t, w), lambda i: (i, 0)), pl.BlockSpec((t, w), lambda i: (i, 0))],
        out_shape=[jax.ShapeDtypeStruct((s, w), BF16), jax.ShapeDtypeStruct((s, w), F32)],
        scratch_shapes=[pltpu.VMEM((t + HALO, w), F32), pltpu.VMEM((t, w), F32), pltpu.VMEM((t, w), F32),
                        pltpu.VMEM((1, w), F32)],
        compiler_params=_cparams(("arbitrary",)),
    )(h0a, h0a, pvec, wa16, wx16)


def _rglru_bwd(h0a, h, dh, pvec, wa16, wx16):
    s = h0a.shape[0]
    w = D_MODEL
    t = min(LRU_T, s)
    nb = s // t
    hb = t // HALO

    def body(ax_ref, axh_ref, h_ref, hh_ref, dh_ref, pv_ref, wa_ref, wx_ref,
             dax_ref, dwa_ref, dwx_ref, dpv_ref, ext_ref, hext_ref, dcext_ref, a_ref, g_ref, gc_ref):
        i = pl.program_id(0)
        blk = nb - 1 - i

        @pl.when(i == 0)
        def _():
            dwa_ref[...] = jnp.zeros_like(dwa_ref)
            dwx_ref[...] = jnp.zeros_like(dwx_ref)
            dpv_ref[...] = jnp.zeros_like(dpv_ref)
            gc_ref[...] = jnp.zeros((1, w), F32)
            dcext_ref[pl.ds(t, HALO), :] = jnp.zeros((HALO, w), F32)

        pv = pv_ref[...]
        ax = ax_ref[...]
        keep = (blk > 0).astype(F32)
        ext_ref[pl.ds(0, HALO), :] = axh_ref[...] * keep
        ext_ref[pl.ds(HALO, t), :] = ax
        hext_ref[pl.ds(0, HALO), :] = hh_ref[...] * keep
        hext_ref[pl.ds(HALO, t), :] = h_ref[...]
        c = _conv(ext_ref, ax, pv, t)
        c16, r, ig, ls, _, a, a2, m = _lru_gates(c, wa_ref, wx_ref, pv)
        a_ref[...] = a

        def steps(k, carry):
            base = pl.multiple_of(t - HALO - k * HALO, HALO)
            for r in reversed(range(HALO)):
                g = dh_ref[pl.ds(base + r, 1), :] + carry
                g_ref[pl.ds(base + r, 1), :] = g
                carry = a_ref[pl.ds(base + r, 1), :] * g
            return carry

        gc_ref[...] = lax.fori_loop(0, t // HALO, steps, gc_ref[...])
        g = g_ref[...]
        hprev = hext_ref[pl.ds(HALO - 1, t), :]
        gm = g * m
        d_la = g * hprev * a - (g * ig * c) * a2 / m
        d_pr = d_la * (LRU_C * ls) * r * (1.0 - r)
        d_pi = gm * c * ig * (1.0 - ig)
        dc = gm * ig
        dpr16 = d_pr.astype(BF16)
        dpi16 = d_pi.astype(BF16)
        dc_parts = []
        for n in range(LRU_BLOCKS):
            sl = slice(n * LRU_BLOCK, (n + 1) * LRU_BLOCK)
            dwa_ref[n] += _dot(c16[:, sl], dpr16[:, sl], TN)
            dwx_ref[n] += _dot(c16[:, sl], dpi16[:, sl], TN)
            dc_parts.append(_dot(dpr16[:, sl], wa_ref[n], NT) + _dot(dpi16[:, sl], wx_ref[n], NT))
        dc = dc + jnp.concatenate(dc_parts, axis=1)
        dcext_ref[pl.ds(0, t), :] = dc
        dax = (pv[3:4] * dc + pv[2:3] * dcext_ref[pl.ds(1, t), :] + pv[1:2] * dcext_ref[pl.ds(2, t), :]
               + pv[0:1] * dcext_ref[pl.ds(3, t), :])
        dax_ref[...] = dax.astype(BF16)
        dcext_ref[pl.ds(t, HALO), :] = dc[:HALO, :]
        sums = [_colsum(dc * ext_ref[pl.ds(HALO - 3 + kk, t), :]) for kk in range(4)]
        sums += [_colsum(dc), _colsum(d_pr), _colsum(d_pi), _colsum(d_la * (LRU_C * r))]
        for kk, part in enumerate(sums):
            dpv_ref[pl.ds(kk, 1), :] += part

        @pl.when(i == nb - 1)
        def _():
            lam = pv[7:8]
            dls = dpv_ref[pl.ds(7, 1), :]
            dpv_ref[pl.ds(7, 1), :] = dls * _sigmoid(-lam)

    full = lambda shp: pl.BlockSpec(shp, lambda i: (0,) * len(shp))
    rev = lambda cb: pl.BlockSpec((t, w), functools.partial(lambda i, cb: (nb - 1 - i, cb), cb=cb))
    halo = lambda cb: pl.BlockSpec(
        (HALO, w), functools.partial(lambda i, cb: (jnp.maximum((nb - 1 - i) * hb - 1, 0), cb), cb=cb))
    gw = (LRU_BLOCKS, LRU_BLOCK, LRU_BLOCK)
    return pl.pallas_call(
        body, name="rglru_bwd", grid=(nb,),
        in_specs=[rev(0), halo(0), rev(0), halo(0), rev(0), full((8, w)), full(gw), full(gw)],
        out_specs=[rev(0), full(gw), full(gw), full((8, w))],
        out_shape=[jax.ShapeDtypeStruct((s, w), BF16), jax.ShapeDtypeStruct(gw, F32),
                   jax.ShapeDtypeStruct(gw, F32), jax.ShapeDtypeStruct((8, w), F32)],
        scratch_shapes=[pltpu.VMEM((t + HALO, w), F32), pltpu.VMEM((t + HALO, w), F32),
                        pltpu.VMEM((t + HALO, w), F32), pltpu.VMEM((t, w), F32), pltpu.VMEM((t, w), F32),
                        pltpu.VMEM((1, w), F32)],
        compiler_params=_cparams(("arbitrary",)),
    )(h0a, h0a, h, h, dh, pvec, wa16, wx16)


SB_T = 256
SB_SUB = 8


def _split16(x):
    hi = x.astype(BF16)
    lo = (x - hi.astype(F32)).astype(BF16)
    return jnp.concatenate([hi, lo], axis=0)


def _sb_scores(q, k, scale, causal):
    z = _dot(q, k, NT) * scale
    log1mb = -(jnp.maximum(z, 0.0) + jnp.log(1.0 + jnp.exp(-jnp.abs(z))))
    if causal is not None:
        log1mb = jnp.where(causal, log1mb, 0.0)
    return z, log1mb


def _sb_cumsum(log1mb, tri):
    tq = log1mb.shape[0]
    cs = _dot(_split16(log1mb), tri, NN)
    return cs[:tq] + cs[tq:]


def _sb_weights(z, cs, run, causal):
    wgt = jnp.exp(z + cs + run)
    if causal is not None:
        wgt = jnp.where(causal, wgt, 0.0)
    return wgt


def _sb_tile(q, k, scale, tri, run, causal):
    z, log1mb = _sb_scores(q, k, scale, causal)
    cs = _sb_cumsum(log1mb, tri)
    return z, _sb_weights(z, cs, run, causal), run + cs[:, 0:1]


SB_DEAD = -105.0


def _sb_alive(run):
    return (jnp.max(run) > SB_DEAD).astype(jnp.int32)


def _sb_more(carry):
    return (carry[0] >= 0) & (carry[1] > 0)


def _sb_fwd(qkv, bg):
    s = qkv.shape[0]
    t = min(SB_T, s)
    nq = s // t
    dh = SB_HEAD_DIM
    scale = 1.0 / math.sqrt(dh)

    sub = SB_SUB if nq % SB_SUB == 0 else 1

    def body(q_ref, k_ref, v_ref, bg_ref, o_ref, y_ref):
        row = lax.broadcasted_iota(jnp.int32, (t, t), 0)
        col = lax.broadcasted_iota(jnp.int32, (t, t), 1)
        tri = jnp.where(row >= col, 1.0, 0.0).astype(BF16)
        causal = col < row
        def tile(q, kb, run, mask):
            off = pl.multiple_of(kb * t, t)
            _, wgt, run = _sb_tile(q, k_ref[pl.ds(off, t), :], scale, tri, run, mask)
            return run, _dot(wgt.astype(BF16), v_ref[pl.ds(off, t), :], NN)

        def rows_of(ref, kb):
            return ref[pl.ds(pl.multiple_of(kb * t, t), t), :]

        iqs = [pl.program_id(1) * sub + u for u in range(sub)]
        qs = [q_ref[pl.ds(u * t, t), :] for u in range(sub)]
        lefts = [jnp.maximum(iq - 1, 0) for iq in iqs]
        scores = [(_sb_scores(q, rows_of(k_ref, iq), scale, causal), _sb_scores(q, rows_of(k_ref, lf), scale, None))
                  for q, iq, lf in zip(qs, iqs, lefts)]
        sums = [(_sb_cumsum(d[1], tri), _sb_cumsum(l[1], tri)) for d, l in scores]
        blocks = []
        for q, iq, lf, (sd, sl), (cd, cl) in zip(qs, iqs, lefts, scores, sums):
            run_d = cd[:, 0:1]
            acc_d = _dot(_sb_weights(sd[0], cd, 0.0, causal).astype(BF16), rows_of(v_ref, iq), NN)
            acc_l = _dot(_sb_weights(sl[0], cl, run_d, None).astype(BF16), rows_of(v_ref, lf), NN)
            blocks.append((iq, q, jnp.where(iq > 0, run_d + cl[:, 0:1], run_d), acc_d + jnp.where(iq > 0, acc_l, 0.0)))
        for u, (iq, q, run, acc) in enumerate(blocks):
            def loop(carry, q=q):
                kb, _, run, acc = carry
                run, part = tile(q, kb, run, None)
                return kb - 1, _sb_alive(run), run, acc + part

            _, _, run, acc = lax.while_loop(_sb_more, loop, (iq - 2, _sb_alive(run), run, acc))
            rows = pl.ds(u * t, t)
            o_ref[rows, :] = acc
            g = bg_ref[rows, :]
            y_ref[rows, :] = (acc * (g * _sigmoid(g))).astype(BF16)

    blk = lambda off: pl.BlockSpec((sub * t, dh), functools.partial(lambda h, i, off: (i, h + off), off=off))
    col = lambda off: pl.BlockSpec((s, dh), functools.partial(lambda h, i, off: (0, h + off), off=off))
    return pl.pallas_call(
        body, name="sb_fwd", grid=(SB_HEADS, nq // sub),
        in_specs=[blk(0), col(SB_HEADS), col(2 * SB_HEADS), blk(0)],
        out_specs=[blk(0), blk(0)],
        out_shape=[jax.ShapeDtypeStruct((s, SB_HEADS * dh), F32), jax.ShapeDtypeStruct((s, SB_HEADS * dh), BF16)],
        compiler_params=_cparams(("parallel", "arbitrary")),
    )(qkv, qkv, qkv, bg)


def _sb_bwd(qkv, dob):
    s = qkv.shape[0]
    t = min(SB_T, s)
    nq = s // t
    dh = SB_HEAD_DIM
    scale = 1.0 / math.sqrt(dh)

    sub = SB_SUB if nq % SB_SUB == 0 else 1

    def body(q_ref, k_ref, v_ref, do_ref, dq_ref, dk_out, dv_out, e_ref, b_ref, pe_ref, pb_ref, dk_ref, dv_ref):
        @pl.when(pl.program_id(1) == 0)
        def _():
            dk_ref[...] = jnp.zeros_like(dk_ref)
            dv_ref[...] = jnp.zeros_like(dv_ref)

        row = lax.broadcasted_iota(jnp.int32, (t, t), 0)
        col = lax.broadcasted_iota(jnp.int32, (t, t), 1)
        tri = jnp.where(row >= col, 1.0, 0.0).astype(BF16)
        tri_x = jnp.where(row < col, 1.0, 0.0).astype(BF16)
        causal = col < row

        def sweep1(q, do, kb, e_dst, b_dst, run, mask, live=None):
            off = pl.multiple_of(kb * t, t)
            v = v_ref[pl.ds(off, t), :]
            z, wgt, run = _sb_tile(q, k_ref[pl.ds(off, t), :], scale, tri, run, mask)
            if live is not None:
                wgt = jnp.where(live, wgt, 0.0)
            e_dst[...] = wgt * _dot(do, v, NT)
            b_dst[...] = _sigmoid(z)
            dv_ref[pl.ds(off, t), :] += _dot(wgt.astype(BF16), do, TN)
            return run

        def sweep2(q, kb, e_src, b_src, pre, mask, live=None):
            off = pl.multiple_of(kb * t, t)
            e = e_src[...]
            beta = b_src[...]
            ps = _dot(_split16(e), tri_x, NN)
            ps = ps[:t] + ps[t:]
            dz = e * (1.0 - beta) - beta * (ps + pre)
            if mask is not None:
                dz = jnp.where(mask, dz, 0.0)
            if live is not None:
                dz = jnp.where(live, dz, 0.0)
            dz16 = (dz * scale).astype(BF16)
            dk_ref[pl.ds(off, t), :] += _dot(dz16, q, TN)
            return pre + ps[:, t - 1:t] + e[:, t - 1:t], _dot(dz16, k_ref[pl.ds(off, t), :], NN)

        blocks = []
        for u in range(sub):
            iq = pl.program_id(1) * sub + u
            q, do = q_ref[pl.ds(u * t, t), :], do_ref[pl.ds(u * t, t), :]
            left = jnp.maximum(iq - 1, 0)
            run_d = sweep1(q, do, iq, pe_ref.at[u, 0], pb_ref.at[u, 0], jnp.zeros((t, 1), F32), causal)
            run_l = sweep1(q, do, left, pe_ref.at[u, 1], pb_ref.at[u, 1], run_d, None, iq > 0)
            blocks.append((iq, q, do, left, jnp.where(iq > 0, run_l, run_d)))
        carried = []
        for iq, q, do, left, run in blocks:
            def loop1(carry, q=q, do=do):
                run = sweep1(q, do, carry[0], e_ref.at[carry[0]], b_ref.at[carry[0]], carry[2], None)
                return carry[0] - 1, _sb_alive(run), run

            first = lax.while_loop(_sb_more, loop1, (iq - 2, _sb_alive(run), run))[0] + 1

            def loop2(kb, carry, q=q):
                pre, part = sweep2(q, kb, e_ref.at[kb], b_ref.at[kb], carry[0], None)
                return pre, carry[1] + part

            carried.append(lax.fori_loop(first, iq - 1, loop2, (jnp.zeros((t, 1), F32), jnp.zeros((t, dh), F32))))
        for u, ((iq, q, do, left, _), (pre, dq)) in enumerate(zip(blocks, carried)):
            pre, dq_l = sweep2(q, left, pe_ref.at[u, 1], pb_ref.at[u, 1], pre, None, iq > 0)
            _, dq_d = sweep2(q, iq, pe_ref.at[u, 0], pb_ref.at[u, 0], pre, causal)
            dq_ref[pl.ds(u * t, t), :] = (dq + dq_l + dq_d).astype(BF16)

        @pl.when(pl.program_id(1) == nq // sub - 1)
        def _():
            dk_out[...] = dk_ref[...].astype(BF16)
            dv_out[...] = dv_ref[...].astype(BF16)

    blk = lambda off: pl.BlockSpec((sub * t, dh), functools.partial(lambda h, i, off: (i, h + off), off=off))
    col = lambda off: pl.BlockSpec((s, dh), functools.partial(lambda h, i, off: (0, h + off), off=off))
    wide = SB_HEADS * dh
    return pl.pallas_call(
        body, name="sb_bwd", grid=(SB_HEADS, nq // sub),
        in_specs=[blk(0), col(SB_HEADS), col(2 * SB_HEADS), blk(0)],
        out_specs=[blk(0), col(0), col(0)],
        out_shape=[jax.ShapeDtypeStruct((s, wide), BF16)] * 3,
        scratch_shapes=[pltpu.VMEM((nq, t, t), F32), pltpu.VMEM((nq, t, t), F32),
                        pltpu.VMEM((sub, 2, t, t), F32), pltpu.VMEM((sub, 2, t, t), F32),
                        pltpu.VMEM((s, dh), F32), pltpu.VMEM((s, dh), F32)],
        compiler_params=_cparams(("parallel", "arbitrary")),
    )(qkv, qkv, qkv, dob)


def _alibi_slope(h):
    return float(2.0 ** (-8.0 * (h + 1) / C_HEADS))


GROUP_ROWS = C_GROUP * WINDOW


def _swa_window(n):
    qb = WINDOW
    i = lax.broadcasted_iota(jnp.int32, (qb, 2 * qb), 0)
    j = lax.broadcasted_iota(jnp.int32, (qb, 2 * qb), 1)
    d = i - j + qb
    valid = (d >= 0) & (d < WINDOW) & ((j >= qb) | (n > 0))
    stacked = lambda x: jnp.concatenate([x] * C_GROUP, axis=0)
    return stacked(d.astype(F32)), stacked(jnp.where(valid, 0.0, -1e30))


def _swa_group_cols(c, sink_ref):
    head = lax.shift_right_logical(lax.broadcasted_iota(jnp.int32, (GROUP_ROWS, 1), 0), WINDOW.bit_length() - 1)
    slope = jnp.zeros((GROUP_ROWS, 1), F32)
    sink = jnp.zeros((GROUP_ROWS, 1), F32)
    for hh in range(C_GROUP):
        slope = jnp.where(head == hh, _alibi_slope(c * C_GROUP + hh), slope)
        sink = jnp.where(head == hh, sink_ref[c * C_GROUP + hh], sink)
    return slope, sink


def _swa_probs(qg, kw, sink, slope, dist, bias, scale):
    sc = _dot(qg, kw, NT) * scale - slope * dist + bias
    m = jnp.maximum(jnp.max(sc, axis=-1, keepdims=True), sink)
    p = jnp.exp(sc - m)
    ps = jnp.exp(sink - m)
    inv = 1.0 / (jnp.sum(p, axis=-1, keepdims=True) + ps)
    return p * inv, ps * inv


def _stack_heads(x, c):
    hd = C_HEAD_DIM
    return jnp.concatenate([x[:, (c * C_GROUP + hh) * hd:(c * C_GROUP + hh + 1) * hd] for hh in range(C_GROUP)], axis=0)


SWA_SUB = 2


def _swa_kv_specs(sub, kvw):
    return [pl.BlockSpec((WINDOW, 2 * kvw), functools.partial(lambda n, w: (n * sub + w, 0), w=w)) for w in range(sub + 1)]


def _swa_fwd(q16, gate, kvp, sinks):
    s = q16.shape[0]
    qb = WINDOW
    hd = C_HEAD_DIM
    scale = 1.0 / math.sqrt(hd)
    kvw = C_KV_HEADS * hd
    sub = SWA_SUB if (s // qb) % SWA_SUB == 0 else 1

    def body(sink_ref, q_ref, g_ref, *refs):
        kv_refs, (o_ref, y_ref) = refs[:sub + 1], refs[sub + 1:]
        for u in range(sub):
            rows = slice(u * qb, (u + 1) * qb)
            q = q_ref[rows, :]
            kv = jnp.concatenate([kv_refs[u][...], kv_refs[u + 1][...]], axis=0)
            dist, bias = _swa_window(pl.program_id(0) * sub + u)
            for c in range(C_KV_HEADS):
                slope, sink = _swa_group_cols(c, sink_ref)
                kw = kv[:, c * hd:(c + 1) * hd]
                vw = kv[:, kvw + c * hd:kvw + (c + 1) * hd]
                p, _ = _swa_probs(_stack_heads(q, c), kw, sink, slope, dist, bias, scale)
                og = _dot(p.astype(BF16), vw, NN)
                for hh in range(C_GROUP):
                    h = c * C_GROUP + hh
                    o_ref[rows, h * hd:(h + 1) * hd] = og[hh * qb:(hh + 1) * qb, :]
        g = g_ref[...]
        y_ref[...] = (o_ref[...] * (g * _sigmoid(g))).astype(BF16)

    wide = C_HEADS * hd
    big = pl.BlockSpec((sub * qb, wide), lambda n: (n, 0))
    return pl.pallas_call(
        body, name="swa_fwd", grid=(s // (sub * qb),),
        in_specs=[pl.BlockSpec(memory_space=pltpu.SMEM), big, big] + _swa_kv_specs(sub, kvw),
        out_specs=[big, big],
        out_shape=[jax.ShapeDtypeStruct((s, wide), F32), jax.ShapeDtypeStruct((s, wide), BF16)],
        compiler_params=_cparams(("arbitrary",)),
    )(sinks, q16, gate, *([kvp] * (sub + 1)))


def _swa_bwd(q16, do16, kvp, sinks):
    s = q16.shape[0]
    qb = WINDOW
    hd = C_HEAD_DIM
    scale = 1.0 / math.sqrt(hd)
    kvw = C_KV_HEADS * hd
    sub = 1
    nstep = s // (sub * qb)

    def body(sink_ref, q_ref, do_ref, *refs):
        kv_refs, (dq_ref, dkv_ref, ds_ref, sacc_ref) = refs[:sub + 1], refs[sub + 1:]

        @pl.when(pl.program_id(0) == 0)
        def _():
            dkv_ref[...] = jnp.zeros_like(dkv_ref)
            sacc_ref[...] = jnp.zeros_like(sacc_ref)

        for u in range(sub):
            rows = slice(u * qb, (u + 1) * qb)
            n = pl.program_id(0) * sub + u
            q = q_ref[rows, :]
            do = do_ref[rows, :]
            kv = jnp.concatenate([kv_refs[u][...], kv_refs[u + 1][...]], axis=0)
            off = pl.multiple_of(n * qb, qb)
            dist, bias = _swa_window(n)
            for c in range(C_KV_HEADS):
                slope, sink = _swa_group_cols(c, sink_ref)
                kw = kv[:, c * hd:(c + 1) * hd]
                vw = kv[:, kvw + c * hd:kvw + (c + 1) * hd]
                qg = _stack_heads(q, c)
                dog = _stack_heads(do, c)
                p, ps = _swa_probs(qg, kw, sink, slope, dist, bias, scale)
                dp = _dot(dog, vw, NT)
                dd = jnp.sum(p * dp, axis=-1, keepdims=True)
                ds16 = (p * (dp - dd) * scale).astype(BF16)
                sacc_ref[c] += ps * dd
                dqg = _dot(ds16, kw, NN).astype(BF16)
                for hh in range(C_GROUP):
                    h = c * C_GROUP + hh
                    dq_ref[rows, h * hd:(h + 1) * hd] = dqg[hh * qb:(hh + 1) * qb, :]
                dkv_ref[pl.ds(off, 2 * qb), c * hd:(c + 1) * hd] += _dot(ds16, qg, TN)
                dkv_ref[pl.ds(off, 2 * qb), kvw + c * hd:kvw + (c + 1) * hd] += _dot(p.astype(BF16), dog, TN)

        @pl.when(pl.program_id(0) == nstep - 1)
        def _():
            lane = lax.broadcasted_iota(jnp.int32, (1, 128), 1)
            row = jnp.zeros((1, 128), F32)
            for c in range(C_KV_HEADS):
                for hh in range(C_GROUP):
                    tot = jnp.sum(sacc_ref[c, pl.ds(hh * qb, qb), :], axis=0, keepdims=True)
                    row = jnp.where(lane == c * C_GROUP + hh, -tot, row)
            ds_ref[...] = row

    wide = C_HEADS * hd
    big = pl.BlockSpec((sub * qb, wide), lambda n: (n, 0))
    return pl.pallas_call(
        body, name="swa_bwd", grid=(nstep,),
        in_specs=[pl.BlockSpec(memory_space=pltpu.SMEM), big, big] + _swa_kv_specs(sub, kvw),
        out_specs=[big,
                   pl.BlockSpec((s + qb, 2 * kvw), lambda n: (0, 0)), pl.BlockSpec((1, 128), lambda n: (0, 0))],
        out_shape=[jax.ShapeDtypeStruct((s, wide), BF16), jax.ShapeDtypeStruct((s + qb, 2 * kvw), F32),
                   jax.ShapeDtypeStruct((1, 128), F32)],
        scratch_shapes=[pltpu.VMEM((C_KV_HEADS, GROUP_ROWS, 1), F32)],
        compiler_params=_cparams(("arbitrary",)),
    )(sinks, q16, do16, *([kvp] * (sub + 1)))


def _adamw(parts, w, m, v, *, name, tr=496):
    npart, r, c = parts.shape
    tr = min(tr, r)
    assert r % tr == 0 and tr % 16 == 0
    c1 = 1.0 / (1.0 - ADAM_B1 ** ADAM_STEP)
    c2 = 1.0 / (1.0 - ADAM_B2 ** ADAM_STEP)

    def body(p_ref, w_ref, m_ref, v_ref, g_ref, d_ref, nm_ref, nv_ref):
        g = p_ref[0].astype(F32)
        for j in range(1, npart):
            g = g + p_ref[j].astype(F32)
        nm = ADAM_B1 * m_ref[...] + (1.0 - ADAM_B1) * g
        nv = ADAM_B2 * v_ref[...] + (1.0 - ADAM_B2) * (g * g)
        g_ref[...] = g
        nm_ref[...] = nm
        nv_ref[...] = nv
        d_ref[...] = -ADAM_LR * ((nm * c1) / (jnp.sqrt(nv * c2) + ADAM_EPS) + ADAM_WD * w_ref[...])

    spec = pl.BlockSpec((tr, c), lambda i: (i, 0))
    return pl.pallas_call(
        body, name=name, grid=(r // tr,),
        in_specs=[pl.BlockSpec((npart, tr, c), lambda i: (0, i, 0)), spec, spec, spec],
        out_specs=[spec] * 4, out_shape=[jax.ShapeDtypeStruct((r, c), F32)] * 4,
        compiler_params=_cparams(("parallel",)),
    )(parts, w, m, v)


GATHER = "gather"
A2A = "a2a"
GATHER_COLS = "gather_cols"
A2A_COLS = "a2a_cols"


def _gather_two_level(bufs, modes, x, *, tr=1024, name):
    nb = len(bufs)
    rows, cols = x.shape
    tr = min(tr, rows)
    steps = rows // tr

    def body(*refs):
        x_ref, src, x16_ref, dst = refs[0], refs[1:1 + nb], refs[1 + nb], refs[2 + nb:2 + 2 * nb]
        send_sems, recv_sems, local_sems = refs[2 + 2 * nb:]
        cx, cy, cc = lax.axis_index("x"), lax.axis_index("y"), lax.axis_index("c")
        me = 4 * cx + 2 * cy + cc
        here, sibling = (cx, cy, cc), (cx, cy, 1 - cc)

        def copy(b, k, origin, to, source):
            return pltpu.make_async_remote_copy(
                src_ref=source, dst_ref=_slot(dst[b], modes[b], origin), send_sem=send_sems.at[b, k - 1],
                recv_sem=recv_sems.at[b, k - 1], device_id=to, device_id_type=pl.DeviceIdType.MESH)

        def first_copies():
            local = [pltpu.make_async_copy(src[b], _slot(dst[b], modes[b], me), local_sems.at[b]) for b in range(nb)]
            sends = [copy(b, k, me, (cx ^ (k >> 2), cy ^ ((k >> 1) & 1), cc ^ (k & 1)), src[b])
                     for k in (1, 2, 4, 6) for b in range(nb)]
            return local, sends

        @pl.when(pl.program_id(0) == 0)
        def _():
            local, sends = first_copies()
            for cp in local + sends:
                cp.start()

        x16_ref[...] = x_ref[...].astype(BF16)

        @pl.when(pl.program_id(0) == steps - 1)
        def _():
            local, sends = first_copies()
            for j in (2, 4, 6):
                for b in range(nb):
                    copy(b, j, me ^ j, here, src[b]).wait_recv()
                    passed = copy(b, j ^ 1, me ^ j, sibling, _slot(dst[b], modes[b], me ^ j))
                    passed.start()
                    sends.append(passed)
            for k in (1, 3, 5, 7):
                for b in range(nb):
                    copy(b, k, me ^ k, here, src[b]).wait_recv()
            for cp in sends:
                cp.wait_send()
            for cp in local:
                cp.wait()

    hbm = pl.BlockSpec(memory_space=pl.ANY)
    tile = pl.BlockSpec((tr, cols), lambda i: (i, 0))
    return pl.pallas_call(
        body, name=name, grid=(steps,), in_specs=[tile] + [hbm] * nb, out_specs=[tile] + [hbm] * nb,
        out_shape=[jax.ShapeDtypeStruct(x.shape, BF16)] + _exchange_out_shapes(bufs, modes),
        scratch_shapes=_exchange_sems(nb),
        compiler_params=pltpu.CompilerParams(dimension_semantics=("arbitrary",), vmem_limit_bytes=VMEM_LIMIT,
                                             has_side_effects=True),
    )(x, *bufs)


def _exchange_out_shapes(bufs, gather):
    shapes = {GATHER: lambda s: (N_DEV,) + s, A2A: lambda s: s, GATHER_COLS: lambda s: (s[0], N_DEV * s[1]),
              A2A_COLS: lambda s: (N_DEV, s[0], s[1] // N_DEV)}
    return [jax.ShapeDtypeStruct(shapes[g](tuple(b.shape)), b.dtype) for b, g in zip(bufs, gather)]


def _sent(src, mode, peer):
    if mode == A2A:
        return src.at[peer]
    if mode == A2A_COLS:
        w = src.shape[1] // N_DEV
        return src.at[:, pl.ds(pl.multiple_of(peer * w, 128), w)]
    return src


def _slot(dst, mode, dev):
    if mode == GATHER_COLS:
        w = dst.shape[1] // N_DEV
        return dst.at[:, pl.ds(pl.multiple_of(dev * w, 128), w)]
    return dst.at[dev]


def _exchange_sems(nb):
    return [pltpu.SemaphoreType.DMA((nb, N_DEV - 1)), pltpu.SemaphoreType.DMA((nb, N_DEV - 1)),
            pltpu.SemaphoreType.DMA((nb,))]


def _exchange_copies(src, dst, gather, sems):
    send_sems, recv_sems, local_sems = sems
    x, y, c = lax.axis_index("x"), lax.axis_index("y"), lax.axis_index("c")
    me = 4 * x + 2 * y + c
    local, sends, recvs = [], [], []
    for b in range(len(src)):
        mine = _sent(src[b], gather[b], me)
        local.append(pltpu.make_async_copy(mine, _slot(dst[b], gather[b], me), local_sems.at[b]))
        for k in range(1, N_DEV):
            px, py, pc = x ^ (k >> 2), y ^ ((k >> 1) & 1), c ^ (k & 1)
            peer = 4 * px + 2 * py + pc
            pair = dict(send_sem=send_sems.at[b, k - 1], recv_sem=recv_sems.at[b, k - 1],
                        device_id_type=pl.DeviceIdType.MESH)
            sends.append(pltpu.make_async_remote_copy(
                src_ref=_sent(src[b], gather[b], peer), dst_ref=_slot(dst[b], gather[b], me), device_id=(px, py, pc),
                **pair))
            recvs.append(pltpu.make_async_remote_copy(
                src_ref=mine, dst_ref=_slot(dst[b], gather[b], peer), device_id=(x, y, c), **pair))
    return local, sends, recvs


def _exchange_start(src, dst, gather, sems):
    local, sends, _ = _exchange_copies(src, dst, gather, sems)
    for cp in local + sends:
        cp.start()


def _exchange_wait(src, dst, gather, sems):
    local, sends, recvs = _exchange_copies(src, dst, gather, sems)
    for cp in sends:
        cp.wait_send()
    for cp in recvs:
        cp.wait_recv()
    for cp in local:
        cp.wait()


R_SMALL = 16
REPL_ROWS = 272
REPL_LEN = REPL_ROWS * D_MODEL


def _small_block(parts):
    flat = jnp.concatenate(parts, axis=-1)
    lead = flat.ndim - 1
    return jnp.pad(flat[..., None, :], [(0, 0)] * lead + [(0, R_SMALL - 1), (0, D_MODEL - flat.shape[-1])])


def _small_vectors(conv_w, ln_g, ln_b):
    return _small_block([conv_w.reshape(512), ln_g.reshape(128), ln_b.reshape(128)])


REPL_SHAPES = ((1, 8, 128, 128), (1, 8, 128, 128), (1, 1024), (1, 1024), (1, 1024), (1, 1024), (1, 1024), (1, 1024),
               (1, 16))


def _pack_repl(parts):
    flat = jnp.concatenate([p.reshape(-1) for p in parts])
    return jnp.concatenate([flat, jnp.zeros((REPL_LEN - flat.shape[0],), F32)]).reshape(REPL_ROWS, D_MODEL)


def _unpack_repl(p):
    flat = p.reshape(-1)
    out, o = [], 0
    for shp in REPL_SHAPES:
        n = math.prod(shp)
        out.append(flat[o:o + n].reshape(shp))
        o += n
    return out


def kernel(x, e_w_in, e_conv_w, e_conv_b, e_w_gate_a, e_b_gate_a, e_w_gate_x, e_b_gate_x, e_lru_lambda, e_w_out, e_ln_g, e_ln_b, o_w_in, o_sinks, o_w_out, o_ln_g, o_ln_b, loss_target, m_e_w_in, m_e_conv_w, m_e_conv_b, m_e_w_gate_a, m_e_b_gate_a, m_e_w_gate_x, m_e_b_gate_x, m_e_lru_lambda, m_e_w_out, m_e_ln_g, m_e_ln_b, m_o_w_in, m_o_sinks, m_o_w_out, m_o_ln_g, m_o_ln_b, v_e_w_in, v_e_conv_w, v_e_conv_b, v_e_w_gate_a, v_e_b_gate_a, v_e_w_gate_x, v_e_b_gate_x, v_e_lru_lambda, v_e_w_out, v_e_ln_g, v_e_ln_b, v_o_w_in, v_o_sinks, v_o_w_out, v_o_ln_g, v_o_ln_b):
    d = D_MODEL
    x0 = x[0]
    target = loss_target[0]
    s = x0.shape[0]

    spack = _small_block([e_conv_w.reshape(512), o_ln_g.reshape(128), o_ln_b.reshape(128)])
    x0_16, wi_e, sall = _gather_two_level([e_w_in[0].astype(BF16), spack], [GATHER_COLS, GATHER], x0,
                                          name="gather_w_in")
    conv_w = sall[:, 0, 0:512].reshape(N_DEV, 4, 128).transpose(1, 0, 2).reshape(4, d)
    ln_g_o = sall[:, 0, 512:640].reshape(1, d)
    ln_b_o = sall[:, 0, 640:768].reshape(1, d)
    pvec = jnp.concatenate([conv_w, e_conv_b, e_b_gate_a, e_b_gate_x, e_lru_lambda], axis=0)
    wa16 = e_w_gate_a[0].astype(BF16)
    wx16 = e_w_gate_x[0].astype(BF16)
    sinks = o_sinks[0]

    h0a, wall_out_e = _matmul(x0_16, wi_e, mode="nn", n_out=2 * d, b_off=0, name="l0_in_a",
                              comm=([e_w_out[0].astype(BF16)], [GATHER]))
    qkv, wall_in_o = _matmul(x0_16, wi_e, mode="nn", n_out=3 * d, b_off=2 * d, out_dtype=BF16, name="l0_in_qkv",
                             comm=([o_w_in[0].astype(BF16)], [GATHER]))
    bg, wall_out_o = _matmul(x0_16, wi_e, mode="nn", n_out=d, b_off=5 * d, name="l0_in_bg",
                             comm=([o_w_out[0].astype(BF16)], [GATHER]))
    wo_e = wall_out_e.reshape(2 * d, d)
    wi_o = wall_in_o.transpose(1, 0, 2).reshape(d, 2304)
    wi_o = jnp.concatenate([wi_o[:, :1024], wi_o[:, 1280:], wi_o[:, 1024:1280]], axis=1)
    wo_o = wall_out_o.reshape(d, d)
    ya, hst = _rglru_fwd(h0a, pvec, wa16, wx16)
    ob, yb = _sb_fwd(qkv, bg)
    z0, x1, x1_16 = _matmul([ya, yb], wo_e, mode="nn", tm=512, name="l0_out",
                            epi=(_epi_ln_fwd, [(x0, 0)], [e_ln_g, e_ln_b], [F32, F32, BF16], 0))

    q1 = _matmul(x1_16, wi_o, mode="nn", n_out=d, b_off=0, out_dtype=BF16, name="l1_in_q")
    g1 = _matmul(x1_16, wi_o, mode="nn", n_out=d, b_off=d, name="l1_in_g")
    kv1 = _matmul(x1_16, wi_o, mode="nn", n_out=256, b_off=2 * d, tn=256, out_dtype=BF16, name="l1_in_kv")
    kvp = jnp.concatenate([jnp.zeros((WINDOW, 256), BF16), kv1], axis=0)
    o1, y1 = _swa_fwd(q1, g1, kvp, sinks)
    dz1, dz1_16, loss_cols, dg_o, db_o = _matmul(
        y1, wo_o, mode="nn", tm=512, name="l1_out",
        epi=(_epi_ln_loss_bwd, [(x1, 0), (target, 0)], [ln_g_o, ln_b_o], [F32, BF16], 3))
    loss_hi = jnp.sum(loss_cols).astype(BF16).astype(F32)
    loss_terms = jnp.stack([loss_hi, jnp.sum(loss_cols) - loss_hi]).reshape(1, 2)

    dwo_o = _matmul(y1, dz1_16, mode="tn", out_dtype=BF16, name="l1_dwout")
    do1, dg1, parts_out_o = _matmul(dz1_16, wo_o, mode="nt", tm=512, name="l1_dy",
                                    comm=([dwo_o.reshape(N_DEV, 128, d)], [A2A]),
                                    epi=(_epi_gate_bwd, [(o1, 0), (g1, 0)], [], [BF16, BF16], 0))
    dq1, dkvp, dsink = _swa_bwd(q1, do1, kvp, sinks)
    dkv1 = dkvp[WINDOW:].astype(BF16)
    dh1 = [dq1, dg1, dkv1]
    dw_qg = _matmul(x1_16, [dq1, dg1], mode="tn", out_dtype=BF16, name="l1_dwin_qg")
    dw_kv = _matmul(x1_16, dkv1, mode="tn", out_dtype=BF16, name="l1_dwin_kv")
    dwi_o = jnp.concatenate([dw_qg[:, :1024], dw_kv, dw_qg[:, 1024:]], axis=1)
    dz0, dz0_16, dg_e, db_e, parts_in_o = _matmul(
        dh1, wi_o, mode="nt", tm=512, tk=2304, name="l1_dx",
        comm=([dwi_o.reshape(d, N_DEV, 288).transpose(1, 0, 2)], [A2A]),
        epi=(_epi_ln_bwd, [(dz1, 0), (z0, 0)], [e_ln_g], [F32, BF16], 2))

    dwo_e = _matmul([ya, yb], dz0_16, mode="tn", out_dtype=BF16, name="l0_dwout")
    dhst, dag, parts_out_e = _matmul(
        dz0_16, wo_e, mode="nt", tm=512, n_out=d, b_off=0, name="l0_dy_a",
        comm=([dwo_e.reshape(N_DEV, 256, d)], [A2A]),
        epi=(_epi_gate_bwd, [(hst, 0), (h0a, d)], [], [F32, BF16], 0))
    dob, dbg = _matmul(dz0_16, wo_e, mode="nt", tm=512, n_out=d, b_off=d, name="l0_dy_b",
                       epi=(_epi_gate_bwd, [(ob, 0), (bg, 0)], [], [BF16, BF16], 0))
    dq0, dk0, dv0 = _sb_bwd(qkv, dob)
    dax, dwa, dwx, dpv = _rglru_bwd(h0a, hst, dhst, pvec, wa16, wx16)
    dh0 = [dax, dag, dq0, dk0, dv0, dbg]
    repl = _pack_repl([dwa, dwx, dpv[4:5], dpv[5:6], dpv[6:7], dpv[7:8], dg_e, db_e, dsink[:, :C_HEADS], loss_terms])
    dwi_e, parts_repl = _matmul(x0_16, dh0, mode="tn", out_dtype=BF16, name="l0_dwin",
                                comm=([repl.astype(BF16)], [GATHER]))
    small = _small_block([dpv[0:4].reshape(4, N_DEV, 128).transpose(1, 0, 2).reshape(N_DEV, 512),
                          dg_o.reshape(N_DEV, 128), db_o.reshape(N_DEV, 128)]).astype(BF16)
    grad_x, parts_in_e, parts_small = _matmul(dh0, wi_e, mode="nt", tm=512, res=dz0, alpha=ALPHA, name="l0_dx",
                                              comm=([dwi_e, small], [A2A_COLS, A2A]))

    res_in_e = _adamw(parts_in_e, e_w_in[0], m_e_w_in[0], v_e_w_in[0], tr=512, name="adamw_w_in_e")
    res_out_e = _adamw(parts_out_e, e_w_out[0], m_e_w_out[0], v_e_w_out[0], name="adamw_w_out_e")
    res_in_o = _adamw(parts_in_o, o_w_in[0], m_o_w_in[0], v_o_w_in[0], tr=512, name="adamw_w_in_o")
    res_out_o = _adamw(parts_out_o, o_w_out[0], m_o_w_out[0], v_o_w_out[0], name="adamw_w_out_o")
    res_small = _adamw(parts_small, _small_vectors(e_conv_w, o_ln_g, o_ln_b),
                       _small_vectors(m_e_conv_w, m_o_ln_g, m_o_ln_b),
                       _small_vectors(v_e_conv_w, v_o_ln_g, v_o_ln_b), name="adamw_vectors")

    w_r = _pack_repl([e_w_gate_a, e_w_gate_x, e_conv_b, e_b_gate_a, e_b_gate_x, e_lru_lambda, e_ln_g, e_ln_b, o_sinks])
    m_r = _pack_repl([m_e_w_gate_a, m_e_w_gate_x, m_e_conv_b, m_e_b_gate_a, m_e_b_gate_x, m_e_lru_lambda, m_e_ln_g,
                      m_e_ln_b, m_o_sinks])
    v_r = _pack_repl([v_e_w_gate_a, v_e_w_gate_x, v_e_conv_b, v_e_b_gate_a, v_e_b_gate_x, v_e_lru_lambda, v_e_ln_g,
                      v_e_ln_b, v_o_sinks])
    g_r, d_r, nm_r, nv_r = _adamw(parts_repl, w_r, m_r, v_r, name="adamw_replicated")
    loss_at = sum(math.prod(shp) for shp in REPL_SHAPES)
    loss = g_r[loss_at // d, loss_at % d] + g_r[(loss_at + 1) // d, (loss_at + 1) % d]

    def assemble(i, rp):
        vec = res_small[i][0]
        cw, lg_o, lb_o = vec[0:512].reshape(1, 4, 128), vec[512:640].reshape(1, 128), vec[640:768].reshape(1, 128)
        w_a, w_x, cb, b_a, b_x, lam, lg_e, lb_e, snk = _unpack_repl(rp)
        return [res_in_e[i][None], cw, cb, w_a, b_a, w_x, b_x, lam, res_out_e[i][None], lg_e, lb_e,
                res_in_o[i][None], snk, res_out_o[i][None], lg_o, lb_o]

    return (loss, grad_x[None], *assemble(0, g_r), *assemble(1, d_r), *assemble(2, nm_r), *assemble(3, nv_r))
```

```python
import functools
import math

import jax
import jax.numpy as jnp
from jax import lax
from jax.experimental import pallas as pl
from jax.experimental.pallas import tpu as pltpu

F32 = jnp.float32
BF16 = jnp.bfloat16

N_DEV = 8
D_MODEL = 1024
LRU_BLOCKS = 8
LRU_BLOCK = 128
LRU_C = 8.0
SB_HEADS = 8
SB_HEAD_DIM = 128
C_HEADS = 16
C_KV_HEADS = 2
C_GROUP = 8
C_HEAD_DIM = 64
WINDOW = 128
DEPTH = 2
ALPHA = float((2 * DEPTH) ** 0.25)
LN_EPS = 1e-5
ADAM_LR = 0.001
ADAM_B1 = 0.9
ADAM_B2 = 0.999
ADAM_EPS = 1e-08
ADAM_WD = 0.01
ADAM_STEP = 10

VMEM_LIMIT = 56 * 1024 * 1024

NN = ((1,), (0,))
NT = ((1,), (1,))
TN = ((0,), (0,))


def _dot(a, b, dims):
    return lax.dot_general(a, b, (dims, ((), ())), preferred_element_type=F32)


def _sigmoid(x):
    return 1.0 / (1.0 + jnp.exp(-x))


def _cparams(sem, vmem=VMEM_LIMIT):
    return pltpu.CompilerParams(dimension_semantics=sem, vmem_limit_bytes=vmem)


def _matmul(a, b, *, mode, n_out=None, b_off=0, out_dtype=F32, res=None, alpha=1.0,
            tm=1024, tn=1024, tk=1024, comm=None, epi=None, name):
    a_list = list(a) if isinstance(a, (list, tuple)) else [a]
    b_list = list(b) if isinstance(b, (list, tuple)) else [b]
    if mode == "tn":
        k = a_list[0].shape[0]
        m = sum(p.shape[1] for p in a_list)
        n = n_out if n_out is not None else sum(p.shape[1] for p in b_list)
        assert all(p.shape[1] == tm for p in a_list) or len(a_list) == 1
        assert all(p.shape[1] == tn for p in b_list) or len(b_list) == 1
    else:
        assert len(b_list) == 1
        m = a_list[0].shape[0]
        k = sum(p.shape[1] for p in a_list)
        n = n_out if n_out is not None else (b.shape[0] if mode == "nt" else b.shape[1])
        if len(a_list) > 1:
            tk = k
    tm, tn, tk = min(tm, m), min(tn, n), min(tk, k)
    assert m % tm == 0 and n % tn == 0 and k % tk == 0 and b_off % tn == 0
    grid = (m // tm, n // tn, k // tk)
    nk = grid[2]
    jo = b_off // tn
    dims = {"nn": NN, "nt": NT, "tn": TN}[mode]
    nc = len(comm[0]) if comm else 0
    if epi:
        assert res is None
        epi_fn, epi_rows, epi_vecs, epi_dtypes, n_sums = epi
        assert n_sums == 0 or grid[1] == 1
    else:
        epi_rows, epi_vecs, epi_dtypes, n_sums = [], [], [out_dtype], 0
    n_ab = len(a_list) + len(b_list)
    n_in = n_ab + (res is not None) + len(epi_rows) + len(epi_vecs)
    n_res = len(epi_dtypes) + n_sums

    def body(*refs):
        a_refs, b_refs = refs[:len(a_list)], refs[len(a_list):n_ab]
        r_ref = refs[n_ab] if res is not None else None
        row_refs = refs[n_in - len(epi_rows) - len(epi_vecs):n_in - len(epi_vecs)]
        vec_refs = refs[n_in - len(epi_vecs):n_in]
        o_refs = refs[n_in + nc:n_in + nc + len(epi_dtypes)]
        sum_refs = refs[n_in + nc + len(epi_dtypes):n_in + nc + n_res]
        if comm:
            step = [pl.program_id(ax) for ax in range(3)]
            c_src = refs[n_in:n_in + nc]
            c_dst = refs[n_in + nc + n_res:n_in + 2 * nc + n_res]
            sems = refs[len(refs) - 3:]

            @pl.when((step[0] == 0) & (step[1] == 0) & (step[2] == 0))
            def _():
                _exchange_start(c_src, c_dst, comm[1], sems)

        def finish(out):
            if not epi:
                if r_ref is not None:
                    out = out + alpha * r_ref[...]
                o_refs[0][...] = out.astype(out_dtype)
                return
            outs, sums = epi_fn(out, [r[...] for r in row_refs], [v[...] for v in vec_refs])
            for o_ref, o in zip(o_refs, outs):
                o_ref[...] = o.astype(o_ref.dtype)
            if sum_refs:
                first_rows = pl.program_id(0) == 0

                @pl.when(first_rows)
                def _():
                    for s_ref, part_sum in zip(sum_refs, sums):
                        s_ref[...] = part_sum

                @pl.when(jnp.logical_not(first_rows))
                def _():
                    for s_ref, part_sum in zip(sum_refs, sums):
                        s_ref[...] += part_sum

        def accumulate(part):
            if nk == 1:
                finish(part)
                return
            acc_ref = refs[n_in + 2 * nc + n_res]
            kk = pl.program_id(2)

            @pl.when(kk == 0)
            def _():
                acc_ref[...] = part

            @pl.when(kk > 0)
            def _():
                acc_ref[...] += part

            @pl.when(kk == nk - 1)
            def _():
                finish(acc_ref[...])

        def product(a_ref, b_val):
            return _dot(a_ref[...].astype(BF16), b_val.astype(BF16), dims)

        if mode != "tn" and len(a_list) > 1:
            part, off = None, 0
            for a_ref in a_refs:
                w = a_ref.shape[1]
                b_val = b_refs[0][off:off + w, :] if mode == "nn" else b_refs[0][:, off:off + w]
                part = product(a_ref, b_val) if part is None else part + product(a_ref, b_val)
                off += w
            accumulate(part)
        elif len(a_list) > 1 or len(b_list) > 1:
            for pa, a_ref in enumerate(a_refs):
                for pb, b_ref in enumerate(b_refs):
                    picked = ([pl.program_id(0) == pa] if len(a_list) > 1 else []) + (
                        [pl.program_id(1) == pb] if len(b_list) > 1 else [])
                    pl.when(functools.reduce(jnp.logical_and, picked))(
                        functools.partial(lambda a_ref, b_ref: accumulate(product(a_ref, b_ref[...])), a_ref, b_ref))
        else:
            accumulate(product(a_refs[0], b_refs[0][...]))

        if comm:
            @pl.when((step[0] == grid[0] - 1) & (step[1] == grid[1] - 1) & (step[2] == grid[2] - 1))
            def _():
                _exchange_wait(c_src, c_dst, comm[1], sems)

    def held(axis, p):
        def index(i, j, kk):
            return (jnp.where((i, j)[axis] == p, kk, 0), 0)
        return index

    if mode == "tn":
        if len(a_list) > 1:
            a_specs = [pl.BlockSpec((tk, tm), held(0, p)) for p in range(len(a_list))]
        else:
            a_specs = [pl.BlockSpec((tk, tm), lambda i, j, kk: (kk, i))]
    elif len(a_list) > 1:
        a_specs = [pl.BlockSpec((tm, p.shape[1]), lambda i, j, kk: (i, 0)) for p in a_list]
    else:
        a_specs = [pl.BlockSpec((tm, tk), lambda i, j, kk: (i, kk))]
    if mode == "nt":
        b_specs = [pl.BlockSpec((tn, tk), lambda i, j, kk: (j + jo, kk))]
    elif len(b_list) > 1:
        b_specs = [pl.BlockSpec((tk, tn), held(1, p)) for p in range(len(b_list))]
    else:
        b_specs = [pl.BlockSpec((tk, tn), lambda i, j, kk: (kk, j + jo))]
    o_spec = pl.BlockSpec((tm, tn), lambda i, j, kk: (i, j))
    hbm = pl.BlockSpec(memory_space=pl.ANY)
    row_specs = [pl.BlockSpec((tm, tn), functools.partial(lambda i, j, kk, o: (i, j + o), o=off // tn))
                 for (_, off) in epi_rows]
    vec_specs = [pl.BlockSpec(v.shape, functools.partial(lambda i, j, kk, nd: (0,) * nd, nd=v.ndim)) for v in epi_vecs]
    in_specs = a_specs + b_specs + ([o_spec] if res is not None else []) + row_specs + vec_specs + [hbm] * nc
    args = (a_list + b_list + ([res] if res is not None else []) + [r for (r, _) in epi_rows] + list(epi_vecs)
            + (list(comm[0]) if comm else []))
    out_specs = [o_spec] * len(epi_dtypes) + [pl.BlockSpec((1, tn), lambda i, j, kk: (0, j))] * n_sums + [hbm] * nc
    out_shape = ([jax.ShapeDtypeStruct((m, n), dt) for dt in epi_dtypes] + [jax.ShapeDtypeStruct((1, n), F32)] * n_sums
                 + (_exchange_out_shapes(*comm) if comm else []))
    scratch = ([pltpu.VMEM((tm, tn), F32)] if nk > 1 else []) + (_exchange_sems(nc) if comm else [])
    if comm or n_sums or n_ab > 2:
        params = pltpu.CompilerParams(dimension_semantics=("arbitrary",) * 3, vmem_limit_bytes=VMEM_LIMIT,
                                      has_side_effects=bool(comm))
    else:
        params = _cparams(("parallel", "parallel", "arbitrary"))
    outs = pl.pallas_call(
        body, name=name, grid=grid, in_specs=in_specs, out_specs=out_specs, out_shape=out_shape,
        scratch_shapes=scratch, compiler_params=params,
    )(*args)
    return tuple(outs) if (comm or epi) else outs[0]


def _ln_stats(z):
    mu = jnp.mean(z, axis=-1, keepdims=True)
    zc = z - mu
    var = jnp.mean(zc * zc, axis=-1, keepdims=True)
    rstd = lax.rsqrt(var + LN_EPS)
    return zc * rstd, rstd


def _ln_bwd(dy, xhat, rstd, g):
    dxh = dy * g
    m1 = jnp.mean(dxh, axis=-1, keepdims=True)
    m2 = jnp.mean(dxh * xhat, axis=-1, keepdims=True)
    return rstd * (dxh - m1 - xhat * m2)


def _colsum(x):
    return jnp.sum(x, axis=0, keepdims=True)


def _epi_ln_fwd(acc, rows, vecs):
    z = ALPHA * rows[0] + acc
    xhat, _ = _ln_stats(z)
    x1 = xhat * vecs[0] + vecs[1]
    return [z, x1, x1], []


def _epi_ln_loss_bwd(acc, rows, vecs):
    inv_c = 1.0 / acc.shape[-1]
    z = ALPHA * rows[0] + acc
    xhat, rstd = _ln_stats(z)
    d = xhat * vecs[0] + vecs[1] - rows[1]
    dy = d * inv_c
    dz = _ln_bwd(dy, xhat, rstd, vecs[0])
    return [dz, dz], [_colsum(0.5 * d * d * inv_c), _colsum(dy * xhat), _colsum(dy)]


def _epi_ln_bwd(acc, rows, vecs):
    dy = acc + ALPHA * rows[0]
    xhat, rstd = _ln_stats(rows[1])
    dz = _ln_bwd(dy, xhat, rstd, vecs[0])
    return [dz, dz], [_colsum(dy * xhat), _colsum(dy)]


def _epi_gate_bwd(acc, rows, vecs):
    oo, gg = rows
    sg = _sigmoid(gg)
    return [acc * (gg * sg), acc * oo * (sg * (1.0 + gg * (1.0 - sg)))], []


LRU_T = 256
HALO = 8


def _log1p(y):
    u = 1.0 + y
    return jnp.where(u == 1.0, y, jnp.log(u) * (y / (u - 1.0)))


def _lru_gates(c, wa_ref, wx_ref, pv):
    c16 = c.astype(BF16)
    pre_r = jnp.concatenate(
        [_dot(c16[:, n * LRU_BLOCK:(n + 1) * LRU_BLOCK], wa_ref[n], NN) for n in range(LRU_BLOCKS)], axis=1)
    pre_i = jnp.concatenate(
        [_dot(c16[:, n * LRU_BLOCK:(n + 1) * LRU_BLOCK], wx_ref[n], NN) for n in range(LRU_BLOCKS)], axis=1)
    r = _sigmoid(pre_r + pv[5:6])
    ig = _sigmoid(pre_i + pv[6:7])
    lam = pv[7:8]
    ls = jnp.minimum(lam, 0.0) - _log1p(jnp.exp(-jnp.abs(lam)))
    la = LRU_C * r * ls
    a = jnp.exp(la)
    a2 = a * a
    m = jnp.sqrt(jnp.tanh(-la) * (a2 + 1.0))
    return c16, r, ig, ls, la, a, a2, m


def _conv(ext_ref, x, pv, t):
    return (pv[4:5] + pv[3:4] * x + pv[2:3] * ext_ref[pl.ds(HALO - 1, t), :]
            + pv[1:2] * ext_ref[pl.ds(HALO - 2, t), :] + pv[0:1] * ext_ref[pl.ds(HALO - 3, t), :])


def _rglru_fwd(h0a, pvec, wa16, wx16):
    s = h0a.shape[0]
    w = D_MODEL
    t = min(LRU_T, s)
    assert s % t == 0

    def body(ax_ref, ag_ref, pv_ref, wa_ref, wx_ref, ya_ref, h_ref, ext_ref, a_ref, u_ref, hc_ref):
        i = pl.program_id(0)

        @pl.when(i == 0)
        def _():
            ext_ref[pl.ds(0, HALO), :] = jnp.zeros((HALO, w), F32)
            hc_ref[...] = jnp.zeros((1, w), F32)

        pv = pv_ref[...]
        ax = ax_ref[...]
        ext_ref[pl.ds(HALO, t), :] = ax
        c = _conv(ext_ref, ax, pv, t)
        ext_ref[pl.ds(0, HALO), :] = ax[t - HALO:, :]
        _, _, ig, _, _, a, _, m = _lru_gates(c, wa_ref, wx_ref, pv)
        a_ref[...] = a
        u_ref[...] = m * (ig * c)

        def steps(k, h):
            base = pl.multiple_of(k * HALO, HALO)
            for r in range(HALO):
                h = a_ref[pl.ds(base + r, 1), :] * h + u_ref[pl.ds(base + r, 1), :]
                h_ref[pl.ds(base + r, 1), :] = h
            return h

        hc_ref[...] = lax.fori_loop(0, t // HALO, steps, hc_ref[...])
        ag = ag_ref[...]
        ya_ref[...] = (h_ref[...] * (ag * _sigmoid(ag))).astype(BF16)

    full = lambda shp: pl.BlockSpec(shp, lambda i: (0,) * len(shp))
    return pl.pallas_call(
        body, name="rglru_fwd", grid=(s // t,),
        in_specs=[pl.BlockSpec((t, w), lambda i: (i, 0)), pl.BlockSpec((t, w), lambda i: (i, 1)),
                  full((8, w)), full((LRU_BLOCKS, LRU_BLOCK, LRU_BLOCK)), full((LRU_BLOCKS, LRU_BLOCK, LRU_BLOCK))],
        out_specs=[pl.BlockSpec((t, w), lambda i: (i, 0)), pl.BlockSpec((t, w), lambda i: (i, 0))],
        out_shape=[jax.ShapeDtypeStruct((s, w), BF16), jax.ShapeDtypeStruct((s, w), F32)],
        scratch_shapes=[pltpu.VMEM((t + HALO, w), F32), pltpu.VMEM((t, w), F32), pltpu.VMEM((t, w), F32),
                        pltpu.VMEM((1, w), F32)],
        compiler_params=_cparams(("arbitrary",)),
    )(h0a, h0a, pvec, wa16, wx16)


def _rglru_bwd(h0a, h, dh, pvec, wa16, wx16):
    s = h0a.shape[0]
    w = D_MODEL
    t = min(LRU_T, s)
    nb = s // t
    hb = t // HALO

    def body(ax_ref, axh_ref, h_ref, hh_ref, dh_ref, pv_ref, wa_ref, wx_ref,
             dax_ref, dwa_ref, dwx_ref, dpv_ref, ext_ref, hext_ref, dcext_ref, a_ref, g_ref, gc_ref):
        i = pl.program_id(0)
        blk = nb - 1 - i

        @pl.when(i == 0)
        def _():
            dwa_ref[...] = jnp.zeros_like(dwa_ref)
            dwx_ref[...] = jnp.zeros_like(dwx_ref)
            dpv_ref[...] = jnp.zeros_like(dpv_ref)
            gc_ref[...] = jnp.zeros((1, w), F32)
            dcext_ref[pl.ds(t, HALO), :] = jnp.zeros((HALO, w), F32)

        pv = pv_ref[...]
        ax = ax_ref[...]
        keep = (blk > 0).astype(F32)
        ext_ref[pl.ds(0, HALO), :] = axh_ref[...] * keep
        ext_ref[pl.ds(HALO, t), :] = ax
        hext_ref[pl.ds(0, HALO), :] = hh_ref[...] * keep
        hext_ref[pl.ds(HALO, t), :] = h_ref[...]
        c = _conv(ext_ref, ax, pv, t)
        c16, r, ig, ls, _, a, a2, m = _lru_gates(c, wa_ref, wx_ref, pv)
        a_ref[...] = a

        def steps(k, carry):
            base = pl.multiple_of(t - HALO - k * HALO, HALO)
            for r in reversed(range(HALO)):
                g = dh_ref[pl.ds(base + r, 1), :] + carry
                g_ref[pl.ds(base + r, 1), :] = g
                carry = a_ref[pl.ds(base + r, 1), :] * g
            return carry

        gc_ref[...] = lax.fori_loop(0, t // HALO, steps, gc_ref[...])
        g = g_ref[...]
        hprev = hext_ref[pl.ds(HALO - 1, t), :]
        gm = g * m
        d_la = g * hprev * a - (g * ig * c) * a2 / m
        d_pr = d_la * (LRU_C * ls) * r * (1.0 - r)
        d_pi = gm * c * ig * (1.0 - ig)
        dc = gm * ig
        dpr16 = d_pr.astype(BF16)
        dpi16 = d_pi.astype(BF16)
        dc_parts = []
        for n in range(LRU_BLOCKS):
            sl = slice(n * LRU_BLOCK, (n + 1) * LRU_BLOCK)
            dwa_ref[n] += _dot(c16[:, sl], dpr16[:, sl], TN)
            dwx_ref[n] += _dot(c16[:, sl], dpi16[:, sl], TN)
            dc_parts.append(_dot(dpr16[:, sl], wa_ref[n], NT) + _dot(dpi16[:, sl], wx_ref[n], NT))
        dc = dc + jnp.concatenate(dc_parts, axis=1)
        dcext_ref[pl.ds(0, t), :] = dc
        dax = (pv[3:4] * dc + pv[2:3] * dcext_ref[pl.ds(1, t), :] + pv[1:2] * dcext_ref[pl.ds(2, t), :]
               + pv[0:1] * dcext_ref[pl.ds(3, t), :])
        dax_ref[...] = dax.astype(BF16)
        dcext_ref[pl.ds(t, HALO), :] = dc[:HALO, :]
        sums = [_colsum(dc * ext_ref[pl.ds(HALO - 3 + kk, t), :]) for kk in range(4)]
        sums += [_colsum(dc), _colsum(d_pr), _colsum(d_pi), _colsum(d_la * (LRU_C * r))]
        for kk, part in enumerate(sums):
            dpv_ref[pl.ds(kk, 1), :] += part

        @pl.when(i == nb - 1)
        def _():
            lam = pv[7:8]
            dls = dpv_ref[pl.ds(7, 1), :]
            dpv_ref[pl.ds(7, 1), :] = dls * _sigmoid(-lam)

    full = lambda shp: pl.BlockSpec(shp, lambda i: (0,) * len(shp))
    rev = lambda cb: pl.BlockSpec((t, w), functools.partial(lambda i, cb: (nb - 1 - i, cb), cb=cb))
    halo = lambda cb: pl.BlockSpec(
        (HALO, w), functools.partial(lambda i, cb: (jnp.maximum((nb - 1 - i) * hb - 1, 0), cb), cb=cb))
    gw = (LRU_BLOCKS, LRU_BLOCK, LRU_BLOCK)
    return pl.pallas_call(
        body, name="rglru_bwd", grid=(nb,),
        in_specs=[rev(0), halo(0), rev(0), halo(0), rev(0), full((8, w)), full(gw), full(gw)],
        out_specs=[rev(0), full(gw), full(gw), full((8, w))],
        out_shape=[jax.ShapeDtypeStruct((s, w), BF16), jax.ShapeDtypeStruct(gw, F32),
                   jax.ShapeDtypeStruct(gw, F32), jax.ShapeDtypeStruct((8, w), F32)],
        scratch_shapes=[pltpu.VMEM((t + HALO, w), F32), pltpu.VMEM((t + HALO, w), F32),
                        pltpu.VMEM((t + HALO, w), F32), pltpu.VMEM((t, w), F32), pltpu.VMEM((t, w), F32),
                        pltpu.VMEM((1, w), F32)],
        compiler_params=_cparams(("arbitrary",)),
    )(h0a, h0a, h, h, dh, pvec, wa16, wx16)


SB_T = 256
SB_SUB = 8


def _split16(x):
    hi = x.astype(BF16)
    lo = (x - hi.astype(F32)).astype(BF16)
    return jnp.concatenate([hi, lo], axis=0)


def _sb_scores(q, k, scale, causal):
    z = _dot(q, k, NT) * scale
    log1mb = -(jnp.maximum(z, 0.0) + jnp.log(1.0 + jnp.exp(-jnp.abs(z))))
    if causal is not None:
        log1mb = jnp.where(causal, log1mb, 0.0)
    return z, log1mb


def _sb_cumsum(log1mb, tri):
    tq = log1mb.shape[0]
    cs = _dot(_split16(log1mb), tri, NN)
    return cs[:tq] + cs[tq:]


def _sb_weights(z, cs, run, causal):
    wgt = jnp.exp(z + cs + run)
    if causal is not None:
        wgt = jnp.where(causal, wgt, 0.0)
    return wgt


def _sb_tile(q, k, scale, tri, run, causal):
    z, log1mb = _sb_scores(q, k, scale, causal)
    cs = _sb_cumsum(log1mb, tri)
    return z, _sb_weights(z, cs, run, causal), run + cs[:, 0:1]


SB_DEAD = -105.0


def _sb_alive(run):
    return (jnp.max(run) > SB_DEAD).astype(jnp.int32)


def _sb_more(carry):
    return (carry[0] >= 0) & (carry[1] > 0)


def _sb_fwd(qkv, bg):
    s = qkv.shape[0]
    t = min(SB_T, s)
    nq = s // t
    dh = SB_HEAD_DIM
    scale = 1.0 / math.sqrt(dh)

    sub = SB_SUB if nq % SB_SUB == 0 else 1

    def body(q_ref, k_ref, v_ref, bg_ref, o_ref, y_ref):
        row = lax.broadcasted_iota(jnp.int32, (t, t), 0)
        col = lax.broadcasted_iota(jnp.int32, (t, t), 1)
        tri = jnp.where(row >= col, 1.0, 0.0).astype(BF16)
        causal = col < row
        def tile(q, kb, run, mask):
            off = pl.multiple_of(kb * t, t)
            _, wgt, run = _sb_tile(q, k_ref[pl.ds(off, t), :], scale, tri, run, mask)
            return run, _dot(wgt.astype(BF16), v_ref[pl.ds(off, t), :], NN)

        def rows_of(ref, kb):
            return ref[pl.ds(pl.multiple_of(kb * t, t), t), :]

        iqs = [pl.program_id(1) * sub + u for u in range(sub)]
        qs = [q_ref[pl.ds(u * t, t), :] for u in range(sub)]
        lefts = [jnp.maximum(iq - 1, 0) for iq in iqs]
        scores = [(_sb_scores(q, rows_of(k_ref, iq), scale, causal), _sb_scores(q, rows_of(k_ref, lf), scale, None))
                  for q, iq, lf in zip(qs, iqs, lefts)]
        sums = [(_sb_cumsum(d[1], tri), _sb_cumsum(l[1], tri)) for d, l in scores]
        blocks = []
        for q, iq, lf, (sd, sl), (cd, cl) in zip(qs, iqs, lefts, scores, sums):
            run_d = cd[:, 0:1]
            acc_d = _dot(_sb_weights(sd[0], cd, 0.0, causal).astype(BF16), rows_of(v_ref, iq), NN)
            acc_l = _dot(_sb_weights(sl[0], cl, run_d, None).astype(BF16), rows_of(v_ref, lf), NN)
            blocks.append((iq, q, jnp.where(iq > 0, run_d + cl[:, 0:1], run_d), acc_d + jnp.where(iq > 0, acc_l, 0.0)))
        for u, (iq, q, run, acc) in enumerate(blocks):
            def loop(carry, q=q):
                kb, _, run, acc = carry
                run, part = tile(q, kb, run, None)
                return kb - 1, _sb_alive(run), run, acc + part

            _, _, run, acc = lax.while_loop(_sb_more, loop, (iq - 2, _sb_alive(run), run, acc))
            rows = pl.ds(u * t, t)
            o_ref[rows, :] = acc
            g = bg_ref[rows, :]
            y_ref[rows, :] = (acc * (g * _sigmoid(g))).astype(BF16)

    blk = lambda off: pl.BlockSpec((sub * t, dh), functools.partial(lambda h, i, off: (i, h + off), off=off))
    col = lambda off: pl.BlockSpec((s, dh), functools.partial(lambda h, i, off: (0, h + off), off=off))
    return pl.pallas_call(
        body, name="sb_fwd", grid=(SB_HEADS, nq // sub),
        in_specs=[blk(0), col(SB_HEADS), col(2 * SB_HEADS), blk(0)],
        out_specs=[blk(0), blk(0)],
        out_shape=[jax.ShapeDtypeStruct((s, SB_HEADS * dh), F32), jax.ShapeDtypeStruct((s, SB_HEADS * dh), BF16)],
        compiler_params=_cparams(("parallel", "arbitrary")),
    )(qkv, qkv, qkv, bg)


def _sb_bwd(qkv, dob):
    s = qkv.shape[0]
    t = min(SB_T, s)
    nq = s // t
    dh = SB_HEAD_DIM
    scale = 1.0 / math.sqrt(dh)

    sub = SB_SUB if nq % SB_SUB == 0 else 1

    def body(q_ref, k_ref, v_ref, do_ref, dq_ref, dk_out, dv_out, e_ref, b_ref, pe_ref, pb_ref, dk_ref, dv_ref):
        @pl.when(pl.program_id(1) == 0)
        def _():
            dk_ref[...] = jnp.zeros_like(dk_ref)
            dv_ref[...] = jnp.zeros_like(dv_ref)

        row = lax.broadcasted_iota(jnp.int32, (t, t), 0)
        col = lax.broadcasted_iota(jnp.int32, (t, t), 1)
        tri = jnp.where(row >= col, 1.0, 0.0).astype(BF16)
        tri_x = jnp.where(row < col, 1.0, 0.0).astype(BF16)
        causal = col < row

        def sweep1(q, do, kb, e_dst, b_dst, run, mask, live=None):
            off = pl.multiple_of(kb * t, t)
            v = v_ref[pl.ds(off, t), :]
            z, wgt, run = _sb_tile(q, k_ref[pl.ds(off, t), :], scale, tri, run, mask)
            if live is not None:
                wgt = jnp.where(live, wgt, 0.0)
            e_dst[...] = wgt * _dot(do, v, NT)
            b_dst[...] = _sigmoid(z)
            dv_ref[pl.ds(off, t), :] += _dot(wgt.astype(BF16), do, TN)
            return run

        def sweep2(q, kb, e_src, b_src, pre, mask, live=None):
            off = pl.multiple_of(kb * t, t)
            e = e_src[...]
            beta = b_src[...]
            ps = _dot(_split16(e), tri_x, NN)
            ps = ps[:t] + ps[t:]
            dz = e * (1.0 - beta) - beta * (ps + pre)
            if mask is not None:
                dz = jnp.where(mask, dz, 0.0)
            if live is not None:
                dz = jnp.where(live, dz, 0.0)
            dz16 = (dz * scale).astype(BF16)
            dk_ref[pl.ds(off, t), :] += _dot(dz16, q, TN)
            return pre + ps[:, t - 1:t] + e[:, t - 1:t], _dot(dz16, k_ref[pl.ds(off, t), :], NN)

        blocks = []
        for u in range(sub):
            iq = pl.program_id(1) * sub + u
            q, do = q_ref[pl.ds(u * t, t), :], do_ref[pl.ds(u * t, t), :]
            left = jnp.maximum(iq - 1, 0)
            run_d = sweep1(q, do, iq, pe_ref.at[u, 0], pb_ref.at[u, 0], jnp.zeros((t, 1), F32), causal)
            run_l = sweep1(q, do, left, pe_ref.at[u, 1], pb_ref.at[u, 1], run_d, None, iq > 0)
            blocks.append((iq, q, do, left, jnp.where(iq > 0, run_l, run_d)))
        carried = []
        for iq, q, do, left, run in blocks:
            def loop1(carry, q=q, do=do):
                run = sweep1(q, do, carry[0], e_ref.at[carry[0]], b_ref.at[carry[0]], carry[2], None)
                return carry[0] - 1, _sb_alive(run), run

            first = lax.while_loop(_sb_more, loop1, (iq - 2, _sb_alive(run), run))[0] + 1

            def loop2(kb, carry, q=q):
                pre, part = sweep2(q, kb, e_ref.at[kb], b_ref.at[kb], carry[0], None)
                return pre, carry[1] + part

            carried.append(lax.fori_loop(first, iq - 1, loop2, (jnp.zeros((t, 1), F32), jnp.zeros((t, dh), F32))))
        for u, ((iq, q, do, left, _), (pre, dq)) in enumerate(zip(blocks, carried)):
            pre, dq_l = sweep2(q, left, pe_ref.at[u, 1], pb_ref.at[u, 1], pre, None, iq > 0)
            _, dq_d = sweep2(q, iq, pe_ref.at[u, 0], pb_ref.at[u, 0], pre, causal)
            dq_ref[pl.ds(u * t, t), :] = (dq + dq_l + dq_d).astype(BF16)

        @pl.when(pl.program_id(1) == nq // sub - 1)
        def _():
            dk_out[...] = dk_ref[...].astype(BF16)
            dv_out[...] = dv_ref[...].astype(BF16)

    blk = lambda off: pl.BlockSpec((sub * t, dh), functools.partial(lambda h, i, off: (i, h + off), off=off))
    col = lambda off: pl.BlockSpec((s, dh), functools.partial(lambda h, i, off: (0, h + off), off=off))
    wide = SB_HEADS * dh
    return pl.pallas_call(
        body, name="sb_bwd", grid=(SB_HEADS, nq // sub),
        in_specs=[blk(0), col(SB_HEADS), col(2 * SB_HEADS), blk(0)],
        out_specs=[blk(0), col(0), col(0)],
        out_shape=[jax.ShapeDtypeStruct((s, wide), BF16)] * 3,
        scratch_shapes=[pltpu.VMEM((nq, t, t), F32), pltpu.VMEM((nq, t, t), F32),
                        pltpu.VMEM((sub, 2, t, t), F32), pltpu.VMEM((sub, 2, t, t), F32),
                        pltpu.VMEM((s, dh), F32), pltpu.VMEM((s, dh), F32)],
        compiler_params=_cparams(("parallel", "arbitrary")),
    )(qkv, qkv, qkv, dob)


def _alibi_slope(h):
    return float(2.0 ** (-8.0 * (h + 1) / C_HEADS))


GROUP_ROWS = C_GROUP * WINDOW


def _swa_window(n):
    qb = WINDOW
    i = lax.broadcasted_iota(jnp.int32, (qb, 2 * qb), 0)
    j = lax.broadcasted_iota(jnp.int32, (qb, 2 * qb), 1)
    d = i - j + qb
    valid = (d >= 0) & (d < WINDOW) & ((j >= qb) | (n > 0))
    stacked = lambda x: jnp.concatenate([x] * C_GROUP, axis=0)
    return stacked(d.astype(F32)), stacked(jnp.where(valid, 0.0, -1e30))


def _swa_group_cols(c, sink_ref):
    head = lax.shift_right_logical(lax.broadcasted_iota(jnp.int32, (GROUP_ROWS, 1), 0), WINDOW.bit_length() - 1)
    slope = jnp.zeros((GROUP_ROWS, 1), F32)
    sink = jnp.zeros((GROUP_ROWS, 1), F32)
    for hh in range(C_GROUP):
        slope = jnp.where(head == hh, _alibi_slope(c * C_GROUP + hh), slope)
        sink = jnp.where(head == hh, sink_ref[c * C_GROUP + hh], sink)
    return slope, sink


def _swa_probs(qg, kw, sink, slope, dist, bias, scale):
    return _swa_softmax(_dot(qg, kw, NT), sink, slope, dist, bias, scale)


def _swa_softmax(raw, sink, slope, dist, bias, scale):
    sc = raw * scale - slope * dist + bias
    m = jnp.maximum(jnp.max(sc, axis=-1, keepdims=True), sink)
    p = jnp.exp(sc - m)
    ps = jnp.exp(sink - m)
    inv = 1.0 / (jnp.sum(p, axis=-1, keepdims=True) + ps)
    return p * inv, ps * inv


def _stack_heads(x, c):
    hd = C_HEAD_DIM
    return jnp.concatenate([x[:, (c * C_GROUP + hh) * hd:(c * C_GROUP + hh + 1) * hd] for hh in range(C_GROUP)], axis=0)


SWA_SUB = 2


def _swa_kv_specs(sub, kvw):
    return [pl.BlockSpec((WINDOW, 2 * kvw), functools.partial(lambda n, w: (n * sub + w, 0), w=w)) for w in range(sub + 1)]


def _swa_fwd(q16, gate, kvp, sinks):
    s = q16.shape[0]
    qb = WINDOW
    hd = C_HEAD_DIM
    scale = 1.0 / math.sqrt(hd)
    kvw = C_KV_HEADS * hd
    sub = SWA_SUB if (s // qb) % SWA_SUB == 0 else 1

    def body(sink_ref, q_ref, g_ref, *refs):
        kv_refs, (o_ref, y_ref) = refs[:sub + 1], refs[sub + 1:]
        units = []
        for u in range(sub):
            kv = jnp.concatenate([kv_refs[u][...], kv_refs[u + 1][...]], axis=0)
            q = q_ref[slice(u * qb, (u + 1) * qb), :]
            for c in range(C_KV_HEADS):
                raw = _dot(_stack_heads(q, c), kv[:, c * hd:(c + 1) * hd], NT)
                units.append((u, c, raw, kv[:, kvw + c * hd:kvw + (c + 1) * hd]))
        windows = [_swa_window(pl.program_id(0) * sub + u) for u in range(sub)]
        probs = []
        for u, c, raw, vw in units:
            slope, sink = _swa_group_cols(c, sink_ref)
            probs.append(_swa_softmax(raw, sink, slope, windows[u][0], windows[u][1], scale)[0])
        for (u, c, raw, vw), p in zip(units, probs):
            og = _dot(p.astype(BF16), vw, NN)
            for hh in range(C_GROUP):
                h = c * C_GROUP + hh
                o_ref[slice(u * qb, (u + 1) * qb), h * hd:(h + 1) * hd] = og[hh * qb:(hh + 1) * qb, :]
        g = g_ref[...]
        y_ref[...] = (o_ref[...] * (g * _sigmoid(g))).astype(BF16)

    wide = C_HEADS * hd
    big = pl.BlockSpec((sub * qb, wide), lambda n: (n, 0))
    return pl.pallas_call(
        body, name="swa_fwd", grid=(s // (sub * qb),),
        in_specs=[pl.BlockSpec(memory_space=pltpu.SMEM), big, big] + _swa_kv_specs(sub, kvw),
        out_specs=[big, big],
        out_shape=[jax.ShapeDtypeStruct((s, wide), F32), jax.ShapeDtypeStruct((s, wide), BF16)],
        compiler_params=_cparams(("arbitrary",)),
    )(sinks, q16, gate, *([kvp] * (sub + 1)))


def _swa_bwd(q16, do16, kvp, sinks):
    s = q16.shape[0]
    qb = WINDOW
    hd = C_HEAD_DIM
    scale = 1.0 / math.sqrt(hd)
    kvw = C_KV_HEADS * hd
    sub = 1
    nstep = s // (sub * qb)

    def body(sink_ref, q_ref, do_ref, *refs):
        kv_refs, (dq_ref, dkv_ref, ds_ref, sacc_ref) = refs[:sub + 1], refs[sub + 1:]

        @pl.when(pl.program_id(0) == 0)
        def _():
            dkv_ref[...] = jnp.zeros_like(dkv_ref)
            sacc_ref[...] = jnp.zeros_like(sacc_ref)

        for u in range(sub):
            rows = slice(u * qb, (u + 1) * qb)
            n = pl.program_id(0) * sub + u
            q = q_ref[rows, :]
            do = do_ref[rows, :]
            kv = jnp.concatenate([kv_refs[u][...], kv_refs[u + 1][...]], axis=0)
            off = pl.multiple_of(n * qb, qb)
            dist, bias = _swa_window(n)
            groups = []
            for c in range(C_KV_HEADS):
                kw = kv[:, c * hd:(c + 1) * hd]
                qg, dog = _stack_heads(q, c), _stack_heads(do, c)
                groups.append((c, kw, qg, dog, _dot(qg, kw, NT), _dot(dog, kv[:, kvw + c * hd:kvw + (c + 1) * hd], NT)))
            chains = []
            for c, kw, qg, dog, raw, dp in groups:
                slope, sink = _swa_group_cols(c, sink_ref)
                p, ps = _swa_softmax(raw, sink, slope, dist, bias, scale)
                dd = jnp.sum(p * dp, axis=-1, keepdims=True)
                sacc_ref[c] += ps * dd
                chains.append(((p * (dp - dd) * scale).astype(BF16), p.astype(BF16)))
            for (c, kw, qg, dog, _, _), (ds16, p16) in zip(groups, chains):
                dqg = _dot(ds16, kw, NN).astype(BF16)
                for hh in range(C_GROUP):
                    h = c * C_GROUP + hh
                    dq_ref[rows, h * hd:(h + 1) * hd] = dqg[hh * qb:(hh + 1) * qb, :]
                dkv_ref[pl.ds(off, 2 * qb), c * hd:(c + 1) * hd] += _dot(ds16, qg, TN)
                dkv_ref[pl.ds(off, 2 * qb), kvw + c * hd:kvw + (c + 1) * hd] += _dot(p16, dog, TN)

        @pl.when(pl.program_id(0) == nstep - 1)
        def _():
            lane = lax.broadcasted_iota(jnp.int32, (1, 128), 1)
            row = jnp.zeros((1, 128), F32)
            for c in range(C_KV_HEADS):
                for hh in range(C_GROUP):
                    tot = jnp.sum(sacc_ref[c, pl.ds(hh * qb, qb), :], axis=0, keepdims=True)
                    row = jnp.where(lane == c * C_GROUP + hh, -tot, row)
            ds_ref[...] = row

    wide = C_HEADS * hd
    big = pl.BlockSpec((sub * qb, wide), lambda n: (n, 0))
    return pl.pallas_call(
        body, name="swa_bwd", grid=(nstep,),
        in_specs=[pl.BlockSpec(memory_space=pltpu.SMEM), big, big] + _swa_kv_specs(sub, kvw),
        out_specs=[big,
                   pl.BlockSpec((s + qb, 2 * kvw), lambda n: (0, 0)), pl.BlockSpec((1, 128), lambda n: (0, 0))],
        out_shape=[jax.ShapeDtypeStruct((s, wide), BF16), jax.ShapeDtypeStruct((s + qb, 2 * kvw), F32),
                   jax.ShapeDtypeStruct((1, 128), F32)],
        scratch_shapes=[pltpu.VMEM((C_KV_HEADS, GROUP_ROWS, 1), F32)],
        compiler_params=_cparams(("arbitrary",)),
    )(sinks, q16, do16, *([kvp] * (sub + 1)))


def _adamw(parts, w, m, v, *, name, tr=496):
    npart, r, c = parts.shape
    tr = min(tr, r)
    assert r % tr == 0 and tr % 16 == 0
    c1 = 1.0 / (1.0 - ADAM_B1 ** ADAM_STEP)
    c2 = 1.0 / (1.0 - ADAM_B2 ** ADAM_STEP)

    def body(p_ref, w_ref, m_ref, v_ref, g_ref, d_ref, nm_ref, nv_ref):
        g = p_ref[0].astype(F32)
        for j in range(1, npart):
            g = g + p_ref[j].astype(F32)
        nm = ADAM_B1 * m_ref[...] + (1.0 - ADAM_B1) * g
        nv = ADAM_B2 * v_ref[...] + (1.0 - ADAM_B2) * (g * g)
        g_ref[...] = g
        nm_ref[...] = nm
        nv_ref[...] = nv
        d_ref[...] = -ADAM_LR * ((nm * c1) / (jnp.sqrt(nv * c2) + ADAM_EPS) + ADAM_WD * w_ref[...])

    spec = pl.BlockSpec((tr, c), lambda i: (i, 0))
    return pl.pallas_call(
        body, name=name, grid=(r // tr,),
        in_specs=[pl.BlockSpec((npart, tr, c), lambda i: (0, i, 0)), spec, spec, spec],
        out_specs=[spec] * 4, out_shape=[jax.ShapeDtypeStruct((r, c), F32)] * 4,
        compiler_params=_cparams(("parallel",)),
    )(parts, w, m, v)


GATHER = "gather"
A2A = "a2a"
GATHER_COLS = "gather_cols"
A2A_COLS = "a2a_cols"


def _gather_two_level(bufs, modes, x, *, tr=1024, name):
    nb = len(bufs)
    rows, cols = x.shape
    tr = min(tr, rows)
    steps = rows // tr

    def body(*refs):
        x_ref, src, x16_ref, dst = refs[0], refs[1:1 + nb], refs[1 + nb], refs[2 + nb:2 + 2 * nb]
        send_sems, recv_sems, local_sems = refs[2 + 2 * nb:]
        cx, cy, cc = lax.axis_index("x"), lax.axis_index("y"), lax.axis_index("c")
        me = 4 * cx + 2 * cy + cc
        here, sibling = (cx, cy, cc), (cx, cy, 1 - cc)

        def copy(b, k, origin, to, source):
            return pltpu.make_async_remote_copy(
                src_ref=source, dst_ref=_slot(dst[b], modes[b], origin), send_sem=send_sems.at[b, k - 1],
                recv_sem=recv_sems.at[b, k - 1], device_id=to, device_id_type=pl.DeviceIdType.MESH)

        def first_copies():
            local = [pltpu.make_async_copy(src[b], _slot(dst[b], modes[b], me), local_sems.at[b]) for b in range(nb)]
            sends = [copy(b, k, me, (cx ^ (k >> 2), cy ^ ((k >> 1) & 1), cc ^ (k & 1)), src[b])
                     for k in (1, 2, 4, 6) for b in range(nb)]
            return local, sends

        @pl.when(pl.program_id(0) == 0)
        def _():
            local, sends = first_copies()
            for cp in local + sends:
                cp.start()

        x16_ref[...] = x_ref[...].astype(BF16)

        @pl.when(pl.program_id(0) == steps - 1)
        def _():
            local, sends = first_copies()
            for j in (2, 4, 6):
                for b in range(nb):
                    copy(b, j, me ^ j, here, src[b]).wait_recv()
                    passed = copy(b, j ^ 1, me ^ j, sibling, _slot(dst[b], modes[b], me ^ j))
                    passed.start()
                    sends.append(passed)
            for k in (1, 3, 5, 7):
                for b in range(nb):
                    copy(b, k, me ^ k, here, src[b]).wait_recv()
            for cp in sends:
                cp.wait_send()
            for cp in local:
                cp.wait()

    hbm = pl.BlockSpec(memory_space=pl.ANY)
    tile = pl.BlockSpec((tr, cols), lambda i: (i, 0))
    return pl.pallas_call(
        body, name=name, grid=(steps,), in_specs=[tile] + [hbm] * nb, out_specs=[tile] + [hbm] * nb,
        out_shape=[jax.ShapeDtypeStruct(x.shape, BF16)] + _exchange_out_shapes(bufs, modes),
        scratch_shapes=_exchange_sems(nb),
        compiler_params=pltpu.CompilerParams(dimension_semantics=("arbitrary",), vmem_limit_bytes=VMEM_LIMIT,
                                             has_side_effects=True),
    )(x, *bufs)


def _exchange_out_shapes(bufs, gather):
    shapes = {GATHER: lambda s: (N_DEV,) + s, A2A: lambda s: s, GATHER_COLS: lambda s: (s[0], N_DEV * s[1]),
              A2A_COLS: lambda s: (N_DEV, s[0], s[1] // N_DEV)}
    return [jax.ShapeDtypeStruct(shapes[g](tuple(b.shape)), b.dtype) for b, g in zip(bufs, gather)]


def _sent(src, mode, peer):
    if mode == A2A:
        return src.at[peer]
    if mode == A2A_COLS:
        w = src.shape[1] // N_DEV
        return src.at[:, pl.ds(pl.multiple_of(peer * w, 128), w)]
    return src


def _slot(dst, mode, dev):
    if mode == GATHER_COLS:
        w = dst.shape[1] // N_DEV
        return dst.at[:, pl.ds(pl.multiple_of(dev * w, 128), w)]
    return dst.at[dev]


def _exchange_sems(nb):
    return [pltpu.SemaphoreType.DMA((nb, N_DEV - 1)), pltpu.SemaphoreType.DMA((nb, N_DEV - 1)),
            pltpu.SemaphoreType.DMA((nb,))]


def _exchange_copies(src, dst, gather, sems):
    send_sems, recv_sems, local_sems = sems
    x, y, c = lax.axis_index("x"), lax.axis_index("y"), lax.axis_index("c")
    me = 4 * x + 2 * y + c
    local, sends, recvs = [], [], []
    for b in range(len(src)):
        mine = _sent(src[b], gather[b], me)
        local.append(pltpu.make_async_copy(mine, _slot(dst[b], gather[b], me), local_sems.at[b]))
        for k in range(1, N_DEV):
            px, py, pc = x ^ (k >> 2), y ^ ((k >> 1) & 1), c ^ (k & 1)
            peer = 4 * px + 2 * py + pc
            pair = dict(send_sem=send_sems.at[b, k - 1], recv_sem=recv_sems.at[b, k - 1],
                        device_id_type=pl.DeviceIdType.MESH)
            sends.append(pltpu.make_async_remote_copy(
                src_ref=_sent(src[b], gather[b], peer), dst_ref=_slot(dst[b], gather[b], me), device_id=(px, py, pc),
                **pair))
            recvs.append(pltpu.make_async_remote_copy(
                src_ref=mine, dst_ref=_slot(dst[b], gather[b], peer), device_id=(x, y, c), **pair))
    return local, sends, recvs


def _exchange_start(src, dst, gather, sems):
    local, sends, _ = _exchange_copies(src, dst, gather, sems)
    for cp in local + sends:
        cp.start()


def _exchange_wait(src, dst, gather, sems):
    local, sends, recvs = _exchange_copies(src, dst, gather, sems)
    for cp in sends:
        cp.wait_send()
    for cp in recvs:
        cp.wait_recv()
    for cp in local:
        cp.wait()


R_SMALL = 16
REPL_ROWS = 272
REPL_LEN = REPL_ROWS * D_MODEL


def _small_block(parts):
    flat = jnp.concatenate(parts, axis=-1)
    lead = flat.ndim - 1
    return jnp.pad(flat[..., None, :], [(0, 0)] * lead + [(0, R_SMALL - 1), (0, D_MODEL - flat.shape[-1])])


def _small_vectors(conv_w, ln_g, ln_b):
    return _small_block([conv_w.reshape(512), ln_g.reshape(128), ln_b.reshape(128)])


REPL_SHAPES = ((1, 8, 128, 128), (1, 8, 128, 128), (1, 1024), (1, 1024), (1, 1024), (1, 1024), (1, 1024), (1, 1024),
               (1, 16))


def _pack_repl(parts):
    flat = jnp.concatenate([p.reshape(-1) for p in parts])
    return jnp.concatenate([flat, jnp.zeros((REPL_LEN - flat.shape[0],), F32)]).reshape(REPL_ROWS, D_MODEL)


def _unpack_repl(p):
    flat = p.reshape(-1)
    out, o = [], 0
    for shp in REPL_SHAPES:
        n = math.prod(shp)
        out.append(flat[o:o + n].reshape(shp))
        o += n
    return out


def kernel(x, e_w_in, e_conv_w, e_conv_b, e_w_gate_a, e_b_gate_a, e_w_gate_x, e_b_gate_x, e_lru_lambda, e_w_out, e_ln_g, e_ln_b, o_w_in, o_sinks, o_w_out, o_ln_g, o_ln_b, loss_target, m_e_w_in, m_e_conv_w, m_e_conv_b, m_e_w_gate_a, m_e_b_gate_a, m_e_w_gate_x, m_e_b_gate_x, m_e_lru_lambda, m_e_w_out, m_e_ln_g, m_e_ln_b, m_o_w_in, m_o_sinks, m_o_w_out, m_o_ln_g, m_o_ln_b, v_e_w_in, v_e_conv_w, v_e_conv_b, v_e_w_gate_a, v_e_b_gate_a, v_e_w_gate_x, v_e_b_gate_x, v_e_lru_lambda, v_e_w_out, v_e_ln_g, v_e_ln_b, v_o_w_in, v_o_sinks, v_o_w_out, v_o_ln_g, v_o_ln_b):
    d = D_MODEL
    x0 = x[0]
    target = loss_target[0]
    s = x0.shape[0]

    spack = _small_block([e_conv_w.reshape(512), o_ln_g.reshape(128), o_ln_b.reshape(128)])
    x0_16, wi_e, sall = _gather_two_level([e_w_in[0].astype(BF16), spack], [GATHER_COLS, GATHER], x0,
                                          name="gather_w_in")
    conv_w = sall[:, 0, 0:512].reshape(N_DEV, 4, 128).transpose(1, 0, 2).reshape(4, d)
    ln_g_o = sall[:, 0, 512:640].reshape(1, d)
    ln_b_o = sall[:, 0, 640:768].reshape(1, d)
    pvec = jnp.concatenate([conv_w, e_conv_b, e_b_gate_a, e_b_gate_x, e_lru_lambda], axis=0)
    wa16 = e_w_gate_a[0].astype(BF16)
    wx16 = e_w_gate_x[0].astype(BF16)
    sinks = o_sinks[0]

    h0a, wall_out_e = _matmul(x0_16, wi_e, mode="nn", n_out=2 * d, b_off=0, name="l0_in_a",
                              comm=([e_w_out[0].astype(BF16)], [GATHER]))
    qkv, wall_in_o = _matmul(x0_16, wi_e, mode="nn", n_out=3 * d, b_off=2 * d, out_dtype=BF16, name="l0_in_qkv",
                             comm=([o_w_in[0].astype(BF16)], [GATHER]))
    bg, wall_out_o = _matmul(x0_16, wi_e, mode="nn", n_out=d, b_off=5 * d, name="l0_in_bg",
                             comm=([o_w_out[0].astype(BF16)], [GATHER]))
    wo_e = wall_out_e.reshape(2 * d, d)
    wi_o = wall_in_o.transpose(1, 0, 2).reshape(d, 2304)
    wi_o = jnp.concatenate([wi_o[:, :1024], wi_o[:, 1280:], wi_o[:, 1024:1280]], axis=1)
    wo_o = wall_out_o.reshape(d, d)
    ya, hst = _rglru_fwd(h0a, pvec, wa16, wx16)
    ob, yb = _sb_fwd(qkv, bg)
    z0, x1, x1_16 = _matmul([ya, yb], wo_e, mode="nn", tm=512, name="l0_out",
                            epi=(_epi_ln_fwd, [(x0, 0)], [e_ln_g, e_ln_b], [F32, F32, BF16], 0))

    q1 = _matmul(x1_16, wi_o, mode="nn", n_out=d, b_off=0, out_dtype=BF16, name="l1_in_q")
    g1 = _matmul(x1_16, wi_o, mode="nn", n_out=d, b_off=d, name="l1_in_g")
    kv1 = _matmul(x1_16, wi_o, mode="nn", n_out=256, b_off=2 * d, tn=256, out_dtype=BF16, name="l1_in_kv")
    kvp = jnp.concatenate([jnp.zeros((WINDOW, 256), BF16), kv1], axis=0)
    o1, y1 = _swa_fwd(q1, g1, kvp, sinks)
    dz1, dz1_16, loss_cols, dg_o, db_o = _matmul(
        y1, wo_o, mode="nn", tm=512, name="l1_out",
        epi=(_epi_ln_loss_bwd, [(x1, 0), (target, 0)], [ln_g_o, ln_b_o], [F32, BF16], 3))
    loss_hi = jnp.sum(loss_cols).astype(BF16).astype(F32)
    loss_terms = jnp.stack([loss_hi, jnp.sum(loss_cols) - loss_hi]).reshape(1, 2)

    dwo_o = _matmul(y1, dz1_16, mode="tn", out_dtype=BF16, name="l1_dwout")
    do1, dg1, parts_out_o = _matmul(dz1_16, wo_o, mode="nt", tm=512, name="l1_dy",
                                    comm=([dwo_o.reshape(N_DEV, 128, d)], [A2A]),
                                    epi=(_epi_gate_bwd, [(o1, 0), (g1, 0)], [], [BF16, BF16], 0))
    dq1, dkvp, dsink = _swa_bwd(q1, do1, kvp, sinks)
    dkv1 = dkvp[WINDOW:].astype(BF16)
    dh1 = [dq1, dg1, dkv1]
    dw_qg = _matmul(x1_16, [dq1, dg1], mode="tn", out_dtype=BF16, name="l1_dwin_qg")
    dw_kv = _matmul(x1_16, dkv1, mode="tn", out_dtype=BF16, name="l1_dwin_kv")
    dwi_o = jnp.concatenate([dw_qg[:, :1024], dw_kv, dw_qg[:, 1024:]], axis=1)
    dz0, dz0_16, dg_e, db_e, parts_in_o = _matmul(
        dh1, wi_o, mode="nt", tm=512, tk=2304, name="l1_dx",
        comm=([dwi_o.reshape(d, N_DEV, 288).transpose(1, 0, 2)], [A2A]),
        epi=(_epi_ln_bwd, [(dz1, 0), (z0, 0)], [e_ln_g], [F32, BF16], 2))

    dwo_e = _matmul([ya, yb], dz0_16, mode="tn", out_dtype=BF16, name="l0_dwout")
    dhst, dag, parts_out_e = _matmul(
        dz0_16, wo_e, mode="nt", tm=512, n_out=d, b_off=0, name="l0_dy_a",
        comm=([dwo_e.reshape(N_DEV, 256, d)], [A2A]),
        epi=(_epi_gate_bwd, [(hst, 0), (h0a, d)], [], [F32, BF16], 0))
    dob, dbg = _matmul(dz0_16, wo_e, mode="nt", tm=512, n_out=d, b_off=d, name="l0_dy_b",
                       epi=(_epi_gate_bwd, [(ob, 0), (bg, 0)], [], [BF16, BF16], 0))
    dq0, dk0, dv0 = _sb_bwd(qkv, dob)
    dax, dwa, dwx, dpv = _rglru_bwd(h0a, hst, dhst, pvec, wa16, wx16)
    dh0 = [dax, dag, dq0, dk0, dv0, dbg]
    repl = _pack_repl([dwa, dwx, dpv[4:5], dpv[5:6], dpv[6:7], dpv[7:8], dg_e, db_e, dsink[:, :C_HEADS], loss_terms])
    dwi_e, parts_repl = _matmul(x0_16, dh0, mode="tn", out_dtype=BF16, name="l0_dwin",
                                comm=([repl.astype(BF16)], [GATHER]))
    small = _small_block([dpv[0:4].reshape(4, N_DEV, 128).transpose(1, 0, 2).reshape(N_DEV, 512),
                          dg_o.reshape(N_DEV, 128), db_o.reshape(N_DEV, 128)]).astype(BF16)
    grad_x, parts_in_e, parts_small = _matmul(dh0, wi_e, mode="nt", tm=512, res=dz0, alpha=ALPHA, name="l0_dx",
                                              comm=([dwi_e, small], [A2A_COLS, A2A]))

    res_in_e = _adamw(parts_in_e, e_w_in[0], m_e_w_in[0], v_e_w_in[0], tr=512, name="adamw_w_in_e")
    res_out_e = _adamw(parts_out_e, e_w_out[0], m_e_w_out[0], v_e_w_out[0], name="adamw_w_out_e")
    res_in_o = _adamw(parts_in_o, o_w_in[0], m_o_w_in[0], v_o_w_in[0], tr=512, name="adamw_w_in_o")
    res_out_o = _adamw(parts_out_o, o_w_out[0], m_o_w_out[0], v_o_w_out[0], name="adamw_w_out_o")
    res_small = _adamw(parts_small, _small_vectors(e_conv_w, o_ln_g, o_ln_b),
                       _small_vectors(m_e_conv_w, m_o_ln_g, m_o_ln_b),
                       _small_vectors(v_e_conv_w, v_o_ln_g, v_o_ln_b), name="adamw_vectors")

    w_r = _pack_repl([e_w_gate_a, e_w_gate_x, e_conv_b, e_b_gate_a, e_b_gate_x, e_lru_lambda, e_ln_g, e_ln_b, o_sinks])
    m_r = _pack_repl([m_e_w_gate_a, m_e_w_gate_x, m_e_conv_b, m_e_b_gate_a, m_e_b_gate_x, m_e_lru_lambda, m_e_ln_g,
                      m_e_ln_b, m_o_sinks])
    v_r = _pack_repl([v_e_w_gate_a, v_e_w_gate_x, v_e_conv_b, v_e_b_gate_a, v_e_b_gate_x, v_e_lru_lambda, v_e_ln_g,
                      v_e_ln_b, v_o_sinks])
    g_r, d_r, nm_r, nv_r = _adamw(parts_repl, w_r, m_r, v_r, name="adamw_replicated")
    loss_at = sum(math.prod(shp) for shp in REPL_SHAPES)
    loss = g_r[loss_at // d, loss_at % d] + g_r[(loss_at + 1) // d, (loss_at + 1) % d]

    def assemble(i, rp):
        vec = res_small[i][0]
        cw, lg_o, lb_o = vec[0:512].reshape(1, 4, 128), vec[512:640].reshape(1, 128), vec[640:768].reshape(1, 128)
        w_a, w_x, cb, b_a, b_x, lam, lg_e, lb_e, snk = _unpack_repl(rp)
        return [res_in_e[i][None], cw, cb, w_a, b_a, w_x, b_x, lam, res_out_e[i][None], lg_e, lb_e,
                res_in_o[i][None], snk, res_out_o[i][None], lg_o, lb_o]

    return (loss, grad_x[None], *assemble(0, g_r), *assemble(1, d_r), *assemble(2, nm_r), *assemble(3, nv_r))
```

```python
import functools
import math

import jax
import jax.numpy as jnp
from jax import lax
from jax.experimental import pallas as pl
from jax.experimental.pallas import tpu as pltpu

F32 = jnp.float32
BF16 = jnp.bfloat16

N_DEV = 8
D_MODEL = 1024
LRU_BLOCKS = 8
LRU_BLOCK = 128
LRU_C = 8.0
SB_HEADS = 8
SB_HEAD_DIM = 128
C_HEADS = 16
C_KV_HEADS = 2
C_GROUP = 8
C_HEAD_DIM = 64
WINDOW = 128
DEPTH = 2
ALPHA = float((2 * DEPTH) ** 0.25)
LN_EPS = 1e-5
ADAM_LR = 0.001
ADAM_B1 = 0.9
ADAM_B2 = 0.999
ADAM_EPS = 1e-08
ADAM_WD = 0.01
ADAM_STEP = 10

VMEM_LIMIT = 56 * 1024 * 1024

NN = ((1,), (0,))
NT = ((1,), (1,))
TN = ((0,), (0,))


def _dot(a, b, dims):
    return lax.dot_general(a, b, (dims, ((), ())), preferred_element_type=F32)


def _sigmoid(x):
    return 1.0 / (1.0 + jnp.exp(-x))


def _cparams(sem, vmem=VMEM_LIMIT):
    return pltpu.CompilerParams(dimension_semantics=sem, vmem_limit_bytes=vmem)


def _matmul(a, b, *, mode, n_out=None, b_off=0, out_dtype=F32, res=None, alpha=1.0,
            tm=1024, tn=1024, tk=1024, comm=None, epi=None, name):
    a_list = list(a) if isinstance(a, (list, tuple)) else [a]
    b_list = list(b) if isinstance(b, (list, tuple)) else [b]
    if mode == "tn":
        k = a_list[0].shape[0]
        m = sum(p.shape[1] for p in a_list)
        n = n_out if n_out is not None else sum(p.shape[1] for p in b_list)
        assert all(p.shape[1] == tm for p in a_list) or len(a_list) == 1
        assert all(p.shape[1] == tn for p in b_list) or len(b_list) == 1
    else:
        assert len(b_list) == 1
        m = a_list[0].shape[0]
        k = sum(p.shape[1] for p in a_list)
        n = n_out if n_out is not None else (b.shape[0] if mode == "nt" else b.shape[1])
        if len(a_list) > 1:
            tk = k
    tm, tn, tk = min(tm, m), min(tn, n), min(tk, k)
    assert m % tm == 0 and n % tn == 0 and k % tk == 0 and b_off % tn == 0
    grid = (m // tm, n // tn, k // tk)
    nk = grid[2]
    jo = b_off // tn
    dims = {"nn": NN, "nt": NT, "tn": TN}[mode]
    nc = len(comm[0]) if comm else 0
    if epi:
        assert res is None
        epi_fn, epi_rows, epi_vecs, epi_dtypes, n_sums = epi
        assert n_sums == 0 or grid[1] == 1
    else:
        epi_rows, epi_vecs, epi_dtypes, n_sums = [], [], [out_dtype], 0
    n_ab = len(a_list) + len(b_list)
    n_in = n_ab + (res is not None) + len(epi_rows) + len(epi_vecs)
    n_res = len(epi_dtypes) + n_sums

    def body(*refs):
        a_refs, b_refs = refs[:len(a_list)], refs[len(a_list):n_ab]
        r_ref = refs[n_ab] if res is not None else None
        row_refs = refs[n_in - len(epi_rows) - len(epi_vecs):n_in - len(epi_vecs)]
        vec_refs = refs[n_in - len(epi_vecs):n_in]
        o_refs = refs[n_in + nc:n_in + nc + len(epi_dtypes)]
        sum_refs = refs[n_in + nc + len(epi_dtypes):n_in + nc + n_res]
        if comm:
            step = [pl.program_id(ax) for ax in range(3)]
            c_src = refs[n_in:n_in + nc]
            c_dst = refs[n_in + nc + n_res:n_in + 2 * nc + n_res]
            sems = refs[len(refs) - 3:]

            @pl.when((step[0] == 0) & (step[1] == 0) & (step[2] == 0))
            def _():
                _exchange_start(c_src, c_dst, comm[1], sems)

        def finish(out):
            if not epi:
                if r_ref is not None:
                    out = out + alpha * r_ref[...]
                o_refs[0][...] = out.astype(out_dtype)
                return
            outs, sums = epi_fn(out, [r[...] for r in row_refs], [v[...] for v in vec_refs])
            for o_ref, o in zip(o_refs, outs):
                o_ref[...] = o.astype(o_ref.dtype)
            if sum_refs:
                first_rows = pl.program_id(0) == 0

                @pl.when(first_rows)
                def _():
                    for s_ref, part_sum in zip(sum_refs, sums):
                        s_ref[...] = part_sum

                @pl.when(jnp.logical_not(first_rows))
                def _():
                    for s_ref, part_sum in zip(sum_refs, sums):
                        s_ref[...] += part_sum

        def accumulate(part):
            if nk == 1:
                finish(part)
                return
            acc_ref = refs[n_in + 2 * nc + n_res]
            kk = pl.program_id(2)

            @pl.when(kk == 0)
            def _():
                acc_ref[...] = part

            @pl.when(kk > 0)
            def _():
                acc_ref[...] += part

            @pl.when(kk == nk - 1)
            def _():
                finish(acc_ref[...])

        def product(a_ref, b_val):
            return _dot(a_ref[...].astype(BF16), b_val.astype(BF16), dims)

        if mode != "tn" and len(a_list) > 1:
            part, off = None, 0
            for a_ref in a_refs:
                w = a_ref.shape[1]
                b_val = b_refs[0][off:off + w, :] if mode == "nn" else b_refs[0][:, off:off + w]
                part = product(a_ref, b_val) if part is None else part + product(a_ref, b_val)
                off += w
            accumulate(part)
        elif len(a_list) > 1 or len(b_list) > 1:
            for pa, a_ref in enumerate(a_refs):
                for pb, b_ref in enumerate(b_refs):
                    picked = ([pl.program_id(0) == pa] if len(a_list) > 1 else []) + (
                        [pl.program_id(1) == pb] if len(b_list) > 1 else [])
                    pl.when(functools.reduce(jnp.logical_and, picked))(
                        functools.partial(lambda a_ref, b_ref: accumulate(product(a_ref, b_ref[...])), a_ref, b_ref))
        else:
            accumulate(product(a_refs[0], b_refs[0][...]))

        if comm:
            @pl.when((step[0] == grid[0] - 1) & (step[1] == grid[1] - 1) & (step[2] == grid[2] - 1))
            def _():
                _exchange_wait(c_src, c_dst, comm[1], sems)

    def held(axis, p):
        def index(i, j, kk):
            return (jnp.where((i, j)[axis] == p, kk, 0), 0)
        return index

    if mode == "tn":
        if len(a_list) > 1:
            a_specs = [pl.BlockSpec((tk, tm), held(0, p)) for p in range(len(a_list))]
        else:
            a_specs = [pl.BlockSpec((tk, tm), lambda i, j, kk: (kk, i))]
    elif len(a_list) > 1:
        a_specs = [pl.BlockSpec((tm, p.shape[1]), lambda i, j, kk: (i, 0)) for p in a_list]
    else:
        a_specs = [pl.BlockSpec((tm, tk), lambda i, j, kk: (i, kk))]
    if mode == "nt":
        b_specs = [pl.BlockSpec((tn, tk), lambda i, j, kk: (j + jo, kk))]
    elif len(b_list) > 1:
        b_specs = [pl.BlockSpec((tk, tn), held(1, p)) for p in range(len(b_list))]
    else:
        b_specs = [pl.BlockSpec((tk, tn), lambda i, j, kk: (kk, j + jo))]
    o_spec = pl.BlockSpec((tm, tn), lambda i, j, kk: (i, j))
    hbm = pl.BlockSpec(memory_space=pl.ANY)
    row_specs = [pl.BlockSpec((tm, tn), functools.partial(lambda i, j, kk, o: (i, j + o), o=off // tn))
                 for (_, off) in epi_rows]
    vec_specs = [pl.BlockSpec(v.shape, functools.partial(lambda i, j, kk, nd: (0,) * nd, nd=v.ndim)) for v in epi_vecs]
    in_specs = a_specs + b_specs + ([o_spec] if res is not None else []) + row_specs + vec_specs + [hbm] * nc
    args = (a_list + b_list + ([res] if res is not None else []) + [r for (r, _) in epi_rows] + list(epi_vecs)
            + (list(comm[0]) if comm else []))
    out_specs = [o_spec] * len(epi_dtypes) + [pl.BlockSpec((1, tn), lambda i, j, kk: (0, j))] * n_sums + [hbm] * nc
    out_shape = ([jax.ShapeDtypeStruct((m, n), dt) for dt in epi_dtypes] + [jax.ShapeDtypeStruct((1, n), F32)] * n_sums
                 + (_exchange_out_shapes(*comm) if comm else []))
    scratch = ([pltpu.VMEM((tm, tn), F32)] if nk > 1 else []) + (_exchange_sems(nc) if comm else [])
    if comm or n_sums or n_ab > 2:
        params = pltpu.CompilerParams(dimension_semantics=("arbitrary",) * 3, vmem_limit_bytes=VMEM_LIMIT,
                                      has_side_effects=bool(comm))
    else:
        params = _cparams(("parallel", "parallel", "arbitrary"))
    outs = pl.pallas_call(
        body, name=name, grid=grid, in_specs=in_specs, out_specs=out_specs, out_shape=out_shape,
        scratch_shapes=scratch, compiler_params=params,
    )(*args)
    return tuple(outs) if (comm or epi) else outs[0]


def _ln_stats(z):
    mu = jnp.mean(z, axis=-1, keepdims=True)
    zc = z - mu
    var = jnp.mean(zc * zc, axis=-1, keepdims=True)
    rstd = lax.rsqrt(var + LN_EPS)
    return zc * rstd, rstd


def _ln_bwd(dy, xhat, rstd, g):
    dxh = dy * g
    m1 = jnp.mean(dxh, axis=-1, keepdims=True)
    m2 = jnp.mean(dxh * xhat, axis=-1, keepdims=True)
    return rstd * (dxh - m1 - xhat * m2)


def _colsum(x):
    return jnp.sum(x, axis=0, keepdims=True)


def _epi_ln_fwd(acc, rows, vecs):
    z = ALPHA * rows[0] + acc
    xhat, _ = _ln_stats(z)
    x1 = xhat * vecs[0] + vecs[1]
    return [z, x1, x1], []


def _epi_ln_loss_bwd(acc, rows, vecs):
    inv_c = 1.0 / acc.shape[-1]
    z = ALPHA * rows[0] + acc
    xhat, rstd = _ln_stats(z)
    d = xhat * vecs[0] + vecs[1] - rows[1]
    dy = d * inv_c
    dz = _ln_bwd(dy, xhat, rstd, vecs[0])
    return [dz, dz], [_colsum(0.5 * d * d * inv_c), _colsum(dy * xhat), _colsum(dy)]


def _epi_ln_bwd(acc, rows, vecs):
    dy = acc + ALPHA * rows[0]
    xhat, rstd = _ln_stats(rows[1])
    dz = _ln_bwd(dy, xhat, rstd, vecs[0])
    return [dz, dz], [_colsum(dy * xhat), _colsum(dy)]


def _epi_gate_bwd(acc, rows, vecs):
    oo, gg = rows
    sg = _sigmoid(gg)
    return [acc * (gg * sg), acc * oo * (sg * (1.0 + gg * (1.0 - sg)))], []


LRU_T = 512
HALO = 8


def _log1p(y):
    u = 1.0 + y
    return jnp.where(u == 1.0, y, jnp.log(u) * (y / (u - 1.0)))


def _lru_gates(c, wa_ref, wx_ref, pv):
    c16 = c.astype(BF16)
    pre_r = jnp.concatenate(
        [_dot(c16[:, n * LRU_BLOCK:(n + 1) * LRU_BLOCK], wa_ref[n], NN) for n in range(LRU_BLOCKS)], axis=1)
    pre_i = jnp.concatenate(
        [_dot(c16[:, n * LRU_BLOCK:(n + 1) * LRU_BLOCK], wx_ref[n], NN) for n in range(LRU_BLOCKS)], axis=1)
    r = _sigmoid(pre_r + pv[5:6])
    ig = _sigmoid(pre_i + pv[6:7])
    lam = pv[7:8]
    ls = jnp.minimum(lam, 0.0) - _log1p(jnp.exp(-jnp.abs(lam)))
    la = LRU_C * r * ls
    a = jnp.exp(la)
    a2 = a * a
    m = jnp.sqrt(jnp.tanh(-la) * (a2 + 1.0))
    return c16, r, ig, ls, la, a, a2, m


def _conv(ext_ref, x, pv, t):
    return (pv[4:5] + pv[3:4] * x + pv[2:3] * ext_ref[pl.ds(HALO - 1, t), :]
            + pv[1:2] * ext_ref[pl.ds(HALO - 2, t), :] + pv[0:1] * ext_ref[pl.ds(HALO - 3, t), :])


def _rglru_fwd(h0a, pvec, wa16, wx16):
    s = h0a.shape[0]
    w = D_MODEL
    t = min(LRU_T, s)
    assert s % t == 0

    def body(ax_ref, ag_ref, pv_ref, wa_ref, wx_ref, ya_ref, h_ref, ext_ref, a_ref, u_ref, hc_ref):
        i = pl.program_id(0)

        @pl.when(i == 0)
        def _():
            ext_ref[pl.ds(0, HALO), :] = jnp.zeros((HALO, w), F32)
            hc_ref[...] = jnp.zeros((1, w), F32)

        pv = pv_ref[...]
        ax = ax_ref[...]
        ext_ref[pl.ds(HALO, t), :] = ax
        c = _conv(ext_ref, ax, pv, t)
        ext_ref[pl.ds(0, HALO), :] = ax[t - HALO:, :]
        _, _, ig, _, _, a, _, m = _lru_gates(c, wa_ref, wx_ref, pv)
        a_ref[...] = a
        u_ref[...] = m * (ig * c)

        def steps(k, h):
            base = pl.multiple_of(k * HALO, HALO)
            for r in range(HALO):
                h = a_ref[pl.ds(base + r, 1), :] * h + u_ref[pl.ds(base + r, 1), :]
                h_ref[pl.ds(base + r, 1), :] = h
            return h

        hc_ref[...] = lax.fori_loop(0, t // HALO, steps, hc_ref[...])
        ag = ag_ref[...]
        ya_ref[...] = (h_ref[...] * (ag * _sigmoid(ag))).astype(BF16)

    full = lambda shp: pl.BlockSpec(shp, lambda i: (0,) * len(shp))
    return pl.pallas_call(
        body, name="rglru_fwd", grid=(s // t,),
        in_specs=[pl.BlockSpec((t, w), lambda i: (i, 0)), pl.BlockSpec((t, w), lambda i: (i, 1)),
                  full((8, w)), full((LRU_BLOCKS, LRU_BLOCK, LRU_BLOCK)), full((LRU_BLOCKS, LRU_BLOCK, LRU_BLOCK))],
        out_specs=[pl.BlockSpec((t, w), lambda i: (i, 0)), pl.BlockSpec((t, w), lambda i: (i, 0))],
        out_shape=[jax.ShapeDtypeStruct((s, w), BF16), jax.ShapeDtypeStruct((s, w), F32)],
        scratch_shapes=[pltpu.VMEM((t + HALO, w), F32), pltpu.VMEM((t, w), F32), pltpu.VMEM((t, w), F32),
                        pltpu.VMEM((1, w), F32)],
        compiler_params=_cparams(("arbitrary",)),
    )(h0a, h0a, pvec, wa16, wx16)


def _rglru_bwd(h0a, h, dh, pvec, wa16, wx16):
    s = h0a.shape[0]
    w = D_MODEL
    t = min(LRU_T, s)
    nb = s // t
    hb = t // HALO

    def body(ax_ref, axh_ref, h_ref, hh_ref, dh_ref, pv_ref, wa_ref, wx_ref,
             dax_ref, dwa_ref, dwx_ref, dpv_ref, ext_ref, hext_ref, dcext_ref, a_ref, g_ref, gc_ref):
        i = pl.program_id(0)
        blk = nb - 1 - i

        @pl.when(i == 0)
        def _():
            dwa_ref[...] = jnp.zeros_like(dwa_ref)
            dwx_ref[...] = jnp.zeros_like(dwx_ref)
            dpv_ref[...] = jnp.zeros_like(dpv_ref)
            gc_ref[...] = jnp.zeros((1, w), F32)
            dcext_ref[pl.ds(t, HALO), :] = jnp.zeros((HALO, w), F32)

        pv = pv_ref[...]
        ax = ax_ref[...]
        keep = (blk > 0).astype(F32)
        ext_ref[pl.ds(0, HALO), :] = axh_ref[...] * keep
        ext_ref[pl.ds(HALO, t), :] = ax
        hext_ref[pl.ds(0, HALO), :] = hh_ref[...] * keep
        hext_ref[pl.ds(HALO, t), :] = h_ref[...]
        c = _conv(ext_ref, ax, pv, t)
        c16, r, ig, ls, _, a, a2, m = _lru_gates(c, wa_ref, wx_ref, pv)
        a_ref[...] = a

        def steps(k, carry):
            base = pl.multiple_of(t - HALO - k * HALO, HALO)
            for r in reversed(range(HALO)):
                g = dh_ref[pl.ds(base + r, 1), :] + carry
                g_ref[pl.ds(base + r, 1), :] = g
                carry = a_ref[pl.ds(base + r, 1), :] * g
            return carry

        gc_ref[...] = lax.fori_loop(0, t // HALO, steps, gc_ref[...])
        g = g_ref[...]
        hprev = hext_ref[pl.ds(HALO - 1, t), :]
        gm = g * m
        d_la = g * hprev * a - (g * ig * c) * a2 / m
        d_pr = d_la * (LRU_C * ls) * r * (1.0 - r)
        d_pi = gm * c * ig * (1.0 - ig)
        dc = gm * ig
        dpr16 = d_pr.astype(BF16)
        dpi16 = d_pi.astype(BF16)
        dc_parts = []
        for n in range(LRU_BLOCKS):
            sl = slice(n * LRU_BLOCK, (n + 1) * LRU_BLOCK)
            dwa_ref[n] += _dot(c16[:, sl], dpr16[:, sl], TN)
            dwx_ref[n] += _dot(c16[:, sl], dpi16[:, sl], TN)
            dc_parts.append(_dot(dpr16[:, sl], wa_ref[n], NT) + _dot(dpi16[:, sl], wx_ref[n], NT))
        dc = dc + jnp.concatenate(dc_parts, axis=1)
        dcext_ref[pl.ds(0, t), :] = dc
        dax = (pv[3:4] * dc + pv[2:3] * dcext_ref[pl.ds(1, t), :] + pv[1:2] * dcext_ref[pl.ds(2, t), :]
               + pv[0:1] * dcext_ref[pl.ds(3, t), :])
        dax_ref[...] = dax.astype(BF16)
        dcext_ref[pl.ds(t, HALO), :] = dc[:HALO, :]
        sums = [_colsum(dc * ext_ref[pl.ds(HALO - 3 + kk, t), :]) for kk in range(4)]
        sums += [_colsum(dc), _colsum(d_pr), _colsum(d_pi), _colsum(d_la * (LRU_C * r))]
        for kk, part in enumerate(sums):
            dpv_ref[pl.ds(kk, 1), :] += part

        @pl.when(i == nb - 1)
        def _():
            lam = pv[7:8]
            dls = dpv_ref[pl.ds(7, 1), :]
            dpv_ref[pl.ds(7, 1), :] = dls * _sigmoid(-lam)

    full = lambda shp: pl.BlockSpec(shp, lambda i: (0,) * len(shp))
    rev = lambda cb: pl.BlockSpec((t, w), functools.partial(lambda i, cb: (nb - 1 - i, cb), cb=cb))
    halo = lambda cb: pl.BlockSpec(
        (HALO, w), functools.partial(lambda i, cb: (jnp.maximum((nb - 1 - i) * hb - 1, 0), cb), cb=cb))
    gw = (LRU_BLOCKS, LRU_BLOCK, LRU_BLOCK)
    return pl.pallas_call(
        body, name="rglru_bwd", grid=(nb,),
        in_specs=[rev(0), halo(0), rev(0), halo(0), rev(0), full((8, w)), full(gw), full(gw)],
        out_specs=[rev(0), full(gw), full(gw), full((8, w))],
        out_shape=[jax.ShapeDtypeStruct((s, w), BF16), jax.ShapeDtypeStruct(gw, F32),
                   jax.ShapeDtypeStruct(gw, F32), jax.ShapeDtypeStruct((8, w), F32)],
        scratch_shapes=[pltpu.VMEM((t + HALO, w), F32), pltpu.VMEM((t + HALO, w), F32),
                        pltpu.VMEM((t + HALO, w), F32), pltpu.VMEM((t, w), F32), pltpu.VMEM((t, w), F32),
                        pltpu.VMEM((1, w), F32)],
        compiler_params=_cparams(("arbitrary",)),
    )(h0a, h0a, h, h, dh, pvec, wa16, wx16)


SB_T = 256
SB_SUB = 8


def _split16(x):
    hi = x.astype(BF16)
    lo = (x - hi.astype(F32)).astype(BF16)
    return jnp.concatenate([hi, lo], axis=0)


def _sb_scores(q, k, scale, causal):
    z = _dot(q, k, NT) * scale
    log1mb = -(jnp.maximum(z, 0.0) + jnp.log(1.0 + jnp.exp(-jnp.abs(z))))
    if causal is not None:
        log1mb = jnp.where(causal, log1mb, 0.0)
    return z, log1mb


def _sb_cumsum(log1mb, tri):
    tq = log1mb.shape[0]
    cs = _dot(_split16(log1mb), tri, NN)
    return cs[:tq] + cs[tq:]


def _sb_weights(z, cs, run, causal):
    wgt = jnp.exp(z + cs + run)
    if causal is not None:
        wgt = jnp.where(causal, wgt, 0.0)
    return wgt


def _sb_tile(q, k, scale, tri, run, causal):
    z, log1mb = _sb_scores(q, k, scale, causal)
    cs = _sb_cumsum(log1mb, tri)
    return z, _sb_weights(z, cs, run, causal), run + cs[:, 0:1]


SB_DEAD = -105.0


def _sb_alive(run):
    return (jnp.max(run) > SB_DEAD).astype(jnp.int32)


def _sb_more(carry):
    return (carry[0] >= 0) & (carry[1] > 0)


def _sb_fwd(qkv, bg):
    s = qkv.shape[0]
    t = min(SB_T, s)
    nq = s // t
    dh = SB_HEAD_DIM
    scale = 1.0 / math.sqrt(dh)

    sub = SB_SUB if nq % SB_SUB == 0 else 1

    def body(q_ref, k_ref, v_ref, bg_ref, o_ref, y_ref):
        row = lax.broadcasted_iota(jnp.int32, (t, t), 0)
        col = lax.broadcasted_iota(jnp.int32, (t, t), 1)
        tri = jnp.where(row >= col, 1.0, 0.0).astype(BF16)
        causal = col < row
        def tile(q, kb, run, mask):
            off = pl.multiple_of(kb * t, t)
            _, wgt, run = _sb_tile(q, k_ref[pl.ds(off, t), :], scale, tri, run, mask)
            return run, _dot(wgt.astype(BF16), v_ref[pl.ds(off, t), :], NN)

        def rows_of(ref, kb):
            return ref[pl.ds(pl.multiple_of(kb * t, t), t), :]

        iqs = [pl.program_id(1) * sub + u for u in range(sub)]
        qs = [q_ref[pl.ds(u * t, t), :] for u in range(sub)]
        lefts = [jnp.maximum(iq - 1, 0) for iq in iqs]
        scores = [(_sb_scores(q, rows_of(k_ref, iq), scale, causal), _sb_scores(q, rows_of(k_ref, lf), scale, None))
                  for q, iq, lf in zip(qs, iqs, lefts)]
        sums = [(_sb_cumsum(d[1], tri), _sb_cumsum(l[1], tri)) for d, l in scores]
        blocks = []
        for q, iq, lf, (sd, sl), (cd, cl) in zip(qs, iqs, lefts, scores, sums):
            run_d = cd[:, 0:1]
            acc_d = _dot(_sb_weights(sd[0], cd, 0.0, causal).astype(BF16), rows_of(v_ref, iq), NN)
            acc_l = _dot(_sb_weights(sl[0], cl, run_d, None).astype(BF16), rows_of(v_ref, lf), NN)
            blocks.append((iq, q, jnp.where(iq > 0, run_d + cl[:, 0:1], run_d), acc_d + jnp.where(iq > 0, acc_l, 0.0)))
        for u, (iq, q, run, acc) in enumerate(blocks):
            def loop(carry, q=q):
                kb, _, run, acc = carry
                run, part = tile(q, kb, run, None)
                return kb - 1, _sb_alive(run), run, acc + part

            _, _, run, acc = lax.while_loop(_sb_more, loop, (iq - 2, _sb_alive(run), run, acc))
            rows = pl.ds(u * t, t)
            o_ref[rows, :] = acc
            g = bg_ref[rows, :]
            y_ref[rows, :] = (acc * (g * _sigmoid(g))).astype(BF16)

    blk = lambda off: pl.BlockSpec((sub * t, dh), functools.partial(lambda h, i, off: (i, h + off), off=off))
    col = lambda off: pl.BlockSpec((s, dh), functools.partial(lambda h, i, off: (0, h + off), off=off))
    return pl.pallas_call(
        body, name="sb_fwd", grid=(SB_HEADS, nq // sub),
        in_specs=[blk(0), col(SB_HEADS), col(2 * SB_HEADS), blk(0)],
        out_specs=[blk(0), blk(0)],
        out_shape=[jax.ShapeDtypeStruct((s, SB_HEADS * dh), F32), jax.ShapeDtypeStruct((s, SB_HEADS * dh), BF16)],
        compiler_params=_cparams(("parallel", "arbitrary")),
    )(qkv, qkv, qkv, bg)


def _sb_bwd(qkv, dob):
    s = qkv.shape[0]
    t = min(SB_T, s)
    nq = s // t
    dh = SB_HEAD_DIM
    scale = 1.0 / math.sqrt(dh)

    sub = SB_SUB if nq % SB_SUB == 0 else 1

    def body(q_ref, k_ref, v_ref, do_ref, dq_ref, dk_out, dv_out, e_ref, b_ref, pe_ref, pb_ref, dk_ref, dv_ref):
        @pl.when(pl.program_id(1) == 0)
        def _():
            dk_ref[...] = jnp.zeros_like(dk_ref)
            dv_ref[...] = jnp.zeros_like(dv_ref)

        row = lax.broadcasted_iota(jnp.int32, (t, t), 0)
        col = lax.broadcasted_iota(jnp.int32, (t, t), 1)
        tri = jnp.where(row >= col, 1.0, 0.0).astype(BF16)
        tri_x = jnp.where(row < col, 1.0, 0.0).astype(BF16)
        causal = col < row

        def sweep1(q, do, kb, e_dst, b_dst, run, mask, live=None):
            off = pl.multiple_of(kb * t, t)
            v = v_ref[pl.ds(off, t), :]
            z, wgt, run = _sb_tile(q, k_ref[pl.ds(off, t), :], scale, tri, run, mask)
            if live is not None:
                wgt = jnp.where(live, wgt, 0.0)
            e_dst[...] = wgt * _dot(do, v, NT)
            b_dst[...] = _sigmoid(z)
            dv_ref[pl.ds(off, t), :] += _dot(wgt.astype(BF16), do, TN)
            return run

        def sweep2(q, kb, e_src, b_src, pre, mask, live=None):
            off = pl.multiple_of(kb * t, t)
            e = e_src[...]
            beta = b_src[...]
            ps = _dot(_split16(e), tri_x, NN)
            ps = ps[:t] + ps[t:]
            dz = e * (1.0 - beta) - beta * (ps + pre)
            if mask is not None:
                dz = jnp.where(mask, dz, 0.0)
            if live is not None:
                dz = jnp.where(live, dz, 0.0)
            dz16 = (dz * scale).astype(BF16)
            dk_ref[pl.ds(off, t), :] += _dot(dz16, q, TN)
            return pre + ps[:, t - 1:t] + e[:, t - 1:t], _dot(dz16, k_ref[pl.ds(off, t), :], NN)

        blocks = []
        for u in range(sub):
            iq = pl.program_id(1) * sub + u
            q, do = q_ref[pl.ds(u * t, t), :], do_ref[pl.ds(u * t, t), :]
            left = jnp.maximum(iq - 1, 0)
            run_d = sweep1(q, do, iq, pe_ref.at[u, 0], pb_ref.at[u, 0], jnp.zeros((t, 1), F32), causal)
            run_l = sweep1(q, do, left, pe_ref.at[u, 1], pb_ref.at[u, 1], run_d, None, iq > 0)
            blocks.append((iq, q, do, left, jnp.where(iq > 0, run_l, run_d)))
        carried = []
        for iq, q, do, left, run in blocks:
            def loop1(carry, q=q, do=do):
                run = sweep1(q, do, carry[0], e_ref.at[carry[0]], b_ref.at[carry[0]], carry[2], None)
                return carry[0] - 1, _sb_alive(run), run

            first = lax.while_loop(_sb_more, loop1, (iq - 2, _sb_alive(run), run))[0] + 1

            def loop2(kb, carry, q=q):
                pre, part = sweep2(q, kb, e_ref.at[kb], b_ref.at[kb], carry[0], None)
                return pre, carry[1] + part

            carried.append(lax.fori_loop(first, iq - 1, loop2, (jnp.zeros((t, 1), F32), jnp.zeros((t, dh), F32))))
        for u, ((iq, q, do, left, _), (pre, dq)) in enumerate(zip(blocks, carried)):
            pre, dq_l = sweep2(q, left, pe_ref.at[u, 1], pb_ref.at[u, 1], pre, None, iq > 0)
            _, dq_d = sweep2(q, iq, pe_ref.at[u, 0], pb_ref.at[u, 0], pre, causal)
            dq_ref[pl.ds(u * t, t), :] = (dq + dq_l + dq_d).astype(BF16)

        @pl.when(pl.program_id(1) == nq // sub - 1)
        def _():
            dk_out[...] = dk_ref[...].astype(BF16)
            dv_out[...] = dv_ref[...].astype(BF16)

    blk = lambda off: pl.BlockSpec((sub * t, dh), functools.partial(lambda h, i, off: (i, h + off), off=off))
    col = lambda off: pl.BlockSpec((s, dh), functools.partial(lambda h, i, off: (0, h + off), off=off))
    wide = SB_HEADS * dh
    return pl.pallas_call(
        body, name="sb_bwd", grid=(SB_HEADS, nq // sub),
        in_specs=[blk(0), col(SB_HEADS), col(2 * SB_HEADS), blk(0)],
        out_specs=[blk(0), col(0), col(0)],
        out_shape=[jax.ShapeDtypeStruct((s, wide), BF16)] * 3,
        scratch_shapes=[pltpu.VMEM((nq, t, t), F32), pltpu.VMEM((nq, t, t), F32),
                        pltpu.VMEM((sub, 2, t, t), F32), pltpu.VMEM((sub, 2, t, t), F32),
                        pltpu.VMEM((s, dh), F32), pltpu.VMEM((s, dh), F32)],
        compiler_params=_cparams(("parallel", "arbitrary")),
    )(qkv, qkv, qkv, dob)


def _alibi_slope(h):
    return float(2.0 ** (-8.0 * (h + 1) / C_HEADS))


GROUP_ROWS = C_GROUP * WINDOW


def _swa_window(n):
    qb = WINDOW
    i = lax.broadcasted_iota(jnp.int32, (qb, 2 * qb), 0)
    j = lax.broadcasted_iota(jnp.int32, (qb, 2 * qb), 1)
    d = i - j + qb
    valid = (d >= 0) & (d < WINDOW) & ((j >= qb) | (n > 0))
    stacked = lambda x: jnp.concatenate([x] * C_GROUP, axis=0)
    return stacked(d.astype(F32)), stacked(jnp.where(valid, 0.0, -1e30))


def _swa_group_cols(c, sink_ref):
    head = lax.shift_right_logical(lax.broadcasted_iota(jnp.int32, (GROUP_ROWS, 1), 0), WINDOW.bit_length() - 1)
    slope = jnp.zeros((GROUP_ROWS, 1), F32)
    sink = jnp.zeros((GROUP_ROWS, 1), F32)
    for hh in range(C_GROUP):
        slope = jnp.where(head == hh, _alibi_slope(c * C_GROUP + hh), slope)
        sink = jnp.where(head == hh, sink_ref[c * C_GROUP + hh], sink)
    return slope, sink


def _swa_probs(qg, kw, sink, slope, dist, bias, scale):
    return _swa_softmax(_dot(qg, kw, NT), sink, slope, dist, bias, scale)


def _swa_softmax(raw, sink, slope, dist, bias, scale):
    sc = raw * scale - slope * dist + bias
    m = jnp.maximum(jnp.max(sc, axis=-1, keepdims=True), sink)
    p = jnp.exp(sc - m)
    ps = jnp.exp(sink - m)
    inv = 1.0 / (jnp.sum(p, axis=-1, keepdims=True) + ps)
    return p * inv, ps * inv


def _stack_heads(x, c):
    hd = C_HEAD_DIM
    return jnp.concatenate([x[:, (c * C_GROUP + hh) * hd:(c * C_GROUP + hh + 1) * hd] for hh in range(C_GROUP)], axis=0)


SWA_SUB = 4


def _swa_kv_specs(sub, kvw):
    return [pl.BlockSpec((WINDOW, 2 * kvw), functools.partial(lambda n, w: (n * sub + w, 0), w=w)) for w in range(sub + 1)]


def _swa_fwd(q16, gate, kvp, sinks):
    s = q16.shape[0]
    qb = WINDOW
    hd = C_HEAD_DIM
    scale = 1.0 / math.sqrt(hd)
    kvw = C_KV_HEADS * hd
    sub = SWA_SUB if (s // qb) % SWA_SUB == 0 else 1

    def body(sink_ref, q_ref, g_ref, *refs):
        kv_refs, (o_ref, y_ref) = refs[:sub + 1], refs[sub + 1:]
        units = []
        for u in range(sub):
            kv = jnp.concatenate([kv_refs[u][...], kv_refs[u + 1][...]], axis=0)
            q = q_ref[slice(u * qb, (u + 1) * qb), :]
            for c in range(C_KV_HEADS):
                raw = _dot(_stack_heads(q, c), kv[:, c * hd:(c + 1) * hd], NT)
                units.append((u, c, raw, kv[:, kvw + c * hd:kvw + (c + 1) * hd]))
        windows = [_swa_window(pl.program_id(0) * sub + u) for u in range(sub)]
        probs = []
        for u, c, raw, vw in units:
            slope, sink = _swa_group_cols(c, sink_ref)
            probs.append(_swa_softmax(raw, sink, slope, windows[u][0], windows[u][1], scale)[0])
        for (u, c, raw, vw), p in zip(units, probs):
            og = _dot(p.astype(BF16), vw, NN)
            for hh in range(C_GROUP):
                h = c * C_GROUP + hh
                o_ref[slice(u * qb, (u + 1) * qb), h * hd:(h + 1) * hd] = og[hh * qb:(hh + 1) * qb, :]
        g = g_ref[...]
        y_ref[...] = (o_ref[...] * (g * _sigmoid(g))).astype(BF16)

    wide = C_HEADS * hd
    big = pl.BlockSpec((sub * qb, wide), lambda n: (n, 0))
    return pl.pallas_call(
        body, name="swa_fwd", grid=(s // (sub * qb),),
        in_specs=[pl.BlockSpec(memory_space=pltpu.SMEM), big, big] + _swa_kv_specs(sub, kvw),
        out_specs=[big, big],
        out_shape=[jax.ShapeDtypeStruct((s, wide), F32), jax.ShapeDtypeStruct((s, wide), BF16)],
        compiler_params=_cparams(("arbitrary",)),
    )(sinks, q16, gate, *([kvp] * (sub + 1)))


def _swa_bwd(q16, do16, kvp, sinks):
    s = q16.shape[0]
    qb = WINDOW
    hd = C_HEAD_DIM
    scale = 1.0 / math.sqrt(hd)
    kvw = C_KV_HEADS * hd
    sub = 1
    nstep = s // (sub * qb)

    def body(sink_ref, q_ref, do_ref, *refs):
        kv_refs, (dq_ref, dkv_ref, ds_ref, sacc_ref) = refs[:sub + 1], refs[sub + 1:]

        @pl.when(pl.program_id(0) == 0)
        def _():
            dkv_ref[...] = jnp.zeros_like(dkv_ref)
            sacc_ref[...] = jnp.zeros_like(sacc_ref)

        for u in range(sub):
            rows = slice(u * qb, (u + 1) * qb)
            n = pl.program_id(0) * sub + u
            q = q_ref[rows, :]
            do = do_ref[rows, :]
            kv = jnp.concatenate([kv_refs[u][...], kv_refs[u + 1][...]], axis=0)
            off = pl.multiple_of(n * qb, qb)
            dist, bias = _swa_window(n)
            groups = []
            for c in range(C_KV_HEADS):
                kw = kv[:, c * hd:(c + 1) * hd]
                qg, dog = _stack_heads(q, c), _stack_heads(do, c)
                groups.append((c, kw, qg, dog, _dot(qg, kw, NT), _dot(dog, kv[:, kvw + c * hd:kvw + (c + 1) * hd], NT)))
            chains = []
            for c, kw, qg, dog, raw, dp in groups:
                slope, sink = _swa_group_cols(c, sink_ref)
                p, ps = _swa_softmax(raw, sink, slope, dist, bias, scale)
                dd = jnp.sum(p * dp, axis=-1, keepdims=True)
                sacc_ref[c] += ps * dd
                chains.append(((p * (dp - dd) * scale).astype(BF16), p.astype(BF16)))
            for (c, kw, qg, dog, _, _), (ds16, p16) in zip(groups, chains):
                dqg = _dot(ds16, kw, NN).astype(BF16)
                for hh in range(C_GROUP):
                    h = c * C_GROUP + hh
                    dq_ref[rows, h * hd:(h + 1) * hd] = dqg[hh * qb:(hh + 1) * qb, :]
                dkv_ref[pl.ds(off, 2 * qb), c * hd:(c + 1) * hd] += _dot(ds16, qg, TN)
                dkv_ref[pl.ds(off, 2 * qb), kvw + c * hd:kvw + (c + 1) * hd] += _dot(p16, dog, TN)

        @pl.when(pl.program_id(0) == nstep - 1)
        def _():
            lane = lax.broadcasted_iota(jnp.int32, (1, 128), 1)
            row = jnp.zeros((1, 128), F32)
            for c in range(C_KV_HEADS):
                for hh in range(C_GROUP):
                    tot = jnp.sum(sacc_ref[c, pl.ds(hh * qb, qb), :], axis=0, keepdims=True)
                    row = jnp.where(lane == c * C_GROUP + hh, -tot, row)
            ds_ref[...] = row

    wide = C_HEADS * hd
    big = pl.BlockSpec((sub * qb, wide), lambda n: (n, 0))
    return pl.pallas_call(
        body, name="swa_bwd", grid=(nstep,),
        in_specs=[pl.BlockSpec(memory_space=pltpu.SMEM), big, big] + _swa_kv_specs(sub, kvw),
        out_specs=[big,
                   pl.BlockSpec((s + qb, 2 * kvw), lambda n: (0, 0)), pl.BlockSpec((1, 128), lambda n: (0, 0))],
        out_shape=[jax.ShapeDtypeStruct((s, wide), BF16), jax.ShapeDtypeStruct((s + qb, 2 * kvw), F32),
                   jax.ShapeDtypeStruct((1, 128), F32)],
        scratch_shapes=[pltpu.VMEM((C_KV_HEADS, GROUP_ROWS, 1), F32)],
        compiler_params=_cparams(("arbitrary",)),
    )(sinks, q16, do16, *([kvp] * (sub + 1)))


def _adamw(parts, w, m, v, *, name, tr=496):
    npart, r, c = parts.shape
    tr = min(tr, r)
    assert r % tr == 0 and tr % 16 == 0
    c1 = 1.0 / (1.0 - ADAM_B1 ** ADAM_STEP)
    c2 = 1.0 / (1.0 - ADAM_B2 ** ADAM_STEP)

    def body(p_ref, w_ref, m_ref, v_ref, g_ref, d_ref, nm_ref, nv_ref):
        g = p_ref[0].astype(F32)
        for j in range(1, npart):
            g = g + p_ref[j].astype(F32)
        nm = ADAM_B1 * m_ref[...] + (1.0 - ADAM_B1) * g
        nv = ADAM_B2 * v_ref[...] + (1.0 - ADAM_B2) * (g * g)
        g_ref[...] = g
        nm_ref[...] = nm
        nv_ref[...] = nv
        d_ref[...] = -ADAM_LR * ((nm * c1) / (jnp.sqrt(nv * c2) + ADAM_EPS) + ADAM_WD * w_ref[...])

    spec = pl.BlockSpec((tr, c), lambda i: (i, 0))
    return pl.pallas_call(
        body, name=name, grid=(r // tr,),
        in_specs=[pl.BlockSpec((npart, tr, c), lambda i: (0, i, 0)), spec, spec, spec],
        out_specs=[spec] * 4, out_shape=[jax.ShapeDtypeStruct((r, c), F32)] * 4,
        compiler_params=_cparams(("parallel",)),
    )(parts, w, m, v)


GATHER = "gather"
A2A = "a2a"
GATHER_COLS = "gather_cols"
A2A_COLS = "a2a_cols"


def _gather_two_level(bufs, modes, x, *, tr=1024, name):
    nb = len(bufs)
    rows, cols = x.shape
    tr = min(tr, rows)
    steps = rows // tr

    def body(*refs):
        x_ref, src, x16_ref, dst = refs[0], refs[1:1 + nb], refs[1 + nb], refs[2 + nb:2 + 2 * nb]
        send_sems, recv_sems, local_sems = refs[2 + 2 * nb:]
        cx, cy, cc = lax.axis_index("x"), lax.axis_index("y"), lax.axis_index("c")
        me = 4 * cx + 2 * cy + cc
        here, sibling = (cx, cy, cc), (cx, cy, 1 - cc)

        def copy(b, k, origin, to, source):
            return pltpu.make_async_remote_copy(
                src_ref=source, dst_ref=_slot(dst[b], modes[b], origin), send_sem=send_sems.at[b, k - 1],
                recv_sem=recv_sems.at[b, k - 1], device_id=to, device_id_type=pl.DeviceIdType.MESH)

        def first_copies():
            local = [pltpu.make_async_copy(src[b], _slot(dst[b], modes[b], me), local_sems.at[b]) for b in range(nb)]
            sends = [copy(b, k, me, (cx ^ (k >> 2), cy ^ ((k >> 1) & 1), cc ^ (k & 1)), src[b])
                     for k in (1, 2, 4, 6) for b in range(nb)]
            return local, sends

        @pl.when(pl.program_id(0) == 0)
        def _():
            local, sends = first_copies()
            for cp in local + sends:
                cp.start()

        x16_ref[...] = x_ref[...].astype(BF16)

        @pl.when(pl.program_id(0) == steps - 1)
        def _():
            local, sends = first_copies()
            for j in (2, 4, 6):
                for b in range(nb):
                    copy(b, j, me ^ j, here, src[b]).wait_recv()
                    passed = copy(b, j ^ 1, me ^ j, sibling, _slot(dst[b], modes[b], me ^ j))
                    passed.start()
                    sends.append(passed)
            for k in (1, 3, 5, 7):
                for b in range(nb):
                    copy(b, k, me ^ k, here, src[b]).wait_recv()
            for cp in sends:
                cp.wait_send()
            for cp in local:
                cp.wait()

    hbm = pl.BlockSpec(memory_space=pl.ANY)
    tile = pl.BlockSpec((tr, cols), lambda i: (i, 0))
    return pl.pallas_call(
        body, name=name, grid=(steps,), in_specs=[tile] + [hbm] * nb, out_specs=[tile] + [hbm] * nb,
        out_shape=[jax.ShapeDtypeStruct(x.shape, BF16)] + _exchange_out_shapes(bufs, modes),
        scratch_shapes=_exchange_sems(nb),
        compiler_params=pltpu.CompilerParams(dimension_semantics=("arbitrary",), vmem_limit_bytes=VMEM_LIMIT,
                                             has_side_effects=True),
    )(x, *bufs)


def _exchange_out_shapes(bufs, gather):
    shapes = {GATHER: lambda s: (N_DEV,) + s, A2A: lambda s: s, GATHER_COLS: lambda s: (s[0], N_DEV * s[1]),
              A2A_COLS: lambda s: (N_DEV, s[0], s[1] // N_DEV)}
    return [jax.ShapeDtypeStruct(shapes[g](tuple(b.shape)), b.dtype) for b, g in zip(bufs, gather)]


def _sent(src, mode, peer):
    if mode == A2A:
        return src.at[peer]
    if mode == A2A_COLS:
        w = src.shape[1] // N_DEV
        return src.at[:, pl.ds(pl.multiple_of(peer * w, 128), w)]
    return src


def _slot(dst, mode, dev):
    if mode == GATHER_COLS:
        w = dst.shape[1] // N_DEV
        return dst.at[:, pl.ds(pl.multiple_of(dev * w, 128), w)]
    return dst.at[dev]


def _exchange_sems(nb):
    return [pltpu.SemaphoreType.DMA((nb, N_DEV - 1)), pltpu.SemaphoreType.DMA((nb, N_DEV - 1)),
            pltpu.SemaphoreType.DMA((nb,))]


def _exchange_copies(src, dst, gather, sems):
    send_sems, recv_sems, local_sems = sems
    x, y, c = lax.axis_index("x"), lax.axis_index("y"), lax.axis_index("c")
    me = 4 * x + 2 * y + c
    local, sends, recvs = [], [], []
    for b in range(len(src)):
        mine = _sent(src[b], gather[b], me)
        local.append(pltpu.make_async_copy(mine, _slot(dst[b], gather[b], me), local_sems.at[b]))
        for k in range(1, N_DEV):
            px, py, pc = x ^ (k >> 2), y ^ ((k >> 1) & 1), c ^ (k & 1)
            peer = 4 * px + 2 * py + pc
            pair = dict(send_sem=send_sems.at[b, k - 1], recv_sem=recv_sems.at[b, k - 1],
                        device_id_type=pl.DeviceIdType.MESH)
            sends.append(pltpu.make_async_remote_copy(
                src_ref=_sent(src[b], gather[b], peer), dst_ref=_slot(dst[b], gather[b], me), device_id=(px, py, pc),
                **pair))
            recvs.append(pltpu.make_async_remote_copy(
                src_ref=mine, dst_ref=_slot(dst[b], gather[b], peer), device_id=(x, y, c), **pair))
    return local, sends, recvs


def _exchange_start(src, dst, gather, sems):
    local, sends, _ = _exchange_copies(src, dst, gather, sems)
    for cp in local + sends:
        cp.start()


def _exchange_wait(src, dst, gather, sems):
    local, sends, recvs = _exchange_copies(src, dst, gather, sems)
    for cp in sends:
        cp.wait_send()
    for cp in recvs:
        cp.wait_recv()
    for cp in local:
        cp.wait()


R_SMALL = 16
REPL_ROWS = 272
REPL_LEN = REPL_ROWS * D_MODEL


def _small_block(parts):
    flat = jnp.concatenate(parts, axis=-1)
    lead = flat.ndim - 1
    return jnp.pad(flat[..., None, :], [(0, 0)] * lead + [(0, R_SMALL - 1), (0, D_MODEL - flat.shape[-1])])


def _small_vectors(conv_w, ln_g, ln_b):
    return _small_block([conv_w.reshape(512), ln_g.reshape(128), ln_b.reshape(128)])


REPL_SHAPES = ((1, 8, 128, 128), (1, 8, 128, 128), (1, 1024), (1, 1024), (1, 1024), (1, 1024), (1, 1024), (1, 1024),
               (1, 16))


def _pack_repl(parts):
    flat = jnp.concatenate([p.reshape(-1) for p in parts])
    return jnp.concatenate([flat, jnp.zeros((REPL_LEN - flat.shape[0],), F32)]).reshape(REPL_ROWS, D_MODEL)


def _unpack_repl(p):
    flat = p.reshape(-1)
    out, o = [], 0
    for shp in REPL_SHAPES:
        n = math.prod(shp)
        out.append(flat[o:o + n].reshape(shp))
        o += n
    return out


def kernel(x, e_w_in, e_conv_w, e_conv_b, e_w_gate_a, e_b_gate_a, e_w_gate_x, e_b_gate_x, e_lru_lambda, e_w_out, e_ln_g, e_ln_b, o_w_in, o_sinks, o_w_out, o_ln_g, o_ln_b, loss_target, m_e_w_in, m_e_conv_w, m_e_conv_b, m_e_w_gate_a, m_e_b_gate_a, m_e_w_gate_x, m_e_b_gate_x, m_e_lru_lambda, m_e_w_out, m_e_ln_g, m_e_ln_b, m_o_w_in, m_o_sinks, m_o_w_out, m_o_ln_g, m_o_ln_b, v_e_w_in, v_e_conv_w, v_e_conv_b, v_e_w_gate_a, v_e_b_gate_a, v_e_w_gate_x, v_e_b_gate_x, v_e_lru_lambda, v_e_w_out, v_e_ln_g, v_e_ln_b, v_o_w_in, v_o_sinks, v_o_w_out, v_o_ln_g, v_o_ln_b):
    d = D_MODEL
    x0 = x[0]
    target = loss_target[0]
    s = x0.shape[0]

    spack = _small_block([e_conv_w.reshape(512), o_ln_g.reshape(128), o_ln_b.reshape(128)])
    x0_16, wi_e, sall = _gather_two_level([e_w_in[0].astype(BF16), spack], [GATHER_COLS, GATHER], x0,
                                          name="gather_w_in")
    conv_w = sall[:, 0, 0:512].reshape(N_DEV, 4, 128).transpose(1, 0, 2).reshape(4, d)
    ln_g_o = sall[:, 0, 512:640].reshape(1, d)
    ln_b_o = sall[:, 0, 640:768].reshape(1, d)
    pvec = jnp.concatenate([conv_w, e_conv_b, e_b_gate_a, e_b_gate_x, e_lru_lambda], axis=0)
    wa16 = e_w_gate_a[0].astype(BF16)
    wx16 = e_w_gate_x[0].astype(BF16)
    sinks = o_sinks[0]

    h0a, wall_out_e = _matmul(x0_16, wi_e, mode="nn", n_out=2 * d, b_off=0, name="l0_in_a",
                              comm=([e_w_out[0].astype(BF16)], [GATHER]))
    qkv, wall_in_o = _matmul(x0_16, wi_e, mode="nn", n_out=3 * d, b_off=2 * d, out_dtype=BF16, name="l0_in_qkv",
                             comm=([o_w_in[0].astype(BF16)], [GATHER]))
    bg, wall_out_o = _matmul(x0_16, wi_e, mode="nn", n_out=d, b_off=5 * d, name="l0_in_bg",
                             comm=([o_w_out[0].astype(BF16)], [GATHER]))
    wo_e = wall_out_e.reshape(2 * d, d)
    wi_o = wall_in_o.transpose(1, 0, 2).reshape(d, 2304)
    wi_o = jnp.concatenate([wi_o[:, :1024], wi_o[:, 1280:], wi_o[:, 1024:1280]], axis=1)
    wo_o = wall_out_o.reshape(d, d)
    ya, hst = _rglru_fwd(h0a, pvec, wa16, wx16)
    ob, yb = _sb_fwd(qkv, bg)
    z0, x1, x1_16 = _matmul([ya, yb], wo_e, mode="nn", tm=512, name="l0_out",
                            epi=(_epi_ln_fwd, [(x0, 0)], [e_ln_g, e_ln_b], [F32, F32, BF16], 0))

    q1 = _matmul(x1_16, wi_o, mode="nn", n_out=d, b_off=0, out_dtype=BF16, name="l1_in_q")
    g1 = _matmul(x1_16, wi_o, mode="nn", n_out=d, b_off=d, name="l1_in_g")
    kv1 = _matmul(x1_16, wi_o, mode="nn", n_out=256, b_off=2 * d, tn=256, out_dtype=BF16, name="l1_in_kv")
    kvp = jnp.concatenate([jnp.zeros((WINDOW, 256), BF16), kv1], axis=0)
    o1, y1 = _swa_fwd(q1, g1, kvp, sinks)
    dz1, dz1_16, loss_cols, dg_o, db_o = _matmul(
        y1, wo_o, mode="nn", tm=512, name="l1_out",
        epi=(_epi_ln_loss_bwd, [(x1, 0), (target, 0)], [ln_g_o, ln_b_o], [F32, BF16], 3))
    loss_hi = jnp.sum(loss_cols).astype(BF16).astype(F32)
    loss_terms = jnp.stack([loss_hi, jnp.sum(loss_cols) - loss_hi]).reshape(1, 2)

    dwo_o = _matmul(y1, dz1_16, mode="tn", out_dtype=BF16, name="l1_dwout")
    do1, dg1, parts_out_o = _matmul(dz1_16, wo_o, mode="nt", tm=512, name="l1_dy",
                                    comm=([dwo_o.reshape(N_DEV, 128, d)], [A2A]),
                                    epi=(_epi_gate_bwd, [(o1, 0), (g1, 0)], [], [BF16, BF16], 0))
    dq1, dkvp, dsink = _swa_bwd(q1, do1, kvp, sinks)
    dkv1 = dkvp[WINDOW:].astype(BF16)
    dh1 = [dq1, dg1, dkv1]
    dw_qg = _matmul(x1_16, [dq1, dg1], mode="tn", out_dtype=BF16, name="l1_dwin_qg")
    dw_kv = _matmul(x1_16, dkv1, mode="tn", out_dtype=BF16, name="l1_dwin_kv")
    dwi_o = jnp.concatenate([dw_qg[:, :1024], dw_kv, dw_qg[:, 1024:]], axis=1)
    dz0, dz0_16, dg_e, db_e, parts_in_o = _matmul(
        dh1, wi_o, mode="nt", tm=512, tk=2304, name="l1_dx",
        comm=([dwi_o.reshape(d, N_DEV, 288).transpose(1, 0, 2)], [A2A]),
        epi=(_epi_ln_bwd, [(dz1, 0), (z0, 0)], [e_ln_g], [F32, BF16], 2))

    dwo_e = _matmul([ya, yb], dz0_16, mode="tn", out_dtype=BF16, name="l0_dwout")
    dhst, dag, parts_out_e = _matmul(
        dz0_16, wo_e, mode="nt", tm=512, n_out=d, b_off=0, name="l0_dy_a",
        comm=([dwo_e.reshape(N_DEV, 256, d)], [A2A]),
        epi=(_epi_gate_bwd, [(hst, 0), (h0a, d)], [], [F32, BF16], 0))
    dob, dbg = _matmul(dz0_16, wo_e, mode="nt", tm=512, n_out=d, b_off=d, name="l0_dy_b",
                       epi=(_epi_gate_bwd, [(ob, 0), (bg, 0)], [], [BF16, BF16], 0))
    dq0, dk0, dv0 = _sb_bwd(qkv, dob)
    dax, dwa, dwx, dpv = _rglru_bwd(h0a, hst, dhst, pvec, wa16, wx16)
    dh0 = [dax, dag, dq0, dk0, dv0, dbg]
    repl = _pack_repl([dwa, dwx, dpv[4:5], dpv[5:6], dpv[6:7], dpv[7:8], dg_e, db_e, dsink[:, :C_HEADS], loss_terms])
    dwi_e, parts_repl = _matmul(x0_16, dh0, mode="tn", out_dtype=BF16, name="l0_dwin",
                                comm=([repl.astype(BF16)], [GATHER]))
    small = _small_block([dpv[0:4].reshape(4, N_DEV, 128).transpose(1, 0, 2).reshape(N_DEV, 512),
                          dg_o.reshape(N_DEV, 128), db_o.reshape(N_DEV, 128)]).astype(BF16)
    grad_x, parts_in_e, parts_small = _matmul(dh0, wi_e, mode="nt", tm=512, res=dz0, alpha=ALPHA, name="l0_dx",
                                              comm=([dwi_e, small], [A2A_COLS, A2A]))

    res_in_e = _adamw(parts_in_e, e_w_in[0], m_e_w_in[0], v_e_w_in[0], tr=512, name="adamw_w_in_e")
    res_out_e = _adamw(parts_out_e, e_w_out[0], m_e_w_out[0], v_e_w_out[0], name="adamw_w_out_e")
    res_in_o = _adamw(parts_in_o, o_w_in[0], m_o_w_in[0], v_o_w_in[0], tr=512, name="adamw_w_in_o")
    res_out_o = _adamw(parts_out_o, o_w_out[0], m_o_w_out[0], v_o_w_out[0], name="adamw_w_out_o")
    res_small = _adamw(parts_small, _small_vectors(e_conv_w, o_ln_g, o_ln_b),
                       _small_vectors(m_e_conv_w, m_o_ln_g, m_o_ln_b),
                       _small_vectors(v_e_conv_w, v_o_ln_g, v_o_ln_b), name="adamw_vectors")

    w_r = _pack_repl([e_w_gate_a, e_w_gate_x, e_conv_b, e_b_gate_a, e_b_gate_x, e_lru_lambda, e_ln_g, e_ln_b, o_sinks])
    m_r = _pack_repl([m_e_w_gate_a, m_e_w_gate_x, m_e_conv_b, m_e_b_gate_a, m_e_b_gate_x, m_e_lru_lambda, m_e_ln_g,
                      m_e_ln_b, m_o_sinks])
    v_r = _pack_repl([v_e_w_gate_a, v_e_w_gate_x, v_e_conv_b, v_e_b_gate_a, v_e_b_gate_x, v_e_lru_lambda, v_e_ln_g,
                      v_e_ln_b, v_o_sinks])
    g_r, d_r, nm_r, nv_r = _adamw(parts_repl, w_r, m_r, v_r, name="adamw_replicated")
    loss_at = sum(math.prod(shp) for shp in REPL_SHAPES)
    loss = g_r[loss_at // d, loss_at % d] + g_r[(loss_at + 1) // d, (loss_at + 1) % d]

    def assemble(i, rp):
        vec = res_small[i][0]
        cw, lg_o, lb_o = vec[0:512].reshape(1, 4, 128), vec[512:640].reshape(1, 128), vec[640:768].reshape(1, 128)
        w_a, w_x, cb, b_a, b_x, lam, lg_e, lb_e, snk = _unpack_repl(rp)
        return [res_in_e[i][None], cw, cb, w_a, b_a, w_x, b_x, lam, res_out_e[i][None], lg_e, lb_e,
                res_in_o[i][None], snk, res_out_o[i][None], lg_o, lb_o]

    return (loss, grad_x[None], *assemble(0, g_r), *assemble(1, d_r), *assemble(2, nm_r), *assemble(3, nv_r))
```
